```python
import math
import jax, jax.numpy as jnp
from jax import lax
import numpy as np

D_MODEL = 2048
BATCH = 8
SEQ = 2048
DEPTH = 1

CHUNK = 64
D_S5 = D_MODEL // 2
S5_GROUP = 16
S5_GROUPS = D_S5 // S5_GROUP
S5_STATE = 64
DN_HEADS = 8
DN_HEAD_DIM = 128
D_DN = DN_HEADS * DN_HEAD_DIM
CONV_K = 4
EPS = 1e-6
IN_SPLIT_SIZES = (D_S5, D_S5, D_DN, D_DN, D_DN, D_DN, DN_HEADS, DN_HEADS, D_MODEL, D_MODEL)
D_IN = 2 * D_S5 + 4 * D_DN + 2 * DN_HEADS + 2 * D_MODEL

kernel_name = "hybrid_s5_gated_deltanet_block"


def _f32(t):
    return t.astype(jnp.float32)


def rmsnorm(x, w):
    xf = _f32(x)
    return xf * lax.rsqrt(jnp.mean(xf * xf, axis=-1, keepdims=True) + EPS) * _f32(w)


def l2norm(t):
    return t * lax.rsqrt(jnp.sum(t * t, axis=-1, keepdims=True) + EPS)


def causal_depthwise_conv(x, w):
    c = x.shape[-1]
    return lax.conv_general_dilated(
        x, w[:, None, :], window_strides=(1,), padding=[(CONV_K - 1, 0)],
        dimension_numbers=("NWC", "WIO", "NWC"), feature_group_count=c)


def s5_mixer(u, z, lam_re, lam_im, log_step, b_re, b_im, c_re, c_im, d_skip, w_glu):
    bsz, l, _ = u.shape
    lam_re, lam_im = _f32(lam_re), _f32(lam_im)
    step = jnp.exp(_f32(log_step))[:, None]
    mag = jnp.exp(lam_re * step)
    abar_re = mag * jnp.cos(lam_im * step)
    abar_im = mag * jnp.sin(lam_im * step)
    den = lam_re * lam_re + lam_im * lam_im
    xr = abar_re - 1.0
    f_re = (xr * lam_re + abar_im * lam_im) / den
    f_im = (abar_im * lam_re - xr * lam_im) / den
    b_re, b_im = _f32(b_re), _f32(b_im)
    bb_re = f_re[..., None] * b_re - f_im[..., None] * b_im
    bb_im = f_re[..., None] * b_im + f_im[..., None] * b_re
    ug = u.reshape(bsz, l, S5_GROUPS, S5_GROUP)
    bu_re = jnp.einsum("blgc,gpc->blgp", ug, bb_re)
    bu_im = jnp.einsum("blgc,gpc->blgp", ug, bb_im)
    a_re = jnp.broadcast_to(abar_re, bu_re.shape)
    a_im = jnp.broadcast_to(abar_im, bu_im.shape)

    def combine(e1, e2):
        a1r, a1i, b1r, b1i = e1
        a2r, a2i, b2r, b2i = e2
        return (a2r * a1r - a2i * a1i,
                a2r * a1i + a2i * a1r,
                a2r * b1r - a2i * b1i + b2r,
                a2r * b1i + a2i * b1r + b2i)

    _, _, s_re, s_im = lax.associative_scan(combine, (a_re, a_im, bu_re, bu_im), axis=1)
    y = (jnp.einsum("blgp,gcp->blgc", s_re, _f32(c_re))
         - jnp.einsum("blgp,gcp->blgc", s_im, _f32(c_im)))
    y = y.reshape(bsz, l, D_S5) + _f32(d_skip) * u
    y = jax.nn.gelu(y)
    y = y * jax.nn.sigmoid(y @ _f32(w_glu))
    return y * jax.nn.silu(z)


def gated_delta_rule(q, k, v, g, beta):
    bsz, l, h, dk = q.shape
    dv = v.shape[-1]
    n = l // CHUNK

    def chunks(t):
        return t.reshape(bsz, n, CHUNK, h, -1).transpose(0, 3, 1, 2, 4)

    q = chunks(q) * (dk ** -0.5)
    k = chunks(k)
    v = chunks(v)
    g = g.reshape(bsz, n, CHUNK, h).transpose(0, 3, 1, 2)
    beta = beta.reshape(bsz, n, CHUNK, h).transpose(0, 3, 1, 2)
    gc = jnp.cumsum(g, axis=-1)
    causal = jnp.tril(jnp.ones((CHUNK, CHUNK), dtype=bool))
    strict = jnp.tril(jnp.ones((CHUNK, CHUNK), dtype=bool), -1)
    decay = jnp.exp(jnp.where(causal, gc[..., :, None] - gc[..., None, :], -jnp.inf))
    kk = jnp.einsum("bhncd,bhnsd->bhncs", k, k)
    a_mat = jnp.where(strict, beta[..., None] * kk * decay, 0.0)
    rhs = jnp.concatenate([v * beta[..., None], k * (beta * jnp.exp(gc))[..., None]], axis=-1)
    sol = lax.linalg.triangular_solve(a_mat, rhs, left_side=True, lower=True,
                                      unit_diagonal=True)
    u_c, w_c = sol[..., :dv], sol[..., dv:]
    qk = jnp.einsum("bhncd,bhnsd->bhncs", q, k) * decay
    q_dec = q * jnp.exp(gc)[..., None]
    k_dec = k * jnp.exp(gc[..., -1:] - gc)[..., None]
    g_last = jnp.exp(gc[..., -1])

    def step(state, inp):
        u_i, w_i, qk_i, qd_i, kd_i, gl_i = inp
        v_new = u_i - jnp.einsum("bhcd,bhde->bhce", w_i, state)
        o = (jnp.einsum("bhcd,bhde->bhce", qd_i, state)
             + jnp.einsum("bhcs,bhse->bhce", qk_i, v_new))
        state = state * gl_i[..., None, None] + jnp.einsum("bhcd,bhce->bhde", kd_i, v_new)
        return state, o

    xs = (jnp.moveaxis(u_c, 2, 0), jnp.moveaxis(w_c, 2, 0), jnp.moveaxis(qk, 2, 0),
          jnp.moveaxis(q_dec, 2, 0), jnp.moveaxis(k_dec, 2, 0), jnp.moveaxis(g_last, 2, 0))
    s0 = jnp.zeros((bsz, h, dk, dv), q.dtype)
    _, o = lax.scan(step, s0, xs)
    return o.transpose(1, 0, 3, 2, 4).reshape(bsz, l, h, dv)


def deltanet_mixer(q, k, v, z, beta_logit, a_logit, conv_w, a_log, dt_bias, norm_w):
    bsz, l, _ = q.shape
    qkv = jax.nn.silu(causal_depthwise_conv(jnp.concatenate([q, k, v], axis=-1), _f32(conv_w)))
    q, k, v = jnp.split(qkv, [D_DN, 2 * D_DN], axis=-1)
    q = l2norm(q.reshape(bsz, l, DN_HEADS, DN_HEAD_DIM))
    k = l2norm(k.reshape(bsz, l, DN_HEADS, DN_HEAD_DIM))
    v = v.reshape(bsz, l, DN_HEADS, DN_HEAD_DIM)
    beta = jax.nn.sigmoid(beta_logit)
    g = -jnp.exp(_f32(a_log)) * jax.nn.softplus(a_logit + _f32(dt_bias))
    o = gated_delta_rule(q, k, v, g, beta)
    o = rmsnorm(o, norm_w) * jax.nn.silu(z.reshape(bsz, l, DN_HEADS, DN_HEAD_DIM))
    return o.reshape(bsz, l, D_DN)


def _fwd_setup_inputs(seed: int = 0) -> dict:
    key = jax.random.key(seed)
    ks = jax.random.split(key, 24)
    f = jnp.float32
    x = jax.random.normal(ks[0], (BATCH, SEQ, D_MODEL), f)
    ln_w = 1.0 + 0.01 * jax.random.normal(ks[1], (DEPTH, D_MODEL), f)
    w_in = jax.random.normal(ks[2], (DEPTH, D_MODEL, D_IN), f) * D_MODEL ** -0.5
    n_idx = jnp.arange(S5_STATE, dtype=f)
    s5_lam_re = -0.5 + 0.01 * jax.random.normal(ks[3], (DEPTH, S5_GROUPS, S5_STATE), f)
    s5_lam_im = math.pi * n_idx + 0.01 * jax.random.normal(ks[4], (DEPTH, S5_GROUPS, S5_STATE), f)
    s5_log_step = jax.random.uniform(ks[5], (DEPTH, S5_GROUPS), f, math.log(1e-3), math.log(1e-1))
    bsc = (2.0 * S5_GROUP) ** -0.5
    s5_b_re = jax.random.normal(ks[6], (DEPTH, S5_GROUPS, S5_STATE, S5_GROUP), f) * bsc
    s5_b_im = jax.random.normal(ks[7], (DEPTH, S5_GROUPS, S5_STATE, S5_GROUP), f) * bsc
    csc = (2.0 * S5_STATE) ** -0.5
    s5_c_re = jax.random.normal(ks[8], (DEPTH, S5_GROUPS, S5_GROUP, S5_STATE), f) * csc
    s5_c_im = jax.random.normal(ks[9], (DEPTH, S5_GROUPS, S5_GROUP, S5_STATE), f) * csc
    s5_d = jax.random.normal(ks[10], (DEPTH, D_S5), f)
    s5_w_glu = jax.random.normal(ks[11], (DEPTH, D_S5, D_S5), f) * D_S5 ** -0.5
    s5_w_up = jax.random.normal(ks[12], (DEPTH, D_S5, D_MODEL), f) * D_S5 ** -0.5
    dn_conv_w = jax.random.normal(ks[13], (DEPTH, CONV_K, 3 * D_DN), f) * CONV_K ** -0.5
    dn_a_log = jnp.log(jax.random.uniform(ks[14], (DEPTH, DN_HEADS), f, 1.0, 16.0))
    dt = jnp.exp(jax.random.uniform(ks[15], (DEPTH, DN_HEADS), f, math.log(1e-3), math.log(1e-1)))
    dn_dt_bias = dt + jnp.log(-jnp.expm1(-dt))
    dn_norm_w = 1.0 + 0.01 * jax.random.normal(ks[16], (DEPTH, DN_HEAD_DIM), f)
    dn_w_up = jax.random.normal(ks[17], (DEPTH, D_DN, D_MODEL), f) * D_DN ** -0.5
    w_out = jax.random.normal(ks[18], (DEPTH, D_MODEL, D_MODEL), f) * D_MODEL ** -0.5
    final_norm_w = 1.0 + 0.01 * jax.random.normal(ks[19], (D_MODEL,), f)
    return {"x": x, "ln_w": ln_w, "w_in": w_in, "s5_lam_re": s5_lam_re, "s5_lam_im": s5_lam_im,
            "s5_log_step": s5_log_step, "s5_b_re": s5_b_re, "s5_b_im": s5_b_im,
            "s5_c_re": s5_c_re, "s5_c_im": s5_c_im, "s5_d": s5_d, "s5_w_glu": s5_w_glu,
            "s5_w_up": s5_w_up, "dn_conv_w": dn_conv_w, "dn_a_log": dn_a_log,
            "dn_dt_bias": dn_dt_bias, "dn_norm_w": dn_norm_w, "dn_w_up": dn_w_up,
            "w_out": w_out, "final_norm_w": final_norm_w}


def _fwd_reference(x, ln_w, w_in, s5_lam_re, s5_lam_im, s5_log_step, s5_b_re, s5_b_im,
              s5_c_re, s5_c_im, s5_d, s5_w_glu, s5_w_up, dn_conv_w, dn_a_log,
              dn_dt_bias, dn_norm_w, dn_w_up, w_out, final_norm_w):
    split_points = np.cumsum(np.array(IN_SPLIT_SIZES))[:-1].tolist()
    for layer in range(DEPTH):
        h = rmsnorm(x, ln_w[layer])
        proj = _f32(h @ _f32(w_in[layer]))
        (u_s, z_s, q, k, v, z_d, beta_l, a_l, gate_s, gate_d) = jnp.split(proj, split_points, axis=-1)
        y_s = s5_mixer(u_s, z_s, s5_lam_re[layer], s5_lam_im[layer], s5_log_step[layer],
                       s5_b_re[layer], s5_b_im[layer], s5_c_re[layer], s5_c_im[layer],
                       s5_d[layer], s5_w_glu[layer]) @ _f32(s5_w_up[layer])
        y_d = deltanet_mixer(q, k, v, z_d, beta_l, a_l, dn_conv_w[layer], dn_a_log[layer],
                             dn_dt_bias[layer], dn_norm_w[layer]) @ _f32(dn_w_up[layer])
        mixed = jax.nn.sigmoid(gate_s) * y_s + jax.nn.sigmoid(gate_d) * y_d
        x = x + (mixed @ _f32(w_out[layer])).astype(x.dtype)
    return rmsnorm(x, final_norm_w).astype(x.dtype)


import jax as _jax
import jax.numpy as _jnp

TWIN_FORMAT = 'train_step'
FWD_PARAMS = ['x', 'ln_w', 'w_in', 's5_lam_re', 's5_lam_im', 's5_log_step', 's5_b_re', 's5_b_im', 's5_c_re', 's5_c_im', 's5_d', 's5_w_glu', 's5_w_up', 'dn_conv_w', 'dn_a_log', 'dn_dt_bias', 'dn_norm_w', 'dn_w_up', 'w_out', 'final_norm_w']
TWIN_WEIGHTS = ['ln_w', 'w_in', 's5_lam_re', 's5_lam_im', 's5_log_step', 's5_b_re', 's5_b_im', 's5_c_re', 's5_c_im', 's5_d', 's5_w_glu', 's5_w_up', 'dn_conv_w', 'dn_a_log', 'dn_dt_bias', 'dn_norm_w', 'dn_w_up', 'w_out', 'final_norm_w']
TWIN_DIFF_INPUT = 'x'
TWIN_INPUTS = ['x', 'ln_w', 'w_in', 's5_lam_re', 's5_lam_im', 's5_log_step', 's5_b_re', 's5_b_im', 's5_c_re', 's5_c_im', 's5_d', 's5_w_glu', 's5_w_up', 'dn_conv_w', 'dn_a_log', 'dn_dt_bias', 'dn_norm_w', 'dn_w_up', 'w_out', 'final_norm_w', 'loss_target', 'm_ln_w', 'm_w_in', 'm_s5_lam_re', 'm_s5_lam_im', 'm_s5_log_step', 'm_s5_b_re', 'm_s5_b_im', 'm_s5_c_re', 'm_s5_c_im', 'm_s5_d', 'm_s5_w_glu', 'm_s5_w_up', 'm_dn_conv_w', 'm_dn_a_log', 'm_dn_dt_bias', 'm_dn_norm_w', 'm_dn_w_up', 'm_w_out', 'm_final_norm_w', 'v_ln_w', 'v_w_in', 'v_s5_lam_re', 'v_s5_lam_im', 'v_s5_log_step', 'v_s5_b_re', 'v_s5_b_im', 'v_s5_c_re', 'v_s5_c_im', 'v_s5_d', 'v_s5_w_glu', 'v_s5_w_up', 'v_dn_conv_w', 'v_dn_a_log', 'v_dn_dt_bias', 'v_dn_norm_w', 'v_dn_w_up', 'v_w_out', 'v_final_norm_w']
TWIN_OUTPUTS = ['loss', 'grad_x', 'grad_ln_w', 'grad_w_in', 'grad_s5_lam_re', 'grad_s5_lam_im', 'grad_s5_log_step', 'grad_s5_b_re', 'grad_s5_b_im', 'grad_s5_c_re', 'grad_s5_c_im', 'grad_s5_d', 'grad_s5_w_glu', 'grad_s5_w_up', 'grad_dn_conv_w', 'grad_dn_a_log', 'grad_dn_dt_bias', 'grad_dn_norm_w', 'grad_dn_w_up', 'grad_w_out', 'grad_final_norm_w', 'delta_ln_w', 'delta_w_in', 'delta_s5_lam_re', 'delta_s5_lam_im', 'delta_s5_log_step', 'delta_s5_b_re', 'delta_s5_b_im', 'delta_s5_c_re', 'delta_s5_c_im', 'delta_s5_d', 'delta_s5_w_glu', 'delta_s5_w_up', 'delta_dn_conv_w', 'delta_dn_a_log', 'delta_dn_dt_bias', 'delta_dn_norm_w', 'delta_dn_w_up', 'delta_w_out', 'delta_final_norm_w', 'new_m_ln_w', 'new_m_w_in', 'new_m_s5_lam_re', 'new_m_s5_lam_im', 'new_m_s5_log_step', 'new_m_s5_b_re', 'new_m_s5_b_im', 'new_m_s5_c_re', 'new_m_s5_c_im', 'new_m_s5_d', 'new_m_s5_w_glu', 'new_m_s5_w_up', 'new_m_dn_conv_w', 'new_m_dn_a_log', 'new_m_dn_dt_bias', 'new_m_dn_norm_w', 'new_m_dn_w_up', 'new_m_w_out', 'new_m_final_norm_w', 'new_v_ln_w', 'new_v_w_in', 'new_v_s5_lam_re', 'new_v_s5_lam_im', 'new_v_s5_log_step', 'new_v_s5_b_re', 'new_v_s5_b_im', 'new_v_s5_c_re', 'new_v_s5_c_im', 'new_v_s5_d', 'new_v_s5_w_glu', 'new_v_s5_w_up', 'new_v_dn_conv_w', 'new_v_dn_a_log', 'new_v_dn_dt_bias', 'new_v_dn_norm_w', 'new_v_dn_w_up', 'new_v_w_out', 'new_v_final_norm_w']
TWIN_LEAF_KINDS = {'loss': 'loss', 'grad_x': 'grad_x', 'grad_ln_w': 'grad_w', 'grad_w_in': 'grad_w', 'grad_s5_lam_re': 'grad_w', 'grad_s5_lam_im': 'grad_w', 'grad_s5_log_step': 'grad_w', 'grad_s5_b_re': 'grad_w', 'grad_s5_b_im': 'grad_w', 'grad_s5_c_re': 'grad_w', 'grad_s5_c_im': 'grad_w', 'grad_s5_d': 'grad_w', 'grad_s5_w_glu': 'grad_w', 'grad_s5_w_up': 'grad_w', 'grad_dn_conv_w': 'grad_w', 'grad_dn_a_log': 'grad_w', 'grad_dn_dt_bias': 'grad_w', 'grad_dn_norm_w': 'grad_w', 'grad_dn_w_up': 'grad_w', 'grad_w_out': 'grad_w', 'grad_final_norm_w': 'grad_w', 'delta_ln_w': 'delta_w', 'delta_w_in': 'delta_w', 'delta_s5_lam_re': 'delta_w', 'delta_s5_lam_im': 'delta_w', 'delta_s5_log_step': 'delta_w', 'delta_s5_b_re': 'delta_w', 'delta_s5_b_im': 'delta_w', 'delta_s5_c_re': 'delta_w', 'delta_s5_c_im': 'delta_w', 'delta_s5_d': 'delta_w', 'delta_s5_w_glu': 'delta_w', 'delta_s5_w_up': 'delta_w', 'delta_dn_conv_w': 'delta_w', 'delta_dn_a_log': 'delta_w', 'delta_dn_dt_bias': 'delta_w', 'delta_dn_norm_w': 'delta_w', 'delta_dn_w_up': 'delta_w', 'delta_w_out': 'delta_w', 'delta_final_norm_w': 'delta_w', 'new_m_ln_w': 'new_m', 'new_m_w_in': 'new_m', 'new_m_s5_lam_re': 'new_m', 'new_m_s5_lam_im': 'new_m', 'new_m_s5_log_step': 'new_m', 'new_m_s5_b_re': 'new_m', 'new_m_s5_b_im': 'new_m', 'new_m_s5_c_re': 'new_m', 'new_m_s5_c_im': 'new_m', 'new_m_s5_d': 'new_m', 'new_m_s5_w_glu': 'new_m', 'new_m_s5_w_up': 'new_m', 'new_m_dn_conv_w': 'new_m', 'new_m_dn_a_log': 'new_m', 'new_m_dn_dt_bias': 'new_m', 'new_m_dn_norm_w': 'new_m', 'new_m_dn_w_up': 'new_m', 'new_m_w_out': 'new_m', 'new_m_final_norm_w': 'new_m', 'new_v_ln_w': 'new_v', 'new_v_w_in': 'new_v', 'new_v_s5_lam_re': 'new_v', 'new_v_s5_lam_im': 'new_v', 'new_v_s5_log_step': 'new_v', 'new_v_s5_b_re': 'new_v', 'new_v_s5_b_im': 'new_v', 'new_v_s5_c_re': 'new_v', 'new_v_s5_c_im': 'new_v', 'new_v_s5_d': 'new_v', 'new_v_s5_w_glu': 'new_v', 'new_v_s5_w_up': 'new_v', 'new_v_dn_conv_w': 'new_v', 'new_v_dn_a_log': 'new_v', 'new_v_dn_dt_bias': 'new_v', 'new_v_dn_norm_w': 'new_v', 'new_v_dn_w_up': 'new_v', 'new_v_w_out': 'new_v', 'new_v_final_norm_w': 'new_v'}


def _forward(args):
    return _fwd_reference(*[args[k] for k in FWD_PARAMS])


def _output_shape():
    out = _jax.eval_shape(lambda: _forward(_fwd_setup_inputs(0)))
    return out.shape, out.dtype

N_MICROBATCH = 1
ADAM_LR = 0.001
ADAM_B1 = 0.9
ADAM_B2 = 0.999
ADAM_EPS = 1e-08
ADAM_WD = 0.01
ADAM_STEP = 10
PER_EXAMPLE_BATCH_AXIS = {'x': 0, 'loss_target': 0}
SHARED_INPUTS = []
_WEIGHT_DTYPES = {'ln_w': _jnp.float32, 'w_in': _jnp.float32, 's5_lam_re': _jnp.float32, 's5_lam_im': _jnp.float32, 's5_log_step': _jnp.float32, 's5_b_re': _jnp.float32, 's5_b_im': _jnp.float32, 's5_c_re': _jnp.float32, 's5_c_im': _jnp.float32, 's5_d': _jnp.float32, 's5_w_glu': _jnp.float32, 's5_w_up': _jnp.float32, 'dn_conv_w': _jnp.float32, 'dn_a_log': _jnp.float32, 'dn_dt_bias': _jnp.float32, 'dn_norm_w': _jnp.float32, 'dn_w_up': _jnp.float32, 'w_out': _jnp.float32, 'final_norm_w': _jnp.float32}
MOMENT_SCALE = {'ln_w': 3.391354e-02, 'w_in': 1.507790e-02, 's5_lam_re': 4.609117e-04, 's5_lam_im': 4.572611e-04, 's5_log_step': 3.702905e-01, 's5_b_re': 3.370387e-04, 's5_b_im': 3.315161e-04, 's5_c_re': 6.748737e-04, 's5_c_im': 6.689514e-04, 's5_d': 1.033120e-02, 's5_w_glu': 2.843961e-03, 's5_w_up': 6.622139e-03, 'dn_conv_w': 1.960938e-02, 'dn_a_log': 8.024636e-02, 'dn_dt_bias': 7.784419e-02, 'dn_norm_w': 7.904739e-02, 'dn_w_up': 1.885568e-02, 'w_out': 2.000089e-02, 'final_norm_w': 7.991961e+00}


def _to_microbatches(a, axis):
    t = _jnp.moveaxis(a, axis, 0)
    t = t.reshape((N_MICROBATCH, t.shape[0] // N_MICROBATCH) + t.shape[1:])
    return _jnp.moveaxis(t, 1, axis + 1)


def setup_inputs(seed: int = 0) -> dict:
    inp = _fwd_setup_inputs(seed)
    key = _jax.random.fold_in(_jax.random.key(seed), 7919)
    shape, _ = _output_shape()
    out = dict(inp)
    out["loss_target"] = _jax.random.normal(_jax.random.fold_in(key, 0), shape, _jnp.float32)
    for i, name in enumerate(TWIN_WEIGHTS):
        w = inp[name].astype(_jnp.float32)
        if MOMENT_SCALE is None:
            s = _jnp.sqrt(_jnp.mean(_jnp.square(w)) + 1e-30)
        else:
            s = MOMENT_SCALE[name]
        km, kv = _jax.random.split(_jax.random.fold_in(key, i + 1))
        out[name] = w
        out["m_" + name] = s * _jax.random.normal(km, w.shape, _jnp.float32)
        out["v_" + name] = (s * s) * _jax.random.uniform(kv, w.shape, _jnp.float32, 0.5, 1.5)
    if N_MICROBATCH > 1:
        for name, axis in PER_EXAMPLE_BATCH_AXIS.items():
            out[name] = _to_microbatches(out[name], axis)
    return {'x': out['x'], 'ln_w': out['ln_w'], 'w_in': out['w_in'], 's5_lam_re': out['s5_lam_re'], 's5_lam_im': out['s5_lam_im'], 's5_log_step': out['s5_log_step'], 's5_b_re': out['s5_b_re'], 's5_b_im': out['s5_b_im'], 's5_c_re': out['s5_c_re'], 's5_c_im': out['s5_c_im'], 's5_d': out['s5_d'], 's5_w_glu': out['s5_w_glu'], 's5_w_up': out['s5_w_up'], 'dn_conv_w': out['dn_conv_w'], 'dn_a_log': out['dn_a_log'], 'dn_dt_bias': out['dn_dt_bias'], 'dn_norm_w': out['dn_norm_w'], 'dn_w_up': out['dn_w_up'], 'w_out': out['w_out'], 'final_norm_w': out['final_norm_w'], 'loss_target': out['loss_target'], 'm_ln_w': out['m_ln_w'], 'm_w_in': out['m_w_in'], 'm_s5_lam_re': out['m_s5_lam_re'], 'm_s5_lam_im': out['m_s5_lam_im'], 'm_s5_log_step': out['m_s5_log_step'], 'm_s5_b_re': out['m_s5_b_re'], 'm_s5_b_im': out['m_s5_b_im'], 'm_s5_c_re': out['m_s5_c_re'], 'm_s5_c_im': out['m_s5_c_im'], 'm_s5_d': out['m_s5_d'], 'm_s5_w_glu': out['m_s5_w_glu'], 'm_s5_w_up': out['m_s5_w_up'], 'm_dn_conv_w': out['m_dn_conv_w'], 'm_dn_a_log': out['m_dn_a_log'], 'm_dn_dt_bias': out['m_dn_dt_bias'], 'm_dn_norm_w': out['m_dn_norm_w'], 'm_dn_w_up': out['m_dn_w_up'], 'm_w_out': out['m_w_out'], 'm_final_norm_w': out['m_final_norm_w'], 'v_ln_w': out['v_ln_w'], 'v_w_in': out['v_w_in'], 'v_s5_lam_re': out['v_s5_lam_re'], 'v_s5_lam_im': out['v_s5_lam_im'], 'v_s5_log_step': out['v_s5_log_step'], 'v_s5_b_re': out['v_s5_b_re'], 'v_s5_b_im': out['v_s5_b_im'], 'v_s5_c_re': out['v_s5_c_re'], 'v_s5_c_im': out['v_s5_c_im'], 'v_s5_d': out['v_s5_d'], 'v_s5_w_glu': out['v_s5_w_glu'], 'v_s5_w_up': out['v_s5_w_up'], 'v_dn_conv_w': out['v_dn_conv_w'], 'v_dn_a_log': out['v_dn_a_log'], 'v_dn_dt_bias': out['v_dn_dt_bias'], 'v_dn_norm_w': out['v_dn_norm_w'], 'v_dn_w_up': out['v_dn_w_up'], 'v_w_out': out['v_w_out'], 'v_final_norm_w': out['v_final_norm_w']}


def _loss(weights, diff, rest, loss_target):
    with _jax.named_scope("forward"):
        args = {**rest, TWIN_DIFF_INPUT: diff, **{k: w.astype(_WEIGHT_DTYPES[k]) for k, w in weights.items()}}
        y = _forward(args)
    with _jax.named_scope("loss_head"):
        err = _jnp.square(y.astype(_jnp.float32) - loss_target)
        return 0.5 * _jnp.sum(_jnp.mean(err, axis=-1)) if err.ndim else 0.5 * err


def _adamw(w, g, m, v):
    m = ADAM_B1 * m + (1.0 - ADAM_B1) * g
    v = ADAM_B2 * v + (1.0 - ADAM_B2) * _jnp.square(g)
    m_hat = m / (1.0 - ADAM_B1 ** ADAM_STEP)
    v_hat = v / (1.0 - ADAM_B2 ** ADAM_STEP)
    delta = -ADAM_LR * (m_hat / (_jnp.sqrt(v_hat) + ADAM_EPS) + ADAM_WD * w)
    return delta, m, v


def reference(x, ln_w, w_in, s5_lam_re, s5_lam_im, s5_log_step, s5_b_re, s5_b_im, s5_c_re, s5_c_im, s5_d, s5_w_glu, s5_w_up, dn_conv_w, dn_a_log, dn_dt_bias, dn_norm_w, dn_w_up, w_out, final_norm_w, loss_target, m_ln_w, m_w_in, m_s5_lam_re, m_s5_lam_im, m_s5_log_step, m_s5_b_re, m_s5_b_im, m_s5_c_re, m_s5_c_im, m_s5_d, m_s5_w_glu, m_s5_w_up, m_dn_conv_w, m_dn_a_log, m_dn_dt_bias, m_dn_norm_w, m_dn_w_up, m_w_out, m_final_norm_w, v_ln_w, v_w_in, v_s5_lam_re, v_s5_lam_im, v_s5_log_step, v_s5_b_re, v_s5_b_im, v_s5_c_re, v_s5_c_im, v_s5_d, v_s5_w_glu, v_s5_w_up, v_dn_conv_w, v_dn_a_log, v_dn_dt_bias, v_dn_norm_w, v_dn_w_up, v_w_out, v_final_norm_w):
    given = dict(x=x, ln_w=ln_w, w_in=w_in, s5_lam_re=s5_lam_re, s5_lam_im=s5_lam_im, s5_log_step=s5_log_step, s5_b_re=s5_b_re, s5_b_im=s5_b_im, s5_c_re=s5_c_re, s5_c_im=s5_c_im, s5_d=s5_d, s5_w_glu=s5_w_glu, s5_w_up=s5_w_up, dn_conv_w=dn_conv_w, dn_a_log=dn_a_log, dn_dt_bias=dn_dt_bias, dn_norm_w=dn_norm_w, dn_w_up=dn_w_up, w_out=w_out, final_norm_w=final_norm_w, loss_target=loss_target, m_ln_w=m_ln_w, m_w_in=m_w_in, m_s5_lam_re=m_s5_lam_re, m_s5_lam_im=m_s5_lam_im, m_s5_log_step=m_s5_log_step, m_s5_b_re=m_s5_b_re, m_s5_b_im=m_s5_b_im, m_s5_c_re=m_s5_c_re, m_s5_c_im=m_s5_c_im, m_s5_d=m_s5_d, m_s5_w_glu=m_s5_w_glu, m_s5_w_up=m_s5_w_up, m_dn_conv_w=m_dn_conv_w, m_dn_a_log=m_dn_a_log, m_dn_dt_bias=m_dn_dt_bias, m_dn_norm_w=m_dn_norm_w, m_dn_w_up=m_dn_w_up, m_w_out=m_w_out, m_final_norm_w=m_final_norm_w, v_ln_w=v_ln_w, v_w_in=v_w_in, v_s5_lam_re=v_s5_lam_re, v_s5_lam_im=v_s5_lam_im, v_s5_log_step=v_s5_log_step, v_s5_b_re=v_s5_b_re, v_s5_b_im=v_s5_b_im, v_s5_c_re=v_s5_c_re, v_s5_c_im=v_s5_c_im, v_s5_d=v_s5_d, v_s5_w_glu=v_s5_w_glu, v_s5_w_up=v_s5_w_up, v_dn_conv_w=v_dn_conv_w, v_dn_a_log=v_dn_a_log, v_dn_dt_bias=v_dn_dt_bias, v_dn_norm_w=v_dn_norm_w, v_dn_w_up=v_dn_w_up, v_w_out=v_w_out, v_final_norm_w=v_final_norm_w)
    weights = {n: given[n] for n in TWIN_WEIGHTS}
    shared = {n: given[n] for n in SHARED_INPUTS}
    per_example = {n: given[n] for n in ['x']}
    grad_fn = _jax.value_and_grad(_loss, argnums=(0, 1))

    def one_microbatch(ex, loss_target):
        ex = dict(ex)
        diff = ex.pop(TWIN_DIFF_INPUT)
        return grad_fn(weights, diff, {**shared, **ex}, loss_target)

    if N_MICROBATCH == 1:
        loss, (grad_w, grad_x) = one_microbatch(per_example, given["loss_target"])
    else:
        def body(carry, xs):
            loss_sum, grad_sum = carry
            l_k, (gw_k, gx_k) = one_microbatch(xs[0], xs[1])
            with _jax.named_scope("update"):
                return (loss_sum + l_k, _jax.tree.map(_jnp.add, grad_sum, gw_k)), gx_k

        init = (_jnp.zeros((), _jnp.float32), _jax.tree.map(_jnp.zeros_like, weights))
        (loss, grad_w), grad_x = _jax.lax.scan(body, init, (per_example, given["loss_target"]))
    with _jax.named_scope("update"):
        delta_w, new_m, new_v = {}, {}, {}
        for n in TWIN_WEIGHTS:
            delta_w[n], new_m[n], new_v[n] = _adamw(weights[n], grad_w[n], given["m_" + n], given["v_" + n])
    return (loss, grad_x, *[grad_w[n] for n in TWIN_WEIGHTS], *[delta_w[n] for n in TWIN_WEIGHTS],
            *[new_m[n] for n in TWIN_WEIGHTS], *[new_v[n] for n in TWIN_WEIGHTS])
```

```python
import functools
import math

import jax
import jax.numpy as jnp
from jax import lax
from jax.experimental import pallas as pl
from jax.experimental.pallas import tpu as pltpu

F32 = jnp.float32
BF16 = jnp.bfloat16
HIGHEST = lax.Precision.HIGHEST
MESH = pl.DeviceIdType.MESH
N_DEV = 8

EPS = 1e-6
S5_GROUP = 16
S5_STATE = 64
S5_GPB = 8
S5_T = 256
DN_HEADS = 8
DN_HEAD_DIM = 128
CHUNK = 64
CONV_K = 4
BA_PAD = 512

ADAM_LR = 0.001
ADAM_B1 = 0.9
ADAM_B2 = 0.999
ADAM_EPS = 1e-08
ADAM_WD = 0.01
ADAM_STEP = 10

VMEM_LIMIT_BYTES = 48 * 1024 * 1024
ROW_TILE = 256


def _cparams(*sem):
    return pltpu.CompilerParams(dimension_semantics=sem if sem else None,
                                vmem_limit_bytes=VMEM_LIMIT_BYTES)


def _sigmoid(x):
    return 1.0 / (1.0 + jnp.exp(-x))


def _silu(x):
    return x * _sigmoid(x)


def _gelu(x):
    return 0.5 * x * (1.0 + jnp.tanh(0.7978845608028654 * (x + 0.044715 * x * x * x)))


def _softplus(x):
    return jnp.maximum(x, 0.0) + jnp.log(1.0 + jnp.exp(-jnp.abs(x)))


def _rmsnorm(x, w):
    return x * lax.rsqrt(jnp.mean(x * x, axis=-1, keepdims=True) + EPS) * w


def _dot(a, b, dims=((1,), (0,)), precision=None):
    return lax.dot_general(a, b, (dims, ((), ())), precision=precision,
                           preferred_element_type=F32)


def _bdot(a, b, dims=((1,), (0,))):
    return _dot(a.astype(BF16), b.astype(BF16), dims)


def _mm(a, b, *, ta=False, tb=False, out_dtype=F32, tm=512, tn=512, tk=None, name):
    k_dim, m_dim = (a.shape if ta else a.shape[::-1])
    n_dim = b.shape[0] if tb else b.shape[1]
    assert (b.shape[1] if tb else b.shape[0]) == k_dim
    tm, tn = min(tm, m_dim), min(tn, n_dim)
    tk = k_dim if tk is None else tk
    assert m_dim % tm == 0 and n_dim % tn == 0 and k_dim % tk == 0
    nk = k_dim // tk
    a_spec = (pl.BlockSpec((tk, tm), lambda i, j, k: (k, i)) if ta
              else pl.BlockSpec((tm, tk), lambda i, j, k: (i, k)))
    b_spec = (pl.BlockSpec((tn, tk), lambda i, j, k: (j, k)) if tb
              else pl.BlockSpec((tk, tn), lambda i, j, k: (k, j)))
    dims = ((0 if ta else 1,), (1 if tb else 0,))

    def body(a_ref, b_ref, o_ref, *scratch):
        p = _bdot(a_ref[...], b_ref[...], dims)
        if nk == 1:
            o_ref[...] = p.astype(o_ref.dtype)
        else:
            acc = scratch[0]
            k = pl.program_id(2)

            @pl.when(k == 0)
            def _():
                acc[...] = p

            @pl.when(k > 0)
            def _():
                acc[...] += p

            @pl.when(k == nk - 1)
            def _():
                o_ref[...] = acc[...].astype(o_ref.dtype)

    return pl.pallas_call(
        body, name=name,
        out_shape=jax.ShapeDtypeStruct((m_dim, n_dim), out_dtype),
        grid=(m_dim // tm, n_dim // tn, nk),
        in_specs=[a_spec, b_spec],
        out_specs=pl.BlockSpec((tm, tn), lambda i, j, k: (i, j)),
        scratch_shapes=[pltpu.VMEM((tm, tn), F32)] if nk > 1 else [],
        compiler_params=_cparams("parallel", "parallel", "arbitrary"),
    )(a, b)


def _rms_fwd(x, w):
    l, d = x.shape

    def body(x_ref, w_ref, h_ref):
        h_ref[...] = _rmsnorm(x_ref[...], w_ref[...]).astype(BF16)

    return pl.pallas_call(
        body, name="rms_fwd",
        out_shape=jax.ShapeDtypeStruct((l, d), BF16),
        grid=(l // ROW_TILE,),
        in_specs=[pl.BlockSpec((ROW_TILE, d), lambda i: (i, 0)),
                  pl.BlockSpec((1, d), lambda i: (0, 0))],
        out_specs=pl.BlockSpec((ROW_TILE, d), lambda i: (i, 0)),
        compiler_params=_cparams("parallel"),
    )(x, w)


def _rms_bwd(x, w, dh, dres):
    l, d = x.shape

    def body(x_ref, w_ref, dh_ref, dres_ref, dx_ref, dw_ref):
        _, vjp = jax.vjp(_rmsnorm, x_ref[...], w_ref[...])
        dx, dw = vjp(dh_ref[...])
        dx_ref[...] = dx + dres_ref[...]

        @pl.when(pl.program_id(0) == 0)
        def _():
            dw_ref[...] = jnp.zeros_like(dw_ref)

        dw_ref[...] += dw

    row = pl.BlockSpec((ROW_TILE, d), lambda i: (i, 0))
    vec = pl.BlockSpec((1, d), lambda i: (0, 0))
    return pl.pallas_call(
        body, name="rms_bwd",
        out_shape=(jax.ShapeDtypeStruct((l, d), F32), jax.ShapeDtypeStruct((1, d), F32)),
        grid=(l // ROW_TILE,),
        in_specs=[row, vec, row, row],
        out_specs=(row, vec),
        compiler_params=_cparams("arbitrary"),
    )(x, w, dh, dres)


def _final(x, r, fw, target):
    l, d = x.shape

    def per_row_loss(x2, w, tgt):
        err = _rmsnorm(x2, w) - tgt
        return 0.5 * jnp.mean(err * err, axis=-1, keepdims=True)

    def body(x_ref, r_ref, w_ref, t_ref, dx_ref, dxb_ref, loss_ref, dw_ref):
        x2 = x_ref[...] + r_ref[...]
        rows, vjp = jax.vjp(functools.partial(per_row_loss, tgt=t_ref[...]), x2, w_ref[...])
        dx2, dw = vjp(jnp.ones_like(rows))
        dx_ref[...] = dx2
        dxb_ref[...] = dx2.astype(BF16)

        @pl.when(pl.program_id(0) == 0)
        def _():
            dw_ref[...] = jnp.zeros_like(dw_ref)
            loss_ref[...] = jnp.zeros_like(loss_ref)

        dw_ref[...] += dw
        loss_ref[...] += jnp.sum(rows, axis=0, keepdims=True)

    row = pl.BlockSpec((ROW_TILE, d), lambda i: (i, 0))
    vec = pl.BlockSpec((1, d), lambda i: (0, 0))
    return pl.pallas_call(
        body, name="final_norm_loss",
        out_shape=(jax.ShapeDtypeStruct((l, d), F32), jax.ShapeDtypeStruct((l, d), BF16),
                   jax.ShapeDtypeStruct((1, 1), F32), jax.ShapeDtypeStruct((1, d), F32)),
        grid=(l // ROW_TILE,),
        in_specs=[row, row, vec, row],
        out_specs=(row, row, pl.BlockSpec((1, 1), lambda i: (0, 0)), vec),
        compiler_params=_cparams("arbitrary"),
    )(x, r, fw, target)


def _merge_fn(gs, gd, ys, yd):
    return _sigmoid(gs) * ys + _sigmoid(gd) * yd


def _merge_fwd(proj, off_gs, off_gd, ys, yd):
    l, d = ys.shape
    cw = 512
    blk = lambda off: pl.BlockSpec((ROW_TILE, cw), lambda i, j: (i, off // cw + j))

    def body(gs_ref, gd_ref, ys_ref, yd_ref, o_ref):
        o_ref[...] = _merge_fn(gs_ref[...], gd_ref[...], ys_ref[...], yd_ref[...]).astype(BF16)

    return pl.pallas_call(
        body, name="merge_fwd",
        out_shape=jax.ShapeDtypeStruct((l, d), BF16),
        grid=(l // ROW_TILE, d // cw),
        in_specs=[blk(off_gs), blk(off_gd), blk(0), blk(0)],
        out_specs=blk(0),
        compiler_params=_cparams("parallel", "parallel"),
    )(proj, proj, ys, yd)


def _merge_bwd(proj, off_gs, off_gd, ys, yd, dmixed):
    l, d = ys.shape
    cw = 512
    blk = lambda off: pl.BlockSpec((ROW_TILE, cw), lambda i, j: (i, off // cw + j))

    def body(gs_ref, gd_ref, ys_ref, yd_ref, dm_ref, dgs_ref, dgd_ref, dys_ref, dyd_ref):
        _, vjp = jax.vjp(_merge_fn, gs_ref[...], gd_ref[...], ys_ref[...], yd_ref[...])
        dgs, dgd, dys, dyd = vjp(dm_ref[...])
        dgs_ref[...] = dgs.astype(BF16)
        dgd_ref[...] = dgd.astype(BF16)
        dys_ref[...] = dys.astype(BF16)
        dyd_ref[...] = dyd.astype(BF16)

    out = jax.ShapeDtypeStruct((l, d), BF16)
    return pl.pallas_call(
        body, name="merge_bwd",
        out_shape=(out, out, out, out),
        grid=(l // ROW_TILE, d // cw),
        in_specs=[blk(off_gs), blk(off_gd), blk(0), blk(0), blk(0)],
        out_specs=(blk(0), blk(0), blk(0), blk(0)),
        compiler_params=_cparams("parallel", "parallel"),
    )(proj, proj, ys, yd, dmixed)


def _s5_disc_fn(lam_re, lam_im, log_step):
    step = jnp.exp(log_step)
    mag = jnp.exp(lam_re * step)
    abar_re = mag * jnp.cos(lam_im * step)
    abar_im = mag * jnp.sin(lam_im * step)
    den = lam_re * lam_re + lam_im * lam_im
    xr = abar_re - 1.0
    f_re = (xr * lam_re + abar_im * lam_im) / den
    f_im = (abar_im * lam_re - xr * lam_im) / den
    return abar_re, abar_im, f_re, f_im


def _s5_disc_fwd(lam_re, lam_im, log_step):
    g, p = lam_re.shape

    def body(lr_ref, li_ref, ls_ref, ar_ref, ai_ref, fr_ref, fi_ref):
        ar, ai, fr, fi = _s5_disc_fn(lr_ref[...], li_ref[...], ls_ref[...])
        ar_ref[...] = ar
        ai_ref[...] = ai
        fr_ref[...] = fr
        fi_ref[...] = fi

    o = jax.ShapeDtypeStruct((g, p), F32)
    return pl.pallas_call(body, name="s5_disc_fwd", out_shape=(o, o, o, o),
                          compiler_params=_cparams())(lam_re, lam_im, log_step)


def _s5_disc_bwd(lam_re, lam_im, log_step, dar, dai, dfr, dfi):
    g, p = lam_re.shape

    def body(lr_ref, li_ref, ls_ref, dar_ref, dai_ref, dfr_ref, dfi_ref, dlr_ref, dli_ref, dls_ref):
        _, vjp = jax.vjp(_s5_disc_fn, lr_ref[...], li_ref[...], ls_ref[...])
        dlr, dli, dls = vjp((dar_ref[...], dai_ref[...], dfr_ref[...], dfi_ref[...]))
        dlr_ref[...] = dlr
        dli_ref[...] = dli
        dls_ref[...] = dls

    o = jax.ShapeDtypeStruct((g, p), F32)
    return pl.pallas_call(body, name="s5_disc_bwd",
                          out_shape=(o, o, jax.ShapeDtypeStruct((g, 1), F32)),
                          compiler_params=_cparams())(lam_re, lam_im, log_step, dar, dai, dfr, dfi)


def _s5_bbar_fwd(f_re, f_im, b_re, b_im):
    n, c = b_re.shape

    def body(fr_ref, fi_ref, br_ref, bi_ref, or_ref, oi_ref):
        fr, fi, br, bi = fr_ref[...], fi_ref[...], br_ref[...], bi_ref[...]
        or_ref[...] = fr * br - fi * bi
        oi_ref[...] = fr * bi + fi * br

    o = jax.ShapeDtypeStruct((n, c), F32)
    return pl.pallas_call(body, name="s5_bbar_fwd", out_shape=(o, o),
                          compiler_params=_cparams())(f_re, f_im, b_re, b_im)


def _s5_bbar_bwd(f_re, f_im, b_re, b_im, dbr, dbi):
    n, c = b_re.shape

    def body(fr_ref, fi_ref, br_ref, bi_ref, dor_ref, doi_ref, dfr_ref, dfi_ref, dbr_ref, dbi_ref):
        fr, fi, br, bi = fr_ref[...], fi_ref[...], br_ref[...], bi_ref[...]
        dor, doi = dor_ref[...], doi_ref[...]
        dfr_ref[...] = jnp.sum(dor * br + doi * bi, axis=-1, keepdims=True)
        dfi_ref[...] = jnp.sum(doi * br - dor * bi, axis=-1, keepdims=True)
        dbr_ref[...] = fr * dor + fi * doi
        dbi_ref[...] = fr * doi - fi * dor

    col = jax.ShapeDtypeStruct((n, 1), F32)
    o = jax.ShapeDtypeStruct((n, c), F32)
    return pl.pallas_call(body, name="s5_bbar_bwd", out_shape=(col, col, o, o),
                          compiler_params=_cparams())(f_re, f_im, b_re, b_im, dbr, dbi)


def _scan_rows(xr, xi, ar, ai, reverse):
    t = xr.shape[0]
    row = lax.broadcasted_iota(jnp.int32, (t, 1), 0)
    pr, pi = ar, ai
    sh = 1
    while sh < t:
        if reverse:
            keep = row < t - sh
            sr, si = pltpu.roll(xr, t - sh, 0), pltpu.roll(xi, t - sh, 0)
        else:
            keep = row >= sh
            sr, si = pltpu.roll(xr, sh, 0), pltpu.roll(xi, sh, 0)
        sr = jnp.where(keep, sr, 0.0)
        si = jnp.where(keep, si, 0.0)
        xr, xi = xr + pr * sr - pi * si, xi + pr * si + pi * sr
        pr, pi = pr * pr - pi * pi, 2.0 * pr * pi
        sh *= 2
    return xr, xi


def _s5_states(u_bf, bbr, bbi, ar, ai, cr, ci):
    t = u_bf.shape[0]
    row = lax.broadcasted_iota(jnp.int32, (t, 1), 0)
    xr = _dot(u_bf, bbr)
    xi = _dot(u_bf, bbi)
    first = row == 0
    xr = xr + jnp.where(first, ar * cr - ai * ci, 0.0)
    xi = xi + jnp.where(first, ar * ci + ai * cr, 0.0)
    return _scan_rows(xr, xi, ar, ai, reverse=False)


def _s5_fwd(proj, bbr, bbi, a_re, a_im, ctr, cti, d_skip, d_s5):
    l = proj.shape[0]
    nb, uc, ns = bbr.shape
    t = min(S5_T, l)
    nt = l // t

    def body(u_ref, bbr_ref, bbi_ref, ar_ref, ai_ref, ctr_ref, cti_ref, d_ref,
             y_ref, car_r_ref, car_i_ref, cr, ci):
        @pl.when(pl.program_id(1) == 0)
        def _():
            cr[...] = jnp.zeros_like(cr)
            ci[...] = jnp.zeros_like(ci)

        car_r_ref[...] = cr[...]
        car_i_ref[...] = ci[...]
        u = u_ref[...]
        sr, si = _s5_states(u.astype(BF16), bbr_ref[...], bbi_ref[...], ar_ref[...], ai_ref[...],
                            cr[...], ci[...])
        cr[...] = sr[t - 1:t, :]
        ci[...] = si[t - 1:t, :]
        y_ref[...] = (_bdot(sr, ctr_ref[...]) - _bdot(si, cti_ref[...]) + d_ref[...] * u)

    per_block = lambda shape: pl.BlockSpec((None,) + shape, lambda b, n: (b, 0, 0))
    return pl.pallas_call(
        body, name="s5_fwd",
        out_shape=(jax.ShapeDtypeStruct((l, d_s5), F32),
                   jax.ShapeDtypeStruct((nt, 1, nb * ns), F32),
                   jax.ShapeDtypeStruct((nt, 1, nb * ns), F32)),
        grid=(nb, nt),
        in_specs=[pl.BlockSpec((t, uc), lambda b, n: (n, b)),
                  per_block((uc, ns)), per_block((uc, ns)),
                  per_block((1, ns)), per_block((1, ns)),
                  per_block((ns, uc)), per_block((ns, uc)),
                  pl.BlockSpec((1, uc), lambda b, n: (0, b))],
        out_specs=(pl.BlockSpec((t, uc), lambda b, n: (n, b)),
                   pl.BlockSpec((None, 1, ns), lambda b, n: (n, 0, b)),
                   pl.BlockSpec((None, 1, ns), lambda b, n: (n, 0, b))),
        scratch_shapes=[pltpu.VMEM((1, ns), F32), pltpu.VMEM((1, ns), F32)],
        compiler_params=_cparams("parallel", "arbitrary"),
    )(proj, bbr, bbi, a_re, a_im, ctr, cti, d_skip)


def _s5_bwd(proj, dy, bbr, bbi, a_re, a_im, cbr, cbi, d_skip, car_r, car_i):
    l, d_s5 = dy.shape
    nb, uc, ns = bbr.shape
    t = min(S5_T, l)
    nt = l // t

    def body(u_ref, dy_ref, bbr_ref, bbi_ref, ar_ref, ai_ref, cbr_ref, cbi_ref, d_ref,
             car_r_ref, car_i_ref,
             du_ref, dar_ref, dai_ref, dbbr_ref, dbbi_ref, dcbr_ref, dcbi_ref, dd_ref, gcr, gci):
        @pl.when(pl.program_id(1) == 0)
        def _():
            gcr[...] = jnp.zeros_like(gcr)
            gci[...] = jnp.zeros_like(gci)
            for ref in (dar_ref, dai_ref, dbbr_ref, dbbi_ref, dcbr_ref, dcbi_ref, dd_ref):
                ref[...] = jnp.zeros_like(ref)

        row = lax.broadcasted_iota(jnp.int32, (t, 1), 0)
        u, dy = u_ref[...], dy_ref[...]
        u_bf, dy_bf = u.astype(BF16), dy.astype(BF16)
        ar, ai = ar_ref[...], ai_ref[...]
        cr, ci = car_r_ref[...], car_i_ref[...]
        sr, si = _s5_states(u_bf, bbr_ref[...], bbi_ref[...], ar, ai, cr, ci)
        first = row == 0
        pr = jnp.where(first, cr, pltpu.roll(sr, 1, 0))
        pi = jnp.where(first, ci, pltpu.roll(si, 1, 0))
        last = row == t - 1
        gr = _dot(dy_bf, cbr_ref[...]) + jnp.where(last, ar * gcr[...] + ai * gci[...], 0.0)
        gi = -_dot(dy_bf, cbi_ref[...]) + jnp.where(last, ar * gci[...] - ai * gcr[...], 0.0)
        gr, gi = _scan_rows(gr, gi, ar, -ai, reverse=True)
        gcr[...] = gr[0:1, :]
        gci[...] = gi[0:1, :]
        dar_ref[...] += jnp.sum(gr * pr + gi * pi, axis=0, keepdims=True)
        dai_ref[...] += jnp.sum(gi * pr - gr * pi, axis=0, keepdims=True)
        gr_bf, gi_bf = gr.astype(BF16), gi.astype(BF16)
        tn = ((0,), (0,))
        dbbr_ref[...] += _dot(u_bf, gr_bf, tn)
        dbbi_ref[...] += _dot(u_bf, gi_bf, tn)
        dcbr_ref[...] += _dot(dy_bf, sr.astype(BF16), tn)
        dcbi_ref[...] -= _dot(dy_bf, si.astype(BF16), tn)
        nt_dims = ((1,), (1,))
        du = _dot(gr_bf, bbr_ref[...], nt_dims) + _dot(gi_bf, bbi_ref[...], nt_dims) + dy * d_ref[...]
        du_ref[...] = du.astype(BF16)
        dd_ref[...] += jnp.sum(dy * u, axis=0, keepdims=True)

    rev = lambda n: nt - 1 - n
    per_block = lambda shape: pl.BlockSpec((None,) + shape, lambda b, n: (b, 0, 0))
    acc = jax.ShapeDtypeStruct((nb, uc, ns), F32)
    vec = jax.ShapeDtypeStruct((nb, 1, ns), F32)
    return pl.pallas_call(
        body, name="s5_bwd",
        out_shape=(jax.ShapeDtypeStruct((l, d_s5), BF16), vec, vec, acc, acc, acc, acc,
                   jax.ShapeDtypeStruct((1, d_s5), F32)),
        grid=(nb, nt),
        in_specs=[pl.BlockSpec((t, uc), lambda b, n: (rev(n), b)),
                  pl.BlockSpec((t, uc), lambda b, n: (rev(n), b)),
                  per_block((uc, ns)), per_block((uc, ns)),
                  per_block((1, ns)), per_block((1, ns)),
                  per_block((uc, ns)), per_block((uc, ns)),
                  pl.BlockSpec((1, uc), lambda b, n: (0, b)),
                  pl.BlockSpec((None, 1, ns), lambda b, n: (rev(n), 0, b)),
                  pl.BlockSpec((None, 1, ns), lambda b, n: (rev(n), 0, b))],
        out_specs=(pl.BlockSpec((t, uc), lambda b, n: (rev(n), b)),
                   per_block((1, ns)), per_block((1, ns)),
                   per_block((uc, ns)), per_block((uc, ns)),
                   per_block((uc, ns)), per_block((uc, ns)),
                   pl.BlockSpec((1, uc), lambda b, n: (0, b))),
        scratch_shapes=[pltpu.VMEM((1, ns), F32), pltpu.VMEM((1, ns), F32)],
        compiler_params=_cparams("parallel", "arbitrary"),
    )(proj, dy, bbr, bbi, a_re, a_im, cbr, cbi, d_skip, car_r, car_i)


def _s5_glu_fwd(y1, proj, off_z, wglu):
    l, d = y1.shape

    def body(y_ref, z_ref, w_ref, o_ref):
        y2 = _gelu(y_ref[...])
        y3 = y2 * _sigmoid(_bdot(y2, w_ref[...]))
        o_ref[...] = (y3 * _silu(z_ref[...])).astype(BF16)

    return pl.pallas_call(
        body, name="s5_glu_fwd",
        out_shape=jax.ShapeDtypeStruct((l, d), BF16),
        grid=(l // ROW_TILE,),
        in_specs=[pl.BlockSpec((ROW_TILE, d), lambda i: (i, 0)),
                  pl.BlockSpec((ROW_TILE, d), lambda i: (i, off_z // d)),
                  pl.BlockSpec((d, d), lambda i: (0, 0))],
        out_specs=pl.BlockSpec((ROW_TILE, d), lambda i: (i, 0)),
        compiler_params=_cparams("parallel"),
    )(y1, proj, wglu)


def _s5_glu_bwd(y1, proj, off_z, wglu, dout):
    l, d = y1.shape

    def body(y_ref, z_ref, w_ref, do_ref, dy_ref, dz_ref, dw_ref):
        y2, gelu_vjp = jax.vjp(_gelu, y_ref[...])
        z = z_ref[...]
        sz, silu_vjp = jax.vjp(_silu, z)
        y2_bf = y2.astype(BF16)
        sg = _sigmoid(_dot(y2_bf, w_ref[...]))
        dout = do_ref[...]
        dy3 = dout * sz
        dz_ref[...] = silu_vjp(dout * (y2 * sg))[0].astype(BF16)
        dgl = (dy3 * y2 * sg * (1.0 - sg)).astype(BF16)
        dy2 = dy3 * sg + _dot(dgl, w_ref[...], ((1,), (1,)))
        dy_ref[...] = gelu_vjp(dy2)[0]

        @pl.when(pl.program_id(0) == 0)
        def _():
            dw_ref[...] = jnp.zeros_like(dw_ref)

        dw_ref[...] += _dot(y2_bf, dgl, ((0,), (0,)))

    row = pl.BlockSpec((ROW_TILE, d), lambda i: (i, 0))
    full = pl.BlockSpec((d, d), lambda i: (0, 0))
    return pl.pallas_call(
        body, name="s5_glu_bwd",
        out_shape=(jax.ShapeDtypeStruct((l, d), F32), jax.ShapeDtypeStruct((l, d), BF16),
                   jax.ShapeDtypeStruct((d, d), F32)),
        grid=(l // ROW_TILE,),
        in_specs=[row, pl.BlockSpec((ROW_TILE, d), lambda i: (i, off_z // d)), full, row],
        out_specs=(row, row, full),
        compiler_params=_cparams("arbitrary"),
    )(y1, proj, wglu, dout)


def _shift_rows(x, k, back=False):
    if k == 0:
        return x
    t = x.shape[0]
    row = lax.broadcasted_iota(jnp.int32, (t, 1), 0)
    if back:
        return jnp.where(row < t - k, pltpu.roll(x, t - k, 0), 0.0)
    return jnp.where(row >= k, pltpu.roll(x, k, 0), 0.0)


def _dn_conv(x, w_ref):
    return sum(w_ref[CONV_K - 1 - k:CONV_K - k, :] * _shift_rows(x, k) for k in range(CONV_K))


def _dn_post_conv(c, j):
    y = _silu(c)
    n = y * lax.rsqrt(jnp.sum(y * y, axis=-1, keepdims=True) + EPS)
    n = n * jnp.where(j < DN_HEADS, DN_HEAD_DIM ** -0.5, 1.0)
    return jnp.where(j < 2 * DN_HEADS, n, y)


def _dn_prep_fwd(proj, off_qkv, conv_w):
    l = proj.shape[0]
    hd = DN_HEAD_DIM
    nblk = 3 * DN_HEADS

    def body(x_ref, w_ref, o_ref):
        o_ref[...] = _dn_post_conv(_dn_conv(x_ref[...], w_ref), pl.program_id(0))

    return pl.pallas_call(
        body, name="dn_prep_fwd",
        out_shape=jax.ShapeDtypeStruct((l, nblk * hd), F32),
        grid=(nblk,),
        in_specs=[pl.BlockSpec((l, hd), lambda j: (0, off_qkv // hd + j)),
                  pl.BlockSpec((CONV_K, hd), lambda j: (0, j))],
        out_specs=pl.BlockSpec((l, hd), lambda j: (0, j)),
        compiler_params=_cparams("parallel"),
    )(proj, conv_w)


def _dn_prep_bwd(proj, off_qkv, conv_w, dqkv):
    l = proj.shape[0]
    hd = DN_HEAD_DIM
    nblk = 3 * DN_HEADS

    def body(x_ref, w_ref, do_ref, dx_ref, dw_ref):
        x = x_ref[...]
        j = pl.program_id(0)
        _, vjp = jax.vjp(functools.partial(_dn_post_conv, j=j), _dn_conv(x, w_ref))
        dc = vjp(do_ref[...])[0]
        dx = sum(w_ref[CONV_K - 1 - k:CONV_K - k, :] * _shift_rows(dc, k, back=True)
                 for k in range(CONV_K))
        dx_ref[...] = dx.astype(BF16)
        for k in range(CONV_K):
            dw_ref[CONV_K - 1 - k:CONV_K - k, :] = jnp.sum(dc * _shift_rows(x, k), axis=0,
                                                           keepdims=True)

    return pl.pallas_call(
        body, name="dn_prep_bwd",
        out_shape=(jax.ShapeDtypeStruct((l, nblk * hd), BF16),
                   jax.ShapeDtypeStruct((CONV_K, nblk * hd), F32)),
        grid=(nblk,),
        in_specs=[pl.BlockSpec((l, hd), lambda j: (0, off_qkv // hd + j)),
                  pl.BlockSpec((CONV_K, hd), lambda j: (0, j)),
                  pl.BlockSpec((l, hd), lambda j: (0, j))],
        out_specs=(pl.BlockSpec((l, hd), lambda j: (0, j)),
                   pl.BlockSpec((CONV_K, hd), lambda j: (0, j))),
        compiler_params=_cparams("parallel"),
    )(proj, conv_w, dqkv)


def _dn_gate_fn(ba, a_log_row, dt_row):
    lane = lax.broadcasted_iota(jnp.int32, ba.shape, 1)
    beta = _sigmoid(ba)
    g = -jnp.exp(a_log_row) * _softplus(ba + dt_row)
    return jnp.where(lane < DN_HEADS, beta, jnp.where(lane < 2 * DN_HEADS, g, 0.0))


def _dn_gates_fwd(proj, off_ba, a_log_row, dt_row):
    l = proj.shape[0]
    row = pl.BlockSpec((ROW_TILE, 128), lambda i: (i, off_ba // 128))
    vec = pl.BlockSpec((1, 128), lambda i: (0, 0))

    def body(ba_ref, al_ref, dt_ref, o_ref):
        o_ref[...] = _dn_gate_fn(ba_ref[...], al_ref[...], dt_ref[...])

    return pl.pallas_call(
        body, name="dn_gates_fwd",
        out_shape=jax.ShapeDtypeStruct((l, 128), F32),
        grid=(l // ROW_TILE,),
        in_specs=[row, vec, vec],
        out_specs=pl.BlockSpec((ROW_TILE, 128), lambda i: (i, 0)),
        compiler_params=_cparams("parallel"),
    )(proj, a_log_row, dt_row)


def _dn_gates_bwd(proj, off_ba, a_log_row, dt_row, dgb_heads):
    l = proj.shape[0]
    nh = dgb_heads.shape[0]
    row = pl.BlockSpec((ROW_TILE, 128), lambda i: (i, off_ba // 128))
    vec = pl.BlockSpec((1, 128), lambda i: (0, 0))

    def body(ba_ref, al_ref, dt_ref, dg_ref, dba_ref, dal_ref, ddt_ref):
        _, vjp = jax.vjp(_dn_gate_fn, ba_ref[...], al_ref[...], dt_ref[...])
        dgb = dg_ref[0]
        for h in range(1, nh):
            dgb = dgb + dg_ref[h]
        dba, dal, ddt = vjp(dgb)
        dba_ref[...] = dba.astype(BF16)

        @pl.when(pl.program_id(0) == 0)
        def _():
            dal_ref[...] = jnp.zeros_like(dal_ref)
            ddt_ref[...] = jnp.zeros_like(ddt_ref)

        dal_ref[...] += dal
        ddt_ref[...] += ddt

    return pl.pallas_call(
        body, name="dn_gates_bwd",
        out_shape=(jax.ShapeDtypeStruct((l, 128), BF16), jax.ShapeDtypeStruct((1, 128), F32),
                   jax.ShapeDtypeStruct((1, 128), F32)),
        grid=(l // ROW_TILE,),
        in_specs=[row, vec, vec, pl.BlockSpec((nh, ROW_TILE, 128), lambda i: (0, i, 0))],
        out_specs=(pl.BlockSpec((ROW_TILE, 128), lambda i: (i, 0)), vec, vec),
        compiler_params=_cparams("arbitrary"),
    )(proj, a_log_row, dt_row, dgb_heads)


def _dn_chunk_fn(state, q, k, v, gb, head):
    c = q.shape[0]
    lane = lax.broadcasted_iota(jnp.int32, gb.shape, 1)
    ri = lax.broadcasted_iota(jnp.int32, (c, c), 0)
    ci = lax.broadcasted_iota(jnp.int32, (c, c), 1)
    causal, strict = ri >= ci, ri > ci
    eye = (ri == ci).astype(F32)
    rowi = lax.broadcasted_iota(jnp.int32, (c, 1), 0)

    pick = lambda m, at: jnp.sum(jnp.where(lane == at, m, 0.0), axis=1, keepdims=True)
    beta = pick(gb, head)
    gc = pick(_dot(causal.astype(F32), gb, precision=HIGHEST), head + DN_HEADS)
    gc_row = jnp.sum(eye * gc, axis=0, keepdims=True)
    decay = jnp.where(causal, jnp.exp(jnp.where(causal, gc - gc_row, 0.0)), 0.0)
    nt_dims = ((1,), (1,))
    a_mat = jnp.where(strict, beta * _bdot(k, k, nt_dims) * decay, 0.0)

    hdot = functools.partial(_dot, precision=HIGHEST)
    t_inv = eye - a_mat
    power = a_mat
    for _ in range(int(math.log2(c)) - 1):
        power = hdot(power, power)
        t_inv = t_inv + hdot(t_inv, power)

    egc = jnp.exp(gc)
    u_c = hdot(t_inv, v * beta)
    w_c = hdot(t_inv, k * (beta * egc))
    qk = _bdot(q, k, nt_dims) * decay
    g_end = jnp.sum(jnp.where(rowi == c - 1, gc, 0.0), axis=0, keepdims=True)
    v_new = u_c - _bdot(w_c, state)
    o = _bdot(q * egc, state) + _bdot(qk, v_new)
    new_state = state * jnp.exp(g_end) + _bdot(k * jnp.exp(g_end - gc), v_new, ((0,), (0,)))
    return o, new_state


def _dn_chunk_specs(n_chunks, order):
    hd, nh = DN_HEAD_DIM, DN_HEADS
    qkv = lambda part: pl.BlockSpec((CHUNK, hd), lambda h, n: (order(n), part * nh + h))
    gb = pl.BlockSpec((CHUNK, 128), lambda h, n: (order(n), 0))
    state = pl.BlockSpec((None, None, hd, hd), lambda h, n: (h, order(n), 0, 0))
    return qkv, gb, state


def _dn_chunk_fwd(qkv, gb):
    l = qkv.shape[0]
    hd, nh = DN_HEAD_DIM, DN_HEADS
    n_chunks = l // CHUNK
    qkv_spec, gb_spec, state_spec = _dn_chunk_specs(n_chunks, lambda n: n)

    def body(q_ref, k_ref, v_ref, gb_ref, o_ref, s_ref, state):
        @pl.when(pl.program_id(1) == 0)
        def _():
            state[...] = jnp.zeros_like(state)

        s_ref[...] = state[...]
        o, new_state = _dn_chunk_fn(state[...], q_ref[...], k_ref[...], v_ref[...], gb_ref[...],
                                    pl.program_id(0))
        o_ref[...] = o
        state[...] = new_state

    return pl.pallas_call(
        body, name="dn_chunk_fwd",
        out_shape=(jax.ShapeDtypeStruct((l, nh * hd), F32),
                   jax.ShapeDtypeStruct((nh, n_chunks, hd, hd), F32)),
        grid=(nh, n_chunks),
        in_specs=[qkv_spec(0), qkv_spec(1), qkv_spec(2), gb_spec],
        out_specs=(pl.BlockSpec((CHUNK, hd), lambda h, n: (n, h)), state_spec),
        scratch_shapes=[pltpu.VMEM((hd, hd), F32)],
        compiler_params=_cparams("parallel", "arbitrary"),
    )(qkv, qkv, qkv, gb)


def _dn_chunk_bwd(qkv, gb, states, do):
    l = qkv.shape[0]
    hd, nh = DN_HEAD_DIM, DN_HEADS
    n_chunks = l // CHUNK
    rev = lambda n: n_chunks - 1 - n
    qkv_spec, gb_spec, state_spec = _dn_chunk_specs(n_chunks, rev)

    def body(q_ref, k_ref, v_ref, gb_ref, s_ref, do_ref, dq_ref, dk_ref, dv_ref, dgb_ref, dstate):
        @pl.when(pl.program_id(1) == 0)
        def _():
            dstate[...] = jnp.zeros_like(dstate)

        fn = functools.partial(_dn_chunk_fn, head=pl.program_id(0))
        _, vjp = jax.vjp(fn, s_ref[...], q_ref[...], k_ref[...], v_ref[...], gb_ref[...])
        ds, dq, dk, dv, dgb = vjp((do_ref[...], dstate[...]))
        dstate[...] = ds
        dq_ref[...] = dq
        dk_ref[...] = dk
        dv_ref[...] = dv
        dgb_ref[...] = dgb

    head_out = pl.BlockSpec((CHUNK, hd), lambda h, n: (rev(n), h))
    out = jax.ShapeDtypeStruct((l, nh * hd), F32)
    return pl.pallas_call(
        body, name="dn_chunk_bwd",
        out_shape=(out, out, out, jax.ShapeDtypeStruct((nh, l, 128), F32)),
        grid=(nh, n_chunks),
        in_specs=[qkv_spec(0), qkv_spec(1), qkv_spec(2), gb_spec, state_spec, head_out],
        out_specs=(head_out, head_out, head_out,
                   pl.BlockSpec((None, CHUNK, 128), lambda h, n: (h, rev(n), 0))),
        scratch_shapes=[pltpu.VMEM((hd, hd), F32)],
        compiler_params=_cparams("parallel", "arbitrary"),
    )(qkv, qkv, qkv, gb, states, do)


def _dn_out_fn(o, z, w):
    return _rmsnorm(o, w) * _silu(z)


def _dn_out_fwd(o, proj, off_z, w):
    l, d = o.shape
    hd = DN_HEAD_DIM
    blk = lambda off: pl.BlockSpec((ROW_TILE, hd), lambda i, h: (i, off // hd + h))

    def body(o_ref, z_ref, w_ref, out_ref):
        out_ref[...] = _dn_out_fn(o_ref[...], z_ref[...], w_ref[...]).astype(BF16)

    return pl.pallas_call(
        body, name="dn_out_fwd",
        out_shape=jax.ShapeDtypeStruct((l, d), BF16),
        grid=(l // ROW_TILE, d // hd),
        in_specs=[blk(0), blk(off_z), pl.BlockSpec((1, hd), lambda i, h: (0, 0))],
        out_specs=blk(0),
        compiler_params=_cparams("parallel", "parallel"),
    )(o, proj, w)


def _dn_out_bwd(o, proj, off_z, w, dout):
    l, d = o.shape
    hd = DN_HEAD_DIM
    blk = lambda off: pl.BlockSpec((ROW_TILE, hd), lambda i, h: (i, off // hd + h))
    vec = pl.BlockSpec((1, hd), lambda i, h: (0, 0))

    def body(o_ref, z_ref, w_ref, dout_ref, do_ref, dz_ref, dw_ref):
        _, vjp = jax.vjp(_dn_out_fn, o_ref[...], z_ref[...], w_ref[...])
        do, dz, dw = vjp(dout_ref[...])
        do_ref[...] = do
        dz_ref[...] = dz.astype(BF16)

        @pl.when((pl.program_id(0) == 0) & (pl.program_id(1) == 0))
        def _():
            dw_ref[...] = jnp.zeros_like(dw_ref)

        dw_ref[...] += dw

    return pl.pallas_call(
        body, name="dn_out_bwd",
        out_shape=(jax.ShapeDtypeStruct((l, d), F32), jax.ShapeDtypeStruct((l, d), BF16),
                   jax.ShapeDtypeStruct((1, hd), F32)),
        grid=(l // ROW_TILE, d // hd),
        in_specs=[blk(0), blk(off_z), vec, blk(0)],
        out_specs=(blk(0), blk(0), vec),
        compiler_params=_cparams("arbitrary", "arbitrary"),
    )(o, proj, w, dout)


def _row_tile(rows, cols, budget_bytes=1 << 20):
    for tr in (rows, 4096, 2048, 1024, 512, 256, 128, 64, 32, 16):
        if tr <= rows and rows % tr == 0 and tr * cols * 4 <= budget_bytes:
            return tr
    raise ValueError((rows, cols))


def _adamw(w, m, v, gslots, name):
    rows, cols = w.shape
    ns = gslots.shape[0]
    tr = _row_tile(rows, cols)
    c1 = 1.0 / (1.0 - ADAM_B1 ** ADAM_STEP)
    c2 = 1.0 / (1.0 - ADAM_B2 ** ADAM_STEP)

    def body(w_ref, m_ref, v_ref, g_ref, go_ref, d_ref, mo_ref, vo_ref):
        g = g_ref[0].astype(F32)
        for s in range(1, ns):
            g = g + g_ref[s].astype(F32)
        m_new = ADAM_B1 * m_ref[...] + (1.0 - ADAM_B1) * g
        v_new = ADAM_B2 * v_ref[...] + (1.0 - ADAM_B2) * (g * g)
        go_ref[...] = g
        mo_ref[...] = m_new
        vo_ref[...] = v_new
        d_ref[...] = -ADAM_LR * ((m_new * c1) / (jnp.sqrt(v_new * c2) + ADAM_EPS) + ADAM_WD * w_ref[...])

    blk = pl.BlockSpec((tr, cols), lambda i: (i, 0))
    o = jax.ShapeDtypeStruct((rows, cols), F32)
    return pl.pallas_call(
        body, name=name, out_shape=(o, o, o, o),
        grid=(rows // tr,),
        in_specs=[blk, blk, blk, pl.BlockSpec((ns, tr, cols), lambda i: (0, i, 0))],
        out_specs=(blk, blk, blk, blk),
        compiler_params=_cparams("parallel"),
    )(w, m, v, gslots)


def _slot_sum(gslots, name):
    ns, rows, cols = gslots.shape
    tr = _row_tile(rows, cols)

    def body(g_ref, o_ref):
        g = g_ref[0]
        for s in range(1, ns):
            g = g + g_ref[s]
        o_ref[...] = g

    return pl.pallas_call(
        body, name=name, out_shape=jax.ShapeDtypeStruct((rows, cols), F32),
        grid=(rows // tr,),
        in_specs=[pl.BlockSpec((ns, tr, cols), lambda i: (0, i, 0))],
        out_specs=pl.BlockSpec((tr, cols), lambda i: (i, 0)),
        compiler_params=_cparams("parallel"),
    )(gslots)


HBM_SPEC = pl.BlockSpec(memory_space=pl.ANY)


def _all_gather(arrs, name):
    n = len(arrs)

    def body(*refs):
        ins, outs = refs[:n], refs[n:2 * n]
        send_sems, recv_sems, local_sems = refs[2 * n:]
        x, y, c = lax.axis_index("x"), lax.axis_index("y"), lax.axis_index("c")
        me, sibling = (x, y, c), (x, y, 1 - c)
        chips = [(1 - x, y), (x, 1 - y), (1 - x, 1 - y)]
        index = lambda px, py, pc: 4 * px + 2 * py + pc

        def copy(a, k, block, to, src=None):
            rows = outs[a].at[index(*block)]
            return pltpu.make_async_remote_copy(
                src_ref=rows if src is None else src, dst_ref=rows,
                send_sem=send_sems.at[a, k], recv_sem=recv_sems.at[a, k],
                device_id=to, device_id_type=MESH)

        mine = [pltpu.make_async_copy(ins[a], outs[a].at[index(*me)], local_sems.at[a])
                for a in range(n)]
        for cp in mine:
            cp.start()
        first = []
        for a in range(n):
            first.append(copy(a, 0, me, sibling, src=ins[a]))
            first += [copy(a, 1 + j, me, (*chip, c), src=ins[a]) for j, chip in enumerate(chips)]
        for cp in first:
            cp.start()
        passed = []
        for j, chip in enumerate(chips):
            for a in range(n):
                copy(a, 1 + j, (*chip, c), me).wait_recv()
                fwd = copy(a, 4 + j, (*chip, c), sibling)
                fwd.start()
                passed.append(fwd)
        for a in range(n):
            copy(a, 0, sibling, me).wait_recv()
            for j, chip in enumerate(chips):
                copy(a, 4 + j, (*chip, 1 - c), me).wait_recv()
        for cp in first + passed:
            cp.wait_send()
        for cp in mine:
            cp.wait()

    return pl.pallas_call(
        body, name=name,
        out_shape=[jax.ShapeDtypeStruct((N_DEV,) + a.shape, a.dtype) for a in arrs],
        in_specs=[HBM_SPEC] * n, out_specs=[HBM_SPEC] * n,
        scratch_shapes=[pltpu.SemaphoreType.DMA((n, 7)), pltpu.SemaphoreType.DMA((n, 7)),
                        pltpu.SemaphoreType.DMA((n,))],
    )(*arrs)


def _slice_exchange(arrs, name):
    n = len(arrs)

    def body(*refs):
        ins, outs = refs[:n], refs[n:2 * n]
        send_sems, recv_sems, local_sems = refs[2 * n:]
        x, y, c = lax.axis_index("x"), lax.axis_index("y"), lax.axis_index("c")
        me = 4 * x + 2 * y + c
        flip = lambda v, bit: 1 - v if bit else v
        mine = [pltpu.make_async_copy(ins[a].at[me], outs[a].at[me], local_sems.at[a])
                for a in range(n)]
        for cp in mine:
            cp.start()
        copies = []
        for k in range(1, N_DEV):
            px, py, pc = flip(x, k & 4), flip(y, k & 2), flip(c, k & 1)
            peer = 4 * px + 2 * py + pc
            for a in range(n):
                copies.append(pltpu.make_async_remote_copy(
                    src_ref=ins[a].at[peer], dst_ref=outs[a].at[me],
                    send_sem=send_sems.at[a, k - 1], recv_sem=recv_sems.at[a, k - 1],
                    device_id=(px, py, pc), device_id_type=MESH))
        for cp in copies:
            cp.start()
        for cp in copies:
            cp.wait()
        for cp in mine:
            cp.wait()

    return pl.pallas_call(
        body, name=name,
        out_shape=[jax.ShapeDtypeStruct(a.shape, a.dtype) for a in arrs],
        in_specs=[HBM_SPEC] * n, out_specs=[HBM_SPEC] * n,
        scratch_shapes=[pltpu.SemaphoreType.DMA((n, 7)), pltpu.SemaphoreType.DMA((n, 7)),
                        pltpu.SemaphoreType.DMA((n,))],
    )(*arrs)


def _block_diag(t):
    nb, gpb, r, c = t.shape
    eye = jnp.eye(gpb, dtype=t.dtype)
    return jnp.einsum("ngrc,gh->ngrhc", t, eye).reshape(nb, gpb * r, gpb * c)


def _diag_blocks(t, r, c):
    nb = t.shape[0]
    gpb = t.shape[1] // r
    t = t.reshape(nb, gpb, r, gpb, c)
    return jnp.einsum("ngrhc,gh->ngrc", t, jnp.eye(gpb, dtype=t.dtype))


def _pack_rows(parts):
    flat = jnp.concatenate([p.reshape(-1).astype(F32) for p in parts])
    pad = (-flat.shape[0]) % (256 * 128)
    return jnp.pad(flat, (0, pad)).reshape(-1, 128)


def _unpack_rows(packed, shapes):
    flat = packed.reshape(-1)
    out, at = [], 0
    for shape in shapes:
        size = math.prod(shape)
        out.append(flat[at:at + size].reshape(shape))
        at += size
    return out


def kernel(x, ln_w, w_in, s5_lam_re, s5_lam_im, s5_log_step, s5_b_re, s5_b_im, s5_c_re, s5_c_im, s5_d, s5_w_glu, s5_w_up, dn_conv_w, dn_a_log, dn_dt_bias, dn_norm_w, dn_w_up, w_out, final_norm_w, loss_target, m_ln_w, m_w_in, m_s5_lam_re, m_s5_lam_im, m_s5_log_step, m_s5_b_re, m_s5_b_im, m_s5_c_re, m_s5_c_im, m_s5_d, m_s5_w_glu, m_s5_w_up, m_dn_conv_w, m_dn_a_log, m_dn_dt_bias, m_dn_norm_w, m_dn_w_up, m_w_out, m_final_norm_w, v_ln_w, v_w_in, v_s5_lam_re, v_s5_lam_im, v_s5_log_step, v_s5_b_re, v_s5_b_im, v_s5_c_re, v_s5_c_im, v_s5_d, v_s5_w_glu, v_s5_w_up, v_dn_conv_w, v_dn_a_log, v_dn_dt_bias, v_dn_norm_w, v_dn_w_up, v_w_out, v_final_norm_w):
    weights = dict(ln_w=ln_w, w_in=w_in, s5_lam_re=s5_lam_re, s5_lam_im=s5_lam_im,
                   s5_log_step=s5_log_step, s5_b_re=s5_b_re, s5_b_im=s5_b_im, s5_c_re=s5_c_re,
                   s5_c_im=s5_c_im, s5_d=s5_d, s5_w_glu=s5_w_glu, s5_w_up=s5_w_up,
                   dn_conv_w=dn_conv_w, dn_a_log=dn_a_log, dn_dt_bias=dn_dt_bias,
                   dn_norm_w=dn_norm_w, dn_w_up=dn_w_up, w_out=w_out, final_norm_w=final_norm_w)
    mom_m = dict(ln_w=m_ln_w, w_in=m_w_in, s5_lam_re=m_s5_lam_re, s5_lam_im=m_s5_lam_im,
                 s5_log_step=m_s5_log_step, s5_b_re=m_s5_b_re, s5_b_im=m_s5_b_im,
                 s5_c_re=m_s5_c_re, s5_c_im=m_s5_c_im, s5_d=m_s5_d, s5_w_glu=m_s5_w_glu,
                 s5_w_up=m_s5_w_up, dn_conv_w=m_dn_conv_w, dn_a_log=m_dn_a_log,
                 dn_dt_bias=m_dn_dt_bias, dn_norm_w=m_dn_norm_w, dn_w_up=m_dn_w_up,
                 w_out=m_w_out, final_norm_w=m_final_norm_w)
    mom_v = dict(ln_w=v_ln_w, w_in=v_w_in, s5_lam_re=v_s5_lam_re, s5_lam_im=v_s5_lam_im,
                 s5_log_step=v_s5_log_step, s5_b_re=v_s5_b_re, s5_b_im=v_s5_b_im,
                 s5_c_re=v_s5_c_re, s5_c_im=v_s5_c_im, s5_d=v_s5_d, s5_w_glu=v_s5_w_glu,
                 s5_w_up=v_s5_w_up, dn_conv_w=v_dn_conv_w, dn_a_log=v_dn_a_log,
                 dn_dt_bias=v_dn_dt_bias, dn_norm_w=v_dn_norm_w, dn_w_up=v_dn_w_up,
                 w_out=v_w_out, final_norm_w=v_final_norm_w)
    names = list(weights)

    l, d = x.shape[1], x.shape[2]
    d_s5 = d // 2
    groups = d_s5 // S5_GROUP
    nb = groups // S5_GPB
    d_dn = DN_HEADS * DN_HEAD_DIM
    w_in_cols = w_in.shape[2]
    d_in = N_DEV * w_in_cols
    off_ba_src = 2 * d_s5 + 4 * d_dn
    off_u, off_zs, off_qkv, off_zd = 0, d_s5, 2 * d_s5, 2 * d_s5 + 3 * d_dn
    off_ba = off_zd + d_dn
    off_gs = off_ba + BA_PAD
    off_gd = off_gs + d
    n_proj = off_gd + d
    x2d, tgt2d = x[0], loss_target[0]
    my_index = 4 * lax.axis_index("x") + 2 * lax.axis_index("y") + lax.axis_index("c")

    g_win, g_glu, g_sup, g_dup, g_wout, g_conv = _all_gather(
        [w_in[0].astype(BF16), s5_w_glu[0].astype(BF16), s5_w_up[0].astype(BF16),
         dn_w_up[0].astype(BF16), w_out[0].astype(BF16), dn_conv_w[0]], name="gather_weights")
    w_full = jnp.transpose(g_win, (1, 0, 2)).reshape(d, d_in)
    w_cat = jnp.concatenate(
        [w_full[:, :off_ba_src],
         jnp.pad(w_full[:, off_ba_src:off_ba_src + 2 * DN_HEADS], ((0, 0), (0, BA_PAD - 2 * DN_HEADS))),
         w_full[:, off_ba_src + 2 * DN_HEADS:]], axis=1)
    wglu_full = g_glu.reshape(d_s5, d_s5)
    wsup_full = jnp.transpose(g_sup, (1, 0, 2)).reshape(d_s5, d)
    wdup_full = jnp.transpose(g_dup, (1, 0, 2)).reshape(d_dn, d)
    wout_full = g_wout.reshape(d, d)
    conv_full = jnp.transpose(g_conv, (1, 0, 2)).reshape(CONV_K, 3 * d_dn)

    lam_re, lam_im = s5_lam_re[0], s5_lam_im[0]
    log_step = s5_log_step[0].reshape(groups, 1)
    b_re = s5_b_re[0].reshape(groups * S5_STATE, S5_GROUP)
    b_im = s5_b_im[0].reshape(groups * S5_STATE, S5_GROUP)
    abar_re, abar_im, f_re, f_im = _s5_disc_fwd(lam_re, lam_im, log_step)
    f_re_col, f_im_col = f_re.reshape(-1, 1), f_im.reshape(-1, 1)
    bb_re, bb_im = _s5_bbar_fwd(f_re_col, f_im_col, b_re, b_im)

    def bb_blocks(t):
        t = t.reshape(nb, S5_GPB, S5_STATE, S5_GROUP).transpose(0, 1, 3, 2)
        return _block_diag(t).astype(BF16)

    def c_blocks(t):
        return _block_diag(t.reshape(nb, S5_GPB, S5_GROUP, S5_STATE)).astype(BF16)

    bbr, bbi = bb_blocks(bb_re), bb_blocks(bb_im)
    cbr, cbi = c_blocks(s5_c_re[0]), c_blocks(s5_c_im[0])
    ctr, cti = jnp.transpose(cbr, (0, 2, 1)), jnp.transpose(cbi, (0, 2, 1))
    a_re = abar_re.reshape(nb, 1, S5_GPB * S5_STATE)
    a_im = abar_im.reshape(nb, 1, S5_GPB * S5_STATE)

    h = _rms_fwd(x2d, ln_w)
    proj = _mm(h, w_cat, tn=1536, name="proj")
    y1, car_r, car_i = _s5_fwd(proj, bbr, bbi, a_re, a_im, ctr, cti, s5_d, d_s5)
    out_s = _s5_glu_fwd(y1, proj, off_zs, wglu_full)
    y_s = _mm(out_s, wsup_full, name="s5_up")

    a_log_row = jnp.pad(dn_a_log, ((0, 0), (DN_HEADS, 128 - 2 * DN_HEADS)))
    dt_row = jnp.pad(dn_dt_bias, ((0, 0), (DN_HEADS, 128 - 2 * DN_HEADS)))
    qkv = _dn_prep_fwd(proj, off_qkv, conv_full)
    gb = _dn_gates_fwd(proj, off_ba, a_log_row, dt_row)
    o_dn, states = _dn_chunk_fwd(qkv, gb)
    out_d = _dn_out_fwd(o_dn, proj, off_zd, dn_norm_w)
    y_d = _mm(out_d, wdup_full, name="dn_up")

    mixed = _merge_fwd(proj, off_gs, off_gd, y_s, y_d)
    branch = _mm(mixed, wout_full, name="w_out")
    dx2, dx2_bf, loss_dev, d_final_w = _final(x2d, branch, final_norm_w.reshape(1, d), tgt2d)

    g_wout_full = _mm(mixed, dx2_bf, ta=True, out_dtype=BF16, name="grad_w_out")
    dmixed = _mm(dx2_bf, wout_full, tb=True, name="d_mixed")
    dgs, dgd, dys, dyd = _merge_bwd(proj, off_gs, off_gd, y_s, y_d, dmixed)

    g_dup_full = _mm(out_d, dyd, ta=True, out_dtype=BF16, name="grad_dn_up")
    dout_d = _mm(dyd, wdup_full, tb=True, name="d_out_d")
    do_dn, dzd, d_norm_w = _dn_out_bwd(o_dn, proj, off_zd, dn_norm_w, dout_d)
    dq, dk, dv, dgb_heads = _dn_chunk_bwd(qkv, gb, states, do_dn)
    dba, d_a_log_row, d_dt_row = _dn_gates_bwd(proj, off_ba, a_log_row, dt_row, dgb_heads)
    dqkv_pre, d_conv_full = _dn_prep_bwd(proj, off_qkv, conv_full,
                                         jnp.concatenate([dq, dk, dv], axis=1))

    g_sup_full = _mm(out_s, dys, ta=True, out_dtype=BF16, name="grad_s5_up")
    dout_s = _mm(dys, wsup_full, tb=True, name="d_out_s")
    dy1, dzs, g_glu_full = _s5_glu_bwd(y1, proj, off_zs, wglu_full, dout_s)
    (du, d_a_re, d_a_im, d_bbr, d_bbi, d_cbr, d_cbi, d_s5_d) = _s5_bwd(
        proj, dy1, bbr, bbi, a_re, a_im, cbr, cbi, s5_d, car_r, car_i)

    def from_bb_blocks(t):
        t = _diag_blocks(t, S5_GROUP, S5_STATE).transpose(0, 1, 3, 2)
        return t.reshape(groups * S5_STATE, S5_GROUP)

    d_f_re, d_f_im, d_b_re, d_b_im = _s5_bbar_bwd(f_re_col, f_im_col, b_re, b_im,
                                                 from_bb_blocks(d_bbr), from_bb_blocks(d_bbi))
    d_lam_re, d_lam_im, d_log_step = _s5_disc_bwd(
        lam_re, lam_im, log_step, d_a_re.reshape(groups, S5_STATE), d_a_im.reshape(groups, S5_STATE),
        d_f_re.reshape(groups, S5_STATE), d_f_im.reshape(groups, S5_STATE))
    d_c_re = _diag_blocks(d_cbr, S5_GROUP, S5_STATE).reshape(groups, S5_GROUP, S5_STATE)
    d_c_im = _diag_blocks(d_cbi, S5_GROUP, S5_STATE).reshape(groups, S5_GROUP, S5_STATE)

    dproj = jnp.concatenate(
        [du, dzs, dqkv_pre, dzd, jnp.pad(dba, ((0, 0), (0, BA_PAD - 128))), dgs, dgd], axis=1)
    g_wcat = _mm(h, dproj, ta=True, out_dtype=BF16, tn=1536, name="grad_w_in")
    dh = _mm(dproj, w_cat, tb=True, tk=1536, name="d_h")
    grad_x, d_ln_w = _rms_bwd(x2d, ln_w, dh, dx2)

    g_win_full = jnp.concatenate(
        [g_wcat[:, :off_ba], g_wcat[:, off_ba:off_ba + 2 * DN_HEADS], g_wcat[:, off_gs:]], axis=1)
    by_cols = lambda t, cols: jnp.transpose(t.reshape(t.shape[0], N_DEV, cols), (1, 0, 2))
    slots = _slice_exchange(
        [by_cols(g_win_full, w_in_cols), g_glu_full.astype(BF16).reshape(N_DEV, d_s5 // N_DEV, d_s5),
         by_cols(g_sup_full, d // N_DEV), by_cols(g_dup_full, d // N_DEV),
         g_wout_full.reshape(N_DEV, d // N_DEV, d)], name="exchange_grads")
    big = ["w_in", "s5_w_glu", "s5_w_up", "dn_w_up", "w_out"]
    results = {}
    for nm, sl in zip(big, slots):
        results[nm] = _adamw(weights[nm][0], mom_m[nm][0], mom_v[nm][0], sl, name="adamw_" + nm)

    small = [nm for nm in names if nm not in big]
    small_grads = dict(
        ln_w=d_ln_w, s5_lam_re=d_lam_re, s5_lam_im=d_lam_im, s5_log_step=d_log_step,
        s5_b_re=d_b_re, s5_b_im=d_b_im, s5_c_re=d_c_re, s5_c_im=d_c_im, s5_d=d_s5_d,
        dn_conv_w=d_conv_full, dn_a_log=d_a_log_row[:, DN_HEADS:2 * DN_HEADS],
        dn_dt_bias=d_dt_row[:, DN_HEADS:2 * DN_HEADS], dn_norm_w=d_norm_w, final_norm_w=d_final_w)
    (all_small,) = _all_gather([_pack_rows([small_grads[nm] for nm in small])], name="gather_small_grads")
    summed = _slot_sum(all_small, name="sum_small_grads")
    full_shapes = [(CONV_K, 3 * d_dn) if nm == "dn_conv_w" else weights[nm].shape for nm in small]
    g_small = dict(zip(small, _unpack_rows(summed, full_shapes)))
    conv_cols = dn_conv_w.shape[2]
    g_small["dn_conv_w"] = lax.dynamic_slice_in_dim(
        g_small["dn_conv_w"], my_index * conv_cols, conv_cols, axis=1).reshape(dn_conv_w.shape)
    packed = [_pack_rows([t[nm] for nm in small]) for t in (weights, mom_m, mom_v, g_small)]
    small_out = _adamw(packed[0], packed[1], packed[2], packed[3][None], name="adamw_small")
    small_shapes = [weights[nm].shape for nm in small]
    for kind, packed_out in enumerate(small_out):
        for nm, val in zip(small, _unpack_rows(packed_out, small_shapes)):
            results.setdefault(nm, [None] * 4)[kind] = val

    loss = lax.psum(loss_dev[0, 0], ("x", "y", "c"))
    outs = [loss, grad_x[None]]
    for kind in range(4):
        outs += [results[nm][kind].reshape(weights[nm].shape) for nm in names]
    return tuple(outs)
```

```python
import functools
import math

import jax
import jax.numpy as jnp
from jax import lax
from jax.experimental import pallas as pl
from jax.experimental.pallas import tpu as pltpu

F32 = jnp.float32
BF16 = jnp.bfloat16
HIGHEST = lax.Precision.HIGHEST
MESH = pl.DeviceIdType.MESH
N_DEV = 8

EPS = 1e-6
S5_GROUP = 16
S5_STATE = 64
S5_GPB = 8
S5_T = 256
DN_HEADS = 8
DN_HEAD_DIM = 128
CHUNK = 64
DN_HEADS_PER_STEP = 8
CONV_K = 4
BA_PAD = 512

ADAM_LR = 0.001
ADAM_B1 = 0.9
ADAM_B2 = 0.999
ADAM_EPS = 1e-08
ADAM_WD = 0.01
ADAM_STEP = 10

VMEM_LIMIT_BYTES = 48 * 1024 * 1024
ROW_TILE = 256


def _cparams(*sem):
    return pltpu.CompilerParams(dimension_semantics=sem if sem else None,
                                vmem_limit_bytes=VMEM_LIMIT_BYTES)


def _sigmoid(x):
    return 1.0 / (1.0 + jnp.exp(-x))


def _silu(x):
    return x * _sigmoid(x)


def _gelu(x):
    return 0.5 * x * (1.0 + jnp.tanh(0.7978845608028654 * (x + 0.044715 * x * x * x)))


def _softplus(x):
    return jnp.maximum(x, 0.0) + jnp.log(1.0 + jnp.exp(-jnp.abs(x)))


def _rmsnorm(x, w):
    return x * lax.rsqrt(jnp.mean(x * x, axis=-1, keepdims=True) + EPS) * w


def _dot(a, b, dims=((1,), (0,)), precision=None):
    return lax.dot_general(a, b, (dims, ((), ())), precision=precision,
                           preferred_element_type=F32)


def _bdot(a, b, dims=((1,), (0,))):
    return _dot(a.astype(BF16), b.astype(BF16), dims)


def _mm(a, b, *, ta=False, tb=False, out_dtype=F32, tm=512, tn=512, tk=None, name):
    k_dim, m_dim = (a.shape if ta else a.shape[::-1])
    n_dim = b.shape[0] if tb else b.shape[1]
    assert (b.shape[1] if tb else b.shape[0]) == k_dim
    tm, tn = min(tm, m_dim), min(tn, n_dim)
    tk = k_dim if tk is None else tk
    assert m_dim % tm == 0 and n_dim % tn == 0 and k_dim % tk == 0
    nk = k_dim // tk
    a_spec = (pl.BlockSpec((tk, tm), lambda i, j, k: (k, i)) if ta
              else pl.BlockSpec((tm, tk), lambda i, j, k: (i, k)))
    b_spec = (pl.BlockSpec((tn, tk), lambda i, j, k: (j, k)) if tb
              else pl.BlockSpec((tk, tn), lambda i, j, k: (k, j)))
    dims = ((0 if ta else 1,), (1 if tb else 0,))

    def body(a_ref, b_ref, o_ref, *scratch):
        p = _bdot(a_ref[...], b_ref[...], dims)
        if nk == 1:
            o_ref[...] = p.astype(o_ref.dtype)
        else:
            acc = scratch[0]
            k = pl.program_id(2)

            @pl.when(k == 0)
            def _():
                acc[...] = p

            @pl.when(k > 0)
            def _():
                acc[...] += p

            @pl.when(k == nk - 1)
            def _():
                o_ref[...] = acc[...].astype(o_ref.dtype)

    return pl.pallas_call(
        body, name=name,
        out_shape=jax.ShapeDtypeStruct((m_dim, n_dim), out_dtype),
        grid=(m_dim // tm, n_dim // tn, nk),
        in_specs=[a_spec, b_spec],
        out_specs=pl.BlockSpec((tm, tn), lambda i, j, k: (i, j)),
        scratch_shapes=[pltpu.VMEM((tm, tn), F32)] if nk > 1 else [],
        compiler_params=_cparams("parallel", "parallel", "arbitrary"),
    )(a, b)


def _rms_fwd(x, w):
    l, d = x.shape

    def body(x_ref, w_ref, h_ref):
        h_ref[...] = _rmsnorm(x_ref[...], w_ref[...]).astype(BF16)

    return pl.pallas_call(
        body, name="rms_fwd",
        out_shape=jax.ShapeDtypeStruct((l, d), BF16),
        grid=(l // ROW_TILE,),
        in_specs=[pl.BlockSpec((ROW_TILE, d), lambda i: (i, 0)),
                  pl.BlockSpec((1, d), lambda i: (0, 0))],
        out_specs=pl.BlockSpec((ROW_TILE, d), lambda i: (i, 0)),
        compiler_params=_cparams("parallel"),
    )(x, w)


def _rms_bwd(x, w, dh, dres):
    l, d = x.shape

    def body(x_ref, w_ref, dh_ref, dres_ref, dx_ref, dw_ref):
        _, vjp = jax.vjp(_rmsnorm, x_ref[...], w_ref[...])
        dx, dw = vjp(dh_ref[...])
        dx_ref[...] = dx + dres_ref[...]

        @pl.when(pl.program_id(0) == 0)
        def _():
            dw_ref[...] = jnp.zeros_like(dw_ref)

        dw_ref[...] += dw

    row = pl.BlockSpec((ROW_TILE, d), lambda i: (i, 0))
    vec = pl.BlockSpec((1, d), lambda i: (0, 0))
    return pl.pallas_call(
        body, name="rms_bwd",
        out_shape=(jax.ShapeDtypeStruct((l, d), F32), jax.ShapeDtypeStruct((1, d), F32)),
        grid=(l // ROW_TILE,),
        in_specs=[row, vec, row, row],
        out_specs=(row, vec),
        compiler_params=_cparams("arbitrary"),
    )(x, w, dh, dres)


def _final(x, r, fw, target):
    l, d = x.shape

    def per_row_loss(x2, w, tgt):
        err = _rmsnorm(x2, w) - tgt
        return 0.5 * jnp.mean(err * err, axis=-1, keepdims=True)

    def body(x_ref, r_ref, w_ref, t_ref, dx_ref, dxb_ref, loss_ref, dw_ref):
        x2 = x_ref[...] + r_ref[...]
        rows, vjp = jax.vjp(functools.partial(per_row_loss, tgt=t_ref[...]), x2, w_ref[...])
        dx2, dw = vjp(jnp.ones_like(rows))
        dx_ref[...] = dx2
        dxb_ref[...] = dx2.astype(BF16)

        @pl.when(pl.program_id(0) == 0)
        def _():
            dw_ref[...] = jnp.zeros_like(dw_ref)
            loss_ref[...] = jnp.zeros_like(loss_ref)

        dw_ref[...] += dw
        loss_ref[...] += jnp.sum(rows, axis=0, keepdims=True)

    row = pl.BlockSpec((ROW_TILE, d), lambda i: (i, 0))
    vec = pl.BlockSpec((1, d), lambda i: (0, 0))
    return pl.pallas_call(
        body, name="final_norm_loss",
        out_shape=(jax.ShapeDtypeStruct((l, d), F32), jax.ShapeDtypeStruct((l, d), BF16),
                   jax.ShapeDtypeStruct((1, 1), F32), jax.ShapeDtypeStruct((1, d), F32)),
        grid=(l // ROW_TILE,),
        in_specs=[row, row, vec, row],
        out_specs=(row, row, pl.BlockSpec((1, 1), lambda i: (0, 0)), vec),
        compiler_params=_cparams("arbitrary"),
    )(x, r, fw, target)


def _merge_fn(gs, gd, ys, yd):
    return _sigmoid(gs) * ys + _sigmoid(gd) * yd


def _merge_fwd(proj, off_gs, off_gd, ys, yd):
    l, d = ys.shape
    cw = 512
    blk = lambda off: pl.BlockSpec((ROW_TILE, cw), lambda i, j: (i, off // cw + j))

    def body(gs_ref, gd_ref, ys_ref, yd_ref, o_ref):
        o_ref[...] = _merge_fn(gs_ref[...], gd_ref[...], ys_ref[...], yd_ref[...]).astype(BF16)

    return pl.pallas_call(
        body, name="merge_fwd",
        out_shape=jax.ShapeDtypeStruct((l, d), BF16),
        grid=(l // ROW_TILE, d // cw),
        in_specs=[blk(off_gs), blk(off_gd), blk(0), blk(0)],
        out_specs=blk(0),
        compiler_params=_cparams("parallel", "parallel"),
    )(proj, proj, ys, yd)


def _merge_bwd(proj, off_gs, off_gd, ys, yd, dmixed):
    l, d = ys.shape
    cw = 512
    blk = lambda off: pl.BlockSpec((ROW_TILE, cw), lambda i, j: (i, off // cw + j))

    def body(gs_ref, gd_ref, ys_ref, yd_ref, dm_ref, dgs_ref, dgd_ref, dys_ref, dyd_ref):
        _, vjp = jax.vjp(_merge_fn, gs_ref[...], gd_ref[...], ys_ref[...], yd_ref[...])
        dgs, dgd, dys, dyd = vjp(dm_ref[...])
        dgs_ref[...] = dgs.astype(BF16)
        dgd_ref[...] = dgd.astype(BF16)
        dys_ref[...] = dys.astype(BF16)
        dyd_ref[...] = dyd.astype(BF16)

    out = jax.ShapeDtypeStruct((l, d), BF16)
    return pl.pallas_call(
        body, name="merge_bwd",
        out_shape=(out, out, out, out),
        grid=(l // ROW_TILE, d // cw),
        in_specs=[blk(off_gs), blk(off_gd), blk(0), blk(0), blk(0)],
        out_specs=(blk(0), blk(0), blk(0), blk(0)),
        compiler_params=_cparams("parallel", "parallel"),
    )(proj, proj, ys, yd, dmixed)


def _s5_disc_fn(lam_re, lam_im, log_step):
    step = jnp.exp(log_step)
    mag = jnp.exp(lam_re * step)
    abar_re = mag * jnp.cos(lam_im * step)
    abar_im = mag * jnp.sin(lam_im * step)
    den = lam_re * lam_re + lam_im * lam_im
    xr = abar_re - 1.0
    f_re = (xr * lam_re + abar_im * lam_im) / den
    f_im = (abar_im * lam_re - xr * lam_im) / den
    return abar_re, abar_im, f_re, f_im


def _s5_disc_fwd(lam_re, lam_im, log_step):
    g, p = lam_re.shape

    def body(lr_ref, li_ref, ls_ref, ar_ref, ai_ref, fr_ref, fi_ref):
        ar, ai, fr, fi = _s5_disc_fn(lr_ref[...], li_ref[...], ls_ref[...])
        ar_ref[...] = ar
        ai_ref[...] = ai
        fr_ref[...] = fr
        fi_ref[...] = fi

    o = jax.ShapeDtypeStruct((g, p), F32)
    return pl.pallas_call(body, name="s5_disc_fwd", out_shape=(o, o, o, o),
                          compiler_params=_cparams())(lam_re, lam_im, log_step)


def _s5_disc_bwd(lam_re, lam_im, log_step, dar, dai, dfr, dfi):
    g, p = lam_re.shape

    def body(lr_ref, li_ref, ls_ref, dar_ref, dai_ref, dfr_ref, dfi_ref, dlr_ref, dli_ref, dls_ref):
        _, vjp = jax.vjp(_s5_disc_fn, lr_ref[...], li_ref[...], ls_ref[...])
        dlr, dli, dls = vjp((dar_ref[...], dai_ref[...], dfr_ref[...], dfi_ref[...]))
        dlr_ref[...] = dlr
        dli_ref[...] = dli
        dls_ref[...] = dls

    o = jax.ShapeDtypeStruct((g, p), F32)
    return pl.pallas_call(body, name="s5_disc_bwd",
                          out_shape=(o, o, jax.ShapeDtypeStruct((g, 1), F32)),
                          compiler_params=_cparams())(lam_re, lam_im, log_step, dar, dai, dfr, dfi)


def _s5_bbar_fwd(f_re, f_im, b_re, b_im):
    n, c = b_re.shape

    def body(fr_ref, fi_ref, br_ref, bi_ref, or_ref, oi_ref):
        fr, fi, br, bi = fr_ref[...], fi_ref[...], br_ref[...], bi_ref[...]
        or_ref[...] = fr * br - fi * bi
        oi_ref[...] = fr * bi + fi * br

    o = jax.ShapeDtypeStruct((n, c), F32)
    return pl.pallas_call(body, name="s5_bbar_fwd", out_shape=(o, o),
                          compiler_params=_cparams())(f_re, f_im, b_re, b_im)


def _s5_bbar_bwd(f_re, f_im, b_re, b_im, dbr, dbi):
    n, c = b_re.shape

    def body(fr_ref, fi_ref, br_ref, bi_ref, dor_ref, doi_ref, dfr_ref, dfi_ref, dbr_ref, dbi_ref):
        fr, fi, br, bi = fr_ref[...], fi_ref[...], br_ref[...], bi_ref[...]
        dor, doi = dor_ref[...], doi_ref[...]
        dfr_ref[...] = jnp.sum(dor * br + doi * bi, axis=-1, keepdims=True)
        dfi_ref[...] = jnp.sum(doi * br - dor * bi, axis=-1, keepdims=True)
        dbr_ref[...] = fr * dor + fi * doi
        dbi_ref[...] = fr * doi - fi * dor

    col = jax.ShapeDtypeStruct((n, 1), F32)
    o = jax.ShapeDtypeStruct((n, c), F32)
    return pl.pallas_call(body, name="s5_bbar_bwd", out_shape=(col, col, o, o),
                          compiler_params=_cparams())(f_re, f_im, b_re, b_im, dbr, dbi)


def _scan_rows(xr, xi, ar, ai, reverse):
    t = xr.shape[0]
    row = lax.broadcasted_iota(jnp.int32, (t, 1), 0)
    pr, pi = ar, ai
    sh = 1
    while sh < t:
        if reverse:
            keep = row < t - sh
            sr, si = pltpu.roll(xr, t - sh, 0), pltpu.roll(xi, t - sh, 0)
        else:
            keep = row >= sh
            sr, si = pltpu.roll(xr, sh, 0), pltpu.roll(xi, sh, 0)
        sr = jnp.where(keep, sr, 0.0)
        si = jnp.where(keep, si, 0.0)
        xr, xi = xr + pr * sr - pi * si, xi + pr * si + pi * sr
        pr, pi = pr * pr - pi * pi, 2.0 * pr * pi
        sh *= 2
    return xr, xi


def _s5_states(u_bf, bbr, bbi, ar, ai, cr, ci):
    t = u_bf.shape[0]
    row = lax.broadcasted_iota(jnp.int32, (t, 1), 0)
    xr = _dot(u_bf, bbr)
    xi = _dot(u_bf, bbi)
    first = row == 0
    xr = xr + jnp.where(first, ar * cr - ai * ci, 0.0)
    xi = xi + jnp.where(first, ar * ci + ai * cr, 0.0)
    return _scan_rows(xr, xi, ar, ai, reverse=False)


def _s5_fwd(proj, bbr, bbi, a_re, a_im, ctr, cti, d_skip, d_s5):
    l = proj.shape[0]
    nb, uc, ns = bbr.shape
    t = min(S5_T, l)
    nt = l // t

    def body(u_ref, bbr_ref, bbi_ref, ar_ref, ai_ref, ctr_ref, cti_ref, d_ref,
             y_ref, car_r_ref, car_i_ref, cr, ci):
        @pl.when(pl.program_id(1) == 0)
        def _():
            cr[...] = jnp.zeros_like(cr)
            ci[...] = jnp.zeros_like(ci)

        car_r_ref[...] = cr[...]
        car_i_ref[...] = ci[...]
        u = u_ref[...]
        sr, si = _s5_states(u.astype(BF16), bbr_ref[...], bbi_ref[...], ar_ref[...], ai_ref[...],
                            cr[...], ci[...])
        cr[...] = sr[t - 1:t, :]
        ci[...] = si[t - 1:t, :]
        y_ref[...] = (_bdot(sr, ctr_ref[...]) - _bdot(si, cti_ref[...]) + d_ref[...] * u)

    per_block = lambda shape: pl.BlockSpec((None,) + shape, lambda b, n: (b, 0, 0))
    return pl.pallas_call(
        body, name="s5_fwd",
        out_shape=(jax.ShapeDtypeStruct((l, d_s5), F32),
                   jax.ShapeDtypeStruct((nt, 1, nb * ns), F32),
                   jax.ShapeDtypeStruct((nt, 1, nb * ns), F32)),
        grid=(nb, nt),
        in_specs=[pl.BlockSpec((t, uc), lambda b, n: (n, b)),
                  per_block((uc, ns)), per_block((uc, ns)),
                  per_block((1, ns)), per_block((1, ns)),
                  per_block((ns, uc)), per_block((ns, uc)),
                  pl.BlockSpec((1, uc), lambda b, n: (0, b))],
        out_specs=(pl.BlockSpec((t, uc), lambda b, n: (n, b)),
                   pl.BlockSpec((None, 1, ns), lambda b, n: (n, 0, b)),
                   pl.BlockSpec((None, 1, ns), lambda b, n: (n, 0, b))),
        scratch_shapes=[pltpu.VMEM((1, ns), F32), pltpu.VMEM((1, ns), F32)],
        compiler_params=_cparams("parallel", "arbitrary"),
    )(proj, bbr, bbi, a_re, a_im, ctr, cti, d_skip)


def _s5_bwd(proj, dy, bbr, bbi, a_re, a_im, cbr, cbi, d_skip, car_r, car_i):
    l, d_s5 = dy.shape
    nb, uc, ns = bbr.shape
    t = min(S5_T, l)
    nt = l // t

    def body(u_ref, dy_ref, bbr_ref, bbi_ref, ar_ref, ai_ref, cbr_ref, cbi_ref, d_ref,
             car_r_ref, car_i_ref,
             du_ref, dar_ref, dai_ref, dbbr_ref, dbbi_ref, dcbr_ref, dcbi_ref, dd_ref, gcr, gci):
        @pl.when(pl.program_id(1) == 0)
        def _():
            gcr[...] = jnp.zeros_like(gcr)
            gci[...] = jnp.zeros_like(gci)
            for ref in (dar_ref, dai_ref, dbbr_ref, dbbi_ref, dcbr_ref, dcbi_ref, dd_ref):
                ref[...] = jnp.zeros_like(ref)

        row = lax.broadcasted_iota(jnp.int32, (t, 1), 0)
        u, dy = u_ref[...], dy_ref[...]
        u_bf, dy_bf = u.astype(BF16), dy.astype(BF16)
        ar, ai = ar_ref[...], ai_ref[...]
        cr, ci = car_r_ref[...], car_i_ref[...]
        sr, si = _s5_states(u_bf, bbr_ref[...], bbi_ref[...], ar, ai, cr, ci)
        first = row == 0
        pr = jnp.where(first, cr, pltpu.roll(sr, 1, 0))
        pi = jnp.where(first, ci, pltpu.roll(si, 1, 0))
        last = row == t - 1
        gr = _dot(dy_bf, cbr_ref[...]) + jnp.where(last, ar * gcr[...] + ai * gci[...], 0.0)
        gi = -_dot(dy_bf, cbi_ref[...]) + jnp.where(last, ar * gci[...] - ai * gcr[...], 0.0)
        gr, gi = _scan_rows(gr, gi, ar, -ai, reverse=True)
        gcr[...] = gr[0:1, :]
        gci[...] = gi[0:1, :]
        dar_ref[...] += jnp.sum(gr * pr + gi * pi, axis=0, keepdims=True)
        dai_ref[...] += jnp.sum(gi * pr - gr * pi, axis=0, keepdims=True)
        gr_bf, gi_bf = gr.astype(BF16), gi.astype(BF16)
        tn = ((0,), (0,))
        dbbr_ref[...] += _dot(u_bf, gr_bf, tn)
        dbbi_ref[...] += _dot(u_bf, gi_bf, tn)
        dcbr_ref[...] += _dot(dy_bf, sr.astype(BF16), tn)
        dcbi_ref[...] -= _dot(dy_bf, si.astype(BF16), tn)
        nt_dims = ((1,), (1,))
        du = _dot(gr_bf, bbr_ref[...], nt_dims) + _dot(gi_bf, bbi_ref[...], nt_dims) + dy * d_ref[...]
        du_ref[...] = du.astype(BF16)
        dd_ref[...] += jnp.sum(dy * u, axis=0, keepdims=True)

    rev = lambda n: nt - 1 - n
    per_block = lambda shape: pl.BlockSpec((None,) + shape, lambda b, n: (b, 0, 0))
    acc = jax.ShapeDtypeStruct((nb, uc, ns), F32)
    vec = jax.ShapeDtypeStruct((nb, 1, ns), F32)
    return pl.pallas_call(
        body, name="s5_bwd",
        out_shape=(jax.ShapeDtypeStruct((l, d_s5), BF16), vec, vec, acc, acc, acc, acc,
                   jax.ShapeDtypeStruct((1, d_s5), F32)),
        grid=(nb, nt),
        in_specs=[pl.BlockSpec((t, uc), lambda b, n: (rev(n), b)),
                  pl.BlockSpec((t, uc), lambda b, n: (rev(n), b)),
                  per_block((uc, ns)), per_block((uc, ns)),
                  per_block((1, ns)), per_block((1, ns)),
                  per_block((uc, ns)), per_block((uc, ns)),
                  pl.BlockSpec((1, uc), lambda b, n: (0, b)),
                  pl.BlockSpec((None, 1, ns), lambda b, n: (rev(n), 0, b)),
                  pl.BlockSpec((None, 1, ns), lambda b, n: (rev(n), 0, b))],
        out_specs=(pl.BlockSpec((t, uc), lambda b, n: (rev(n), b)),
                   per_block((1, ns)), per_block((1, ns)),
                   per_block((uc, ns)), per_block((uc, ns)),
                   per_block((uc, ns)), per_block((uc, ns)),
                   pl.BlockSpec((1, uc), lambda b, n: (0, b))),
        scratch_shapes=[pltpu.VMEM((1, ns), F32), pltpu.VMEM((1, ns), F32)],
        compiler_params=_cparams("parallel", "arbitrary"),
    )(proj, dy, bbr, bbi, a_re, a_im, cbr, cbi, d_skip, car_r, car_i)


def _s5_glu_fwd(y1, proj, off_z, wglu):
    l, d = y1.shape

    def body(y_ref, z_ref, w_ref, o_ref):
        y2 = _gelu(y_ref[...])
        y3 = y2 * _sigmoid(_bdot(y2, w_ref[...]))
        o_ref[...] = (y3 * _silu(z_ref[...])).astype(BF16)

    return pl.pallas_call(
        body, name="s5_glu_fwd",
        out_shape=jax.ShapeDtypeStruct((l, d), BF16),
        grid=(l // ROW_TILE,),
        in_specs=[pl.BlockSpec((ROW_TILE, d), lambda i: (i, 0)),
                  pl.BlockSpec((ROW_TILE, d), lambda i: (i, off_z // d)),
                  pl.BlockSpec((d, d), lambda i: (0, 0))],
        out_specs=pl.BlockSpec((ROW_TILE, d), lambda i: (i, 0)),
        compiler_params=_cparams("parallel"),
    )(y1, proj, wglu)


def _s5_glu_bwd(y1, proj, off_z, wglu, dout):
    l, d = y1.shape

    def body(y_ref, z_ref, w_ref, do_ref, dy_ref, dz_ref, dw_ref):
        y2, gelu_vjp = jax.vjp(_gelu, y_ref[...])
        z = z_ref[...]
        sz, silu_vjp = jax.vjp(_silu, z)
        y2_bf = y2.astype(BF16)
        sg = _sigmoid(_dot(y2_bf, w_ref[...]))
        dout = do_ref[...]
        dy3 = dout * sz
        dz_ref[...] = silu_vjp(dout * (y2 * sg))[0].astype(BF16)
        dgl = (dy3 * y2 * sg * (1.0 - sg)).astype(BF16)
        dy2 = dy3 * sg + _dot(dgl, w_ref[...], ((1,), (1,)))
        dy_ref[...] = gelu_vjp(dy2)[0]

        @pl.when(pl.program_id(0) == 0)
        def _():
            dw_ref[...] = jnp.zeros_like(dw_ref)

        dw_ref[...] += _dot(y2_bf, dgl, ((0,), (0,)))

    row = pl.BlockSpec((ROW_TILE, d), lambda i: (i, 0))
    full = pl.BlockSpec((d, d), lambda i: (0, 0))
    return pl.pallas_call(
        body, name="s5_glu_bwd",
        out_shape=(jax.ShapeDtypeStruct((l, d), F32), jax.ShapeDtypeStruct((l, d), BF16),
                   jax.ShapeDtypeStruct((d, d), F32)),
        grid=(l // ROW_TILE,),
        in_specs=[row, pl.BlockSpec((ROW_TILE, d), lambda i: (i, off_z // d)), full, row],
        out_specs=(row, row, full),
        compiler_params=_cparams("arbitrary"),
    )(y1, proj, wglu, dout)


def _shift_rows(x, k, back=False):
    if k == 0:
        return x
    t = x.shape[0]
    row = lax.broadcasted_iota(jnp.int32, (t, 1), 0)
    if back:
        return jnp.where(row < t - k, pltpu.roll(x, t - k, 0), 0.0)
    return jnp.where(row >= k, pltpu.roll(x, k, 0), 0.0)


def _dn_conv(x, w_ref):
    return sum(w_ref[CONV_K - 1 - k:CONV_K - k, :] * _shift_rows(x, k) for k in range(CONV_K))


def _dn_post_conv(c, j):
    y = _silu(c)
    n = y * lax.rsqrt(jnp.sum(y * y, axis=-1, keepdims=True) + EPS)
    n = n * jnp.where(j < DN_HEADS, DN_HEAD_DIM ** -0.5, 1.0)
    return jnp.where(j < 2 * DN_HEADS, n, y)


def _dn_prep_fwd(proj, off_qkv, conv_w):
    l = proj.shape[0]
    hd = DN_HEAD_DIM
    nblk = 3 * DN_HEADS

    def body(x_ref, w_ref, o_ref):
        o_ref[...] = _dn_post_conv(_dn_conv(x_ref[...], w_ref), pl.program_id(0))

    return pl.pallas_call(
        body, name="dn_prep_fwd",
        out_shape=jax.ShapeDtypeStruct((l, nblk * hd), F32),
        grid=(nblk,),
        in_specs=[pl.BlockSpec((l, hd), lambda j: (0, off_qkv // hd + j)),
                  pl.BlockSpec((CONV_K, hd), lambda j: (0, j))],
        out_specs=pl.BlockSpec((l, hd), lambda j: (0, j)),
        compiler_params=_cparams("parallel"),
    )(proj, conv_w)


def _dn_prep_bwd(proj, off_qkv, conv_w, dqkv):
    l = proj.shape[0]
    hd = DN_HEAD_DIM
    nblk = 3 * DN_HEADS

    def body(x_ref, w_ref, do_ref, dx_ref, dw_ref):
        x = x_ref[...]
        j = pl.program_id(0)
        _, vjp = jax.vjp(functools.partial(_dn_post_conv, j=j), _dn_conv(x, w_ref))
        dc = vjp(do_ref[...])[0]
        dx = sum(w_ref[CONV_K - 1 - k:CONV_K - k, :] * _shift_rows(dc, k, back=True)
                 for k in range(CONV_K))
        dx_ref[...] = dx.astype(BF16)
        for k in range(CONV_K):
            dw_ref[CONV_K - 1 - k:CONV_K - k, :] = jnp.sum(dc * _shift_rows(x, k), axis=0,
                                                           keepdims=True)

    return pl.pallas_call(
        body, name="dn_prep_bwd",
        out_shape=(jax.ShapeDtypeStruct((l, nblk * hd), BF16),
                   jax.ShapeDtypeStruct((CONV_K, nblk * hd), F32)),
        grid=(nblk,),
        in_specs=[pl.BlockSpec((l, hd), lambda j: (0, off_qkv // hd + j)),
                  pl.BlockSpec((CONV_K, hd), lambda j: (0, j)),
                  pl.BlockSpec((l, hd), lambda j: (0, j))],
        out_specs=(pl.BlockSpec((l, hd), lambda j: (0, j)),
                   pl.BlockSpec((CONV_K, hd), lambda j: (0, j))),
        compiler_params=_cparams("parallel"),
    )(proj, conv_w, dqkv)


def _dn_gate_fn(ba, a_log_row, dt_row):
    lane = lax.broadcasted_iota(jnp.int32, ba.shape, 1)
    beta = _sigmoid(ba)
    g = -jnp.exp(a_log_row) * _softplus(ba + dt_row)
    return jnp.where(lane < DN_HEADS, beta, jnp.where(lane < 2 * DN_HEADS, g, 0.0))


def _dn_gates_fwd(proj, off_ba, a_log_row, dt_row):
    l = proj.shape[0]
    row = pl.BlockSpec((ROW_TILE, 128), lambda i: (i, off_ba // 128))
    vec = pl.BlockSpec((1, 128), lambda i: (0, 0))

    def body(ba_ref, al_ref, dt_ref, o_ref):
        o_ref[...] = _dn_gate_fn(ba_ref[...], al_ref[...], dt_ref[...])

    return pl.pallas_call(
        body, name="dn_gates_fwd",
        out_shape=jax.ShapeDtypeStruct((l, 128), F32),
        grid=(l // ROW_TILE,),
        in_specs=[row, vec, vec],
        out_specs=pl.BlockSpec((ROW_TILE, 128), lambda i: (i, 0)),
        compiler_params=_cparams("parallel"),
    )(proj, a_log_row, dt_row)


def _dn_gates_bwd(proj, off_ba, a_log_row, dt_row, dgb_heads):
    l = proj.shape[0]
    nh = dgb_heads.shape[0]
    row = pl.BlockSpec((ROW_TILE, 128), lambda i: (i, off_ba // 128))
    vec = pl.BlockSpec((1, 128), lambda i: (0, 0))

    def body(ba_ref, al_ref, dt_ref, dg_ref, dba_ref, dal_ref, ddt_ref):
        _, vjp = jax.vjp(_dn_gate_fn, ba_ref[...], al_ref[...], dt_ref[...])
        dgb = dg_ref[0]
        for h in range(1, nh):
            dgb = dgb + dg_ref[h]
        dba, dal, ddt = vjp(dgb)
        dba_ref[...] = dba.astype(BF16)

        @pl.when(pl.program_id(0) == 0)
        def _():
            dal_ref[...] = jnp.zeros_like(dal_ref)
            ddt_ref[...] = jnp.zeros_like(ddt_ref)

        dal_ref[...] += dal
        ddt_ref[...] += ddt

    return pl.pallas_call(
        body, name="dn_gates_bwd",
        out_shape=(jax.ShapeDtypeStruct((l, 128), BF16), jax.ShapeDtypeStruct((1, 128), F32),
                   jax.ShapeDtypeStruct((1, 128), F32)),
        grid=(l // ROW_TILE,),
        in_specs=[row, vec, vec, pl.BlockSpec((nh, ROW_TILE, 128), lambda i: (0, i, 0))],
        out_specs=(pl.BlockSpec((ROW_TILE, 128), lambda i: (i, 0)), vec, vec),
        compiler_params=_cparams("arbitrary"),
    )(proj, a_log_row, dt_row, dgb_heads)


def _dn_chunk_fn(states, qs, ks, vs, gb, heads):
    c = qs[0].shape[0]
    each = lambda f, *lists: [f(*args) for args in zip(*lists)]
    lane = lax.broadcasted_iota(jnp.int32, gb.shape, 1)
    ri = lax.broadcasted_iota(jnp.int32, (c, c), 0)
    ci = lax.broadcasted_iota(jnp.int32, (c, c), 1)
    causal, strict = ri >= ci, ri > ci
    eye = (ri == ci).astype(F32)
    rowi = lax.broadcasted_iota(jnp.int32, (c, 1), 0)
    nt_dims = ((1,), (1,))
    hdot = functools.partial(_dot, precision=HIGHEST)

    pick = lambda m, at: jnp.sum(jnp.where(lane == at, m, 0.0), axis=1, keepdims=True)
    gb_cum = hdot(causal.astype(F32), gb)
    beta = [pick(gb, h) for h in heads]
    gc = [pick(gb_cum, h + DN_HEADS) for h in heads]
    gc_row = each(lambda g: jnp.sum(eye * g, axis=0, keepdims=True), gc)
    decay = each(lambda g, gr: jnp.where(causal, jnp.exp(jnp.where(causal, g - gr, 0.0)), 0.0),
                 gc, gc_row)
    kk = each(lambda k: _bdot(k, k, nt_dims), ks)
    a_mat = each(lambda b, m, dc: jnp.where(strict, b * m * dc, 0.0), beta, kk, decay)

    t_inv = each(lambda a: eye - a, a_mat)
    power = a_mat
    for _ in range(int(math.log2(c)) - 1):
        power = each(lambda p: hdot(p, p), power)
        t_inv = each(lambda t, p: t + hdot(t, p), t_inv, power)

    egc = each(jnp.exp, gc)
    u_c = each(lambda t, v, b: hdot(t, v * b), t_inv, vs, beta)
    w_c = each(lambda t, k, b, e: hdot(t, k * (b * e)), t_inv, ks, beta, egc)
    qk = each(lambda q, k, dc: _bdot(q, k, nt_dims) * dc, qs, ks, decay)
    g_end = each(lambda g: jnp.sum(jnp.where(rowi == c - 1, g, 0.0), axis=0, keepdims=True), gc)
    v_new = each(lambda u, w, s: u - _bdot(w, s), u_c, w_c, states)
    o = each(lambda q, e, s, m, vn: _bdot(q * e, s) + _bdot(m, vn), qs, egc, states, qk, v_new)
    new_states = each(
        lambda s, ge, k, g, vn: s * jnp.exp(ge) + _bdot(k * jnp.exp(ge - g), vn, ((0,), (0,))),
        states, g_end, ks, gc, v_new)
    return o, new_states


def _dn_chunk_specs(order):
    hd, nh, hps = DN_HEAD_DIM, DN_HEADS, DN_HEADS_PER_STEP
    qkv = lambda part: pl.BlockSpec((CHUNK, hps * hd), lambda h, n: (order(n), part * (nh // hps) + h))
    gb = pl.BlockSpec((CHUNK, 128), lambda h, n: (order(n), 0))
    state = pl.BlockSpec((hps, None, hd, hd), lambda h, n: (h, order(n), 0, 0))
    return qkv, gb, state


def _dn_chunk_fwd(qkv, gb):
    l = qkv.shape[0]
    hd, nh, hps = DN_HEAD_DIM, DN_HEADS, DN_HEADS_PER_STEP
    n_chunks = l // CHUNK
    qkv_spec, gb_spec, state_spec = _dn_chunk_specs(lambda n: n)

    def body(q_ref, k_ref, v_ref, gb_ref, o_ref, s_ref, state):
        @pl.when(pl.program_id(1) == 0)
        def _():
            state[...] = jnp.zeros_like(state)

        cols = [slice(i * hd, (i + 1) * hd) for i in range(hps)]
        states = [state[i] for i in range(hps)]
        for i in range(hps):
            s_ref[i] = states[i]
        o, new_states = _dn_chunk_fn(
            states, [q_ref[:, cs] for cs in cols], [k_ref[:, cs] for cs in cols],
            [v_ref[:, cs] for cs in cols], gb_ref[...],
            [pl.program_id(0) * hps + i for i in range(hps)])
        for i in range(hps):
            o_ref[:, cols[i]] = o[i]
            state[i] = new_states[i]

    return pl.pallas_call(
        body, name="dn_chunk_fwd",
        out_shape=(jax.ShapeDtypeStruct((l, nh * hd), F32),
                   jax.ShapeDtypeStruct((nh, n_chunks, hd, hd), F32)),
        grid=(nh // hps, n_chunks),
        in_specs=[qkv_spec(0), qkv_spec(1), qkv_spec(2), gb_spec],
        out_specs=(pl.BlockSpec((CHUNK, hps * hd), lambda h, n: (n, h)), state_spec),
        scratch_shapes=[pltpu.VMEM((hps, hd, hd), F32)],
        compiler_params=_cparams("parallel", "arbitrary"),
    )(qkv, qkv, qkv, gb)


def _dn_chunk_bwd(qkv, gb, states, do):
    l = qkv.shape[0]
    hd, nh, hps = DN_HEAD_DIM, DN_HEADS, DN_HEADS_PER_STEP
    n_chunks = l // CHUNK
    rev = lambda n: n_chunks - 1 - n
    qkv_spec, gb_spec, state_spec = _dn_chunk_specs(rev)

    def body(q_ref, k_ref, v_ref, gb_ref, s_ref, do_ref, dq_ref, dk_ref, dv_ref, dgb_ref, dstate):
        @pl.when(pl.program_id(1) == 0)
        def _():
            dstate[...] = jnp.zeros_like(dstate)

        cols = [slice(i * hd, (i + 1) * hd) for i in range(hps)]
        fn = functools.partial(_dn_chunk_fn, heads=[pl.program_id(0) * hps + i for i in range(hps)])
        _, vjp = jax.vjp(fn, [s_ref[i] for i in range(hps)], [q_ref[:, cs] for cs in cols],
                         [k_ref[:, cs] for cs in cols], [v_ref[:, cs] for cs in cols], gb_ref[...])
        ds, dq, dk, dv, dgb = vjp(([do_ref[:, cs] for cs in cols], [dstate[i] for i in range(hps)]))
        for i in range(hps):
            dstate[i] = ds[i]
            dq_ref[:, cols[i]] = dq[i]
            dk_ref[:, cols[i]] = dk[i]
            dv_ref[:, cols[i]] = dv[i]
        dgb_ref[...] = dgb

    head_out = pl.BlockSpec((CHUNK, hps * hd), lambda h, n: (rev(n), h))
    out = jax.ShapeDtypeStruct((l, nh * hd), F32)
    return pl.pallas_call(
        body, name="dn_chunk_bwd",
        out_shape=(out, out, out, jax.ShapeDtypeStruct((nh // hps, l, 128), F32)),
        grid=(nh // hps, n_chunks),
        in_specs=[qkv_spec(0), qkv_spec(1), qkv_spec(2), gb_spec, state_spec, head_out],
        out_specs=(head_out, head_out, head_out,
                   pl.BlockSpec((None, CHUNK, 128), lambda h, n: (h, rev(n), 0))),
        scratch_shapes=[pltpu.VMEM((hps, hd, hd), F32)],
        compiler_params=_cparams("parallel", "arbitrary"),
    )(qkv, qkv, qkv, gb, states, do)


def _dn_out_fn(o, z, w):
    return _rmsnorm(o, w) * _silu(z)


def _dn_out_fwd(o, proj, off_z, w):
    l, d = o.shape
    hd = DN_HEAD_DIM
    blk = lambda off: pl.BlockSpec((ROW_TILE, hd), lambda i, h: (i, off // hd + h))

    def body(o_ref, z_ref, w_ref, out_ref):
        out_ref[...] = _dn_out_fn(o_ref[...], z_ref[...], w_ref[...]).astype(BF16)

    return pl.pallas_call(
        body, name="dn_out_fwd",
        out_shape=jax.ShapeDtypeStruct((l, d), BF16),
        grid=(l // ROW_TILE, d // hd),
        in_specs=[blk(0), blk(off_z), pl.BlockSpec((1, hd), lambda i, h: (0, 0))],
        out_specs=blk(0),
        compiler_params=_cparams("parallel", "parallel"),
    )(o, proj, w)


def _dn_out_bwd(o, proj, off_z, w, dout):
    l, d = o.shape
    hd = DN_HEAD_DIM
    blk = lambda off: pl.BlockSpec((ROW_TILE, hd), lambda i, h: (i, off // hd + h))
    vec = pl.BlockSpec((1, hd), lambda i, h: (0, 0))

    def body(o_ref, z_ref, w_ref, dout_ref, do_ref, dz_ref, dw_ref):
        _, vjp = jax.vjp(_dn_out_fn, o_ref[...], z_ref[...], w_ref[...])
        do, dz, dw = vjp(dout_ref[...])
        do_ref[...] = do
        dz_ref[...] = dz.astype(BF16)

        @pl.when((pl.program_id(0) == 0) & (pl.program_id(1) == 0))
        def _():
            dw_ref[...] = jnp.zeros_like(dw_ref)

        dw_ref[...] += dw

    return pl.pallas_call(
        body, name="dn_out_bwd",
        out_shape=(jax.ShapeDtypeStruct((l, d), F32), jax.ShapeDtypeStruct((l, d), BF16),
                   jax.ShapeDtypeStruct((1, hd), F32)),
        grid=(l // ROW_TILE, d // hd),
        in_specs=[blk(0), blk(off_z), vec, blk(0)],
        out_specs=(blk(0), blk(0), vec),
        compiler_params=_cparams("arbitrary", "arbitrary"),
    )(o, proj, w, dout)


def _row_tile(rows, cols, budget_bytes=1 << 20):
    for tr in (rows, 4096, 2048, 1024, 512, 256, 128, 64, 32, 16):
        if tr <= rows and rows % tr == 0 and tr * cols * 4 <= budget_bytes:
            return tr
    raise ValueError((rows, cols))


def _adamw(w, m, v, gslots, name):
    rows, cols = w.shape
    ns = gslots.shape[0]
    tr = _row_tile(rows, cols)
    c1 = 1.0 / (1.0 - ADAM_B1 ** ADAM_STEP)
    c2 = 1.0 / (1.0 - ADAM_B2 ** ADAM_STEP)

    def body(w_ref, m_ref, v_ref, g_ref, go_ref, d_ref, mo_ref, vo_ref):
        g = g_ref[0].astype(F32)
        for s in range(1, ns):
            g = g + g_ref[s].astype(F32)
        m_new = ADAM_B1 * m_ref[...] + (1.0 - ADAM_B1) * g
        v_new = ADAM_B2 * v_ref[...] + (1.0 - ADAM_B2) * (g * g)
        go_ref[...] = g
        mo_ref[...] = m_new
        vo_ref[...] = v_new
        d_ref[...] = -ADAM_LR * ((m_new * c1) / (jnp.sqrt(v_new * c2) + ADAM_EPS) + ADAM_WD * w_ref[...])

    blk = pl.BlockSpec((tr, cols), lambda i: (i, 0))
    o = jax.ShapeDtypeStruct((rows, cols), F32)
    return pl.pallas_call(
        body, name=name, out_shape=(o, o, o, o),
        grid=(rows // tr,),
        in_specs=[blk, blk, blk, pl.BlockSpec((ns, tr, cols), lambda i: (0, i, 0))],
        out_specs=(blk, blk, blk, blk),
        compiler_params=_cparams("parallel"),
    )(w, m, v, gslots)


def _slot_sum(gslots, name):
    ns, rows, cols = gslots.shape
    tr = _row_tile(rows, cols)

    def body(g_ref, o_ref):
        g = g_ref[0]
        for s in range(1, ns):
            g = g + g_ref[s]
        o_ref[...] = g

    return pl.pallas_call(
        body, name=name, out_shape=jax.ShapeDtypeStruct((rows, cols), F32),
        grid=(rows // tr,),
        in_specs=[pl.BlockSpec((ns, tr, cols), lambda i: (0, i, 0))],
        out_specs=pl.BlockSpec((tr, cols), lambda i: (i, 0)),
        compiler_params=_cparams("parallel"),
    )(gslots)


HBM_SPEC = pl.BlockSpec(memory_space=pl.ANY)


def _all_gather(arrs, name):
    n = len(arrs)

    def body(*refs):
        ins, outs = refs[:n], refs[n:2 * n]
        send_sems, recv_sems, local_sems = refs[2 * n:]
        x, y, c = lax.axis_index("x"), lax.axis_index("y"), lax.axis_index("c")
        me, sibling = (x, y, c), (x, y, 1 - c)
        chips = [(1 - x, y), (x, 1 - y), (1 - x, 1 - y)]
        index = lambda px, py, pc: 4 * px + 2 * py + pc

        def copy(a, k, block, to, src=None):
            rows = outs[a].at[index(*block)]
            return pltpu.make_async_remote_copy(
                src_ref=rows if src is None else src, dst_ref=rows,
                send_sem=send_sems.at[a, k], recv_sem=recv_sems.at[a, k],
                device_id=to, device_id_type=MESH)

        mine = [pltpu.make_async_copy(ins[a], outs[a].at[index(*me)], local_sems.at[a])
                for a in range(n)]
        for cp in mine:
            cp.start()
        first = []
        for a in range(n):
            first.append(copy(a, 0, me, sibling, src=ins[a]))
            first += [copy(a, 1 + j, me, (*chip, c), src=ins[a]) for j, chip in enumerate(chips)]
        for cp in first:
            cp.start()
        passed = []
        for j, chip in enumerate(chips):
            for a in range(n):
                copy(a, 1 + j, (*chip, c), me).wait_recv()
                fwd = copy(a, 4 + j, (*chip, c), sibling)
                fwd.start()
                passed.append(fwd)
        for a in range(n):
            copy(a, 0, sibling, me).wait_recv()
            for j, chip in enumerate(chips):
                copy(a, 4 + j, (*chip, 1 - c), me).wait_recv()
        for cp in first + passed:
            cp.wait_send()
        for cp in mine:
            cp.wait()

    return pl.pallas_call(
        body, name=name,
        out_shape=[jax.ShapeDtypeStruct((N_DEV,) + a.shape, a.dtype) for a in arrs],
        in_specs=[HBM_SPEC] * n, out_specs=[HBM_SPEC] * n,
        scratch_shapes=[pltpu.SemaphoreType.DMA((n, 7)), pltpu.SemaphoreType.DMA((n, 7)),
                        pltpu.SemaphoreType.DMA((n,))],
    )(*arrs)


def _sibling_swap(arrs, name):
    n = len(arrs)

    def body(*refs):
        ins, outs = refs[:n], refs[n:2 * n]
        send_sems, recv_sems = refs[2 * n:]
        x, y, c = lax.axis_index("x"), lax.axis_index("y"), lax.axis_index("c")
        copies = [pltpu.make_async_remote_copy(
            src_ref=ins[a].at[1 - c], dst_ref=outs[a],
            send_sem=send_sems.at[a], recv_sem=recv_sems.at[a],
            device_id=(x, y, 1 - c), device_id_type=MESH) for a in range(n)]
        for cp in copies:
            cp.start()
        for cp in copies:
            cp.wait()

    return pl.pallas_call(
        body, name=name,
        out_shape=[jax.ShapeDtypeStruct(a.shape[1:], a.dtype) for a in arrs],
        in_specs=[HBM_SPEC] * n, out_specs=[HBM_SPEC] * n,
        scratch_shapes=[pltpu.SemaphoreType.DMA((n,)), pltpu.SemaphoreType.DMA((n,))],
    )(*arrs)


def _pair_sum(mine, theirs, core, name):
    _, rows, cols = mine.shape
    tr = _row_tile(rows, cols)

    def body(core_ref, a_ref, b_ref, o_ref):
        o_ref[...] = (a_ref[...].astype(F32) + b_ref[...].astype(F32)).astype(o_ref.dtype)

    return pl.pallas_call(
        body, name=name, out_shape=jax.ShapeDtypeStruct((rows, cols), mine.dtype),
        grid_spec=pltpu.PrefetchScalarGridSpec(
            num_scalar_prefetch=1, grid=(rows // tr,),
            in_specs=[pl.BlockSpec((None, tr, cols), lambda i, core_ref: (core_ref[0], i, 0)),
                      pl.BlockSpec((tr, cols), lambda i, core_ref: (i, 0))],
            out_specs=pl.BlockSpec((tr, cols), lambda i, core_ref: (i, 0))),
        compiler_params=_cparams("parallel"),
    )(core, mine, theirs)


def _chip_exchange(arrs, name):
    n = len(arrs)

    def body(*refs):
        ins, outs = refs[:n], refs[n:2 * n]
        send_sems, recv_sems, local_sems = refs[2 * n:]
        x, y, c = lax.axis_index("x"), lax.axis_index("y"), lax.axis_index("c")
        me = 2 * x + y
        flip = lambda v, bit: 1 - v if bit else v
        mine = [pltpu.make_async_copy(ins[a].at[me], outs[a].at[me], local_sems.at[a])
                for a in range(n)]
        for cp in mine:
            cp.start()
        copies = []
        for k in range(1, 4):
            px, py = flip(x, k & 2), flip(y, k & 1)
            for a in range(n):
                copies.append(pltpu.make_async_remote_copy(
                    src_ref=ins[a].at[2 * px + py], dst_ref=outs[a].at[me],
                    send_sem=send_sems.at[a, k - 1], recv_sem=recv_sems.at[a, k - 1],
                    device_id=(px, py, c), device_id_type=MESH))
        for cp in copies:
            cp.start()
        for cp in copies:
            cp.wait()
        for cp in mine:
            cp.wait()

    return pl.pallas_call(
        body, name=name,
        out_shape=[jax.ShapeDtypeStruct(a.shape, a.dtype) for a in arrs],
        in_specs=[HBM_SPEC] * n, out_specs=[HBM_SPEC] * n,
        scratch_shapes=[pltpu.SemaphoreType.DMA((n, 3)), pltpu.SemaphoreType.DMA((n, 3)),
                        pltpu.SemaphoreType.DMA((n,))],
    )(*arrs)


def _block_diag(t):
    nb, gpb, r, c = t.shape
    eye = jnp.eye(gpb, dtype=t.dtype)
    return jnp.einsum("ngrc,gh->ngrhc", t, eye).reshape(nb, gpb * r, gpb * c)


def _diag_blocks(t, r, c):
    nb = t.shape[0]
    gpb = t.shape[1] // r
    t = t.reshape(nb, gpb, r, gpb, c)
    return jnp.einsum("ngrhc,gh->ngrc", t, jnp.eye(gpb, dtype=t.dtype))


def _pack_rows(parts):
    flat = jnp.concatenate([p.reshape(-1).astype(F32) for p in parts])
    pad = (-flat.shape[0]) % (256 * 128)
    return jnp.pad(flat, (0, pad)).reshape(-1, 128)


def _unpack_rows(packed, shapes):
    flat = packed.reshape(-1)
    out, at = [], 0
    for shape in shapes:
        size = math.prod(shape)
        out.append(flat[at:at + size].reshape(shape))
        at += size
    return out


def kernel(x, ln_w, w_in, s5_lam_re, s5_lam_im, s5_log_step, s5_b_re, s5_b_im, s5_c_re, s5_c_im, s5_d, s5_w_glu, s5_w_up, dn_conv_w, dn_a_log, dn_dt_bias, dn_norm_w, dn_w_up, w_out, final_norm_w, loss_target, m_ln_w, m_w_in, m_s5_lam_re, m_s5_lam_im, m_s5_log_step, m_s5_b_re, m_s5_b_im, m_s5_c_re, m_s5_c_im, m_s5_d, m_s5_w_glu, m_s5_w_up, m_dn_conv_w, m_dn_a_log, m_dn_dt_bias, m_dn_norm_w, m_dn_w_up, m_w_out, m_final_norm_w, v_ln_w, v_w_in, v_s5_lam_re, v_s5_lam_im, v_s5_log_step, v_s5_b_re, v_s5_b_im, v_s5_c_re, v_s5_c_im, v_s5_d, v_s5_w_glu, v_s5_w_up, v_dn_conv_w, v_dn_a_log, v_dn_dt_bias, v_dn_norm_w, v_dn_w_up, v_w_out, v_final_norm_w):
    weights = dict(ln_w=ln_w, w_in=w_in, s5_lam_re=s5_lam_re, s5_lam_im=s5_lam_im,
                   s5_log_step=s5_log_step, s5_b_re=s5_b_re, s5_b_im=s5_b_im, s5_c_re=s5_c_re,
                   s5_c_im=s5_c_im, s5_d=s5_d, s5_w_glu=s5_w_glu, s5_w_up=s5_w_up,
                   dn_conv_w=dn_conv_w, dn_a_log=dn_a_log, dn_dt_bias=dn_dt_bias,
                   dn_norm_w=dn_norm_w, dn_w_up=dn_w_up, w_out=w_out, final_norm_w=final_norm_w)
    mom_m = dict(ln_w=m_ln_w, w_in=m_w_in, s5_lam_re=m_s5_lam_re, s5_lam_im=m_s5_lam_im,
                 s5_log_step=m_s5_log_step, s5_b_re=m_s5_b_re, s5_b_im=m_s5_b_im,
                 s5_c_re=m_s5_c_re, s5_c_im=m_s5_c_im, s5_d=m_s5_d, s5_w_glu=m_s5_w_glu,
                 s5_w_up=m_s5_w_up, dn_conv_w=m_dn_conv_w, dn_a_log=m_dn_a_log,
                 dn_dt_bias=m_dn_dt_bias, dn_norm_w=m_dn_norm_w, dn_w_up=m_dn_w_up,
                 w_out=m_w_out, final_norm_w=m_final_norm_w)
    mom_v = dict(ln_w=v_ln_w, w_in=v_w_in, s5_lam_re=v_s5_lam_re, s5_lam_im=v_s5_lam_im,
                 s5_log_step=v_s5_log_step, s5_b_re=v_s5_b_re, s5_b_im=v_s5_b_im,
                 s5_c_re=v_s5_c_re, s5_c_im=v_s5_c_im, s5_d=v_s5_d, s5_w_glu=v_s5_w_glu,
                 s5_w_up=v_s5_w_up, dn_conv_w=v_dn_conv_w, dn_a_log=v_dn_a_log,
                 dn_dt_bias=v_dn_dt_bias, dn_norm_w=v_dn_norm_w, dn_w_up=v_dn_w_up,
                 w_out=v_w_out, final_norm_w=v_final_norm_w)
    names = list(weights)

    l, d = x.shape[1], x.shape[2]
    d_s5 = d // 2
    groups = d_s5 // S5_GROUP
    nb = groups // S5_GPB
    d_dn = DN_HEADS * DN_HEAD_DIM
    w_in_cols = w_in.shape[2]
    d_in = N_DEV * w_in_cols
    off_ba_src = 2 * d_s5 + 4 * d_dn
    off_u, off_zs, off_qkv, off_zd = 0, d_s5, 2 * d_s5, 2 * d_s5 + 3 * d_dn
    off_ba = off_zd + d_dn
    off_gs = off_ba + BA_PAD
    off_gd = off_gs + d
    n_proj = off_gd + d
    x2d, tgt2d = x[0], loss_target[0]
    my_index = 4 * lax.axis_index("x") + 2 * lax.axis_index("y") + lax.axis_index("c")

    g_win, g_glu, g_sup, g_dup, g_wout, g_conv = _all_gather(
        [w_in[0].astype(BF16), s5_w_glu[0].astype(BF16), s5_w_up[0].astype(BF16),
         dn_w_up[0].astype(BF16), w_out[0].astype(BF16), dn_conv_w[0]], name="gather_weights")
    w_full = jnp.transpose(g_win, (1, 0, 2)).reshape(d, d_in)
    w_cat = jnp.concatenate(
        [w_full[:, :off_ba_src],
         jnp.pad(w_full[:, off_ba_src:off_ba_src + 2 * DN_HEADS], ((0, 0), (0, BA_PAD - 2 * DN_HEADS))),
         w_full[:, off_ba_src + 2 * DN_HEADS:]], axis=1)
    wglu_full = g_glu.reshape(d_s5, d_s5)
    wsup_full = jnp.transpose(g_sup, (1, 0, 2)).reshape(d_s5, d)
    wdup_full = jnp.transpose(g_dup, (1, 0, 2)).reshape(d_dn, d)
    wout_full = g_wout.reshape(d, d)
    conv_full = jnp.transpose(g_conv, (1, 0, 2)).reshape(CONV_K, 3 * d_dn)

    lam_re, lam_im = s5_lam_re[0], s5_lam_im[0]
    log_step = s5_log_step[0].reshape(groups, 1)
    b_re = s5_b_re[0].reshape(groups * S5_STATE, S5_GROUP)
    b_im = s5_b_im[0].reshape(groups * S5_STATE, S5_GROUP)
    abar_re, abar_im, f_re, f_im = _s5_disc_fwd(lam_re, lam_im, log_step)
    f_re_col, f_im_col = f_re.reshape(-1, 1), f_im.reshape(-1, 1)
    bb_re, bb_im = _s5_bbar_fwd(f_re_col, f_im_col, b_re, b_im)

    def bb_blocks(t):
        t = t.reshape(nb, S5_GPB, S5_STATE, S5_GROUP).transpose(0, 1, 3, 2)
        return _block_diag(t).astype(BF16)

    def c_blocks(t):
        return _block_diag(t.reshape(nb, S5_GPB, S5_GROUP, S5_STATE)).astype(BF16)

    bbr, bbi = bb_blocks(bb_re), bb_blocks(bb_im)
    cbr, cbi = c_blocks(s5_c_re[0]), c_blocks(s5_c_im[0])
    ctr, cti = jnp.transpose(cbr, (0, 2, 1)), jnp.transpose(cbi, (0, 2, 1))
    a_re = abar_re.reshape(nb, 1, S5_GPB * S5_STATE)
    a_im = abar_im.reshape(nb, 1, S5_GPB * S5_STATE)

    h = _rms_fwd(x2d, ln_w)
    proj = _mm(h, w_cat, tn=1536, name="proj")
    y1, car_r, car_i = _s5_fwd(proj, bbr, bbi, a_re, a_im, ctr, cti, s5_d, d_s5)
    out_s = _s5_glu_fwd(y1, proj, off_zs, wglu_full)
    y_s = _mm(out_s, wsup_full, name="s5_up")

    a_log_row = jnp.pad(dn_a_log, ((0, 0), (DN_HEADS, 128 - 2 * DN_HEADS)))
    dt_row = jnp.pad(dn_dt_bias, ((0, 0), (DN_HEADS, 128 - 2 * DN_HEADS)))
    qkv = _dn_prep_fwd(proj, off_qkv, conv_full)
    gb = _dn_gates_fwd(proj, off_ba, a_log_row, dt_row)
    o_dn, states = _dn_chunk_fwd(qkv, gb)
    out_d = _dn_out_fwd(o_dn, proj, off_zd, dn_norm_w)
    y_d = _mm(out_d, wdup_full, name="dn_up")

    mixed = _merge_fwd(proj, off_gs, off_gd, y_s, y_d)
    branch = _mm(mixed, wout_full, name="w_out")
    dx2, dx2_bf, loss_dev, d_final_w = _final(x2d, branch, final_norm_w.reshape(1, d), tgt2d)

    g_wout_full = _mm(mixed, dx2_bf, ta=True, out_dtype=BF16, name="grad_w_out")
    dmixed = _mm(dx2_bf, wout_full, tb=True, name="d_mixed")
    dgs, dgd, dys, dyd = _merge_bwd(proj, off_gs, off_gd, y_s, y_d, dmixed)

    g_dup_full = _mm(out_d, dyd, ta=True, out_dtype=BF16, name="grad_dn_up")
    dout_d = _mm(dyd, wdup_full, tb=True, name="d_out_d")
    do_dn, dzd, d_norm_w = _dn_out_bwd(o_dn, proj, off_zd, dn_norm_w, dout_d)
    dq, dk, dv, dgb_heads = _dn_chunk_bwd(qkv, gb, states, do_dn)
    dba, d_a_log_row, d_dt_row = _dn_gates_bwd(proj, off_ba, a_log_row, dt_row, dgb_heads)
    dqkv_pre, d_conv_full = _dn_prep_bwd(proj, off_qkv, conv_full,
                                         jnp.concatenate([dq, dk, dv], axis=1))

    g_sup_full = _mm(out_s, dys, ta=True, out_dtype=BF16, name="grad_s5_up")
    dout_s = _mm(dys, wsup_full, tb=True, name="d_out_s")
    dy1, dzs, g_glu_full = _s5_glu_bwd(y1, proj, off_zs, wglu_full, dout_s)
    (du, d_a_re, d_a_im, d_bbr, d_bbi, d_cbr, d_cbi, d_s5_d) = _s5_bwd(
        proj, dy1, bbr, bbi, a_re, a_im, cbr, cbi, s5_d, car_r, car_i)

    def from_bb_blocks(t):
        t = _diag_blocks(t, S5_GROUP, S5_STATE).transpose(0, 1, 3, 2)
        return t.reshape(groups * S5_STATE, S5_GROUP)

    d_f_re, d_f_im, d_b_re, d_b_im = _s5_bbar_bwd(f_re_col, f_im_col, b_re, b_im,
                                                 from_bb_blocks(d_bbr), from_bb_blocks(d_bbi))
    d_lam_re, d_lam_im, d_log_step = _s5_disc_bwd(
        lam_re, lam_im, log_step, d_a_re.reshape(groups, S5_STATE), d_a_im.reshape(groups, S5_STATE),
        d_f_re.reshape(groups, S5_STATE), d_f_im.reshape(groups, S5_STATE))
    d_c_re = _diag_blocks(d_cbr, S5_GROUP, S5_STATE).reshape(groups, S5_GROUP, S5_STATE)
    d_c_im = _diag_blocks(d_cbi, S5_GROUP, S5_STATE).reshape(groups, S5_GROUP, S5_STATE)

    dproj = jnp.concatenate(
        [du, dzs, dqkv_pre, dzd, jnp.pad(dba, ((0, 0), (0, BA_PAD - 128))), dgs, dgd], axis=1)
    g_wcat = _mm(h, dproj, ta=True, out_dtype=BF16, tn=1536, name="grad_w_in")
    dh = _mm(dproj, w_cat, tb=True, tm=1024, tn=1024, tk=1536, name="d_h")
    grad_x, d_ln_w = _rms_bwd(x2d, ln_w, dh, dx2)

    g_win_full = jnp.concatenate(
        [g_wcat[:, :off_ba], g_wcat[:, off_ba:off_ba + 2 * DN_HEADS], g_wcat[:, off_gs:]], axis=1)
    def by_dest(t, axis):
        if axis == 1:
            return t.reshape(t.shape[0], 4, 2, t.shape[1] // N_DEV).transpose(2, 1, 0, 3)
        return t.reshape(4, 2, t.shape[0] // N_DEV, t.shape[1]).transpose(1, 0, 2, 3)

    big = ["w_in", "s5_w_glu", "s5_w_up", "dn_w_up", "w_out"]
    parts = [by_dest(g_win_full, 1), by_dest(g_glu_full.astype(BF16), 0), by_dest(g_sup_full, 1),
             by_dest(g_dup_full, 1), by_dest(g_wout_full, 0)]
    from_sibling = _sibling_swap(parts, name="swap_grads")
    core = lax.axis_index("c").astype(jnp.int32).reshape(1)
    chip_sums = [
        _pair_sum(p.reshape(2, -1, p.shape[-1]), got.reshape(-1, got.shape[-1]), core,
                  name="pair_sum_" + nm).reshape(got.shape)
        for nm, p, got in zip(big, parts, from_sibling)]
    slots = _chip_exchange(chip_sums, name="exchange_grads")
    results = {}
    for nm, sl in zip(big, slots):
        results[nm] = _adamw(weights[nm][0], mom_m[nm][0], mom_v[nm][0], sl, name="adamw_" + nm)

    small = [nm for nm in names if nm not in big]
    small_grads = dict(
        ln_w=d_ln_w, s5_lam_re=d_lam_re, s5_lam_im=d_lam_im, s5_log_step=d_log_step,
        s5_b_re=d_b_re, s5_b_im=d_b_im, s5_c_re=d_c_re, s5_c_im=d_c_im, s5_d=d_s5_d,
        dn_conv_w=d_conv_full, dn_a_log=d_a_log_row[:, DN_HEADS:2 * DN_HEADS],
        dn_dt_bias=d_dt_row[:, DN_HEADS:2 * DN_HEADS], dn_norm_w=d_norm_w, final_norm_w=d_final_w)
    (all_small,) = _all_gather([_pack_rows([small_grads[nm] for nm in small])], name="gather_small_grads")
    summed = _slot_sum(all_small, name="sum_small_grads")
    full_shapes = [(CONV_K, 3 * d_dn) if nm == "dn_conv_w" else weights[nm].shape for nm in small]
    g_small = dict(zip(small, _unpack_rows(summed, full_shapes)))
    conv_cols = dn_conv_w.shape[2]
    g_small["dn_conv_w"] = lax.dynamic_slice_in_dim(
        g_small["dn_conv_w"], my_index * conv_cols, conv_cols, axis=1).reshape(dn_conv_w.shape)
    packed = [_pack_rows([t[nm] for nm in small]) for t in (weights, mom_m, mom_v, g_small)]
    small_out = _adamw(packed[0], packed[1], packed[2], packed[3][None], name="adamw_small")
    small_shapes = [weights[nm].shape for nm in small]
    for kind, packed_out in enumerate(small_out):
        for nm, val in zip(small, _unpack_rows(packed_out, small_shapes)):
            results.setdefault(nm, [None] * 4)[kind] = val

    loss = lax.psum(loss_dev[0, 0], ("x", "y", "c"))
    outs = [loss, grad_x[None]]
    for kind in range(4):
        outs += [results[nm][kind].reshape(weights[nm].shape) for nm in names]
    return tuple(outs)
```

```python
import functools
import math

import jax
import jax.numpy as jnp
from jax import lax
from jax.experimental import pallas as pl
from jax.experimental.pallas import tpu as pltpu

F32 = jnp.float32
BF16 = jnp.bfloat16
HIGHEST = lax.Precision.HIGHEST
MESH = pl.DeviceIdType.MESH
N_DEV = 8

EPS = 1e-6
S5_GROUP = 16
S5_STATE = 64
S5_GPB = 8
S5_T = 256
DN_HEADS = 8
DN_HEAD_DIM = 128
CHUNK = 64
DN_HEADS_PER_STEP = 8
CONV_K = 4
BA_PAD = 512

ADAM_LR = 0.001
ADAM_B1 = 0.9
ADAM_B2 = 0.999
ADAM_EPS = 1e-08
ADAM_WD = 0.01
ADAM_STEP = 10

VMEM_LIMIT_BYTES = 48 * 1024 * 1024
ROW_TILE = 256


def _cparams(*sem):
    return pltpu.CompilerParams(dimension_semantics=sem if sem else None,
                                vmem_limit_bytes=VMEM_LIMIT_BYTES)


def _sigmoid(x):
    return 1.0 / (1.0 + jnp.exp(-x))


def _silu(x):
    return x * _sigmoid(x)


def _gelu(x):
    return 0.5 * x * (1.0 + jnp.tanh(0.7978845608028654 * (x + 0.044715 * x * x * x)))


def _softplus(x):
    return jnp.maximum(x, 0.0) + jnp.log(1.0 + jnp.exp(-jnp.abs(x)))


def _rmsnorm(x, w):
    return x * lax.rsqrt(jnp.mean(x * x, axis=-1, keepdims=True) + EPS) * w


def _dot(a, b, dims=((1,), (0,)), precision=None):
    return lax.dot_general(a, b, (dims, ((), ())), precision=precision,
                           preferred_element_type=F32)


def _bdot(a, b, dims=((1,), (0,))):
    return _dot(a.astype(BF16), b.astype(BF16), dims)


def _mm(a, b, *, ta=False, tb=False, out_dtype=F32, tm=512, tn=512, tk=None, name):
    k_dim, m_dim = (a.shape if ta else a.shape[::-1])
    n_dim = b.shape[0] if tb else b.shape[1]
    assert (b.shape[1] if tb else b.shape[0]) == k_dim
    tm, tn = min(tm, m_dim), min(tn, n_dim)
    tk = k_dim if tk is None else tk
    assert m_dim % tm == 0 and n_dim % tn == 0 and k_dim % tk == 0
    nk = k_dim // tk
    a_spec = (pl.BlockSpec((tk, tm), lambda i, j, k: (k, i)) if ta
              else pl.BlockSpec((tm, tk), lambda i, j, k: (i, k)))
    b_spec = (pl.BlockSpec((tn, tk), lambda i, j, k: (j, k)) if tb
              else pl.BlockSpec((tk, tn), lambda i, j, k: (k, j)))
    dims = ((0 if ta else 1,), (1 if tb else 0,))

    def body(a_ref, b_ref, o_ref, *scratch):
        p = _bdot(a_ref[...], b_ref[...], dims)
        if nk == 1:
            o_ref[...] = p.astype(o_ref.dtype)
        else:
            acc = scratch[0]
            k = pl.program_id(2)

            @pl.when(k == 0)
            def _():
                acc[...] = p

            @pl.when(k > 0)
            def _():
                acc[...] += p

            @pl.when(k == nk - 1)
            def _():
                o_ref[...] = acc[...].astype(o_ref.dtype)

    return pl.pallas_call(
        body, name=name,
        out_shape=jax.ShapeDtypeStruct((m_dim, n_dim), out_dtype),
        grid=(m_dim // tm, n_dim // tn, nk),
        in_specs=[a_spec, b_spec],
        out_specs=pl.BlockSpec((tm, tn), lambda i, j, k: (i, j)),
        scratch_shapes=[pltpu.VMEM((tm, tn), F32)] if nk > 1 else [],
        compiler_params=_cparams("parallel", "parallel", "arbitrary"),
    )(a, b)


def _rms_fwd(x, w):
    l, d = x.shape

    def body(x_ref, w_ref, h_ref):
        h_ref[...] = _rmsnorm(x_ref[...], w_ref[...]).astype(BF16)

    return pl.pallas_call(
        body, name="rms_fwd",
        out_shape=jax.ShapeDtypeStruct((l, d), BF16),
        grid=(l // ROW_TILE,),
        in_specs=[pl.BlockSpec((ROW_TILE, d), lambda i: (i, 0)),
                  pl.BlockSpec((1, d), lambda i: (0, 0))],
        out_specs=pl.BlockSpec((ROW_TILE, d), lambda i: (i, 0)),
        compiler_params=_cparams("parallel"),
    )(x, w)


def _rms_bwd(x, w, dh, dres):
    l, d = x.shape

    def body(x_ref, w_ref, dh_ref, dres_ref, dx_ref, dw_ref):
        _, vjp = jax.vjp(_rmsnorm, x_ref[...], w_ref[...])
        dx, dw = vjp(dh_ref[...])
        dx_ref[...] = dx + dres_ref[...]

        @pl.when(pl.program_id(0) == 0)
        def _():
            dw_ref[...] = jnp.zeros_like(dw_ref)

        dw_ref[...] += dw

    row = pl.BlockSpec((ROW_TILE, d), lambda i: (i, 0))
    vec = pl.BlockSpec((1, d), lambda i: (0, 0))
    return pl.pallas_call(
        body, name="rms_bwd",
        out_shape=(jax.ShapeDtypeStruct((l, d), F32), jax.ShapeDtypeStruct((1, d), F32)),
        grid=(l // ROW_TILE,),
        in_specs=[row, vec, row, row],
        out_specs=(row, vec),
        compiler_params=_cparams("arbitrary"),
    )(x, w, dh, dres)


def _final(x, r, fw, target):
    l, d = x.shape

    def per_row_loss(x2, w, tgt):
        err = _rmsnorm(x2, w) - tgt
        return 0.5 * jnp.mean(err * err, axis=-1, keepdims=True)

    def body(x_ref, r_ref, w_ref, t_ref, dx_ref, dxb_ref, loss_ref, dw_ref):
        x2 = x_ref[...] + r_ref[...]
        rows, vjp = jax.vjp(functools.partial(per_row_loss, tgt=t_ref[...]), x2, w_ref[...])
        dx2, dw = vjp(jnp.ones_like(rows))
        dx_ref[...] = dx2
        dxb_ref[...] = dx2.astype(BF16)

        @pl.when(pl.program_id(0) == 0)
        def _():
            dw_ref[...] = jnp.zeros_like(dw_ref)
            loss_ref[...] = jnp.zeros_like(loss_ref)

        dw_ref[...] += dw
        loss_ref[...] += jnp.sum(rows, axis=0, keepdims=True)

    row = pl.BlockSpec((ROW_TILE, d), lambda i: (i, 0))
    vec = pl.BlockSpec((1, d), lambda i: (0, 0))
    return pl.pallas_call(
        body, name="final_norm_loss",
        out_shape=(jax.ShapeDtypeStruct((l, d), F32), jax.ShapeDtypeStruct((l, d), BF16),
                   jax.ShapeDtypeStruct((1, 1), F32), jax.ShapeDtypeStruct((1, d), F32)),
        grid=(l // ROW_TILE,),
        in_specs=[row, row, vec, row],
        out_specs=(row, row, pl.BlockSpec((1, 1), lambda i: (0, 0)), vec),
        compiler_params=_cparams("arbitrary"),
    )(x, r, fw, target)


def _merge_fn(gs, gd, ys, yd):
    return _sigmoid(gs) * ys + _sigmoid(gd) * yd


def _merge_fwd(proj, off_gs, off_gd, ys, yd):
    l, d = ys.shape
    cw = 512
    blk = lambda off: pl.BlockSpec((ROW_TILE, cw), lambda i, j: (i, off // cw + j))

    def body(gs_ref, gd_ref, ys_ref, yd_ref, o_ref):
        o_ref[...] = _merge_fn(gs_ref[...], gd_ref[...], ys_ref[...], yd_ref[...]).astype(BF16)

    return pl.pallas_call(
        body, name="merge_fwd",
        out_shape=jax.ShapeDtypeStruct((l, d), BF16),
        grid=(l // ROW_TILE, d // cw),
        in_specs=[blk(off_gs), blk(off_gd), blk(0), blk(0)],
        out_specs=blk(0),
        compiler_params=_cparams("parallel", "parallel"),
    )(proj, proj, ys, yd)


def _merge_bwd(proj, off_gs, off_gd, ys, yd, dmixed):
    l, d = ys.shape
    cw = 512
    blk = lambda off: pl.BlockSpec((ROW_TILE, cw), lambda i, j: (i, off // cw + j))

    def body(gs_ref, gd_ref, ys_ref, yd_ref, dm_ref, dgs_ref, dgd_ref, dys_ref, dyd_ref):
        _, vjp = jax.vjp(_merge_fn, gs_ref[...], gd_ref[...], ys_ref[...], yd_ref[...])
        dgs, dgd, dys, dyd = vjp(dm_ref[...])
        dgs_ref[...] = dgs.astype(BF16)
        dgd_ref[...] = dgd.astype(BF16)
        dys_ref[...] = dys.astype(BF16)
        dyd_ref[...] = dyd.astype(BF16)

    out = jax.ShapeDtypeStruct((l, d), BF16)
    return pl.pallas_call(
        body, name="merge_bwd",
        out_shape=(out, out, out, out),
        grid=(l // ROW_TILE, d // cw),
        in_specs=[blk(off_gs), blk(off_gd), blk(0), blk(0), blk(0)],
        out_specs=(blk(0), blk(0), blk(0), blk(0)),
        compiler_params=_cparams("parallel", "parallel"),
    )(proj, proj, ys, yd, dmixed)


def _s5_disc_fn(lam_re, lam_im, log_step):
    step = jnp.exp(log_step)
    mag = jnp.exp(lam_re * step)
    abar_re = mag * jnp.cos(lam_im * step)
    abar_im = mag * jnp.sin(lam_im * step)
    den = lam_re * lam_re + lam_im * lam_im
    xr = abar_re - 1.0
    f_re = (xr * lam_re + abar_im * lam_im) / den
    f_im = (abar_im * lam_re - xr * lam_im) / den
    return abar_re, abar_im, f_re, f_im


def _s5_disc_fwd(lam_re, lam_im, log_step):
    g, p = lam_re.shape

    def body(lr_ref, li_ref, ls_ref, ar_ref, ai_ref, fr_ref, fi_ref):
        ar, ai, fr, fi = _s5_disc_fn(lr_ref[...], li_ref[...], ls_ref[...])
        ar_ref[...] = ar
        ai_ref[...] = ai
        fr_ref[...] = fr
        fi_ref[...] = fi

    o = jax.ShapeDtypeStruct((g, p), F32)
    return pl.pallas_call(body, name="s5_disc_fwd", out_shape=(o, o, o, o),
                          compiler_params=_cparams())(lam_re, lam_im, log_step)


def _s5_disc_bwd(lam_re, lam_im, log_step, dar, dai, dfr, dfi):
    g, p = lam_re.shape

    def body(lr_ref, li_ref, ls_ref, dar_ref, dai_ref, dfr_ref, dfi_ref, dlr_ref, dli_ref, dls_ref):
        _, vjp = jax.vjp(_s5_disc_fn, lr_ref[...], li_ref[...], ls_ref[...])
        dlr, dli, dls = vjp((dar_ref[...], dai_ref[...], dfr_ref[...], dfi_ref[...]))
        dlr_ref[...] = dlr
        dli_ref[...] = dli
        dls_ref[...] = dls

    o = jax.ShapeDtypeStruct((g, p), F32)
    return pl.pallas_call(body, name="s5_disc_bwd",
                          out_shape=(o, o, jax.ShapeDtypeStruct((g, 1), F32)),
                          compiler_params=_cparams())(lam_re, lam_im, log_step, dar, dai, dfr, dfi)


def _s5_bbar_fwd(f_re, f_im, b_re, b_im):
    n, c = b_re.shape

    def body(fr_ref, fi_ref, br_ref, bi_ref, or_ref, oi_ref):
        fr, fi, br, bi = fr_ref[...], fi_ref[...], br_ref[...], bi_ref[...]
        or_ref[...] = fr * br - fi * bi
        oi_ref[...] = fr * bi + fi * br

    o = jax.ShapeDtypeStruct((n, c), F32)
    return pl.pallas_call(body, name="s5_bbar_fwd", out_shape=(o, o),
                          compiler_params=_cparams())(f_re, f_im, b_re, b_im)


def _s5_bbar_bwd(f_re, f_im, b_re, b_im, dbr, dbi):
    n, c = b_re.shape

    def body(fr_ref, fi_ref, br_ref, bi_ref, dor_ref, doi_ref, dfr_ref, dfi_ref, dbr_ref, dbi_ref):
        fr, fi, br, bi = fr_ref[...], fi_ref[...], br_ref[...], bi_ref[...]
        dor, doi = dor_ref[...], doi_ref[...]
        dfr_ref[...] = jnp.sum(dor * br + doi * bi, axis=-1, keepdims=True)
        dfi_ref[...] = jnp.sum(doi * br - dor * bi, axis=-1, keepdims=True)
        dbr_ref[...] = fr * dor + fi * doi
        dbi_ref[...] = fr * doi - fi * dor

    col = jax.ShapeDtypeStruct((n, 1), F32)
    o = jax.ShapeDtypeStruct((n, c), F32)
    return pl.pallas_call(body, name="s5_bbar_bwd", out_shape=(col, col, o, o),
                          compiler_params=_cparams())(f_re, f_im, b_re, b_im, dbr, dbi)


def _scan_rows(xr, xi, ar, ai, reverse):
    t = xr.shape[0]
    row = lax.broadcasted_iota(jnp.int32, (t, 1), 0)
    pr, pi = ar, ai
    sh = 1
    while sh < t:
        if reverse:
            keep = row < t - sh
            sr, si = pltpu.roll(xr, t - sh, 0), pltpu.roll(xi, t - sh, 0)
        else:
            keep = row >= sh
            sr, si = pltpu.roll(xr, sh, 0), pltpu.roll(xi, sh, 0)
        sr = jnp.where(keep, sr, 0.0)
        si = jnp.where(keep, si, 0.0)
        xr, xi = xr + pr * sr - pi * si, xi + pr * si + pi * sr
        pr, pi = pr * pr - pi * pi, 2.0 * pr * pi
        sh *= 2
    return xr, xi


def _s5_states(u_bf, bbr, bbi, ar, ai, cr, ci):
    t = u_bf.shape[0]
    row = lax.broadcasted_iota(jnp.int32, (t, 1), 0)
    xr = _dot(u_bf, bbr)
    xi = _dot(u_bf, bbi)
    first = row == 0
    xr = xr + jnp.where(first, ar * cr - ai * ci, 0.0)
    xi = xi + jnp.where(first, ar * ci + ai * cr, 0.0)
    return _scan_rows(xr, xi, ar, ai, reverse=False)


def _s5_fwd(proj, bbr, bbi, a_re, a_im, ctr, cti, d_skip, d_s5):
    l = proj.shape[0]
    nb, uc, ns = bbr.shape
    t = min(S5_T, l)
    nt = l // t

    def body(u_ref, bbr_ref, bbi_ref, ar_ref, ai_ref, ctr_ref, cti_ref, d_ref,
             y_ref, car_r_ref, car_i_ref, cr, ci):
        @pl.when(pl.program_id(1) == 0)
        def _():
            cr[...] = jnp.zeros_like(cr)
            ci[...] = jnp.zeros_like(ci)

        car_r_ref[...] = cr[...]
        car_i_ref[...] = ci[...]
        u = u_ref[...]
        sr, si = _s5_states(u.astype(BF16), bbr_ref[...], bbi_ref[...], ar_ref[...], ai_ref[...],
                            cr[...], ci[...])
        cr[...] = sr[t - 1:t, :]
        ci[...] = si[t - 1:t, :]
        y_ref[...] = (_bdot(sr, ctr_ref[...]) - _bdot(si, cti_ref[...]) + d_ref[...] * u)

    per_block = lambda shape: pl.BlockSpec((None,) + shape, lambda b, n: (b, 0, 0))
    return pl.pallas_call(
        body, name="s5_fwd",
        out_shape=(jax.ShapeDtypeStruct((l, d_s5), F32),
                   jax.ShapeDtypeStruct((nt, 1, nb * ns), F32),
                   jax.ShapeDtypeStruct((nt, 1, nb * ns), F32)),
        grid=(nb, nt),
        in_specs=[pl.BlockSpec((t, uc), lambda b, n: (n, b)),
                  per_block((uc, ns)), per_block((uc, ns)),
                  per_block((1, ns)), per_block((1, ns)),
                  per_block((ns, uc)), per_block((ns, uc)),
                  pl.BlockSpec((1, uc), lambda b, n: (0, b))],
        out_specs=(pl.BlockSpec((t, uc), lambda b, n: (n, b)),
                   pl.BlockSpec((None, 1, ns), lambda b, n: (n, 0, b)),
                   pl.BlockSpec((None, 1, ns), lambda b, n: (n, 0, b))),
        scratch_shapes=[pltpu.VMEM((1, ns), F32), pltpu.VMEM((1, ns), F32)],
        compiler_params=_cparams("parallel", "arbitrary"),
    )(proj, bbr, bbi, a_re, a_im, ctr, cti, d_skip)


def _s5_bwd(proj, dy, bbr, bbi, a_re, a_im, cbr, cbi, d_skip, car_r, car_i):
    l, d_s5 = dy.shape
    nb, uc, ns = bbr.shape
    t = min(S5_T, l)
    nt = l // t

    def body(u_ref, dy_ref, bbr_ref, bbi_ref, ar_ref, ai_ref, cbr_ref, cbi_ref, d_ref,
             car_r_ref, car_i_ref,
             du_ref, dar_ref, dai_ref, dbbr_ref, dbbi_ref, dcbr_ref, dcbi_ref, dd_ref, gcr, gci):
        @pl.when(pl.program_id(1) == 0)
        def _():
            gcr[...] = jnp.zeros_like(gcr)
            gci[...] = jnp.zeros_like(gci)
            for ref in (dar_ref, dai_ref, dbbr_ref, dbbi_ref, dcbr_ref, dcbi_ref, dd_ref):
                ref[...] = jnp.zeros_like(ref)

        row = lax.broadcasted_iota(jnp.int32, (t, 1), 0)
        u, dy = u_ref[...], dy_ref[...]
        u_bf, dy_bf = u.astype(BF16), dy.astype(BF16)
        ar, ai = ar_ref[...], ai_ref[...]
        cr, ci = car_r_ref[...], car_i_ref[...]
        sr, si = _s5_states(u_bf, bbr_ref[...], bbi_ref[...], ar, ai, cr, ci)
        first = row == 0
        pr = jnp.where(first, cr, pltpu.roll(sr, 1, 0))
        pi = jnp.where(first, ci, pltpu.roll(si, 1, 0))
        last = row == t - 1
        gr = _dot(dy_bf, cbr_ref[...]) + jnp.where(last, ar * gcr[...] + ai * gci[...], 0.0)
        gi = -_dot(dy_bf, cbi_ref[...]) + jnp.where(last, ar * gci[...] - ai * gcr[...], 0.0)
        gr, gi = _scan_rows(gr, gi, ar, -ai, reverse=True)
        gcr[...] = gr[0:1, :]
        gci[...] = gi[0:1, :]
        dar_ref[...] += jnp.sum(gr * pr + gi * pi, axis=0, keepdims=True)
        dai_ref[...] += jnp.sum(gi * pr - gr * pi, axis=0, keepdims=True)
        gr_bf, gi_bf = gr.astype(BF16), gi.astype(BF16)
        tn = ((0,), (0,))
        dbbr_ref[...] += _dot(u_bf, gr_bf, tn)
        dbbi_ref[...] += _dot(u_bf, gi_bf, tn)
        dcbr_ref[...] += _dot(dy_bf, sr.astype(BF16), tn)
        dcbi_ref[...] -= _dot(dy_bf, si.astype(BF16), tn)
        nt_dims = ((1,), (1,))
        du = _dot(gr_bf, bbr_ref[...], nt_dims) + _dot(gi_bf, bbi_ref[...], nt_dims) + dy * d_ref[...]
        du_ref[...] = du.astype(BF16)
        dd_ref[...] += jnp.sum(dy * u, axis=0, keepdims=True)

    rev = lambda n: nt - 1 - n
    per_block = lambda shape: pl.BlockSpec((None,) + shape, lambda b, n: (b, 0, 0))
    acc = jax.ShapeDtypeStruct((nb, uc, ns), F32)
    vec = jax.ShapeDtypeStruct((nb, 1, ns), F32)
    return pl.pallas_call(
        body, name="s5_bwd",
        out_shape=(jax.ShapeDtypeStruct((l, d_s5), BF16), vec, vec, acc, acc, acc, acc,
                   jax.ShapeDtypeStruct((1, d_s5), F32)),
        grid=(nb, nt),
        in_specs=[pl.BlockSpec((t, uc), lambda b, n: (rev(n), b)),
                  pl.BlockSpec((t, uc), lambda b, n: (rev(n), b)),
                  per_block((uc, ns)), per_block((uc, ns)),
                  per_block((1, ns)), per_block((1, ns)),
                  per_block((uc, ns)), per_block((uc, ns)),
                  pl.BlockSpec((1, uc), lambda b, n: (0, b)),
                  pl.BlockSpec((None, 1, ns), lambda b, n: (rev(n), 0, b)),
                  pl.BlockSpec((None, 1, ns), lambda b, n: (rev(n), 0, b))],
        out_specs=(pl.BlockSpec((t, uc), lambda b, n: (rev(n), b)),
                   per_block((1, ns)), per_block((1, ns)),
                   per_block((uc, ns)), per_block((uc, ns)),
                   per_block((uc, ns)), per_block((uc, ns)),
                   pl.BlockSpec((1, uc), lambda b, n: (0, b))),
        scratch_shapes=[pltpu.VMEM((1, ns), F32), pltpu.VMEM((1, ns), F32)],
        compiler_params=_cparams("parallel", "arbitrary"),
    )(proj, dy, bbr, bbi, a_re, a_im, cbr, cbi, d_skip, car_r, car_i)


def _s5_glu_fwd(y1, proj, off_z, wglu):
    l, d = y1.shape

    def body(y_ref, z_ref, w_ref, o_ref):
        y2 = _gelu(y_ref[...])
        y3 = y2 * _sigmoid(_bdot(y2, w_ref[...]))
        o_ref[...] = (y3 * _silu(z_ref[...])).astype(BF16)

    return pl.pallas_call(
        body, name="s5_glu_fwd",
        out_shape=jax.ShapeDtypeStruct((l, d), BF16),
        grid=(l // ROW_TILE,),
        in_specs=[pl.BlockSpec((ROW_TILE, d), lambda i: (i, 0)),
                  pl.BlockSpec((ROW_TILE, d), lambda i: (i, off_z // d)),
                  pl.BlockSpec((d, d), lambda i: (0, 0))],
        out_specs=pl.BlockSpec((ROW_TILE, d), lambda i: (i, 0)),
        compiler_params=_cparams("parallel"),
    )(y1, proj, wglu)


def _s5_glu_bwd(y1, proj, off_z, wglu, dout):
    l, d = y1.shape

    def body(y_ref, z_ref, w_ref, do_ref, dy_ref, dz_ref, dw_ref):
        y2, gelu_vjp = jax.vjp(_gelu, y_ref[...])
        z = z_ref[...]
        sz, silu_vjp = jax.vjp(_silu, z)
        y2_bf = y2.astype(BF16)
        sg = _sigmoid(_dot(y2_bf, w_ref[...]))
        dout = do_ref[...]
        dy3 = dout * sz
        dz_ref[...] = silu_vjp(dout * (y2 * sg))[0].astype(BF16)
        dgl = (dy3 * y2 * sg * (1.0 - sg)).astype(BF16)
        dy2 = dy3 * sg + _dot(dgl, w_ref[...], ((1,), (1,)))
        dy_ref[...] = gelu_vjp(dy2)[0]

        @pl.when(pl.program_id(0) == 0)
        def _():
            dw_ref[...] = jnp.zeros_like(dw_ref)

        dw_ref[...] += _dot(y2_bf, dgl, ((0,), (0,)))

    row = pl.BlockSpec((ROW_TILE, d), lambda i: (i, 0))
    full = pl.BlockSpec((d, d), lambda i: (0, 0))
    return pl.pallas_call(
        body, name="s5_glu_bwd",
        out_shape=(jax.ShapeDtypeStruct((l, d), F32), jax.ShapeDtypeStruct((l, d), BF16),
                   jax.ShapeDtypeStruct((d, d), F32)),
        grid=(l // ROW_TILE,),
        in_specs=[row, pl.BlockSpec((ROW_TILE, d), lambda i: (i, off_z // d)), full, row],
        out_specs=(row, row, full),
        compiler_params=_cparams("arbitrary"),
    )(y1, proj, wglu, dout)


def _shift_rows(x, k, back=False):
    if k == 0:
        return x
    t = x.shape[0]
    row = lax.broadcasted_iota(jnp.int32, (t, 1), 0)
    if back:
        return jnp.where(row < t - k, pltpu.roll(x, t - k, 0), 0.0)
    return jnp.where(row >= k, pltpu.roll(x, k, 0), 0.0)


def _dn_conv(x, w_ref):
    return sum(w_ref[CONV_K - 1 - k:CONV_K - k, :] * _shift_rows(x, k) for k in range(CONV_K))


def _dn_post_conv(c, j):
    y = _silu(c)
    n = y * lax.rsqrt(jnp.sum(y * y, axis=-1, keepdims=True) + EPS)
    n = n * jnp.where(j < DN_HEADS, DN_HEAD_DIM ** -0.5, 1.0)
    return jnp.where(j < 2 * DN_HEADS, n, y)


def _dn_prep_fwd(proj, off_qkv, conv_w):
    l = proj.shape[0]
    hd = DN_HEAD_DIM
    nblk = 3 * DN_HEADS

    def body(x_ref, w_ref, o_ref):
        o_ref[...] = _dn_post_conv(_dn_conv(x_ref[...], w_ref), pl.program_id(0))

    return pl.pallas_call(
        body, name="dn_prep_fwd",
        out_shape=jax.ShapeDtypeStruct((l, nblk * hd), F32),
        grid=(nblk,),
        in_specs=[pl.BlockSpec((l, hd), lambda j: (0, off_qkv // hd + j)),
                  pl.BlockSpec((CONV_K, hd), lambda j: (0, j))],
        out_specs=pl.BlockSpec((l, hd), lambda j: (0, j)),
        compiler_params=_cparams("parallel"),
    )(proj, conv_w)


def _dn_prep_bwd(proj, off_qkv, conv_w, dqkv):
    l = proj.shape[0]
    hd = DN_HEAD_DIM
    nblk = 3 * DN_HEADS

    def body(x_ref, w_ref, do_ref, dx_ref, dw_ref):
        x = x_ref[...]
        j = pl.program_id(0)
        _, vjp = jax.vjp(functools.partial(_dn_post_conv, j=j), _dn_conv(x, w_ref))
        dc = vjp(do_ref[...])[0]
        dx = sum(w_ref[CONV_K - 1 - k:CONV_K - k, :] * _shift_rows(dc, k, back=True)
                 for k in range(CONV_K))
        dx_ref[...] = dx.astype(BF16)
        for k in range(CONV_K):
            dw_ref[CONV_K - 1 - k:CONV_K - k, :] = jnp.sum(dc * _shift_rows(x, k), axis=0,
                                                           keepdims=True)

    return pl.pallas_call(
        body, name="dn_prep_bwd",
        out_shape=(jax.ShapeDtypeStruct((l, nblk * hd), BF16),
                   jax.ShapeDtypeStruct((CONV_K, nblk * hd), F32)),
        grid=(nblk,),
        in_specs=[pl.BlockSpec((l, hd), lambda j: (0, off_qkv // hd + j)),
                  pl.BlockSpec((CONV_K, hd), lambda j: (0, j)),
                  pl.BlockSpec((None, l, hd), lambda j: (j // DN_HEADS, 0, j % DN_HEADS))],
        out_specs=(pl.BlockSpec((l, hd), lambda j: (0, j)),
                   pl.BlockSpec((CONV_K, hd), lambda j: (0, j))),
        compiler_params=_cparams("parallel"),
    )(proj, conv_w, dqkv)


def _dn_gate_fn(ba, a_log_row, dt_row):
    lane = lax.broadcasted_iota(jnp.int32, ba.shape, 1)
    beta = _sigmoid(ba)
    g = -jnp.exp(a_log_row) * _softplus(ba + dt_row)
    return jnp.where(lane < DN_HEADS, beta, jnp.where(lane < 2 * DN_HEADS, g, 0.0))


def _dn_gates_fwd(proj, off_ba, a_log_row, dt_row):
    l = proj.shape[0]
    row = pl.BlockSpec((ROW_TILE, 128), lambda i: (i, off_ba // 128))
    vec = pl.BlockSpec((1, 128), lambda i: (0, 0))

    def body(ba_ref, al_ref, dt_ref, o_ref):
        o_ref[...] = _dn_gate_fn(ba_ref[...], al_ref[...], dt_ref[...])

    return pl.pallas_call(
        body, name="dn_gates_fwd",
        out_shape=jax.ShapeDtypeStruct((l, 128), F32),
        grid=(l // ROW_TILE,),
        in_specs=[row, vec, vec],
        out_specs=pl.BlockSpec((ROW_TILE, 128), lambda i: (i, 0)),
        compiler_params=_cparams("parallel"),
    )(proj, a_log_row, dt_row)


def _dn_gates_bwd(proj, off_ba, a_log_row, dt_row, dgb_heads):
    l = proj.shape[0]
    nh = dgb_heads.shape[0]
    row = pl.BlockSpec((ROW_TILE, 128), lambda i: (i, off_ba // 128))
    vec = pl.BlockSpec((1, 128), lambda i: (0, 0))

    def body(ba_ref, al_ref, dt_ref, dg_ref, dba_ref, dal_ref, ddt_ref):
        _, vjp = jax.vjp(_dn_gate_fn, ba_ref[...], al_ref[...], dt_ref[...])
        dgb = dg_ref[0]
        for h in range(1, nh):
            dgb = dgb + dg_ref[h]
        dba, dal, ddt = vjp(dgb)
        dba_ref[...] = dba.astype(BF16)

        @pl.when(pl.program_id(0) == 0)
        def _():
            dal_ref[...] = jnp.zeros_like(dal_ref)
            ddt_ref[...] = jnp.zeros_like(ddt_ref)

        dal_ref[...] += dal
        ddt_ref[...] += ddt

    return pl.pallas_call(
        body, name="dn_gates_bwd",
        out_shape=(jax.ShapeDtypeStruct((l, 128), BF16), jax.ShapeDtypeStruct((1, 128), F32),
                   jax.ShapeDtypeStruct((1, 128), F32)),
        grid=(l // ROW_TILE,),
        in_specs=[row, vec, vec, pl.BlockSpec((nh, ROW_TILE, 128), lambda i: (0, i, 0))],
        out_specs=(pl.BlockSpec((ROW_TILE, 128), lambda i: (i, 0)), vec, vec),
        compiler_params=_cparams("arbitrary"),
    )(proj, a_log_row, dt_row, dgb_heads)


def _dn_chunk_fn(states, qs, ks, vs, gb, heads):
    c = qs[0].shape[0]
    each = lambda f, *lists: [f(*args) for args in zip(*lists)]
    lane = lax.broadcasted_iota(jnp.int32, gb.shape, 1)
    ri = lax.broadcasted_iota(jnp.int32, (c, c), 0)
    ci = lax.broadcasted_iota(jnp.int32, (c, c), 1)
    causal, strict = ri >= ci, ri > ci
    eye = (ri == ci).astype(F32)
    rowi = lax.broadcasted_iota(jnp.int32, (c, 1), 0)
    nt_dims = ((1,), (1,))
    hdot = functools.partial(_dot, precision=HIGHEST)

    pick = lambda m, at: jnp.sum(jnp.where(lane == at, m, 0.0), axis=1, keepdims=True)
    gb_cum = hdot(causal.astype(F32), gb)
    beta = [pick(gb, h) for h in heads]
    gc = [pick(gb_cum, h + DN_HEADS) for h in heads]
    gc_row = each(lambda g: jnp.sum(eye * g, axis=0, keepdims=True), gc)
    decay = each(lambda g, gr: jnp.where(causal, jnp.exp(jnp.where(causal, g - gr, 0.0)), 0.0),
                 gc, gc_row)
    kk = each(lambda k: _bdot(k, k, nt_dims), ks)
    a_mat = each(lambda b, m, dc: jnp.where(strict, b * m * dc, 0.0), beta, kk, decay)

    t_inv = each(lambda a: eye - a, a_mat)
    power = a_mat
    for _ in range(int(math.log2(c)) - 1):
        power = each(lambda p: hdot(p, p), power)
        t_inv = each(lambda t, p: t + hdot(t, p), t_inv, power)

    egc = each(jnp.exp, gc)
    u_c = each(lambda t, v, b: hdot(t, v * b), t_inv, vs, beta)
    w_c = each(lambda t, k, b, e: hdot(t, k * (b * e)), t_inv, ks, beta, egc)
    qk = each(lambda q, k, dc: _bdot(q, k, nt_dims) * dc, qs, ks, decay)
    g_end = each(lambda g: jnp.sum(jnp.where(rowi == c - 1, g, 0.0), axis=0, keepdims=True), gc)
    v_new = each(lambda u, w, s: u - _bdot(w, s), u_c, w_c, states)
    o = each(lambda q, e, s, m, vn: _bdot(q * e, s) + _bdot(m, vn), qs, egc, states, qk, v_new)
    new_states = each(
        lambda s, ge, k, g, vn: s * jnp.exp(ge) + _bdot(k * jnp.exp(ge - g), vn, ((0,), (0,))),
        states, g_end, ks, gc, v_new)
    return o, new_states


def _dn_chunk_specs(order):
    hd, nh, hps = DN_HEAD_DIM, DN_HEADS, DN_HEADS_PER_STEP
    qkv = lambda part: pl.BlockSpec((CHUNK, hps * hd), lambda h, n: (order(n), part * (nh // hps) + h))
    gb = pl.BlockSpec((CHUNK, 128), lambda h, n: (order(n), 0))
    state = pl.BlockSpec((hps, None, hd, hd), lambda h, n: (h, order(n), 0, 0))
    return qkv, gb, state


def _dn_chunk_fwd(qkv, gb):
    l = qkv.shape[0]
    hd, nh, hps = DN_HEAD_DIM, DN_HEADS, DN_HEADS_PER_STEP
    n_chunks = l // CHUNK
    qkv_spec, gb_spec, state_spec = _dn_chunk_specs(lambda n: n)

    def body(q_ref, k_ref, v_ref, gb_ref, o_ref, s_ref, state):
        @pl.when(pl.program_id(1) == 0)
        def _():
            state[...] = jnp.zeros_like(state)

        cols = [slice(i * hd, (i + 1) * hd) for i in range(hps)]
        states = [state[i] for i in range(hps)]
        for i in range(hps):
            s_ref[i] = states[i]
        o, new_states = _dn_chunk_fn(
            states, [q_ref[:, cs] for cs in cols], [k_ref[:, cs] for cs in cols],
            [v_ref[:, cs] for cs in cols], gb_ref[...],
            [pl.program_id(0) * hps + i for i in range(hps)])
        for i in range(hps):
            o_ref[:, cols[i]] = o[i]
            state[i] = new_states[i]

    return pl.pallas_call(
        body, name="dn_chunk_fwd",
        out_shape=(jax.ShapeDtypeStruct((l, nh * hd), F32),
                   jax.ShapeDtypeStruct((nh, n_chunks, hd, hd), F32)),
        grid=(nh // hps, n_chunks),
        in_specs=[qkv_spec(0), qkv_spec(1), qkv_spec(2), gb_spec],
        out_specs=(pl.BlockSpec((CHUNK, hps * hd), lambda h, n: (n, h)), state_spec),
        scratch_shapes=[pltpu.VMEM((hps, hd, hd), F32)],
        compiler_params=_cparams("parallel", "arbitrary"),
    )(qkv, qkv, qkv, gb)


def _dn_chunk_bwd(qkv, gb, states, do):
    l = qkv.shape[0]
    hd, nh, hps = DN_HEAD_DIM, DN_HEADS, DN_HEADS_PER_STEP
    n_chunks = l // CHUNK
    rev = lambda n: n_chunks - 1 - n
    qkv_spec, gb_spec, state_spec = _dn_chunk_specs(rev)

    def body(q_ref, k_ref, v_ref, gb_ref, s_ref, do_ref, dqkv_ref, dgb_ref, dstate):
        @pl.when(pl.program_id(1) == 0)
        def _():
            dstate[...] = jnp.zeros_like(dstate)

        cols = [slice(i * hd, (i + 1) * hd) for i in range(hps)]
        fn = functools.partial(_dn_chunk_fn, heads=[pl.program_id(0) * hps + i for i in range(hps)])
        _, vjp = jax.vjp(fn, [s_ref[i] for i in range(hps)], [q_ref[:, cs] for cs in cols],
                         [k_ref[:, cs] for cs in cols], [v_ref[:, cs] for cs in cols], gb_ref[...])
        ds, dq, dk, dv, dgb = vjp(([do_ref[:, cs] for cs in cols], [dstate[i] for i in range(hps)]))
        for i in range(hps):
            dstate[i] = ds[i]
            dqkv_ref[0, :, cols[i]] = dq[i]
            dqkv_ref[1, :, cols[i]] = dk[i]
            dqkv_ref[2, :, cols[i]] = dv[i]
        dgb_ref[...] = dgb

    head_out = pl.BlockSpec((CHUNK, hps * hd), lambda h, n: (rev(n), h))
    return pl.pallas_call(
        body, name="dn_chunk_bwd",
        out_shape=(jax.ShapeDtypeStruct((3, l, nh * hd), F32),
                   jax.ShapeDtypeStruct((nh // hps, l, 128), F32)),
        grid=(nh // hps, n_chunks),
        in_specs=[qkv_spec(0), qkv_spec(1), qkv_spec(2), gb_spec, state_spec, head_out],
        out_specs=(pl.BlockSpec((3, CHUNK, hps * hd), lambda h, n: (0, rev(n), h)),
                   pl.BlockSpec((None, CHUNK, 128), lambda h, n: (h, rev(n), 0))),
        scratch_shapes=[pltpu.VMEM((hps, hd, hd), F32)],
        compiler_params=_cparams("parallel", "arbitrary"),
    )(qkv, qkv, qkv, gb, states, do)


def _dn_out_fn(o, z, w):
    return _rmsnorm(o, w) * _silu(z)


def _dn_out_fwd(o, proj, off_z, w):
    l, d = o.shape
    hd = DN_HEAD_DIM
    blk = lambda off: pl.BlockSpec((ROW_TILE, hd), lambda i, h: (i, off // hd + h))

    def body(o_ref, z_ref, w_ref, out_ref):
        out_ref[...] = _dn_out_fn(o_ref[...], z_ref[...], w_ref[...]).astype(BF16)

    return pl.pallas_call(
        body, name="dn_out_fwd",
        out_shape=jax.ShapeDtypeStruct((l, d), BF16),
        grid=(l // ROW_TILE, d // hd),
        in_specs=[blk(0), blk(off_z), pl.BlockSpec((1, hd), lambda i, h: (0, 0))],
        out_specs=blk(0),
        compiler_params=_cparams("parallel", "parallel"),
    )(o, proj, w)


def _dn_out_bwd(o, proj, off_z, w, dout):
    l, d = o.shape
    hd = DN_HEAD_DIM
    blk = lambda off: pl.BlockSpec((ROW_TILE, hd), lambda i, h: (i, off // hd + h))
    vec = pl.BlockSpec((1, hd), lambda i, h: (0, 0))

    def body(o_ref, z_ref, w_ref, dout_ref, do_ref, dz_ref, dw_ref):
        _, vjp = jax.vjp(_dn_out_fn, o_ref[...], z_ref[...], w_ref[...])
        do, dz, dw = vjp(dout_ref[...])
        do_ref[...] = do
        dz_ref[...] = dz.astype(BF16)

        @pl.when((pl.program_id(0) == 0) & (pl.program_id(1) == 0))
        def _():
            dw_ref[...] = jnp.zeros_like(dw_ref)

        dw_ref[...] += dw

    return pl.pallas_call(
        body, name="dn_out_bwd",
        out_shape=(jax.ShapeDtypeStruct((l, d), F32), jax.ShapeDtypeStruct((l, d), BF16),
                   jax.ShapeDtypeStruct((1, hd), F32)),
        grid=(l // ROW_TILE, d // hd),
        in_specs=[blk(0), blk(off_z), vec, blk(0)],
        out_specs=(blk(0), blk(0), vec),
        compiler_params=_cparams("arbitrary", "arbitrary"),
    )(o, proj, w, dout)


def _tile_2d(rows, cols, budget_bytes=1 << 20):
    for tr in (rows, 4096, 2048, 1024, 512, 256, 128, 64, 32, 16):
        if tr <= rows and rows % tr == 0 and tr * cols * 4 <= budget_bytes:
            return tr, cols
    for tc in (2048, 1024, 512, 256, 128):
        if cols % tc == 0 and rows * tc * 4 <= 2 * budget_bytes:
            return rows, tc
    raise ValueError((rows, cols))


def _adamw(w, m, v, gslots, name):
    rows, cols = w.shape
    ns = gslots.shape[0]
    tr, tc = _tile_2d(rows, cols)
    c1 = 1.0 / (1.0 - ADAM_B1 ** ADAM_STEP)
    c2 = 1.0 / (1.0 - ADAM_B2 ** ADAM_STEP)

    def body(w_ref, m_ref, v_ref, g_ref, go_ref, d_ref, mo_ref, vo_ref):
        g = g_ref[0].astype(F32)
        for s in range(1, ns):
            g = g + g_ref[s].astype(F32)
        m_new = ADAM_B1 * m_ref[...] + (1.0 - ADAM_B1) * g
        v_new = ADAM_B2 * v_ref[...] + (1.0 - ADAM_B2) * (g * g)
        go_ref[...] = g
        mo_ref[...] = m_new
        vo_ref[...] = v_new
        d_ref[...] = -ADAM_LR * ((m_new * c1) / (jnp.sqrt(v_new * c2) + ADAM_EPS) + ADAM_WD * w_ref[...])

    blk = pl.BlockSpec((tr, tc), lambda i, j: (i, j))
    o = jax.ShapeDtypeStruct((rows, cols), F32)
    return pl.pallas_call(
        body, name=name, out_shape=(o, o, o, o),
        grid=(rows // tr, cols // tc),
        in_specs=[blk, blk, blk, pl.BlockSpec((ns, tr, tc), lambda i, j: (0, i, j))],
        out_specs=(blk, blk, blk, blk),
        compiler_params=_cparams("parallel", "parallel"),
    )(w, m, v, gslots)


def _slot_sum(gslots, name):
    ns, rows, cols = gslots.shape
    tr, tc = _tile_2d(rows, cols)

    def body(g_ref, o_ref):
        g = g_ref[0]
        for s in range(1, ns):
            g = g + g_ref[s]
        o_ref[...] = g

    return pl.pallas_call(
        body, name=name, out_shape=jax.ShapeDtypeStruct((rows, cols), F32),
        grid=(rows // tr, cols // tc),
        in_specs=[pl.BlockSpec((ns, tr, tc), lambda i, j: (0, i, j))],
        out_specs=pl.BlockSpec((tr, tc), lambda i, j: (i, j)),
        compiler_params=_cparams("parallel", "parallel"),
    )(gslots)


HBM_SPEC = pl.BlockSpec(memory_space=pl.ANY)


def _all_gather(arrs, name):
    n = len(arrs)

    def body(*refs):
        ins, outs = refs[:n], refs[n:2 * n]
        send_sems, recv_sems, local_sems = refs[2 * n:]
        x, y, c = lax.axis_index("x"), lax.axis_index("y"), lax.axis_index("c")
        me, sibling = (x, y, c), (x, y, 1 - c)
        chips = [(1 - x, y), (x, 1 - y), (1 - x, 1 - y)]
        index = lambda px, py, pc: 4 * px + 2 * py + pc

        def copy(a, k, block, to, src=None):
            rows = outs[a].at[index(*block)]
            return pltpu.make_async_remote_copy(
                src_ref=rows if src is None else src, dst_ref=rows,
                send_sem=send_sems.at[a, k], recv_sem=recv_sems.at[a, k],
                device_id=to, device_id_type=MESH)

        mine = [pltpu.make_async_copy(ins[a], outs[a].at[index(*me)], local_sems.at[a])
                for a in range(n)]
        for cp in mine:
            cp.start()
        first = []
        for a in range(n):
            first.append(copy(a, 0, me, sibling, src=ins[a]))
            first += [copy(a, 1 + j, me, (*chip, c), src=ins[a]) for j, chip in enumerate(chips)]
        for cp in first:
            cp.start()
        passed = []
        for j, chip in enumerate(chips):
            for a in range(n):
                copy(a, 1 + j, (*chip, c), me).wait_recv()
                fwd = copy(a, 4 + j, (*chip, c), sibling)
                fwd.start()
                passed.append(fwd)
        for a in range(n):
            copy(a, 0, sibling, me).wait_recv()
            for j, chip in enumerate(chips):
                copy(a, 4 + j, (*chip, 1 - c), me).wait_recv()
        for cp in first + passed:
            cp.wait_send()
        for cp in mine:
            cp.wait()

    return pl.pallas_call(
        body, name=name,
        out_shape=[jax.ShapeDtypeStruct((N_DEV,) + a.shape, a.dtype) for a in arrs],
        in_specs=[HBM_SPEC] * n, out_specs=[HBM_SPEC] * n,
        scratch_shapes=[pltpu.SemaphoreType.DMA((n, 7)), pltpu.SemaphoreType.DMA((n, 7)),
                        pltpu.SemaphoreType.DMA((n,))],
    )(*arrs)


def _sibling_swap(arrs, name):
    n = len(arrs)

    def body(*refs):
        ins, outs = refs[:n], refs[n:2 * n]
        send_sems, recv_sems = refs[2 * n:]
        x, y, c = lax.axis_index("x"), lax.axis_index("y"), lax.axis_index("c")
        copies = [pltpu.make_async_remote_copy(
            src_ref=ins[a].at[1 - c], dst_ref=outs[a],
            send_sem=send_sems.at[a], recv_sem=recv_sems.at[a],
            device_id=(x, y, 1 - c), device_id_type=MESH) for a in range(n)]
        for cp in copies:
            cp.start()
        for cp in copies:
            cp.wait()

    return pl.pallas_call(
        body, name=name,
        out_shape=[jax.ShapeDtypeStruct(a.shape[1:], a.dtype) for a in arrs],
        in_specs=[HBM_SPEC] * n, out_specs=[HBM_SPEC] * n,
        scratch_shapes=[pltpu.SemaphoreType.DMA((n,)), pltpu.SemaphoreType.DMA((n,))],
    )(*arrs)


def _pair_sum(mine, theirs, core, name):
    _, rows, cols = mine.shape
    tr, tc = _tile_2d(rows, cols, budget_bytes=2 << 20)

    def body(core_ref, a_ref, b_ref, o_ref):
        o_ref[...] = (a_ref[...].astype(F32) + b_ref[...].astype(F32)).astype(o_ref.dtype)

    return pl.pallas_call(
        body, name=name, out_shape=jax.ShapeDtypeStruct((rows, cols), mine.dtype),
        grid_spec=pltpu.PrefetchScalarGridSpec(
            num_scalar_prefetch=1, grid=(rows // tr, cols // tc),
            in_specs=[pl.BlockSpec((None, tr, tc), lambda i, j, core_ref: (core_ref[0], i, j)),
                      pl.BlockSpec((tr, tc), lambda i, j, core_ref: (i, j))],
            out_specs=pl.BlockSpec((tr, tc), lambda i, j, core_ref: (i, j))),
        compiler_params=_cparams("parallel", "parallel"),
    )(core, mine, theirs)


def _chip_exchange(arrs, name):
    n = len(arrs)

    def body(*refs):
        ins, outs = refs[:n], refs[n:2 * n]
        send_sems, recv_sems, local_sems = refs[2 * n:]
        x, y, c = lax.axis_index("x"), lax.axis_index("y"), lax.axis_index("c")
        me = 2 * x + y
        flip = lambda v, bit: 1 - v if bit else v
        mine = [pltpu.make_async_copy(ins[a].at[me], outs[a].at[me], local_sems.at[a])
                for a in range(n)]
        for cp in mine:
            cp.start()
        copies = []
        for k in range(1, 4):
            px, py = flip(x, k & 2), flip(y, k & 1)
            for a in range(n):
                copies.append(pltpu.make_async_remote_copy(
                    src_ref=ins[a].at[2 * px + py], dst_ref=outs[a].at[me],
                    send_sem=send_sems.at[a, k - 1], recv_sem=recv_sems.at[a, k - 1],
                    device_id=(px, py, c), device_id_type=MESH))
        for cp in copies:
            cp.start()
        for cp in copies:
            cp.wait()
        for cp in mine:
            cp.wait()

    return pl.pallas_call(
        body, name=name,
        out_shape=[jax.ShapeDtypeStruct(a.shape, a.dtype) for a in arrs],
        in_specs=[HBM_SPEC] * n, out_specs=[HBM_SPEC] * n,
        scratch_shapes=[pltpu.SemaphoreType.DMA((n, 3)), pltpu.SemaphoreType.DMA((n, 3)),
                        pltpu.SemaphoreType.DMA((n,))],
    )(*arrs)


def _block_diag(t):
    nb, gpb, r, c = t.shape
    eye = jnp.eye(gpb, dtype=t.dtype)
    return jnp.einsum("ngrc,gh->ngrhc", t, eye).reshape(nb, gpb * r, gpb * c)


def _diag_blocks(t, r, c):
    nb = t.shape[0]
    gpb = t.shape[1] // r
    t = t.reshape(nb, gpb, r, gpb, c)
    return jnp.einsum("ngrhc,gh->ngrc", t, jnp.eye(gpb, dtype=t.dtype))


def _pack_rows(parts):
    flat = jnp.concatenate([p.reshape(-1).astype(F32) for p in parts])
    pad = (-flat.shape[0]) % (256 * 128)
    return jnp.pad(flat, (0, pad)).reshape(-1, 128)


def _unpack_rows(packed, shapes):
    flat = packed.reshape(-1)
    out, at = [], 0
    for shape in shapes:
        size = math.prod(shape)
        out.append(flat[at:at + size].reshape(shape))
        at += size
    return out


def kernel(x, ln_w, w_in, s5_lam_re, s5_lam_im, s5_log_step, s5_b_re, s5_b_im, s5_c_re, s5_c_im, s5_d, s5_w_glu, s5_w_up, dn_conv_w, dn_a_log, dn_dt_bias, dn_norm_w, dn_w_up, w_out, final_norm_w, loss_target, m_ln_w, m_w_in, m_s5_lam_re, m_s5_lam_im, m_s5_log_step, m_s5_b_re, m_s5_b_im, m_s5_c_re, m_s5_c_im, m_s5_d, m_s5_w_glu, m_s5_w_up, m_dn_conv_w, m_dn_a_log, m_dn_dt_bias, m_dn_norm_w, m_dn_w_up, m_w_out, m_final_norm_w, v_ln_w, v_w_in, v_s5_lam_re, v_s5_lam_im, v_s5_log_step, v_s5_b_re, v_s5_b_im, v_s5_c_re, v_s5_c_im, v_s5_d, v_s5_w_glu, v_s5_w_up, v_dn_conv_w, v_dn_a_log, v_dn_dt_bias, v_dn_norm_w, v_dn_w_up, v_w_out, v_final_norm_w):
    weights = dict(ln_w=ln_w, w_in=w_in, s5_lam_re=s5_lam_re, s5_lam_im=s5_lam_im,
                   s5_log_step=s5_log_step, s5_b_re=s5_b_re, s5_b_im=s5_b_im, s5_c_re=s5_c_re,
                   s5_c_im=s5_c_im, s5_d=s5_d, s5_w_glu=s5_w_glu, s5_w_up=s5_w_up,
                   dn_conv_w=dn_conv_w, dn_a_log=dn_a_log, dn_dt_bias=dn_dt_bias,
                   dn_norm_w=dn_norm_w, dn_w_up=dn_w_up, w_out=w_out, final_norm_w=final_norm_w)
    mom_m = dict(ln_w=m_ln_w, w_in=m_w_in, s5_lam_re=m_s5_lam_re, s5_lam_im=m_s5_lam_im,
                 s5_log_step=m_s5_log_step, s5_b_re=m_s5_b_re, s5_b_im=m_s5_b_im,
                 s5_c_re=m_s5_c_re, s5_c_im=m_s5_c_im, s5_d=m_s5_d, s5_w_glu=m_s5_w_glu,
                 s5_w_up=m_s5_w_up, dn_conv_w=m_dn_conv_w, dn_a_log=m_dn_a_log,
                 dn_dt_bias=m_dn_dt_bias, dn_norm_w=m_dn_norm_w, dn_w_up=m_dn_w_up,
                 w_out=m_w_out, final_norm_w=m_final_norm_w)
    mom_v = dict(ln_w=v_ln_w, w_in=v_w_in, s5_lam_re=v_s5_lam_re, s5_lam_im=v_s5_lam_im,
                 s5_log_step=v_s5_log_step, s5_b_re=v_s5_b_re, s5_b_im=v_s5_b_im,
                 s5_c_re=v_s5_c_re, s5_c_im=v_s5_c_im, s5_d=v_s5_d, s5_w_glu=v_s5_w_glu,
                 s5_w_up=v_s5_w_up, dn_conv_w=v_dn_conv_w, dn_a_log=v_dn_a_log,
                 dn_dt_bias=v_dn_dt_bias, dn_norm_w=v_dn_norm_w, dn_w_up=v_dn_w_up,
                 w_out=v_w_out, final_norm_w=v_final_norm_w)
    names = list(weights)

    l, d = x.shape[1], x.shape[2]
    d_s5 = d // 2
    groups = d_s5 // S5_GROUP
    nb = groups // S5_GPB
    d_dn = DN_HEADS * DN_HEAD_DIM
    w_in_cols = w_in.shape[2]
    d_in = N_DEV * w_in_cols
    off_ba_src = 2 * d_s5 + 4 * d_dn
    off_u, off_zs, off_qkv, off_zd = 0, d_s5, 2 * d_s5, 2 * d_s5 + 3 * d_dn
    off_ba = off_zd + d_dn
    off_gs = off_ba + BA_PAD
    off_gd = off_gs + d
    n_proj = off_gd + d
    x2d, tgt2d = x[0], loss_target[0]
    my_index = 4 * lax.axis_index("x") + 2 * lax.axis_index("y") + lax.axis_index("c")

    g_win, g_glu, g_sup, g_dup, g_wout, g_conv = _all_gather(
        [jnp.transpose(w_in[0]).astype(BF16), s5_w_glu[0].astype(BF16), s5_w_up[0].astype(BF16),
         dn_w_up[0].astype(BF16), w_out[0].astype(BF16), dn_conv_w[0]], name="gather_weights")
    w_full_t = g_win.reshape(d_in, d)
    ba_end = off_ba_src + 2 * DN_HEADS
    w_cat_t = jnp.concatenate(
        [w_full_t[:ba_end], jnp.zeros((BA_PAD - 2 * DN_HEADS, d), BF16), w_full_t[ba_end:]],
        axis=0)
    wglu_full = g_glu.reshape(d_s5, d_s5)
    wsup_full = jnp.transpose(g_sup, (1, 0, 2)).reshape(d_s5, d)
    wdup_full = jnp.transpose(g_dup, (1, 0, 2)).reshape(d_dn, d)
    wout_full = g_wout.reshape(d, d)
    conv_full = jnp.transpose(g_conv, (1, 0, 2)).reshape(CONV_K, 3 * d_dn)

    lam_re, lam_im = s5_lam_re[0], s5_lam_im[0]
    log_step = s5_log_step[0].reshape(groups, 1)
    b_re = s5_b_re[0].reshape(groups * S5_STATE, S5_GROUP)
    b_im = s5_b_im[0].reshape(groups * S5_STATE, S5_GROUP)
    abar_re, abar_im, f_re, f_im = _s5_disc_fwd(lam_re, lam_im, log_step)
    f_re_col, f_im_col = f_re.reshape(-1, 1), f_im.reshape(-1, 1)
    bb_re, bb_im = _s5_bbar_fwd(f_re_col, f_im_col, b_re, b_im)

    def bb_blocks(t):
        t = t.reshape(nb, S5_GPB, S5_STATE, S5_GROUP).transpose(0, 1, 3, 2)
        return _block_diag(t).astype(BF16)

    def c_blocks(t):
        return _block_diag(t.reshape(nb, S5_GPB, S5_GROUP, S5_STATE)).astype(BF16)

    bbr, bbi = bb_blocks(bb_re), bb_blocks(bb_im)
    cbr, cbi = c_blocks(s5_c_re[0]), c_blocks(s5_c_im[0])
    ctr, cti = jnp.transpose(cbr, (0, 2, 1)), jnp.transpose(cbi, (0, 2, 1))
    a_re = abar_re.reshape(nb, 1, S5_GPB * S5_STATE)
    a_im = abar_im.reshape(nb, 1, S5_GPB * S5_STATE)

    h = _rms_fwd(x2d, ln_w)
    proj = _mm(h, w_cat_t, tb=True, tn=1536, name="proj")
    y1, car_r, car_i = _s5_fwd(proj, bbr, bbi, a_re, a_im, ctr, cti, s5_d, d_s5)
    out_s = _s5_glu_fwd(y1, proj, off_zs, wglu_full)
    y_s = _mm(out_s, wsup_full, name="s5_up")

    a_log_row = jnp.pad(dn_a_log, ((0, 0), (DN_HEADS, 128 - 2 * DN_HEADS)))
    dt_row = jnp.pad(dn_dt_bias, ((0, 0), (DN_HEADS, 128 - 2 * DN_HEADS)))
    qkv = _dn_prep_fwd(proj, off_qkv, conv_full)
    gb = _dn_gates_fwd(proj, off_ba, a_log_row, dt_row)
    o_dn, states = _dn_chunk_fwd(qkv, gb)
    out_d = _dn_out_fwd(o_dn, proj, off_zd, dn_norm_w)
    y_d = _mm(out_d, wdup_full, name="dn_up")

    mixed = _merge_fwd(proj, off_gs, off_gd, y_s, y_d)
    branch = _mm(mixed, wout_full, name="w_out")
    dx2, dx2_bf, loss_dev, d_final_w = _final(x2d, branch, final_norm_w.reshape(1, d), tgt2d)

    g_wout_full = _mm(mixed, dx2_bf, ta=True, out_dtype=BF16, name="grad_w_out")
    dmixed = _mm(dx2_bf, wout_full, tb=True, name="d_mixed")
    dgs, dgd, dys, dyd = _merge_bwd(proj, off_gs, off_gd, y_s, y_d, dmixed)

    g_dup_full = _mm(out_d, dyd, ta=True, out_dtype=BF16, name="grad_dn_up")
    dout_d = _mm(dyd, wdup_full, tb=True, name="d_out_d")
    do_dn, dzd, d_norm_w = _dn_out_bwd(o_dn, proj, off_zd, dn_norm_w, dout_d)
    dqkv, dgb_heads = _dn_chunk_bwd(qkv, gb, states, do_dn)
    dba, d_a_log_row, d_dt_row = _dn_gates_bwd(proj, off_ba, a_log_row, dt_row, dgb_heads)
    dqkv_pre, d_conv_full = _dn_prep_bwd(proj, off_qkv, conv_full, dqkv)

    g_sup_full = _mm(out_s, dys, ta=True, out_dtype=BF16, name="grad_s5_up")
    dout_s = _mm(dys, wsup_full, tb=True, name="d_out_s")
    dy1, dzs, g_glu_full = _s5_glu_bwd(y1, proj, off_zs, wglu_full, dout_s)
    (du, d_a_re, d_a_im, d_bbr, d_bbi, d_cbr, d_cbi, d_s5_d) = _s5_bwd(
        proj, dy1, bbr, bbi, a_re, a_im, cbr, cbi, s5_d, car_r, car_i)

    def from_bb_blocks(t):
        t = _diag_blocks(t, S5_GROUP, S5_STATE).transpose(0, 1, 3, 2)
        return t.reshape(groups * S5_STATE, S5_GROUP)

    d_f_re, d_f_im, d_b_re, d_b_im = _s5_bbar_bwd(f_re_col, f_im_col, b_re, b_im,
                                                 from_bb_blocks(d_bbr), from_bb_blocks(d_bbi))
    d_lam_re, d_lam_im, d_log_step = _s5_disc_bwd(
        lam_re, lam_im, log_step, d_a_re.reshape(groups, S5_STATE), d_a_im.reshape(groups, S5_STATE),
        d_f_re.reshape(groups, S5_STATE), d_f_im.reshape(groups, S5_STATE))
    d_c_re = _diag_blocks(d_cbr, S5_GROUP, S5_STATE).reshape(groups, S5_GROUP, S5_STATE)
    d_c_im = _diag_blocks(d_cbi, S5_GROUP, S5_STATE).reshape(groups, S5_GROUP, S5_STATE)

    dproj = jnp.concatenate(
        [du, dzs, dqkv_pre, dzd, jnp.pad(dba, ((0, 0), (0, BA_PAD - 128))), dgs, dgd], axis=1)
    g_wcat_t = _mm(dproj, h, ta=True, out_dtype=BF16, tm=768, tn=d, name="grad_w_in")
    dh = _mm(dproj, w_cat_t, tm=1024, tn=1024, tk=1536, name="d_h")
    grad_x, d_ln_w = _rms_bwd(x2d, ln_w, dh, dx2)

    g_win_full_t = jnp.concatenate([g_wcat_t[:off_ba + 2 * DN_HEADS], g_wcat_t[off_gs:]], axis=0)

    def by_dest(t, axis):
        if axis == 1:
            return t.reshape(t.shape[0], 4, 2, t.shape[1] // N_DEV).transpose(2, 1, 0, 3)
        return t.reshape(4, 2, t.shape[0] // N_DEV, t.shape[1]).transpose(1, 0, 2, 3)

    big = ["w_in", "s5_w_glu", "s5_w_up", "dn_w_up", "w_out"]
    parts = [by_dest(g_win_full_t, 0), by_dest(g_glu_full.astype(BF16), 0), by_dest(g_sup_full, 1),
             by_dest(g_dup_full, 1), by_dest(g_wout_full, 0)]
    from_sibling = _sibling_swap(parts, name="swap_grads")
    core = lax.axis_index("c").astype(jnp.int32).reshape(1)
    chip_sums = [
        _pair_sum(p.reshape(2, -1, p.shape[-1]), got.reshape(-1, got.shape[-1]), core,
                  name="pair_sum_" + nm).reshape(got.shape)
        for nm, p, got in zip(big, parts, from_sibling)]
    slots = _chip_exchange(chip_sums, name="exchange_grads")
    results = {}
    for nm, sl in zip(big, slots):
        operands = [weights[nm][0], mom_m[nm][0], mom_v[nm][0]]
        if nm == "w_in":
            operands = [jnp.transpose(t) for t in operands]
        res = _adamw(*operands, sl, name="adamw_" + nm)
        results[nm] = [jnp.transpose(t) for t in res] if nm == "w_in" else res

    small = [nm for nm in names if nm not in big]
    small_grads = dict(
        ln_w=d_ln_w, s5_lam_re=d_lam_re, s5_lam_im=d_lam_im, s5_log_step=d_log_step,
        s5_b_re=d_b_re, s5_b_im=d_b_im, s5_c_re=d_c_re, s5_c_im=d_c_im, s5_d=d_s5_d,
        dn_conv_w=d_conv_full, dn_a_log=d_a_log_row[:, DN_HEADS:2 * DN_HEADS],
        dn_dt_bias=d_dt_row[:, DN_HEADS:2 * DN_HEADS], dn_norm_w=d_norm_w, final_norm_w=d_final_w)
    (all_small,) = _all_gather([_pack_rows([small_grads[nm] for nm in small])], name="gather_small_grads")
    summed = _slot_sum(all_small, name="sum_small_grads")
    full_shapes = [(CONV_K, 3 * d_dn) if nm == "dn_conv_w" else weights[nm].shape for nm in small]
    g_small = dict(zip(small, _unpack_rows(summed, full_shapes)))
    conv_cols = dn_conv_w.shape[2]
    g_small["dn_conv_w"] = lax.dynamic_slice_in_dim(
        g_small["dn_conv_w"], my_index * conv_cols, conv_cols, axis=1).reshape(dn_conv_w.shape)
    packed = [_pack_rows([t[nm] for nm in small]) for t in (weights, mom_m, mom_v, g_small)]
    small_out = _adamw(packed[0], packed[1], packed[2], packed[3][None], name="adamw_small")
    small_shapes = [weights[nm].shape for nm in small]
    for kind, packed_out in enumerate(small_out):
        for nm, val in zip(small, _unpack_rows(packed_out, small_shapes)):
            results.setdefault(nm, [None] * 4)[kind] = val

    loss = lax.psum(loss_dev[0, 0], ("x", "y", "c"))
    outs = [loss, grad_x[None]]
    for kind in range(4):
        outs += [results[nm][kind].reshape(weights[nm].shape) for nm in names]
    return tuple(outs)
```

```python
import functools
import math

import jax
import jax.numpy as jnp
from jax import lax
from jax.experimental import pallas as pl
from jax.experimental.pallas import tpu as pltpu

F32 = jnp.float32
BF16 = jnp.bfloat16
HIGHEST = lax.Precision.HIGHEST
MESH = pl.DeviceIdType.MESH
N_DEV = 8

EPS = 1e-6
S5_GROUP = 16
S5_STATE = 64
S5_GPB = 8
S5_T = 256
DN_HEADS = 8
DN_HEAD_DIM = 128
CHUNK = 64
DN_HEADS_PER_STEP = 8
CONV_K = 4
BA_PAD = 512

ADAM_LR = 0.001
ADAM_B1 = 0.9
ADAM_B2 = 0.999
ADAM_EPS = 1e-08
ADAM_WD = 0.01
ADAM_STEP = 10

VMEM_LIMIT_BYTES = 48 * 1024 * 1024
ROW_TILE = 256


def _cparams(*sem):
    return pltpu.CompilerParams(dimension_semantics=sem if sem else None,
                                vmem_limit_bytes=VMEM_LIMIT_BYTES)


def _sigmoid(x):
    return 1.0 / (1.0 + jnp.exp(-x))


def _silu(x):
    return x * _sigmoid(x)


def _gelu(x):
    return 0.5 * x * (1.0 + jnp.tanh(0.7978845608028654 * (x + 0.044715 * x * x * x)))


def _softplus(x):
    return jnp.maximum(x, 0.0) + jnp.log(1.0 + jnp.exp(-jnp.abs(x)))


def _rmsnorm(x, w):
    return x * lax.rsqrt(jnp.mean(x * x, axis=-1, keepdims=True) + EPS) * w


def _dot(a, b, dims=((1,), (0,)), precision=None):
    return lax.dot_general(a, b, (dims, ((), ())), precision=precision,
                           preferred_element_type=F32)


def _bdot(a, b, dims=((1,), (0,))):
    return _dot(a.astype(BF16), b.astype(BF16), dims)


def _split_bf16(a):
    hi = a.astype(BF16)
    return hi, (a - hi.astype(F32)).astype(BF16)


def _dot3_dims(a, b, dims):
    ah, al = _split_bf16(a)
    bh, bl = _split_bf16(b)
    return _dot(ah, bh, dims) + (_dot(ah, bl, dims) + _dot(al, bh, dims))


@jax.custom_vjp
def _dot3(a, b):
    return _dot3_dims(a, b, ((1,), (0,)))


def _dot3_fwd(a, b):
    return _dot3(a, b), (a, b)


def _dot3_bwd(res, g):
    a, b = res
    return _dot3_dims(g, b, ((1,), (1,))), _dot3_dims(a, g, ((0,), (0,)))


_dot3.defvjp(_dot3_fwd, _dot3_bwd)


def _mm(a, b, *, ta=False, tb=False, out_dtype=F32, tm=512, tn=512, tk=None, name):
    k_dim, m_dim = (a.shape if ta else a.shape[::-1])
    n_dim = b.shape[0] if tb else b.shape[1]
    assert (b.shape[1] if tb else b.shape[0]) == k_dim
    tm, tn = min(tm, m_dim), min(tn, n_dim)
    tk = k_dim if tk is None else tk
    assert m_dim % tm == 0 and n_dim % tn == 0 and k_dim % tk == 0
    nk = k_dim // tk
    a_spec = (pl.BlockSpec((tk, tm), lambda i, j, k: (k, i)) if ta
              else pl.BlockSpec((tm, tk), lambda i, j, k: (i, k)))
    b_spec = (pl.BlockSpec((tn, tk), lambda i, j, k: (j, k)) if tb
              else pl.BlockSpec((tk, tn), lambda i, j, k: (k, j)))
    dims = ((0 if ta else 1,), (1 if tb else 0,))

    def body(a_ref, b_ref, o_ref, *scratch):
        p = _bdot(a_ref[...], b_ref[...], dims)
        if nk == 1:
            o_ref[...] = p.astype(o_ref.dtype)
        else:
            acc = scratch[0]
            k = pl.program_id(2)

            @pl.when(k == 0)
            def _():
                acc[...] = p

            @pl.when(k > 0)
            def _():
                acc[...] += p

            @pl.when(k == nk - 1)
            def _():
                o_ref[...] = acc[...].astype(o_ref.dtype)

    return pl.pallas_call(
        body, name=name,
        out_shape=jax.ShapeDtypeStruct((m_dim, n_dim), out_dtype),
        grid=(m_dim // tm, n_dim // tn, nk),
        in_specs=[a_spec, b_spec],
        out_specs=pl.BlockSpec((tm, tn), lambda i, j, k: (i, j)),
        scratch_shapes=[pltpu.VMEM((tm, tn), F32)] if nk > 1 else [],
        compiler_params=_cparams("parallel", "parallel", "arbitrary"),
    )(a, b)


def _rms_fwd(x, w):
    l, d = x.shape

    def body(x_ref, w_ref, h_ref):
        h_ref[...] = _rmsnorm(x_ref[...], w_ref[...]).astype(BF16)

    return pl.pallas_call(
        body, name="rms_fwd",
        out_shape=jax.ShapeDtypeStruct((l, d), BF16),
        grid=(l // ROW_TILE,),
        in_specs=[pl.BlockSpec((ROW_TILE, d), lambda i: (i, 0)),
                  pl.BlockSpec((1, d), lambda i: (0, 0))],
        out_specs=pl.BlockSpec((ROW_TILE, d), lambda i: (i, 0)),
        compiler_params=_cparams("parallel"),
    )(x, w)


def _rms_bwd(x, w, dh, dres):
    l, d = x.shape

    def body(x_ref, w_ref, dh_ref, dres_ref, dx_ref, dw_ref):
        _, vjp = jax.vjp(_rmsnorm, x_ref[...], w_ref[...])
        dx, dw = vjp(dh_ref[...])
        dx_ref[...] = dx + dres_ref[...]

        @pl.when(pl.program_id(0) == 0)
        def _():
            dw_ref[...] = jnp.zeros_like(dw_ref)

        dw_ref[...] += dw

    row = pl.BlockSpec((ROW_TILE, d), lambda i: (i, 0))
    vec = pl.BlockSpec((1, d), lambda i: (0, 0))
    return pl.pallas_call(
        body, name="rms_bwd",
        out_shape=(jax.ShapeDtypeStruct((l, d), F32), jax.ShapeDtypeStruct((1, d), F32)),
        grid=(l // ROW_TILE,),
        in_specs=[row, vec, row, row],
        out_specs=(row, vec),
        compiler_params=_cparams("arbitrary"),
    )(x, w, dh, dres)


def _final(x, r, fw, target):
    l, d = x.shape

    def per_row_loss(x2, w, tgt):
        err = _rmsnorm(x2, w) - tgt
        return 0.5 * jnp.mean(err * err, axis=-1, keepdims=True)

    def body(x_ref, r_ref, w_ref, t_ref, dx_ref, dxb_ref, loss_ref, dw_ref):
        x2 = x_ref[...] + r_ref[...]
        rows, vjp = jax.vjp(functools.partial(per_row_loss, tgt=t_ref[...]), x2, w_ref[...])
        dx2, dw = vjp(jnp.ones_like(rows))
        dx_ref[...] = dx2
        dxb_ref[...] = dx2.astype(BF16)

        @pl.when(pl.program_id(0) == 0)
        def _():
            dw_ref[...] = jnp.zeros_like(dw_ref)
            loss_ref[...] = jnp.zeros_like(loss_ref)

        dw_ref[...] += dw
        loss_ref[...] += jnp.sum(rows, axis=0, keepdims=True)

    row = pl.BlockSpec((ROW_TILE, d), lambda i: (i, 0))
    vec = pl.BlockSpec((1, d), lambda i: (0, 0))
    return pl.pallas_call(
        body, name="final_norm_loss",
        out_shape=(jax.ShapeDtypeStruct((l, d), F32), jax.ShapeDtypeStruct((l, d), BF16),
                   jax.ShapeDtypeStruct((1, 1), F32), jax.ShapeDtypeStruct((1, d), F32)),
        grid=(l // ROW_TILE,),
        in_specs=[row, row, vec, row],
        out_specs=(row, row, pl.BlockSpec((1, 1), lambda i: (0, 0)), vec),
        compiler_params=_cparams("arbitrary"),
    )(x, r, fw, target)


def _merge_fn(gs, gd, ys, yd):
    return _sigmoid(gs) * ys + _sigmoid(gd) * yd


def _merge_fwd(proj, off_gs, off_gd, ys, yd):
    l, d = ys.shape
    cw = 512
    blk = lambda off: pl.BlockSpec((ROW_TILE, cw), lambda i, j: (i, off // cw + j))

    def body(gs_ref, gd_ref, ys_ref, yd_ref, o_ref):
        o_ref[...] = _merge_fn(gs_ref[...], gd_ref[...], ys_ref[...], yd_ref[...]).astype(BF16)

    return pl.pallas_call(
        body, name="merge_fwd",
        out_shape=jax.ShapeDtypeStruct((l, d), BF16),
        grid=(l // ROW_TILE, d // cw),
        in_specs=[blk(off_gs), blk(off_gd), blk(0), blk(0)],
        out_specs=blk(0),
        compiler_params=_cparams("parallel", "parallel"),
    )(proj, proj, ys, yd)


def _merge_bwd(proj, off_gs, off_gd, ys, yd, dmixed):
    l, d = ys.shape
    cw = 512
    blk = lambda off: pl.BlockSpec((ROW_TILE, cw), lambda i, j: (i, off // cw + j))

    def body(gs_ref, gd_ref, ys_ref, yd_ref, dm_ref, dgs_ref, dgd_ref, dys_ref, dyd_ref):
        _, vjp = jax.vjp(_merge_fn, gs_ref[...], gd_ref[...], ys_ref[...], yd_ref[...])
        dgs, dgd, dys, dyd = vjp(dm_ref[...])
        dgs_ref[...] = dgs.astype(BF16)
        dgd_ref[...] = dgd.astype(BF16)
        dys_ref[...] = dys.astype(BF16)
        dyd_ref[...] = dyd.astype(BF16)

    out = jax.ShapeDtypeStruct((l, d), BF16)
    return pl.pallas_call(
        body, name="merge_bwd",
        out_shape=(out, out, out, out),
        grid=(l // ROW_TILE, d // cw),
        in_specs=[blk(off_gs), blk(off_gd), blk(0), blk(0), blk(0)],
        out_specs=(blk(0), blk(0), blk(0), blk(0)),
        compiler_params=_cparams("parallel", "parallel"),
    )(proj, proj, ys, yd, dmixed)


def _s5_disc_fn(lam_re, lam_im, log_step):
    step = jnp.exp(log_step)
    mag = jnp.exp(lam_re * step)
    abar_re = mag * jnp.cos(lam_im * step)
    abar_im = mag * jnp.sin(lam_im * step)
    den = lam_re * lam_re + lam_im * lam_im
    xr = abar_re - 1.0
    f_re = (xr * lam_re + abar_im * lam_im) / den
    f_im = (abar_im * lam_re - xr * lam_im) / den
    return abar_re, abar_im, f_re, f_im


def _s5_disc_fwd(lam_re, lam_im, log_step):
    g, p = lam_re.shape

    def body(lr_ref, li_ref, ls_ref, ar_ref, ai_ref, fr_ref, fi_ref):
        ar, ai, fr, fi = _s5_disc_fn(lr_ref[...], li_ref[...], ls_ref[...])
        ar_ref[...] = ar
        ai_ref[...] = ai
        fr_ref[...] = fr
        fi_ref[...] = fi

    o = jax.ShapeDtypeStruct((g, p), F32)
    return pl.pallas_call(body, name="s5_disc_fwd", out_shape=(o, o, o, o),
                          compiler_params=_cparams())(lam_re, lam_im, log_step)


def _s5_disc_bwd(lam_re, lam_im, log_step, dar, dai, dfr, dfi):
    g, p = lam_re.shape

    def body(lr_ref, li_ref, ls_ref, dar_ref, dai_ref, dfr_ref, dfi_ref, dlr_ref, dli_ref, dls_ref):
        _, vjp = jax.vjp(_s5_disc_fn, lr_ref[...], li_ref[...], ls_ref[...])
        dlr, dli, dls = vjp((dar_ref[...], dai_ref[...], dfr_ref[...], dfi_ref[...]))
        dlr_ref[...] = dlr
        dli_ref[...] = dli
        dls_ref[...] = dls

    o = jax.ShapeDtypeStruct((g, p), F32)
    return pl.pallas_call(body, name="s5_disc_bwd",
                          out_shape=(o, o, jax.ShapeDtypeStruct((g, 1), F32)),
                          compiler_params=_cparams())(lam_re, lam_im, log_step, dar, dai, dfr, dfi)


def _s5_bbar_fwd(f_re, f_im, b_re, b_im):
    n, c = b_re.shape

    def body(fr_ref, fi_ref, br_ref, bi_ref, or_ref, oi_ref):
        fr, fi, br, bi = fr_ref[...], fi_ref[...], br_ref[...], bi_ref[...]
        or_ref[...] = fr * br - fi * bi
        oi_ref[...] = fr * bi + fi * br

    o = jax.ShapeDtypeStruct((n, c), F32)
    return pl.pallas_call(body, name="s5_bbar_fwd", out_shape=(o, o),
                          compiler_params=_cparams())(f_re, f_im, b_re, b_im)


def _s5_bbar_bwd(f_re, f_im, b_re, b_im, dbr, dbi):
    n, c = b_re.shape

    def body(fr_ref, fi_ref, br_ref, bi_ref, dor_ref, doi_ref, dfr_ref, dfi_ref, dbr_ref, dbi_ref):
        fr, fi, br, bi = fr_ref[...], fi_ref[...], br_ref[...], bi_ref[...]
        dor, doi = dor_ref[...], doi_ref[...]
        dfr_ref[...] = jnp.sum(dor * br + doi * bi, axis=-1, keepdims=True)
        dfi_ref[...] = jnp.sum(doi * br - dor * bi, axis=-1, keepdims=True)
        dbr_ref[...] = fr * dor + fi * doi
        dbi_ref[...] = fr * doi - fi * dor

    col = jax.ShapeDtypeStruct((n, 1), F32)
    o = jax.ShapeDtypeStruct((n, c), F32)
    return pl.pallas_call(body, name="s5_bbar_bwd", out_shape=(col, col, o, o),
                          compiler_params=_cparams())(f_re, f_im, b_re, b_im, dbr, dbi)


SUBLANES = 8


def _scan_groups(xr, xi, ar, ai, reverse):
    t = xr.shape[0]
    sub = lax.broadcasted_iota(jnp.int32, (t, 1), 0) & (SUBLANES - 1)
    pr, pi = ar, ai
    for sh in (1, 2, 4):
        if reverse:
            keep = sub < SUBLANES - sh
            sr, si = pltpu.roll(xr, t - sh, 0), pltpu.roll(xi, t - sh, 0)
        else:
            keep = sub >= sh
            sr, si = pltpu.roll(xr, sh, 0), pltpu.roll(xi, sh, 0)
        sr = jnp.where(keep, sr, 0.0)
        si = jnp.where(keep, si, 0.0)
        xr, xi = xr + pr * sr - pi * si, xi + pr * si + pi * sr
        pr, pi = pr * pr - pi * pi, 2.0 * pr * pi
    return xr, xi


def _scan_rows(xr, xi, ar, ai, cr, ci, sr_ref, si_ref, reverse):
    t, n = xr.shape
    xr, xi = _scan_groups(xr, xi, ar, ai, reverse)
    sr_ref[...] = xr
    si_ref[...] = xi
    sub = lax.broadcasted_iota(jnp.int32, (SUBLANES, n), 0)
    seed = sub == (SUBLANES - 1 if reverse else 0)
    pwr, pwi = _scan_groups(jnp.where(seed, ar, 0.0), jnp.where(seed, ai, 0.0), ar, ai, reverse)
    groups = range(t // SUBLANES)
    edge = 0 if reverse else SUBLANES - 1
    for g in (reversed(groups) if reverse else groups):
        rows = slice(g * SUBLANES, (g + 1) * SUBLANES)
        vr = sr_ref[rows, :] + (pwr * cr - pwi * ci)
        vi = si_ref[rows, :] + (pwr * ci + pwi * cr)
        sr_ref[rows, :] = vr
        si_ref[rows, :] = vi
        cr, ci = vr[edge:edge + 1, :], vi[edge:edge + 1, :]
    return cr, ci


def _s5_states(u_bf, bbr, bbi, ar, ai, cr, ci, sr_ref, si_ref):
    return _scan_rows(_dot(u_bf, bbr), _dot(u_bf, bbi), ar, ai, cr, ci, sr_ref, si_ref, reverse=False)


def _s5_fwd(proj, bbr, bbi, a_re, a_im, ctr, cti, d_skip, d_s5):
    l = proj.shape[0]
    nb, uc, ns = bbr.shape
    t = min(S5_T, l)
    nt = l // t

    def body(u_ref, bbr_ref, bbi_ref, ar_ref, ai_ref, ctr_ref, cti_ref, d_ref,
             y_ref, car_r_ref, car_i_ref, cr, ci, sr_ref, si_ref):
        @pl.when(pl.program_id(1) == 0)
        def _():
            cr[...] = jnp.zeros_like(cr)
            ci[...] = jnp.zeros_like(ci)

        car_r_ref[...] = cr[...]
        car_i_ref[...] = ci[...]
        u = u_ref[...]
        cr[...], ci[...] = _s5_states(u.astype(BF16), bbr_ref[...], bbi_ref[...], ar_ref[...],
                                      ai_ref[...], cr[...], ci[...], sr_ref, si_ref)
        y_ref[...] = (_bdot(sr_ref[...], ctr_ref[...]) - _bdot(si_ref[...], cti_ref[...])
                      + d_ref[...] * u)

    per_block = lambda shape: pl.BlockSpec((None,) + shape, lambda b, n: (b, 0, 0))
    return pl.pallas_call(
        body, name="s5_fwd",
        out_shape=(jax.ShapeDtypeStruct((l, d_s5), F32),
                   jax.ShapeDtypeStruct((nt, 1, nb * ns), F32),
                   jax.ShapeDtypeStruct((nt, 1, nb * ns), F32)),
        grid=(nb, nt),
        in_specs=[pl.BlockSpec((t, uc), lambda b, n: (n, b)),
                  per_block((uc, ns)), per_block((uc, ns)),
                  per_block((1, ns)), per_block((1, ns)),
                  per_block((ns, uc)), per_block((ns, uc)),
                  pl.BlockSpec((1, uc), lambda b, n: (0, b))],
        out_specs=(pl.BlockSpec((t, uc), lambda b, n: (n, b)),
                   pl.BlockSpec((None, 1, ns), lambda b, n: (n, 0, b)),
                   pl.BlockSpec((None, 1, ns), lambda b, n: (n, 0, b))),
        scratch_shapes=[pltpu.VMEM((1, ns), F32), pltpu.VMEM((1, ns), F32),
                        pltpu.VMEM((t, ns), F32), pltpu.VMEM((t, ns), F32)],
        compiler_params=_cparams("parallel", "arbitrary"),
    )(proj, bbr, bbi, a_re, a_im, ctr, cti, d_skip)


def _s5_bwd(proj, dy, bbr, bbi, a_re, a_im, cbr, cbi, d_skip, car_r, car_i):
    l, d_s5 = dy.shape
    nb, uc, ns = bbr.shape
    t = min(S5_T, l)
    nt = l // t

    def body(u_ref, dy_ref, bbr_ref, bbi_ref, ar_ref, ai_ref, cbr_ref, cbi_ref, d_ref,
             car_r_ref, car_i_ref,
             du_ref, dar_ref, dai_ref, dbbr_ref, dbbi_ref, dcbr_ref, dcbi_ref, dd_ref, gcr, gci,
             sr_ref, si_ref, gr_ref, gi_ref):
        @pl.when(pl.program_id(1) == 0)
        def _():
            gcr[...] = jnp.zeros_like(gcr)
            gci[...] = jnp.zeros_like(gci)
            for ref in (dar_ref, dai_ref, dbbr_ref, dbbi_ref, dcbr_ref, dcbi_ref, dd_ref):
                ref[...] = jnp.zeros_like(ref)

        row = lax.broadcasted_iota(jnp.int32, (t, 1), 0)
        u, dy = u_ref[...], dy_ref[...]
        u_bf, dy_bf = u.astype(BF16), dy.astype(BF16)
        ar, ai = ar_ref[...], ai_ref[...]
        cr, ci = car_r_ref[...], car_i_ref[...]
        _s5_states(u_bf, bbr_ref[...], bbi_ref[...], ar, ai, cr, ci, sr_ref, si_ref)
        sr, si = sr_ref[...], si_ref[...]
        first = row == 0
        pr = jnp.where(first, cr, pltpu.roll(sr, 1, 0))
        pi = jnp.where(first, ci, pltpu.roll(si, 1, 0))
        gcr[...], gci[...] = _scan_rows(_dot(dy_bf, cbr_ref[...]), -_dot(dy_bf, cbi_ref[...]), ar, -ai,
                                        gcr[...], gci[...], gr_ref, gi_ref, reverse=True)
        gr, gi = gr_ref[...], gi_ref[...]
        dar_ref[...] += jnp.sum(gr * pr + gi * pi, axis=0, keepdims=True)
        dai_ref[...] += jnp.sum(gi * pr - gr * pi, axis=0, keepdims=True)
        gr_bf, gi_bf = gr.astype(BF16), gi.astype(BF16)
        tn = ((0,), (0,))
        dbbr_ref[...] += _dot(u_bf, gr_bf, tn)
        dbbi_ref[...] += _dot(u_bf, gi_bf, tn)
        dcbr_ref[...] += _dot(dy_bf, sr.astype(BF16), tn)
        dcbi_ref[...] -= _dot(dy_bf, si.astype(BF16), tn)
        nt_dims = ((1,), (1,))
        du = _dot(gr_bf, bbr_ref[...], nt_dims) + _dot(gi_bf, bbi_ref[...], nt_dims) + dy * d_ref[...]
        du_ref[...] = du.astype(BF16)
        dd_ref[...] += jnp.sum(dy * u, axis=0, keepdims=True)

    rev = lambda n: nt - 1 - n
    per_block = lambda shape: pl.BlockSpec((None,) + shape, lambda b, n: (b, 0, 0))
    acc = jax.ShapeDtypeStruct((nb, uc, ns), F32)
    vec = jax.ShapeDtypeStruct((nb, 1, ns), F32)
    return pl.pallas_call(
        body, name="s5_bwd",
        out_shape=(jax.ShapeDtypeStruct((l, d_s5), BF16), vec, vec, acc, acc, acc, acc,
                   jax.ShapeDtypeStruct((1, d_s5), F32)),
        grid=(nb, nt),
        in_specs=[pl.BlockSpec((t, uc), lambda b, n: (rev(n), b)),
                  pl.BlockSpec((t, uc), lambda b, n: (rev(n), b)),
                  per_block((uc, ns)), per_block((uc, ns)),
                  per_block((1, ns)), per_block((1, ns)),
                  per_block((uc, ns)), per_block((uc, ns)),
                  pl.BlockSpec((1, uc), lambda b, n: (0, b)),
                  pl.BlockSpec((None, 1, ns), lambda b, n: (rev(n), 0, b)),
                  pl.BlockSpec((None, 1, ns), lambda b, n: (rev(n), 0, b))],
        out_specs=(pl.BlockSpec((t, uc), lambda b, n: (rev(n), b)),
                   per_block((1, ns)), per_block((1, ns)),
                   per_block((uc, ns)), per_block((uc, ns)),
                   per_block((uc, ns)), per_block((uc, ns)),
                   pl.BlockSpec((1, uc), lambda b, n: (0, b))),
        scratch_shapes=[pltpu.VMEM((1, ns), F32), pltpu.VMEM((1, ns), F32)]
        + [pltpu.VMEM((t, ns), F32)] * 4,
        compiler_params=_cparams("parallel", "arbitrary"),
    )(proj, dy, bbr, bbi, a_re, a_im, cbr, cbi, d_skip, car_r, car_i)


def _s5_glu_fwd(y1, proj, off_z, wglu):
    l, d = y1.shape

    def body(y_ref, z_ref, w_ref, o_ref):
        y2 = _gelu(y_ref[...])
        y3 = y2 * _sigmoid(_bdot(y2, w_ref[...]))
        o_ref[...] = (y3 * _silu(z_ref[...])).astype(BF16)

    return pl.pallas_call(
        body, name="s5_glu_fwd",
        out_shape=jax.ShapeDtypeStruct((l, d), BF16),
        grid=(l // ROW_TILE,),
        in_specs=[pl.BlockSpec((ROW_TILE, d), lambda i: (i, 0)),
                  pl.BlockSpec((ROW_TILE, d), lambda i: (i, off_z // d)),
                  pl.BlockSpec((d, d), lambda i: (0, 0))],
        out_specs=pl.BlockSpec((ROW_TILE, d), lambda i: (i, 0)),
        compiler_params=_cparams("parallel"),
    )(y1, proj, wglu)


def _s5_glu_bwd(y1, proj, off_z, wglu, dout):
    l, d = y1.shape

    def body(y_ref, z_ref, w_ref, do_ref, dy_ref, dz_ref, dw_ref):
        y2, gelu_vjp = jax.vjp(_gelu, y_ref[...])
        z = z_ref[...]
        sz, silu_vjp = jax.vjp(_silu, z)
        y2_bf = y2.astype(BF16)
        sg = _sigmoid(_dot(y2_bf, w_ref[...]))
        dout = do_ref[...]
        dy3 = dout * sz
        dz_ref[...] = silu_vjp(dout * (y2 * sg))[0].astype(BF16)
        dgl = (dy3 * y2 * sg * (1.0 - sg)).astype(BF16)
        dy2 = dy3 * sg + _dot(dgl, w_ref[...], ((1,), (1,)))
        dy_ref[...] = gelu_vjp(dy2)[0]

        @pl.when(pl.program_id(0) == 0)
        def _():
            dw_ref[...] = jnp.zeros_like(dw_ref)

        dw_ref[...] += _dot(y2_bf, dgl, ((0,), (0,)))

    row = pl.BlockSpec((ROW_TILE, d), lambda i: (i, 0))
    full = pl.BlockSpec((d, d), lambda i: (0, 0))
    return pl.pallas_call(
        body, name="s5_glu_bwd",
        out_shape=(jax.ShapeDtypeStruct((l, d), F32), jax.ShapeDtypeStruct((l, d), BF16),
                   jax.ShapeDtypeStruct((d, d), F32)),
        grid=(l // ROW_TILE,),
        in_specs=[row, pl.BlockSpec((ROW_TILE, d), lambda i: (i, off_z // d)), full, row],
        out_specs=(row, row, full),
        compiler_params=_cparams("arbitrary"),
    )(y1, proj, wglu, dout)


def _shift_rows(x, k, back=False):
    if k == 0:
        return x
    t = x.shape[0]
    row = lax.broadcasted_iota(jnp.int32, (t, 1), 0)
    if back:
        return jnp.where(row < t - k, pltpu.roll(x, t - k, 0), 0.0)
    return jnp.where(row >= k, pltpu.roll(x, k, 0), 0.0)


def _dn_conv(x, w_ref):
    return sum(w_ref[CONV_K - 1 - k:CONV_K - k, :] * _shift_rows(x, k) for k in range(CONV_K))


def _dn_post_conv(c, j):
    y = _silu(c)
    n = y * lax.rsqrt(jnp.sum(y * y, axis=-1, keepdims=True) + EPS)
    n = n * jnp.where(j < DN_HEADS, DN_HEAD_DIM ** -0.5, 1.0)
    return jnp.where(j < 2 * DN_HEADS, n, y)


def _dn_prep_fwd(proj, off_qkv, conv_w):
    l = proj.shape[0]
    hd = DN_HEAD_DIM
    nblk = 3 * DN_HEADS

    def body(x_ref, w_ref, o_ref):
        o_ref[...] = _dn_post_conv(_dn_conv(x_ref[...], w_ref), pl.program_id(0))

    return pl.pallas_call(
        body, name="dn_prep_fwd",
        out_shape=jax.ShapeDtypeStruct((l, nblk * hd), F32),
        grid=(nblk,),
        in_specs=[pl.BlockSpec((l, hd), lambda j: (0, off_qkv // hd + j)),
                  pl.BlockSpec((CONV_K, hd), lambda j: (0, j))],
        out_specs=pl.BlockSpec((l, hd), lambda j: (0, j)),
        compiler_params=_cparams("parallel"),
    )(proj, conv_w)


def _dn_prep_bwd(proj, off_qkv, conv_w, dqkv):
    l = proj.shape[0]
    hd = DN_HEAD_DIM
    nblk = 3 * DN_HEADS

    def body(x_ref, w_ref, do_ref, dx_ref, dw_ref):
        x = x_ref[...]
        j = pl.program_id(0)
        _, vjp = jax.vjp(functools.partial(_dn_post_conv, j=j), _dn_conv(x, w_ref))
        dc = vjp(do_ref[...])[0]
        dx = sum(w_ref[CONV_K - 1 - k:CONV_K - k, :] * _shift_rows(dc, k, back=True)
                 for k in range(CONV_K))
        dx_ref[...] = dx.astype(BF16)
        for k in range(CONV_K):
            dw_ref[CONV_K - 1 - k:CONV_K - k, :] = jnp.sum(dc * _shift_rows(x, k), axis=0,
                                                           keepdims=True)

    return pl.pallas_call(
        body, name="dn_prep_bwd",
        out_shape=(jax.ShapeDtypeStruct((l, nblk * hd), BF16),
                   jax.ShapeDtypeStruct((CONV_K, nblk * hd), F32)),
        grid=(nblk,),
        in_specs=[pl.BlockSpec((l, hd), lambda j: (0, off_qkv // hd + j)),
                  pl.BlockSpec((CONV_K, hd), lambda j: (0, j)),
                  pl.BlockSpec((None, l, hd), lambda j: (j // DN_HEADS, 0, j % DN_HEADS))],
        out_specs=(pl.BlockSpec((l, hd), lambda j: (0, j)),
                   pl.BlockSpec((CONV_K, hd), lambda j: (0, j))),
        compiler_params=_cparams("parallel"),
    )(proj, conv_w, dqkv)


def _dn_gate_fn(ba, a_log_row, dt_row):
    lane = lax.broadcasted_iota(jnp.int32, ba.shape, 1)
    beta = _sigmoid(ba)
    g = -jnp.exp(a_log_row) * _softplus(ba + dt_row)
    return jnp.where(lane < DN_HEADS, beta, jnp.where(lane < 2 * DN_HEADS, g, 0.0))


def _dn_gates_fwd(proj, off_ba, a_log_row, dt_row):
    l = proj.shape[0]
    row = pl.BlockSpec((ROW_TILE, 128), lambda i: (i, off_ba // 128))
    vec = pl.BlockSpec((1, 128), lambda i: (0, 0))

    def body(ba_ref, al_ref, dt_ref, o_ref):
        o_ref[...] = _dn_gate_fn(ba_ref[...], al_ref[...], dt_ref[...])

    return pl.pallas_call(
        body, name="dn_gates_fwd",
        out_shape=jax.ShapeDtypeStruct((l, 128), F32),
        grid=(l // ROW_TILE,),
        in_specs=[row, vec, vec],
        out_specs=pl.BlockSpec((ROW_TILE, 128), lambda i: (i, 0)),
        compiler_params=_cparams("parallel"),
    )(proj, a_log_row, dt_row)


def _dn_gates_bwd(proj, off_ba, a_log_row, dt_row, dgb_heads):
    l = proj.shape[0]
    nh = dgb_heads.shape[0]
    row = pl.BlockSpec((ROW_TILE, 128), lambda i: (i, off_ba // 128))
    vec = pl.BlockSpec((1, 128), lambda i: (0, 0))

    def body(ba_ref, al_ref, dt_ref, dg_ref, dba_ref, dal_ref, ddt_ref):
        _, vjp = jax.vjp(_dn_gate_fn, ba_ref[...], al_ref[...], dt_ref[...])
        dgb = dg_ref[0]
        for h in range(1, nh):
            dgb = dgb + dg_ref[h]
        dba, dal, ddt = vjp(dgb)
        dba_ref[...] = dba.astype(BF16)

        @pl.when(pl.program_id(0) == 0)
        def _():
            dal_ref[...] = jnp.zeros_like(dal_ref)
            ddt_ref[...] = jnp.zeros_like(ddt_ref)

        dal_ref[...] += dal
        ddt_ref[...] += ddt

    return pl.pallas_call(
        body, name="dn_gates_bwd",
        out_shape=(jax.ShapeDtypeStruct((l, 128), BF16), jax.ShapeDtypeStruct((1, 128), F32),
                   jax.ShapeDtypeStruct((1, 128), F32)),
        grid=(l // ROW_TILE,),
        in_specs=[row, vec, vec, pl.BlockSpec((nh, ROW_TILE, 128), lambda i: (0, i, 0))],
        out_specs=(pl.BlockSpec((ROW_TILE, 128), lambda i: (i, 0)), vec, vec),
        compiler_params=_cparams("arbitrary"),
    )(proj, a_log_row, dt_row, dgb_heads)


def _dn_chunk_fn(states, qs, ks, vs, gb, heads):
    c = qs[0].shape[0]
    each = lambda f, *lists: [f(*args) for args in zip(*lists)]
    lane = lax.broadcasted_iota(jnp.int32, gb.shape, 1)
    ri = lax.broadcasted_iota(jnp.int32, (c, c), 0)
    ci = lax.broadcasted_iota(jnp.int32, (c, c), 1)
    causal, strict = ri >= ci, ri > ci
    eye = (ri == ci).astype(F32)
    rowi = lax.broadcasted_iota(jnp.int32, (c, 1), 0)
    nt_dims = ((1,), (1,))
    hdot = functools.partial(_dot, precision=HIGHEST)

    pick = lambda m, at: jnp.sum(jnp.where(lane == at, m, 0.0), axis=1, keepdims=True)
    gb_cum = hdot(causal.astype(F32), gb)
    beta = [pick(gb, h) for h in heads]
    gc = [pick(gb_cum, h + DN_HEADS) for h in heads]
    gc_row = each(lambda g: jnp.sum(eye * g, axis=0, keepdims=True), gc)
    decay = each(lambda g, gr: jnp.where(causal, jnp.exp(jnp.where(causal, g - gr, 0.0)), 0.0),
                 gc, gc_row)
    kk = each(lambda k: _bdot(k, k, nt_dims), ks)
    a_mat = each(lambda b, m, dc: jnp.where(strict, b * m * dc, 0.0), beta, kk, decay)

    t_inv = each(lambda a: eye - a, a_mat)
    power = a_mat
    for _ in range(int(math.log2(c)) - 1):
        power = each(lambda p: _dot3(p, p), power)
        t_inv = each(lambda t, p: t + _dot3(t, p), t_inv, power)

    egc = each(jnp.exp, gc)
    u_c = each(lambda t, v, b: _dot3(t, v * b), t_inv, vs, beta)
    w_c = each(lambda t, k, b, e: _dot3(t, k * (b * e)), t_inv, ks, beta, egc)
    qk = each(lambda q, k, dc: _bdot(q, k, nt_dims) * dc, qs, ks, decay)
    g_end = each(lambda g: jnp.sum(jnp.where(rowi == c - 1, g, 0.0), axis=0, keepdims=True), gc)
    v_new = each(lambda u, w, s: u - _bdot(w, s), u_c, w_c, states)
    o = each(lambda q, e, s, m, vn: _bdot(q * e, s) + _bdot(m, vn), qs, egc, states, qk, v_new)
    new_states = each(
        lambda s, ge, k, g, vn: s * jnp.exp(ge) + _bdot(k * jnp.exp(ge - g), vn, ((0,), (0,))),
        states, g_end, ks, gc, v_new)
    return o, new_states


def _dn_chunk_specs(order):
    hd, nh, hps = DN_HEAD_DIM, DN_HEADS, DN_HEADS_PER_STEP
    qkv = lambda part: pl.BlockSpec((CHUNK, hps * hd), lambda h, n: (order(n), part * (nh // hps) + h))
    gb = pl.BlockSpec((CHUNK, 128), lambda h, n: (order(n), 0))
    state = pl.BlockSpec((hps, None, hd, hd), lambda h, n: (h, order(n), 0, 0))
    return qkv, gb, state


def _dn_chunk_fwd(qkv, gb):
    l = qkv.shape[0]
    hd, nh, hps = DN_HEAD_DIM, DN_HEADS, DN_HEADS_PER_STEP
    n_chunks = l // CHUNK
    qkv_spec, gb_spec, state_spec = _dn_chunk_specs(lambda n: n)

    def body(q_ref, k_ref, v_ref, gb_ref, o_ref, s_ref, state):
        @pl.when(pl.program_id(1) == 0)
        def _():
            state[...] = jnp.zeros_like(state)

        cols = [slice(i * hd, (i + 1) * hd) for i in range(hps)]
        states = [state[i] for i in range(hps)]
        for i in range(hps):
            s_ref[i] = states[i]
        o, new_states = _dn_chunk_fn(
            states, [q_ref[:, cs] for cs in cols], [k_ref[:, cs] for cs in cols],
            [v_ref[:, cs] for cs in cols], gb_ref[...],
            [pl.program_id(0) * hps + i for i in range(hps)])
        for i in range(hps):
            o_ref[:, cols[i]] = o[i]
            state[i] = new_states[i]

    return pl.pallas_call(
        body, name="dn_chunk_fwd",
        out_shape=(jax.ShapeDtypeStruct((l, nh * hd), F32),
                   jax.ShapeDtypeStruct((nh, n_chunks, hd, hd), F32)),
        grid=(nh // hps, n_chunks),
        in_specs=[qkv_spec(0), qkv_spec(1), qkv_spec(2), gb_spec],
        out_specs=(pl.BlockSpec((CHUNK, hps * hd), lambda h, n: (n, h)), state_spec),
        scratch_shapes=[pltpu.VMEM((hps, hd, hd), F32)],
        compiler_params=_cparams("parallel", "arbitrary"),
    )(qkv, qkv, qkv, gb)


def _dn_chunk_bwd(qkv, gb, states, do):
    l = qkv.shape[0]
    hd, nh, hps = DN_HEAD_DIM, DN_HEADS, DN_HEADS_PER_STEP
    n_chunks = l // CHUNK
    rev = lambda n: n_chunks - 1 - n
    qkv_spec, gb_spec, state_spec = _dn_chunk_specs(rev)

    def body(q_ref, k_ref, v_ref, gb_ref, s_ref, do_ref, dqkv_ref, dgb_ref, dstate):
        @pl.when(pl.program_id(1) == 0)
        def _():
            dstate[...] = jnp.zeros_like(dstate)

        cols = [slice(i * hd, (i + 1) * hd) for i in range(hps)]
        fn = functools.partial(_dn_chunk_fn, heads=[pl.program_id(0) * hps + i for i in range(hps)])
        _, vjp = jax.vjp(fn, [s_ref[i] for i in range(hps)], [q_ref[:, cs] for cs in cols],
                         [k_ref[:, cs] for cs in cols], [v_ref[:, cs] for cs in cols], gb_ref[...])
        ds, dq, dk, dv, dgb = vjp(([do_ref[:, cs] for cs in cols], [dstate[i] for i in range(hps)]))
        for i in range(hps):
            dstate[i] = ds[i]
            dqkv_ref[0, :, cols[i]] = dq[i]
            dqkv_ref[1, :, cols[i]] = dk[i]
            dqkv_ref[2, :, cols[i]] = dv[i]
        dgb_ref[...] = dgb

    head_out = pl.BlockSpec((CHUNK, hps * hd), lambda h, n: (rev(n), h))
    return pl.pallas_call(
        body, name="dn_chunk_bwd",
        out_shape=(jax.ShapeDtypeStruct((3, l, nh * hd), F32),
                   jax.ShapeDtypeStruct((nh // hps, l, 128), F32)),
        grid=(nh // hps, n_chunks),
        in_specs=[qkv_spec(0), qkv_spec(1), qkv_spec(2), gb_spec, state_spec, head_out],
        out_specs=(pl.BlockSpec((3, CHUNK, hps * hd), lambda h, n: (0, rev(n), h)),
                   pl.BlockSpec((None, CHUNK, 128), lambda h, n: (h, rev(n), 0))),
        scratch_shapes=[pltpu.VMEM((hps, hd, hd), F32)],
        compiler_params=_cparams("parallel", "arbitrary"),
    )(qkv, qkv, qkv, gb, states, do)


def _dn_out_fn(o, z, w):
    return _rmsnorm(o, w) * _silu(z)


def _dn_out_fwd(o, proj, off_z, w):
    l, d = o.shape
    hd = DN_HEAD_DIM
    blk = lambda off: pl.BlockSpec((ROW_TILE, hd), lambda i, h: (i, off // hd + h))

    def body(o_ref, z_ref, w_ref, out_ref):
        out_ref[...] = _dn_out_fn(o_ref[...], z_ref[...], w_ref[...]).astype(BF16)

    return pl.pallas_call(
        body, name="dn_out_fwd",
        out_shape=jax.ShapeDtypeStruct((l, d), BF16),
        grid=(l // ROW_TILE, d // hd),
        in_specs=[blk(0), blk(off_z), pl.BlockSpec((1, hd), lambda i, h: (0, 0))],
        out_specs=blk(0),
        compiler_params=_cparams("parallel", "parallel"),
    )(o, proj, w)


def _dn_out_bwd(o, proj, off_z, w, dout):
    l, d = o.shape
    hd = DN_HEAD_DIM
    blk = lambda off: pl.BlockSpec((ROW_TILE, hd), lambda i, h: (i, off // hd + h))
    vec = pl.BlockSpec((1, hd), lambda i, h: (0, 0))

    def body(o_ref, z_ref, w_ref, dout_ref, do_ref, dz_ref, dw_ref):
        _, vjp = jax.vjp(_dn_out_fn, o_ref[...], z_ref[...], w_ref[...])
        do, dz, dw = vjp(dout_ref[...])
        do_ref[...] = do
        dz_ref[...] = dz.astype(BF16)

        @pl.when((pl.program_id(0) == 0) & (pl.program_id(1) == 0))
        def _():
            dw_ref[...] = jnp.zeros_like(dw_ref)

        dw_ref[...] += dw

    return pl.pallas_call(
        body, name="dn_out_bwd",
        out_shape=(jax.ShapeDtypeStruct((l, d), F32), jax.ShapeDtypeStruct((l, d), BF16),
                   jax.ShapeDtypeStruct((1, hd), F32)),
        grid=(l // ROW_TILE, d // hd),
        in_specs=[blk(0), blk(off_z), vec, blk(0)],
        out_specs=(blk(0), blk(0), vec),
        compiler_params=_cparams("arbitrary", "arbitrary"),
    )(o, proj, w, dout)


def _tile_2d(rows, cols, budget_bytes=1 << 20):
    for tr in (rows, 4096, 2048, 1024, 512, 256, 128, 64, 32, 16):
        if tr <= rows and rows % tr == 0 and tr * cols * 4 <= budget_bytes:
            return tr, cols
    for tc in (2048, 1024, 512, 256, 128):
        if cols % tc == 0 and rows * tc * 4 <= 2 * budget_bytes:
            return rows, tc
    raise ValueError((rows, cols))


def _adamw(w, m, v, gslots, name):
    rows, cols = w.shape
    ns = gslots.shape[0]
    tr, tc = _tile_2d(rows, cols)
    c1 = 1.0 / (1.0 - ADAM_B1 ** ADAM_STEP)
    c2 = 1.0 / (1.0 - ADAM_B2 ** ADAM_STEP)

    def body(w_ref, m_ref, v_ref, g_ref, go_ref, d_ref, mo_ref, vo_ref):
        g = g_ref[0].astype(F32)
        for s in range(1, ns):
            g = g + g_ref[s].astype(F32)
        m_new = ADAM_B1 * m_ref[...] + (1.0 - ADAM_B1) * g
        v_new = ADAM_B2 * v_ref[...] + (1.0 - ADAM_B2) * (g * g)
        go_ref[...] = g
        mo_ref[...] = m_new
        vo_ref[...] = v_new
        d_ref[...] = -ADAM_LR * ((m_new * c1) / (jnp.sqrt(v_new * c2) + ADAM_EPS) + ADAM_WD * w_ref[...])

    blk = pl.BlockSpec((tr, tc), lambda i, j: (i, j))
    o = jax.ShapeDtypeStruct((rows, cols), F32)
    return pl.pallas_call(
        body, name=name, out_shape=(o, o, o, o),
        grid=(rows // tr, cols // tc),
        in_specs=[blk, blk, blk, pl.BlockSpec((ns, tr, tc), lambda i, j: (0, i, j))],
        out_specs=(blk, blk, blk, blk),
        compiler_params=_cparams("parallel", "parallel"),
    )(w, m, v, gslots)


def _slot_sum(gslots, name):
    ns, rows, cols = gslots.shape
    tr, tc = _tile_2d(rows, cols)

    def body(g_ref, o_ref):
        g = g_ref[0]
        for s in range(1, ns):
            g = g + g_ref[s]
        o_ref[...] = g

    return pl.pallas_call(
        body, name=name, out_shape=jax.ShapeDtypeStruct((rows, cols), F32),
        grid=(rows // tr, cols // tc),
        in_specs=[pl.BlockSpec((ns, tr, tc), lambda i, j: (0, i, j))],
        out_specs=pl.BlockSpec((tr, tc), lambda i, j: (i, j)),
        compiler_params=_cparams("parallel", "parallel"),
    )(gslots)


HBM_SPEC = pl.BlockSpec(memory_space=pl.ANY)


def _all_gather(arrs, name):
    n = len(arrs)

    def body(*refs):
        ins, outs = refs[:n], refs[n:2 * n]
        send_sems, recv_sems, local_sems = refs[2 * n:]
        x, y, c = lax.axis_index("x"), lax.axis_index("y"), lax.axis_index("c")
        me, sibling = (x, y, c), (x, y, 1 - c)
        chips = [(1 - x, y), (x, 1 - y), (1 - x, 1 - y)]
        index = lambda px, py, pc: 4 * px + 2 * py + pc

        def copy(a, k, block, to, src=None):
            rows = outs[a].at[index(*block)]
            return pltpu.make_async_remote_copy(
                src_ref=rows if src is None else src, dst_ref=rows,
                send_sem=send_sems.at[a, k], recv_sem=recv_sems.at[a, k],
                device_id=to, device_id_type=MESH)

        mine = [pltpu.make_async_copy(ins[a], outs[a].at[index(*me)], local_sems.at[a])
                for a in range(n)]
        for cp in mine:
            cp.start()
        first = []
        for a in range(n):
            first.append(copy(a, 0, me, sibling, src=ins[a]))
            first += [copy(a, 1 + j, me, (*chip, c), src=ins[a]) for j, chip in enumerate(chips)]
        for cp in first:
            cp.start()
        passed = []
        for j, chip in enumerate(chips):
            for a in range(n):
                copy(a, 1 + j, (*chip, c), me).wait_recv()
                fwd = copy(a, 4 + j, (*chip, c), sibling)
                fwd.start()
                passed.append(fwd)
        for a in range(n):
            copy(a, 0, sibling, me).wait_recv()
            for j, chip in enumerate(chips):
                copy(a, 4 + j, (*chip, 1 - c), me).wait_recv()
        for cp in first + passed:
            cp.wait_send()
        for cp in mine:
            cp.wait()

    return pl.pallas_call(
        body, name=name,
        out_shape=[jax.ShapeDtypeStruct((N_DEV,) + a.shape, a.dtype) for a in arrs],
        in_specs=[HBM_SPEC] * n, out_specs=[HBM_SPEC] * n,
        scratch_shapes=[pltpu.SemaphoreType.DMA((n, 7)), pltpu.SemaphoreType.DMA((n, 7)),
                        pltpu.SemaphoreType.DMA((n,))],
    )(*arrs)


def _sibling_swap(arrs, name):
    n = len(arrs)

    def body(*refs):
        ins, outs = refs[:n], refs[n:2 * n]
        send_sems, recv_sems = refs[2 * n:]
        x, y, c = lax.axis_index("x"), lax.axis_index("y"), lax.axis_index("c")
        copies = [pltpu.make_async_remote_copy(
            src_ref=ins[a].at[1 - c], dst_ref=outs[a],
            send_sem=send_sems.at[a], recv_sem=recv_sems.at[a],
            device_id=(x, y, 1 - c), device_id_type=MESH) for a in range(n)]
        for cp in copies:
            cp.start()
        for cp in copies:
            cp.wait()

    return pl.pallas_call(
        body, name=name,
        out_shape=[jax.ShapeDtypeStruct(a.shape[1:], a.dtype) for a in arrs],
        in_specs=[HBM_SPEC] * n, out_specs=[HBM_SPEC] * n,
        scratch_shapes=[pltpu.SemaphoreType.DMA((n,)), pltpu.SemaphoreType.DMA((n,))],
    )(*arrs)


def _pair_sum(mine, theirs, core, name):
    _, rows, cols = mine.shape
    tr, tc = _tile_2d(rows, cols, budget_bytes=2 << 20)

    def body(core_ref, a_ref, b_ref, o_ref):
        o_ref[...] = (a_ref[...].astype(F32) + b_ref[...].astype(F32)).astype(o_ref.dtype)

    return pl.pallas_call(
        body, name=name, out_shape=jax.ShapeDtypeStruct((rows, cols), mine.dtype),
        grid_spec=pltpu.PrefetchScalarGridSpec(
            num_scalar_prefetch=1, grid=(rows // tr, cols // tc),
            in_specs=[pl.BlockSpec((None, tr, tc), lambda i, j, core_ref: (core_ref[0], i, j)),
                      pl.BlockSpec((tr, tc), lambda i, j, core_ref: (i, j))],
            out_specs=pl.BlockSpec((tr, tc), lambda i, j, core_ref: (i, j))),
        compiler_params=_cparams("parallel", "parallel"),
    )(core, mine, theirs)


def _chip_exchange(arrs, name):
    n = len(arrs)

    def body(*refs):
        ins, outs = refs[:n], refs[n:2 * n]
        send_sems, recv_sems, local_sems = refs[2 * n:]
        x, y, c = lax.axis_index("x"), lax.axis_index("y"), lax.axis_index("c")
        me = 2 * x + y
        flip = lambda v, bit: 1 - v if bit else v
        mine = [pltpu.make_async_copy(ins[a].at[me], outs[a].at[me], local_sems.at[a])
                for a in range(n)]
        for cp in mine:
            cp.start()
        copies = []
        for k in range(1, 4):
            px, py = flip(x, k & 2), flip(y, k & 1)
            for a in range(n):
                copies.append(pltpu.make_async_remote_copy(
                    src_ref=ins[a].at[2 * px + py], dst_ref=outs[a].at[me],
                    send_sem=send_sems.at[a, k - 1], recv_sem=recv_sems.at[a, k - 1],
                    device_id=(px, py, c), device_id_type=MESH))
        for cp in copies:
            cp.start()
        for cp in copies:
            cp.wait()
        for cp in mine:
            cp.wait()

    return pl.pallas_call(
        body, name=name,
        out_shape=[jax.ShapeDtypeStruct(a.shape, a.dtype) for a in arrs],
        in_specs=[HBM_SPEC] * n, out_specs=[HBM_SPEC] * n,
        scratch_shapes=[pltpu.SemaphoreType.DMA((n, 3)), pltpu.SemaphoreType.DMA((n, 3)),
                        pltpu.SemaphoreType.DMA((n,))],
    )(*arrs)


def _block_diag(t):
    nb, gpb, r, c = t.shape
    eye = jnp.eye(gpb, dtype=t.dtype)
    return jnp.einsum("ngrc,gh->ngrhc", t, eye).reshape(nb, gpb * r, gpb * c)


def _diag_blocks(t, r, c):
    nb = t.shape[0]
    gpb = t.shape[1] // r
    t = t.reshape(nb, gpb, r, gpb, c)
    return jnp.einsum("ngrhc,gh->ngrc", t, jnp.eye(gpb, dtype=t.dtype))


def _pack_rows(parts):
    flat = jnp.concatenate([p.reshape(-1).astype(F32) for p in parts])
    pad = (-flat.shape[0]) % (256 * 128)
    return jnp.pad(flat, (0, pad)).reshape(-1, 128)


def _unpack_rows(packed, shapes):
    flat = packed.reshape(-1)
    out, at = [], 0
    for shape in shapes:
        size = math.prod(shape)
        out.append(flat[at:at + size].reshape(shape))
        at += size
    return out


def kernel(x, ln_w, w_in, s5_lam_re, s5_lam_im, s5_log_step, s5_b_re, s5_b_im, s5_c_re, s5_c_im, s5_d, s5_w_glu, s5_w_up, dn_conv_w, dn_a_log, dn_dt_bias, dn_norm_w, dn_w_up, w_out, final_norm_w, loss_target, m_ln_w, m_w_in, m_s5_lam_re, m_s5_lam_im, m_s5_log_step, m_s5_b_re, m_s5_b_im, m_s5_c_re, m_s5_c_im, m_s5_d, m_s5_w_glu, m_s5_w_up, m_dn_conv_w, m_dn_a_log, m_dn_dt_bias, m_dn_norm_w, m_dn_w_up, m_w_out, m_final_norm_w, v_ln_w, v_w_in, v_s5_lam_re, v_s5_lam_im, v_s5_log_step, v_s5_b_re, v_s5_b_im, v_s5_c_re, v_s5_c_im, v_s5_d, v_s5_w_glu, v_s5_w_up, v_dn_conv_w, v_dn_a_log, v_dn_dt_bias, v_dn_norm_w, v_dn_w_up, v_w_out, v_final_norm_w):
    weights = dict(ln_w=ln_w, w_in=w_in, s5_lam_re=s5_lam_re, s5_lam_im=s5_lam_im,
                   s5_log_step=s5_log_step, s5_b_re=s5_b_re, s5_b_im=s5_b_im, s5_c_re=s5_c_re,
                   s5_c_im=s5_c_im, s5_d=s5_d, s5_w_glu=s5_w_glu, s5_w_up=s5_w_up,
                   dn_conv_w=dn_conv_w, dn_a_log=dn_a_log, dn_dt_bias=dn_dt_bias,
                   dn_norm_w=dn_norm_w, dn_w_up=dn_w_up, w_out=w_out, final_norm_w=final_norm_w)
    mom_m = dict(ln_w=m_ln_w, w_in=m_w_in, s5_lam_re=m_s5_lam_re, s5_lam_im=m_s5_lam_im,
                 s5_log_step=m_s5_log_step, s5_b_re=m_s5_b_re, s5_b_im=m_s5_b_im,
                 s5_c_re=m_s5_c_re, s5_c_im=m_s5_c_im, s5_d=m_s5_d, s5_w_glu=m_s5_w_glu,
                 s5_w_up=m_s5_w_up, dn_conv_w=m_dn_conv_w, dn_a_log=m_dn_a_log,
                 dn_dt_bias=m_dn_dt_bias, dn_norm_w=m_dn_norm_w, dn_w_up=m_dn_w_up,
                 w_out=m_w_out, final_norm_w=m_final_norm_w)
    mom_v = dict(ln_w=v_ln_w, w_in=v_w_in, s5_lam_re=v_s5_lam_re, s5_lam_im=v_s5_lam_im,
                 s5_log_step=v_s5_log_step, s5_b_re=v_s5_b_re, s5_b_im=v_s5_b_im,
                 s5_c_re=v_s5_c_re, s5_c_im=v_s5_c_im, s5_d=v_s5_d, s5_w_glu=v_s5_w_glu,
                 s5_w_up=v_s5_w_up, dn_conv_w=v_dn_conv_w, dn_a_log=v_dn_a_log,
                 dn_dt_bias=v_dn_dt_bias, dn_norm_w=v_dn_norm_w, dn_w_up=v_dn_w_up,
                 w_out=v_w_out, final_norm_w=v_final_norm_w)
    names = list(weights)

    l, d = x.shape[1], x.shape[2]
    d_s5 = d // 2
    groups = d_s5 // S5_GROUP
    nb = groups // S5_GPB
    d_dn = DN_HEADS * DN_HEAD_DIM
    w_in_cols = w_in.shape[2]
    d_in = N_DEV * w_in_cols
    off_ba_src = 2 * d_s5 + 4 * d_dn
    off_u, off_zs, off_qkv, off_zd = 0, d_s5, 2 * d_s5, 2 * d_s5 + 3 * d_dn
    off_ba = off_zd + d_dn
    off_gs = off_ba + BA_PAD
    off_gd = off_gs + d
    n_proj = off_gd + d
    x2d, tgt2d = x[0], loss_target[0]
    my_index = 4 * lax.axis_index("x") + 2 * lax.axis_index("y") + lax.axis_index("c")

    g_win, g_glu, g_sup, g_dup, g_wout, g_conv = _all_gather(
        [jnp.transpose(w_in[0]).astype(BF16), s5_w_glu[0].astype(BF16), s5_w_up[0].astype(BF16),
         dn_w_up[0].astype(BF16), w_out[0].astype(BF16), dn_conv_w[0]], name="gather_weights")
    ba_end = off_ba_src + 2 * DN_HEADS
    ba_zeros = BA_PAD - 2 * DN_HEADS
    pieces = []
    for dev in range(N_DEV):
        lo, hi = dev * w_in_cols, (dev + 1) * w_in_cols
        if lo < ba_end <= hi:
            pieces += [g_win[dev, :ba_end - lo], jnp.zeros((ba_zeros, d), BF16), g_win[dev, ba_end - lo:]]
        else:
            pieces.append(g_win[dev])
    w_cat_t = jnp.concatenate(pieces, axis=0)
    wglu_full = g_glu.reshape(d_s5, d_s5)
    wsup_full = jnp.transpose(g_sup, (1, 0, 2)).reshape(d_s5, d)
    wdup_full = jnp.transpose(g_dup, (1, 0, 2)).reshape(d_dn, d)
    wout_full = g_wout.reshape(d, d)
    conv_full = jnp.transpose(g_conv, (1, 0, 2)).reshape(CONV_K, 3 * d_dn)

    lam_re, lam_im = s5_lam_re[0], s5_lam_im[0]
    log_step = s5_log_step[0].reshape(groups, 1)
    b_re = s5_b_re[0].reshape(groups * S5_STATE, S5_GROUP)
    b_im = s5_b_im[0].reshape(groups * S5_STATE, S5_GROUP)
    abar_re, abar_im, f_re, f_im = _s5_disc_fwd(lam_re, lam_im, log_step)
    f_re_col, f_im_col = f_re.reshape(-1, 1), f_im.reshape(-1, 1)
    bb_re, bb_im = _s5_bbar_fwd(f_re_col, f_im_col, b_re, b_im)

    def bb_blocks(t):
        t = t.reshape(nb, S5_GPB, S5_STATE, S5_GROUP).transpose(0, 1, 3, 2)
        return _block_diag(t).astype(BF16)

    def c_blocks(t):
        return _block_diag(t.reshape(nb, S5_GPB, S5_GROUP, S5_STATE)).astype(BF16)

    bbr, bbi = bb_blocks(bb_re), bb_blocks(bb_im)
    cbr, cbi = c_blocks(s5_c_re[0]), c_blocks(s5_c_im[0])
    ctr, cti = jnp.transpose(cbr, (0, 2, 1)), jnp.transpose(cbi, (0, 2, 1))
    a_re = abar_re.reshape(nb, 1, S5_GPB * S5_STATE)
    a_im = abar_im.reshape(nb, 1, S5_GPB * S5_STATE)

    h = _rms_fwd(x2d, ln_w)
    proj = _mm(h, w_cat_t, tb=True, tn=1536, name="proj")
    y1, car_r, car_i = _s5_fwd(proj, bbr, bbi, a_re, a_im, ctr, cti, s5_d, d_s5)
    out_s = _s5_glu_fwd(y1, proj, off_zs, wglu_full)
    y_s = _mm(out_s, wsup_full, name="s5_up")

    a_log_row = jnp.pad(dn_a_log, ((0, 0), (DN_HEADS, 128 - 2 * DN_HEADS)))
    dt_row = jnp.pad(dn_dt_bias, ((0, 0), (DN_HEADS, 128 - 2 * DN_HEADS)))
    qkv = _dn_prep_fwd(proj, off_qkv, conv_full)
    gb = _dn_gates_fwd(proj, off_ba, a_log_row, dt_row)
    o_dn, states = _dn_chunk_fwd(qkv, gb)
    out_d = _dn_out_fwd(o_dn, proj, off_zd, dn_norm_w)
    y_d = _mm(out_d, wdup_full, name="dn_up")

    mixed = _merge_fwd(proj, off_gs, off_gd, y_s, y_d)
    branch = _mm(mixed, wout_full, name="w_out")
    dx2, dx2_bf, loss_dev, d_final_w = _final(x2d, branch, final_norm_w.reshape(1, d), tgt2d)

    g_wout_full = _mm(mixed, dx2_bf, ta=True, out_dtype=BF16, name="grad_w_out")
    dmixed = _mm(dx2_bf, wout_full, tb=True, name="d_mixed")
    dgs, dgd, dys, dyd = _merge_bwd(proj, off_gs, off_gd, y_s, y_d, dmixed)

    g_dup_full = _mm(out_d, dyd, ta=True, out_dtype=BF16, name="grad_dn_up")
    dout_d = _mm(dyd, wdup_full, tb=True, name="d_out_d")
    do_dn, dzd, d_norm_w = _dn_out_bwd(o_dn, proj, off_zd, dn_norm_w, dout_d)
    dqkv, dgb_heads = _dn_chunk_bwd(qkv, gb, states, do_dn)
    dba, d_a_log_row, d_dt_row = _dn_gates_bwd(proj, off_ba, a_log_row, dt_row, dgb_heads)
    dqkv_pre, d_conv_full = _dn_prep_bwd(proj, off_qkv, conv_full, dqkv)

    g_sup_full = _mm(out_s, dys, ta=True, out_dtype=BF16, name="grad_s5_up")
    dout_s = _mm(dys, wsup_full, tb=True, name="d_out_s")
    dy1, dzs, g_glu_full = _s5_glu_bwd(y1, proj, off_zs, wglu_full, dout_s)
    (du, d_a_re, d_a_im, d_bbr, d_bbi, d_cbr, d_cbi, d_s5_d) = _s5_bwd(
        proj, dy1, bbr, bbi, a_re, a_im, cbr, cbi, s5_d, car_r, car_i)

    def from_bb_blocks(t):
        t = _diag_blocks(t, S5_GROUP, S5_STATE).transpose(0, 1, 3, 2)
        return t.reshape(groups * S5_STATE, S5_GROUP)

    d_f_re, d_f_im, d_b_re, d_b_im = _s5_bbar_bwd(f_re_col, f_im_col, b_re, b_im,
                                                 from_bb_blocks(d_bbr), from_bb_blocks(d_bbi))
    d_lam_re, d_lam_im, d_log_step = _s5_disc_bwd(
        lam_re, lam_im, log_step, d_a_re.reshape(groups, S5_STATE), d_a_im.reshape(groups, S5_STATE),
        d_f_re.reshape(groups, S5_STATE), d_f_im.reshape(groups, S5_STATE))
    d_c_re = _diag_blocks(d_cbr, S5_GROUP, S5_STATE).reshape(groups, S5_GROUP, S5_STATE)
    d_c_im = _diag_blocks(d_cbi, S5_GROUP, S5_STATE).reshape(groups, S5_GROUP, S5_STATE)

    dproj = jnp.concatenate(
        [du, dzs, dqkv_pre, dzd, jnp.pad(dba, ((0, 0), (0, BA_PAD - 128))), dgs, dgd], axis=1)
    g_wcat_t = _mm(dproj, h, ta=True, out_dtype=BF16, tm=768, tn=d, name="grad_w_in")
    dh = _mm(dproj, w_cat_t, tm=1024, tn=1024, tk=1536, name="d_h")
    grad_x, d_ln_w = _rms_bwd(x2d, ln_w, dh, dx2)

    def w_in_rows(dev):
        lo, hi = dev * w_in_cols, (dev + 1) * w_in_cols
        if hi <= ba_end:
            return g_wcat_t[lo:hi]
        if lo >= ba_end:
            return g_wcat_t[lo + ba_zeros:hi + ba_zeros]
        return jnp.concatenate([g_wcat_t[lo:ba_end], g_wcat_t[ba_end + ba_zeros:hi + ba_zeros]], axis=0)

    g_win_parts = jnp.stack([jnp.stack([w_in_rows(2 * chip + core) for chip in range(4)])
                             for core in range(2)])

    def by_dest(t, axis):
        if axis == 1:
            return t.reshape(t.shape[0], 4, 2, t.shape[1] // N_DEV).transpose(2, 1, 0, 3)
        return t.reshape(4, 2, t.shape[0] // N_DEV, t.shape[1]).transpose(1, 0, 2, 3)

    big = ["w_in", "s5_w_glu", "s5_w_up", "dn_w_up", "w_out"]
    parts = [g_win_parts, by_dest(g_glu_full.astype(BF16), 0), by_dest(g_sup_full, 1),
             by_dest(g_dup_full, 1), by_dest(g_wout_full, 0)]
    from_sibling = _sibling_swap(parts, name="swap_grads")
    core = lax.axis_index("c").astype(jnp.int32).reshape(1)
    chip_sums = [
        _pair_sum(p.reshape(2, -1, p.shape[-1]), got.reshape(-1, got.shape[-1]), core,
                  name="pair_sum_" + nm).reshape(got.shape)
        for nm, p, got in zip(big, parts, from_sibling)]
    slots = _chip_exchange(chip_sums, name="exchange_grads")
    results = {}
    for nm, sl in zip(big, slots):
        operands = [weights[nm][0], mom_m[nm][0], mom_v[nm][0]]
        if nm == "w_in":
            operands = [jnp.transpose(t) for t in operands]
        res = _adamw(*operands, sl, name="adamw_" + nm)
        results[nm] = [jnp.transpose(t) for t in res] if nm == "w_in" else res

    small = [nm for nm in names if nm not in big]
    small_grads = dict(
        ln_w=d_ln_w, s5_lam_re=d_lam_re, s5_lam_im=d_lam_im, s5_log_step=d_log_step,
        s5_b_re=d_b_re, s5_b_im=d_b_im, s5_c_re=d_c_re, s5_c_im=d_c_im, s5_d=d_s5_d,
        dn_conv_w=d_conv_full, dn_a_log=d_a_log_row[:, DN_HEADS:2 * DN_HEADS],
        dn_dt_bias=d_dt_row[:, DN_HEADS:2 * DN_HEADS], dn_norm_w=d_norm_w, final_norm_w=d_final_w)
    (all_small,) = _all_gather([_pack_rows([small_grads[nm] for nm in small])], name="gather_small_grads")
    summed = _slot_sum(all_small, name="sum_small_grads")
    full_shapes = [(CONV_K, 3 * d_dn) if nm == "dn_conv_w" else weights[nm].shape for nm in small]
    g_small = dict(zip(small, _unpack_rows(summed, full_shapes)))
    conv_cols = dn_conv_w.shape[2]
    g_small["dn_conv_w"] = lax.dynamic_slice_in_dim(
        g_small["dn_conv_w"], my_index * conv_cols, conv_cols, axis=1).reshape(dn_conv_w.shape)
    packed = [_pack_rows([t[nm] for nm in small]) for t in (weights, mom_m, mom_v, g_small)]
    small_out = _adamw(packed[0], packed[1], packed[2], packed[3][None], name="adamw_small")
    small_shapes = [weights[nm].shape for nm in small]
    for kind, packed_out in enumerate(small_out):
        for nm, val in zip(small, _unpack_rows(packed_out, small_shapes)):
            results.setdefault(nm, [None] * 4)[kind] = val

    loss = lax.psum(loss_dev[0, 0], ("x", "y", "c"))
    outs = [loss, grad_x[None]]
    for kind in range(4):
        outs += [results[nm][kind].reshape(weights[nm].shape) for nm in names]
    return tuple(outs)
```

```python
import functools
import math

import jax
import jax.numpy as jnp
from jax import lax
from jax.experimental import pallas as pl
from jax.experimental.pallas import tpu as pltpu

F32 = jnp.float32
BF16 = jnp.bfloat16
HIGHEST = lax.Precision.HIGHEST
MESH = pl.DeviceIdType.MESH
N_DEV = 8

EPS = 1e-6
S5_GROUP = 16
S5_STATE = 64
S5_GPB = 8
S5_T = 256
DN_HEADS = 8
DN_HEAD_DIM = 128
CHUNK = 64
DN_HEADS_PER_STEP = 8
CONV_K = 4
BA_PAD = 512

ADAM_LR = 0.001
ADAM_B1 = 0.9
ADAM_B2 = 0.999
ADAM_EPS = 1e-08
ADAM_WD = 0.01
ADAM_STEP = 10

VMEM_LIMIT_BYTES = 48 * 1024 * 1024
ROW_TILE = 256


def _cparams(*sem):
    return pltpu.CompilerParams(dimension_semantics=sem if sem else None,
                                vmem_limit_bytes=VMEM_LIMIT_BYTES)


def _sigmoid(x):
    return 1.0 / (1.0 + jnp.exp(-x))


def _silu(x):
    return x * _sigmoid(x)


def _gelu(x):
    return 0.5 * x * (1.0 + jnp.tanh(0.7978845608028654 * (x + 0.044715 * x * x * x)))


def _softplus(x):
    return jnp.maximum(x, 0.0) + jnp.log(1.0 + jnp.exp(-jnp.abs(x)))


def _rmsnorm(x, w):
    return x * lax.rsqrt(jnp.mean(x * x, axis=-1, keepdims=True) + EPS) * w


def _dot(a, b, dims=((1,), (0,)), precision=None):
    return lax.dot_general(a, b, (dims, ((), ())), precision=precision,
                           preferred_element_type=F32)


def _bdot(a, b, dims=((1,), (0,))):
    return _dot(a.astype(BF16), b.astype(BF16), dims)


def _split_bf16(a):
    hi = a.astype(BF16)
    return hi, (a - hi.astype(F32)).astype(BF16)


def _dot3_dims(a, b, dims):
    ah, al = _split_bf16(a)
    bh, bl = _split_bf16(b)
    return _dot(ah, bh, dims) + (_dot(ah, bl, dims) + _dot(al, bh, dims))


@jax.custom_vjp
def _dot3(a, b):
    return _dot3_dims(a, b, ((1,), (0,)))


def _dot3_fwd(a, b):
    return _dot3(a, b), (a, b)


def _dot3_bwd(res, g):
    a, b = res
    return _dot3_dims(g, b, ((1,), (1,))), _dot3_dims(a, g, ((0,), (0,)))


_dot3.defvjp(_dot3_fwd, _dot3_bwd)


def _mm(a, b, *, ta=False, tb=False, out_dtype=F32, tm=512, tn=512, tk=None, after=None, name):
    k_dim, m_dim = (a.shape if ta else a.shape[::-1])
    n_dim = b.shape[0] if tb else b.shape[1]
    assert (b.shape[1] if tb else b.shape[0]) == k_dim
    tm, tn = min(tm, m_dim), min(tn, n_dim)
    tk = k_dim if tk is None else tk
    assert m_dim % tm == 0 and n_dim % tn == 0 and k_dim % tk == 0
    nk = k_dim // tk
    a_spec = (pl.BlockSpec((tk, tm), lambda i, j, k: (k, i)) if ta
              else pl.BlockSpec((tm, tk), lambda i, j, k: (i, k)))
    b_spec = (pl.BlockSpec((tn, tk), lambda i, j, k: (j, k)) if tb
              else pl.BlockSpec((tk, tn), lambda i, j, k: (k, j)))
    dims = ((0 if ta else 1,), (1 if tb else 0,))

    def body(a_ref, b_ref, *rest):
        o_ref, *scratch = rest[1:] if after is not None else rest
        p = _bdot(a_ref[...], b_ref[...], dims)
        if nk == 1:
            o_ref[...] = p.astype(o_ref.dtype)
        else:
            acc = scratch[0]
            k = pl.program_id(2)

            @pl.when(k == 0)
            def _():
                acc[...] = p

            @pl.when(k > 0)
            def _():
                acc[...] += p

            @pl.when(k == nk - 1)
            def _():
                o_ref[...] = acc[...].astype(o_ref.dtype)

    return pl.pallas_call(
        body, name=name,
        out_shape=jax.ShapeDtypeStruct((m_dim, n_dim), out_dtype),
        grid=(m_dim // tm, n_dim // tn, nk),
        in_specs=[a_spec, b_spec] + ([pl.BlockSpec((8, 128), lambda i, j, k: (0, 0))]
                                     if after is not None else []),
        out_specs=pl.BlockSpec((tm, tn), lambda i, j, k: (i, j)),
        scratch_shapes=[pltpu.VMEM((tm, tn), F32)] if nk > 1 else [],
        compiler_params=_cparams("parallel", "parallel", "arbitrary"),
    )(a, b, *([after] if after is not None else []))


def _rms_fwd(x, w):
    l, d = x.shape

    def body(x_ref, w_ref, h_ref):
        h_ref[...] = _rmsnorm(x_ref[...], w_ref[...]).astype(BF16)

    return pl.pallas_call(
        body, name="rms_fwd",
        out_shape=jax.ShapeDtypeStruct((l, d), BF16),
        grid=(l // ROW_TILE,),
        in_specs=[pl.BlockSpec((ROW_TILE, d), lambda i: (i, 0)),
                  pl.BlockSpec((1, d), lambda i: (0, 0))],
        out_specs=pl.BlockSpec((ROW_TILE, d), lambda i: (i, 0)),
        compiler_params=_cparams("parallel"),
    )(x, w)


def _rms_bwd(x, w, dh, dres):
    l, d = x.shape

    def body(x_ref, w_ref, dh_ref, dres_ref, dx_ref, dw_ref):
        _, vjp = jax.vjp(_rmsnorm, x_ref[...], w_ref[...])
        dx, dw = vjp(dh_ref[...])
        dx_ref[...] = dx + dres_ref[...]

        @pl.when(pl.program_id(0) == 0)
        def _():
            dw_ref[...] = jnp.zeros_like(dw_ref)

        dw_ref[...] += dw

    row = pl.BlockSpec((ROW_TILE, d), lambda i: (i, 0))
    vec = pl.BlockSpec((1, d), lambda i: (0, 0))
    return pl.pallas_call(
        body, name="rms_bwd",
        out_shape=(jax.ShapeDtypeStruct((l, d), F32), jax.ShapeDtypeStruct((1, d), F32)),
        grid=(l // ROW_TILE,),
        in_specs=[row, vec, row, row],
        out_specs=(row, vec),
        compiler_params=_cparams("arbitrary"),
    )(x, w, dh, dres)


def _final(x, r, fw, target):
    l, d = x.shape

    def per_row_loss(x2, w, tgt):
        err = _rmsnorm(x2, w) - tgt
        return 0.5 * jnp.mean(err * err, axis=-1, keepdims=True)

    def body(x_ref, r_ref, w_ref, t_ref, dx_ref, dxb_ref, loss_ref, dw_ref):
        x2 = x_ref[...] + r_ref[...]
        rows, vjp = jax.vjp(functools.partial(per_row_loss, tgt=t_ref[...]), x2, w_ref[...])
        dx2, dw = vjp(jnp.ones_like(rows))
        dx_ref[...] = dx2
        dxb_ref[...] = dx2.astype(BF16)

        @pl.when(pl.program_id(0) == 0)
        def _():
            dw_ref[...] = jnp.zeros_like(dw_ref)
            loss_ref[...] = jnp.zeros_like(loss_ref)

        dw_ref[...] += dw
        loss_ref[...] += jnp.sum(rows, axis=0, keepdims=True)

    row = pl.BlockSpec((ROW_TILE, d), lambda i: (i, 0))
    vec = pl.BlockSpec((1, d), lambda i: (0, 0))
    return pl.pallas_call(
        body, name="final_norm_loss",
        out_shape=(jax.ShapeDtypeStruct((l, d), F32), jax.ShapeDtypeStruct((l, d), BF16),
                   jax.ShapeDtypeStruct((1, 1), F32), jax.ShapeDtypeStruct((1, d), F32)),
        grid=(l // ROW_TILE,),
        in_specs=[row, row, vec, row],
        out_specs=(row, row, pl.BlockSpec((1, 1), lambda i: (0, 0)), vec),
        compiler_params=_cparams("arbitrary"),
    )(x, r, fw, target)


def _merge_fn(gs, gd, ys, yd):
    return _sigmoid(gs) * ys + _sigmoid(gd) * yd


def _merge_fwd(proj, off_gs, off_gd, ys, yd):
    l, d = ys.shape
    cw = 512
    blk = lambda off: pl.BlockSpec((ROW_TILE, cw), lambda i, j: (i, off // cw + j))

    def body(gs_ref, gd_ref, ys_ref, yd_ref, o_ref):
        o_ref[...] = _merge_fn(gs_ref[...], gd_ref[...], ys_ref[...], yd_ref[...]).astype(BF16)

    return pl.pallas_call(
        body, name="merge_fwd",
        out_shape=jax.ShapeDtypeStruct((l, d), BF16),
        grid=(l // ROW_TILE, d // cw),
        in_specs=[blk(off_gs), blk(off_gd), blk(0), blk(0)],
        out_specs=blk(0),
        compiler_params=_cparams("parallel", "parallel"),
    )(proj, proj, ys, yd)


def _merge_bwd(proj, off_gs, off_gd, ys, yd, dmixed):
    l, d = ys.shape
    cw = 512
    blk = lambda off: pl.BlockSpec((ROW_TILE, cw), lambda i, j: (i, off // cw + j))

    def body(gs_ref, gd_ref, ys_ref, yd_ref, dm_ref, dgs_ref, dgd_ref, dys_ref, dyd_ref):
        _, vjp = jax.vjp(_merge_fn, gs_ref[...], gd_ref[...], ys_ref[...], yd_ref[...])
        dgs, dgd, dys, dyd = vjp(dm_ref[...])
        dgs_ref[...] = dgs.astype(BF16)
        dgd_ref[...] = dgd.astype(BF16)
        dys_ref[...] = dys.astype(BF16)
        dyd_ref[...] = dyd.astype(BF16)

    out = jax.ShapeDtypeStruct((l, d), BF16)
    return pl.pallas_call(
        body, name="merge_bwd",
        out_shape=(out, out, out, out),
        grid=(l // ROW_TILE, d // cw),
        in_specs=[blk(off_gs), blk(off_gd), blk(0), blk(0), blk(0)],
        out_specs=(blk(0), blk(0), blk(0), blk(0)),
        compiler_params=_cparams("parallel", "parallel"),
    )(proj, proj, ys, yd, dmixed)


def _s5_disc_fn(lam_re, lam_im, log_step):
    step = jnp.exp(log_step)
    mag = jnp.exp(lam_re * step)
    abar_re = mag * jnp.cos(lam_im * step)
    abar_im = mag * jnp.sin(lam_im * step)
    den = lam_re * lam_re + lam_im * lam_im
    xr = abar_re - 1.0
    f_re = (xr * lam_re + abar_im * lam_im) / den
    f_im = (abar_im * lam_re - xr * lam_im) / den
    return abar_re, abar_im, f_re, f_im


def _s5_disc_fwd(lam_re, lam_im, log_step):
    g, p = lam_re.shape

    def body(lr_ref, li_ref, ls_ref, ar_ref, ai_ref, fr_ref, fi_ref):
        ar, ai, fr, fi = _s5_disc_fn(lr_ref[...], li_ref[...], ls_ref[...])
        ar_ref[...] = ar
        ai_ref[...] = ai
        fr_ref[...] = fr
        fi_ref[...] = fi

    o = jax.ShapeDtypeStruct((g, p), F32)
    return pl.pallas_call(body, name="s5_disc_fwd", out_shape=(o, o, o, o),
                          compiler_params=_cparams())(lam_re, lam_im, log_step)


def _s5_disc_bwd(lam_re, lam_im, log_step, dar, dai, dfr, dfi):
    g, p = lam_re.shape

    def body(lr_ref, li_ref, ls_ref, dar_ref, dai_ref, dfr_ref, dfi_ref, dlr_ref, dli_ref, dls_ref):
        _, vjp = jax.vjp(_s5_disc_fn, lr_ref[...], li_ref[...], ls_ref[...])
        dlr, dli, dls = vjp((dar_ref[...], dai_ref[...], dfr_ref[...], dfi_ref[...]))
        dlr_ref[...] = dlr
        dli_ref[...] = dli
        dls_ref[...] = dls

    o = jax.ShapeDtypeStruct((g, p), F32)
    return pl.pallas_call(body, name="s5_disc_bwd",
                          out_shape=(o, o, jax.ShapeDtypeStruct((g, 1), F32)),
                          compiler_params=_cparams())(lam_re, lam_im, log_step, dar, dai, dfr, dfi)


def _s5_bbar_fwd(f_re, f_im, b_re, b_im):
    n, c = b_re.shape

    def body(fr_ref, fi_ref, br_ref, bi_ref, or_ref, oi_ref):
        fr, fi, br, bi = fr_ref[...], fi_ref[...], br_ref[...], bi_ref[...]
        or_ref[...] = fr * br - fi * bi
        oi_ref[...] = fr * bi + fi * br

    o = jax.ShapeDtypeStruct((n, c), F32)
    return pl.pallas_call(body, name="s5_bbar_fwd", out_shape=(o, o),
                          compiler_params=_cparams())(f_re, f_im, b_re, b_im)


def _s5_bbar_bwd(f_re, f_im, b_re, b_im, dbr, dbi):
    n, c = b_re.shape

    def body(fr_ref, fi_ref, br_ref, bi_ref, dor_ref, doi_ref, dfr_ref, dfi_ref, dbr_ref, dbi_ref):
        fr, fi, br, bi = fr_ref[...], fi_ref[...], br_ref[...], bi_ref[...]
        dor, doi = dor_ref[...], doi_ref[...]
        dfr_ref[...] = jnp.sum(dor * br + doi * bi, axis=-1, keepdims=True)
        dfi_ref[...] = jnp.sum(doi * br - dor * bi, axis=-1, keepdims=True)
        dbr_ref[...] = fr * dor + fi * doi
        dbi_ref[...] = fr * doi - fi * dor

    col = jax.ShapeDtypeStruct((n, 1), F32)
    o = jax.ShapeDtypeStruct((n, c), F32)
    return pl.pallas_call(body, name="s5_bbar_bwd", out_shape=(col, col, o, o),
                          compiler_params=_cparams())(f_re, f_im, b_re, b_im, dbr, dbi)


SUBLANES = 8


def _scan_groups(xr, xi, ar, ai, reverse):
    t = xr.shape[0]
    sub = lax.broadcasted_iota(jnp.int32, (t, 1), 0) & (SUBLANES - 1)
    pr, pi = ar, ai
    for sh in (1, 2, 4):
        if reverse:
            keep = sub < SUBLANES - sh
            sr, si = pltpu.roll(xr, t - sh, 0), pltpu.roll(xi, t - sh, 0)
        else:
            keep = sub >= sh
            sr, si = pltpu.roll(xr, sh, 0), pltpu.roll(xi, sh, 0)
        sr = jnp.where(keep, sr, 0.0)
        si = jnp.where(keep, si, 0.0)
        xr, xi = xr + pr * sr - pi * si, xi + pr * si + pi * sr
        pr, pi = pr * pr - pi * pi, 2.0 * pr * pi
    return xr, xi


def _scan_rows(xr, xi, ar, ai, cr, ci, sr_ref, si_ref, reverse):
    t, n = xr.shape
    xr, xi = _scan_groups(xr, xi, ar, ai, reverse)
    sr_ref[...] = xr
    si_ref[...] = xi
    sub = lax.broadcasted_iota(jnp.int32, (SUBLANES, n), 0)
    seed = sub == (SUBLANES - 1 if reverse else 0)
    pwr, pwi = _scan_groups(jnp.where(seed, ar, 0.0), jnp.where(seed, ai, 0.0), ar, ai, reverse)
    groups = range(t // SUBLANES)
    edge = 0 if reverse else SUBLANES - 1
    for g in (reversed(groups) if reverse else groups):
        rows = slice(g * SUBLANES, (g + 1) * SUBLANES)
        vr = sr_ref[rows, :] + (pwr * cr - pwi * ci)
        vi = si_ref[rows, :] + (pwr * ci + pwi * cr)
        sr_ref[rows, :] = vr
        si_ref[rows, :] = vi
        cr, ci = vr[edge:edge + 1, :], vi[edge:edge + 1, :]
    return cr, ci


def _s5_states(u_bf, bbr, bbi, ar, ai, cr, ci, sr_ref, si_ref):
    return _scan_rows(_dot(u_bf, bbr), _dot(u_bf, bbi), ar, ai, cr, ci, sr_ref, si_ref, reverse=False)


def _s5_fwd(proj, bbr, bbi, a_re, a_im, ctr, cti, d_skip, d_s5):
    l = proj.shape[0]
    nb, uc, ns = bbr.shape
    t = min(S5_T, l)
    nt = l // t

    def body(u_ref, bbr_ref, bbi_ref, ar_ref, ai_ref, ctr_ref, cti_ref, d_ref,
             y_ref, car_r_ref, car_i_ref, cr, ci, sr_ref, si_ref):
        @pl.when(pl.program_id(1) == 0)
        def _():
            cr[...] = jnp.zeros_like(cr)
            ci[...] = jnp.zeros_like(ci)

        car_r_ref[...] = cr[...]
        car_i_ref[...] = ci[...]
        u = u_ref[...]
        cr[...], ci[...] = _s5_states(u.astype(BF16), bbr_ref[...], bbi_ref[...], ar_ref[...],
                                      ai_ref[...], cr[...], ci[...], sr_ref, si_ref)
        y_ref[...] = (_bdot(sr_ref[...], ctr_ref[...]) - _bdot(si_ref[...], cti_ref[...])
                      + d_ref[...] * u)

    per_block = lambda shape: pl.BlockSpec((None,) + shape, lambda b, n: (b, 0, 0))
    return pl.pallas_call(
        body, name="s5_fwd",
        out_shape=(jax.ShapeDtypeStruct((l, d_s5), F32),
                   jax.ShapeDtypeStruct((nt, 1, nb * ns), F32),
                   jax.ShapeDtypeStruct((nt, 1, nb * ns), F32)),
        grid=(nb, nt),
        in_specs=[pl.BlockSpec((t, uc), lambda b, n: (n, b)),
                  per_block((uc, ns)), per_block((uc, ns)),
                  per_block((1, ns)), per_block((1, ns)),
                  per_block((ns, uc)), per_block((ns, uc)),
                  pl.BlockSpec((1, uc), lambda b, n: (0, b))],
        out_specs=(pl.BlockSpec((t, uc), lambda b, n: (n, b)),
                   pl.BlockSpec((None, 1, ns), lambda b, n: (n, 0, b)),
                   pl.BlockSpec((None, 1, ns), lambda b, n: (n, 0, b))),
        scratch_shapes=[pltpu.VMEM((1, ns), F32), pltpu.VMEM((1, ns), F32),
                        pltpu.VMEM((t, ns), F32), pltpu.VMEM((t, ns), F32)],
        compiler_params=_cparams("parallel", "arbitrary"),
    )(proj, bbr, bbi, a_re, a_im, ctr, cti, d_skip)


def _s5_bwd(proj, dy, bbr, bbi, a_re, a_im, cbr, cbi, d_skip, car_r, car_i):
    l, d_s5 = dy.shape
    nb, uc, ns = bbr.shape
    t = min(S5_T, l)
    nt = l // t

    def body(u_ref, dy_ref, bbr_ref, bbi_ref, ar_ref, ai_ref, cbr_ref, cbi_ref, d_ref,
             car_r_ref, car_i_ref,
             du_ref, dar_ref, dai_ref, dbbr_ref, dbbi_ref, dcbr_ref, dcbi_ref, dd_ref, gcr, gci,
             sr_ref, si_ref, gr_ref, gi_ref):
        @pl.when(pl.program_id(1) == 0)
        def _():
            gcr[...] = jnp.zeros_like(gcr)
            gci[...] = jnp.zeros_like(gci)
            for ref in (dar_ref, dai_ref, dbbr_ref, dbbi_ref, dcbr_ref, dcbi_ref, dd_ref):
                ref[...] = jnp.zeros_like(ref)

        row = lax.broadcasted_iota(jnp.int32, (t, 1), 0)
        u, dy = u_ref[...], dy_ref[...]
        u_bf, dy_bf = u.astype(BF16), dy.astype(BF16)
        ar, ai = ar_ref[...], ai_ref[...]
        cr, ci = car_r_ref[...], car_i_ref[...]
        _s5_states(u_bf, bbr_ref[...], bbi_ref[...], ar, ai, cr, ci, sr_ref, si_ref)
        sr, si = sr_ref[...], si_ref[...]
        first = row == 0
        pr = jnp.where(first, cr, pltpu.roll(sr, 1, 0))
        pi = jnp.where(first, ci, pltpu.roll(si, 1, 0))
        gcr[...], gci[...] = _scan_rows(_dot(dy_bf, cbr_ref[...]), -_dot(dy_bf, cbi_ref[...]), ar, -ai,
                                        gcr[...], gci[...], gr_ref, gi_ref, reverse=True)
        gr, gi = gr_ref[...], gi_ref[...]
        dar_ref[...] += jnp.sum(gr * pr + gi * pi, axis=0, keepdims=True)
        dai_ref[...] += jnp.sum(gi * pr - gr * pi, axis=0, keepdims=True)
        gr_bf, gi_bf = gr.astype(BF16), gi.astype(BF16)
        tn = ((0,), (0,))
        dbbr_ref[...] += _dot(u_bf, gr_bf, tn)
        dbbi_ref[...] += _dot(u_bf, gi_bf, tn)
        dcbr_ref[...] += _dot(dy_bf, sr.astype(BF16), tn)
        dcbi_ref[...] -= _dot(dy_bf, si.astype(BF16), tn)
        nt_dims = ((1,), (1,))
        du = _dot(gr_bf, bbr_ref[...], nt_dims) + _dot(gi_bf, bbi_ref[...], nt_dims) + dy * d_ref[...]
        du_ref[...] = du.astype(BF16)
        dd_ref[...] += jnp.sum(dy * u, axis=0, keepdims=True)

    rev = lambda n: nt - 1 - n
    per_block = lambda shape: pl.BlockSpec((None,) + shape, lambda b, n: (b, 0, 0))
    acc = jax.ShapeDtypeStruct((nb, uc, ns), F32)
    vec = jax.ShapeDtypeStruct((nb, 1, ns), F32)
    return pl.pallas_call(
        body, name="s5_bwd",
        out_shape=(jax.ShapeDtypeStruct((l, d_s5), BF16), vec, vec, acc, acc, acc, acc,
                   jax.ShapeDtypeStruct((1, d_s5), F32)),
        grid=(nb, nt),
        in_specs=[pl.BlockSpec((t, uc), lambda b, n: (rev(n), b)),
                  pl.BlockSpec((t, uc), lambda b, n: (rev(n), b)),
                  per_block((uc, ns)), per_block((uc, ns)),
                  per_block((1, ns)), per_block((1, ns)),
                  per_block((uc, ns)), per_block((uc, ns)),
                  pl.BlockSpec((1, uc), lambda b, n: (0, b)),
                  pl.BlockSpec((None, 1, ns), lambda b, n: (rev(n), 0, b)),
                  pl.BlockSpec((None, 1, ns), lambda b, n: (rev(n), 0, b))],
        out_specs=(pl.BlockSpec((t, uc), lambda b, n: (rev(n), b)),
                   per_block((1, ns)), per_block((1, ns)),
                   per_block((uc, ns)), per_block((uc, ns)),
                   per_block((uc, ns)), per_block((uc, ns)),
                   pl.BlockSpec((1, uc), lambda b, n: (0, b))),
        scratch_shapes=[pltpu.VMEM((1, ns), F32), pltpu.VMEM((1, ns), F32)]
        + [pltpu.VMEM((t, ns), F32)] * 4,
        compiler_params=_cparams("parallel", "arbitrary"),
    )(proj, dy, bbr, bbi, a_re, a_im, cbr, cbi, d_skip, car_r, car_i)


def _s5_glu_fwd(y1, proj, off_z, wglu):
    l, d = y1.shape

    def body(y_ref, z_ref, w_ref, o_ref):
        y2 = _gelu(y_ref[...])
        y3 = y2 * _sigmoid(_bdot(y2, w_ref[...]))
        o_ref[...] = (y3 * _silu(z_ref[...])).astype(BF16)

    return pl.pallas_call(
        body, name="s5_glu_fwd",
        out_shape=jax.ShapeDtypeStruct((l, d), BF16),
        grid=(l // ROW_TILE,),
        in_specs=[pl.BlockSpec((ROW_TILE, d), lambda i: (i, 0)),
                  pl.BlockSpec((ROW_TILE, d), lambda i: (i, off_z // d)),
                  pl.BlockSpec((d, d), lambda i: (0, 0))],
        out_specs=pl.BlockSpec((ROW_TILE, d), lambda i: (i, 0)),
        compiler_params=_cparams("parallel"),
    )(y1, proj, wglu)


def _s5_glu_bwd(y1, proj, off_z, wglu, dout):
    l, d = y1.shape

    def body(y_ref, z_ref, w_ref, do_ref, dy_ref, dz_ref, dw_ref):
        y2, gelu_vjp = jax.vjp(_gelu, y_ref[...])
        z = z_ref[...]
        sz, silu_vjp = jax.vjp(_silu, z)
        y2_bf = y2.astype(BF16)
        sg = _sigmoid(_dot(y2_bf, w_ref[...]))
        dout = do_ref[...]
        dy3 = dout * sz
        dz_ref[...] = silu_vjp(dout * (y2 * sg))[0].astype(BF16)
        dgl = (dy3 * y2 * sg * (1.0 - sg)).astype(BF16)
        dy2 = dy3 * sg + _dot(dgl, w_ref[...], ((1,), (1,)))
        dy_ref[...] = gelu_vjp(dy2)[0]

        @pl.when(pl.program_id(0) == 0)
        def _():
            dw_ref[...] = jnp.zeros_like(dw_ref)

        dw_ref[...] += _dot(y2_bf, dgl, ((0,), (0,)))

    row = pl.BlockSpec((ROW_TILE, d), lambda i: (i, 0))
    full = pl.BlockSpec((d, d), lambda i: (0, 0))
    return pl.pallas_call(
        body, name="s5_glu_bwd",
        out_shape=(jax.ShapeDtypeStruct((l, d), F32), jax.ShapeDtypeStruct((l, d), BF16),
                   jax.ShapeDtypeStruct((d, d), F32)),
        grid=(l // ROW_TILE,),
        in_specs=[row, pl.BlockSpec((ROW_TILE, d), lambda i: (i, off_z // d)), full, row],
        out_specs=(row, row, full),
        compiler_params=_cparams("arbitrary"),
    )(y1, proj, wglu, dout)


def _shift_rows(x, k, back=False):
    if k == 0:
        return x
    t = x.shape[0]
    row = lax.broadcasted_iota(jnp.int32, (t, 1), 0)
    if back:
        return jnp.where(row < t - k, pltpu.roll(x, t - k, 0), 0.0)
    return jnp.where(row >= k, pltpu.roll(x, k, 0), 0.0)


def _dn_conv(x, w_ref):
    return sum(w_ref[CONV_K - 1 - k:CONV_K - k, :] * _shift_rows(x, k) for k in range(CONV_K))


def _dn_post_conv(c, j):
    y = _silu(c)
    n = y * lax.rsqrt(jnp.sum(y * y, axis=-1, keepdims=True) + EPS)
    n = n * jnp.where(j < DN_HEADS, DN_HEAD_DIM ** -0.5, 1.0)
    return jnp.where(j < 2 * DN_HEADS, n, y)


def _dn_prep_fwd(proj, off_qkv, conv_w):
    l = proj.shape[0]
    hd = DN_HEAD_DIM
    nblk = 3 * DN_HEADS

    def body(x_ref, w_ref, o_ref):
        o_ref[...] = _dn_post_conv(_dn_conv(x_ref[...], w_ref), pl.program_id(0))

    return pl.pallas_call(
        body, name="dn_prep_fwd",
        out_shape=jax.ShapeDtypeStruct((l, nblk * hd), F32),
        grid=(nblk,),
        in_specs=[pl.BlockSpec((l, hd), lambda j: (0, off_qkv // hd + j)),
                  pl.BlockSpec((CONV_K, hd), lambda j: (0, j))],
        out_specs=pl.BlockSpec((l, hd), lambda j: (0, j)),
        compiler_params=_cparams("parallel"),
    )(proj, conv_w)


def _dn_prep_bwd(proj, off_qkv, conv_w, dqkv):
    l = proj.shape[0]
    hd = DN_HEAD_DIM
    nblk = 3 * DN_HEADS

    def body(x_ref, w_ref, do_ref, dx_ref, dw_ref):
        x = x_ref[...]
        j = pl.program_id(0)
        _, vjp = jax.vjp(functools.partial(_dn_post_conv, j=j), _dn_conv(x, w_ref))
        dc = vjp(do_ref[...])[0]
        dx = sum(w_ref[CONV_K - 1 - k:CONV_K - k, :] * _shift_rows(dc, k, back=True)
                 for k in range(CONV_K))
        dx_ref[...] = dx.astype(BF16)
        for k in range(CONV_K):
            dw_ref[CONV_K - 1 - k:CONV_K - k, :] = jnp.sum(dc * _shift_rows(x, k), axis=0,
                                                           keepdims=True)

    return pl.pallas_call(
        body, name="dn_prep_bwd",
        out_shape=(jax.ShapeDtypeStruct((l, nblk * hd), BF16),
                   jax.ShapeDtypeStruct((CONV_K, nblk * hd), F32)),
        grid=(nblk,),
        in_specs=[pl.BlockSpec((l, hd), lambda j: (0, off_qkv // hd + j)),
                  pl.BlockSpec((CONV_K, hd), lambda j: (0, j)),
                  pl.BlockSpec((None, l, hd), lambda j: (j // DN_HEADS, 0, j % DN_HEADS))],
        out_specs=(pl.BlockSpec((l, hd), lambda j: (0, j)),
                   pl.BlockSpec((CONV_K, hd), lambda j: (0, j))),
        compiler_params=_cparams("parallel"),
    )(proj, conv_w, dqkv)


def _dn_gate_fn(ba, a_log_row, dt_row):
    lane = lax.broadcasted_iota(jnp.int32, ba.shape, 1)
    beta = _sigmoid(ba)
    g = -jnp.exp(a_log_row) * _softplus(ba + dt_row)
    return jnp.where(lane < DN_HEADS, beta, jnp.where(lane < 2 * DN_HEADS, g, 0.0))


def _dn_gates_fwd(proj, off_ba, a_log_row, dt_row):
    l = proj.shape[0]
    row = pl.BlockSpec((ROW_TILE, 128), lambda i: (i, off_ba // 128))
    vec = pl.BlockSpec((1, 128), lambda i: (0, 0))

    def body(ba_ref, al_ref, dt_ref, o_ref):
        o_ref[...] = _dn_gate_fn(ba_ref[...], al_ref[...], dt_ref[...])

    return pl.pallas_call(
        body, name="dn_gates_fwd",
        out_shape=jax.ShapeDtypeStruct((l, 128), F32),
        grid=(l // ROW_TILE,),
        in_specs=[row, vec, vec],
        out_specs=pl.BlockSpec((ROW_TILE, 128), lambda i: (i, 0)),
        compiler_params=_cparams("parallel"),
    )(proj, a_log_row, dt_row)


def _dn_gates_bwd(proj, off_ba, a_log_row, dt_row, dgb_heads):
    l = proj.shape[0]
    nh = dgb_heads.shape[0]
    row = pl.BlockSpec((ROW_TILE, 128), lambda i: (i, off_ba // 128))
    vec = pl.BlockSpec((1, 128), lambda i: (0, 0))

    def body(ba_ref, al_ref, dt_ref, dg_ref, dba_ref, dal_ref, ddt_ref):
        _, vjp = jax.vjp(_dn_gate_fn, ba_ref[...], al_ref[...], dt_ref[...])
        dgb = dg_ref[0]
        for h in range(1, nh):
            dgb = dgb + dg_ref[h]
        dba, dal, ddt = vjp(dgb)
        dba_ref[...] = dba.astype(BF16)

        @pl.when(pl.program_id(0) == 0)
        def _():
            dal_ref[...] = jnp.zeros_like(dal_ref)
            ddt_ref[...] = jnp.zeros_like(ddt_ref)

        dal_ref[...] += dal
        ddt_ref[...] += ddt

    return pl.pallas_call(
        body, name="dn_gates_bwd",
        out_shape=(jax.ShapeDtypeStruct((l, 128), BF16), jax.ShapeDtypeStruct((1, 128), F32),
                   jax.ShapeDtypeStruct((1, 128), F32)),
        grid=(l // ROW_TILE,),
        in_specs=[row, vec, vec, pl.BlockSpec((nh, ROW_TILE, 128), lambda i: (0, i, 0))],
        out_specs=(pl.BlockSpec((ROW_TILE, 128), lambda i: (i, 0)), vec, vec),
        compiler_params=_cparams("arbitrary"),
    )(proj, a_log_row, dt_row, dgb_heads)


def _dn_chunk_fn(states, qs, ks, vs, gb, heads):
    c = qs[0].shape[0]
    each = lambda f, *lists: [f(*args) for args in zip(*lists)]
    lane = lax.broadcasted_iota(jnp.int32, gb.shape, 1)
    ri = lax.broadcasted_iota(jnp.int32, (c, c), 0)
    ci = lax.broadcasted_iota(jnp.int32, (c, c), 1)
    causal, strict = ri >= ci, ri > ci
    eye = (ri == ci).astype(F32)
    rowi = lax.broadcasted_iota(jnp.int32, (c, 1), 0)
    nt_dims = ((1,), (1,))
    hdot = functools.partial(_dot, precision=HIGHEST)

    pick = lambda m, at: jnp.sum(jnp.where(lane == at, m, 0.0), axis=1, keepdims=True)
    gb_cum = hdot(causal.astype(F32), gb)
    beta = [pick(gb, h) for h in heads]
    gc = [pick(gb_cum, h + DN_HEADS) for h in heads]
    gc_row = each(lambda g: jnp.sum(eye * g, axis=0, keepdims=True), gc)
    decay = each(lambda g, gr: jnp.where(causal, jnp.exp(jnp.where(causal, g - gr, 0.0)), 0.0),
                 gc, gc_row)
    kk = each(lambda k: _bdot(k, k, nt_dims), ks)
    a_mat = each(lambda b, m, dc: jnp.where(strict, b * m * dc, 0.0), beta, kk, decay)

    t_inv = each(lambda a: eye - a, a_mat)
    power = a_mat
    for _ in range(int(math.log2(c)) - 1):
        power = each(lambda p: _dot3(p, p), power)
        t_inv = each(lambda t, p: t + _dot3(t, p), t_inv, power)

    egc = each(jnp.exp, gc)
    u_c = each(lambda t, v, b: _dot3(t, v * b), t_inv, vs, beta)
    w_c = each(lambda t, k, b, e: _dot3(t, k * (b * e)), t_inv, ks, beta, egc)
    qk = each(lambda q, k, dc: _bdot(q, k, nt_dims) * dc, qs, ks, decay)
    g_end = each(lambda g: jnp.sum(jnp.where(rowi == c - 1, g, 0.0), axis=0, keepdims=True), gc)
    v_new = each(lambda u, w, s: u - _bdot(w, s), u_c, w_c, states)
    o = each(lambda q, e, s, m, vn: _bdot(q * e, s) + _bdot(m, vn), qs, egc, states, qk, v_new)
    new_states = each(
        lambda s, ge, k, g, vn: s * jnp.exp(ge) + _bdot(k * jnp.exp(ge - g), vn, ((0,), (0,))),
        states, g_end, ks, gc, v_new)
    return o, new_states


def _dn_chunk_specs(order):
    hd, nh, hps = DN_HEAD_DIM, DN_HEADS, DN_HEADS_PER_STEP
    qkv = lambda part: pl.BlockSpec((CHUNK, hps * hd), lambda h, n: (order(n), part * (nh // hps) + h))
    gb = pl.BlockSpec((CHUNK, 128), lambda h, n: (order(n), 0))
    state = pl.BlockSpec((hps, None, hd, hd), lambda h, n: (h, order(n), 0, 0))
    return qkv, gb, state


def _dn_chunk_fwd(qkv, gb):
    l = qkv.shape[0]
    hd, nh, hps = DN_HEAD_DIM, DN_HEADS, DN_HEADS_PER_STEP
    n_chunks = l // CHUNK
    qkv_spec, gb_spec, state_spec = _dn_chunk_specs(lambda n: n)

    def body(q_ref, k_ref, v_ref, gb_ref, o_ref, s_ref, state):
        @pl.when(pl.program_id(1) == 0)
        def _():
            state[...] = jnp.zeros_like(state)

        cols = [slice(i * hd, (i + 1) * hd) for i in range(hps)]
        states = [state[i] for i in range(hps)]
        for i in range(hps):
            s_ref[i] = states[i]
        o, new_states = _dn_chunk_fn(
            states, [q_ref[:, cs] for cs in cols], [k_ref[:, cs] for cs in cols],
            [v_ref[:, cs] for cs in cols], gb_ref[...],
            [pl.program_id(0) * hps + i for i in range(hps)])
        for i in range(hps):
            o_ref[:, cols[i]] = o[i]
            state[i] = new_states[i]

    return pl.pallas_call(
        body, name="dn_chunk_fwd",
        out_shape=(jax.ShapeDtypeStruct((l, nh * hd), F32),
                   jax.ShapeDtypeStruct((nh, n_chunks, hd, hd), F32)),
        grid=(nh // hps, n_chunks),
        in_specs=[qkv_spec(0), qkv_spec(1), qkv_spec(2), gb_spec],
        out_specs=(pl.BlockSpec((CHUNK, hps * hd), lambda h, n: (n, h)), state_spec),
        scratch_shapes=[pltpu.VMEM((hps, hd, hd), F32)],
        compiler_params=_cparams("parallel", "arbitrary"),
    )(qkv, qkv, qkv, gb)


def _dn_chunk_bwd(qkv, gb, states, do):
    l = qkv.shape[0]
    hd, nh, hps = DN_HEAD_DIM, DN_HEADS, DN_HEADS_PER_STEP
    n_chunks = l // CHUNK
    rev = lambda n: n_chunks - 1 - n
    qkv_spec, gb_spec, state_spec = _dn_chunk_specs(rev)

    def body(q_ref, k_ref, v_ref, gb_ref, s_ref, do_ref, dqkv_ref, dgb_ref, dstate):
        @pl.when(pl.program_id(1) == 0)
        def _():
            dstate[...] = jnp.zeros_like(dstate)

        cols = [slice(i * hd, (i + 1) * hd) for i in range(hps)]
        fn = functools.partial(_dn_chunk_fn, heads=[pl.program_id(0) * hps + i for i in range(hps)])
        _, vjp = jax.vjp(fn, [s_ref[i] for i in range(hps)], [q_ref[:, cs] for cs in cols],
                         [k_ref[:, cs] for cs in cols], [v_ref[:, cs] for cs in cols], gb_ref[...])
        ds, dq, dk, dv, dgb = vjp(([do_ref[:, cs] for cs in cols], [dstate[i] for i in range(hps)]))
        for i in range(hps):
            dstate[i] = ds[i]
            dqkv_ref[0, :, cols[i]] = dq[i]
            dqkv_ref[1, :, cols[i]] = dk[i]
            dqkv_ref[2, :, cols[i]] = dv[i]
        dgb_ref[...] = dgb

    head_out = pl.BlockSpec((CHUNK, hps * hd), lambda h, n: (rev(n), h))
    return pl.pallas_call(
        body, name="dn_chunk_bwd",
        out_shape=(jax.ShapeDtypeStruct((3, l, nh * hd), F32),
                   jax.ShapeDtypeStruct((nh // hps, l, 128), F32)),
        grid=(nh // hps, n_chunks),
        in_specs=[qkv_spec(0), qkv_spec(1), qkv_spec(2), gb_spec, state_spec, head_out],
        out_specs=(pl.BlockSpec((3, CHUNK, hps * hd), lambda h, n: (0, rev(n), h)),
                   pl.BlockSpec((None, CHUNK, 128), lambda h, n: (h, rev(n), 0))),
        scratch_shapes=[pltpu.VMEM((hps, hd, hd), F32)],
        compiler_params=_cparams("parallel", "arbitrary"),
    )(qkv, qkv, qkv, gb, states, do)


def _dn_out_fn(o, z, w):
    return _rmsnorm(o, w) * _silu(z)


def _dn_out_fwd(o, proj, off_z, w):
    l, d = o.shape
    hd = DN_HEAD_DIM
    blk = lambda off: pl.BlockSpec((ROW_TILE, hd), lambda i, h: (i, off // hd + h))

    def body(o_ref, z_ref, w_ref, out_ref):
        out_ref[...] = _dn_out_fn(o_ref[...], z_ref[...], w_ref[...]).astype(BF16)

    return pl.pallas_call(
        body, name="dn_out_fwd",
        out_shape=jax.ShapeDtypeStruct((l, d), BF16),
        grid=(l // ROW_TILE, d // hd),
        in_specs=[blk(0), blk(off_z), pl.BlockSpec((1, hd), lambda i, h: (0, 0))],
        out_specs=blk(0),
        compiler_params=_cparams("parallel", "parallel"),
    )(o, proj, w)


def _dn_out_bwd(o, proj, off_z, w, dout):
    l, d = o.shape
    hd = DN_HEAD_DIM
    blk = lambda off: pl.BlockSpec((ROW_TILE, hd), lambda i, h: (i, off // hd + h))
    vec = pl.BlockSpec((1, hd), lambda i, h: (0, 0))

    def body(o_ref, z_ref, w_ref, dout_ref, do_ref, dz_ref, dw_ref):
        _, vjp = jax.vjp(_dn_out_fn, o_ref[...], z_ref[...], w_ref[...])
        do, dz, dw = vjp(dout_ref[...])
        do_ref[...] = do
        dz_ref[...] = dz.astype(BF16)

        @pl.when((pl.program_id(0) == 0) & (pl.program_id(1) == 0))
        def _():
            dw_ref[...] = jnp.zeros_like(dw_ref)

        dw_ref[...] += dw

    return pl.pallas_call(
        body, name="dn_out_bwd",
        out_shape=(jax.ShapeDtypeStruct((l, d), F32), jax.ShapeDtypeStruct((l, d), BF16),
                   jax.ShapeDtypeStruct((1, hd), F32)),
        grid=(l // ROW_TILE, d // hd),
        in_specs=[blk(0), blk(off_z), vec, blk(0)],
        out_specs=(blk(0), blk(0), vec),
        compiler_params=_cparams("arbitrary", "arbitrary"),
    )(o, proj, w, dout)


def _tile_2d(rows, cols, budget_bytes=1 << 20):
    for tr in (rows, 4096, 2048, 1024, 512, 256, 128, 64, 32, 16):
        if tr <= rows and rows % tr == 0 and tr * cols * 4 <= budget_bytes:
            return tr, cols
    for tc in (2048, 1024, 512, 256, 128):
        if cols % tc == 0 and rows * tc * 4 <= 2 * budget_bytes:
            return rows, tc
    raise ValueError((rows, cols))


def _adamw_update(g, w_ref, m_ref, v_ref, go_ref, d_ref, mo_ref, vo_ref):
    c1 = 1.0 / (1.0 - ADAM_B1 ** ADAM_STEP)
    c2 = 1.0 / (1.0 - ADAM_B2 ** ADAM_STEP)
    m_new = ADAM_B1 * m_ref[...] + (1.0 - ADAM_B1) * g
    v_new = ADAM_B2 * v_ref[...] + (1.0 - ADAM_B2) * (g * g)
    go_ref[...] = g
    mo_ref[...] = m_new
    vo_ref[...] = v_new
    d_ref[...] = -ADAM_LR * ((m_new * c1) / (jnp.sqrt(v_new * c2) + ADAM_EPS) + ADAM_WD * w_ref[...])


def _adamw(w, m, v, gslots, name):
    rows, cols = w.shape
    ns = gslots.shape[0]
    tr, tc = _tile_2d(rows, cols)

    def body(w_ref, m_ref, v_ref, g_ref, go_ref, d_ref, mo_ref, vo_ref):
        g = g_ref[0].astype(F32)
        for s in range(1, ns):
            g = g + g_ref[s].astype(F32)
        _adamw_update(g, w_ref, m_ref, v_ref, go_ref, d_ref, mo_ref, vo_ref)

    blk = pl.BlockSpec((tr, tc), lambda i, j: (i, j))
    o = jax.ShapeDtypeStruct((rows, cols), F32)
    return pl.pallas_call(
        body, name=name, out_shape=(o, o, o, o),
        grid=(rows // tr, cols // tc),
        in_specs=[blk, blk, blk, pl.BlockSpec((ns, tr, tc), lambda i, j: (0, i, j))],
        out_specs=(blk, blk, blk, blk),
        compiler_params=_cparams("parallel", "parallel"),
    )(w, m, v, gslots)


def _slot_sum(gslots, name):
    ns, rows, cols = gslots.shape
    tr, tc = _tile_2d(rows, cols)

    def body(g_ref, o_ref):
        g = g_ref[0]
        for s in range(1, ns):
            g = g + g_ref[s]
        o_ref[...] = g

    return pl.pallas_call(
        body, name=name, out_shape=jax.ShapeDtypeStruct((rows, cols), F32),
        grid=(rows // tr, cols // tc),
        in_specs=[pl.BlockSpec((ns, tr, tc), lambda i, j: (0, i, j))],
        out_specs=pl.BlockSpec((tr, tc), lambda i, j: (i, j)),
        compiler_params=_cparams("parallel", "parallel"),
    )(gslots)


HBM_SPEC = pl.BlockSpec(memory_space=pl.ANY)


def _all_gather(arrs, name):
    n = len(arrs)

    def body(*refs):
        ins, outs = refs[:n], refs[n:2 * n]
        send_sems, recv_sems, local_sems = refs[2 * n:]
        x, y, c = lax.axis_index("x"), lax.axis_index("y"), lax.axis_index("c")
        me, sibling = (x, y, c), (x, y, 1 - c)
        chips = [(1 - x, y), (x, 1 - y), (1 - x, 1 - y)]
        index = lambda px, py, pc: 4 * px + 2 * py + pc

        def copy(a, k, block, to, src=None):
            rows = outs[a].at[index(*block)]
            return pltpu.make_async_remote_copy(
                src_ref=rows if src is None else src, dst_ref=rows,
                send_sem=send_sems.at[a, k], recv_sem=recv_sems.at[a, k],
                device_id=to, device_id_type=MESH)

        mine = [pltpu.make_async_copy(ins[a], outs[a].at[index(*me)], local_sems.at[a])
                for a in range(n)]
        for cp in mine:
            cp.start()
        first = []
        for a in range(n):
            first.append(copy(a, 0, me, sibling, src=ins[a]))
            first += [copy(a, 1 + j, me, (*chip, c), src=ins[a]) for j, chip in enumerate(chips)]
        for cp in first:
            cp.start()
        passed = []
        for j, chip in enumerate(chips):
            for a in range(n):
                copy(a, 1 + j, (*chip, c), me).wait_recv()
                fwd = copy(a, 4 + j, (*chip, c), sibling)
                fwd.start()
                passed.append(fwd)
        for a in range(n):
            copy(a, 0, sibling, me).wait_recv()
            for j, chip in enumerate(chips):
                copy(a, 4 + j, (*chip, 1 - c), me).wait_recv()
        for cp in first + passed:
            cp.wait_send()
        for cp in mine:
            cp.wait()

    return pl.pallas_call(
        body, name=name,
        out_shape=[jax.ShapeDtypeStruct((N_DEV,) + a.shape, a.dtype) for a in arrs],
        in_specs=[HBM_SPEC] * n, out_specs=[HBM_SPEC] * n,
        scratch_shapes=[pltpu.SemaphoreType.DMA((n, 7)), pltpu.SemaphoreType.DMA((n, 7)),
                        pltpu.SemaphoreType.DMA((n,))],
    )(*arrs)


def _sibling_swap(arrs, name):
    n = len(arrs)

    def body(*refs):
        ins, outs = refs[:n], refs[n:2 * n]
        send_sems, recv_sems = refs[2 * n:]
        x, y, c = lax.axis_index("x"), lax.axis_index("y"), lax.axis_index("c")
        copies = [pltpu.make_async_remote_copy(
            src_ref=ins[a].at[1 - c], dst_ref=outs[a],
            send_sem=send_sems.at[a], recv_sem=recv_sems.at[a],
            device_id=(x, y, 1 - c), device_id_type=MESH) for a in range(n)]
        for cp in copies:
            cp.start()
        for cp in copies:
            cp.wait()

    return pl.pallas_call(
        body, name=name,
        out_shape=[jax.ShapeDtypeStruct(a.shape[1:], a.dtype) for a in arrs],
        in_specs=[HBM_SPEC] * n, out_specs=[HBM_SPEC] * n,
        scratch_shapes=[pltpu.SemaphoreType.DMA((n,)), pltpu.SemaphoreType.DMA((n,))],
    )(*arrs)


def _pair_sum(mine, theirs, core, name):
    _, rows, cols = mine.shape
    tr, tc = _tile_2d(rows, cols, budget_bytes=2 << 20)

    def body(core_ref, a_ref, b_ref, o_ref):
        o_ref[...] = (a_ref[...].astype(F32) + b_ref[...].astype(F32)).astype(o_ref.dtype)

    return pl.pallas_call(
        body, name=name, out_shape=jax.ShapeDtypeStruct((rows, cols), mine.dtype),
        grid_spec=pltpu.PrefetchScalarGridSpec(
            num_scalar_prefetch=1, grid=(rows // tr, cols // tc),
            in_specs=[pl.BlockSpec((None, tr, tc), lambda i, j, core_ref: (core_ref[0], i, j)),
                      pl.BlockSpec((tr, tc), lambda i, j, core_ref: (i, j))],
            out_specs=pl.BlockSpec((tr, tc), lambda i, j, core_ref: (i, j))),
        compiler_params=_cparams("parallel", "parallel"),
    )(core, mine, theirs)


HBM_ONLY = pl.BlockSpec(memory_space=pltpu.HBM)
SEM_SPEC = pl.BlockSpec(memory_space=pltpu.SEMAPHORE)
SPLIT_COPY_EFFECT = pltpu.SideEffectType.DATAFLOW_SIDE_EFFECTING


def _chip_peers():
    x, y, c = lax.axis_index("x"), lax.axis_index("y"), lax.axis_index("c")
    flip = lambda v, bit: 1 - v if bit else v
    peers = []
    for k in range(1, 4):
        px, py = flip(x, k & 2), flip(y, k & 1)
        peers.append((2 * px + py, (px, py, c)))
    return 2 * x + y, peers


def _chip_exchange_start(arrs, name):
    n = len(arrs)
    n_sems = 3 * n

    def body(*refs):
        srcs, lands = refs[:n], refs[n:2 * n]
        send_sems, recv_sems = refs[2 * n:2 * n + n_sems], refs[2 * n + n_sems:2 * n + 2 * n_sems]
        token = refs[-1]
        me, peers = _chip_peers()
        for k, (peer_chip, peer_id) in enumerate(peers):
            for a in range(n):
                pltpu.make_async_remote_copy(
                    src_ref=srcs[a].at[peer_chip], dst_ref=lands[a].at[me],
                    send_sem=send_sems[3 * a + k], recv_sem=recv_sems[3 * a + k],
                    device_id=peer_id, device_id_type=MESH).start()
        token[...] = jnp.zeros_like(token)

    hbm = lambda a: pltpu.HBM(a.shape, a.dtype)
    operands = [pltpu.with_memory_space_constraint(a, pltpu.HBM) for a in arrs]
    operands += [pltpu.with_memory_space_constraint(lax.empty(a.shape, a.dtype), pltpu.HBM) for a in arrs]
    out = pl.pallas_call(
        body, name=name,
        out_shape=(*[pltpu.SemaphoreType.DMA(())] * (2 * n_sems),
                   *[hbm(a) for a in arrs], *[hbm(a) for a in arrs],
                   jax.ShapeDtypeStruct((8, 128), F32)),
        in_specs=[HBM_ONLY] * (2 * n),
        out_specs=(*[SEM_SPEC] * (2 * n_sems), *[HBM_ONLY] * (2 * n),
                   pl.BlockSpec(memory_space=pltpu.VMEM)),
        input_output_aliases={i: 2 * n_sems + i for i in range(2 * n)},
        compiler_params=pltpu.CompilerParams(has_side_effects=SPLIT_COPY_EFFECT),
    )(*operands)
    sems, rest = list(out[:2 * n_sems]), out[2 * n_sems:]
    return sems, list(rest[:n]), list(rest[n:2 * n]), rest[-1]


def _chip_exchange_wait(sems, srcs, lands, after, name):
    n = len(srcs)
    n_sems = 3 * n

    def body(*refs):
        src_refs, land_refs = refs[:n], refs[n:2 * n]
        send_sems, recv_sems = refs[2 * n:2 * n + n_sems], refs[2 * n + n_sems:2 * n + 2 * n_sems]
        me, peers = _chip_peers()
        for k, (peer_chip, peer_id) in enumerate(peers):
            for a in range(n):
                copy = pltpu.make_async_remote_copy(
                    src_ref=src_refs[a].at[peer_chip], dst_ref=land_refs[a].at[me],
                    send_sem=send_sems[3 * a + k], recv_sem=recv_sems[3 * a + k],
                    device_id=peer_id, device_id_type=MESH)
                copy.wait_send()
                copy.wait_recv()

    hbm = lambda a: pltpu.HBM(a.shape, a.dtype)
    out = pl.pallas_call(
        body, name=name,
        out_shape=(*[hbm(a) for a in srcs], *[hbm(a) for a in lands]),
        in_specs=[HBM_ONLY] * (2 * n) + [SEM_SPEC] * (2 * n_sems) + [pl.BlockSpec(memory_space=pl.ANY)],
        out_specs=tuple([HBM_ONLY] * (2 * n)),
        input_output_aliases={i: i for i in range(2 * n)},
        compiler_params=pltpu.CompilerParams(has_side_effects=SPLIT_COPY_EFFECT),
    )(*srcs, *lands, *sems, after)
    return list(out[n:])


def _adamw_exchanged(w, m, v, own, landed, chip, name):
    rows, cols = w.shape
    tr, tc = _tile_2d(rows, cols)

    def body(chip_ref, w_ref, m_ref, v_ref, own_ref, l1_ref, l2_ref, l3_ref, go_ref, d_ref, mo_ref, vo_ref):
        g = own_ref[...].astype(F32)
        for ref in (l1_ref, l2_ref, l3_ref):
            g = g + ref[...].astype(F32)
        _adamw_update(g, w_ref, m_ref, v_ref, go_ref, d_ref, mo_ref, vo_ref)

    blk = pl.BlockSpec((tr, tc), lambda i, j, chip_ref: (i, j))
    slot = lambda k: pl.BlockSpec((None, tr, tc), lambda i, j, chip_ref: (chip_ref[0] ^ k, i, j))
    o = jax.ShapeDtypeStruct((rows, cols), F32)
    return pl.pallas_call(
        body, name=name, out_shape=(o, o, o, o),
        grid_spec=pltpu.PrefetchScalarGridSpec(
            num_scalar_prefetch=1, grid=(rows // tr, cols // tc),
            in_specs=[blk, blk, blk, slot(0), slot(1), slot(2), slot(3)],
            out_specs=(blk, blk, blk, blk)),
        compiler_params=_cparams("parallel", "parallel"),
    )(chip, w, m, v, own, landed, landed, landed)


def _chip_exchange(arrs, name):
    n = len(arrs)

    def body(*refs):
        ins, outs = refs[:n], refs[n:2 * n]
        send_sems, recv_sems, local_sems = refs[2 * n:]
        x, y, c = lax.axis_index("x"), lax.axis_index("y"), lax.axis_index("c")
        me = 2 * x + y
        flip = lambda v, bit: 1 - v if bit else v
        mine = [pltpu.make_async_copy(ins[a].at[me], outs[a].at[me], local_sems.at[a])
                for a in range(n)]
        for cp in mine:
            cp.start()
        copies = []
        for k in range(1, 4):
            px, py = flip(x, k & 2), flip(y, k & 1)
            for a in range(n):
                copies.append(pltpu.make_async_remote_copy(
                    src_ref=ins[a].at[2 * px + py], dst_ref=outs[a].at[me],
                    send_sem=send_sems.at[a, k - 1], recv_sem=recv_sems.at[a, k - 1],
                    device_id=(px, py, c), device_id_type=MESH))
        for cp in copies:
            cp.start()
        for cp in copies:
            cp.wait()
        for cp in mine:
            cp.wait()

    return pl.pallas_call(
        body, name=name,
        out_shape=[jax.ShapeDtypeStruct(a.shape, a.dtype) for a in arrs],
        in_specs=[HBM_SPEC] * n, out_specs=[HBM_SPEC] * n,
        scratch_shapes=[pltpu.SemaphoreType.DMA((n, 3)), pltpu.SemaphoreType.DMA((n, 3)),
                        pltpu.SemaphoreType.DMA((n,))],
    )(*arrs)


def _block_diag(t):
    nb, gpb, r, c = t.shape
    eye = jnp.eye(gpb, dtype=t.dtype)
    return jnp.einsum("ngrc,gh->ngrhc", t, eye).reshape(nb, gpb * r, gpb * c)


def _diag_blocks(t, r, c):
    nb = t.shape[0]
    gpb = t.shape[1] // r
    t = t.reshape(nb, gpb, r, gpb, c)
    return jnp.einsum("ngrhc,gh->ngrc", t, jnp.eye(gpb, dtype=t.dtype))


def _pack_rows(parts):
    flat = jnp.concatenate([p.reshape(-1).astype(F32) for p in parts])
    pad = (-flat.shape[0]) % (256 * 128)
    return jnp.pad(flat, (0, pad)).reshape(-1, 128)


def _unpack_rows(packed, shapes):
    flat = packed.reshape(-1)
    out, at = [], 0
    for shape in shapes:
        size = math.prod(shape)
        out.append(flat[at:at + size].reshape(shape))
        at += size
    return out


def kernel(x, ln_w, w_in, s5_lam_re, s5_lam_im, s5_log_step, s5_b_re, s5_b_im, s5_c_re, s5_c_im, s5_d, s5_w_glu, s5_w_up, dn_conv_w, dn_a_log, dn_dt_bias, dn_norm_w, dn_w_up, w_out, final_norm_w, loss_target, m_ln_w, m_w_in, m_s5_lam_re, m_s5_lam_im, m_s5_log_step, m_s5_b_re, m_s5_b_im, m_s5_c_re, m_s5_c_im, m_s5_d, m_s5_w_glu, m_s5_w_up, m_dn_conv_w, m_dn_a_log, m_dn_dt_bias, m_dn_norm_w, m_dn_w_up, m_w_out, m_final_norm_w, v_ln_w, v_w_in, v_s5_lam_re, v_s5_lam_im, v_s5_log_step, v_s5_b_re, v_s5_b_im, v_s5_c_re, v_s5_c_im, v_s5_d, v_s5_w_glu, v_s5_w_up, v_dn_conv_w, v_dn_a_log, v_dn_dt_bias, v_dn_norm_w, v_dn_w_up, v_w_out, v_final_norm_w):
    weights = dict(ln_w=ln_w, w_in=w_in, s5_lam_re=s5_lam_re, s5_lam_im=s5_lam_im,
                   s5_log_step=s5_log_step, s5_b_re=s5_b_re, s5_b_im=s5_b_im, s5_c_re=s5_c_re,
                   s5_c_im=s5_c_im, s5_d=s5_d, s5_w_glu=s5_w_glu, s5_w_up=s5_w_up,
                   dn_conv_w=dn_conv_w, dn_a_log=dn_a_log, dn_dt_bias=dn_dt_bias,
                   dn_norm_w=dn_norm_w, dn_w_up=dn_w_up, w_out=w_out, final_norm_w=final_norm_w)
    mom_m = dict(ln_w=m_ln_w, w_in=m_w_in, s5_lam_re=m_s5_lam_re, s5_lam_im=m_s5_lam_im,
                 s5_log_step=m_s5_log_step, s5_b_re=m_s5_b_re, s5_b_im=m_s5_b_im,
                 s5_c_re=m_s5_c_re, s5_c_im=m_s5_c_im, s5_d=m_s5_d, s5_w_glu=m_s5_w_glu,
                 s5_w_up=m_s5_w_up, dn_conv_w=m_dn_conv_w, dn_a_log=m_dn_a_log,
                 dn_dt_bias=m_dn_dt_bias, dn_norm_w=m_dn_norm_w, dn_w_up=m_dn_w_up,
                 w_out=m_w_out, final_norm_w=m_final_norm_w)
    mom_v = dict(ln_w=v_ln_w, w_in=v_w_in, s5_lam_re=v_s5_lam_re, s5_lam_im=v_s5_lam_im,
                 s5_log_step=v_s5_log_step, s5_b_re=v_s5_b_re, s5_b_im=v_s5_b_im,
                 s5_c_re=v_s5_c_re, s5_c_im=v_s5_c_im, s5_d=v_s5_d, s5_w_glu=v_s5_w_glu,
                 s5_w_up=v_s5_w_up, dn_conv_w=v_dn_conv_w, dn_a_log=v_dn_a_log,
                 dn_dt_bias=v_dn_dt_bias, dn_norm_w=v_dn_norm_w, dn_w_up=v_dn_w_up,
                 w_out=v_w_out, final_norm_w=v_final_norm_w)
    names = list(weights)

    l, d = x.shape[1], x.shape[2]
    d_s5 = d // 2
    groups = d_s5 // S5_GROUP
    nb = groups // S5_GPB
    d_dn = DN_HEADS * DN_HEAD_DIM
    w_in_cols = w_in.shape[2]
    d_in = N_DEV * w_in_cols
    off_ba_src = 2 * d_s5 + 4 * d_dn
    off_u, off_zs, off_qkv, off_zd = 0, d_s5, 2 * d_s5, 2 * d_s5 + 3 * d_dn
    off_ba = off_zd + d_dn
    off_gs = off_ba + BA_PAD
    off_gd = off_gs + d
    n_proj = off_gd + d
    x2d, tgt2d = x[0], loss_target[0]
    my_index = 4 * lax.axis_index("x") + 2 * lax.axis_index("y") + lax.axis_index("c")

    g_win, g_glu, g_sup, g_dup, g_wout, g_conv = _all_gather(
        [jnp.transpose(w_in[0]).astype(BF16), s5_w_glu[0].astype(BF16), s5_w_up[0].astype(BF16),
         dn_w_up[0].astype(BF16), w_out[0].astype(BF16), dn_conv_w[0]], name="gather_weights")
    ba_end = off_ba_src + 2 * DN_HEADS
    ba_zeros = BA_PAD - 2 * DN_HEADS
    w_full_t = g_win.reshape(d_in, d)
    w_cat_t = jnp.concatenate(
        [w_full_t[:ba_end], jnp.zeros((ba_zeros, d), BF16), w_full_t[ba_end:]], axis=0)
    wglu_full = g_glu.reshape(d_s5, d_s5)
    wsup_full = jnp.transpose(g_sup, (1, 0, 2)).reshape(d_s5, d)
    wdup_full = jnp.transpose(g_dup, (1, 0, 2)).reshape(d_dn, d)
    wout_full = g_wout.reshape(d, d)
    conv_full = jnp.transpose(g_conv, (1, 0, 2)).reshape(CONV_K, 3 * d_dn)

    lam_re, lam_im = s5_lam_re[0], s5_lam_im[0]
    log_step = s5_log_step[0].reshape(groups, 1)
    b_re = s5_b_re[0].reshape(groups * S5_STATE, S5_GROUP)
    b_im = s5_b_im[0].reshape(groups * S5_STATE, S5_GROUP)
    abar_re, abar_im, f_re, f_im = _s5_disc_fwd(lam_re, lam_im, log_step)
    f_re_col, f_im_col = f_re.reshape(-1, 1), f_im.reshape(-1, 1)
    bb_re, bb_im = _s5_bbar_fwd(f_re_col, f_im_col, b_re, b_im)

    def bb_blocks(t):
        t = t.reshape(nb, S5_GPB, S5_STATE, S5_GROUP).transpose(0, 1, 3, 2)
        return _block_diag(t).astype(BF16)

    def c_blocks(t):
        return _block_diag(t.reshape(nb, S5_GPB, S5_GROUP, S5_STATE)).astype(BF16)

    bbr, bbi = bb_blocks(bb_re), bb_blocks(bb_im)
    cbr, cbi = c_blocks(s5_c_re[0]), c_blocks(s5_c_im[0])
    ctr, cti = jnp.transpose(cbr, (0, 2, 1)), jnp.transpose(cbi, (0, 2, 1))
    a_re = abar_re.reshape(nb, 1, S5_GPB * S5_STATE)
    a_im = abar_im.reshape(nb, 1, S5_GPB * S5_STATE)

    h = _rms_fwd(x2d, ln_w)
    proj = _mm(h, w_cat_t, tb=True, tn=1536, name="proj")
    y1, car_r, car_i = _s5_fwd(proj, bbr, bbi, a_re, a_im, ctr, cti, s5_d, d_s5)
    out_s = _s5_glu_fwd(y1, proj, off_zs, wglu_full)
    y_s = _mm(out_s, wsup_full, name="s5_up")

    a_log_row = jnp.pad(dn_a_log, ((0, 0), (DN_HEADS, 128 - 2 * DN_HEADS)))
    dt_row = jnp.pad(dn_dt_bias, ((0, 0), (DN_HEADS, 128 - 2 * DN_HEADS)))
    qkv = _dn_prep_fwd(proj, off_qkv, conv_full)
    gb = _dn_gates_fwd(proj, off_ba, a_log_row, dt_row)
    o_dn, states = _dn_chunk_fwd(qkv, gb)
    out_d = _dn_out_fwd(o_dn, proj, off_zd, dn_norm_w)
    y_d = _mm(out_d, wdup_full, name="dn_up")

    mixed = _merge_fwd(proj, off_gs, off_gd, y_s, y_d)
    branch = _mm(mixed, wout_full, name="w_out")
    dx2, dx2_bf, loss_dev, d_final_w = _final(x2d, branch, final_norm_w.reshape(1, d), tgt2d)

    g_wout_full = _mm(mixed, dx2_bf, ta=True, out_dtype=BF16, name="grad_w_out")
    dmixed = _mm(dx2_bf, wout_full, tb=True, name="d_mixed")
    dgs, dgd, dys, dyd = _merge_bwd(proj, off_gs, off_gd, y_s, y_d, dmixed)

    g_dup_full = _mm(out_d, dyd, ta=True, out_dtype=BF16, name="grad_dn_up")
    dout_d = _mm(dyd, wdup_full, tb=True, name="d_out_d")
    do_dn, dzd, d_norm_w = _dn_out_bwd(o_dn, proj, off_zd, dn_norm_w, dout_d)
    dqkv, dgb_heads = _dn_chunk_bwd(qkv, gb, states, do_dn)
    dba, d_a_log_row, d_dt_row = _dn_gates_bwd(proj, off_ba, a_log_row, dt_row, dgb_heads)
    dqkv_pre, d_conv_full = _dn_prep_bwd(proj, off_qkv, conv_full, dqkv)

    g_sup_full = _mm(out_s, dys, ta=True, out_dtype=BF16, name="grad_s5_up")
    dout_s = _mm(dys, wsup_full, tb=True, name="d_out_s")
    dy1, dzs, g_glu_full = _s5_glu_bwd(y1, proj, off_zs, wglu_full, dout_s)

    def by_dest(t, axis):
        if axis == 1:
            return t.reshape(t.shape[0], 4, 2, t.shape[1] // N_DEV).transpose(2, 1, 0, 3)
        return t.reshape(4, 2, t.shape[0] // N_DEV, t.shape[1]).transpose(1, 0, 2, 3)

    core = lax.axis_index("c").astype(jnp.int32).reshape(1)
    chip = (2 * lax.axis_index("x") + lax.axis_index("y")).astype(jnp.int32).reshape(1)

    def chip_sums_of(which, parts, tag):
        from_sibling = _sibling_swap(parts, name="swap_grads_" + tag)
        return [_pair_sum(p.reshape(2, -1, p.shape[-1]), got.reshape(-1, got.shape[-1]), core,
                          name="pair_sum_" + nm).reshape(got.shape)
                for nm, p, got in zip(which, parts, from_sibling)]

    early = ["s5_w_glu", "s5_w_up", "dn_w_up", "w_out"]
    sums_a = chip_sums_of(early, [by_dest(g_glu_full.astype(BF16), 0), by_dest(g_sup_full, 1),
                                  by_dest(g_dup_full, 1), by_dest(g_wout_full, 0)], "a")
    sems_a, src_a, land_a, token_a = _chip_exchange_start(sums_a, name="exchange_start_a")

    (du, d_a_re, d_a_im, d_bbr, d_bbi, d_cbr, d_cbi, d_s5_d) = _s5_bwd(
        proj, dy1, bbr, bbi, a_re, a_im, cbr, cbi, s5_d + token_a[:1, :1], car_r, car_i)

    def from_bb_blocks(t):
        t = _diag_blocks(t, S5_GROUP, S5_STATE).transpose(0, 1, 3, 2)
        return t.reshape(groups * S5_STATE, S5_GROUP)

    d_f_re, d_f_im, d_b_re, d_b_im = _s5_bbar_bwd(f_re_col, f_im_col, b_re, b_im,
                                                 from_bb_blocks(d_bbr), from_bb_blocks(d_bbi))
    d_lam_re, d_lam_im, d_log_step = _s5_disc_bwd(
        lam_re, lam_im, log_step, d_a_re.reshape(groups, S5_STATE), d_a_im.reshape(groups, S5_STATE),
        d_f_re.reshape(groups, S5_STATE), d_f_im.reshape(groups, S5_STATE))
    d_c_re = _diag_blocks(d_cbr, S5_GROUP, S5_STATE).reshape(groups, S5_GROUP, S5_STATE)
    d_c_im = _diag_blocks(d_cbi, S5_GROUP, S5_STATE).reshape(groups, S5_GROUP, S5_STATE)

    dproj = jnp.concatenate(
        [du, dzs, dqkv_pre, dzd, jnp.pad(dba, ((0, 0), (0, BA_PAD - 128))), dgs, dgd], axis=1)
    g_wcat_t = _mm(dproj, h, ta=True, out_dtype=BF16, tm=768, tn=d, name="grad_w_in")
    g_win_full_t = jnp.concatenate([g_wcat_t[:ba_end], g_wcat_t[ba_end + ba_zeros:]], axis=0)
    sums_b = chip_sums_of(["w_in"], [by_dest(g_win_full_t, 0)], "b")
    sems_b, src_b, land_b, token_b = _chip_exchange_start(sums_b, name="exchange_start_b")
    dh = _mm(dproj, w_cat_t, tm=1024, tn=1024, tk=1536, after=token_b, name="d_h")
    grad_x, d_ln_w = _rms_bwd(x2d, ln_w, dh, dx2)
    big = ["w_in"] + early
    results = {}

    small = [nm for nm in names if nm not in big]
    small_grads = dict(
        ln_w=d_ln_w, s5_lam_re=d_lam_re, s5_lam_im=d_lam_im, s5_log_step=d_log_step,
        s5_b_re=d_b_re, s5_b_im=d_b_im, s5_c_re=d_c_re, s5_c_im=d_c_im, s5_d=d_s5_d,
        dn_conv_w=d_conv_full, dn_a_log=d_a_log_row[:, DN_HEADS:2 * DN_HEADS],
        dn_dt_bias=d_dt_row[:, DN_HEADS:2 * DN_HEADS], dn_norm_w=d_norm_w, final_norm_w=d_final_w)
    (all_small,) = _all_gather([_pack_rows([small_grads[nm] for nm in small])], name="gather_small_grads")
    summed = _slot_sum(all_small, name="sum_small_grads")
    full_shapes = [(CONV_K, 3 * d_dn) if nm == "dn_conv_w" else weights[nm].shape for nm in small]
    g_small = dict(zip(small, _unpack_rows(summed, full_shapes)))
    conv_cols = dn_conv_w.shape[2]
    g_small["dn_conv_w"] = lax.dynamic_slice_in_dim(
        g_small["dn_conv_w"], my_index * conv_cols, conv_cols, axis=1).reshape(dn_conv_w.shape)
    packed = [_pack_rows([t[nm] for nm in small]) for t in (weights, mom_m, mom_v, g_small)]
    small_out = _adamw(packed[0], packed[1], packed[2], packed[3][None], name="adamw_small")
    small_shapes = [weights[nm].shape for nm in small]
    for kind, packed_out in enumerate(small_out):
        for nm, val in zip(small, _unpack_rows(packed_out, small_shapes)):
            results.setdefault(nm, [None] * 4)[kind] = val

    land_a = _chip_exchange_wait(sems_a, src_a, land_a, small_out[0], name="exchange_wait_a")
    for nm, own, landed in zip(early, src_a, land_a):
        results[nm] = _adamw_exchanged(weights[nm][0], mom_m[nm][0], mom_v[nm][0], own, landed, chip,
                                       name="adamw_" + nm)
    (land_b,) = _chip_exchange_wait(sems_b, src_b, land_b, results[early[-1]][0], name="exchange_wait_b")
    res = _adamw_exchanged(jnp.transpose(w_in[0]), jnp.transpose(m_w_in[0]), jnp.transpose(v_w_in[0]),
                           src_b[0], land_b, chip, name="adamw_w_in")
    results["w_in"] = [jnp.transpose(t) for t in res]

    loss = lax.psum(loss_dev[0, 0], ("x", "y", "c"))
    outs = [loss, grad_x[None]]
    for kind in range(4):
        outs += [results[nm][kind].reshape(weights[nm].shape) for nm in names]
    return tuple(outs)
```

```python
import functools
import math

import jax
import jax.numpy as jnp
from jax import lax
from jax.experimental import pallas as pl
from jax.experimental.pallas import tpu as pltpu

F32 = jnp.float32
BF16 = jnp.bfloat16
HIGHEST = lax.Precision.HIGHEST
MESH = pl.DeviceIdType.MESH
N_DEV = 8

EPS = 1e-6
S5_GROUP = 16
S5_STATE = 64
S5_GPB = 8
S5_T = 256
DN_HEADS = 8
DN_HEAD_DIM = 128
CHUNK = 64
DN_HEADS_PER_STEP = 8
CONV_K = 4
BA_PAD = 512

ADAM_LR = 0.001
ADAM_B1 = 0.9
ADAM_B2 = 0.999
ADAM_EPS = 1e-08
ADAM_WD = 0.01
ADAM_STEP = 10

VMEM_LIMIT_BYTES = 48 * 1024 * 1024
ROW_TILE = 256


def _cparams(*sem):
    return pltpu.CompilerParams(dimension_semantics=sem if sem else None,
                                vmem_limit_bytes=VMEM_LIMIT_BYTES)


def _sigmoid(x):
    return 1.0 / (1.0 + jnp.exp(-x))


def _silu(x):
    return x * _sigmoid(x)


def _gelu(x):
    return 0.5 * x * (1.0 + jnp.tanh(0.7978845608028654 * (x + 0.044715 * x * x * x)))


def _softplus(x):
    return jnp.maximum(x, 0.0) + jnp.log(1.0 + jnp.exp(-jnp.abs(x)))


def _rmsnorm(x, w):
    return x * lax.rsqrt(jnp.mean(x * x, axis=-1, keepdims=True) + EPS) * w


def _dot(a, b, dims=((1,), (0,)), precision=None):
    return lax.dot_general(a, b, (dims, ((), ())), precision=precision,
                           preferred_element_type=F32)


def _bdot(a, b, dims=((1,), (0,))):
    return _dot(a.astype(BF16), b.astype(BF16), dims)


def _split_bf16(a):
    hi = a.astype(BF16)
    return hi, (a - hi.astype(F32)).astype(BF16)


def _dot3_dims(a, b, dims):
    ah, al = _split_bf16(a)
    bh, bl = _split_bf16(b)
    return _dot(ah, bh, dims) + (_dot(ah, bl, dims) + _dot(al, bh, dims))


@jax.custom_vjp
def _dot3(a, b):
    return _dot3_dims(a, b, ((1,), (0,)))


def _dot3_fwd(a, b):
    return _dot3(a, b), (a, b)


def _dot3_bwd(res, g):
    a, b = res
    return _dot3_dims(g, b, ((1,), (1,))), _dot3_dims(a, g, ((0,), (0,)))


_dot3.defvjp(_dot3_fwd, _dot3_bwd)


def _mm(a, b, *, ta=False, tb=False, out_dtype=F32, tm=512, tn=512, tk=None, after=None, name):
    k_dim, m_dim = (a.shape if ta else a.shape[::-1])
    n_dim = b.shape[0] if tb else b.shape[1]
    assert (b.shape[1] if tb else b.shape[0]) == k_dim
    tm, tn = min(tm, m_dim), min(tn, n_dim)
    tk = k_dim if tk is None else tk
    assert m_dim % tm == 0 and n_dim % tn == 0 and k_dim % tk == 0
    nk = k_dim // tk
    a_spec = (pl.BlockSpec((tk, tm), lambda i, j, k: (k, i)) if ta
              else pl.BlockSpec((tm, tk), lambda i, j, k: (i, k)))
    b_spec = (pl.BlockSpec((tn, tk), lambda i, j, k: (j, k)) if tb
              else pl.BlockSpec((tk, tn), lambda i, j, k: (k, j)))
    dims = ((0 if ta else 1,), (1 if tb else 0,))

    def body(a_ref, b_ref, *rest):
        o_ref, *scratch = rest[1:] if after is not None else rest
        p = _bdot(a_ref[...], b_ref[...], dims)
        if nk == 1:
            o_ref[...] = p.astype(o_ref.dtype)
        else:
            acc = scratch[0]
            k = pl.program_id(2)

            @pl.when(k == 0)
            def _():
                acc[...] = p

            @pl.when(k > 0)
            def _():
                acc[...] += p

            @pl.when(k == nk - 1)
            def _():
                o_ref[...] = acc[...].astype(o_ref.dtype)

    return pl.pallas_call(
        body, name=name,
        out_shape=jax.ShapeDtypeStruct((m_dim, n_dim), out_dtype),
        grid=(m_dim // tm, n_dim // tn, nk),
        in_specs=[a_spec, b_spec] + ([pl.BlockSpec((8, 128), lambda i, j, k: (0, 0))]
                                     if after is not None else []),
        out_specs=pl.BlockSpec((tm, tn), lambda i, j, k: (i, j)),
        scratch_shapes=[pltpu.VMEM((tm, tn), F32)] if nk > 1 else [],
        compiler_params=_cparams("parallel", "parallel", "arbitrary"),
    )(a, b, *([after] if after is not None else []))


def _rms_fwd(x, w):
    l, d = x.shape

    def body(x_ref, w_ref, h_ref):
        h_ref[...] = _rmsnorm(x_ref[...], w_ref[...]).astype(BF16)

    return pl.pallas_call(
        body, name="rms_fwd",
        out_shape=jax.ShapeDtypeStruct((l, d), BF16),
        grid=(l // ROW_TILE,),
        in_specs=[pl.BlockSpec((ROW_TILE, d), lambda i: (i, 0)),
                  pl.BlockSpec((1, d), lambda i: (0, 0))],
        out_specs=pl.BlockSpec((ROW_TILE, d), lambda i: (i, 0)),
        compiler_params=_cparams("parallel"),
    )(x, w)


def _rms_bwd(x, w, dh, dres):
    l, d = x.shape

    def body(x_ref, w_ref, dh_ref, dres_ref, dx_ref, dw_ref):
        _, vjp = jax.vjp(_rmsnorm, x_ref[...], w_ref[...])
        dx, dw = vjp(dh_ref[...])
        dx_ref[...] = dx + dres_ref[...]

        @pl.when(pl.program_id(0) == 0)
        def _():
            dw_ref[...] = jnp.zeros_like(dw_ref)

        dw_ref[...] += dw

    row = pl.BlockSpec((ROW_TILE, d), lambda i: (i, 0))
    vec = pl.BlockSpec((1, d), lambda i: (0, 0))
    return pl.pallas_call(
        body, name="rms_bwd",
        out_shape=(jax.ShapeDtypeStruct((l, d), F32), jax.ShapeDtypeStruct((1, d), F32)),
        grid=(l // ROW_TILE,),
        in_specs=[row, vec, row, row],
        out_specs=(row, vec),
        compiler_params=_cparams("arbitrary"),
    )(x, w, dh, dres)


def _final(x, r, fw, target):
    l, d = x.shape

    def per_row_loss(x2, w, tgt):
        err = _rmsnorm(x2, w) - tgt
        return 0.5 * jnp.mean(err * err, axis=-1, keepdims=True)

    def body(x_ref, r_ref, w_ref, t_ref, dx_ref, dxb_ref, loss_ref, dw_ref):
        x2 = x_ref[...] + r_ref[...]
        rows, vjp = jax.vjp(functools.partial(per_row_loss, tgt=t_ref[...]), x2, w_ref[...])
        dx2, dw = vjp(jnp.ones_like(rows))
        dx_ref[...] = dx2
        dxb_ref[...] = dx2.astype(BF16)

        @pl.when(pl.program_id(0) == 0)
        def _():
            dw_ref[...] = jnp.zeros_like(dw_ref)
            loss_ref[...] = jnp.zeros_like(loss_ref)

        dw_ref[...] += dw
        loss_ref[...] += jnp.sum(rows, axis=0, keepdims=True)

    row = pl.BlockSpec((ROW_TILE, d), lambda i: (i, 0))
    vec = pl.BlockSpec((1, d), lambda i: (0, 0))
    return pl.pallas_call(
        body, name="final_norm_loss",
        out_shape=(jax.ShapeDtypeStruct((l, d), F32), jax.ShapeDtypeStruct((l, d), BF16),
                   jax.ShapeDtypeStruct((1, 1), F32), jax.ShapeDtypeStruct((1, d), F32)),
        grid=(l // ROW_TILE,),
        in_specs=[row, row, vec, row],
        out_specs=(row, row, pl.BlockSpec((1, 1), lambda i: (0, 0)), vec),
        compiler_params=_cparams("arbitrary"),
    )(x, r, fw, target)


def _merge_fn(gs, gd, ys, yd):
    return _sigmoid(gs) * ys + _sigmoid(gd) * yd


def _merge_fwd(proj, off_gs, off_gd, ys, yd):
    l, d = ys.shape
    cw = 512
    blk = lambda off: pl.BlockSpec((ROW_TILE, cw), lambda i, j: (i, off // cw + j))

    def body(gs_ref, gd_ref, ys_ref, yd_ref, o_ref):
        o_ref[...] = _merge_fn(gs_ref[...], gd_ref[...], ys_ref[...], yd_ref[...]).astype(BF16)

    return pl.pallas_call(
        body, name="merge_fwd",
        out_shape=jax.ShapeDtypeStruct((l, d), BF16),
        grid=(l // ROW_TILE, d // cw),
        in_specs=[blk(off_gs), blk(off_gd), blk(0), blk(0)],
        out_specs=blk(0),
        compiler_params=_cparams("parallel", "parallel"),
    )(proj, proj, ys, yd)


def _merge_bwd(proj, off_gs, off_gd, ys, yd, dmixed):
    l, d = ys.shape
    cw = 512
    blk = lambda off: pl.BlockSpec((ROW_TILE, cw), lambda i, j: (i, off // cw + j))

    def body(gs_ref, gd_ref, ys_ref, yd_ref, dm_ref, dgs_ref, dgd_ref, dys_ref, dyd_ref):
        _, vjp = jax.vjp(_merge_fn, gs_ref[...], gd_ref[...], ys_ref[...], yd_ref[...])
        dgs, dgd, dys, dyd = vjp(dm_ref[...])
        dgs_ref[...] = dgs.astype(BF16)
        dgd_ref[...] = dgd.astype(BF16)
        dys_ref[...] = dys.astype(BF16)
        dyd_ref[...] = dyd.astype(BF16)

    out = jax.ShapeDtypeStruct((l, d), BF16)
    return pl.pallas_call(
        body, name="merge_bwd",
        out_shape=(out, out, out, out),
        grid=(l // ROW_TILE, d // cw),
        in_specs=[blk(off_gs), blk(off_gd), blk(0), blk(0), blk(0)],
        out_specs=(blk(0), blk(0), blk(0), blk(0)),
        compiler_params=_cparams("parallel", "parallel"),
    )(proj, proj, ys, yd, dmixed)


def _s5_disc_fn(lam_re, lam_im, log_step):
    step = jnp.exp(log_step)
    mag = jnp.exp(lam_re * step)
    abar_re = mag * jnp.cos(lam_im * step)
    abar_im = mag * jnp.sin(lam_im * step)
    den = lam_re * lam_re + lam_im * lam_im
    xr = abar_re - 1.0
    f_re = (xr * lam_re + abar_im * lam_im) / den
    f_im = (abar_im * lam_re - xr * lam_im) / den
    return abar_re, abar_im, f_re, f_im


def _s5_disc_fwd(lam_re, lam_im, log_step):
    g, p = lam_re.shape

    def body(lr_ref, li_ref, ls_ref, ar_ref, ai_ref, fr_ref, fi_ref):
        ar, ai, fr, fi = _s5_disc_fn(lr_ref[...], li_ref[...], ls_ref[...])
        ar_ref[...] = ar
        ai_ref[...] = ai
        fr_ref[...] = fr
        fi_ref[...] = fi

    o = jax.ShapeDtypeStruct((g, p), F32)
    return pl.pallas_call(body, name="s5_disc_fwd", out_shape=(o, o, o, o),
                          compiler_params=_cparams())(lam_re, lam_im, log_step)


def _s5_disc_bwd(lam_re, lam_im, log_step, dar, dai, dfr, dfi):
    g, p = lam_re.shape

    def body(lr_ref, li_ref, ls_ref, dar_ref, dai_ref, dfr_ref, dfi_ref, dlr_ref, dli_ref, dls_ref):
        _, vjp = jax.vjp(_s5_disc_fn, lr_ref[...], li_ref[...], ls_ref[...])
        dlr, dli, dls = vjp((dar_ref[...], dai_ref[...], dfr_ref[...], dfi_ref[...]))
        dlr_ref[...] = dlr
        dli_ref[...] = dli
        dls_ref[...] = dls

    o = jax.ShapeDtypeStruct((g, p), F32)
    return pl.pallas_call(body, name="s5_disc_bwd",
                          out_shape=(o, o, jax.ShapeDtypeStruct((g, 1), F32)),
                          compiler_params=_cparams())(lam_re, lam_im, log_step, dar, dai, dfr, dfi)


def _s5_bbar_fwd(f_re, f_im, b_re, b_im):
    n, c = b_re.shape

    def body(fr_ref, fi_ref, br_ref, bi_ref, or_ref, oi_ref):
        fr, fi, br, bi = fr_ref[...], fi_ref[...], br_ref[...], bi_ref[...]
        or_ref[...] = fr * br - fi * bi
        oi_ref[...] = fr * bi + fi * br

    o = jax.ShapeDtypeStruct((n, c), F32)
    return pl.pallas_call(body, name="s5_bbar_fwd", out_shape=(o, o),
                          compiler_params=_cparams())(f_re, f_im, b_re, b_im)


def _s5_bbar_bwd(f_re, f_im, b_re, b_im, dbr, dbi):
    n, c = b_re.shape

    def body(fr_ref, fi_ref, br_ref, bi_ref, dor_ref, doi_ref, dfr_ref, dfi_ref, dbr_ref, dbi_ref):
        fr, fi, br, bi = fr_ref[...], fi_ref[...], br_ref[...], bi_ref[...]
        dor, doi = dor_ref[...], doi_ref[...]
        dfr_ref[...] = jnp.sum(dor * br + doi * bi, axis=-1, keepdims=True)
        dfi_ref[...] = jnp.sum(doi * br - dor * bi, axis=-1, keepdims=True)
        dbr_ref[...] = fr * dor + fi * doi
        dbi_ref[...] = fr * doi - fi * dor

    col = jax.ShapeDtypeStruct((n, 1), F32)
    o = jax.ShapeDtypeStruct((n, c), F32)
    return pl.pallas_call(body, name="s5_bbar_bwd", out_shape=(col, col, o, o),
                          compiler_params=_cparams())(f_re, f_im, b_re, b_im, dbr, dbi)


SUBLANES = 8


def _scan_groups(xr, xi, ar, ai, reverse):
    t = xr.shape[0]
    sub = lax.broadcasted_iota(jnp.int32, (t, 1), 0) & (SUBLANES - 1)
    pr, pi = ar, ai
    for sh in (1, 2, 4):
        if reverse:
            keep = sub < SUBLANES - sh
            sr, si = pltpu.roll(xr, t - sh, 0), pltpu.roll(xi, t - sh, 0)
        else:
            keep = sub >= sh
            sr, si = pltpu.roll(xr, sh, 0), pltpu.roll(xi, sh, 0)
        sr = jnp.where(keep, sr, 0.0)
        si = jnp.where(keep, si, 0.0)
        xr, xi = xr + pr * sr - pi * si, xi + pr * si + pi * sr
        pr, pi = pr * pr - pi * pi, 2.0 * pr * pi
    return xr, xi


def _scan_rows(xr, xi, ar, ai, cr, ci, sr_ref, si_ref, reverse):
    t, n = xr.shape
    xr, xi = _scan_groups(xr, xi, ar, ai, reverse)
    sr_ref[...] = xr
    si_ref[...] = xi
    sub = lax.broadcasted_iota(jnp.int32, (SUBLANES, n), 0)
    seed = sub == (SUBLANES - 1 if reverse else 0)
    pwr, pwi = _scan_groups(jnp.where(seed, ar, 0.0), jnp.where(seed, ai, 0.0), ar, ai, reverse)
    groups = range(t // SUBLANES)
    edge = 0 if reverse else SUBLANES - 1
    for g in (reversed(groups) if reverse else groups):
        rows = slice(g * SUBLANES, (g + 1) * SUBLANES)
        vr = sr_ref[rows, :] + (pwr * cr - pwi * ci)
        vi = si_ref[rows, :] + (pwr * ci + pwi * cr)
        sr_ref[rows, :] = vr
        si_ref[rows, :] = vi
        cr, ci = vr[edge:edge + 1, :], vi[edge:edge + 1, :]
    return cr, ci


def _s5_states(u_bf, bbr, bbi, ar, ai, cr, ci, sr_ref, si_ref):
    return _scan_rows(_dot(u_bf, bbr), _dot(u_bf, bbi), ar, ai, cr, ci, sr_ref, si_ref, reverse=False)


def _s5_fwd(proj, bbr, bbi, a_re, a_im, ctr, cti, d_skip, d_s5):
    l = proj.shape[0]
    nb, uc, ns = bbr.shape
    t = min(S5_T, l)
    nt = l // t

    def body(u_ref, bbr_ref, bbi_ref, ar_ref, ai_ref, ctr_ref, cti_ref, d_ref,
             y_ref, car_r_ref, car_i_ref, cr, ci, sr_ref, si_ref):
        @pl.when(pl.program_id(1) == 0)
        def _():
            cr[...] = jnp.zeros_like(cr)
            ci[...] = jnp.zeros_like(ci)

        car_r_ref[...] = cr[...]
        car_i_ref[...] = ci[...]
        u = u_ref[...]
        cr[...], ci[...] = _s5_states(u.astype(BF16), bbr_ref[...], bbi_ref[...], ar_ref[...],
                                      ai_ref[...], cr[...], ci[...], sr_ref, si_ref)
        y_ref[...] = (_bdot(sr_ref[...], ctr_ref[...]) - _bdot(si_ref[...], cti_ref[...])
                      + d_ref[...] * u)

    per_block = lambda shape: pl.BlockSpec((None,) + shape, lambda b, n: (b, 0, 0))
    return pl.pallas_call(
        body, name="s5_fwd",
        out_shape=(jax.ShapeDtypeStruct((l, d_s5), F32),
                   jax.ShapeDtypeStruct((nt, 1, nb * ns), F32),
                   jax.ShapeDtypeStruct((nt, 1, nb * ns), F32)),
        grid=(nb, nt),
        in_specs=[pl.BlockSpec((t, uc), lambda b, n: (n, b)),
                  per_block((uc, ns)), per_block((uc, ns)),
                  per_block((1, ns)), per_block((1, ns)),
                  per_block((ns, uc)), per_block((ns, uc)),
                  pl.BlockSpec((1, uc), lambda b, n: (0, b))],
        out_specs=(pl.BlockSpec((t, uc), lambda b, n: (n, b)),
                   pl.BlockSpec((None, 1, ns), lambda b, n: (n, 0, b)),
                   pl.BlockSpec((None, 1, ns), lambda b, n: (n, 0, b))),
        scratch_shapes=[pltpu.VMEM((1, ns), F32), pltpu.VMEM((1, ns), F32),
                        pltpu.VMEM((t, ns), F32), pltpu.VMEM((t, ns), F32)],
        compiler_params=_cparams("parallel", "arbitrary"),
    )(proj, bbr, bbi, a_re, a_im, ctr, cti, d_skip)


def _s5_bwd(proj, dy, bbr, bbi, a_re, a_im, cbr, cbi, d_skip, car_r, car_i):
    l, d_s5 = dy.shape
    nb, uc, ns = bbr.shape
    t = min(S5_T, l)
    nt = l // t

    def body(u_ref, dy_ref, bbr_ref, bbi_ref, ar_ref, ai_ref, cbr_ref, cbi_ref, d_ref,
             car_r_ref, car_i_ref,
             du_ref, dar_ref, dai_ref, dbbr_ref, dbbi_ref, dcbr_ref, dcbi_ref, dd_ref, gcr, gci,
             sr_ref, si_ref, gr_ref, gi_ref):
        @pl.when(pl.program_id(1) == 0)
        def _():
            gcr[...] = jnp.zeros_like(gcr)
            gci[...] = jnp.zeros_like(gci)
            for ref in (dar_ref, dai_ref, dbbr_ref, dbbi_ref, dcbr_ref, dcbi_ref, dd_ref):
                ref[...] = jnp.zeros_like(ref)

        row = lax.broadcasted_iota(jnp.int32, (t, 1), 0)
        u, dy = u_ref[...], dy_ref[...]
        u_bf, dy_bf = u.astype(BF16), dy.astype(BF16)
        ar, ai = ar_ref[...], ai_ref[...]
        cr, ci = car_r_ref[...], car_i_ref[...]
        _s5_states(u_bf, bbr_ref[...], bbi_ref[...], ar, ai, cr, ci, sr_ref, si_ref)
        sr, si = sr_ref[...], si_ref[...]
        first = row == 0
        pr = jnp.where(first, cr, pltpu.roll(sr, 1, 0))
        pi = jnp.where(first, ci, pltpu.roll(si, 1, 0))
        gcr[...], gci[...] = _scan_rows(_dot(dy_bf, cbr_ref[...]), -_dot(dy_bf, cbi_ref[...]), ar, -ai,
                                        gcr[...], gci[...], gr_ref, gi_ref, reverse=True)
        gr, gi = gr_ref[...], gi_ref[...]
        dar_ref[...] += jnp.sum(gr * pr + gi * pi, axis=0, keepdims=True)
        dai_ref[...] += jnp.sum(gi * pr - gr * pi, axis=0, keepdims=True)
        gr_bf, gi_bf = gr.astype(BF16), gi.astype(BF16)
        tn = ((0,), (0,))
        dbbr_ref[...] += _dot(u_bf, gr_bf, tn)
        dbbi_ref[...] += _dot(u_bf, gi_bf, tn)
        dcbr_ref[...] += _dot(dy_bf, sr.astype(BF16), tn)
        dcbi_ref[...] -= _dot(dy_bf, si.astype(BF16), tn)
        nt_dims = ((1,), (1,))
        du = _dot(gr_bf, bbr_ref[...], nt_dims) + _dot(gi_bf, bbi_ref[...], nt_dims) + dy * d_ref[...]
        du_ref[...] = du.astype(BF16)
        dd_ref[...] += jnp.sum(dy * u, axis=0, keepdims=True)

    rev = lambda n: nt - 1 - n
    per_block = lambda shape: pl.BlockSpec((None,) + shape, lambda b, n: (b, 0, 0))
    acc = jax.ShapeDtypeStruct((nb, uc, ns), F32)
    vec = jax.ShapeDtypeStruct((nb, 1, ns), F32)
    return pl.pallas_call(
        body, name="s5_bwd",
        out_shape=(jax.ShapeDtypeStruct((l, d_s5), BF16), vec, vec, acc, acc, acc, acc,
                   jax.ShapeDtypeStruct((1, d_s5), F32)),
        grid=(nb, nt),
        in_specs=[pl.BlockSpec((t, uc), lambda b, n: (rev(n), b)),
                  pl.BlockSpec((t, uc), lambda b, n: (rev(n), b)),
                  per_block((uc, ns)), per_block((uc, ns)),
                  per_block((1, ns)), per_block((1, ns)),
                  per_block((uc, ns)), per_block((uc, ns)),
                  pl.BlockSpec((1, uc), lambda b, n: (0, b)),
                  pl.BlockSpec((None, 1, ns), lambda b, n: (rev(n), 0, b)),
                  pl.BlockSpec((None, 1, ns), lambda b, n: (rev(n), 0, b))],
        out_specs=(pl.BlockSpec((t, uc), lambda b, n: (rev(n), b)),
                   per_block((1, ns)), per_block((1, ns)),
                   per_block((uc, ns)), per_block((uc, ns)),
                   per_block((uc, ns)), per_block((uc, ns)),
                   pl.BlockSpec((1, uc), lambda b, n: (0, b))),
        scratch_shapes=[pltpu.VMEM((1, ns), F32), pltpu.VMEM((1, ns), F32)]
        + [pltpu.VMEM((t, ns), F32)] * 4,
        compiler_params=_cparams("parallel", "arbitrary"),
    )(proj, dy, bbr, bbi, a_re, a_im, cbr, cbi, d_skip, car_r, car_i)


def _s5_glu_fwd(y1, proj, off_z, wglu):
    l, d = y1.shape

    def body(y_ref, z_ref, w_ref, o_ref):
        y2 = _gelu(y_ref[...])
        y3 = y2 * _sigmoid(_bdot(y2, w_ref[...]))
        o_ref[...] = (y3 * _silu(z_ref[...])).astype(BF16)

    return pl.pallas_call(
        body, name="s5_glu_fwd",
        out_shape=jax.ShapeDtypeStruct((l, d), BF16),
        grid=(l // ROW_TILE,),
        in_specs=[pl.BlockSpec((ROW_TILE, d), lambda i: (i, 0)),
                  pl.BlockSpec((ROW_TILE, d), lambda i: (i, off_z // d)),
                  pl.BlockSpec((d, d), lambda i: (0, 0))],
        out_specs=pl.BlockSpec((ROW_TILE, d), lambda i: (i, 0)),
        compiler_params=_cparams("parallel"),
    )(y1, proj, wglu)


def _s5_glu_bwd(y1, proj, off_z, wglu, dout):
    l, d = y1.shape

    def body(y_ref, z_ref, w_ref, do_ref, dy_ref, dz_ref, dw_ref):
        y2, gelu_vjp = jax.vjp(_gelu, y_ref[...])
        z = z_ref[...]
        sz, silu_vjp = jax.vjp(_silu, z)
        y2_bf = y2.astype(BF16)
        sg = _sigmoid(_dot(y2_bf, w_ref[...]))
        dout = do_ref[...]
        dy3 = dout * sz
        dz_ref[...] = silu_vjp(dout * (y2 * sg))[0].astype(BF16)
        dgl = (dy3 * y2 * sg * (1.0 - sg)).astype(BF16)
        dy2 = dy3 * sg + _dot(dgl, w_ref[...], ((1,), (1,)))
        dy_ref[...] = gelu_vjp(dy2)[0]

        @pl.when(pl.program_id(0) == 0)
        def _():
            dw_ref[...] = jnp.zeros_like(dw_ref)

        dw_ref[...] += _dot(y2_bf, dgl, ((0,), (0,)))

    row = pl.BlockSpec((ROW_TILE, d), lambda i: (i, 0))
    full = pl.BlockSpec((d, d), lambda i: (0, 0))
    return pl.pallas_call(
        body, name="s5_glu_bwd",
        out_shape=(jax.ShapeDtypeStruct((l, d), F32), jax.ShapeDtypeStruct((l, d), BF16),
                   jax.ShapeDtypeStruct((d, d), F32)),
        grid=(l // ROW_TILE,),
        in_specs=[row, pl.BlockSpec((ROW_TILE, d), lambda i: (i, off_z // d)), full, row],
        out_specs=(row, row, full),
        compiler_params=_cparams("arbitrary"),
    )(y1, proj, wglu, dout)


def _shift_rows(x, k, back=False):
    if k == 0:
        return x
    t = x.shape[0]
    row = lax.broadcasted_iota(jnp.int32, (t, 1), 0)
    if back:
        return jnp.where(row < t - k, pltpu.roll(x, t - k, 0), 0.0)
    return jnp.where(row >= k, pltpu.roll(x, k, 0), 0.0)


def _dn_conv(x, w_ref):
    return sum(w_ref[CONV_K - 1 - k:CONV_K - k, :] * _shift_rows(x, k) for k in range(CONV_K))


def _dn_post_conv(c, j):
    y = _silu(c)
    n = y * lax.rsqrt(jnp.sum(y * y, axis=-1, keepdims=True) + EPS)
    n = n * jnp.where(j < DN_HEADS, DN_HEAD_DIM ** -0.5, 1.0)
    return jnp.where(j < 2 * DN_HEADS, n, y)


def _dn_prep_fwd(proj, off_qkv, conv_w):
    l = proj.shape[0]
    hd = DN_HEAD_DIM
    nblk = 3 * DN_HEADS

    def body(x_ref, w_ref, o_ref):
        o_ref[...] = _dn_post_conv(_dn_conv(x_ref[...], w_ref), pl.program_id(0))

    return pl.pallas_call(
        body, name="dn_prep_fwd",
        out_shape=jax.ShapeDtypeStruct((l, nblk * hd), F32),
        grid=(nblk,),
        in_specs=[pl.BlockSpec((l, hd), lambda j: (0, off_qkv // hd + j)),
                  pl.BlockSpec((CONV_K, hd), lambda j: (0, j))],
        out_specs=pl.BlockSpec((l, hd), lambda j: (0, j)),
        compiler_params=_cparams("parallel"),
    )(proj, conv_w)


def _dn_prep_bwd(proj, off_qkv, conv_w, dqkv):
    l = proj.shape[0]
    hd = DN_HEAD_DIM
    nblk = 3 * DN_HEADS

    def body(x_ref, w_ref, do_ref, dx_ref, dw_ref):
        x = x_ref[...]
        j = pl.program_id(0)
        _, vjp = jax.vjp(functools.partial(_dn_post_conv, j=j), _dn_conv(x, w_ref))
        dc = vjp(do_ref[...])[0]
        dx = sum(w_ref[CONV_K - 1 - k:CONV_K - k, :] * _shift_rows(dc, k, back=True)
                 for k in range(CONV_K))
        dx_ref[...] = dx.astype(BF16)
        for k in range(CONV_K):
            dw_ref[CONV_K - 1 - k:CONV_K - k, :] = jnp.sum(dc * _shift_rows(x, k), axis=0,
                                                           keepdims=True)

    return pl.pallas_call(
        body, name="dn_prep_bwd",
        out_shape=(jax.ShapeDtypeStruct((l, nblk * hd), BF16),
                   jax.ShapeDtypeStruct((CONV_K, nblk * hd), F32)),
        grid=(nblk,),
        in_specs=[pl.BlockSpec((l, hd), lambda j: (0, off_qkv // hd + j)),
                  pl.BlockSpec((CONV_K, hd), lambda j: (0, j)),
                  pl.BlockSpec((None, l, hd), lambda j: (j // DN_HEADS, 0, j % DN_HEADS))],
        out_specs=(pl.BlockSpec((l, hd), lambda j: (0, j)),
                   pl.BlockSpec((CONV_K, hd), lambda j: (0, j))),
        compiler_params=_cparams("parallel"),
    )(proj, conv_w, dqkv)


def _dn_gate_fn(ba, a_log_row, dt_row):
    lane = lax.broadcasted_iota(jnp.int32, ba.shape, 1)
    beta = _sigmoid(ba)
    g = -jnp.exp(a_log_row) * _softplus(ba + dt_row)
    return jnp.where(lane < DN_HEADS, beta, jnp.where(lane < 2 * DN_HEADS, g, 0.0))


def _dn_gates_fwd(proj, off_ba, a_log_row, dt_row):
    l = proj.shape[0]
    row = pl.BlockSpec((ROW_TILE, 128), lambda i: (i, off_ba // 128))
    vec = pl.BlockSpec((1, 128), lambda i: (0, 0))

    def body(ba_ref, al_ref, dt_ref, o_ref):
        o_ref[...] = _dn_gate_fn(ba_ref[...], al_ref[...], dt_ref[...])

    return pl.pallas_call(
        body, name="dn_gates_fwd",
        out_shape=jax.ShapeDtypeStruct((l, 128), F32),
        grid=(l // ROW_TILE,),
        in_specs=[row, vec, vec],
        out_specs=pl.BlockSpec((ROW_TILE, 128), lambda i: (i, 0)),
        compiler_params=_cparams("parallel"),
    )(proj, a_log_row, dt_row)


def _dn_gates_bwd(proj, off_ba, a_log_row, dt_row, dgb_heads):
    l = proj.shape[0]
    nh = dgb_heads.shape[0]
    row = pl.BlockSpec((ROW_TILE, 128), lambda i: (i, off_ba // 128))
    vec = pl.BlockSpec((1, 128), lambda i: (0, 0))

    def body(ba_ref, al_ref, dt_ref, dg_ref, dba_ref, dal_ref, ddt_ref):
        _, vjp = jax.vjp(_dn_gate_fn, ba_ref[...], al_ref[...], dt_ref[...])
        dgb = dg_ref[0]
        for h in range(1, nh):
            dgb = dgb + dg_ref[h]
        dba, dal, ddt = vjp(dgb)
        dba_ref[...] = dba.astype(BF16)

        @pl.when(pl.program_id(0) == 0)
        def _():
            dal_ref[...] = jnp.zeros_like(dal_ref)
            ddt_ref[...] = jnp.zeros_like(ddt_ref)

        dal_ref[...] += dal
        ddt_ref[...] += ddt

    return pl.pallas_call(
        body, name="dn_gates_bwd",
        out_shape=(jax.ShapeDtypeStruct((l, 128), BF16), jax.ShapeDtypeStruct((1, 128), F32),
                   jax.ShapeDtypeStruct((1, 128), F32)),
        grid=(l // ROW_TILE,),
        in_specs=[row, vec, vec, pl.BlockSpec((nh, ROW_TILE, 128), lambda i: (0, i, 0))],
        out_specs=(pl.BlockSpec((ROW_TILE, 128), lambda i: (i, 0)), vec, vec),
        compiler_params=_cparams("arbitrary"),
    )(proj, a_log_row, dt_row, dgb_heads)


def _dn_chunk_fn(states, qs, ks, vs, gb, heads):
    c = qs[0].shape[0]
    each = lambda f, *lists: [f(*args) for args in zip(*lists)]
    lane = lax.broadcasted_iota(jnp.int32, gb.shape, 1)
    ri = lax.broadcasted_iota(jnp.int32, (c, c), 0)
    ci = lax.broadcasted_iota(jnp.int32, (c, c), 1)
    causal, strict = ri >= ci, ri > ci
    eye = (ri == ci).astype(F32)
    rowi = lax.broadcasted_iota(jnp.int32, (c, 1), 0)
    nt_dims = ((1,), (1,))
    hdot = functools.partial(_dot, precision=HIGHEST)

    pick = lambda m, at: jnp.sum(jnp.where(lane == at, m, 0.0), axis=1, keepdims=True)
    gb_cum = hdot(causal.astype(F32), gb)
    beta = [pick(gb, h) for h in heads]
    gc = [pick(gb_cum, h + DN_HEADS) for h in heads]
    gc_row = each(lambda g: jnp.sum(eye * g, axis=0, keepdims=True), gc)
    decay = each(lambda g, gr: jnp.where(causal, jnp.exp(jnp.where(causal, g - gr, 0.0)), 0.0),
                 gc, gc_row)
    kk = each(lambda k: _bdot(k, k, nt_dims), ks)
    a_mat = each(lambda b, m, dc: jnp.where(strict, b * m * dc, 0.0), beta, kk, decay)

    t_inv = each(lambda a: eye - a, a_mat)
    power = a_mat
    for _ in range(int(math.log2(c)) - 1):
        power = each(lambda p: _dot3(p, p), power)
        t_inv = each(lambda t, p: t + _dot3(t, p), t_inv, power)

    egc = each(jnp.exp, gc)
    u_c = each(lambda t, v, b: _dot3(t, v * b), t_inv, vs, beta)
    w_c = each(lambda t, k, b, e: _dot3(t, k * (b * e)), t_inv, ks, beta, egc)
    qk = each(lambda q, k, dc: _bdot(q, k, nt_dims) * dc, qs, ks, decay)
    g_end = each(lambda g: jnp.sum(jnp.where(rowi == c - 1, g, 0.0), axis=0, keepdims=True), gc)
    v_new = each(lambda u, w, s: u - _bdot(w, s), u_c, w_c, states)
    o = each(lambda q, e, s, m, vn: _bdot(q * e, s) + _bdot(m, vn), qs, egc, states, qk, v_new)
    new_states = each(
        lambda s, ge, k, g, vn: s * jnp.exp(ge) + _bdot(k * jnp.exp(ge - g), vn, ((0,), (0,))),
        states, g_end, ks, gc, v_new)
    return o, new_states


def _dn_chunk_specs(order):
    hd, nh, hps = DN_HEAD_DIM, DN_HEADS, DN_HEADS_PER_STEP
    qkv = lambda part: pl.BlockSpec((CHUNK, hps * hd), lambda h, n: (order(n), part * (nh // hps) + h))
    gb = pl.BlockSpec((CHUNK, 128), lambda h, n: (order(n), 0))
    state = pl.BlockSpec((hps, None, hd, hd), lambda h, n: (h, order(n), 0, 0))
    return qkv, gb, state


def _dn_chunk_fwd(qkv, gb):
    l = qkv.shape[0]
    hd, nh, hps = DN_HEAD_DIM, DN_HEADS, DN_HEADS_PER_STEP
    n_chunks = l // CHUNK
    qkv_spec, gb_spec, state_spec = _dn_chunk_specs(lambda n: n)

    def body(q_ref, k_ref, v_ref, gb_ref, o_ref, s_ref, state):
        @pl.when(pl.program_id(1) == 0)
        def _():
            state[...] = jnp.zeros_like(state)

        cols = [slice(i * hd, (i + 1) * hd) for i in range(hps)]
        states = [state[i] for i in range(hps)]
        for i in range(hps):
            s_ref[i] = states[i]
        o, new_states = _dn_chunk_fn(
            states, [q_ref[:, cs] for cs in cols], [k_ref[:, cs] for cs in cols],
            [v_ref[:, cs] for cs in cols], gb_ref[...],
            [pl.program_id(0) * hps + i for i in range(hps)])
        for i in range(hps):
            o_ref[:, cols[i]] = o[i]
            state[i] = new_states[i]

    return pl.pallas_call(
        body, name="dn_chunk_fwd",
        out_shape=(jax.ShapeDtypeStruct((l, nh * hd), F32),
                   jax.ShapeDtypeStruct((nh, n_chunks, hd, hd), F32)),
        grid=(nh // hps, n_chunks),
        in_specs=[qkv_spec(0), qkv_spec(1), qkv_spec(2), gb_spec],
        out_specs=(pl.BlockSpec((CHUNK, hps * hd), lambda h, n: (n, h)), state_spec),
        scratch_shapes=[pltpu.VMEM((hps, hd, hd), F32)],
        compiler_params=_cparams("parallel", "arbitrary"),
    )(qkv, qkv, qkv, gb)


def _dn_chunk_bwd(qkv, gb, states, do):
    l = qkv.shape[0]
    hd, nh, hps = DN_HEAD_DIM, DN_HEADS, DN_HEADS_PER_STEP
    n_chunks = l // CHUNK
    rev = lambda n: n_chunks - 1 - n
    qkv_spec, gb_spec, state_spec = _dn_chunk_specs(rev)

    def body(q_ref, k_ref, v_ref, gb_ref, s_ref, do_ref, dqkv_ref, dgb_ref, dstate):
        @pl.when(pl.program_id(1) == 0)
        def _():
            dstate[...] = jnp.zeros_like(dstate)

        cols = [slice(i * hd, (i + 1) * hd) for i in range(hps)]
        fn = functools.partial(_dn_chunk_fn, heads=[pl.program_id(0) * hps + i for i in range(hps)])
        _, vjp = jax.vjp(fn, [s_ref[i] for i in range(hps)], [q_ref[:, cs] for cs in cols],
                         [k_ref[:, cs] for cs in cols], [v_ref[:, cs] for cs in cols], gb_ref[...])
        ds, dq, dk, dv, dgb = vjp(([do_ref[:, cs] for cs in cols], [dstate[i] for i in range(hps)]))
        for i in range(hps):
            dstate[i] = ds[i]
            dqkv_ref[0, :, cols[i]] = dq[i]
            dqkv_ref[1, :, cols[i]] = dk[i]
            dqkv_ref[2, :, cols[i]] = dv[i]
        dgb_ref[...] = dgb

    head_out = pl.BlockSpec((CHUNK, hps * hd), lambda h, n: (rev(n), h))
    return pl.pallas_call(
        body, name="dn_chunk_bwd",
        out_shape=(jax.ShapeDtypeStruct((3, l, nh * hd), F32),
                   jax.ShapeDtypeStruct((nh // hps, l, 128), F32)),
        grid=(nh // hps, n_chunks),
        in_specs=[qkv_spec(0), qkv_spec(1), qkv_spec(2), gb_spec, state_spec, head_out],
        out_specs=(pl.BlockSpec((3, CHUNK, hps * hd), lambda h, n: (0, rev(n), h)),
                   pl.BlockSpec((None, CHUNK, 128), lambda h, n: (h, rev(n), 0))),
        scratch_shapes=[pltpu.VMEM((hps, hd, hd), F32)],
        compiler_params=_cparams("parallel", "arbitrary"),
    )(qkv, qkv, qkv, gb, states, do)


def _dn_out_fn(o, z, w):
    return _rmsnorm(o, w) * _silu(z)


def _dn_out_fwd(o, proj, off_z, w):
    l, d = o.shape
    hd = DN_HEAD_DIM
    blk = lambda off: pl.BlockSpec((ROW_TILE, hd), lambda i, h: (i, off // hd + h))

    def body(o_ref, z_ref, w_ref, out_ref):
        out_ref[...] = _dn_out_fn(o_ref[...], z_ref[...], w_ref[...]).astype(BF16)

    return pl.pallas_call(
        body, name="dn_out_fwd",
        out_shape=jax.ShapeDtypeStruct((l, d), BF16),
        grid=(l // ROW_TILE, d // hd),
        in_specs=[blk(0), blk(off_z), pl.BlockSpec((1, hd), lambda i, h: (0, 0))],
        out_specs=blk(0),
        compiler_params=_cparams("parallel", "parallel"),
    )(o, proj, w)


def _dn_out_bwd(o, proj, off_z, w, dout):
    l, d = o.shape
    hd = DN_HEAD_DIM
    blk = lambda off: pl.BlockSpec((ROW_TILE, hd), lambda i, h: (i, off // hd + h))
    vec = pl.BlockSpec((1, hd), lambda i, h: (0, 0))

    def body(o_ref, z_ref, w_ref, dout_ref, do_ref, dz_ref, dw_ref):
        _, vjp = jax.vjp(_dn_out_fn, o_ref[...], z_ref[...], w_ref[...])
        do, dz, dw = vjp(dout_ref[...])
        do_ref[...] = do
        dz_ref[...] = dz.astype(BF16)

        @pl.when((pl.program_id(0) == 0) & (pl.program_id(1) == 0))
        def _():
            dw_ref[...] = jnp.zeros_like(dw_ref)

        dw_ref[...] += dw

    return pl.pallas_call(
        body, name="dn_out_bwd",
        out_shape=(jax.ShapeDtypeStruct((l, d), F32), jax.ShapeDtypeStruct((l, d), BF16),
                   jax.ShapeDtypeStruct((1, hd), F32)),
        grid=(l // ROW_TILE, d // hd),
        in_specs=[blk(0), blk(off_z), vec, blk(0)],
        out_specs=(blk(0), blk(0), vec),
        compiler_params=_cparams("arbitrary", "arbitrary"),
    )(o, proj, w, dout)


def _tile_2d(rows, cols, budget_bytes=1 << 20):
    for tr in (rows, 4096, 2048, 1024, 512, 256, 128, 64, 32, 16):
        if tr <= rows and rows % tr == 0 and tr * cols * 4 <= budget_bytes:
            return tr, cols
    for tc in (2048, 1024, 512, 256, 128):
        if cols % tc == 0 and rows * tc * 4 <= 2 * budget_bytes:
            return rows, tc
    raise ValueError((rows, cols))


def _adamw_update(g, w_ref, m_ref, v_ref, go_ref, d_ref, mo_ref, vo_ref):
    c1 = 1.0 / (1.0 - ADAM_B1 ** ADAM_STEP)
    c2 = 1.0 / (1.0 - ADAM_B2 ** ADAM_STEP)
    m_new = ADAM_B1 * m_ref[...] + (1.0 - ADAM_B1) * g
    v_new = ADAM_B2 * v_ref[...] + (1.0 - ADAM_B2) * (g * g)
    go_ref[...] = g
    mo_ref[...] = m_new
    vo_ref[...] = v_new
    d_ref[...] = -ADAM_LR * ((m_new * c1) / (jnp.sqrt(v_new * c2) + ADAM_EPS) + ADAM_WD * w_ref[...])


def _adamw(w, m, v, gslots, name):
    rows, cols = w.shape
    ns = gslots.shape[0]
    tr, tc = _tile_2d(rows, cols)

    def body(w_ref, m_ref, v_ref, g_ref, go_ref, d_ref, mo_ref, vo_ref):
        g = g_ref[0].astype(F32)
        for s in range(1, ns):
            g = g + g_ref[s].astype(F32)
        _adamw_update(g, w_ref, m_ref, v_ref, go_ref, d_ref, mo_ref, vo_ref)

    blk = pl.BlockSpec((tr, tc), lambda i, j: (i, j))
    o = jax.ShapeDtypeStruct((rows, cols), F32)
    return pl.pallas_call(
        body, name=name, out_shape=(o, o, o, o),
        grid=(rows // tr, cols // tc),
        in_specs=[blk, blk, blk, pl.BlockSpec((ns, tr, tc), lambda i, j: (0, i, j))],
        out_specs=(blk, blk, blk, blk),
        compiler_params=_cparams("parallel", "parallel"),
    )(w, m, v, gslots)


def _slot_sum(gslots, name):
    ns, rows, cols = gslots.shape
    tr, tc = _tile_2d(rows, cols)

    def body(g_ref, o_ref):
        g = g_ref[0]
        for s in range(1, ns):
            g = g + g_ref[s]
        o_ref[...] = g

    return pl.pallas_call(
        body, name=name, out_shape=jax.ShapeDtypeStruct((rows, cols), F32),
        grid=(rows // tr, cols // tc),
        in_specs=[pl.BlockSpec((ns, tr, tc), lambda i, j: (0, i, j))],
        out_specs=pl.BlockSpec((tr, tc), lambda i, j: (i, j)),
        compiler_params=_cparams("parallel", "parallel"),
    )(gslots)


HBM_SPEC = pl.BlockSpec(memory_space=pl.ANY)


def _all_gather(arrs, name):
    n = len(arrs)

    def body(*refs):
        ins, outs = refs[:n], refs[n:2 * n]
        send_sems, recv_sems, local_sems = refs[2 * n:]
        x, y, c = lax.axis_index("x"), lax.axis_index("y"), lax.axis_index("c")
        me, sibling = (x, y, c), (x, y, 1 - c)
        chips = [(1 - x, y), (x, 1 - y), (1 - x, 1 - y)]
        index = lambda px, py, pc: 4 * px + 2 * py + pc

        def copy(a, k, block, to, src=None):
            rows = outs[a].at[index(*block)]
            return pltpu.make_async_remote_copy(
                src_ref=rows if src is None else src, dst_ref=rows,
                send_sem=send_sems.at[a, k], recv_sem=recv_sems.at[a, k],
                device_id=to, device_id_type=MESH)

        mine = [pltpu.make_async_copy(ins[a], outs[a].at[index(*me)], local_sems.at[a])
                for a in range(n)]
        for cp in mine:
            cp.start()
        first = []
        for a in range(n):
            first.append(copy(a, 0, me, sibling, src=ins[a]))
            first += [copy(a, 1 + j, me, (*chip, c), src=ins[a]) for j, chip in enumerate(chips)]
        for cp in first:
            cp.start()
        passed = []
        for j, chip in enumerate(chips):
            for a in range(n):
                copy(a, 1 + j, (*chip, c), me).wait_recv()
                fwd = copy(a, 4 + j, (*chip, c), sibling)
                fwd.start()
                passed.append(fwd)
        for a in range(n):
            copy(a, 0, sibling, me).wait_recv()
            for j, chip in enumerate(chips):
                copy(a, 4 + j, (*chip, 1 - c), me).wait_recv()
        for cp in first + passed:
            cp.wait_send()
        for cp in mine:
            cp.wait()

    return pl.pallas_call(
        body, name=name,
        out_shape=[jax.ShapeDtypeStruct((N_DEV,) + a.shape, a.dtype) for a in arrs],
        in_specs=[HBM_SPEC] * n, out_specs=[HBM_SPEC] * n,
        scratch_shapes=[pltpu.SemaphoreType.DMA((n, 7)), pltpu.SemaphoreType.DMA((n, 7)),
                        pltpu.SemaphoreType.DMA((n,))],
    )(*arrs)


def _sibling_swap(arrs, name):
    n = len(arrs)

    def body(*refs):
        ins, outs = refs[:n], refs[n:2 * n]
        send_sems, recv_sems = refs[2 * n:]
        x, y, c = lax.axis_index("x"), lax.axis_index("y"), lax.axis_index("c")
        copies = [pltpu.make_async_remote_copy(
            src_ref=ins[a].at[1 - c], dst_ref=outs[a],
            send_sem=send_sems.at[a], recv_sem=recv_sems.at[a],
            device_id=(x, y, 1 - c), device_id_type=MESH) for a in range(n)]
        for cp in copies:
            cp.start()
        for cp in copies:
            cp.wait()

    return pl.pallas_call(
        body, name=name,
        out_shape=[jax.ShapeDtypeStruct(a.shape[1:], a.dtype) for a in arrs],
        in_specs=[HBM_SPEC] * n, out_specs=[HBM_SPEC] * n,
        scratch_shapes=[pltpu.SemaphoreType.DMA((n,)), pltpu.SemaphoreType.DMA((n,))],
    )(*arrs)


def _pair_sum(mine, theirs, core, name):
    _, rows, cols = mine.shape
    tr, tc = _tile_2d(rows, cols, budget_bytes=2 << 20)

    def body(core_ref, a_ref, b_ref, o_ref):
        o_ref[...] = (a_ref[...].astype(F32) + b_ref[...].astype(F32)).astype(o_ref.dtype)

    return pl.pallas_call(
        body, name=name, out_shape=jax.ShapeDtypeStruct((rows, cols), mine.dtype),
        grid_spec=pltpu.PrefetchScalarGridSpec(
            num_scalar_prefetch=1, grid=(rows // tr, cols // tc),
            in_specs=[pl.BlockSpec((None, tr, tc), lambda i, j, core_ref: (core_ref[0], i, j)),
                      pl.BlockSpec((tr, tc), lambda i, j, core_ref: (i, j))],
            out_specs=pl.BlockSpec((tr, tc), lambda i, j, core_ref: (i, j))),
        compiler_params=_cparams("parallel", "parallel"),
    )(core, mine, theirs)


HBM_ONLY = pl.BlockSpec(memory_space=pltpu.HBM)
SEM_SPEC = pl.BlockSpec(memory_space=pltpu.SEMAPHORE)
SPLIT_COPY_EFFECT = pltpu.SideEffectType.DATAFLOW_SIDE_EFFECTING


def _flip(v, bit):
    return 1 - v if bit else v


def _chip_slices_plan(n):
    def plan():
        x, y, c = lax.axis_index("x"), lax.axis_index("y"), lax.axis_index("c")
        copies = []
        for k in range(1, 4):
            px, py = _flip(x, k & 2), _flip(y, k & 1)
            copies += [(a, 2 * px + py, 2 * x + y, (px, py, c)) for a in range(n)]
        return copies
    return plan, 3 * n


def _gather_plan(n):
    def plan():
        x, y, c = lax.axis_index("x"), lax.axis_index("y"), lax.axis_index("c")
        copies = []
        for k in range(1, N_DEV):
            peer = (_flip(x, k & 4), _flip(y, k & 2), _flip(c, k & 1))
            copies += [(a, None, 4 * x + 2 * y + c, peer) for a in range(n)]
        return copies
    return plan, 7 * n


def _planned_copies(plan, srcs, lands, send_sems, recv_sems):
    return [pltpu.make_async_remote_copy(
        src_ref=srcs[a] if src_at is None else srcs[a].at[src_at], dst_ref=lands[a].at[land_at],
        send_sem=send_sems[i], recv_sem=recv_sems[i], device_id=peer, device_id_type=MESH)
        for i, (a, src_at, land_at, peer) in enumerate(plan())]


def _split_exchange_start(plan_and_count, arrs, land_shapes, name, after=None):
    plan, n_sems = plan_and_count
    n = len(arrs)

    n_in = 2 * n + (after is not None)

    def body(*refs):
        srcs, lands = refs[:n], refs[n:2 * n]
        send_sems, recv_sems = refs[n_in:n_in + n_sems], refs[n_in + n_sems:n_in + 2 * n_sems]
        token = refs[-1]
        for copy in _planned_copies(plan, srcs, lands, send_sems, recv_sems):
            copy.start()
        token[...] = jnp.zeros_like(token)

    hbm = lambda a: pltpu.HBM(a.shape, a.dtype)
    operands = [pltpu.with_memory_space_constraint(a, pltpu.HBM) for a in arrs]
    operands += [pltpu.with_memory_space_constraint(lax.empty(shape, a.dtype), pltpu.HBM)
                 for a, shape in zip(arrs, land_shapes)]
    out = pl.pallas_call(
        body, name=name,
        out_shape=(*[pltpu.SemaphoreType.DMA(())] * (2 * n_sems),
                   *[hbm(a) for a in operands],
                   jax.ShapeDtypeStruct((8, 128), F32)),
        in_specs=[HBM_ONLY] * (2 * n) + [pl.BlockSpec(memory_space=pl.ANY)] * (after is not None),
        out_specs=(*[SEM_SPEC] * (2 * n_sems), *[HBM_ONLY] * (2 * n),
                   pl.BlockSpec(memory_space=pltpu.VMEM)),
        input_output_aliases={i: 2 * n_sems + i for i in range(2 * n)},
        compiler_params=pltpu.CompilerParams(has_side_effects=SPLIT_COPY_EFFECT),
    )(*operands, *([after] if after is not None else []))
    sems, rest = list(out[:2 * n_sems]), out[2 * n_sems:]
    return sems, list(rest[:n]), list(rest[n:2 * n]), rest[-1]


def _split_exchange_wait(plan_and_count, sems, srcs, lands, after, name):
    plan, n_sems = plan_and_count
    n = len(srcs)

    def body(*refs):
        src_refs, land_refs = refs[:n], refs[n:2 * n]
        send_sems, recv_sems = refs[2 * n:2 * n + n_sems], refs[2 * n + n_sems:2 * n + 2 * n_sems]
        for copy in _planned_copies(plan, src_refs, land_refs, send_sems, recv_sems):
            copy.wait_send()
            copy.wait_recv()

    hbm = lambda a: pltpu.HBM(a.shape, a.dtype)
    out = pl.pallas_call(
        body, name=name,
        out_shape=(*[hbm(a) for a in srcs], *[hbm(a) for a in lands]),
        in_specs=[HBM_ONLY] * (2 * n) + [SEM_SPEC] * (2 * n_sems) + [pl.BlockSpec(memory_space=pl.ANY)],
        out_specs=tuple([HBM_ONLY] * (2 * n)),
        input_output_aliases={i: i for i in range(2 * n)},
        compiler_params=pltpu.CompilerParams(has_side_effects=SPLIT_COPY_EFFECT),
    )(*srcs, *lands, *sems, after)
    return list(out[n:])


def _adamw_exchanged(w, m, v, own, landed, chip, name):
    rows, cols = w.shape
    tr, tc = _tile_2d(rows, cols)

    def body(chip_ref, w_ref, m_ref, v_ref, own_ref, l1_ref, l2_ref, l3_ref, go_ref, d_ref, mo_ref, vo_ref):
        g = own_ref[...].astype(F32)
        for ref in (l1_ref, l2_ref, l3_ref):
            g = g + ref[...].astype(F32)
        _adamw_update(g, w_ref, m_ref, v_ref, go_ref, d_ref, mo_ref, vo_ref)

    blk = pl.BlockSpec((tr, tc), lambda i, j, chip_ref: (i, j))
    slot = lambda k: pl.BlockSpec((None, tr, tc), lambda i, j, chip_ref: (chip_ref[0] ^ k, i, j))
    o = jax.ShapeDtypeStruct((rows, cols), F32)
    return pl.pallas_call(
        body, name=name, out_shape=(o, o, o, o),
        grid_spec=pltpu.PrefetchScalarGridSpec(
            num_scalar_prefetch=1, grid=(rows // tr, cols // tc),
            in_specs=[blk, blk, blk, slot(0), slot(1), slot(2), slot(3)],
            out_specs=(blk, blk, blk, blk)),
        compiler_params=_cparams("parallel", "parallel"),
    )(chip, w, m, v, own, landed, landed, landed)


def _block_diag(t):
    nb, gpb, r, c = t.shape
    eye = jnp.eye(gpb, dtype=t.dtype)
    return jnp.einsum("ngrc,gh->ngrhc", t, eye).reshape(nb, gpb * r, gpb * c)


def _diag_blocks(t, r, c):
    nb = t.shape[0]
    gpb = t.shape[1] // r
    t = t.reshape(nb, gpb, r, gpb, c)
    return jnp.einsum("ngrhc,gh->ngrc", t, jnp.eye(gpb, dtype=t.dtype))


def _pack_rows(parts):
    flat = jnp.concatenate([p.reshape(-1).astype(F32) for p in parts])
    pad = (-flat.shape[0]) % (256 * 128)
    return jnp.pad(flat, (0, pad)).reshape(-1, 128)


def _unpack_rows(packed, shapes):
    flat = packed.reshape(-1)
    out, at = [], 0
    for shape in shapes:
        size = math.prod(shape)
        out.append(flat[at:at + size].reshape(shape))
        at += size
    return out


def kernel(x, ln_w, w_in, s5_lam_re, s5_lam_im, s5_log_step, s5_b_re, s5_b_im, s5_c_re, s5_c_im, s5_d, s5_w_glu, s5_w_up, dn_conv_w, dn_a_log, dn_dt_bias, dn_norm_w, dn_w_up, w_out, final_norm_w, loss_target, m_ln_w, m_w_in, m_s5_lam_re, m_s5_lam_im, m_s5_log_step, m_s5_b_re, m_s5_b_im, m_s5_c_re, m_s5_c_im, m_s5_d, m_s5_w_glu, m_s5_w_up, m_dn_conv_w, m_dn_a_log, m_dn_dt_bias, m_dn_norm_w, m_dn_w_up, m_w_out, m_final_norm_w, v_ln_w, v_w_in, v_s5_lam_re, v_s5_lam_im, v_s5_log_step, v_s5_b_re, v_s5_b_im, v_s5_c_re, v_s5_c_im, v_s5_d, v_s5_w_glu, v_s5_w_up, v_dn_conv_w, v_dn_a_log, v_dn_dt_bias, v_dn_norm_w, v_dn_w_up, v_w_out, v_final_norm_w):
    weights = dict(ln_w=ln_w, w_in=w_in, s5_lam_re=s5_lam_re, s5_lam_im=s5_lam_im,
                   s5_log_step=s5_log_step, s5_b_re=s5_b_re, s5_b_im=s5_b_im, s5_c_re=s5_c_re,
                   s5_c_im=s5_c_im, s5_d=s5_d, s5_w_glu=s5_w_glu, s5_w_up=s5_w_up,
                   dn_conv_w=dn_conv_w, dn_a_log=dn_a_log, dn_dt_bias=dn_dt_bias,
                   dn_norm_w=dn_norm_w, dn_w_up=dn_w_up, w_out=w_out, final_norm_w=final_norm_w)
    mom_m = dict(ln_w=m_ln_w, w_in=m_w_in, s5_lam_re=m_s5_lam_re, s5_lam_im=m_s5_lam_im,
                 s5_log_step=m_s5_log_step, s5_b_re=m_s5_b_re, s5_b_im=m_s5_b_im,
                 s5_c_re=m_s5_c_re, s5_c_im=m_s5_c_im, s5_d=m_s5_d, s5_w_glu=m_s5_w_glu,
                 s5_w_up=m_s5_w_up, dn_conv_w=m_dn_conv_w, dn_a_log=m_dn_a_log,
                 dn_dt_bias=m_dn_dt_bias, dn_norm_w=m_dn_norm_w, dn_w_up=m_dn_w_up,
                 w_out=m_w_out, final_norm_w=m_final_norm_w)
    mom_v = dict(ln_w=v_ln_w, w_in=v_w_in, s5_lam_re=v_s5_lam_re, s5_lam_im=v_s5_lam_im,
                 s5_log_step=v_s5_log_step, s5_b_re=v_s5_b_re, s5_b_im=v_s5_b_im,
                 s5_c_re=v_s5_c_re, s5_c_im=v_s5_c_im, s5_d=v_s5_d, s5_w_glu=v_s5_w_glu,
                 s5_w_up=v_s5_w_up, dn_conv_w=v_dn_conv_w, dn_a_log=v_dn_a_log,
                 dn_dt_bias=v_dn_dt_bias, dn_norm_w=v_dn_norm_w, dn_w_up=v_dn_w_up,
                 w_out=v_w_out, final_norm_w=v_final_norm_w)
    names = list(weights)

    l, d = x.shape[1], x.shape[2]
    d_s5 = d // 2
    groups = d_s5 // S5_GROUP
    nb = groups // S5_GPB
    d_dn = DN_HEADS * DN_HEAD_DIM
    w_in_cols = w_in.shape[2]
    d_in = N_DEV * w_in_cols
    off_ba_src = 2 * d_s5 + 4 * d_dn
    off_u, off_zs, off_qkv, off_zd = 0, d_s5, 2 * d_s5, 2 * d_s5 + 3 * d_dn
    off_ba = off_zd + d_dn
    off_gs = off_ba + BA_PAD
    off_gd = off_gs + d
    n_proj = off_gd + d
    x2d, tgt2d = x[0], loss_target[0]
    my_index = 4 * lax.axis_index("x") + 2 * lax.axis_index("y") + lax.axis_index("c")

    g_win, g_conv = _all_gather([jnp.transpose(w_in[0]).astype(BF16), dn_conv_w[0]], name="gather_weights")
    late_plan = _gather_plan(4)
    late_shards = [s5_w_glu[0].astype(BF16), s5_w_up[0].astype(BF16), dn_w_up[0].astype(BF16),
                   w_out[0].astype(BF16)]
    late_sems, late_shards, late_lands, late_token = _split_exchange_start(
        late_plan, late_shards, [(N_DEV,) + s.shape for s in late_shards], name="gather_late_start",
        after=g_conv)
    ba_end = off_ba_src + 2 * DN_HEADS
    ba_zeros = BA_PAD - 2 * DN_HEADS
    w_full_t = g_win.reshape(d_in, d)
    w_cat_t = jnp.concatenate(
        [w_full_t[:ba_end], jnp.zeros((ba_zeros, d), BF16), w_full_t[ba_end:]], axis=0)
    conv_full = jnp.transpose(g_conv, (1, 0, 2)).reshape(CONV_K, 3 * d_dn)

    lam_re, lam_im = s5_lam_re[0], s5_lam_im[0]
    log_step = s5_log_step[0].reshape(groups, 1)
    b_re = s5_b_re[0].reshape(groups * S5_STATE, S5_GROUP)
    b_im = s5_b_im[0].reshape(groups * S5_STATE, S5_GROUP)
    abar_re, abar_im, f_re, f_im = _s5_disc_fwd(lam_re, lam_im, log_step)
    f_re_col, f_im_col = f_re.reshape(-1, 1), f_im.reshape(-1, 1)
    bb_re, bb_im = _s5_bbar_fwd(f_re_col, f_im_col, b_re, b_im)

    def bb_blocks(t):
        t = t.reshape(nb, S5_GPB, S5_STATE, S5_GROUP).transpose(0, 1, 3, 2)
        return _block_diag(t).astype(BF16)

    def c_blocks(t):
        return _block_diag(t.reshape(nb, S5_GPB, S5_GROUP, S5_STATE)).astype(BF16)

    bbr, bbi = bb_blocks(bb_re), bb_blocks(bb_im)
    cbr, cbi = c_blocks(s5_c_re[0]), c_blocks(s5_c_im[0])
    ctr, cti = jnp.transpose(cbr, (0, 2, 1)), jnp.transpose(cbi, (0, 2, 1))
    a_re = abar_re.reshape(nb, 1, S5_GPB * S5_STATE)
    a_im = abar_im.reshape(nb, 1, S5_GPB * S5_STATE)

    h = _rms_fwd(x2d, ln_w)
    proj = _mm(h, w_cat_t, tb=True, tn=1536, after=late_token, name="proj")
    y1, car_r, car_i = _s5_fwd(proj, bbr, bbi, a_re, a_im, ctr, cti, s5_d, d_s5)
    a_log_row = jnp.pad(dn_a_log, ((0, 0), (DN_HEADS, 128 - 2 * DN_HEADS)))
    dt_row = jnp.pad(dn_dt_bias, ((0, 0), (DN_HEADS, 128 - 2 * DN_HEADS)))
    qkv = _dn_prep_fwd(proj, off_qkv, conv_full)
    gb = _dn_gates_fwd(proj, off_ba, a_log_row, dt_row)
    o_dn, states = _dn_chunk_fwd(qkv, gb)

    late_lands = _split_exchange_wait(late_plan, late_sems, late_shards, late_lands, o_dn,
                                      name="gather_late_wait")
    g_glu, g_sup, g_dup, g_wout = [
        lax.dynamic_update_slice(land, shard[None], (my_index, 0, 0))
        for land, shard in zip(late_lands, late_shards)]
    wglu_full = g_glu.reshape(d_s5, d_s5)
    wsup_full = jnp.transpose(g_sup, (1, 0, 2)).reshape(d_s5, d)
    wdup_full = jnp.transpose(g_dup, (1, 0, 2)).reshape(d_dn, d)
    wout_full = g_wout.reshape(d, d)

    out_s = _s5_glu_fwd(y1, proj, off_zs, wglu_full)
    y_s = _mm(out_s, wsup_full, name="s5_up")
    out_d = _dn_out_fwd(o_dn, proj, off_zd, dn_norm_w)
    y_d = _mm(out_d, wdup_full, name="dn_up")

    mixed = _merge_fwd(proj, off_gs, off_gd, y_s, y_d)
    branch = _mm(mixed, wout_full, name="w_out")
    dx2, dx2_bf, loss_dev, d_final_w = _final(x2d, branch, final_norm_w.reshape(1, d), tgt2d)

    g_wout_full = _mm(mixed, dx2_bf, ta=True, out_dtype=BF16, name="grad_w_out")
    dmixed = _mm(dx2_bf, wout_full, tb=True, name="d_mixed")
    dgs, dgd, dys, dyd = _merge_bwd(proj, off_gs, off_gd, y_s, y_d, dmixed)

    g_dup_full = _mm(out_d, dyd, ta=True, out_dtype=BF16, name="grad_dn_up")
    dout_d = _mm(dyd, wdup_full, tb=True, name="d_out_d")
    do_dn, dzd, d_norm_w = _dn_out_bwd(o_dn, proj, off_zd, dn_norm_w, dout_d)
    dqkv, dgb_heads = _dn_chunk_bwd(qkv, gb, states, do_dn)
    dba, d_a_log_row, d_dt_row = _dn_gates_bwd(proj, off_ba, a_log_row, dt_row, dgb_heads)
    dqkv_pre, d_conv_full = _dn_prep_bwd(proj, off_qkv, conv_full, dqkv)

    g_sup_full = _mm(out_s, dys, ta=True, out_dtype=BF16, name="grad_s5_up")
    dout_s = _mm(dys, wsup_full, tb=True, name="d_out_s")
    dy1, dzs, g_glu_full = _s5_glu_bwd(y1, proj, off_zs, wglu_full, dout_s)

    def by_dest(t, axis):
        if axis == 1:
            return t.reshape(t.shape[0], 4, 2, t.shape[1] // N_DEV).transpose(2, 1, 0, 3)
        return t.reshape(4, 2, t.shape[0] // N_DEV, t.shape[1]).transpose(1, 0, 2, 3)

    core = lax.axis_index("c").astype(jnp.int32).reshape(1)
    chip = (2 * lax.axis_index("x") + lax.axis_index("y")).astype(jnp.int32).reshape(1)

    def chip_sums_of(which, parts, tag):
        from_sibling = _sibling_swap(parts, name="swap_grads_" + tag)
        return [_pair_sum(p.reshape(2, -1, p.shape[-1]), got.reshape(-1, got.shape[-1]), core,
                          name="pair_sum_" + nm).reshape(got.shape)
                for nm, p, got in zip(which, parts, from_sibling)]

    early = ["s5_w_glu", "s5_w_up", "dn_w_up", "w_out"]
    sums_a = chip_sums_of(early, [by_dest(g_glu_full.astype(BF16), 0), by_dest(g_sup_full, 1),
                                  by_dest(g_dup_full, 1), by_dest(g_wout_full, 0)], "a")
    plan_a = _chip_slices_plan(len(sums_a))
    sems_a, src_a, land_a, token_a = _split_exchange_start(
        plan_a, sums_a, [t.shape for t in sums_a], name="exchange_start_a")

    (du, d_a_re, d_a_im, d_bbr, d_bbi, d_cbr, d_cbi, d_s5_d) = _s5_bwd(
        proj, dy1, bbr, bbi, a_re, a_im, cbr, cbi, s5_d + token_a[:1, :1], car_r, car_i)

    def from_bb_blocks(t):
        t = _diag_blocks(t, S5_GROUP, S5_STATE).transpose(0, 1, 3, 2)
        return t.reshape(groups * S5_STATE, S5_GROUP)

    d_f_re, d_f_im, d_b_re, d_b_im = _s5_bbar_bwd(f_re_col, f_im_col, b_re, b_im,
                                                 from_bb_blocks(d_bbr), from_bb_blocks(d_bbi))
    d_lam_re, d_lam_im, d_log_step = _s5_disc_bwd(
        lam_re, lam_im, log_step, d_a_re.reshape(groups, S5_STATE), d_a_im.reshape(groups, S5_STATE),
        d_f_re.reshape(groups, S5_STATE), d_f_im.reshape(groups, S5_STATE))
    d_c_re = _diag_blocks(d_cbr, S5_GROUP, S5_STATE).reshape(groups, S5_GROUP, S5_STATE)
    d_c_im = _diag_blocks(d_cbi, S5_GROUP, S5_STATE).reshape(groups, S5_GROUP, S5_STATE)

    dproj = jnp.concatenate(
        [du, dzs, dqkv_pre, dzd, jnp.pad(dba, ((0, 0), (0, BA_PAD - 128))), dgs, dgd], axis=1)
    g_wcat_t = _mm(dproj, h, ta=True, out_dtype=BF16, tm=768, tn=d, name="grad_w_in")
    g_win_full_t = jnp.concatenate([g_wcat_t[:ba_end], g_wcat_t[ba_end + ba_zeros:]], axis=0)
    sums_b = chip_sums_of(["w_in"], [by_dest(g_win_full_t, 0)], "b")
    plan_b = _chip_slices_plan(1)
    sems_b, src_b, land_b, token_b = _split_exchange_start(
        plan_b, sums_b, [t.shape for t in sums_b], name="exchange_start_b")
    dh = _mm(dproj, w_cat_t, tm=1024, tn=1024, tk=1536, after=token_b, name="d_h")
    grad_x, d_ln_w = _rms_bwd(x2d, ln_w, dh, dx2)
    big = ["w_in"] + early
    results = {}

    small = [nm for nm in names if nm not in big]
    small_grads = dict(
        ln_w=d_ln_w, s5_lam_re=d_lam_re, s5_lam_im=d_lam_im, s5_log_step=d_log_step,
        s5_b_re=d_b_re, s5_b_im=d_b_im, s5_c_re=d_c_re, s5_c_im=d_c_im, s5_d=d_s5_d,
        dn_conv_w=d_conv_full, dn_a_log=d_a_log_row[:, DN_HEADS:2 * DN_HEADS],
        dn_dt_bias=d_dt_row[:, DN_HEADS:2 * DN_HEADS], dn_norm_w=d_norm_w, final_norm_w=d_final_w)
    (all_small,) = _all_gather([_pack_rows([small_grads[nm] for nm in small])], name="gather_small_grads")
    summed = _slot_sum(all_small, name="sum_small_grads")
    full_shapes = [(CONV_K, 3 * d_dn) if nm == "dn_conv_w" else weights[nm].shape for nm in small]
    g_small = dict(zip(small, _unpack_rows(summed, full_shapes)))
    conv_cols = dn_conv_w.shape[2]
    g_small["dn_conv_w"] = lax.dynamic_slice_in_dim(
        g_small["dn_conv_w"], my_index * conv_cols, conv_cols, axis=1).reshape(dn_conv_w.shape)
    packed = [_pack_rows([t[nm] for nm in small]) for t in (weights, mom_m, mom_v, g_small)]
    small_out = _adamw(packed[0], packed[1], packed[2], packed[3][None], name="adamw_small")
    small_shapes = [weights[nm].shape for nm in small]
    for kind, packed_out in enumerate(small_out):
        for nm, val in zip(small, _unpack_rows(packed_out, small_shapes)):
            results.setdefault(nm, [None] * 4)[kind] = val

    land_a = _split_exchange_wait(plan_a, sems_a, src_a, land_a, small_out[0], name="exchange_wait_a")
    for nm, own, landed in zip(early, src_a, land_a):
        results[nm] = _adamw_exchanged(weights[nm][0], mom_m[nm][0], mom_v[nm][0], own, landed, chip,
                                       name="adamw_" + nm)
    (land_b,) = _split_exchange_wait(plan_b, sems_b, src_b, land_b, results[early[-1]][0],
                                     name="exchange_wait_b")
    res = _adamw_exchanged(jnp.transpose(w_in[0]), jnp.transpose(m_w_in[0]), jnp.transpose(v_w_in[0]),
                           src_b[0], land_b, chip, name="adamw_w_in")
    results["w_in"] = [jnp.transpose(t) for t in res]

    loss = lax.psum(loss_dev[0, 0], ("x", "y", "c"))
    outs = [loss, grad_x[None]]
    for kind in range(4):
        outs += [results[nm][kind].reshape(weights[nm].shape) for nm in names]
    return tuple(outs)
```

```python
import functools
import math

import jax
import jax.numpy as jnp
from jax import lax
from jax.experimental import pallas as pl
from jax.experimental.pallas import tpu as pltpu

F32 = jnp.float32
BF16 = jnp.bfloat16
HIGHEST = lax.Precision.HIGHEST
MESH = pl.DeviceIdType.MESH
N_DEV = 8

EPS = 1e-6
S5_GROUP = 16
S5_STATE = 64
S5_GPB = 8
S5_T = 256
DN_HEADS = 8
DN_HEAD_DIM = 128
CHUNK = 64
DN_HEADS_PER_STEP = 8
CONV_K = 4
BA_PAD = 512

ADAM_LR = 0.001
ADAM_B1 = 0.9
ADAM_B2 = 0.999
ADAM_EPS = 1e-08
ADAM_WD = 0.01
ADAM_STEP = 10

VMEM_LIMIT_BYTES = 48 * 1024 * 1024
ROW_TILE = 256


def _cparams(*sem):
    return pltpu.CompilerParams(dimension_semantics=sem if sem else None,
                                vmem_limit_bytes=VMEM_LIMIT_BYTES)


def _sigmoid(x):
    return 1.0 / (1.0 + jnp.exp(-x))


def _silu(x):
    return x * _sigmoid(x)


def _gelu(x):
    return 0.5 * x * (1.0 + jnp.tanh(0.7978845608028654 * (x + 0.044715 * x * x * x)))


def _softplus(x):
    return jnp.maximum(x, 0.0) + jnp.log(1.0 + jnp.exp(-jnp.abs(x)))


def _rmsnorm(x, w):
    return x * lax.rsqrt(jnp.mean(x * x, axis=-1, keepdims=True) + EPS) * w


def _dot(a, b, dims=((1,), (0,)), precision=None):
    return lax.dot_general(a, b, (dims, ((), ())), precision=precision,
                           preferred_element_type=F32)


def _bdot(a, b, dims=((1,), (0,))):
    return _dot(a.astype(BF16), b.astype(BF16), dims)


def _split_bf16(a):
    hi = a.astype(BF16)
    return hi, (a - hi.astype(F32)).astype(BF16)


def _dot3_dims(a, b, dims):
    ah, al = _split_bf16(a)
    bh, bl = _split_bf16(b)
    return _dot(ah, bh, dims) + (_dot(ah, bl, dims) + _dot(al, bh, dims))


@jax.custom_vjp
def _dot3(a, b):
    return _dot3_dims(a, b, ((1,), (0,)))


def _dot3_fwd(a, b):
    return _dot3(a, b), (a, b)


def _dot3_bwd(res, g):
    a, b = res
    return _dot3_dims(g, b, ((1,), (1,))), _dot3_dims(a, g, ((0,), (0,)))


_dot3.defvjp(_dot3_fwd, _dot3_bwd)


def _mm(a, b, *, ta=False, tb=False, out_dtype=F32, tm=512, tn=512, tk=None, after=None,
        b_shards=False, out_shards=False, name):
    k_dim, m_dim = (a.shape if ta else a.shape[::-1])
    if b_shards:
        n_dim, tn, tk = ((b.shape[1], tn, b.shape[2]) if tb else (b.shape[0] * b.shape[2], b.shape[2], tk))
        assert (b.shape[0] * b.shape[2] if tb else b.shape[1]) == k_dim
    else:
        n_dim = b.shape[0] if tb else b.shape[1]
        assert (b.shape[1] if tb else b.shape[0]) == k_dim
    tm, tn = min(tm, m_dim), min(tn, n_dim)
    tk = k_dim if tk is None else tk
    assert m_dim % tm == 0 and n_dim % tn == 0 and k_dim % tk == 0
    nk = k_dim // tk
    a_spec = (pl.BlockSpec((tk, tm), lambda i, j, k: (k, i)) if ta
              else pl.BlockSpec((tm, tk), lambda i, j, k: (i, k)))
    if b_shards:
        b_spec = (pl.BlockSpec((None, tn, tk), lambda i, j, k: (k, j, 0)) if tb
                  else pl.BlockSpec((None, tk, tn), lambda i, j, k: (j, k, 0)))
    else:
        b_spec = (pl.BlockSpec((tn, tk), lambda i, j, k: (j, k)) if tb
                  else pl.BlockSpec((tk, tn), lambda i, j, k: (k, j)))
    dims = ((0 if ta else 1,), (1 if tb else 0,))

    def body(a_ref, b_ref, *rest):
        o_ref, *scratch = rest[1:] if after is not None else rest
        p = _bdot(a_ref[...], b_ref[...], dims)
        if nk == 1:
            o_ref[...] = p.astype(o_ref.dtype)
        else:
            acc = scratch[0]
            k = pl.program_id(2)

            @pl.when(k == 0)
            def _():
                acc[...] = p

            @pl.when(k > 0)
            def _():
                acc[...] += p

            @pl.when(k == nk - 1)
            def _():
                o_ref[...] = acc[...].astype(o_ref.dtype)

    return pl.pallas_call(
        body, name=name,
        out_shape=jax.ShapeDtypeStruct((n_dim // tn, m_dim, tn) if out_shards else (m_dim, n_dim),
                                       out_dtype),
        grid=(m_dim // tm, n_dim // tn, nk),
        in_specs=[a_spec, b_spec] + ([pl.BlockSpec((8, 128), lambda i, j, k: (0, 0))]
                                     if after is not None else []),
        out_specs=(pl.BlockSpec((None, tm, tn), lambda i, j, k: (j, i, 0)) if out_shards
                   else pl.BlockSpec((tm, tn), lambda i, j, k: (i, j))),
        scratch_shapes=[pltpu.VMEM((tm, tn), F32)] if nk > 1 else [],
        compiler_params=_cparams("parallel", "parallel", "arbitrary"),
    )(a, b, *([after] if after is not None else []))


def _rms_fwd(x, w):
    l, d = x.shape

    def body(x_ref, w_ref, h_ref):
        h_ref[...] = _rmsnorm(x_ref[...], w_ref[...]).astype(BF16)

    return pl.pallas_call(
        body, name="rms_fwd",
        out_shape=jax.ShapeDtypeStruct((l, d), BF16),
        grid=(l // ROW_TILE,),
        in_specs=[pl.BlockSpec((ROW_TILE, d), lambda i: (i, 0)),
                  pl.BlockSpec((1, d), lambda i: (0, 0))],
        out_specs=pl.BlockSpec((ROW_TILE, d), lambda i: (i, 0)),
        compiler_params=_cparams("parallel"),
    )(x, w)


def _rms_bwd(x, w, dh, dres):
    l, d = x.shape

    def body(x_ref, w_ref, dh_ref, dres_ref, dx_ref, dw_ref):
        _, vjp = jax.vjp(_rmsnorm, x_ref[...], w_ref[...])
        dx, dw = vjp(dh_ref[...])
        dx_ref[...] = dx + dres_ref[...]

        @pl.when(pl.program_id(0) == 0)
        def _():
            dw_ref[...] = jnp.zeros_like(dw_ref)

        dw_ref[...] += dw

    row = pl.BlockSpec((ROW_TILE, d), lambda i: (i, 0))
    vec = pl.BlockSpec((1, d), lambda i: (0, 0))
    return pl.pallas_call(
        body, name="rms_bwd",
        out_shape=(jax.ShapeDtypeStruct((l, d), F32), jax.ShapeDtypeStruct((1, d), F32)),
        grid=(l // ROW_TILE,),
        in_specs=[row, vec, row, row],
        out_specs=(row, vec),
        compiler_params=_cparams("arbitrary"),
    )(x, w, dh, dres)


def _final(x, r, fw, target):
    l, d = x.shape

    def per_row_loss(x2, w, tgt):
        err = _rmsnorm(x2, w) - tgt
        return 0.5 * jnp.mean(err * err, axis=-1, keepdims=True)

    def body(x_ref, r_ref, w_ref, t_ref, dx_ref, dxb_ref, loss_ref, dw_ref):
        x2 = x_ref[...] + r_ref[...]
        rows, vjp = jax.vjp(functools.partial(per_row_loss, tgt=t_ref[...]), x2, w_ref[...])
        dx2, dw = vjp(jnp.ones_like(rows))
        dx_ref[...] = dx2
        dxb_ref[...] = dx2.astype(BF16)

        @pl.when(pl.program_id(0) == 0)
        def _():
            dw_ref[...] = jnp.zeros_like(dw_ref)
            loss_ref[...] = jnp.zeros_like(loss_ref)

        dw_ref[...] += dw
        loss_ref[...] += jnp.sum(rows, axis=0, keepdims=True)

    row = pl.BlockSpec((ROW_TILE, d), lambda i: (i, 0))
    vec = pl.BlockSpec((1, d), lambda i: (0, 0))
    return pl.pallas_call(
        body, name="final_norm_loss",
        out_shape=(jax.ShapeDtypeStruct((l, d), F32), jax.ShapeDtypeStruct((l, d), BF16),
                   jax.ShapeDtypeStruct((1, 1), F32), jax.ShapeDtypeStruct((1, d), F32)),
        grid=(l // ROW_TILE,),
        in_specs=[row, row, vec, row],
        out_specs=(row, row, pl.BlockSpec((1, 1), lambda i: (0, 0)), vec),
        compiler_params=_cparams("arbitrary"),
    )(x, r, fw, target)


def _merge_fn(gs, gd, ys, yd):
    return _sigmoid(gs) * ys + _sigmoid(gd) * yd


def _merge_fwd(proj, off_gs, off_gd, ys, yd):
    l, d = ys.shape
    cw = 512
    blk = lambda off: pl.BlockSpec((ROW_TILE, cw), lambda i, j: (i, off // cw + j))

    def body(gs_ref, gd_ref, ys_ref, yd_ref, o_ref):
        o_ref[...] = _merge_fn(gs_ref[...], gd_ref[...], ys_ref[...], yd_ref[...]).astype(BF16)

    return pl.pallas_call(
        body, name="merge_fwd",
        out_shape=jax.ShapeDtypeStruct((l, d), BF16),
        grid=(l // ROW_TILE, d // cw),
        in_specs=[blk(off_gs), blk(off_gd), blk(0), blk(0)],
        out_specs=blk(0),
        compiler_params=_cparams("parallel", "parallel"),
    )(proj, proj, ys, yd)


def _merge_bwd(proj, off_gs, off_gd, ys, yd, dmixed):
    l, d = ys.shape
    cw = 512
    blk = lambda off: pl.BlockSpec((ROW_TILE, cw), lambda i, j: (i, off // cw + j))

    def body(gs_ref, gd_ref, ys_ref, yd_ref, dm_ref, dgs_ref, dgd_ref, dys_ref, dyd_ref):
        _, vjp = jax.vjp(_merge_fn, gs_ref[...], gd_ref[...], ys_ref[...], yd_ref[...])
        dgs, dgd, dys, dyd = vjp(dm_ref[...])
        dgs_ref[...] = dgs.astype(BF16)
        dgd_ref[...] = dgd.astype(BF16)
        dys_ref[...] = dys.astype(BF16)
        dyd_ref[...] = dyd.astype(BF16)

    out = jax.ShapeDtypeStruct((l, d), BF16)
    return pl.pallas_call(
        body, name="merge_bwd",
        out_shape=(out, out, out, out),
        grid=(l // ROW_TILE, d // cw),
        in_specs=[blk(off_gs), blk(off_gd), blk(0), blk(0), blk(0)],
        out_specs=(blk(0), blk(0), blk(0), blk(0)),
        compiler_params=_cparams("parallel", "parallel"),
    )(proj, proj, ys, yd, dmixed)


def _s5_disc_fn(lam_re, lam_im, log_step):
    step = jnp.exp(log_step)
    mag = jnp.exp(lam_re * step)
    abar_re = mag * jnp.cos(lam_im * step)
    abar_im = mag * jnp.sin(lam_im * step)
    den = lam_re * lam_re + lam_im * lam_im
    xr = abar_re - 1.0
    f_re = (xr * lam_re + abar_im * lam_im) / den
    f_im = (abar_im * lam_re - xr * lam_im) / den
    return abar_re, abar_im, f_re, f_im


def _s5_disc_fwd(lam_re, lam_im, log_step):
    g, p = lam_re.shape

    def body(lr_ref, li_ref, ls_ref, ar_ref, ai_ref, fr_ref, fi_ref):
        ar, ai, fr, fi = _s5_disc_fn(lr_ref[...], li_ref[...], ls_ref[...])
        ar_ref[...] = ar
        ai_ref[...] = ai
        fr_ref[...] = fr
        fi_ref[...] = fi

    o = jax.ShapeDtypeStruct((g, p), F32)
    return pl.pallas_call(body, name="s5_disc_fwd", out_shape=(o, o, o, o),
                          compiler_params=_cparams())(lam_re, lam_im, log_step)


def _s5_disc_bwd(lam_re, lam_im, log_step, dar, dai, dfr, dfi):
    g, p = lam_re.shape

    def body(lr_ref, li_ref, ls_ref, dar_ref, dai_ref, dfr_ref, dfi_ref, dlr_ref, dli_ref, dls_ref):
        _, vjp = jax.vjp(_s5_disc_fn, lr_ref[...], li_ref[...], ls_ref[...])
        dlr, dli, dls = vjp((dar_ref[...], dai_ref[...], dfr_ref[...], dfi_ref[...]))
        dlr_ref[...] = dlr
        dli_ref[...] = dli
        dls_ref[...] = dls

    o = jax.ShapeDtypeStruct((g, p), F32)
    return pl.pallas_call(body, name="s5_disc_bwd",
                          out_shape=(o, o, jax.ShapeDtypeStruct((g, 1), F32)),
                          compiler_params=_cparams())(lam_re, lam_im, log_step, dar, dai, dfr, dfi)


def _s5_bbar_fwd(f_re, f_im, b_re, b_im):
    n, c = b_re.shape

    def body(fr_ref, fi_ref, br_ref, bi_ref, or_ref, oi_ref):
        fr, fi, br, bi = fr_ref[...], fi_ref[...], br_ref[...], bi_ref[...]
        or_ref[...] = fr * br - fi * bi
        oi_ref[...] = fr * bi + fi * br

    o = jax.ShapeDtypeStruct((n, c), F32)
    return pl.pallas_call(body, name="s5_bbar_fwd", out_shape=(o, o),
                          compiler_params=_cparams())(f_re, f_im, b_re, b_im)


def _s5_bbar_bwd(f_re, f_im, b_re, b_im, dbr, dbi):
    n, c = b_re.shape

    def body(fr_ref, fi_ref, br_ref, bi_ref, dor_ref, doi_ref, dfr_ref, dfi_ref, dbr_ref, dbi_ref):
        fr, fi, br, bi = fr_ref[...], fi_ref[...], br_ref[...], bi_ref[...]
        dor, doi = dor_ref[...], doi_ref[...]
        dfr_ref[...] = jnp.sum(dor * br + doi * bi, axis=-1, keepdims=True)
        dfi_ref[...] = jnp.sum(doi * br - dor * bi, axis=-1, keepdims=True)
        dbr_ref[...] = fr * dor + fi * doi
        dbi_ref[...] = fr * doi - fi * dor

    col = jax.ShapeDtypeStruct((n, 1), F32)
    o = jax.ShapeDtypeStruct((n, c), F32)
    return pl.pallas_call(body, name="s5_bbar_bwd", out_shape=(col, col, o, o),
                          compiler_params=_cparams())(f_re, f_im, b_re, b_im, dbr, dbi)


SUBLANES = 8


def _scan_groups(xr, xi, ar, ai, reverse):
    t = xr.shape[0]
    sub = lax.broadcasted_iota(jnp.int32, (t, 1), 0) & (SUBLANES - 1)
    pr, pi = ar, ai
    for sh in (1, 2, 4):
        if reverse:
            keep = sub < SUBLANES - sh
            sr, si = pltpu.roll(xr, t - sh, 0), pltpu.roll(xi, t - sh, 0)
        else:
            keep = sub >= sh
            sr, si = pltpu.roll(xr, sh, 0), pltpu.roll(xi, sh, 0)
        sr = jnp.where(keep, sr, 0.0)
        si = jnp.where(keep, si, 0.0)
        xr, xi = xr + pr * sr - pi * si, xi + pr * si + pi * sr
        pr, pi = pr * pr - pi * pi, 2.0 * pr * pi
    return xr, xi


def _scan_rows(xr, xi, ar, ai, cr, ci, sr_ref, si_ref, reverse):
    t, n = xr.shape
    xr, xi = _scan_groups(xr, xi, ar, ai, reverse)
    sr_ref[...] = xr
    si_ref[...] = xi
    sub = lax.broadcasted_iota(jnp.int32, (SUBLANES, n), 0)
    seed = sub == (SUBLANES - 1 if reverse else 0)
    pwr, pwi = _scan_groups(jnp.where(seed, ar, 0.0), jnp.where(seed, ai, 0.0), ar, ai, reverse)
    groups = range(t // SUBLANES)
    edge = 0 if reverse else SUBLANES - 1
    for g in (reversed(groups) if reverse else groups):
        rows = slice(g * SUBLANES, (g + 1) * SUBLANES)
        vr = sr_ref[rows, :] + (pwr * cr - pwi * ci)
        vi = si_ref[rows, :] + (pwr * ci + pwi * cr)
        sr_ref[rows, :] = vr
        si_ref[rows, :] = vi
        cr, ci = vr[edge:edge + 1, :], vi[edge:edge + 1, :]
    return cr, ci


def _s5_states(u_bf, bbr, bbi, ar, ai, cr, ci, sr_ref, si_ref):
    return _scan_rows(_dot(u_bf, bbr), _dot(u_bf, bbi), ar, ai, cr, ci, sr_ref, si_ref, reverse=False)


def _s5_fwd(proj, bbr, bbi, a_re, a_im, ctr, cti, d_skip, d_s5):
    l = proj.shape[0]
    nb, uc, ns = bbr.shape
    t = min(S5_T, l)
    nt = l // t

    def body(u_ref, bbr_ref, bbi_ref, ar_ref, ai_ref, ctr_ref, cti_ref, d_ref,
             y_ref, car_r_ref, car_i_ref, cr, ci, sr_ref, si_ref):
        @pl.when(pl.program_id(1) == 0)
        def _():
            cr[...] = jnp.zeros_like(cr)
            ci[...] = jnp.zeros_like(ci)

        car_r_ref[...] = cr[...]
        car_i_ref[...] = ci[...]
        u = u_ref[...]
        cr[...], ci[...] = _s5_states(u.astype(BF16), bbr_ref[...], bbi_ref[...], ar_ref[...],
                                      ai_ref[...], cr[...], ci[...], sr_ref, si_ref)
        y_ref[...] = (_bdot(sr_ref[...], ctr_ref[...]) - _bdot(si_ref[...], cti_ref[...])
                      + d_ref[...] * u)

    per_block = lambda shape: pl.BlockSpec((None,) + shape, lambda b, n: (b, 0, 0))
    return pl.pallas_call(
        body, name="s5_fwd",
        out_shape=(jax.ShapeDtypeStruct((l, d_s5), F32),
                   jax.ShapeDtypeStruct((nt, 1, nb * ns), F32),
                   jax.ShapeDtypeStruct((nt, 1, nb * ns), F32)),
        grid=(nb, nt),
        in_specs=[pl.BlockSpec((t, uc), lambda b, n: (n, b)),
                  per_block((uc, ns)), per_block((uc, ns)),
                  per_block((1, ns)), per_block((1, ns)),
                  per_block((ns, uc)), per_block((ns, uc)),
                  pl.BlockSpec((1, uc), lambda b, n: (0, b))],
        out_specs=(pl.BlockSpec((t, uc), lambda b, n: (n, b)),
                   pl.BlockSpec((None, 1, ns), lambda b, n: (n, 0, b)),
                   pl.BlockSpec((None, 1, ns), lambda b, n: (n, 0, b))),
        scratch_shapes=[pltpu.VMEM((1, ns), F32), pltpu.VMEM((1, ns), F32),
                        pltpu.VMEM((t, ns), F32), pltpu.VMEM((t, ns), F32)],
        compiler_params=_cparams("parallel", "arbitrary"),
    )(proj, bbr, bbi, a_re, a_im, ctr, cti, d_skip)


def _s5_bwd(proj, dy, bbr, bbi, a_re, a_im, cbr, cbi, d_skip, car_r, car_i):
    l, d_s5 = dy.shape
    nb, uc, ns = bbr.shape
    t = min(S5_T, l)
    nt = l // t

    def body(u_ref, dy_ref, bbr_ref, bbi_ref, ar_ref, ai_ref, cbr_ref, cbi_ref, d_ref,
             car_r_ref, car_i_ref,
             du_ref, dar_ref, dai_ref, dbbr_ref, dbbi_ref, dcbr_ref, dcbi_ref, dd_ref, gcr, gci,
             sr_ref, si_ref, gr_ref, gi_ref):
        @pl.when(pl.program_id(1) == 0)
        def _():
            gcr[...] = jnp.zeros_like(gcr)
            gci[...] = jnp.zeros_like(gci)
            for ref in (dar_ref, dai_ref, dbbr_ref, dbbi_ref, dcbr_ref, dcbi_ref, dd_ref):
                ref[...] = jnp.zeros_like(ref)

        row = lax.broadcasted_iota(jnp.int32, (t, 1), 0)
        u, dy = u_ref[...], dy_ref[...]
        u_bf, dy_bf = u.astype(BF16), dy.astype(BF16)
        ar, ai = ar_ref[...], ai_ref[...]
        cr, ci = car_r_ref[...], car_i_ref[...]
        _s5_states(u_bf, bbr_ref[...], bbi_ref[...], ar, ai, cr, ci, sr_ref, si_ref)
        sr, si = sr_ref[...], si_ref[...]
        first = row == 0
        pr = jnp.where(first, cr, pltpu.roll(sr, 1, 0))
        pi = jnp.where(first, ci, pltpu.roll(si, 1, 0))
        gcr[...], gci[...] = _scan_rows(_dot(dy_bf, cbr_ref[...]), -_dot(dy_bf, cbi_ref[...]), ar, -ai,
                                        gcr[...], gci[...], gr_ref, gi_ref, reverse=True)
        gr, gi = gr_ref[...], gi_ref[...]
        dar_ref[...] += jnp.sum(gr * pr + gi * pi, axis=0, keepdims=True)
        dai_ref[...] += jnp.sum(gi * pr - gr * pi, axis=0, keepdims=True)
        gr_bf, gi_bf = gr.astype(BF16), gi.astype(BF16)
        tn = ((0,), (0,))
        dbbr_ref[...] += _dot(u_bf, gr_bf, tn)
        dbbi_ref[...] += _dot(u_bf, gi_bf, tn)
        dcbr_ref[...] += _dot(dy_bf, sr.astype(BF16), tn)
        dcbi_ref[...] -= _dot(dy_bf, si.astype(BF16), tn)
        nt_dims = ((1,), (1,))
        du = _dot(gr_bf, bbr_ref[...], nt_dims) + _dot(gi_bf, bbi_ref[...], nt_dims) + dy * d_ref[...]
        du_ref[...] = du.astype(BF16)
        dd_ref[...] += jnp.sum(dy * u, axis=0, keepdims=True)

    rev = lambda n: nt - 1 - n
    per_block = lambda shape: pl.BlockSpec((None,) + shape, lambda b, n: (b, 0, 0))
    acc = jax.ShapeDtypeStruct((nb, uc, ns), F32)
    vec = jax.ShapeDtypeStruct((nb, 1, ns), F32)
    return pl.pallas_call(
        body, name="s5_bwd",
        out_shape=(jax.ShapeDtypeStruct((l, d_s5), BF16), vec, vec, acc, acc, acc, acc,
                   jax.ShapeDtypeStruct((1, d_s5), F32)),
        grid=(nb, nt),
        in_specs=[pl.BlockSpec((t, uc), lambda b, n: (rev(n), b)),
                  pl.BlockSpec((t, uc), lambda b, n: (rev(n), b)),
                  per_block((uc, ns)), per_block((uc, ns)),
                  per_block((1, ns)), per_block((1, ns)),
                  per_block((uc, ns)), per_block((uc, ns)),
                  pl.BlockSpec((1, uc), lambda b, n: (0, b)),
                  pl.BlockSpec((None, 1, ns), lambda b, n: (rev(n), 0, b)),
                  pl.BlockSpec((None, 1, ns), lambda b, n: (rev(n), 0, b))],
        out_specs=(pl.BlockSpec((t, uc), lambda b, n: (rev(n), b)),
                   per_block((1, ns)), per_block((1, ns)),
                   per_block((uc, ns)), per_block((uc, ns)),
                   per_block((uc, ns)), per_block((uc, ns)),
                   pl.BlockSpec((1, uc), lambda b, n: (0, b))),
        scratch_shapes=[pltpu.VMEM((1, ns), F32), pltpu.VMEM((1, ns), F32)]
        + [pltpu.VMEM((t, ns), F32)] * 4,
        compiler_params=_cparams("parallel", "arbitrary"),
    )(proj, dy, bbr, bbi, a_re, a_im, cbr, cbi, d_skip, car_r, car_i)


def _s5_glu_fwd(y1, proj, off_z, wglu):
    l, d = y1.shape

    def body(y_ref, z_ref, w_ref, o_ref):
        y2 = _gelu(y_ref[...])
        y3 = y2 * _sigmoid(_bdot(y2, w_ref[...]))
        o_ref[...] = (y3 * _silu(z_ref[...])).astype(BF16)

    return pl.pallas_call(
        body, name="s5_glu_fwd",
        out_shape=jax.ShapeDtypeStruct((l, d), BF16),
        grid=(l // ROW_TILE,),
        in_specs=[pl.BlockSpec((ROW_TILE, d), lambda i: (i, 0)),
                  pl.BlockSpec((ROW_TILE, d), lambda i: (i, off_z // d)),
                  pl.BlockSpec((d, d), lambda i: (0, 0))],
        out_specs=pl.BlockSpec((ROW_TILE, d), lambda i: (i, 0)),
        compiler_params=_cparams("parallel"),
    )(y1, proj, wglu)


def _s5_glu_bwd(y1, proj, off_z, wglu, dout):
    l, d = y1.shape

    def body(y_ref, z_ref, w_ref, do_ref, dy_ref, dz_ref, dw_ref):
        y2, gelu_vjp = jax.vjp(_gelu, y_ref[...])
        z = z_ref[...]
        sz, silu_vjp = jax.vjp(_silu, z)
        y2_bf = y2.astype(BF16)
        sg = _sigmoid(_dot(y2_bf, w_ref[...]))
        dout = do_ref[...]
        dy3 = dout * sz
        dz_ref[...] = silu_vjp(dout * (y2 * sg))[0].astype(BF16)
        dgl = (dy3 * y2 * sg * (1.0 - sg)).astype(BF16)
        dy2 = dy3 * sg + _dot(dgl, w_ref[...], ((1,), (1,)))
        dy_ref[...] = gelu_vjp(dy2)[0]

        @pl.when(pl.program_id(0) == 0)
        def _():
            dw_ref[...] = jnp.zeros_like(dw_ref)

        dw_ref[...] += _dot(y2_bf, dgl, ((0,), (0,)))

    row = pl.BlockSpec((ROW_TILE, d), lambda i: (i, 0))
    full = pl.BlockSpec((d, d), lambda i: (0, 0))
    return pl.pallas_call(
        body, name="s5_glu_bwd",
        out_shape=(jax.ShapeDtypeStruct((l, d), F32), jax.ShapeDtypeStruct((l, d), BF16),
                   jax.ShapeDtypeStruct((d, d), F32)),
        grid=(l // ROW_TILE,),
        in_specs=[row, pl.BlockSpec((ROW_TILE, d), lambda i: (i, off_z // d)), full, row],
        out_specs=(row, row, full),
        compiler_params=_cparams("arbitrary"),
    )(y1, proj, wglu, dout)


def _shift_rows(x, k, back=False):
    if k == 0:
        return x
    t = x.shape[0]
    row = lax.broadcasted_iota(jnp.int32, (t, 1), 0)
    if back:
        return jnp.where(row < t - k, pltpu.roll(x, t - k, 0), 0.0)
    return jnp.where(row >= k, pltpu.roll(x, k, 0), 0.0)


def _dn_conv(x, w_ref):
    return sum(w_ref[CONV_K - 1 - k:CONV_K - k, :] * _shift_rows(x, k) for k in range(CONV_K))


def _dn_post_conv(c, j):
    y = _silu(c)
    n = y * lax.rsqrt(jnp.sum(y * y, axis=-1, keepdims=True) + EPS)
    n = n * jnp.where(j < DN_HEADS, DN_HEAD_DIM ** -0.5, 1.0)
    return jnp.where(j < 2 * DN_HEADS, n, y)


def _dn_prep_fwd(proj, off_qkv, conv_w):
    l = proj.shape[0]
    hd = DN_HEAD_DIM
    nblk = 3 * DN_HEADS

    def body(x_ref, w_ref, o_ref):
        o_ref[...] = _dn_post_conv(_dn_conv(x_ref[...], w_ref), pl.program_id(0))

    return pl.pallas_call(
        body, name="dn_prep_fwd",
        out_shape=jax.ShapeDtypeStruct((l, nblk * hd), F32),
        grid=(nblk,),
        in_specs=[pl.BlockSpec((l, hd), lambda j: (0, off_qkv // hd + j)),
                  pl.BlockSpec((CONV_K, hd), lambda j: (0, j))],
        out_specs=pl.BlockSpec((l, hd), lambda j: (0, j)),
        compiler_params=_cparams("parallel"),
    )(proj, conv_w)


def _dn_prep_bwd(proj, off_qkv, conv_w, dqkv):
    l = proj.shape[0]
    hd = DN_HEAD_DIM
    nblk = 3 * DN_HEADS

    def body(x_ref, w_ref, do_ref, dx_ref, dw_ref):
        x = x_ref[...]
        j = pl.program_id(0)
        _, vjp = jax.vjp(functools.partial(_dn_post_conv, j=j), _dn_conv(x, w_ref))
        dc = vjp(do_ref[...])[0]
        dx = sum(w_ref[CONV_K - 1 - k:CONV_K - k, :] * _shift_rows(dc, k, back=True)
                 for k in range(CONV_K))
        dx_ref[...] = dx.astype(BF16)
        for k in range(CONV_K):
            dw_ref[CONV_K - 1 - k:CONV_K - k, :] = jnp.sum(dc * _shift_rows(x, k), axis=0,
                                                           keepdims=True)

    return pl.pallas_call(
        body, name="dn_prep_bwd",
        out_shape=(jax.ShapeDtypeStruct((l, nblk * hd), BF16),
                   jax.ShapeDtypeStruct((CONV_K, nblk * hd), F32)),
        grid=(nblk,),
        in_specs=[pl.BlockSpec((l, hd), lambda j: (0, off_qkv // hd + j)),
                  pl.BlockSpec((CONV_K, hd), lambda j: (0, j)),
                  pl.BlockSpec((None, l, hd), lambda j: (j // DN_HEADS, 0, j % DN_HEADS))],
        out_specs=(pl.BlockSpec((l, hd), lambda j: (0, j)),
                   pl.BlockSpec((CONV_K, hd), lambda j: (0, j))),
        compiler_params=_cparams("parallel"),
    )(proj, conv_w, dqkv)


def _dn_gate_fn(ba, a_log_row, dt_row):
    lane = lax.broadcasted_iota(jnp.int32, ba.shape, 1)
    beta = _sigmoid(ba)
    g = -jnp.exp(a_log_row) * _softplus(ba + dt_row)
    return jnp.where(lane < DN_HEADS, beta, jnp.where(lane < 2 * DN_HEADS, g, 0.0))


def _dn_gates_fwd(proj, off_ba, a_log_row, dt_row):
    l = proj.shape[0]
    row = pl.BlockSpec((ROW_TILE, 128), lambda i: (i, off_ba // 128))
    vec = pl.BlockSpec((1, 128), lambda i: (0, 0))

    def body(ba_ref, al_ref, dt_ref, o_ref):
        o_ref[...] = _dn_gate_fn(ba_ref[...], al_ref[...], dt_ref[...])

    return pl.pallas_call(
        body, name="dn_gates_fwd",
        out_shape=jax.ShapeDtypeStruct((l, 128), F32),
        grid=(l // ROW_TILE,),
        in_specs=[row, vec, vec],
        out_specs=pl.BlockSpec((ROW_TILE, 128), lambda i: (i, 0)),
        compiler_params=_cparams("parallel"),
    )(proj, a_log_row, dt_row)


def _dn_gates_bwd(proj, off_ba, a_log_row, dt_row, dgb_heads):
    l = proj.shape[0]
    nh = dgb_heads.shape[0]
    row = pl.BlockSpec((ROW_TILE, 128), lambda i: (i, off_ba // 128))
    vec = pl.BlockSpec((1, 128), lambda i: (0, 0))

    def body(ba_ref, al_ref, dt_ref, dg_ref, dba_ref, dal_ref, ddt_ref):
        _, vjp = jax.vjp(_dn_gate_fn, ba_ref[...], al_ref[...], dt_ref[...])
        dgb = dg_ref[0]
        for h in range(1, nh):
            dgb = dgb + dg_ref[h]
        dba, dal, ddt = vjp(dgb)
        dba_ref[...] = dba.astype(BF16)

        @pl.when(pl.program_id(0) == 0)
        def _():
            dal_ref[...] = jnp.zeros_like(dal_ref)
            ddt_ref[...] = jnp.zeros_like(ddt_ref)

        dal_ref[...] += dal
        ddt_ref[...] += ddt

    return pl.pallas_call(
        body, name="dn_gates_bwd",
        out_shape=(jax.ShapeDtypeStruct((l, 128), BF16), jax.ShapeDtypeStruct((1, 128), F32),
                   jax.ShapeDtypeStruct((1, 128), F32)),
        grid=(l // ROW_TILE,),
        in_specs=[row, vec, vec, pl.BlockSpec((nh, ROW_TILE, 128), lambda i: (0, i, 0))],
        out_specs=(pl.BlockSpec((ROW_TILE, 128), lambda i: (i, 0)), vec, vec),
        compiler_params=_cparams("arbitrary"),
    )(proj, a_log_row, dt_row, dgb_heads)


def _dn_chunk_fn(states, qs, ks, vs, gb, heads):
    c = qs[0].shape[0]
    each = lambda f, *lists: [f(*args) for args in zip(*lists)]
    lane = lax.broadcasted_iota(jnp.int32, gb.shape, 1)
    ri = lax.broadcasted_iota(jnp.int32, (c, c), 0)
    ci = lax.broadcasted_iota(jnp.int32, (c, c), 1)
    causal, strict = ri >= ci, ri > ci
    eye = (ri == ci).astype(F32)
    rowi = lax.broadcasted_iota(jnp.int32, (c, 1), 0)
    nt_dims = ((1,), (1,))
    hdot = functools.partial(_dot, precision=HIGHEST)

    pick = lambda m, at: jnp.sum(jnp.where(lane == at, m, 0.0), axis=1, keepdims=True)
    gb_cum = hdot(causal.astype(F32), gb)
    beta = [pick(gb, h) for h in heads]
    gc = [pick(gb_cum, h + DN_HEADS) for h in heads]
    gc_row = each(lambda g: jnp.sum(eye * g, axis=0, keepdims=True), gc)
    decay = each(lambda g, gr: jnp.where(causal, jnp.exp(jnp.where(causal, g - gr, 0.0)), 0.0),
                 gc, gc_row)
    kk = each(lambda k: _bdot(k, k, nt_dims), ks)
    a_mat = each(lambda b, m, dc: jnp.where(strict, b * m * dc, 0.0), beta, kk, decay)

    t_inv = each(lambda a: eye - a, a_mat)
    power = a_mat
    for _ in range(int(math.log2(c)) - 1):
        power = each(lambda p: _bdot(p, p), power)
        t_inv = each(lambda t, p: t + _bdot(t, p), t_inv, power)

    egc = each(jnp.exp, gc)
    u_c = each(lambda t, v, b: _dot3(t, v * b), t_inv, vs, beta)
    w_c = each(lambda t, k, b, e: _dot3(t, k * (b * e)), t_inv, ks, beta, egc)
    qk = each(lambda q, k, dc: _bdot(q, k, nt_dims) * dc, qs, ks, decay)
    g_end = each(lambda g: jnp.sum(jnp.where(rowi == c - 1, g, 0.0), axis=0, keepdims=True), gc)
    v_new = each(lambda u, w, s: u - _bdot(w, s), u_c, w_c, states)
    o = each(lambda q, e, s, m, vn: _bdot(q * e, s) + _bdot(m, vn), qs, egc, states, qk, v_new)
    new_states = each(
        lambda s, ge, k, g, vn: s * jnp.exp(ge) + _bdot(k * jnp.exp(ge - g), vn, ((0,), (0,))),
        states, g_end, ks, gc, v_new)
    return o, new_states


def _dn_chunk_specs(order):
    hd, nh, hps = DN_HEAD_DIM, DN_HEADS, DN_HEADS_PER_STEP
    qkv = lambda part: pl.BlockSpec((CHUNK, hps * hd), lambda h, n: (order(n), part * (nh // hps) + h))
    gb = pl.BlockSpec((CHUNK, 128), lambda h, n: (order(n), 0))
    state = pl.BlockSpec((hps, None, hd, hd), lambda h, n: (h, order(n), 0, 0))
    return qkv, gb, state


def _dn_chunk_fwd(qkv, gb):
    l = qkv.shape[0]
    hd, nh, hps = DN_HEAD_DIM, DN_HEADS, DN_HEADS_PER_STEP
    n_chunks = l // CHUNK
    qkv_spec, gb_spec, state_spec = _dn_chunk_specs(lambda n: n)

    def body(q_ref, k_ref, v_ref, gb_ref, o_ref, s_ref, state):
        @pl.when(pl.program_id(1) == 0)
        def _():
            state[...] = jnp.zeros_like(state)

        cols = [slice(i * hd, (i + 1) * hd) for i in range(hps)]
        states = [state[i] for i in range(hps)]
        for i in range(hps):
            s_ref[i] = states[i]
        o, new_states = _dn_chunk_fn(
            states, [q_ref[:, cs] for cs in cols], [k_ref[:, cs] for cs in cols],
            [v_ref[:, cs] for cs in cols], gb_ref[...],
            [pl.program_id(0) * hps + i for i in range(hps)])
        for i in range(hps):
            o_ref[:, cols[i]] = o[i]
            state[i] = new_states[i]

    return pl.pallas_call(
        body, name="dn_chunk_fwd",
        out_shape=(jax.ShapeDtypeStruct((l, nh * hd), F32),
                   jax.ShapeDtypeStruct((nh, n_chunks, hd, hd), F32)),
        grid=(nh // hps, n_chunks),
        in_specs=[qkv_spec(0), qkv_spec(1), qkv_spec(2), gb_spec],
        out_specs=(pl.BlockSpec((CHUNK, hps * hd), lambda h, n: (n, h)), state_spec),
        scratch_shapes=[pltpu.VMEM((hps, hd, hd), F32)],
        compiler_params=_cparams("parallel", "arbitrary"),
    )(qkv, qkv, qkv, gb)


def _dn_chunk_bwd(qkv, gb, states, do):
    l = qkv.shape[0]
    hd, nh, hps = DN_HEAD_DIM, DN_HEADS, DN_HEADS_PER_STEP
    n_chunks = l // CHUNK
    rev = lambda n: n_chunks - 1 - n
    qkv_spec, gb_spec, state_spec = _dn_chunk_specs(rev)

    def body(q_ref, k_ref, v_ref, gb_ref, s_ref, do_ref, dqkv_ref, dgb_ref, dstate):
        @pl.when(pl.program_id(1) == 0)
        def _():
            dstate[...] = jnp.zeros_like(dstate)

        cols = [slice(i * hd, (i + 1) * hd) for i in range(hps)]
        fn = functools.partial(_dn_chunk_fn, heads=[pl.program_id(0) * hps + i for i in range(hps)])
        _, vjp = jax.vjp(fn, [s_ref[i] for i in range(hps)], [q_ref[:, cs] for cs in cols],
                         [k_ref[:, cs] for cs in cols], [v_ref[:, cs] for cs in cols], gb_ref[...])
        ds, dq, dk, dv, dgb = vjp(([do_ref[:, cs] for cs in cols], [dstate[i] for i in range(hps)]))
        for i in range(hps):
            dstate[i] = ds[i]
            dqkv_ref[0, :, cols[i]] = dq[i]
            dqkv_ref[1, :, cols[i]] = dk[i]
            dqkv_ref[2, :, cols[i]] = dv[i]
        dgb_ref[...] = dgb

    head_out = pl.BlockSpec((CHUNK, hps * hd), lambda h, n: (rev(n), h))
    return pl.pallas_call(
        body, name="dn_chunk_bwd",
        out_shape=(jax.ShapeDtypeStruct((3, l, nh * hd), F32),
                   jax.ShapeDtypeStruct((nh // hps, l, 128), F32)),
        grid=(nh // hps, n_chunks),
        in_specs=[qkv_spec(0), qkv_spec(1), qkv_spec(2), gb_spec, state_spec, head_out],
        out_specs=(pl.BlockSpec((3, CHUNK, hps * hd), lambda h, n: (0, rev(n), h)),
                   pl.BlockSpec((None, CHUNK, 128), lambda h, n: (h, rev(n), 0))),
        scratch_shapes=[pltpu.VMEM((hps, hd, hd), F32)],
        compiler_params=_cparams("parallel", "arbitrary"),
    )(qkv, qkv, qkv, gb, states, do)


def _dn_out_fn(o, z, w):
    return _rmsnorm(o, w) * _silu(z)


def _dn_out_fwd(o, proj, off_z, w):
    l, d = o.shape
    hd = DN_HEAD_DIM
    blk = lambda off: pl.BlockSpec((ROW_TILE, hd), lambda i, h: (i, off // hd + h))

    def body(o_ref, z_ref, w_ref, out_ref):
        out_ref[...] = _dn_out_fn(o_ref[...], z_ref[...], w_ref[...]).astype(BF16)

    return pl.pallas_call(
        body, name="dn_out_fwd",
        out_shape=jax.ShapeDtypeStruct((l, d), BF16),
        grid=(l // ROW_TILE, d // hd),
        in_specs=[blk(0), blk(off_z), pl.BlockSpec((1, hd), lambda i, h: (0, 0))],
        out_specs=blk(0),
        compiler_params=_cparams("parallel", "parallel"),
    )(o, proj, w)


def _dn_out_bwd(o, proj, off_z, w, dout):
    l, d = o.shape
    hd = DN_HEAD_DIM
    blk = lambda off: pl.BlockSpec((ROW_TILE, hd), lambda i, h: (i, off // hd + h))
    vec = pl.BlockSpec((1, hd), lambda i, h: (0, 0))

    def body(o_ref, z_ref, w_ref, dout_ref, do_ref, dz_ref, dw_ref):
        _, vjp = jax.vjp(_dn_out_fn, o_ref[...], z_ref[...], w_ref[...])
        do, dz, dw = vjp(dout_ref[...])
        do_ref[...] = do
        dz_ref[...] = dz.astype(BF16)

        @pl.when((pl.program_id(0) == 0) & (pl.program_id(1) == 0))
        def _():
            dw_ref[...] = jnp.zeros_like(dw_ref)

        dw_ref[...] += dw

    return pl.pallas_call(
        body, name="dn_out_bwd",
        out_shape=(jax.ShapeDtypeStruct((l, d), F32), jax.ShapeDtypeStruct((l, d), BF16),
                   jax.ShapeDtypeStruct((1, hd), F32)),
        grid=(l // ROW_TILE, d // hd),
        in_specs=[blk(0), blk(off_z), vec, blk(0)],
        out_specs=(blk(0), blk(0), vec),
        compiler_params=_cparams("arbitrary", "arbitrary"),
    )(o, proj, w, dout)


def _tile_2d(rows, cols, budget_bytes=1 << 20):
    for tr in (rows, 4096, 2048, 1024, 512, 256, 128, 64, 32, 16):
        if tr <= rows and rows % tr == 0 and tr * cols * 4 <= budget_bytes:
            return tr, cols
    for tc in (2048, 1024, 512, 256, 128):
        if cols % tc == 0 and rows * tc * 4 <= 2 * budget_bytes:
            return rows, tc
    raise ValueError((rows, cols))


def _adamw_update(g, w_ref, m_ref, v_ref, go_ref, d_ref, mo_ref, vo_ref):
    c1 = 1.0 / (1.0 - ADAM_B1 ** ADAM_STEP)
    c2 = 1.0 / (1.0 - ADAM_B2 ** ADAM_STEP)
    m_new = ADAM_B1 * m_ref[...] + (1.0 - ADAM_B1) * g
    v_new = ADAM_B2 * v_ref[...] + (1.0 - ADAM_B2) * (g * g)
    go_ref[...] = g
    mo_ref[...] = m_new
    vo_ref[...] = v_new
    d_ref[...] = -ADAM_LR * ((m_new * c1) / (jnp.sqrt(v_new * c2) + ADAM_EPS) + ADAM_WD * w_ref[...])


def _adamw(w, m, v, gslots, name):
    rows, cols = w.shape
    ns = gslots.shape[0]
    tr, tc = _tile_2d(rows, cols)

    def body(w_ref, m_ref, v_ref, g_ref, go_ref, d_ref, mo_ref, vo_ref):
        g = g_ref[0].astype(F32)
        for s in range(1, ns):
            g = g + g_ref[s].astype(F32)
        _adamw_update(g, w_ref, m_ref, v_ref, go_ref, d_ref, mo_ref, vo_ref)

    blk = pl.BlockSpec((tr, tc), lambda i, j: (i, j))
    o = jax.ShapeDtypeStruct((rows, cols), F32)
    return pl.pallas_call(
        body, name=name, out_shape=(o, o, o, o),
        grid=(rows // tr, cols // tc),
        in_specs=[blk, blk, blk, pl.BlockSpec((ns, tr, tc), lambda i, j: (0, i, j))],
        out_specs=(blk, blk, blk, blk),
        compiler_params=_cparams("parallel", "parallel"),
    )(w, m, v, gslots)


def _slot_sum(gslots, name):
    ns, rows, cols = gslots.shape
    tr, tc = _tile_2d(rows, cols)

    def body(g_ref, o_ref):
        g = g_ref[0]
        for s in range(1, ns):
            g = g + g_ref[s]
        o_ref[...] = g

    return pl.pallas_call(
        body, name=name, out_shape=jax.ShapeDtypeStruct((rows, cols), F32),
        grid=(rows // tr, cols // tc),
        in_specs=[pl.BlockSpec((ns, tr, tc), lambda i, j: (0, i, j))],
        out_specs=pl.BlockSpec((tr, tc), lambda i, j: (i, j)),
        compiler_params=_cparams("parallel", "parallel"),
    )(gslots)


HBM_SPEC = pl.BlockSpec(memory_space=pl.ANY)


def _all_gather(arrs, name):
    n = len(arrs)

    def body(*refs):
        ins, outs = refs[:n], refs[n:2 * n]
        send_sems, recv_sems, local_sems = refs[2 * n:]
        x, y, c = lax.axis_index("x"), lax.axis_index("y"), lax.axis_index("c")
        me, sibling = (x, y, c), (x, y, 1 - c)
        chips = [(1 - x, y), (x, 1 - y), (1 - x, 1 - y)]
        index = lambda px, py, pc: 4 * px + 2 * py + pc

        def copy(a, k, block, to, src=None):
            rows = outs[a].at[index(*block)]
            return pltpu.make_async_remote_copy(
                src_ref=rows if src is None else src, dst_ref=rows,
                send_sem=send_sems.at[a, k], recv_sem=recv_sems.at[a, k],
                device_id=to, device_id_type=MESH)

        mine = [pltpu.make_async_copy(ins[a], outs[a].at[index(*me)], local_sems.at[a])
                for a in range(n)]
        for cp in mine:
            cp.start()
        first = []
        for a in range(n):
            first.append(copy(a, 0, me, sibling, src=ins[a]))
            first += [copy(a, 1 + j, me, (*chip, c), src=ins[a]) for j, chip in enumerate(chips)]
        for cp in first:
            cp.start()
        passed = []
        for j, chip in enumerate(chips):
            for a in range(n):
                copy(a, 1 + j, (*chip, c), me).wait_recv()
                fwd = copy(a, 4 + j, (*chip, c), sibling)
                fwd.start()
                passed.append(fwd)
        for a in range(n):
            copy(a, 0, sibling, me).wait_recv()
            for j, chip in enumerate(chips):
                copy(a, 4 + j, (*chip, 1 - c), me).wait_recv()
        for cp in first + passed:
            cp.wait_send()
        for cp in mine:
            cp.wait()

    return pl.pallas_call(
        body, name=name,
        out_shape=[jax.ShapeDtypeStruct((N_DEV,) + a.shape, a.dtype) for a in arrs],
        in_specs=[HBM_SPEC] * n, out_specs=[HBM_SPEC] * n,
        scratch_shapes=[pltpu.SemaphoreType.DMA((n, 7)), pltpu.SemaphoreType.DMA((n, 7)),
                        pltpu.SemaphoreType.DMA((n,))],
    )(*arrs)


def _sibling_swap(arrs, name):
    n = len(arrs)

    def body(*refs):
        ins, outs = refs[:n], refs[n:2 * n]
        send_sems, recv_sems = refs[2 * n:]
        x, y, c = lax.axis_index("x"), lax.axis_index("y"), lax.axis_index("c")
        copies = [pltpu.make_async_remote_copy(
            src_ref=ins[a].at[1 - c], dst_ref=outs[a],
            send_sem=send_sems.at[a], recv_sem=recv_sems.at[a],
            device_id=(x, y, 1 - c), device_id_type=MESH) for a in range(n)]
        for cp in copies:
            cp.start()
        for cp in copies:
            cp.wait()

    return pl.pallas_call(
        body, name=name,
        out_shape=[jax.ShapeDtypeStruct(a.shape[1:], a.dtype) for a in arrs],
        in_specs=[HBM_SPEC] * n, out_specs=[HBM_SPEC] * n,
        scratch_shapes=[pltpu.SemaphoreType.DMA((n,)), pltpu.SemaphoreType.DMA((n,))],
    )(*arrs)


def _pair_sum(mine, theirs, core, name):
    _, rows, cols = mine.shape
    tr, tc = _tile_2d(rows, cols, budget_bytes=2 << 20)

    def body(core_ref, a_ref, b_ref, o_ref):
        o_ref[...] = (a_ref[...].astype(F32) + b_ref[...].astype(F32)).astype(o_ref.dtype)

    return pl.pallas_call(
        body, name=name, out_shape=jax.ShapeDtypeStruct((rows, cols), mine.dtype),
        grid_spec=pltpu.PrefetchScalarGridSpec(
            num_scalar_prefetch=1, grid=(rows // tr, cols // tc),
            in_specs=[pl.BlockSpec((None, tr, tc), lambda i, j, core_ref: (core_ref[0], i, j)),
                      pl.BlockSpec((tr, tc), lambda i, j, core_ref: (i, j))],
            out_specs=pl.BlockSpec((tr, tc), lambda i, j, core_ref: (i, j))),
        compiler_params=_cparams("parallel", "parallel"),
    )(core, mine, theirs)


HBM_ONLY = pl.BlockSpec(memory_space=pltpu.HBM)
SEM_SPEC = pl.BlockSpec(memory_space=pltpu.SEMAPHORE)
SPLIT_COPY_EFFECT = pltpu.SideEffectType.DATAFLOW_SIDE_EFFECTING


def _flip(v, bit):
    return 1 - v if bit else v


def _chip_slices_plan(n):
    def plan():
        x, y, c = lax.axis_index("x"), lax.axis_index("y"), lax.axis_index("c")
        copies = []
        for k in range(1, 4):
            px, py = _flip(x, k & 2), _flip(y, k & 1)
            copies += [(a, 2 * px + py, 2 * x + y, (px, py, c)) for a in range(n)]
        return copies
    return plan, 3 * n


def _gather_plan(n):
    def plan():
        x, y, c = lax.axis_index("x"), lax.axis_index("y"), lax.axis_index("c")
        copies = []
        for k in range(1, N_DEV):
            peer = (_flip(x, k & 4), _flip(y, k & 2), _flip(c, k & 1))
            copies += [(a, None, 4 * x + 2 * y + c, peer) for a in range(n)]
        return copies
    return plan, 7 * n


def _planned_copies(plan, srcs, lands, send_sems, recv_sems):
    return [pltpu.make_async_remote_copy(
        src_ref=srcs[a] if src_at is None else srcs[a].at[src_at], dst_ref=lands[a].at[land_at],
        send_sem=send_sems[i], recv_sem=recv_sems[i], device_id=peer, device_id_type=MESH)
        for i, (a, src_at, land_at, peer) in enumerate(plan())]


def _split_exchange_start(plan_and_count, arrs, land_shapes, name, after=None):
    plan, n_sems = plan_and_count
    n = len(arrs)

    n_in = 2 * n + (after is not None)

    def body(*refs):
        srcs, lands = refs[:n], refs[n:2 * n]
        send_sems, recv_sems = refs[n_in:n_in + n_sems], refs[n_in + n_sems:n_in + 2 * n_sems]
        token = refs[-1]
        for copy in _planned_copies(plan, srcs, lands, send_sems, recv_sems):
            copy.start()
        token[...] = jnp.zeros_like(token)

    hbm = lambda a: pltpu.HBM(a.shape, a.dtype)
    operands = [pltpu.with_memory_space_constraint(a, pltpu.HBM) for a in arrs]
    operands += [pltpu.with_memory_space_constraint(lax.empty(shape, a.dtype), pltpu.HBM)
                 for a, shape in zip(arrs, land_shapes)]
    out = pl.pallas_call(
        body, name=name,
        out_shape=(*[pltpu.SemaphoreType.DMA(())] * (2 * n_sems),
                   *[hbm(a) for a in operands],
                   jax.ShapeDtypeStruct((8, 128), F32)),
        in_specs=[HBM_ONLY] * (2 * n) + [pl.BlockSpec(memory_space=pl.ANY)] * (after is not None),
        out_specs=(*[SEM_SPEC] * (2 * n_sems), *[HBM_ONLY] * (2 * n),
                   pl.BlockSpec(memory_space=pltpu.VMEM)),
        input_output_aliases={i: 2 * n_sems + i for i in range(2 * n)},
        compiler_params=pltpu.CompilerParams(has_side_effects=SPLIT_COPY_EFFECT),
    )(*operands, *([after] if after is not None else []))
    sems, rest = list(out[:2 * n_sems]), out[2 * n_sems:]
    return sems, list(rest[:n]), list(rest[n:2 * n]), rest[-1]


def _split_exchange_wait(plan_and_count, sems, srcs, lands, after, name):
    plan, n_sems = plan_and_count
    n = len(srcs)

    def body(*refs):
        src_refs, land_refs = refs[:n], refs[n:2 * n]
        send_sems, recv_sems = refs[2 * n:2 * n + n_sems], refs[2 * n + n_sems:2 * n + 2 * n_sems]
        for copy in _planned_copies(plan, src_refs, land_refs, send_sems, recv_sems):
            copy.wait_send()
            copy.wait_recv()

    hbm = lambda a: pltpu.HBM(a.shape, a.dtype)
    out = pl.pallas_call(
        body, name=name,
        out_shape=(*[hbm(a) for a in srcs], *[hbm(a) for a in lands]),
        in_specs=[HBM_ONLY] * (2 * n) + [SEM_SPEC] * (2 * n_sems) + [pl.BlockSpec(memory_space=pl.ANY)],
        out_specs=tuple([HBM_ONLY] * (2 * n)),
        input_output_aliases={i: i for i in range(2 * n)},
        compiler_params=pltpu.CompilerParams(has_side_effects=SPLIT_COPY_EFFECT),
    )(*srcs, *lands, *sems, after)
    return list(out[n:])


def _adamw_exchanged(w, m, v, own, landed, chip, name):
    rows, cols = w.shape
    tr, tc = _tile_2d(rows, cols)

    def body(chip_ref, w_ref, m_ref, v_ref, own_ref, l1_ref, l2_ref, l3_ref, go_ref, d_ref, mo_ref, vo_ref):
        g = own_ref[...].astype(F32)
        for ref in (l1_ref, l2_ref, l3_ref):
            g = g + ref[...].astype(F32)
        _adamw_update(g, w_ref, m_ref, v_ref, go_ref, d_ref, mo_ref, vo_ref)

    blk = pl.BlockSpec((tr, tc), lambda i, j, chip_ref: (i, j))
    slot = lambda k: pl.BlockSpec((None, tr, tc), lambda i, j, chip_ref: (chip_ref[0] ^ k, i, j))
    o = jax.ShapeDtypeStruct((rows, cols), F32)
    return pl.pallas_call(
        body, name=name, out_shape=(o, o, o, o),
        grid_spec=pltpu.PrefetchScalarGridSpec(
            num_scalar_prefetch=1, grid=(rows // tr, cols // tc),
            in_specs=[blk, blk, blk, slot(0), slot(1), slot(2), slot(3)],
            out_specs=(blk, blk, blk, blk)),
        compiler_params=_cparams("parallel", "parallel"),
    )(chip, w, m, v, own, landed, landed, landed)


def _block_diag(t):
    nb, gpb, r, c = t.shape
    eye = jnp.eye(gpb, dtype=t.dtype)
    return jnp.einsum("ngrc,gh->ngrhc", t, eye).reshape(nb, gpb * r, gpb * c)


def _diag_blocks(t, r, c):
    nb = t.shape[0]
    gpb = t.shape[1] // r
    t = t.reshape(nb, gpb, r, gpb, c)
    return jnp.einsum("ngrhc,gh->ngrc", t, jnp.eye(gpb, dtype=t.dtype))


def _pack_rows(parts):
    flat = jnp.concatenate([p.reshape(-1).astype(F32) for p in parts])
    pad = (-flat.shape[0]) % (256 * 128)
    return jnp.pad(flat, (0, pad)).reshape(-1, 128)


def _unpack_rows(packed, shapes):
    flat = packed.reshape(-1)
    out, at = [], 0
    for shape in shapes:
        size = math.prod(shape)
        out.append(flat[at:at + size].reshape(shape))
        at += size
    return out


def kernel(x, ln_w, w_in, s5_lam_re, s5_lam_im, s5_log_step, s5_b_re, s5_b_im, s5_c_re, s5_c_im, s5_d, s5_w_glu, s5_w_up, dn_conv_w, dn_a_log, dn_dt_bias, dn_norm_w, dn_w_up, w_out, final_norm_w, loss_target, m_ln_w, m_w_in, m_s5_lam_re, m_s5_lam_im, m_s5_log_step, m_s5_b_re, m_s5_b_im, m_s5_c_re, m_s5_c_im, m_s5_d, m_s5_w_glu, m_s5_w_up, m_dn_conv_w, m_dn_a_log, m_dn_dt_bias, m_dn_norm_w, m_dn_w_up, m_w_out, m_final_norm_w, v_ln_w, v_w_in, v_s5_lam_re, v_s5_lam_im, v_s5_log_step, v_s5_b_re, v_s5_b_im, v_s5_c_re, v_s5_c_im, v_s5_d, v_s5_w_glu, v_s5_w_up, v_dn_conv_w, v_dn_a_log, v_dn_dt_bias, v_dn_norm_w, v_dn_w_up, v_w_out, v_final_norm_w):
    weights = dict(ln_w=ln_w, w_in=w_in, s5_lam_re=s5_lam_re, s5_lam_im=s5_lam_im,
                   s5_log_step=s5_log_step, s5_b_re=s5_b_re, s5_b_im=s5_b_im, s5_c_re=s5_c_re,
                   s5_c_im=s5_c_im, s5_d=s5_d, s5_w_glu=s5_w_glu, s5_w_up=s5_w_up,
                   dn_conv_w=dn_conv_w, dn_a_log=dn_a_log, dn_dt_bias=dn_dt_bias,
                   dn_norm_w=dn_norm_w, dn_w_up=dn_w_up, w_out=w_out, final_norm_w=final_norm_w)
    mom_m = dict(ln_w=m_ln_w, w_in=m_w_in, s5_lam_re=m_s5_lam_re, s5_lam_im=m_s5_lam_im,
                 s5_log_step=m_s5_log_step, s5_b_re=m_s5_b_re, s5_b_im=m_s5_b_im,
                 s5_c_re=m_s5_c_re, s5_c_im=m_s5_c_im, s5_d=m_s5_d, s5_w_glu=m_s5_w_glu,
                 s5_w_up=m_s5_w_up, dn_conv_w=m_dn_conv_w, dn_a_log=m_dn_a_log,
                 dn_dt_bias=m_dn_dt_bias, dn_norm_w=m_dn_norm_w, dn_w_up=m_dn_w_up,
                 w_out=m_w_out, final_norm_w=m_final_norm_w)
    mom_v = dict(ln_w=v_ln_w, w_in=v_w_in, s5_lam_re=v_s5_lam_re, s5_lam_im=v_s5_lam_im,
                 s5_log_step=v_s5_log_step, s5_b_re=v_s5_b_re, s5_b_im=v_s5_b_im,
                 s5_c_re=v_s5_c_re, s5_c_im=v_s5_c_im, s5_d=v_s5_d, s5_w_glu=v_s5_w_glu,
                 s5_w_up=v_s5_w_up, dn_conv_w=v_dn_conv_w, dn_a_log=v_dn_a_log,
                 dn_dt_bias=v_dn_dt_bias, dn_norm_w=v_dn_norm_w, dn_w_up=v_dn_w_up,
                 w_out=v_w_out, final_norm_w=v_final_norm_w)
    names = list(weights)

    l, d = x.shape[1], x.shape[2]
    d_s5 = d // 2
    groups = d_s5 // S5_GROUP
    nb = groups // S5_GPB
    d_dn = DN_HEADS * DN_HEAD_DIM
    w_in_cols = w_in.shape[2]
    d_in = N_DEV * w_in_cols
    off_ba_src = 2 * d_s5 + 4 * d_dn
    off_u, off_zs, off_qkv, off_zd = 0, d_s5, 2 * d_s5, 2 * d_s5 + 3 * d_dn
    off_ba = off_zd + d_dn
    off_gs = off_ba + BA_PAD
    off_gd = off_gs + d
    n_proj = off_gd + d
    x2d, tgt2d = x[0], loss_target[0]
    my_index = 4 * lax.axis_index("x") + 2 * lax.axis_index("y") + lax.axis_index("c")

    g_win, g_conv = _all_gather([jnp.transpose(w_in[0]).astype(BF16), dn_conv_w[0]], name="gather_weights")
    late_plan = _gather_plan(4)
    late_shards = [s5_w_glu[0].astype(BF16), s5_w_up[0].astype(BF16), dn_w_up[0].astype(BF16),
                   w_out[0].astype(BF16)]
    late_sems, late_shards, late_lands, late_token = _split_exchange_start(
        late_plan, late_shards, [(N_DEV,) + s.shape for s in late_shards], name="gather_late_start",
        after=g_conv)
    ba_end = off_ba_src + 2 * DN_HEADS
    ba_zeros = BA_PAD - 2 * DN_HEADS
    w_full_t = g_win.reshape(d_in, d)
    w_cat_t = jnp.concatenate(
        [w_full_t[:ba_end], jnp.zeros((ba_zeros, d), BF16), w_full_t[ba_end:]], axis=0)
    conv_full = jnp.transpose(g_conv, (1, 0, 2)).reshape(CONV_K, 3 * d_dn)

    lam_re, lam_im = s5_lam_re[0], s5_lam_im[0]
    log_step = s5_log_step[0].reshape(groups, 1)
    b_re = s5_b_re[0].reshape(groups * S5_STATE, S5_GROUP)
    b_im = s5_b_im[0].reshape(groups * S5_STATE, S5_GROUP)
    abar_re, abar_im, f_re, f_im = _s5_disc_fwd(lam_re, lam_im, log_step)
    f_re_col, f_im_col = f_re.reshape(-1, 1), f_im.reshape(-1, 1)
    bb_re, bb_im = _s5_bbar_fwd(f_re_col, f_im_col, b_re, b_im)

    def bb_blocks(t):
        t = t.reshape(nb, S5_GPB, S5_STATE, S5_GROUP).transpose(0, 1, 3, 2)
        return _block_diag(t).astype(BF16)

    def c_blocks(t):
        return _block_diag(t.reshape(nb, S5_GPB, S5_GROUP, S5_STATE)).astype(BF16)

    bbr, bbi = bb_blocks(bb_re), bb_blocks(bb_im)
    cbr, cbi = c_blocks(s5_c_re[0]), c_blocks(s5_c_im[0])
    ctr, cti = jnp.transpose(cbr, (0, 2, 1)), jnp.transpose(cbi, (0, 2, 1))
    a_re = abar_re.reshape(nb, 1, S5_GPB * S5_STATE)
    a_im = abar_im.reshape(nb, 1, S5_GPB * S5_STATE)

    h = _rms_fwd(x2d, ln_w)
    proj = _mm(h, w_cat_t, tb=True, tn=1536, after=late_token, name="proj")
    y1, car_r, car_i = _s5_fwd(proj, bbr, bbi, a_re, a_im, ctr, cti, s5_d, d_s5)
    a_log_row = jnp.pad(dn_a_log, ((0, 0), (DN_HEADS, 128 - 2 * DN_HEADS)))
    dt_row = jnp.pad(dn_dt_bias, ((0, 0), (DN_HEADS, 128 - 2 * DN_HEADS)))
    qkv = _dn_prep_fwd(proj, off_qkv, conv_full)
    gb = _dn_gates_fwd(proj, off_ba, a_log_row, dt_row)
    o_dn, states = _dn_chunk_fwd(qkv, gb)

    late_lands = _split_exchange_wait(late_plan, late_sems, late_shards, late_lands, o_dn,
                                      name="gather_late_wait")
    g_glu, g_sup, g_dup, g_wout = [
        lax.dynamic_update_slice(land, shard[None], (my_index, 0, 0))
        for land, shard in zip(late_lands, late_shards)]
    wglu_full = g_glu.reshape(d_s5, d_s5)
    wout_full = g_wout.reshape(d, d)

    out_s = _s5_glu_fwd(y1, proj, off_zs, wglu_full)
    y_s = _mm(out_s, g_sup, b_shards=True, name="s5_up")
    out_d = _dn_out_fwd(o_dn, proj, off_zd, dn_norm_w)
    y_d = _mm(out_d, g_dup, b_shards=True, name="dn_up")

    mixed = _merge_fwd(proj, off_gs, off_gd, y_s, y_d)
    branch = _mm(mixed, wout_full, name="w_out")
    dx2, dx2_bf, loss_dev, d_final_w = _final(x2d, branch, final_norm_w.reshape(1, d), tgt2d)

    g_wout_full = _mm(mixed, dx2_bf, ta=True, out_dtype=BF16, name="grad_w_out")
    dmixed = _mm(dx2_bf, wout_full, tb=True, name="d_mixed")
    dgs, dgd, dys, dyd = _merge_bwd(proj, off_gs, off_gd, y_s, y_d, dmixed)

    shard_cols = d // N_DEV
    g_dup_shards = _mm(out_d, dyd, ta=True, out_dtype=BF16, tn=shard_cols, out_shards=True,
                       name="grad_dn_up")
    dout_d = _mm(dyd, g_dup, tb=True, b_shards=True, name="d_out_d")
    do_dn, dzd, d_norm_w = _dn_out_bwd(o_dn, proj, off_zd, dn_norm_w, dout_d)
    dqkv, dgb_heads = _dn_chunk_bwd(qkv, gb, states, do_dn)
    dba, d_a_log_row, d_dt_row = _dn_gates_bwd(proj, off_ba, a_log_row, dt_row, dgb_heads)
    dqkv_pre, d_conv_full = _dn_prep_bwd(proj, off_qkv, conv_full, dqkv)

    g_sup_shards = _mm(out_s, dys, ta=True, out_dtype=BF16, tn=shard_cols, out_shards=True,
                       name="grad_s5_up")
    dout_s = _mm(dys, g_sup, tb=True, b_shards=True, name="d_out_s")
    dy1, dzs, g_glu_full = _s5_glu_bwd(y1, proj, off_zs, wglu_full, dout_s)

    def by_dest(t, stacked=False):
        if stacked:
            return t.reshape(4, 2, *t.shape[1:]).transpose(1, 0, 2, 3)
        return t.reshape(4, 2, t.shape[0] // N_DEV, t.shape[1]).transpose(1, 0, 2, 3)

    core = lax.axis_index("c").astype(jnp.int32).reshape(1)
    chip = (2 * lax.axis_index("x") + lax.axis_index("y")).astype(jnp.int32).reshape(1)

    def chip_sums_of(which, parts, tag):
        from_sibling = _sibling_swap(parts, name="swap_grads_" + tag)
        return [_pair_sum(p.reshape(2, -1, p.shape[-1]), got.reshape(-1, got.shape[-1]), core,
                          name="pair_sum_" + nm).reshape(got.shape)
                for nm, p, got in zip(which, parts, from_sibling)]

    early = ["s5_w_glu", "s5_w_up", "dn_w_up", "w_out"]
    sums_a = chip_sums_of(early, [by_dest(g_glu_full.astype(BF16)), by_dest(g_sup_shards, stacked=True),
                                  by_dest(g_dup_shards, stacked=True), by_dest(g_wout_full)], "a")
    plan_a = _chip_slices_plan(len(sums_a))
    sems_a, src_a, land_a, token_a = _split_exchange_start(
        plan_a, sums_a, [t.shape for t in sums_a], name="exchange_start_a")

    (du, d_a_re, d_a_im, d_bbr, d_bbi, d_cbr, d_cbi, d_s5_d) = _s5_bwd(
        proj, dy1, bbr, bbi, a_re, a_im, cbr, cbi, s5_d + token_a[:1, :1], car_r, car_i)

    def from_bb_blocks(t):
        t = _diag_blocks(t, S5_GROUP, S5_STATE).transpose(0, 1, 3, 2)
        return t.reshape(groups * S5_STATE, S5_GROUP)

    d_f_re, d_f_im, d_b_re, d_b_im = _s5_bbar_bwd(f_re_col, f_im_col, b_re, b_im,
                                                 from_bb_blocks(d_bbr), from_bb_blocks(d_bbi))
    d_lam_re, d_lam_im, d_log_step = _s5_disc_bwd(
        lam_re, lam_im, log_step, d_a_re.reshape(groups, S5_STATE), d_a_im.reshape(groups, S5_STATE),
        d_f_re.reshape(groups, S5_STATE), d_f_im.reshape(groups, S5_STATE))
    d_c_re = _diag_blocks(d_cbr, S5_GROUP, S5_STATE).reshape(groups, S5_GROUP, S5_STATE)
    d_c_im = _diag_blocks(d_cbi, S5_GROUP, S5_STATE).reshape(groups, S5_GROUP, S5_STATE)

    dproj = jnp.concatenate(
        [du, dzs, dqkv_pre, dzd, jnp.pad(dba, ((0, 0), (0, BA_PAD - 128))), dgs, dgd], axis=1)
    g_wcat_t = _mm(dproj, h, ta=True, out_dtype=BF16, tm=768, tn=d, name="grad_w_in")
    g_win_full_t = jnp.concatenate([g_wcat_t[:ba_end], g_wcat_t[ba_end + ba_zeros:]], axis=0)
    sums_b = chip_sums_of(["w_in"], [by_dest(g_win_full_t)], "b")
    plan_b = _chip_slices_plan(1)
    sems_b, src_b, land_b, token_b = _split_exchange_start(
        plan_b, sums_b, [t.shape for t in sums_b], name="exchange_start_b")
    dh = _mm(dproj, w_cat_t, tm=1024, tn=1024, tk=1536, after=token_b, name="d_h")
    grad_x, d_ln_w = _rms_bwd(x2d, ln_w, dh, dx2)
    big = ["w_in"] + early
    results = {}

    small = [nm for nm in names if nm not in big]
    small_grads = dict(
        ln_w=d_ln_w, s5_lam_re=d_lam_re, s5_lam_im=d_lam_im, s5_log_step=d_log_step,
        s5_b_re=d_b_re, s5_b_im=d_b_im, s5_c_re=d_c_re, s5_c_im=d_c_im, s5_d=d_s5_d,
        dn_conv_w=d_conv_full, dn_a_log=d_a_log_row[:, DN_HEADS:2 * DN_HEADS],
        dn_dt_bias=d_dt_row[:, DN_HEADS:2 * DN_HEADS], dn_norm_w=d_norm_w, final_norm_w=d_final_w)
    (all_small,) = _all_gather([_pack_rows([small_grads[nm] for nm in small])], name="gather_small_grads")
    summed = _slot_sum(all_small, name="sum_small_grads")
    full_shapes = [(CONV_K, 3 * d_dn) if nm == "dn_conv_w" else weights[nm].shape for nm in small]
    g_small = dict(zip(small, _unpack_rows(summed, full_shapes)))
    conv_cols = dn_conv_w.shape[2]
    g_small["dn_conv_w"] = lax.dynamic_slice_in_dim(
        g_small["dn_conv_w"], my_index * conv_cols, conv_cols, axis=1).reshape(dn_conv_w.shape)
    packed = [_pack_rows([t[nm] for nm in small]) for t in (weights, mom_m, mom_v, g_small)]
    small_out = _adamw(packed[0], packed[1], packed[2], packed[3][None], name="adamw_small")
    small_shapes = [weights[nm].shape for nm in small]
    for kind, packed_out in enumerate(small_out):
        for nm, val in zip(small, _unpack_rows(packed_out, small_shapes)):
            results.setdefault(nm, [None] * 4)[kind] = val

    land_a = _split_exchange_wait(plan_a, sems_a, src_a, land_a, small_out[0], name="exchange_wait_a")
    for nm, own, landed in zip(early, src_a, land_a):
        results[nm] = _adamw_exchanged(weights[nm][0], mom_m[nm][0], mom_v[nm][0], own, landed, chip,
                                       name="adamw_" + nm)
    (land_b,) = _split_exchange_wait(plan_b, sems_b, src_b, land_b, results[early[-1]][0],
                                     name="exchange_wait_b")
    res = _adamw_exchanged(jnp.transpose(w_in[0]), jnp.transpose(m_w_in[0]), jnp.transpose(v_w_in[0]),
                           src_b[0], land_b, chip, name="adamw_w_in")
    results["w_in"] = [jnp.transpose(t) for t in res]

    loss = lax.psum(loss_dev[0, 0], ("x", "y", "c"))
    outs = [loss, grad_x[None]]
    for kind in range(4):
        outs += [results[nm][kind].reshape(weights[nm].shape) for nm in names]
    return tuple(outs)
```

```python
import functools
import math

import jax
import jax.numpy as jnp
from jax import lax
from jax.experimental import pallas as pl
from jax.experimental.pallas import tpu as pltpu

F32 = jnp.float32
BF16 = jnp.bfloat16
HIGHEST = lax.Precision.HIGHEST
MESH = pl.DeviceIdType.MESH
N_DEV = 8

EPS = 1e-6
S5_GROUP = 16
S5_STATE = 64
S5_GPB = 8
S5_T = 256
DN_HEADS = 8
DN_HEAD_DIM = 128
CHUNK = 64
DN_HEADS_PER_STEP = 8
CONV_K = 4
BA_PAD = 512

ADAM_LR = 0.001
ADAM_B1 = 0.9
ADAM_B2 = 0.999
ADAM_EPS = 1e-08
ADAM_WD = 0.01
ADAM_STEP = 10

VMEM_LIMIT_BYTES = 48 * 1024 * 1024
ROW_TILE = 256


def _cparams(*sem):
    return pltpu.CompilerParams(dimension_semantics=sem if sem else None,
                                vmem_limit_bytes=VMEM_LIMIT_BYTES)


def _sigmoid(x):
    return 1.0 / (1.0 + jnp.exp(-x))


def _silu(x):
    return x * _sigmoid(x)


def _gelu(x):
    return 0.5 * x * (1.0 + jnp.tanh(0.7978845608028654 * (x + 0.044715 * x * x * x)))


def _softplus(x):
    return jnp.maximum(x, 0.0) + jnp.log(1.0 + jnp.exp(-jnp.abs(x)))


def _rmsnorm(x, w):
    return x * lax.rsqrt(jnp.mean(x * x, axis=-1, keepdims=True) + EPS) * w


def _dot(a, b, dims=((1,), (0,)), precision=None):
    return lax.dot_general(a, b, (dims, ((), ())), precision=precision,
                           preferred_element_type=F32)


def _bdot(a, b, dims=((1,), (0,))):
    return _dot(a.astype(BF16), b.astype(BF16), dims)


def _split_bf16(a):
    hi = a.astype(BF16)
    return hi, (a - hi.astype(F32)).astype(BF16)


def _dot3_dims(a, b, dims):
    ah, al = _split_bf16(a)
    bh, bl = _split_bf16(b)
    return _dot(ah, bh, dims) + (_dot(ah, bl, dims) + _dot(al, bh, dims))


@jax.custom_vjp
def _dot3(a, b):
    return _dot3_dims(a, b, ((1,), (0,)))


def _dot3_fwd(a, b):
    return _dot3(a, b), (a, b)


def _dot3_bwd(res, g):
    a, b = res
    return _dot3_dims(g, b, ((1,), (1,))), _dot3_dims(a, g, ((0,), (0,)))


_dot3.defvjp(_dot3_fwd, _dot3_bwd)


def _mm(a, b, *, ta=False, tb=False, out_dtype=F32, tm=512, tn=512, tk=None, after=None, b_rows=None,
        addend=None, name):
    k_dim, m_dim = (a.shape if ta else a.shape[::-1])
    b_rows = b.shape[0] if b_rows is None else b_rows
    n_dim = b_rows if tb else b.shape[1]
    assert (b.shape[1] if tb else b_rows) == k_dim and b_rows <= b.shape[0]
    tm, tn = min(tm, m_dim), min(tn, n_dim)
    tk = k_dim if tk is None else tk
    assert m_dim % tm == 0 and n_dim % tn == 0 and k_dim % tk == 0
    nk = k_dim // tk
    a_spec = (pl.BlockSpec((tk, tm), lambda i, j, k: (k, i)) if ta
              else pl.BlockSpec((tm, tk), lambda i, j, k: (i, k)))
    b_spec = (pl.BlockSpec((tn, tk), lambda i, j, k: (j, k)) if tb
              else pl.BlockSpec((tk, tn), lambda i, j, k: (k, j)))
    dims = ((0 if ta else 1,), (1 if tb else 0,))

    extras = ([after] if after is not None else []) + ([addend] if addend is not None else [])
    extra_specs = ([pl.BlockSpec((8, 128), lambda i, j, k: (0, 0))] if after is not None else []) + (
        [pl.BlockSpec((tm, tn), lambda i, j, k: (i, j))] if addend is not None else [])

    def body(a_ref, b_ref, *rest):
        o_ref, *scratch = rest[len(extras):]
        p = _bdot(a_ref[...], b_ref[...], dims)
        finish = (lambda v: v + rest[len(extras) - 1][...]) if addend is not None else (lambda v: v)
        if nk == 1:
            o_ref[...] = finish(p).astype(o_ref.dtype)
        else:
            acc = scratch[0]
            k = pl.program_id(2)

            @pl.when(k == 0)
            def _():
                acc[...] = p

            @pl.when(k > 0)
            def _():
                acc[...] += p

            @pl.when(k == nk - 1)
            def _():
                o_ref[...] = finish(acc[...]).astype(o_ref.dtype)

    return pl.pallas_call(
        body, name=name,
        out_shape=jax.ShapeDtypeStruct((m_dim, n_dim), out_dtype),
        grid=(m_dim // tm, n_dim // tn, nk),
        in_specs=[a_spec, b_spec] + extra_specs,
        out_specs=pl.BlockSpec((tm, tn), lambda i, j, k: (i, j)),
        scratch_shapes=[pltpu.VMEM((tm, tn), F32)] if nk > 1 else [],
        compiler_params=_cparams("parallel", "parallel", "arbitrary"),
    )(a, b, *extras)


def _rms_fwd(x, w):
    l, d = x.shape

    def body(x_ref, w_ref, h_ref):
        h_ref[...] = _rmsnorm(x_ref[...], w_ref[...]).astype(BF16)

    return pl.pallas_call(
        body, name="rms_fwd",
        out_shape=jax.ShapeDtypeStruct((l, d), BF16),
        grid=(l // ROW_TILE,),
        in_specs=[pl.BlockSpec((ROW_TILE, d), lambda i: (i, 0)),
                  pl.BlockSpec((1, d), lambda i: (0, 0))],
        out_specs=pl.BlockSpec((ROW_TILE, d), lambda i: (i, 0)),
        compiler_params=_cparams("parallel"),
    )(x, w)


def _rms_bwd(x, w, dh, dres):
    l, d = x.shape

    def body(x_ref, w_ref, dh_ref, dres_ref, dx_ref, dw_ref):
        _, vjp = jax.vjp(_rmsnorm, x_ref[...], w_ref[...])
        dx, dw = vjp(dh_ref[...])
        dx_ref[...] = dx + dres_ref[...]

        @pl.when(pl.program_id(0) == 0)
        def _():
            dw_ref[...] = jnp.zeros_like(dw_ref)

        dw_ref[...] += dw

    row = pl.BlockSpec((ROW_TILE, d), lambda i: (i, 0))
    vec = pl.BlockSpec((1, d), lambda i: (0, 0))
    return pl.pallas_call(
        body, name="rms_bwd",
        out_shape=(jax.ShapeDtypeStruct((l, d), F32), jax.ShapeDtypeStruct((1, d), F32)),
        grid=(l // ROW_TILE,),
        in_specs=[row, vec, row, row],
        out_specs=(row, vec),
        compiler_params=_cparams("arbitrary"),
    )(x, w, dh, dres)


def _final(x, r, fw, target):
    l, d = x.shape

    def per_row_loss(x2, w, tgt):
        err = _rmsnorm(x2, w) - tgt
        return 0.5 * jnp.mean(err * err, axis=-1, keepdims=True)

    def body(x_ref, r_ref, w_ref, t_ref, dx_ref, dxb_ref, loss_ref, dw_ref):
        x2 = x_ref[...] + r_ref[...]
        rows, vjp = jax.vjp(functools.partial(per_row_loss, tgt=t_ref[...]), x2, w_ref[...])
        dx2, dw = vjp(jnp.ones_like(rows))
        dx_ref[...] = dx2
        dxb_ref[...] = dx2.astype(BF16)

        @pl.when(pl.program_id(0) == 0)
        def _():
            dw_ref[...] = jnp.zeros_like(dw_ref)
            loss_ref[...] = jnp.zeros_like(loss_ref)

        dw_ref[...] += dw
        loss_ref[...] += jnp.sum(rows, axis=0, keepdims=True)

    row = pl.BlockSpec((ROW_TILE, d), lambda i: (i, 0))
    vec = pl.BlockSpec((1, d), lambda i: (0, 0))
    return pl.pallas_call(
        body, name="final_norm_loss",
        out_shape=(jax.ShapeDtypeStruct((l, d), F32), jax.ShapeDtypeStruct((l, d), BF16),
                   jax.ShapeDtypeStruct((1, 1), F32), jax.ShapeDtypeStruct((1, d), F32)),
        grid=(l // ROW_TILE,),
        in_specs=[row, row, vec, row],
        out_specs=(row, row, pl.BlockSpec((1, 1), lambda i: (0, 0)), vec),
        compiler_params=_cparams("arbitrary"),
    )(x, r, fw, target)


def _merge_fn(gs, gd, ys, yd):
    return _sigmoid(gs) * ys + _sigmoid(gd) * yd


def _merge_fwd(proj, off_gs, off_gd, ys, yd):
    l, d = ys.shape
    cw = 512
    blk = lambda off: pl.BlockSpec((ROW_TILE, cw), lambda i, j: (i, off // cw + j))

    def body(gs_ref, gd_ref, ys_ref, yd_ref, o_ref):
        o_ref[...] = _merge_fn(gs_ref[...], gd_ref[...], ys_ref[...], yd_ref[...]).astype(BF16)

    return pl.pallas_call(
        body, name="merge_fwd",
        out_shape=jax.ShapeDtypeStruct((l, d), BF16),
        grid=(l // ROW_TILE, d // cw),
        in_specs=[blk(off_gs), blk(off_gd), blk(0), blk(0)],
        out_specs=blk(0),
        compiler_params=_cparams("parallel", "parallel"),
    )(proj, proj, ys, yd)


def _merge_bwd(proj, off_gs, off_gd, ys, yd, dmixed):
    l, d = ys.shape
    cw = 512
    blk = lambda off: pl.BlockSpec((ROW_TILE, cw), lambda i, j: (i, off // cw + j))

    def body(gs_ref, gd_ref, ys_ref, yd_ref, dm_ref, dgs_ref, dgd_ref, dys_ref, dyd_ref):
        _, vjp = jax.vjp(_merge_fn, gs_ref[...], gd_ref[...], ys_ref[...], yd_ref[...])
        dgs, dgd, dys, dyd = vjp(dm_ref[...])
        dgs_ref[...] = dgs.astype(BF16)
        dgd_ref[...] = dgd.astype(BF16)
        dys_ref[...] = dys.astype(BF16)
        dyd_ref[...] = dyd.astype(BF16)

    out = jax.ShapeDtypeStruct((l, d), BF16)
    return pl.pallas_call(
        body, name="merge_bwd",
        out_shape=(out, out, out, out),
        grid=(l // ROW_TILE, d // cw),
        in_specs=[blk(off_gs), blk(off_gd), blk(0), blk(0), blk(0)],
        out_specs=(blk(0), blk(0), blk(0), blk(0)),
        compiler_params=_cparams("parallel", "parallel"),
    )(proj, proj, ys, yd, dmixed)


def _s5_disc_fn(lam_re, lam_im, log_step):
    step = jnp.exp(log_step)
    mag = jnp.exp(lam_re * step)
    abar_re = mag * jnp.cos(lam_im * step)
    abar_im = mag * jnp.sin(lam_im * step)
    den = lam_re * lam_re + lam_im * lam_im
    xr = abar_re - 1.0
    f_re = (xr * lam_re + abar_im * lam_im) / den
    f_im = (abar_im * lam_re - xr * lam_im) / den
    return abar_re, abar_im, f_re, f_im


def _s5_disc_fwd(lam_re, lam_im, log_step):
    g, p = lam_re.shape

    def body(lr_ref, li_ref, ls_ref, ar_ref, ai_ref, fr_ref, fi_ref):
        ar, ai, fr, fi = _s5_disc_fn(lr_ref[...], li_ref[...], ls_ref[...])
        ar_ref[...] = ar
        ai_ref[...] = ai
        fr_ref[...] = fr
        fi_ref[...] = fi

    o = jax.ShapeDtypeStruct((g, p), F32)
    return pl.pallas_call(body, name="s5_disc_fwd", out_shape=(o, o, o, o),
                          compiler_params=_cparams())(lam_re, lam_im, log_step)


def _s5_disc_bwd(lam_re, lam_im, log_step, dar, dai, dfr, dfi):
    g, p = lam_re.shape

    def body(lr_ref, li_ref, ls_ref, dar_ref, dai_ref, dfr_ref, dfi_ref, dlr_ref, dli_ref, dls_ref):
        _, vjp = jax.vjp(_s5_disc_fn, lr_ref[...], li_ref[...], ls_ref[...])
        dlr, dli, dls = vjp((dar_ref[...], dai_ref[...], dfr_ref[...], dfi_ref[...]))
        dlr_ref[...] = dlr
        dli_ref[...] = dli
        dls_ref[...] = dls

    o = jax.ShapeDtypeStruct((g, p), F32)
    return pl.pallas_call(body, name="s5_disc_bwd",
                          out_shape=(o, o, jax.ShapeDtypeStruct((g, 1), F32)),
                          compiler_params=_cparams())(lam_re, lam_im, log_step, dar, dai, dfr, dfi)


def _s5_bbar_fwd(f_re, f_im, b_re, b_im):
    n, c = b_re.shape

    def body(fr_ref, fi_ref, br_ref, bi_ref, or_ref, oi_ref):
        fr, fi, br, bi = fr_ref[...], fi_ref[...], br_ref[...], bi_ref[...]
        or_ref[...] = fr * br - fi * bi
        oi_ref[...] = fr * bi + fi * br

    o = jax.ShapeDtypeStruct((n, c), F32)
    return pl.pallas_call(body, name="s5_bbar_fwd", out_shape=(o, o),
                          compiler_params=_cparams())(f_re, f_im, b_re, b_im)


def _s5_bbar_bwd(f_re, f_im, b_re, b_im, dbr, dbi):
    n, c = b_re.shape

    def body(fr_ref, fi_ref, br_ref, bi_ref, dor_ref, doi_ref, dfr_ref, dfi_ref, dbr_ref, dbi_ref):
        fr, fi, br, bi = fr_ref[...], fi_ref[...], br_ref[...], bi_ref[...]
        dor, doi = dor_ref[...], doi_ref[...]
        dfr_ref[...] = jnp.sum(dor * br + doi * bi, axis=-1, keepdims=True)
        dfi_ref[...] = jnp.sum(doi * br - dor * bi, axis=-1, keepdims=True)
        dbr_ref[...] = fr * dor + fi * doi
        dbi_ref[...] = fr * doi - fi * dor

    col = jax.ShapeDtypeStruct((n, 1), F32)
    o = jax.ShapeDtypeStruct((n, c), F32)
    return pl.pallas_call(body, name="s5_bbar_bwd", out_shape=(col, col, o, o),
                          compiler_params=_cparams())(f_re, f_im, b_re, b_im, dbr, dbi)


SUBLANES = 8


def _scan_groups(xr, xi, ar, ai, reverse):
    t = xr.shape[0]
    sub = lax.broadcasted_iota(jnp.int32, (t, 1), 0) & (SUBLANES - 1)
    pr, pi = ar, ai
    for sh in (1, 2, 4):
        if reverse:
            keep = sub < SUBLANES - sh
            sr, si = pltpu.roll(xr, t - sh, 0), pltpu.roll(xi, t - sh, 0)
        else:
            keep = sub >= sh
            sr, si = pltpu.roll(xr, sh, 0), pltpu.roll(xi, sh, 0)
        sr = jnp.where(keep, sr, 0.0)
        si = jnp.where(keep, si, 0.0)
        xr, xi = xr + pr * sr - pi * si, xi + pr * si + pi * sr
        pr, pi = pr * pr - pi * pi, 2.0 * pr * pi
    return xr, xi


def _scan_rows(xr, xi, ar, ai, cr, ci, sr_ref, si_ref, reverse):
    t, n = xr.shape
    xr, xi = _scan_groups(xr, xi, ar, ai, reverse)
    sr_ref[...] = xr
    si_ref[...] = xi
    sub = lax.broadcasted_iota(jnp.int32, (SUBLANES, n), 0)
    seed = sub == (SUBLANES - 1 if reverse else 0)
    pwr, pwi = _scan_groups(jnp.where(seed, ar, 0.0), jnp.where(seed, ai, 0.0), ar, ai, reverse)
    groups = range(t // SUBLANES)
    edge = 0 if reverse else SUBLANES - 1
    for g in (reversed(groups) if reverse else groups):
        rows = slice(g * SUBLANES, (g + 1) * SUBLANES)
        vr = sr_ref[rows, :] + (pwr * cr - pwi * ci)
        vi = si_ref[rows, :] + (pwr * ci + pwi * cr)
        sr_ref[rows, :] = vr
        si_ref[rows, :] = vi
        cr, ci = vr[edge:edge + 1, :], vi[edge:edge + 1, :]
    return cr, ci


def _s5_states(u_bf, bbr, bbi, ar, ai, cr, ci, sr_ref, si_ref):
    return _scan_rows(_dot(u_bf, bbr), _dot(u_bf, bbi), ar, ai, cr, ci, sr_ref, si_ref, reverse=False)


def _s5_fwd(proj, bbr, bbi, a_re, a_im, ctr, cti, d_skip, d_s5):
    l = proj.shape[0]
    nb, uc, ns = bbr.shape
    t = min(S5_T, l)
    nt = l // t

    def body(u_ref, bbr_ref, bbi_ref, ar_ref, ai_ref, ctr_ref, cti_ref, d_ref,
             y_ref, car_r_ref, car_i_ref, sr_ref, si_ref, cr, ci):
        @pl.when(pl.program_id(1) == 0)
        def _():
            cr[...] = jnp.zeros_like(cr)
            ci[...] = jnp.zeros_like(ci)

        car_r_ref[...] = cr[...]
        car_i_ref[...] = ci[...]
        u = u_ref[...]
        cr[...], ci[...] = _s5_states(u.astype(BF16), bbr_ref[...], bbi_ref[...], ar_ref[...],
                                      ai_ref[...], cr[...], ci[...], sr_ref, si_ref)
        y_ref[...] = (_bdot(sr_ref[...], ctr_ref[...]) - _bdot(si_ref[...], cti_ref[...])
                      + d_ref[...] * u)

    per_block = lambda shape: pl.BlockSpec((None,) + shape, lambda b, n: (b, 0, 0))
    return pl.pallas_call(
        body, name="s5_fwd",
        out_shape=(jax.ShapeDtypeStruct((l, d_s5), F32),
                   jax.ShapeDtypeStruct((nt, 1, nb * ns), F32),
                   jax.ShapeDtypeStruct((nt, 1, nb * ns), F32),
                   jax.ShapeDtypeStruct((l, nb * ns), F32),
                   jax.ShapeDtypeStruct((l, nb * ns), F32)),
        grid=(nb, nt),
        in_specs=[pl.BlockSpec((t, uc), lambda b, n: (n, b)),
                  per_block((uc, ns)), per_block((uc, ns)),
                  per_block((1, ns)), per_block((1, ns)),
                  per_block((ns, uc)), per_block((ns, uc)),
                  pl.BlockSpec((1, uc), lambda b, n: (0, b))],
        out_specs=(pl.BlockSpec((t, uc), lambda b, n: (n, b)),
                   pl.BlockSpec((None, 1, ns), lambda b, n: (n, 0, b)),
                   pl.BlockSpec((None, 1, ns), lambda b, n: (n, 0, b)),
                   pl.BlockSpec((t, ns), lambda b, n: (n, b)),
                   pl.BlockSpec((t, ns), lambda b, n: (n, b))),
        scratch_shapes=[pltpu.VMEM((1, ns), F32), pltpu.VMEM((1, ns), F32)],
        compiler_params=_cparams("parallel", "arbitrary"),
    )(proj, bbr, bbi, a_re, a_im, ctr, cti, d_skip)


def _s5_bwd(proj, dy, bbr, bbi, a_re, a_im, cbr, cbi, d_skip, car_r, car_i, states_r, states_i):
    l, d_s5 = dy.shape
    nb, uc, ns = bbr.shape
    t = min(S5_T, l)
    nt = l // t

    def body(u_ref, dy_ref, bbr_ref, bbi_ref, ar_ref, ai_ref, cbr_ref, cbi_ref, d_ref,
             car_r_ref, car_i_ref, sr_ref, si_ref,
             du_ref, dar_ref, dai_ref, dbbr_ref, dbbi_ref, dcbr_ref, dcbi_ref, dd_ref, gcr, gci,
             gr_ref, gi_ref):
        @pl.when(pl.program_id(1) == 0)
        def _():
            gcr[...] = jnp.zeros_like(gcr)
            gci[...] = jnp.zeros_like(gci)
            for ref in (dar_ref, dai_ref, dbbr_ref, dbbi_ref, dcbr_ref, dcbi_ref, dd_ref):
                ref[...] = jnp.zeros_like(ref)

        row = lax.broadcasted_iota(jnp.int32, (t, 1), 0)
        u, dy = u_ref[...], dy_ref[...]
        u_bf, dy_bf = u.astype(BF16), dy.astype(BF16)
        ar, ai = ar_ref[...], ai_ref[...]
        cr, ci = car_r_ref[...], car_i_ref[...]
        sr, si = sr_ref[...], si_ref[...]
        first = row == 0
        pr = jnp.where(first, cr, pltpu.roll(sr, 1, 0))
        pi = jnp.where(first, ci, pltpu.roll(si, 1, 0))
        gcr[...], gci[...] = _scan_rows(_dot(dy_bf, cbr_ref[...]), -_dot(dy_bf, cbi_ref[...]), ar, -ai,
                                        gcr[...], gci[...], gr_ref, gi_ref, reverse=True)
        gr, gi = gr_ref[...], gi_ref[...]
        dar_ref[...] += jnp.sum(gr * pr + gi * pi, axis=0, keepdims=True)
        dai_ref[...] += jnp.sum(gi * pr - gr * pi, axis=0, keepdims=True)
        gr_bf, gi_bf = gr.astype(BF16), gi.astype(BF16)
        tn = ((0,), (0,))
        dbbr_ref[...] += _dot(u_bf, gr_bf, tn)
        dbbi_ref[...] += _dot(u_bf, gi_bf, tn)
        dcbr_ref[...] += _dot(dy_bf, sr.astype(BF16), tn)
        dcbi_ref[...] -= _dot(dy_bf, si.astype(BF16), tn)
        nt_dims = ((1,), (1,))
        du = _dot(gr_bf, bbr_ref[...], nt_dims) + _dot(gi_bf, bbi_ref[...], nt_dims) + dy * d_ref[...]
        du_ref[...] = du.astype(BF16)
        dd_ref[...] += jnp.sum(dy * u, axis=0, keepdims=True)

    rev = lambda n: nt - 1 - n
    per_block = lambda shape: pl.BlockSpec((None,) + shape, lambda b, n: (b, 0, 0))
    acc = jax.ShapeDtypeStruct((nb, uc, ns), F32)
    vec = jax.ShapeDtypeStruct((nb, 1, ns), F32)
    return pl.pallas_call(
        body, name="s5_bwd",
        out_shape=(jax.ShapeDtypeStruct((l, d_s5), BF16), vec, vec, acc, acc, acc, acc,
                   jax.ShapeDtypeStruct((1, d_s5), F32)),
        grid=(nb, nt),
        in_specs=[pl.BlockSpec((t, uc), lambda b, n: (rev(n), b)),
                  pl.BlockSpec((t, uc), lambda b, n: (rev(n), b)),
                  per_block((uc, ns)), per_block((uc, ns)),
                  per_block((1, ns)), per_block((1, ns)),
                  per_block((uc, ns)), per_block((uc, ns)),
                  pl.BlockSpec((1, uc), lambda b, n: (0, b)),
                  pl.BlockSpec((None, 1, ns), lambda b, n: (rev(n), 0, b)),
                  pl.BlockSpec((None, 1, ns), lambda b, n: (rev(n), 0, b)),
                  pl.BlockSpec((t, ns), lambda b, n: (rev(n), b)),
                  pl.BlockSpec((t, ns), lambda b, n: (rev(n), b))],
        out_specs=(pl.BlockSpec((t, uc), lambda b, n: (rev(n), b)),
                   per_block((1, ns)), per_block((1, ns)),
                   per_block((uc, ns)), per_block((uc, ns)),
                   per_block((uc, ns)), per_block((uc, ns)),
                   pl.BlockSpec((1, uc), lambda b, n: (0, b))),
        scratch_shapes=[pltpu.VMEM((1, ns), F32), pltpu.VMEM((1, ns), F32)]
        + [pltpu.VMEM((t, ns), F32)] * 2,
        compiler_params=_cparams("parallel", "arbitrary"),
    )(proj, dy, bbr, bbi, a_re, a_im, cbr, cbi, d_skip, car_r, car_i, states_r, states_i)


def _s5_glu_fwd(y1, proj, off_z, wglu):
    l, d = y1.shape

    def body(y_ref, z_ref, w_ref, o_ref):
        y2 = _gelu(y_ref[...])
        y3 = y2 * _sigmoid(_bdot(y2, w_ref[...]))
        o_ref[...] = (y3 * _silu(z_ref[...])).astype(BF16)

    return pl.pallas_call(
        body, name="s5_glu_fwd",
        out_shape=jax.ShapeDtypeStruct((l, d), BF16),
        grid=(l // ROW_TILE,),
        in_specs=[pl.BlockSpec((ROW_TILE, d), lambda i: (i, 0)),
                  pl.BlockSpec((ROW_TILE, d), lambda i: (i, off_z // d)),
                  pl.BlockSpec((d, d), lambda i: (0, 0))],
        out_specs=pl.BlockSpec((ROW_TILE, d), lambda i: (i, 0)),
        compiler_params=_cparams("parallel"),
    )(y1, proj, wglu)


def _s5_glu_bwd(y1, proj, off_z, wglu, dout):
    l, d = y1.shape

    def body(y_ref, z_ref, w_ref, do_ref, dy_ref, dz_ref, dw_ref):
        y2, gelu_vjp = jax.vjp(_gelu, y_ref[...])
        z = z_ref[...]
        sz, silu_vjp = jax.vjp(_silu, z)
        y2_bf = y2.astype(BF16)
        sg = _sigmoid(_dot(y2_bf, w_ref[...]))
        dout = do_ref[...]
        dy3 = dout * sz
        dz_ref[...] = silu_vjp(dout * (y2 * sg))[0].astype(BF16)
        dgl = (dy3 * y2 * sg * (1.0 - sg)).astype(BF16)
        dy2 = dy3 * sg + _dot(dgl, w_ref[...], ((1,), (1,)))
        dy_ref[...] = gelu_vjp(dy2)[0]

        @pl.when(pl.program_id(0) == 0)
        def _():
            dw_ref[...] = jnp.zeros_like(dw_ref)

        dw_ref[...] += _dot(y2_bf, dgl, ((0,), (0,)))

    row = pl.BlockSpec((ROW_TILE, d), lambda i: (i, 0))
    full = pl.BlockSpec((d, d), lambda i: (0, 0))
    return pl.pallas_call(
        body, name="s5_glu_bwd",
        out_shape=(jax.ShapeDtypeStruct((l, d), F32), jax.ShapeDtypeStruct((l, d), BF16),
                   jax.ShapeDtypeStruct((d, d), F32)),
        grid=(l // ROW_TILE,),
        in_specs=[row, pl.BlockSpec((ROW_TILE, d), lambda i: (i, off_z // d)), full, row],
        out_specs=(row, row, full),
        compiler_params=_cparams("arbitrary"),
    )(y1, proj, wglu, dout)


def _shift_rows(x, k, back=False):
    if k == 0:
        return x
    t = x.shape[0]
    row = lax.broadcasted_iota(jnp.int32, (t, 1), 0)
    if back:
        return jnp.where(row < t - k, pltpu.roll(x, t - k, 0), 0.0)
    return jnp.where(row >= k, pltpu.roll(x, k, 0), 0.0)


def _dn_conv(x, w_ref):
    return sum(w_ref[CONV_K - 1 - k:CONV_K - k, :] * _shift_rows(x, k) for k in range(CONV_K))


def _dn_post_conv(c, j):
    y = _silu(c)
    n = y * lax.rsqrt(jnp.sum(y * y, axis=-1, keepdims=True) + EPS)
    n = n * jnp.where(j < DN_HEADS, DN_HEAD_DIM ** -0.5, 1.0)
    return jnp.where(j < 2 * DN_HEADS, n, y)


def _dn_prep_fwd(proj, off_qkv, conv_w):
    l = proj.shape[0]
    hd = DN_HEAD_DIM
    nblk = 3 * DN_HEADS

    def body(x_ref, w_ref, o_ref):
        o_ref[...] = _dn_post_conv(_dn_conv(x_ref[...], w_ref), pl.program_id(0))

    return pl.pallas_call(
        body, name="dn_prep_fwd",
        out_shape=jax.ShapeDtypeStruct((l, nblk * hd), F32),
        grid=(nblk,),
        in_specs=[pl.BlockSpec((l, hd), lambda j: (0, off_qkv // hd + j)),
                  pl.BlockSpec((CONV_K, hd), lambda j: (0, j))],
        out_specs=pl.BlockSpec((l, hd), lambda j: (0, j)),
        compiler_params=_cparams("parallel"),
    )(proj, conv_w)


def _dn_prep_bwd(proj, off_qkv, conv_w, dqkv):
    l = proj.shape[0]
    hd = DN_HEAD_DIM
    nblk = 3 * DN_HEADS

    def body(x_ref, w_ref, do_ref, dx_ref, dw_ref):
        x = x_ref[...]
        j = pl.program_id(0)
        _, vjp = jax.vjp(functools.partial(_dn_post_conv, j=j), _dn_conv(x, w_ref))
        dc = vjp(do_ref[...])[0]
        dx = sum(w_ref[CONV_K - 1 - k:CONV_K - k, :] * _shift_rows(dc, k, back=True)
                 for k in range(CONV_K))
        dx_ref[...] = dx.astype(BF16)
        for k in range(CONV_K):
            dw_ref[CONV_K - 1 - k:CONV_K - k, :] = jnp.sum(dc * _shift_rows(x, k), axis=0,
                                                           keepdims=True)

    return pl.pallas_call(
        body, name="dn_prep_bwd",
        out_shape=(jax.ShapeDtypeStruct((l, nblk * hd), BF16),
                   jax.ShapeDtypeStruct((CONV_K, nblk * hd), F32)),
        grid=(nblk,),
        in_specs=[pl.BlockSpec((l, hd), lambda j: (0, off_qkv // hd + j)),
                  pl.BlockSpec((CONV_K, hd), lambda j: (0, j)),
                  pl.BlockSpec((None, l, hd), lambda j: (j // DN_HEADS, 0, j % DN_HEADS))],
        out_specs=(pl.BlockSpec((l, hd), lambda j: (0, j)),
                   pl.BlockSpec((CONV_K, hd), lambda j: (0, j))),
        compiler_params=_cparams("parallel"),
    )(proj, conv_w, dqkv)


def _dn_gate_fn(ba, a_log_row, dt_row):
    lane = lax.broadcasted_iota(jnp.int32, ba.shape, 1)
    beta = _sigmoid(ba)
    g = -jnp.exp(a_log_row) * _softplus(ba + dt_row)
    return jnp.where(lane < DN_HEADS, beta, jnp.where(lane < 2 * DN_HEADS, g, 0.0))


def _dn_gates_fwd(proj, off_ba, a_log_row, dt_row):
    l = proj.shape[0]
    row = pl.BlockSpec((ROW_TILE, 128), lambda i: (i, off_ba // 128))
    vec = pl.BlockSpec((1, 128), lambda i: (0, 0))

    def body(ba_ref, al_ref, dt_ref, o_ref):
        o_ref[...] = _dn_gate_fn(ba_ref[...], al_ref[...], dt_ref[...])

    return pl.pallas_call(
        body, name="dn_gates_fwd",
        out_shape=jax.ShapeDtypeStruct((l, 128), F32),
        grid=(l // ROW_TILE,),
        in_specs=[row, vec, vec],
        out_specs=pl.BlockSpec((ROW_TILE, 128), lambda i: (i, 0)),
        compiler_params=_cparams("parallel"),
    )(proj, a_log_row, dt_row)


def _dn_gates_bwd(proj, off_ba, a_log_row, dt_row, dgb_heads):
    l = proj.shape[0]
    nh = dgb_heads.shape[0]
    row = pl.BlockSpec((ROW_TILE, 128), lambda i: (i, off_ba // 128))
    vec = pl.BlockSpec((1, 128), lambda i: (0, 0))

    def body(ba_ref, al_ref, dt_ref, dg_ref, dba_ref, dal_ref, ddt_ref):
        _, vjp = jax.vjp(_dn_gate_fn, ba_ref[...], al_ref[...], dt_ref[...])
        dgb = dg_ref[0]
        for h in range(1, nh):
            dgb = dgb + dg_ref[h]
        dba, dal, ddt = vjp(dgb)
        dba_ref[...] = dba.astype(BF16)

        @pl.when(pl.program_id(0) == 0)
        def _():
            dal_ref[...] = jnp.zeros_like(dal_ref)
            ddt_ref[...] = jnp.zeros_like(ddt_ref)

        dal_ref[...] += dal
        ddt_ref[...] += ddt

    return pl.pallas_call(
        body, name="dn_gates_bwd",
        out_shape=(jax.ShapeDtypeStruct((l, 128), BF16), jax.ShapeDtypeStruct((1, 128), F32),
                   jax.ShapeDtypeStruct((1, 128), F32)),
        grid=(l // ROW_TILE,),
        in_specs=[row, vec, vec, pl.BlockSpec((nh, ROW_TILE, 128), lambda i: (0, i, 0))],
        out_specs=(pl.BlockSpec((ROW_TILE, 128), lambda i: (i, 0)), vec, vec),
        compiler_params=_cparams("arbitrary"),
    )(proj, a_log_row, dt_row, dgb_heads)


def _dn_chunk_fn(states, qs, ks, vs, gb, heads):
    c = qs[0].shape[0]
    each = lambda f, *lists: [f(*args) for args in zip(*lists)]
    lane = lax.broadcasted_iota(jnp.int32, gb.shape, 1)
    ri = lax.broadcasted_iota(jnp.int32, (c, c), 0)
    ci = lax.broadcasted_iota(jnp.int32, (c, c), 1)
    causal, strict = ri >= ci, ri > ci
    eye = (ri == ci).astype(F32)
    rowi = lax.broadcasted_iota(jnp.int32, (c, 1), 0)
    nt_dims = ((1,), (1,))
    hdot = functools.partial(_dot, precision=HIGHEST)

    pick = lambda m, at: jnp.sum(jnp.where(lane == at, m, 0.0), axis=1, keepdims=True)
    gb_cum = hdot(causal.astype(F32), gb)
    beta = [pick(gb, h) for h in heads]
    gc = [pick(gb_cum, h + DN_HEADS) for h in heads]
    gc_row = each(lambda g: jnp.sum(eye * g, axis=0, keepdims=True), gc)
    decay = each(lambda g, gr: jnp.where(causal, jnp.exp(jnp.where(causal, g - gr, 0.0)), 0.0),
                 gc, gc_row)
    kk = each(lambda k: _bdot(k, k, nt_dims), ks)
    a_mat = each(lambda b, m, dc: jnp.where(strict, b * m * dc, 0.0), beta, kk, decay)

    t_inv = each(lambda a: eye - a, a_mat)
    power = a_mat
    for _ in range(int(math.log2(c)) - 1):
        power = each(lambda p: _bdot(p, p), power)
        t_inv = each(lambda t, p: t + _bdot(t, p), t_inv, power)

    egc = each(jnp.exp, gc)
    u_c = each(lambda t, v, b: _dot3(t, v * b), t_inv, vs, beta)
    w_c = each(lambda t, k, b, e: _dot3(t, k * (b * e)), t_inv, ks, beta, egc)
    qk = each(lambda q, k, dc: _bdot(q, k, nt_dims) * dc, qs, ks, decay)
    g_end = each(lambda g: jnp.sum(jnp.where(rowi == c - 1, g, 0.0), axis=0, keepdims=True), gc)
    v_new = each(lambda u, w, s: u - _bdot(w, s), u_c, w_c, states)
    o = each(lambda q, e, s, m, vn: _bdot(q * e, s) + _bdot(m, vn), qs, egc, states, qk, v_new)
    new_states = each(
        lambda s, ge, k, g, vn: s * jnp.exp(ge) + _bdot(k * jnp.exp(ge - g), vn, ((0,), (0,))),
        states, g_end, ks, gc, v_new)
    return o, new_states


def _dn_chunk_specs(order):
    hd, nh, hps = DN_HEAD_DIM, DN_HEADS, DN_HEADS_PER_STEP
    qkv = lambda part: pl.BlockSpec((CHUNK, hps * hd), lambda h, n: (order(n), part * (nh // hps) + h))
    gb = pl.BlockSpec((CHUNK, 128), lambda h, n: (order(n), 0))
    state = pl.BlockSpec((hps, None, hd, hd), lambda h, n: (h, order(n), 0, 0))
    return qkv, gb, state


def _dn_chunk_fwd(qkv, gb):
    l = qkv.shape[0]
    hd, nh, hps = DN_HEAD_DIM, DN_HEADS, DN_HEADS_PER_STEP
    n_chunks = l // CHUNK
    qkv_spec, gb_spec, state_spec = _dn_chunk_specs(lambda n: n)

    def body(q_ref, k_ref, v_ref, gb_ref, o_ref, s_ref, state):
        @pl.when(pl.program_id(1) == 0)
        def _():
            state[...] = jnp.zeros_like(state)

        cols = [slice(i * hd, (i + 1) * hd) for i in range(hps)]
        states = [state[i] for i in range(hps)]
        for i in range(hps):
            s_ref[i] = states[i]
        o, new_states = _dn_chunk_fn(
            states, [q_ref[:, cs] for cs in cols], [k_ref[:, cs] for cs in cols],
            [v_ref[:, cs] for cs in cols], gb_ref[...],
            [pl.program_id(0) * hps + i for i in range(hps)])
        for i in range(hps):
            o_ref[:, cols[i]] = o[i]
            state[i] = new_states[i]

    return pl.pallas_call(
        body, name="dn_chunk_fwd",
        out_shape=(jax.ShapeDtypeStruct((l, nh * hd), F32),
                   jax.ShapeDtypeStruct((nh, n_chunks, hd, hd), F32)),
        grid=(nh // hps, n_chunks),
        in_specs=[qkv_spec(0), qkv_spec(1), qkv_spec(2), gb_spec],
        out_specs=(pl.BlockSpec((CHUNK, hps * hd), lambda h, n: (n, h)), state_spec),
        scratch_shapes=[pltpu.VMEM((hps, hd, hd), F32)],
        compiler_params=_cparams("parallel", "arbitrary"),
    )(qkv, qkv, qkv, gb)


def _dn_chunk_bwd(qkv, gb, states, do):
    l = qkv.shape[0]
    hd, nh, hps = DN_HEAD_DIM, DN_HEADS, DN_HEADS_PER_STEP
    n_chunks = l // CHUNK
    rev = lambda n: n_chunks - 1 - n
    qkv_spec, gb_spec, state_spec = _dn_chunk_specs(rev)

    def body(q_ref, k_ref, v_ref, gb_ref, s_ref, do_ref, dqkv_ref, dgb_ref, dstate):
        @pl.when(pl.program_id(1) == 0)
        def _():
            dstate[...] = jnp.zeros_like(dstate)

        cols = [slice(i * hd, (i + 1) * hd) for i in range(hps)]
        fn = functools.partial(_dn_chunk_fn, heads=[pl.program_id(0) * hps + i for i in range(hps)])
        _, vjp = jax.vjp(fn, [s_ref[i] for i in range(hps)], [q_ref[:, cs] for cs in cols],
                         [k_ref[:, cs] for cs in cols], [v_ref[:, cs] for cs in cols], gb_ref[...])
        ds, dq, dk, dv, dgb = vjp(([do_ref[:, cs] for cs in cols], [dstate[i] for i in range(hps)]))
        for i in range(hps):
            dstate[i] = ds[i]
            dqkv_ref[0, :, cols[i]] = dq[i]
            dqkv_ref[1, :, cols[i]] = dk[i]
            dqkv_ref[2, :, cols[i]] = dv[i]
        dgb_ref[...] = dgb

    head_out = pl.BlockSpec((CHUNK, hps * hd), lambda h, n: (rev(n), h))
    return pl.pallas_call(
        body, name="dn_chunk_bwd",
        out_shape=(jax.ShapeDtypeStruct((3, l, nh * hd), F32),
                   jax.ShapeDtypeStruct((nh // hps, l, 128), F32)),
        grid=(nh // hps, n_chunks),
        in_specs=[qkv_spec(0), qkv_spec(1), qkv_spec(2), gb_spec, state_spec, head_out],
        out_specs=(pl.BlockSpec((3, CHUNK, hps * hd), lambda h, n: (0, rev(n), h)),
                   pl.BlockSpec((None, CHUNK, 128), lambda h, n: (h, rev(n), 0))),
        scratch_shapes=[pltpu.VMEM((hps, hd, hd), F32)],
        compiler_params=_cparams("parallel", "arbitrary"),
    )(qkv, qkv, qkv, gb, states, do)


def _dn_out_fn(o, z, w):
    return _rmsnorm(o, w) * _silu(z)


def _dn_out_fwd(o, proj, off_z, w):
    l, d = o.shape
    hd = DN_HEAD_DIM
    blk = lambda off: pl.BlockSpec((ROW_TILE, hd), lambda i, h: (i, off // hd + h))

    def body(o_ref, z_ref, w_ref, out_ref):
        out_ref[...] = _dn_out_fn(o_ref[...], z_ref[...], w_ref[...]).astype(BF16)

    return pl.pallas_call(
        body, name="dn_out_fwd",
        out_shape=jax.ShapeDtypeStruct((l, d), BF16),
        grid=(l // ROW_TILE, d // hd),
        in_specs=[blk(0), blk(off_z), pl.BlockSpec((1, hd), lambda i, h: (0, 0))],
        out_specs=blk(0),
        compiler_params=_cparams("parallel", "parallel"),
    )(o, proj, w)


def _dn_out_bwd(o, proj, off_z, w, dout):
    l, d = o.shape
    hd = DN_HEAD_DIM
    blk = lambda off: pl.BlockSpec((ROW_TILE, hd), lambda i, h: (i, off // hd + h))
    vec = pl.BlockSpec((1, hd), lambda i, h: (0, 0))

    def body(o_ref, z_ref, w_ref, dout_ref, do_ref, dz_ref, dw_ref):
        _, vjp = jax.vjp(_dn_out_fn, o_ref[...], z_ref[...], w_ref[...])
        do, dz, dw = vjp(dout_ref[...])
        do_ref[...] = do
        dz_ref[...] = dz.astype(BF16)

        @pl.when((pl.program_id(0) == 0) & (pl.program_id(1) == 0))
        def _():
            dw_ref[...] = jnp.zeros_like(dw_ref)

        dw_ref[...] += dw

    return pl.pallas_call(
        body, name="dn_out_bwd",
        out_shape=(jax.ShapeDtypeStruct((l, d), F32), jax.ShapeDtypeStruct((l, d), BF16),
                   jax.ShapeDtypeStruct((1, hd), F32)),
        grid=(l // ROW_TILE, d // hd),
        in_specs=[blk(0), blk(off_z), vec, blk(0)],
        out_specs=(blk(0), blk(0), vec),
        compiler_params=_cparams("arbitrary", "arbitrary"),
    )(o, proj, w, dout)


def _tile_2d(rows, cols, budget_bytes=1 << 20):
    for tr in (rows, 4096, 2048, 1024, 512, 256, 128, 64, 32, 16):
        if tr <= rows and rows % tr == 0 and tr * cols * 4 <= budget_bytes:
            return tr, cols
    for tc in (2048, 1024, 512, 256, 128):
        if cols % tc == 0 and rows * tc * 4 <= 2 * budget_bytes:
            return rows, tc
    raise ValueError((rows, cols))


def _adamw_update(g, w_ref, m_ref, v_ref, go_ref, d_ref, mo_ref, vo_ref):
    c1 = 1.0 / (1.0 - ADAM_B1 ** ADAM_STEP)
    c2 = 1.0 / (1.0 - ADAM_B2 ** ADAM_STEP)
    m_new = ADAM_B1 * m_ref[...] + (1.0 - ADAM_B1) * g
    v_new = ADAM_B2 * v_ref[...] + (1.0 - ADAM_B2) * (g * g)
    go_ref[...] = g
    mo_ref[...] = m_new
    vo_ref[...] = v_new
    d_ref[...] = -ADAM_LR * ((m_new * c1) / (jnp.sqrt(v_new * c2) + ADAM_EPS) + ADAM_WD * w_ref[...])


def _adamw(w, m, v, gslots, name):
    rows, cols = w.shape
    ns = gslots.shape[0]
    tr, tc = _tile_2d(rows, cols)

    def body(w_ref, m_ref, v_ref, g_ref, go_ref, d_ref, mo_ref, vo_ref):
        g = g_ref[0].astype(F32)
        for s in range(1, ns):
            g = g + g_ref[s].astype(F32)
        _adamw_update(g, w_ref, m_ref, v_ref, go_ref, d_ref, mo_ref, vo_ref)

    blk = pl.BlockSpec((tr, tc), lambda i, j: (i, j))
    o = jax.ShapeDtypeStruct((rows, cols), F32)
    return pl.pallas_call(
        body, name=name, out_shape=(o, o, o, o),
        grid=(rows // tr, cols // tc),
        in_specs=[blk, blk, blk, pl.BlockSpec((ns, tr, tc), lambda i, j: (0, i, j))],
        out_specs=(blk, blk, blk, blk),
        compiler_params=_cparams("parallel", "parallel"),
    )(w, m, v, gslots)


def _slot_sum(gslots, name):
    ns, rows, cols = gslots.shape
    tr, tc = _tile_2d(rows, cols)

    def body(g_ref, o_ref):
        g = g_ref[0]
        for s in range(1, ns):
            g = g + g_ref[s]
        o_ref[...] = g

    return pl.pallas_call(
        body, name=name, out_shape=jax.ShapeDtypeStruct((rows, cols), F32),
        grid=(rows // tr, cols // tc),
        in_specs=[pl.BlockSpec((ns, tr, tc), lambda i, j: (0, i, j))],
        out_specs=pl.BlockSpec((tr, tc), lambda i, j: (i, j)),
        compiler_params=_cparams("parallel", "parallel"),
    )(gslots)


HBM_SPEC = pl.BlockSpec(memory_space=pl.ANY)


def _all_gather(arrs, name):
    n = len(arrs)

    def body(*refs):
        ins, outs = refs[:n], refs[n:2 * n]
        send_sems, recv_sems, local_sems = refs[2 * n:]
        x, y, c = lax.axis_index("x"), lax.axis_index("y"), lax.axis_index("c")
        me, sibling = (x, y, c), (x, y, 1 - c)
        chips = [(1 - x, y), (x, 1 - y), (1 - x, 1 - y)]
        index = lambda px, py, pc: 4 * px + 2 * py + pc

        def copy(a, k, block, to, src=None):
            rows = outs[a].at[index(*block)]
            return pltpu.make_async_remote_copy(
                src_ref=rows if src is None else src, dst_ref=rows,
                send_sem=send_sems.at[a, k], recv_sem=recv_sems.at[a, k],
                device_id=to, device_id_type=MESH)

        mine = [pltpu.make_async_copy(ins[a], outs[a].at[index(*me)], local_sems.at[a])
                for a in range(n)]
        for cp in mine:
            cp.start()
        first = []
        for a in range(n):
            first.append(copy(a, 0, me, sibling, src=ins[a]))
            first += [copy(a, 1 + j, me, (*chip, c), src=ins[a]) for j, chip in enumerate(chips)]
        for cp in first:
            cp.start()
        passed = []
        for j, chip in enumerate(chips):
            for a in range(n):
                copy(a, 1 + j, (*chip, c), me).wait_recv()
                fwd = copy(a, 4 + j, (*chip, c), sibling)
                fwd.start()
                passed.append(fwd)
        for a in range(n):
            copy(a, 0, sibling, me).wait_recv()
            for j, chip in enumerate(chips):
                copy(a, 4 + j, (*chip, 1 - c), me).wait_recv()
        for cp in first + passed:
            cp.wait_send()
        for cp in mine:
            cp.wait()

    return pl.pallas_call(
        body, name=name,
        out_shape=[jax.ShapeDtypeStruct((N_DEV,) + a.shape, a.dtype) for a in arrs],
        in_specs=[HBM_SPEC] * n, out_specs=[HBM_SPEC] * n,
        scratch_shapes=[pltpu.SemaphoreType.DMA((n, 7)), pltpu.SemaphoreType.DMA((n, 7)),
                        pltpu.SemaphoreType.DMA((n,))],
    )(*arrs)


def _sibling_swap(arrs, name):
    n = len(arrs)

    def body(*refs):
        ins, outs = refs[:n], refs[n:2 * n]
        send_sems, recv_sems = refs[2 * n:]
        x, y, c = lax.axis_index("x"), lax.axis_index("y"), lax.axis_index("c")
        copies = [pltpu.make_async_remote_copy(
            src_ref=ins[a].at[1 - c], dst_ref=outs[a],
            send_sem=send_sems.at[a], recv_sem=recv_sems.at[a],
            device_id=(x, y, 1 - c), device_id_type=MESH) for a in range(n)]
        for cp in copies:
            cp.start()
        for cp in copies:
            cp.wait()

    return pl.pallas_call(
        body, name=name,
        out_shape=[jax.ShapeDtypeStruct(a.shape[1:], a.dtype) for a in arrs],
        in_specs=[HBM_SPEC] * n, out_specs=[HBM_SPEC] * n,
        scratch_shapes=[pltpu.SemaphoreType.DMA((n,)), pltpu.SemaphoreType.DMA((n,))],
    )(*arrs)


def _pair_sum(mine, theirs, core, name):
    _, rows, cols = mine.shape
    tr, tc = _tile_2d(rows, cols, budget_bytes=2 << 20)

    def body(core_ref, a_ref, b_ref, o_ref):
        o_ref[...] = (a_ref[...].astype(F32) + b_ref[...].astype(F32)).astype(o_ref.dtype)

    return pl.pallas_call(
        body, name=name, out_shape=jax.ShapeDtypeStruct((rows, cols), mine.dtype),
        grid_spec=pltpu.PrefetchScalarGridSpec(
            num_scalar_prefetch=1, grid=(rows // tr, cols // tc),
            in_specs=[pl.BlockSpec((None, tr, tc), lambda i, j, core_ref: (core_ref[0], i, j)),
                      pl.BlockSpec((tr, tc), lambda i, j, core_ref: (i, j))],
            out_specs=pl.BlockSpec((tr, tc), lambda i, j, core_ref: (i, j))),
        compiler_params=_cparams("parallel", "parallel"),
    )(core, mine, theirs)


HBM_ONLY = pl.BlockSpec(memory_space=pltpu.HBM)
SEM_SPEC = pl.BlockSpec(memory_space=pltpu.SEMAPHORE)
SPLIT_COPY_EFFECT = pltpu.SideEffectType.DATAFLOW_SIDE_EFFECTING


def _flip(v, bit):
    return 1 - v if bit else v


def _chip_slices_plan(n):
    def plan():
        x, y, c = lax.axis_index("x"), lax.axis_index("y"), lax.axis_index("c")
        copies = []
        for k in range(1, 4):
            px, py = _flip(x, k & 2), _flip(y, k & 1)
            copies += [(a, 2 * px + py, 2 * x + y, (px, py, c)) for a in range(n)]
        return copies
    return plan, 3 * n


def _gather_plan(n):
    def plan():
        x, y, c = lax.axis_index("x"), lax.axis_index("y"), lax.axis_index("c")
        copies = []
        for k in range(1, N_DEV):
            peer = (_flip(x, k & 4), _flip(y, k & 2), _flip(c, k & 1))
            copies += [(a, None, 4 * x + 2 * y + c, peer) for a in range(n)]
        return copies
    return plan, 7 * n


def _planned_copies(plan, srcs, lands, send_sems, recv_sems):
    return [pltpu.make_async_remote_copy(
        src_ref=srcs[a] if src_at is None else srcs[a].at[src_at], dst_ref=lands[a].at[land_at],
        send_sem=send_sems[i], recv_sem=recv_sems[i], device_id=peer, device_id_type=MESH)
        for i, (a, src_at, land_at, peer) in enumerate(plan())]


def _split_exchange_start(plan_and_count, arrs, land_shapes, name, after=None):
    plan, n_sems = plan_and_count
    n = len(arrs)

    n_in = 2 * n + (after is not None)

    def body(*refs):
        srcs, lands = refs[:n], refs[n:2 * n]
        send_sems, recv_sems = refs[n_in:n_in + n_sems], refs[n_in + n_sems:n_in + 2 * n_sems]
        token = refs[-1]
        for copy in _planned_copies(plan, srcs, lands, send_sems, recv_sems):
            copy.start()
        token[...] = jnp.zeros_like(token)

    hbm = lambda a: pltpu.HBM(a.shape, a.dtype)
    operands = [pltpu.with_memory_space_constraint(a, pltpu.HBM) for a in arrs]
    operands += [pltpu.with_memory_space_constraint(lax.empty(shape, a.dtype), pltpu.HBM)
                 for a, shape in zip(arrs, land_shapes)]
    out = pl.pallas_call(
        body, name=name,
        out_shape=(*[pltpu.SemaphoreType.DMA(())] * (2 * n_sems),
                   *[hbm(a) for a in operands],
                   jax.ShapeDtypeStruct((8, 128), F32)),
        in_specs=[HBM_ONLY] * (2 * n) + [pl.BlockSpec(memory_space=pl.ANY)] * (after is not None),
        out_specs=(*[SEM_SPEC] * (2 * n_sems), *[HBM_ONLY] * (2 * n),
                   pl.BlockSpec(memory_space=pltpu.VMEM)),
        input_output_aliases={i: 2 * n_sems + i for i in range(2 * n)},
        compiler_params=pltpu.CompilerParams(has_side_effects=SPLIT_COPY_EFFECT),
    )(*operands, *([after] if after is not None else []))
    sems, rest = list(out[:2 * n_sems]), out[2 * n_sems:]
    return sems, list(rest[:n]), list(rest[n:2 * n]), rest[-1]


def _split_exchange_wait(plan_and_count, sems, srcs, lands, after, name):
    plan, n_sems = plan_and_count
    n = len(srcs)

    def body(*refs):
        src_refs, land_refs = refs[:n], refs[n:2 * n]
        send_sems, recv_sems = refs[2 * n:2 * n + n_sems], refs[2 * n + n_sems:2 * n + 2 * n_sems]
        for copy in _planned_copies(plan, src_refs, land_refs, send_sems, recv_sems):
            copy.wait_send()
            copy.wait_recv()

    hbm = lambda a: pltpu.HBM(a.shape, a.dtype)
    out = pl.pallas_call(
        body, name=name,
        out_shape=(*[hbm(a) for a in srcs], *[hbm(a) for a in lands]),
        in_specs=[HBM_ONLY] * (2 * n) + [SEM_SPEC] * (2 * n_sems) + [pl.BlockSpec(memory_space=pl.ANY)],
        out_specs=tuple([HBM_ONLY] * (2 * n)),
        input_output_aliases={i: i for i in range(2 * n)},
        compiler_params=pltpu.CompilerParams(has_side_effects=SPLIT_COPY_EFFECT),
    )(*srcs, *lands, *sems, after)
    return list(out[n:])


def _adamw_exchanged(w, m, v, own, landed, chip, name):
    rows, cols = w.shape
    tr, tc = _tile_2d(rows, cols)

    def body(chip_ref, w_ref, m_ref, v_ref, own_ref, l1_ref, l2_ref, l3_ref, go_ref, d_ref, mo_ref, vo_ref):
        g = own_ref[...].astype(F32)
        for ref in (l1_ref, l2_ref, l3_ref):
            g = g + ref[...].astype(F32)
        _adamw_update(g, w_ref, m_ref, v_ref, go_ref, d_ref, mo_ref, vo_ref)

    blk = pl.BlockSpec((tr, tc), lambda i, j, chip_ref: (i, j))
    slot = lambda k: pl.BlockSpec((None, tr, tc), lambda i, j, chip_ref: (chip_ref[0] ^ k, i, j))
    o = jax.ShapeDtypeStruct((rows, cols), F32)
    return pl.pallas_call(
        body, name=name, out_shape=(o, o, o, o),
        grid_spec=pltpu.PrefetchScalarGridSpec(
            num_scalar_prefetch=1, grid=(rows // tr, cols // tc),
            in_specs=[blk, blk, blk, slot(0), slot(1), slot(2), slot(3)],
            out_specs=(blk, blk, blk, blk)),
        compiler_params=_cparams("parallel", "parallel"),
    )(chip, w, m, v, own, landed, landed, landed)


def _block_diag(t):
    nb, gpb, r, c = t.shape
    eye = jnp.eye(gpb, dtype=t.dtype)
    return jnp.einsum("ngrc,gh->ngrhc", t, eye).reshape(nb, gpb * r, gpb * c)


def _diag_blocks(t, r, c):
    nb = t.shape[0]
    gpb = t.shape[1] // r
    t = t.reshape(nb, gpb, r, gpb, c)
    return jnp.einsum("ngrhc,gh->ngrc", t, jnp.eye(gpb, dtype=t.dtype))


def _pack_rows(parts):
    flat = jnp.concatenate([p.reshape(-1).astype(F32) for p in parts])
    pad = (-flat.shape[0]) % (256 * 128)
    return jnp.pad(flat, (0, pad)).reshape(-1, 128)


def _unpack_rows(packed, shapes):
    flat = packed.reshape(-1)
    out, at = [], 0
    for shape in shapes:
        size = math.prod(shape)
        out.append(flat[at:at + size].reshape(shape))
        at += size
    return out


def kernel(x, ln_w, w_in, s5_lam_re, s5_lam_im, s5_log_step, s5_b_re, s5_b_im, s5_c_re, s5_c_im, s5_d, s5_w_glu, s5_w_up, dn_conv_w, dn_a_log, dn_dt_bias, dn_norm_w, dn_w_up, w_out, final_norm_w, loss_target, m_ln_w, m_w_in, m_s5_lam_re, m_s5_lam_im, m_s5_log_step, m_s5_b_re, m_s5_b_im, m_s5_c_re, m_s5_c_im, m_s5_d, m_s5_w_glu, m_s5_w_up, m_dn_conv_w, m_dn_a_log, m_dn_dt_bias, m_dn_norm_w, m_dn_w_up, m_w_out, m_final_norm_w, v_ln_w, v_w_in, v_s5_lam_re, v_s5_lam_im, v_s5_log_step, v_s5_b_re, v_s5_b_im, v_s5_c_re, v_s5_c_im, v_s5_d, v_s5_w_glu, v_s5_w_up, v_dn_conv_w, v_dn_a_log, v_dn_dt_bias, v_dn_norm_w, v_dn_w_up, v_w_out, v_final_norm_w):
    weights = dict(ln_w=ln_w, w_in=w_in, s5_lam_re=s5_lam_re, s5_lam_im=s5_lam_im,
                   s5_log_step=s5_log_step, s5_b_re=s5_b_re, s5_b_im=s5_b_im, s5_c_re=s5_c_re,
                   s5_c_im=s5_c_im, s5_d=s5_d, s5_w_glu=s5_w_glu, s5_w_up=s5_w_up,
                   dn_conv_w=dn_conv_w, dn_a_log=dn_a_log, dn_dt_bias=dn_dt_bias,
                   dn_norm_w=dn_norm_w, dn_w_up=dn_w_up, w_out=w_out, final_norm_w=final_norm_w)
    mom_m = dict(ln_w=m_ln_w, w_in=m_w_in, s5_lam_re=m_s5_lam_re, s5_lam_im=m_s5_lam_im,
                 s5_log_step=m_s5_log_step, s5_b_re=m_s5_b_re, s5_b_im=m_s5_b_im,
                 s5_c_re=m_s5_c_re, s5_c_im=m_s5_c_im, s5_d=m_s5_d, s5_w_glu=m_s5_w_glu,
                 s5_w_up=m_s5_w_up, dn_conv_w=m_dn_conv_w, dn_a_log=m_dn_a_log,
                 dn_dt_bias=m_dn_dt_bias, dn_norm_w=m_dn_norm_w, dn_w_up=m_dn_w_up,
                 w_out=m_w_out, final_norm_w=m_final_norm_w)
    mom_v = dict(ln_w=v_ln_w, w_in=v_w_in, s5_lam_re=v_s5_lam_re, s5_lam_im=v_s5_lam_im,
                 s5_log_step=v_s5_log_step, s5_b_re=v_s5_b_re, s5_b_im=v_s5_b_im,
                 s5_c_re=v_s5_c_re, s5_c_im=v_s5_c_im, s5_d=v_s5_d, s5_w_glu=v_s5_w_glu,
                 s5_w_up=v_s5_w_up, dn_conv_w=v_dn_conv_w, dn_a_log=v_dn_a_log,
                 dn_dt_bias=v_dn_dt_bias, dn_norm_w=v_dn_norm_w, dn_w_up=v_dn_w_up,
                 w_out=v_w_out, final_norm_w=v_final_norm_w)
    names = list(weights)

    l, d = x.shape[1], x.shape[2]
    d_s5 = d // 2
    groups = d_s5 // S5_GROUP
    nb = groups // S5_GPB
    d_dn = DN_HEADS * DN_HEAD_DIM
    w_in_cols = w_in.shape[2]
    d_in = N_DEV * w_in_cols
    off_ba_src = 2 * d_s5 + 4 * d_dn
    off_u, off_zs, off_qkv, off_zd = 0, d_s5, 2 * d_s5, 2 * d_s5 + 3 * d_dn
    off_ba = off_zd + d_dn
    n_main = off_ba + BA_PAD
    off_gs, off_gd = 0, d
    x2d, tgt2d = x[0], loss_target[0]
    my_index = 4 * lax.axis_index("x") + 2 * lax.axis_index("y") + lax.axis_index("c")

    g_win, g_conv = _all_gather([jnp.transpose(w_in[0]).astype(BF16), dn_conv_w[0]], name="gather_weights")
    late_plan = _gather_plan(4)
    late_shards = [s5_w_glu[0].astype(BF16), s5_w_up[0].astype(BF16), dn_w_up[0].astype(BF16),
                   w_out[0].astype(BF16)]
    late_sems, late_shards, late_lands, late_token = _split_exchange_start(
        late_plan, late_shards, [(N_DEV,) + s.shape for s in late_shards], name="gather_late_start",
        after=g_conv)
    ba_end = off_ba_src + 2 * DN_HEADS
    w_full_t = g_win.reshape(d_in, d)
    w_gates_t = w_full_t[ba_end:]
    conv_full = jnp.transpose(g_conv, (1, 0, 2)).reshape(CONV_K, 3 * d_dn)

    lam_re, lam_im = s5_lam_re[0], s5_lam_im[0]
    log_step = s5_log_step[0].reshape(groups, 1)
    b_re = s5_b_re[0].reshape(groups * S5_STATE, S5_GROUP)
    b_im = s5_b_im[0].reshape(groups * S5_STATE, S5_GROUP)
    abar_re, abar_im, f_re, f_im = _s5_disc_fwd(lam_re, lam_im, log_step)
    f_re_col, f_im_col = f_re.reshape(-1, 1), f_im.reshape(-1, 1)
    bb_re, bb_im = _s5_bbar_fwd(f_re_col, f_im_col, b_re, b_im)

    def bb_blocks(t):
        t = t.reshape(nb, S5_GPB, S5_STATE, S5_GROUP).transpose(0, 1, 3, 2)
        return _block_diag(t).astype(BF16)

    def c_blocks(t):
        return _block_diag(t.reshape(nb, S5_GPB, S5_GROUP, S5_STATE)).astype(BF16)

    bbr, bbi = bb_blocks(bb_re), bb_blocks(bb_im)
    cbr, cbi = c_blocks(s5_c_re[0]), c_blocks(s5_c_im[0])
    ctr, cti = jnp.transpose(cbr, (0, 2, 1)), jnp.transpose(cbi, (0, 2, 1))
    a_re = abar_re.reshape(nb, 1, S5_GPB * S5_STATE)
    a_im = abar_im.reshape(nb, 1, S5_GPB * S5_STATE)

    h = _rms_fwd(x2d, ln_w)
    proj = _mm(h, w_full_t, tb=True, b_rows=n_main, tn=n_main // 4, after=late_token, name="proj")
    proj_gates = _mm(h, w_gates_t, tb=True, tn=1024, name="proj_gates")
    y1, car_r, car_i, states_r, states_i = _s5_fwd(proj, bbr, bbi, a_re, a_im, ctr, cti, s5_d, d_s5)
    a_log_row = jnp.pad(dn_a_log, ((0, 0), (DN_HEADS, 128 - 2 * DN_HEADS)))
    dt_row = jnp.pad(dn_dt_bias, ((0, 0), (DN_HEADS, 128 - 2 * DN_HEADS)))
    qkv = _dn_prep_fwd(proj, off_qkv, conv_full)
    gb = _dn_gates_fwd(proj, off_ba, a_log_row, dt_row)
    o_dn, states = _dn_chunk_fwd(qkv, gb)

    late_lands = _split_exchange_wait(late_plan, late_sems, late_shards, late_lands, o_dn,
                                      name="gather_late_wait")
    g_glu, g_sup, g_dup, g_wout = [
        lax.dynamic_update_slice(land, shard[None], (my_index, 0, 0))
        for land, shard in zip(late_lands, late_shards)]
    wglu_full = g_glu.reshape(d_s5, d_s5)
    wsup_full = jnp.transpose(g_sup, (1, 0, 2)).reshape(d_s5, d)
    wdup_full = jnp.transpose(g_dup, (1, 0, 2)).reshape(d_dn, d)
    wout_full = g_wout.reshape(d, d)

    out_s = _s5_glu_fwd(y1, proj, off_zs, wglu_full)
    y_s = _mm(out_s, wsup_full, name="s5_up")
    out_d = _dn_out_fwd(o_dn, proj, off_zd, dn_norm_w)
    y_d = _mm(out_d, wdup_full, name="dn_up")

    mixed = _merge_fwd(proj_gates, off_gs, off_gd, y_s, y_d)
    branch = _mm(mixed, wout_full, name="w_out")
    dx2, dx2_bf, loss_dev, d_final_w = _final(x2d, branch, final_norm_w.reshape(1, d), tgt2d)

    g_wout_full = _mm(mixed, dx2_bf, ta=True, out_dtype=BF16, name="grad_w_out")
    dmixed = _mm(dx2_bf, wout_full, tb=True, name="d_mixed")
    dgs, dgd, dys, dyd = _merge_bwd(proj_gates, off_gs, off_gd, y_s, y_d, dmixed)

    g_dup_full = _mm(out_d, dyd, ta=True, out_dtype=BF16, name="grad_dn_up")
    dout_d = _mm(dyd, wdup_full, tb=True, name="d_out_d")
    do_dn, dzd, d_norm_w = _dn_out_bwd(o_dn, proj, off_zd, dn_norm_w, dout_d)
    dqkv, dgb_heads = _dn_chunk_bwd(qkv, gb, states, do_dn)
    dba, d_a_log_row, d_dt_row = _dn_gates_bwd(proj, off_ba, a_log_row, dt_row, dgb_heads)
    dqkv_pre, d_conv_full = _dn_prep_bwd(proj, off_qkv, conv_full, dqkv)

    g_sup_full = _mm(out_s, dys, ta=True, out_dtype=BF16, name="grad_s5_up")
    dout_s = _mm(dys, wsup_full, tb=True, name="d_out_s")
    dy1, dzs, g_glu_full = _s5_glu_bwd(y1, proj, off_zs, wglu_full, dout_s)

    def by_dest(t, axis=0):
        if axis == 1:
            return t.reshape(t.shape[0], 4, 2, t.shape[1] // N_DEV).transpose(2, 1, 0, 3)
        return t.reshape(4, 2, t.shape[0] // N_DEV, t.shape[1]).transpose(1, 0, 2, 3)

    core = lax.axis_index("c").astype(jnp.int32).reshape(1)
    chip = (2 * lax.axis_index("x") + lax.axis_index("y")).astype(jnp.int32).reshape(1)

    def chip_sums_of(which, parts, tag):
        from_sibling = _sibling_swap(parts, name="swap_grads_" + tag)
        return [_pair_sum(p.reshape(2, -1, p.shape[-1]), got.reshape(-1, got.shape[-1]), core,
                          name="pair_sum_" + nm).reshape(got.shape)
                for nm, p, got in zip(which, parts, from_sibling)]

    early = ["s5_w_glu", "s5_w_up", "dn_w_up", "w_out"]
    sums_a = chip_sums_of(early, [by_dest(g_glu_full.astype(BF16)), by_dest(g_sup_full, 1),
                                  by_dest(g_dup_full, 1), by_dest(g_wout_full)], "a")
    plan_a = _chip_slices_plan(len(sums_a))
    sems_a, src_a, land_a, token_a = _split_exchange_start(
        plan_a, sums_a, [t.shape for t in sums_a], name="exchange_start_a")

    (du, d_a_re, d_a_im, d_bbr, d_bbi, d_cbr, d_cbi, d_s5_d) = _s5_bwd(
        proj, dy1, bbr, bbi, a_re, a_im, cbr, cbi, s5_d + token_a[:1, :1], car_r, car_i,
        states_r, states_i)

    def from_bb_blocks(t):
        t = _diag_blocks(t, S5_GROUP, S5_STATE).transpose(0, 1, 3, 2)
        return t.reshape(groups * S5_STATE, S5_GROUP)

    d_f_re, d_f_im, d_b_re, d_b_im = _s5_bbar_bwd(f_re_col, f_im_col, b_re, b_im,
                                                 from_bb_blocks(d_bbr), from_bb_blocks(d_bbi))
    d_lam_re, d_lam_im, d_log_step = _s5_disc_bwd(
        lam_re, lam_im, log_step, d_a_re.reshape(groups, S5_STATE), d_a_im.reshape(groups, S5_STATE),
        d_f_re.reshape(groups, S5_STATE), d_f_im.reshape(groups, S5_STATE))
    d_c_re = _diag_blocks(d_cbr, S5_GROUP, S5_STATE).reshape(groups, S5_GROUP, S5_STATE)
    d_c_im = _diag_blocks(d_cbi, S5_GROUP, S5_STATE).reshape(groups, S5_GROUP, S5_STATE)

    dproj = jnp.concatenate([du, dzs, dqkv_pre, dzd, jnp.pad(dba, ((0, 0), (0, BA_PAD - 128)))], axis=1)
    dproj_gates = jnp.concatenate([dgs, dgd], axis=1)
    g_main_t = _mm(dproj, h, ta=True, out_dtype=BF16, tm=512, tn=d, name="grad_w_in")
    g_gates_t = _mm(dproj_gates, h, ta=True, out_dtype=BF16, tm=1024, tn=d, name="grad_w_in_gates")
    g_win_full_t = jnp.concatenate([g_main_t[:ba_end], g_gates_t], axis=0)
    sums_b = chip_sums_of(["w_in"], [by_dest(g_win_full_t)], "b")
    plan_b = _chip_slices_plan(1)
    sems_b, src_b, land_b, token_b = _split_exchange_start(
        plan_b, sums_b, [t.shape for t in sums_b], name="exchange_start_b")
    dh_main = _mm(dproj, w_full_t, b_rows=n_main, tm=1024, tn=1024, tk=n_main // 4, after=token_b,
                  name="d_h_main")
    dh = _mm(dproj_gates, w_gates_t, tm=1024, tn=1024, tk=2048, addend=dh_main, name="d_h")
    grad_x, d_ln_w = _rms_bwd(x2d, ln_w, dh, dx2)
    big = ["w_in"] + early
    results = {}

    small = [nm for nm in names if nm not in big]
    small_grads = dict(
        ln_w=d_ln_w, s5_lam_re=d_lam_re, s5_lam_im=d_lam_im, s5_log_step=d_log_step,
        s5_b_re=d_b_re, s5_b_im=d_b_im, s5_c_re=d_c_re, s5_c_im=d_c_im, s5_d=d_s5_d,
        dn_conv_w=d_conv_full, dn_a_log=d_a_log_row[:, DN_HEADS:2 * DN_HEADS],
        dn_dt_bias=d_dt_row[:, DN_HEADS:2 * DN_HEADS], dn_norm_w=d_norm_w, final_norm_w=d_final_w)
    (all_small,) = _all_gather([_pack_rows([small_grads[nm] for nm in small])], name="gather_small_grads")
    summed = _slot_sum(all_small, name="sum_small_grads")
    full_shapes = [(CONV_K, 3 * d_dn) if nm == "dn_conv_w" else weights[nm].shape for nm in small]
    g_small = dict(zip(small, _unpack_rows(summed, full_shapes)))
    conv_cols = dn_conv_w.shape[2]
    g_small["dn_conv_w"] = lax.dynamic_slice_in_dim(
        g_small["dn_conv_w"], my_index * conv_cols, conv_cols, axis=1).reshape(dn_conv_w.shape)
    packed = [_pack_rows([t[nm] for nm in small]) for t in (weights, mom_m, mom_v, g_small)]
    small_out = _adamw(packed[0], packed[1], packed[2], packed[3][None], name="adamw_small")
    small_shapes = [weights[nm].shape for nm in small]
    for kind, packed_out in enumerate(small_out):
        for nm, val in zip(small, _unpack_rows(packed_out, small_shapes)):
            results.setdefault(nm, [None] * 4)[kind] = val

    land_a = _split_exchange_wait(plan_a, sems_a, src_a, land_a, small_out[0], name="exchange_wait_a")
    for nm, own, landed in zip(early, src_a, land_a):
        results[nm] = _adamw_exchanged(weights[nm][0], mom_m[nm][0], mom_v[nm][0], own, landed, chip,
                                       name="adamw_" + nm)
    (land_b,) = _split_exchange_wait(plan_b, sems_b, src_b, land_b, results[early[-1]][0],
                                     name="exchange_wait_b")
    res = _adamw_exchanged(jnp.transpose(w_in[0]), jnp.transpose(m_w_in[0]), jnp.transpose(v_w_in[0]),
                           src_b[0], land_b, chip, name="adamw_w_in")
    results["w_in"] = [jnp.transpose(t) for t in res]

    loss = lax.psum(loss_dev[0, 0], ("x", "y", "c"))
    outs = [loss, grad_x[None]]
    for kind in range(4):
        outs += [results[nm][kind].reshape(weights[nm].shape) for nm in names]
    return tuple(outs)
```

```python
import functools
import math

import jax
import jax.numpy as jnp
from jax import lax
from jax.experimental import pallas as pl
from jax.experimental.pallas import tpu as pltpu

F32 = jnp.float32
BF16 = jnp.bfloat16
HIGHEST = lax.Precision.HIGHEST
MESH = pl.DeviceIdType.MESH
N_DEV = 8

EPS = 1e-6
S5_GROUP = 16
S5_STATE = 64
S5_GPB = 8
S5_T = 256
DN_HEADS = 8
DN_HEAD_DIM = 128
CHUNK = 64
DN_HEADS_PER_STEP = 8
CONV_K = 4
BA_PAD = 512

ADAM_LR = 0.001
ADAM_B1 = 0.9
ADAM_B2 = 0.999
ADAM_EPS = 1e-08
ADAM_WD = 0.01
ADAM_STEP = 10

VMEM_LIMIT_BYTES = 48 * 1024 * 1024
ROW_TILE = 256


def _cparams(*sem):
    return pltpu.CompilerParams(dimension_semantics=sem if sem else None,
                                vmem_limit_bytes=VMEM_LIMIT_BYTES)


def _sigmoid(x):
    return 1.0 / (1.0 + jnp.exp(-x))


def _silu(x):
    return x * _sigmoid(x)


def _gelu(x):
    return 0.5 * x * (1.0 + jnp.tanh(0.7978845608028654 * (x + 0.044715 * x * x * x)))


def _softplus(x):
    return jnp.maximum(x, 0.0) + jnp.log(1.0 + jnp.exp(-jnp.abs(x)))


def _rmsnorm(x, w):
    return x * lax.rsqrt(jnp.mean(x * x, axis=-1, keepdims=True) + EPS) * w


def _dot(a, b, dims=((1,), (0,)), precision=None):
    return lax.dot_general(a, b, (dims, ((), ())), precision=precision,
                           preferred_element_type=F32)


def _bdot(a, b, dims=((1,), (0,))):
    return _dot(a.astype(BF16), b.astype(BF16), dims)


def _split_bf16(a):
    hi = a.astype(BF16)
    return hi, (a - hi.astype(F32)).astype(BF16)


def _dot3_dims(a, b, dims):
    ah, al = _split_bf16(a)
    bh, bl = _split_bf16(b)
    return _dot(ah, bh, dims) + (_dot(ah, bl, dims) + _dot(al, bh, dims))


@jax.custom_vjp
def _dot3(a, b):
    return _dot3_dims(a, b, ((1,), (0,)))


def _dot3_fwd(a, b):
    return _dot3(a, b), (a, b)


def _dot3_bwd(res, g):
    a, b = res
    return _dot3_dims(g, b, ((1,), (1,))), _dot3_dims(a, g, ((0,), (0,)))


_dot3.defvjp(_dot3_fwd, _dot3_bwd)


def _mm(a, b, *, ta=False, tb=False, out_dtype=F32, tm=512, tn=512, tk=None, after=None, b_rows=None,
        addend=None, name):
    k_dim, m_dim = (a.shape if ta else a.shape[::-1])
    b_rows = b.shape[0] if b_rows is None else b_rows
    n_dim = b_rows if tb else b.shape[1]
    assert (b.shape[1] if tb else b_rows) == k_dim and b_rows <= b.shape[0]
    tm, tn = min(tm, m_dim), min(tn, n_dim)
    tk = k_dim if tk is None else tk
    assert m_dim % tm == 0 and n_dim % tn == 0 and k_dim % tk == 0
    nk = k_dim // tk
    a_spec = (pl.BlockSpec((tk, tm), lambda i, j, k: (k, i)) if ta
              else pl.BlockSpec((tm, tk), lambda i, j, k: (i, k)))
    b_spec = (pl.BlockSpec((tn, tk), lambda i, j, k: (j, k)) if tb
              else pl.BlockSpec((tk, tn), lambda i, j, k: (k, j)))
    dims = ((0 if ta else 1,), (1 if tb else 0,))

    extras = ([after] if after is not None else []) + ([addend] if addend is not None else [])
    extra_specs = ([pl.BlockSpec((8, 128), lambda i, j, k: (0, 0))] if after is not None else []) + (
        [pl.BlockSpec((tm, tn), lambda i, j, k: (i, j))] if addend is not None else [])

    def body(a_ref, b_ref, *rest):
        o_ref, *scratch = rest[len(extras):]
        p = _bdot(a_ref[...], b_ref[...], dims)
        finish = (lambda v: v + rest[len(extras) - 1][...]) if addend is not None else (lambda v: v)
        if nk == 1:
            o_ref[...] = finish(p).astype(o_ref.dtype)
        else:
            acc = scratch[0]
            k = pl.program_id(2)

            @pl.when(k == 0)
            def _():
                acc[...] = p

            @pl.when(k > 0)
            def _():
                acc[...] += p

            @pl.when(k == nk - 1)
            def _():
                o_ref[...] = finish(acc[...]).astype(o_ref.dtype)

    return pl.pallas_call(
        body, name=name,
        out_shape=jax.ShapeDtypeStruct((m_dim, n_dim), out_dtype),
        grid=(m_dim // tm, n_dim // tn, nk),
        in_specs=[a_spec, b_spec] + extra_specs,
        out_specs=pl.BlockSpec((tm, tn), lambda i, j, k: (i, j)),
        scratch_shapes=[pltpu.VMEM((tm, tn), F32)] if nk > 1 else [],
        compiler_params=_cparams("parallel", "parallel", "arbitrary"),
    )(a, b, *extras)


def _rms_fwd(x, w):
    l, d = x.shape

    def body(x_ref, w_ref, h_ref):
        h_ref[...] = _rmsnorm(x_ref[...], w_ref[...]).astype(BF16)

    return pl.pallas_call(
        body, name="rms_fwd",
        out_shape=jax.ShapeDtypeStruct((l, d), BF16),
        grid=(l // ROW_TILE,),
        in_specs=[pl.BlockSpec((ROW_TILE, d), lambda i: (i, 0)),
                  pl.BlockSpec((1, d), lambda i: (0, 0))],
        out_specs=pl.BlockSpec((ROW_TILE, d), lambda i: (i, 0)),
        compiler_params=_cparams("parallel"),
    )(x, w)


def _rms_bwd(x, w, dh, dres):
    l, d = x.shape

    def body(x_ref, w_ref, dh_ref, dres_ref, dx_ref, dw_ref):
        _, vjp = jax.vjp(_rmsnorm, x_ref[...], w_ref[...])
        dx, dw = vjp(dh_ref[...])
        dx_ref[...] = dx + dres_ref[...]

        @pl.when(pl.program_id(0) == 0)
        def _():
            dw_ref[...] = jnp.zeros_like(dw_ref)

        dw_ref[...] += dw

    row = pl.BlockSpec((ROW_TILE, d), lambda i: (i, 0))
    vec = pl.BlockSpec((1, d), lambda i: (0, 0))
    return pl.pallas_call(
        body, name="rms_bwd",
        out_shape=(jax.ShapeDtypeStruct((l, d), F32), jax.ShapeDtypeStruct((1, d), F32)),
        grid=(l // ROW_TILE,),
        in_specs=[row, vec, row, row],
        out_specs=(row, vec),
        compiler_params=_cparams("arbitrary"),
    )(x, w, dh, dres)


def _final(x, r, fw, target):
    l, d = x.shape

    def per_row_loss(x2, w, tgt):
        err = _rmsnorm(x2, w) - tgt
        return 0.5 * jnp.mean(err * err, axis=-1, keepdims=True)

    def body(x_ref, r_ref, w_ref, t_ref, dx_ref, dxb_ref, loss_ref, dw_ref):
        x2 = x_ref[...] + r_ref[...]
        rows, vjp = jax.vjp(functools.partial(per_row_loss, tgt=t_ref[...]), x2, w_ref[...])
        dx2, dw = vjp(jnp.ones_like(rows))
        dx_ref[...] = dx2
        dxb_ref[...] = dx2.astype(BF16)

        @pl.when(pl.program_id(0) == 0)
        def _():
            dw_ref[...] = jnp.zeros_like(dw_ref)
            loss_ref[...] = jnp.zeros_like(loss_ref)

        dw_ref[...] += dw
        loss_ref[...] += jnp.sum(rows, axis=0, keepdims=True)

    row = pl.BlockSpec((ROW_TILE, d), lambda i: (i, 0))
    vec = pl.BlockSpec((1, d), lambda i: (0, 0))
    return pl.pallas_call(
        body, name="final_norm_loss",
        out_shape=(jax.ShapeDtypeStruct((l, d), F32), jax.ShapeDtypeStruct((l, d), BF16),
                   jax.ShapeDtypeStruct((1, 1), F32), jax.ShapeDtypeStruct((1, d), F32)),
        grid=(l // ROW_TILE,),
        in_specs=[row, row, vec, row],
        out_specs=(row, row, pl.BlockSpec((1, 1), lambda i: (0, 0)), vec),
        compiler_params=_cparams("arbitrary"),
    )(x, r, fw, target)


def _merge_fn(gs, gd, ys, yd):
    return _sigmoid(gs) * ys + _sigmoid(gd) * yd


def _merge_fwd(proj, off_gs, off_gd, ys, yd):
    l, d = ys.shape
    cw = 512
    blk = lambda off: pl.BlockSpec((ROW_TILE, cw), lambda i, j: (i, off // cw + j))

    def body(gs_ref, gd_ref, ys_ref, yd_ref, o_ref):
        o_ref[...] = _merge_fn(gs_ref[...], gd_ref[...], ys_ref[...], yd_ref[...]).astype(BF16)

    return pl.pallas_call(
        body, name="merge_fwd",
        out_shape=jax.ShapeDtypeStruct((l, d), BF16),
        grid=(l // ROW_TILE, d // cw),
        in_specs=[blk(off_gs), blk(off_gd), blk(0), blk(0)],
        out_specs=blk(0),
        compiler_params=_cparams("parallel", "parallel"),
    )(proj, proj, ys, yd)


def _merge_bwd(proj, off_gs, off_gd, ys, yd, dmixed):
    l, d = ys.shape
    cw = 512
    blk = lambda off: pl.BlockSpec((ROW_TILE, cw), lambda i, j: (i, off // cw + j))

    def body(gs_ref, gd_ref, ys_ref, yd_ref, dm_ref, dgs_ref, dgd_ref, dys_ref, dyd_ref):
        _, vjp = jax.vjp(_merge_fn, gs_ref[...], gd_ref[...], ys_ref[...], yd_ref[...])
        dgs, dgd, dys, dyd = vjp(dm_ref[...])
        dgs_ref[...] = dgs.astype(BF16)
        dgd_ref[...] = dgd.astype(BF16)
        dys_ref[...] = dys.astype(BF16)
        dyd_ref[...] = dyd.astype(BF16)

    out = jax.ShapeDtypeStruct((l, d), BF16)
    return pl.pallas_call(
        body, name="merge_bwd",
        out_shape=(out, out, out, out),
        grid=(l // ROW_TILE, d // cw),
        in_specs=[blk(off_gs), blk(off_gd), blk(0), blk(0), blk(0)],
        out_specs=(blk(0), blk(0), blk(0), blk(0)),
        compiler_params=_cparams("parallel", "parallel"),
    )(proj, proj, ys, yd, dmixed)


def _s5_disc_fn(lam_re, lam_im, log_step):
    step = jnp.exp(log_step)
    mag = jnp.exp(lam_re * step)
    abar_re = mag * jnp.cos(lam_im * step)
    abar_im = mag * jnp.sin(lam_im * step)
    den = lam_re * lam_re + lam_im * lam_im
    xr = abar_re - 1.0
    f_re = (xr * lam_re + abar_im * lam_im) / den
    f_im = (abar_im * lam_re - xr * lam_im) / den
    return abar_re, abar_im, f_re, f_im


def _s5_disc_fwd(lam_re, lam_im, log_step):
    g, p = lam_re.shape

    def body(lr_ref, li_ref, ls_ref, ar_ref, ai_ref, fr_ref, fi_ref):
        ar, ai, fr, fi = _s5_disc_fn(lr_ref[...], li_ref[...], ls_ref[...])
        ar_ref[...] = ar
        ai_ref[...] = ai
        fr_ref[...] = fr
        fi_ref[...] = fi

    o = jax.ShapeDtypeStruct((g, p), F32)
    return pl.pallas_call(body, name="s5_disc_fwd", out_shape=(o, o, o, o),
                          compiler_params=_cparams())(lam_re, lam_im, log_step)


def _s5_disc_bwd(lam_re, lam_im, log_step, dar, dai, dfr, dfi):
    g, p = lam_re.shape

    def body(lr_ref, li_ref, ls_ref, dar_ref, dai_ref, dfr_ref, dfi_ref, dlr_ref, dli_ref, dls_ref):
        _, vjp = jax.vjp(_s5_disc_fn, lr_ref[...], li_ref[...], ls_ref[...])
        dlr, dli, dls = vjp((dar_ref[...], dai_ref[...], dfr_ref[...], dfi_ref[...]))
        dlr_ref[...] = dlr
        dli_ref[...] = dli
        dls_ref[...] = dls

    o = jax.ShapeDtypeStruct((g, p), F32)
    return pl.pallas_call(body, name="s5_disc_bwd",
                          out_shape=(o, o, jax.ShapeDtypeStruct((g, 1), F32)),
                          compiler_params=_cparams())(lam_re, lam_im, log_step, dar, dai, dfr, dfi)


def _s5_bbar_fwd(f_re, f_im, b_re, b_im):
    n, c = b_re.shape

    def body(fr_ref, fi_ref, br_ref, bi_ref, or_ref, oi_ref):
        fr, fi, br, bi = fr_ref[...], fi_ref[...], br_ref[...], bi_ref[...]
        or_ref[...] = fr * br - fi * bi
        oi_ref[...] = fr * bi + fi * br

    o = jax.ShapeDtypeStruct((n, c), F32)
    return pl.pallas_call(body, name="s5_bbar_fwd", out_shape=(o, o),
                          compiler_params=_cparams())(f_re, f_im, b_re, b_im)


def _s5_bbar_bwd(f_re, f_im, b_re, b_im, dbr, dbi):
    n, c = b_re.shape

    def body(fr_ref, fi_ref, br_ref, bi_ref, dor_ref, doi_ref, dfr_ref, dfi_ref, dbr_ref, dbi_ref):
        fr, fi, br, bi = fr_ref[...], fi_ref[...], br_ref[...], bi_ref[...]
        dor, doi = dor_ref[...], doi_ref[...]
        dfr_ref[...] = jnp.sum(dor * br + doi * bi, axis=-1, keepdims=True)
        dfi_ref[...] = jnp.sum(doi * br - dor * bi, axis=-1, keepdims=True)
        dbr_ref[...] = fr * dor + fi * doi
        dbi_ref[...] = fr * doi - fi * dor

    col = jax.ShapeDtypeStruct((n, 1), F32)
    o = jax.ShapeDtypeStruct((n, c), F32)
    return pl.pallas_call(body, name="s5_bbar_bwd", out_shape=(col, col, o, o),
                          compiler_params=_cparams())(f_re, f_im, b_re, b_im, dbr, dbi)


SUBLANES = 8


def _scan_groups(xr, xi, ar, ai, reverse):
    t = xr.shape[0]
    sub = lax.broadcasted_iota(jnp.int32, (t, 1), 0) & (SUBLANES - 1)
    pr, pi = ar, ai
    for sh in (1, 2, 4):
        if reverse:
            keep = sub < SUBLANES - sh
            sr, si = pltpu.roll(xr, t - sh, 0), pltpu.roll(xi, t - sh, 0)
        else:
            keep = sub >= sh
            sr, si = pltpu.roll(xr, sh, 0), pltpu.roll(xi, sh, 0)
        sr = jnp.where(keep, sr, 0.0)
        si = jnp.where(keep, si, 0.0)
        xr, xi = xr + pr * sr - pi * si, xi + pr * si + pi * sr
        pr, pi = pr * pr - pi * pi, 2.0 * pr * pi
    return xr, xi


def _scan_rows(xr, xi, ar, ai, cr, ci, sr_ref, si_ref, reverse):
    t, n = xr.shape
    xr, xi = _scan_groups(xr, xi, ar, ai, reverse)
    sr_ref[...] = xr
    si_ref[...] = xi
    sub = lax.broadcasted_iota(jnp.int32, (SUBLANES, n), 0)
    seed = sub == (SUBLANES - 1 if reverse else 0)
    pwr, pwi = _scan_groups(jnp.where(seed, ar, 0.0), jnp.where(seed, ai, 0.0), ar, ai, reverse)
    groups = range(t // SUBLANES)
    edge = 0 if reverse else SUBLANES - 1
    for g in (reversed(groups) if reverse else groups):
        rows = slice(g * SUBLANES, (g + 1) * SUBLANES)
        vr = sr_ref[rows, :] + (pwr * cr - pwi * ci)
        vi = si_ref[rows, :] + (pwr * ci + pwi * cr)
        sr_ref[rows, :] = vr
        si_ref[rows, :] = vi
        cr, ci = vr[edge:edge + 1, :], vi[edge:edge + 1, :]
    return cr, ci


def _s5_states(u_bf, bbr, bbi, ar, ai, cr, ci, sr_ref, si_ref):
    return _scan_rows(_dot(u_bf, bbr), _dot(u_bf, bbi), ar, ai, cr, ci, sr_ref, si_ref, reverse=False)


def _s5_fwd(proj, bbr, bbi, a_re, a_im, ctr, cti, d_skip, d_s5):
    l = proj.shape[0]
    nb, uc, ns = bbr.shape
    t = min(S5_T, l)
    nt = l // t

    def body(u_ref, bbr_ref, bbi_ref, ar_ref, ai_ref, ctr_ref, cti_ref, d_ref,
             y_ref, car_r_ref, car_i_ref, sr_ref, si_ref, cr, ci):
        @pl.when(pl.program_id(1) == 0)
        def _():
            cr[...] = jnp.zeros_like(cr)
            ci[...] = jnp.zeros_like(ci)

        car_r_ref[...] = cr[...]
        car_i_ref[...] = ci[...]
        u = u_ref[...]
        cr[...], ci[...] = _s5_states(u.astype(BF16), bbr_ref[...], bbi_ref[...], ar_ref[...],
                                      ai_ref[...], cr[...], ci[...], sr_ref, si_ref)
        y_ref[...] = (_bdot(sr_ref[...], ctr_ref[...]) - _bdot(si_ref[...], cti_ref[...])
                      + d_ref[...] * u)

    per_block = lambda shape: pl.BlockSpec((None,) + shape, lambda b, n: (b, 0, 0))
    return pl.pallas_call(
        body, name="s5_fwd",
        out_shape=(jax.ShapeDtypeStruct((l, d_s5), F32),
                   jax.ShapeDtypeStruct((nt, 1, nb * ns), F32),
                   jax.ShapeDtypeStruct((nt, 1, nb * ns), F32),
                   jax.ShapeDtypeStruct((l, nb * ns), F32),
                   jax.ShapeDtypeStruct((l, nb * ns), F32)),
        grid=(nb, nt),
        in_specs=[pl.BlockSpec((t, uc), lambda b, n: (n, b)),
                  per_block((uc, ns)), per_block((uc, ns)),
                  per_block((1, ns)), per_block((1, ns)),
                  per_block((ns, uc)), per_block((ns, uc)),
                  pl.BlockSpec((1, uc), lambda b, n: (0, b))],
        out_specs=(pl.BlockSpec((t, uc), lambda b, n: (n, b)),
                   pl.BlockSpec((None, 1, ns), lambda b, n: (n, 0, b)),
                   pl.BlockSpec((None, 1, ns), lambda b, n: (n, 0, b)),
                   pl.BlockSpec((t, ns), lambda b, n: (n, b)),
                   pl.BlockSpec((t, ns), lambda b, n: (n, b))),
        scratch_shapes=[pltpu.VMEM((1, ns), F32), pltpu.VMEM((1, ns), F32)],
        compiler_params=_cparams("parallel", "arbitrary"),
    )(proj, bbr, bbi, a_re, a_im, ctr, cti, d_skip)


def _s5_bwd(proj, dy, bbr, bbi, a_re, a_im, cbr, cbi, d_skip, car_r, car_i, states_r, states_i):
    l, d_s5 = dy.shape
    nb, uc, ns = bbr.shape
    t = min(S5_T, l)
    nt = l // t

    def body(u_ref, dy_ref, bbr_ref, bbi_ref, ar_ref, ai_ref, cbr_ref, cbi_ref, d_ref,
             car_r_ref, car_i_ref, sr_ref, si_ref,
             du_ref, dar_ref, dai_ref, dbbr_ref, dbbi_ref, dcbr_ref, dcbi_ref, dd_ref, gcr, gci,
             gr_ref, gi_ref):
        @pl.when(pl.program_id(1) == 0)
        def _():
            gcr[...] = jnp.zeros_like(gcr)
            gci[...] = jnp.zeros_like(gci)
            for ref in (dar_ref, dai_ref, dbbr_ref, dbbi_ref, dcbr_ref, dcbi_ref, dd_ref):
                ref[...] = jnp.zeros_like(ref)

        row = lax.broadcasted_iota(jnp.int32, (t, 1), 0)
        u, dy = u_ref[...], dy_ref[...]
        u_bf, dy_bf = u.astype(BF16), dy.astype(BF16)
        ar, ai = ar_ref[...], ai_ref[...]
        cr, ci = car_r_ref[...], car_i_ref[...]
        sr, si = sr_ref[...], si_ref[...]
        first = row == 0
        pr = jnp.where(first, cr, pltpu.roll(sr, 1, 0))
        pi = jnp.where(first, ci, pltpu.roll(si, 1, 0))
        gcr[...], gci[...] = _scan_rows(_dot(dy_bf, cbr_ref[...]), -_dot(dy_bf, cbi_ref[...]), ar, -ai,
                                        gcr[...], gci[...], gr_ref, gi_ref, reverse=True)
        gr, gi = gr_ref[...], gi_ref[...]
        dar_ref[...] += jnp.sum(gr * pr + gi * pi, axis=0, keepdims=True)
        dai_ref[...] += jnp.sum(gi * pr - gr * pi, axis=0, keepdims=True)
        gr_bf, gi_bf = gr.astype(BF16), gi.astype(BF16)
        tn = ((0,), (0,))
        dbbr_ref[...] += _dot(u_bf, gr_bf, tn)
        dbbi_ref[...] += _dot(u_bf, gi_bf, tn)
        dcbr_ref[...] += _dot(dy_bf, sr.astype(BF16), tn)
        dcbi_ref[...] -= _dot(dy_bf, si.astype(BF16), tn)
        nt_dims = ((1,), (1,))
        du = _dot(gr_bf, bbr_ref[...], nt_dims) + _dot(gi_bf, bbi_ref[...], nt_dims) + dy * d_ref[...]
        du_ref[...] = du.astype(BF16)
        dd_ref[...] += jnp.sum(dy * u, axis=0, keepdims=True)

    rev = lambda n: nt - 1 - n
    per_block = lambda shape: pl.BlockSpec((None,) + shape, lambda b, n: (b, 0, 0))
    acc = jax.ShapeDtypeStruct((nb, uc, ns), F32)
    vec = jax.ShapeDtypeStruct((nb, 1, ns), F32)
    return pl.pallas_call(
        body, name="s5_bwd",
        out_shape=(jax.ShapeDtypeStruct((l, d_s5), BF16), vec, vec, acc, acc, acc, acc,
                   jax.ShapeDtypeStruct((1, d_s5), F32)),
        grid=(nb, nt),
        in_specs=[pl.BlockSpec((t, uc), lambda b, n: (rev(n), b)),
                  pl.BlockSpec((t, uc), lambda b, n: (rev(n), b)),
                  per_block((uc, ns)), per_block((uc, ns)),
                  per_block((1, ns)), per_block((1, ns)),
                  per_block((uc, ns)), per_block((uc, ns)),
                  pl.BlockSpec((1, uc), lambda b, n: (0, b)),
                  pl.BlockSpec((None, 1, ns), lambda b, n: (rev(n), 0, b)),
                  pl.BlockSpec((None, 1, ns), lambda b, n: (rev(n), 0, b)),
                  pl.BlockSpec((t, ns), lambda b, n: (rev(n), b)),
                  pl.BlockSpec((t, ns), lambda b, n: (rev(n), b))],
        out_specs=(pl.BlockSpec((t, uc), lambda b, n: (rev(n), b)),
                   per_block((1, ns)), per_block((1, ns)),
                   per_block((uc, ns)), per_block((uc, ns)),
                   per_block((uc, ns)), per_block((uc, ns)),
                   pl.BlockSpec((1, uc), lambda b, n: (0, b))),
        scratch_shapes=[pltpu.VMEM((1, ns), F32), pltpu.VMEM((1, ns), F32)]
        + [pltpu.VMEM((t, ns), F32)] * 2,
        compiler_params=_cparams("parallel", "arbitrary"),
    )(proj, dy, bbr, bbi, a_re, a_im, cbr, cbi, d_skip, car_r, car_i, states_r, states_i)


def _s5_glu_fwd(y1, proj, off_z, wglu):
    l, d = y1.shape

    def body(y_ref, z_ref, w_ref, o_ref):
        y2 = _gelu(y_ref[...])
        y3 = y2 * _sigmoid(_bdot(y2, w_ref[...]))
        o_ref[...] = (y3 * _silu(z_ref[...])).astype(BF16)

    return pl.pallas_call(
        body, name="s5_glu_fwd",
        out_shape=jax.ShapeDtypeStruct((l, d), BF16),
        grid=(l // ROW_TILE,),
        in_specs=[pl.BlockSpec((ROW_TILE, d), lambda i: (i, 0)),
                  pl.BlockSpec((ROW_TILE, d), lambda i: (i, off_z // d)),
                  pl.BlockSpec((d, d), lambda i: (0, 0))],
        out_specs=pl.BlockSpec((ROW_TILE, d), lambda i: (i, 0)),
        compiler_params=_cparams("parallel"),
    )(y1, proj, wglu)


def _s5_glu_bwd(y1, proj, off_z, wglu, dout):
    l, d = y1.shape

    def body(y_ref, z_ref, w_ref, do_ref, dy_ref, dz_ref, dw_ref):
        y2, gelu_vjp = jax.vjp(_gelu, y_ref[...])
        z = z_ref[...]
        sz, silu_vjp = jax.vjp(_silu, z)
        y2_bf = y2.astype(BF16)
        sg = _sigmoid(_dot(y2_bf, w_ref[...]))
        dout = do_ref[...]
        dy3 = dout * sz
        dz_ref[...] = silu_vjp(dout * (y2 * sg))[0].astype(BF16)
        dgl = (dy3 * y2 * sg * (1.0 - sg)).astype(BF16)
        dy2 = dy3 * sg + _dot(dgl, w_ref[...], ((1,), (1,)))
        dy_ref[...] = gelu_vjp(dy2)[0]

        @pl.when(pl.program_id(0) == 0)
        def _():
            dw_ref[...] = jnp.zeros_like(dw_ref)

        dw_ref[...] += _dot(y2_bf, dgl, ((0,), (0,)))

    row = pl.BlockSpec((ROW_TILE, d), lambda i: (i, 0))
    full = pl.BlockSpec((d, d), lambda i: (0, 0))
    return pl.pallas_call(
        body, name="s5_glu_bwd",
        out_shape=(jax.ShapeDtypeStruct((l, d), F32), jax.ShapeDtypeStruct((l, d), BF16),
                   jax.ShapeDtypeStruct((d, d), F32)),
        grid=(l // ROW_TILE,),
        in_specs=[row, pl.BlockSpec((ROW_TILE, d), lambda i: (i, off_z // d)), full, row],
        out_specs=(row, row, full),
        compiler_params=_cparams("arbitrary"),
    )(y1, proj, wglu, dout)


def _shift_rows(x, k, back=False):
    if k == 0:
        return x
    t = x.shape[0]
    row = lax.broadcasted_iota(jnp.int32, (t, 1), 0)
    if back:
        return jnp.where(row < t - k, pltpu.roll(x, t - k, 0), 0.0)
    return jnp.where(row >= k, pltpu.roll(x, k, 0), 0.0)


def _dn_conv(x, w_ref):
    return sum(w_ref[CONV_K - 1 - k:CONV_K - k, :] * _shift_rows(x, k) for k in range(CONV_K))


def _dn_post_conv(c, j):
    y = _silu(c)
    n = y * lax.rsqrt(jnp.sum(y * y, axis=-1, keepdims=True) + EPS)
    n = n * jnp.where(j < DN_HEADS, DN_HEAD_DIM ** -0.5, 1.0)
    return jnp.where(j < 2 * DN_HEADS, n, y)


def _dn_prep_fwd(proj, off_qkv, conv_w):
    l = proj.shape[0]
    hd = DN_HEAD_DIM
    nblk = 3 * DN_HEADS

    def body(x_ref, w_ref, o_ref):
        o_ref[...] = _dn_post_conv(_dn_conv(x_ref[...], w_ref), pl.program_id(0))

    return pl.pallas_call(
        body, name="dn_prep_fwd",
        out_shape=jax.ShapeDtypeStruct((l, nblk * hd), F32),
        grid=(nblk,),
        in_specs=[pl.BlockSpec((l, hd), lambda j: (0, off_qkv // hd + j)),
                  pl.BlockSpec((CONV_K, hd), lambda j: (0, j))],
        out_specs=pl.BlockSpec((l, hd), lambda j: (0, j)),
        compiler_params=_cparams("parallel"),
    )(proj, conv_w)


def _dn_prep_bwd(proj, off_qkv, conv_w, dqkv):
    l = proj.shape[0]
    hd = DN_HEAD_DIM
    nblk = 3 * DN_HEADS

    def body(x_ref, w_ref, do_ref, dx_ref, dw_ref):
        x = x_ref[...]
        j = pl.program_id(0)
        _, vjp = jax.vjp(functools.partial(_dn_post_conv, j=j), _dn_conv(x, w_ref))
        dc = vjp(do_ref[...])[0]
        dx = sum(w_ref[CONV_K - 1 - k:CONV_K - k, :] * _shift_rows(dc, k, back=True)
                 for k in range(CONV_K))
        dx_ref[...] = dx.astype(BF16)
        for k in range(CONV_K):
            dw_ref[CONV_K - 1 - k:CONV_K - k, :] = jnp.sum(dc * _shift_rows(x, k), axis=0,
                                                           keepdims=True)

    return pl.pallas_call(
        body, name="dn_prep_bwd",
        out_shape=(jax.ShapeDtypeStruct((l, nblk * hd), BF16),
                   jax.ShapeDtypeStruct((CONV_K, nblk * hd), F32)),
        grid=(nblk,),
        in_specs=[pl.BlockSpec((l, hd), lambda j: (0, off_qkv // hd + j)),
                  pl.BlockSpec((CONV_K, hd), lambda j: (0, j)),
                  pl.BlockSpec((None, l, hd), lambda j: (j // DN_HEADS, 0, j % DN_HEADS))],
        out_specs=(pl.BlockSpec((l, hd), lambda j: (0, j)),
                   pl.BlockSpec((CONV_K, hd), lambda j: (0, j))),
        compiler_params=_cparams("parallel"),
    )(proj, conv_w, dqkv)


def _dn_gate_fn(ba, a_log_row, dt_row):
    lane = lax.broadcasted_iota(jnp.int32, ba.shape, 1)
    beta = _sigmoid(ba)
    g = -jnp.exp(a_log_row) * _softplus(ba + dt_row)
    return jnp.where(lane < DN_HEADS, beta, jnp.where(lane < 2 * DN_HEADS, g, 0.0))


def _dn_gates_fwd(proj, off_ba, a_log_row, dt_row):
    l = proj.shape[0]
    row = pl.BlockSpec((ROW_TILE, 128), lambda i: (i, off_ba // 128))
    vec = pl.BlockSpec((1, 128), lambda i: (0, 0))

    def body(ba_ref, al_ref, dt_ref, o_ref):
        o_ref[...] = _dn_gate_fn(ba_ref[...], al_ref[...], dt_ref[...])

    return pl.pallas_call(
        body, name="dn_gates_fwd",
        out_shape=jax.ShapeDtypeStruct((l, 128), F32),
        grid=(l // ROW_TILE,),
        in_specs=[row, vec, vec],
        out_specs=pl.BlockSpec((ROW_TILE, 128), lambda i: (i, 0)),
        compiler_params=_cparams("parallel"),
    )(proj, a_log_row, dt_row)


def _dn_gates_bwd(proj, off_ba, a_log_row, dt_row, dgb_heads):
    l = proj.shape[0]
    nh = dgb_heads.shape[0]
    row = pl.BlockSpec((ROW_TILE, 128), lambda i: (i, off_ba // 128))
    vec = pl.BlockSpec((1, 128), lambda i: (0, 0))

    def body(ba_ref, al_ref, dt_ref, dg_ref, dba_ref, dal_ref, ddt_ref):
        _, vjp = jax.vjp(_dn_gate_fn, ba_ref[...], al_ref[...], dt_ref[...])
        dgb = dg_ref[0]
        for h in range(1, nh):
            dgb = dgb + dg_ref[h]
        dba, dal, ddt = vjp(dgb)
        dba_ref[...] = dba.astype(BF16)

        @pl.when(pl.program_id(0) == 0)
        def _():
            dal_ref[...] = jnp.zeros_like(dal_ref)
            ddt_ref[...] = jnp.zeros_like(ddt_ref)

        dal_ref[...] += dal
        ddt_ref[...] += ddt

    return pl.pallas_call(
        body, name="dn_gates_bwd",
        out_shape=(jax.ShapeDtypeStruct((l, 128), BF16), jax.ShapeDtypeStruct((1, 128), F32),
                   jax.ShapeDtypeStruct((1, 128), F32)),
        grid=(l // ROW_TILE,),
        in_specs=[row, vec, vec, pl.BlockSpec((nh, ROW_TILE, 128), lambda i: (0, i, 0))],
        out_specs=(pl.BlockSpec((ROW_TILE, 128), lambda i: (i, 0)), vec, vec),
        compiler_params=_cparams("arbitrary"),
    )(proj, a_log_row, dt_row, dgb_heads)


def _dn_chunk_fn(states, qs, ks, vs, gb, heads):
    c = qs[0].shape[0]
    each = lambda f, *lists: [f(*args) for args in zip(*lists)]
    lane = lax.broadcasted_iota(jnp.int32, gb.shape, 1)
    ri = lax.broadcasted_iota(jnp.int32, (c, c), 0)
    ci = lax.broadcasted_iota(jnp.int32, (c, c), 1)
    causal, strict = ri >= ci, ri > ci
    eye = (ri == ci).astype(F32)
    rowi = lax.broadcasted_iota(jnp.int32, (c, 1), 0)
    nt_dims = ((1,), (1,))
    hdot = functools.partial(_dot, precision=HIGHEST)

    pick = lambda m, at: jnp.sum(jnp.where(lane == at, m, 0.0), axis=1, keepdims=True)
    gb_cum = hdot(causal.astype(F32), gb)
    beta = [pick(gb, h) for h in heads]
    gc = [pick(gb_cum, h + DN_HEADS) for h in heads]
    gc_row = each(lambda g: jnp.sum(eye * g, axis=0, keepdims=True), gc)
    decay = each(lambda g, gr: jnp.where(causal, jnp.exp(jnp.where(causal, g - gr, 0.0)), 0.0),
                 gc, gc_row)
    kk = each(lambda k: _bdot(k, k, nt_dims), ks)
    a_mat = each(lambda b, m, dc: jnp.where(strict, b * m * dc, 0.0), beta, kk, decay)

    t_inv = each(lambda a: eye - a, a_mat)
    power = a_mat
    for _ in range(int(math.log2(c)) - 1):
        power = each(lambda p: _bdot(p, p), power)
        t_inv = each(lambda t, p: t + _bdot(t, p), t_inv, power)

    egc = each(jnp.exp, gc)
    u_c = each(lambda t, v, b: _dot3(t, v * b), t_inv, vs, beta)
    w_c = each(lambda t, k, b, e: _dot3(t, k * (b * e)), t_inv, ks, beta, egc)
    qk = each(lambda q, k, dc: _bdot(q, k, nt_dims) * dc, qs, ks, decay)
    g_end = each(lambda g: jnp.sum(jnp.where(rowi == c - 1, g, 0.0), axis=0, keepdims=True), gc)
    v_new = each(lambda u, w, s: u - _bdot(w, s), u_c, w_c, states)
    o = each(lambda q, e, s, m, vn: _bdot(q * e, s) + _bdot(m, vn), qs, egc, states, qk, v_new)
    new_states = each(
        lambda s, ge, k, g, vn: s * jnp.exp(ge) + _bdot(k * jnp.exp(ge - g), vn, ((0,), (0,))),
        states, g_end, ks, gc, v_new)
    return o, new_states


def _dn_chunk_specs(order):
    hd, nh, hps = DN_HEAD_DIM, DN_HEADS, DN_HEADS_PER_STEP
    qkv = lambda part: pl.BlockSpec((CHUNK, hps * hd), lambda h, n: (order(n), part * (nh // hps) + h))
    gb = pl.BlockSpec((CHUNK, 128), lambda h, n: (order(n), 0))
    state = pl.BlockSpec((hps, None, hd, hd), lambda h, n: (h, order(n), 0, 0))
    return qkv, gb, state


def _dn_chunk_fwd(qkv, gb):
    l = qkv.shape[0]
    hd, nh, hps = DN_HEAD_DIM, DN_HEADS, DN_HEADS_PER_STEP
    n_chunks = l // CHUNK
    qkv_spec, gb_spec, state_spec = _dn_chunk_specs(lambda n: n)

    def body(q_ref, k_ref, v_ref, gb_ref, o_ref, s_ref, state):
        @pl.when(pl.program_id(1) == 0)
        def _():
            state[...] = jnp.zeros_like(state)

        cols = [slice(i * hd, (i + 1) * hd) for i in range(hps)]
        states = [state[i] for i in range(hps)]
        for i in range(hps):
            s_ref[i] = states[i]
        o, new_states = _dn_chunk_fn(
            states, [q_ref[:, cs] for cs in cols], [k_ref[:, cs] for cs in cols],
            [v_ref[:, cs] for cs in cols], gb_ref[...],
            [pl.program_id(0) * hps + i for i in range(hps)])
        for i in range(hps):
            o_ref[:, cols[i]] = o[i]
            state[i] = new_states[i]

    return pl.pallas_call(
        body, name="dn_chunk_fwd",
        out_shape=(jax.ShapeDtypeStruct((l, nh * hd), F32),
                   jax.ShapeDtypeStruct((nh, n_chunks, hd, hd), F32)),
        grid=(nh // hps, n_chunks),
        in_specs=[qkv_spec(0), qkv_spec(1), qkv_spec(2), gb_spec],
        out_specs=(pl.BlockSpec((CHUNK, hps * hd), lambda h, n: (n, h)), state_spec),
        scratch_shapes=[pltpu.VMEM((hps, hd, hd), F32)],
        compiler_params=_cparams("parallel", "arbitrary"),
    )(qkv, qkv, qkv, gb)


def _dn_chunk_bwd(qkv, gb, states, do):
    l = qkv.shape[0]
    hd, nh, hps = DN_HEAD_DIM, DN_HEADS, DN_HEADS_PER_STEP
    n_chunks = l // CHUNK
    rev = lambda n: n_chunks - 1 - n
    qkv_spec, gb_spec, state_spec = _dn_chunk_specs(rev)

    def body(q_ref, k_ref, v_ref, gb_ref, s_ref, do_ref, dqkv_ref, dgb_ref, dstate):
        @pl.when(pl.program_id(1) == 0)
        def _():
            dstate[...] = jnp.zeros_like(dstate)

        cols = [slice(i * hd, (i + 1) * hd) for i in range(hps)]
        fn = functools.partial(_dn_chunk_fn, heads=[pl.program_id(0) * hps + i for i in range(hps)])
        _, vjp = jax.vjp(fn, [s_ref[i] for i in range(hps)], [q_ref[:, cs] for cs in cols],
                         [k_ref[:, cs] for cs in cols], [v_ref[:, cs] for cs in cols], gb_ref[...])
        ds, dq, dk, dv, dgb = vjp(([do_ref[:, cs] for cs in cols], [dstate[i] for i in range(hps)]))
        for i in range(hps):
            dstate[i] = ds[i]
            dqkv_ref[0, :, cols[i]] = dq[i]
            dqkv_ref[1, :, cols[i]] = dk[i]
            dqkv_ref[2, :, cols[i]] = dv[i]
        dgb_ref[...] = dgb

    head_out = pl.BlockSpec((CHUNK, hps * hd), lambda h, n: (rev(n), h))
    return pl.pallas_call(
        body, name="dn_chunk_bwd",
        out_shape=(jax.ShapeDtypeStruct((3, l, nh * hd), F32),
                   jax.ShapeDtypeStruct((nh // hps, l, 128), F32)),
        grid=(nh // hps, n_chunks),
        in_specs=[qkv_spec(0), qkv_spec(1), qkv_spec(2), gb_spec, state_spec, head_out],
        out_specs=(pl.BlockSpec((3, CHUNK, hps * hd), lambda h, n: (0, rev(n), h)),
                   pl.BlockSpec((None, CHUNK, 128), lambda h, n: (h, rev(n), 0))),
        scratch_shapes=[pltpu.VMEM((hps, hd, hd), F32)],
        compiler_params=_cparams("parallel", "arbitrary"),
    )(qkv, qkv, qkv, gb, states, do)


def _dn_out_fn(o, z, w):
    return _rmsnorm(o, w) * _silu(z)


def _dn_out_fwd(o, proj, off_z, w):
    l, d = o.shape
    hd = DN_HEAD_DIM
    blk = lambda off: pl.BlockSpec((ROW_TILE, hd), lambda i, h: (i, off // hd + h))

    def body(o_ref, z_ref, w_ref, out_ref):
        out_ref[...] = _dn_out_fn(o_ref[...], z_ref[...], w_ref[...]).astype(BF16)

    return pl.pallas_call(
        body, name="dn_out_fwd",
        out_shape=jax.ShapeDtypeStruct((l, d), BF16),
        grid=(l // ROW_TILE, d // hd),
        in_specs=[blk(0), blk(off_z), pl.BlockSpec((1, hd), lambda i, h: (0, 0))],
        out_specs=blk(0),
        compiler_params=_cparams("parallel", "parallel"),
    )(o, proj, w)


def _dn_out_bwd(o, proj, off_z, w, dout):
    l, d = o.shape
    hd = DN_HEAD_DIM
    blk = lambda off: pl.BlockSpec((ROW_TILE, hd), lambda i, h: (i, off // hd + h))
    vec = pl.BlockSpec((1, hd), lambda i, h: (0, 0))

    def body(o_ref, z_ref, w_ref, dout_ref, do_ref, dz_ref, dw_ref):
        _, vjp = jax.vjp(_dn_out_fn, o_ref[...], z_ref[...], w_ref[...])
        do, dz, dw = vjp(dout_ref[...])
        do_ref[...] = do
        dz_ref[...] = dz.astype(BF16)

        @pl.when((pl.program_id(0) == 0) & (pl.program_id(1) == 0))
        def _():
            dw_ref[...] = jnp.zeros_like(dw_ref)

        dw_ref[...] += dw

    return pl.pallas_call(
        body, name="dn_out_bwd",
        out_shape=(jax.ShapeDtypeStruct((l, d), F32), jax.ShapeDtypeStruct((l, d), BF16),
                   jax.ShapeDtypeStruct((1, hd), F32)),
        grid=(l // ROW_TILE, d // hd),
        in_specs=[blk(0), blk(off_z), vec, blk(0)],
        out_specs=(blk(0), blk(0), vec),
        compiler_params=_cparams("arbitrary", "arbitrary"),
    )(o, proj, w, dout)


def _tile_2d(rows, cols, budget_bytes=1 << 20):
    for tr in (rows, 4096, 2048, 1024, 512, 256, 128, 64, 32, 16):
        if tr <= rows and rows % tr == 0 and tr * cols * 4 <= budget_bytes:
            return tr, cols
    for tc in (2048, 1024, 512, 256, 128):
        if cols % tc == 0 and rows * tc * 4 <= 2 * budget_bytes:
            return rows, tc
    raise ValueError((rows, cols))


def _adamw_update(g, w_ref, m_ref, v_ref, go_ref, d_ref, mo_ref, vo_ref):
    c1 = 1.0 / (1.0 - ADAM_B1 ** ADAM_STEP)
    c2 = 1.0 / (1.0 - ADAM_B2 ** ADAM_STEP)
    m_new = ADAM_B1 * m_ref[...] + (1.0 - ADAM_B1) * g
    v_new = ADAM_B2 * v_ref[...] + (1.0 - ADAM_B2) * (g * g)
    go_ref[...] = g
    mo_ref[...] = m_new
    vo_ref[...] = v_new
    d_ref[...] = -ADAM_LR * ((m_new * c1) / (jnp.sqrt(v_new * c2) + ADAM_EPS) + ADAM_WD * w_ref[...])


def _adamw(w, m, v, gslots, name):
    rows, cols = w.shape
    ns = gslots.shape[0]
    tr, tc = _tile_2d(rows, cols)

    def body(w_ref, m_ref, v_ref, g_ref, go_ref, d_ref, mo_ref, vo_ref):
        g = g_ref[0].astype(F32)
        for s in range(1, ns):
            g = g + g_ref[s].astype(F32)
        _adamw_update(g, w_ref, m_ref, v_ref, go_ref, d_ref, mo_ref, vo_ref)

    blk = pl.BlockSpec((tr, tc), lambda i, j: (i, j))
    o = jax.ShapeDtypeStruct((rows, cols), F32)
    return pl.pallas_call(
        body, name=name, out_shape=(o, o, o, o),
        grid=(rows // tr, cols // tc),
        in_specs=[blk, blk, blk, pl.BlockSpec((ns, tr, tc), lambda i, j: (0, i, j))],
        out_specs=(blk, blk, blk, blk),
        compiler_params=_cparams("parallel", "parallel"),
    )(w, m, v, gslots)


def _slot_sum(gslots, name):
    ns, rows, cols = gslots.shape
    tr, tc = _tile_2d(rows, cols)

    def body(g_ref, o_ref):
        g = g_ref[0]
        for s in range(1, ns):
            g = g + g_ref[s]
        o_ref[...] = g

    return pl.pallas_call(
        body, name=name, out_shape=jax.ShapeDtypeStruct((rows, cols), F32),
        grid=(rows // tr, cols // tc),
        in_specs=[pl.BlockSpec((ns, tr, tc), lambda i, j: (0, i, j))],
        out_specs=pl.BlockSpec((tr, tc), lambda i, j: (i, j)),
        compiler_params=_cparams("parallel", "parallel"),
    )(gslots)


HBM_SPEC = pl.BlockSpec(memory_space=pl.ANY)


def _all_gather(arrs, name, relayed=()):
    n = len(arrs)
    n_sems = 13

    def body(*refs):
        ins, outs = refs[:n], refs[n:2 * n]
        send_sems, recv_sems, local_sems = refs[2 * n:]
        x, y, c = lax.axis_index("x"), lax.axis_index("y"), lax.axis_index("c")
        me, sibling = (x, y, c), (x, y, 1 - c)
        chips = [(1 - x, y), (x, 1 - y), (1 - x, 1 - y)]
        index = lambda px, py, pc: 4 * px + 2 * py + pc

        def copy(a, k, block, to, src=None, cols=None):
            dst = outs[a].at[index(*block)]
            src = dst if src is None else src
            if cols is not None:
                dst, src = dst.at[:, cols], src.at[:, cols]
            return pltpu.make_async_remote_copy(
                src_ref=src, dst_ref=dst, send_sem=send_sems.at[a, k], recv_sem=recv_sems.at[a, k],
                device_id=to, device_id_type=MESH)

        mine = [pltpu.make_async_copy(ins[a], outs[a].at[index(*me)], local_sems.at[a])
                for a in range(n)]
        for cp in mine:
            cp.start()
        sends = []

        def start(cp):
            cp.start()
            sends.append(cp)

        halves = {a: (pl.ds(0, arrs[a].shape[1] // 2), pl.ds(arrs[a].shape[1] // 2, arrs[a].shape[1] // 2))
                  for a in relayed}
        near_x, near_y, far = [(*chip, c) for chip in chips]
        for a in range(n):
            start(copy(a, 0, me, sibling, src=ins[a]))
            if a in relayed:
                left, right = halves[a]
                for k, to, cols in ((1, near_x, left), (3, near_y, right), (2, near_x, right), (4, near_y, left)):
                    start(copy(a, k, me, to, src=ins[a], cols=cols))
            else:
                for j, chip in enumerate(chips):
                    start(copy(a, 1 + j, me, (*chip, c), src=ins[a]))
        for a in relayed:
            left, right = halves[a]
            for k, block, cols, onward, to_sibling in (
                    (1, near_x, left, (5, near_y), 7), (3, near_y, right, (6, near_x), 10),
                    (2, near_x, right, None, 8), (4, near_y, left, None, 9),
                    (5, far, left, None, 11), (6, far, right, None, 12)):
                copy(a, k, block, me, cols=cols).wait_recv()
                if onward is not None:
                    start(copy(a, onward[0], block, onward[1], cols=cols))
                start(copy(a, to_sibling, block, sibling, cols=cols))
        for j, chip in enumerate(chips):
            for a in range(n):
                if a not in relayed:
                    copy(a, 1 + j, (*chip, c), me).wait_recv()
                    start(copy(a, 4 + j, (*chip, c), sibling))
        for a in range(n):
            copy(a, 0, sibling, me).wait_recv()
            if a in relayed:
                left, right = halves[a]
                for k, chip, cols in ((7, chips[0], left), (8, chips[0], right), (9, chips[1], left),
                                      (10, chips[1], right), (11, chips[2], left), (12, chips[2], right)):
                    copy(a, k, (*chip, 1 - c), me, cols=cols).wait_recv()
            else:
                for j, chip in enumerate(chips):
                    copy(a, 4 + j, (*chip, 1 - c), me).wait_recv()
        for cp in sends:
            cp.wait_send()
        for cp in mine:
            cp.wait()

    return pl.pallas_call(
        body, name=name,
        out_shape=[jax.ShapeDtypeStruct((N_DEV,) + a.shape, a.dtype) for a in arrs],
        in_specs=[HBM_SPEC] * n, out_specs=[HBM_SPEC] * n,
        scratch_shapes=[pltpu.SemaphoreType.DMA((n, n_sems)), pltpu.SemaphoreType.DMA((n, n_sems)),
                        pltpu.SemaphoreType.DMA((n,))],
    )(*arrs)


def _sibling_swap(arrs, name):
    n = len(arrs)

    def body(*refs):
        ins, outs = refs[:n], refs[n:2 * n]
        send_sems, recv_sems = refs[2 * n:]
        x, y, c = lax.axis_index("x"), lax.axis_index("y"), lax.axis_index("c")
        copies = [pltpu.make_async_remote_copy(
            src_ref=ins[a].at[:, 1 - c], dst_ref=outs[a],
            send_sem=send_sems.at[a], recv_sem=recv_sems.at[a],
            device_id=(x, y, 1 - c), device_id_type=MESH) for a in range(n)]
        for cp in copies:
            cp.start()
        for cp in copies:
            cp.wait()

    return pl.pallas_call(
        body, name=name,
        out_shape=[jax.ShapeDtypeStruct(a.shape[:1] + a.shape[2:], a.dtype) for a in arrs],
        in_specs=[HBM_SPEC] * n, out_specs=[HBM_SPEC] * n,
        scratch_shapes=[pltpu.SemaphoreType.DMA((n,)), pltpu.SemaphoreType.DMA((n,))],
    )(*arrs)


def _pair_sum(mine, theirs, core, name):
    chips, _, rows, cols = mine.shape
    tr, tc = _tile_2d(rows, cols, budget_bytes=2 << 20)

    def body(core_ref, a_ref, b_ref, o_ref):
        o_ref[...] = (a_ref[...].astype(F32) + b_ref[...].astype(F32)).astype(o_ref.dtype)

    slab = pl.BlockSpec((None, tr, tc), lambda ch, i, j, core_ref: (ch, i, j))
    return pl.pallas_call(
        body, name=name, out_shape=jax.ShapeDtypeStruct((chips, rows, cols), mine.dtype),
        grid_spec=pltpu.PrefetchScalarGridSpec(
            num_scalar_prefetch=1, grid=(chips, rows // tr, cols // tc),
            in_specs=[pl.BlockSpec((None, None, tr, tc),
                                   lambda ch, i, j, core_ref: (ch, core_ref[0], i, j)), slab],
            out_specs=slab),
        compiler_params=_cparams("parallel", "parallel", "parallel"),
    )(core, mine, theirs)


HBM_ONLY = pl.BlockSpec(memory_space=pltpu.HBM)
SEM_SPEC = pl.BlockSpec(memory_space=pltpu.SEMAPHORE)
SPLIT_COPY_EFFECT = pltpu.SideEffectType.DATAFLOW_SIDE_EFFECTING


def _flip(v, bit):
    return 1 - v if bit else v


def _chip_slices_plan(n):
    def plan():
        x, y, c = lax.axis_index("x"), lax.axis_index("y"), lax.axis_index("c")
        copies = []
        for k in range(1, 4):
            px, py = _flip(x, k & 2), _flip(y, k & 1)
            copies += [(a, 2 * px + py, 2 * x + y, (px, py, c)) for a in range(n)]
        return copies
    return plan, 3 * n


def _gather_plan(n):
    def plan():
        x, y, c = lax.axis_index("x"), lax.axis_index("y"), lax.axis_index("c")
        copies = []
        for k in range(1, N_DEV):
            peer = (_flip(x, k & 4), _flip(y, k & 2), _flip(c, k & 1))
            copies += [(a, None, 4 * x + 2 * y + c, peer) for a in range(n)]
        return copies
    return plan, 7 * n


def _planned_copies(plan, srcs, lands, send_sems, recv_sems):
    return [pltpu.make_async_remote_copy(
        src_ref=srcs[a] if src_at is None else srcs[a].at[src_at], dst_ref=lands[a].at[land_at],
        send_sem=send_sems[i], recv_sem=recv_sems[i], device_id=peer, device_id_type=MESH)
        for i, (a, src_at, land_at, peer) in enumerate(plan())]


def _split_exchange_start(plan_and_count, arrs, land_shapes, name, after=None):
    plan, n_sems = plan_and_count
    n = len(arrs)

    n_in = 2 * n + (after is not None)

    def body(*refs):
        srcs, lands = refs[:n], refs[n:2 * n]
        send_sems, recv_sems = refs[n_in:n_in + n_sems], refs[n_in + n_sems:n_in + 2 * n_sems]
        token = refs[-1]
        for copy in _planned_copies(plan, srcs, lands, send_sems, recv_sems):
            copy.start()
        token[...] = jnp.zeros_like(token)

    hbm = lambda a: pltpu.HBM(a.shape, a.dtype)
    operands = [pltpu.with_memory_space_constraint(a, pltpu.HBM) for a in arrs]
    operands += [pltpu.with_memory_space_constraint(lax.empty(shape, a.dtype), pltpu.HBM)
                 for a, shape in zip(arrs, land_shapes)]
    out = pl.pallas_call(
        body, name=name,
        out_shape=(*[pltpu.SemaphoreType.DMA(())] * (2 * n_sems),
                   *[hbm(a) for a in operands],
                   jax.ShapeDtypeStruct((8, 128), F32)),
        in_specs=[HBM_ONLY] * (2 * n) + [pl.BlockSpec(memory_space=pl.ANY)] * (after is not None),
        out_specs=(*[SEM_SPEC] * (2 * n_sems), *[HBM_ONLY] * (2 * n),
                   pl.BlockSpec(memory_space=pltpu.VMEM)),
        input_output_aliases={i: 2 * n_sems + i for i in range(2 * n)},
        compiler_params=pltpu.CompilerParams(has_side_effects=SPLIT_COPY_EFFECT),
    )(*operands, *([after] if after is not None else []))
    sems, rest = list(out[:2 * n_sems]), out[2 * n_sems:]
    return sems, list(rest[:n]), list(rest[n:2 * n]), rest[-1]


def _split_exchange_wait(plan_and_count, sems, srcs, lands, after, name):
    plan, n_sems = plan_and_count
    n = len(srcs)

    def body(*refs):
        src_refs, land_refs = refs[:n], refs[n:2 * n]
        send_sems, recv_sems = refs[2 * n:2 * n + n_sems], refs[2 * n + n_sems:2 * n + 2 * n_sems]
        for copy in _planned_copies(plan, src_refs, land_refs, send_sems, recv_sems):
            copy.wait_send()
            copy.wait_recv()

    hbm = lambda a: pltpu.HBM(a.shape, a.dtype)
    out = pl.pallas_call(
        body, name=name,
        out_shape=(*[hbm(a) for a in srcs], *[hbm(a) for a in lands]),
        in_specs=[HBM_ONLY] * (2 * n) + [SEM_SPEC] * (2 * n_sems) + [pl.BlockSpec(memory_space=pl.ANY)],
        out_specs=tuple([HBM_ONLY] * (2 * n)),
        input_output_aliases={i: i for i in range(2 * n)},
        compiler_params=pltpu.CompilerParams(has_side_effects=SPLIT_COPY_EFFECT),
    )(*srcs, *lands, *sems, after)
    return list(out[n:])


def _adamw_exchanged(w, m, v, own, landed, chip, name):
    rows, cols = w.shape
    tr, tc = _tile_2d(rows, cols)

    def body(chip_ref, w_ref, m_ref, v_ref, own_ref, l1_ref, l2_ref, l3_ref, go_ref, d_ref, mo_ref, vo_ref):
        g = own_ref[...].astype(F32)
        for ref in (l1_ref, l2_ref, l3_ref):
            g = g + ref[...].astype(F32)
        _adamw_update(g, w_ref, m_ref, v_ref, go_ref, d_ref, mo_ref, vo_ref)

    blk = pl.BlockSpec((tr, tc), lambda i, j, chip_ref: (i, j))
    slot = lambda k: pl.BlockSpec((None, tr, tc), lambda i, j, chip_ref: (chip_ref[0] ^ k, i, j))
    o = jax.ShapeDtypeStruct((rows, cols), F32)
    return pl.pallas_call(
        body, name=name, out_shape=(o, o, o, o),
        grid_spec=pltpu.PrefetchScalarGridSpec(
            num_scalar_prefetch=1, grid=(rows // tr, cols // tc),
            in_specs=[blk, blk, blk, slot(0), slot(1), slot(2), slot(3)],
            out_specs=(blk, blk, blk, blk)),
        compiler_params=_cparams("parallel", "parallel"),
    )(chip, w, m, v, own, landed, landed, landed)


def _block_diag(t):
    nb, gpb, r, c = t.shape
    eye = jnp.eye(gpb, dtype=t.dtype)
    return jnp.einsum("ngrc,gh->ngrhc", t, eye).reshape(nb, gpb * r, gpb * c)


def _diag_blocks(t, r, c):
    nb = t.shape[0]
    gpb = t.shape[1] // r
    t = t.reshape(nb, gpb, r, gpb, c)
    return jnp.einsum("ngrhc,gh->ngrc", t, jnp.eye(gpb, dtype=t.dtype))


def _pack_rows(parts):
    flat = jnp.concatenate([p.reshape(-1).astype(F32) for p in parts])
    pad = (-flat.shape[0]) % (256 * 128)
    return jnp.pad(flat, (0, pad)).reshape(-1, 128)


def _unpack_rows(packed, shapes):
    flat = packed.reshape(-1)
    out, at = [], 0
    for shape in shapes:
        size = math.prod(shape)
        out.append(flat[at:at + size].reshape(shape))
        at += size
    return out


def kernel(x, ln_w, w_in, s5_lam_re, s5_lam_im, s5_log_step, s5_b_re, s5_b_im, s5_c_re, s5_c_im, s5_d, s5_w_glu, s5_w_up, dn_conv_w, dn_a_log, dn_dt_bias, dn_norm_w, dn_w_up, w_out, final_norm_w, loss_target, m_ln_w, m_w_in, m_s5_lam_re, m_s5_lam_im, m_s5_log_step, m_s5_b_re, m_s5_b_im, m_s5_c_re, m_s5_c_im, m_s5_d, m_s5_w_glu, m_s5_w_up, m_dn_conv_w, m_dn_a_log, m_dn_dt_bias, m_dn_norm_w, m_dn_w_up, m_w_out, m_final_norm_w, v_ln_w, v_w_in, v_s5_lam_re, v_s5_lam_im, v_s5_log_step, v_s5_b_re, v_s5_b_im, v_s5_c_re, v_s5_c_im, v_s5_d, v_s5_w_glu, v_s5_w_up, v_dn_conv_w, v_dn_a_log, v_dn_dt_bias, v_dn_norm_w, v_dn_w_up, v_w_out, v_final_norm_w):
    weights = dict(ln_w=ln_w, w_in=w_in, s5_lam_re=s5_lam_re, s5_lam_im=s5_lam_im,
                   s5_log_step=s5_log_step, s5_b_re=s5_b_re, s5_b_im=s5_b_im, s5_c_re=s5_c_re,
                   s5_c_im=s5_c_im, s5_d=s5_d, s5_w_glu=s5_w_glu, s5_w_up=s5_w_up,
                   dn_conv_w=dn_conv_w, dn_a_log=dn_a_log, dn_dt_bias=dn_dt_bias,
                   dn_norm_w=dn_norm_w, dn_w_up=dn_w_up, w_out=w_out, final_norm_w=final_norm_w)
    mom_m = dict(ln_w=m_ln_w, w_in=m_w_in, s5_lam_re=m_s5_lam_re, s5_lam_im=m_s5_lam_im,
                 s5_log_step=m_s5_log_step, s5_b_re=m_s5_b_re, s5_b_im=m_s5_b_im,
                 s5_c_re=m_s5_c_re, s5_c_im=m_s5_c_im, s5_d=m_s5_d, s5_w_glu=m_s5_w_glu,
                 s5_w_up=m_s5_w_up, dn_conv_w=m_dn_conv_w, dn_a_log=m_dn_a_log,
                 dn_dt_bias=m_dn_dt_bias, dn_norm_w=m_dn_norm_w, dn_w_up=m_dn_w_up,
                 w_out=m_w_out, final_norm_w=m_final_norm_w)
    mom_v = dict(ln_w=v_ln_w, w_in=v_w_in, s5_lam_re=v_s5_lam_re, s5_lam_im=v_s5_lam_im,
                 s5_log_step=v_s5_log_step, s5_b_re=v_s5_b_re, s5_b_im=v_s5_b_im,
                 s5_c_re=v_s5_c_re, s5_c_im=v_s5_c_im, s5_d=v_s5_d, s5_w_glu=v_s5_w_glu,
                 s5_w_up=v_s5_w_up, dn_conv_w=v_dn_conv_w, dn_a_log=v_dn_a_log,
                 dn_dt_bias=v_dn_dt_bias, dn_norm_w=v_dn_norm_w, dn_w_up=v_dn_w_up,
                 w_out=v_w_out, final_norm_w=v_final_norm_w)
    names = list(weights)

    l, d = x.shape[1], x.shape[2]
    d_s5 = d // 2
    groups = d_s5 // S5_GROUP
    nb = groups // S5_GPB
    d_dn = DN_HEADS * DN_HEAD_DIM
    w_in_cols = w_in.shape[2]
    d_in = N_DEV * w_in_cols
    off_ba_src = 2 * d_s5 + 4 * d_dn
    off_u, off_zs, off_qkv, off_zd = 0, d_s5, 2 * d_s5, 2 * d_s5 + 3 * d_dn
    off_ba = off_zd + d_dn
    n_main = off_ba + BA_PAD
    off_gs, off_gd = 0, d
    x2d, tgt2d = x[0], loss_target[0]
    my_index = 4 * lax.axis_index("x") + 2 * lax.axis_index("y") + lax.axis_index("c")

    g_win, g_conv = _all_gather([jnp.transpose(w_in[0]).astype(BF16), dn_conv_w[0]], name="gather_weights",
                                relayed=(0,))
    late_plan = _gather_plan(4)
    late_shards = [s5_w_glu[0].astype(BF16), s5_w_up[0].astype(BF16), dn_w_up[0].astype(BF16),
                   w_out[0].astype(BF16)]
    late_sems, late_shards, late_lands, late_token = _split_exchange_start(
        late_plan, late_shards, [(N_DEV,) + s.shape for s in late_shards], name="gather_late_start",
        after=g_conv)
    ba_end = off_ba_src + 2 * DN_HEADS
    w_full_t = g_win.reshape(d_in, d)
    w_gates_t = w_full_t[ba_end:]
    conv_full = jnp.transpose(g_conv, (1, 0, 2)).reshape(CONV_K, 3 * d_dn)

    lam_re, lam_im = s5_lam_re[0], s5_lam_im[0]
    log_step = s5_log_step[0].reshape(groups, 1)
    b_re = s5_b_re[0].reshape(groups * S5_STATE, S5_GROUP)
    b_im = s5_b_im[0].reshape(groups * S5_STATE, S5_GROUP)
    abar_re, abar_im, f_re, f_im = _s5_disc_fwd(lam_re, lam_im, log_step)
    f_re_col, f_im_col = f_re.reshape(-1, 1), f_im.reshape(-1, 1)
    bb_re, bb_im = _s5_bbar_fwd(f_re_col, f_im_col, b_re, b_im)

    def bb_blocks(t):
        t = t.reshape(nb, S5_GPB, S5_STATE, S5_GROUP).transpose(0, 1, 3, 2)
        return _block_diag(t).astype(BF16)

    def c_blocks(t):
        return _block_diag(t.reshape(nb, S5_GPB, S5_GROUP, S5_STATE)).astype(BF16)

    bbr, bbi = bb_blocks(bb_re), bb_blocks(bb_im)
    cbr, cbi = c_blocks(s5_c_re[0]), c_blocks(s5_c_im[0])
    ctr, cti = jnp.transpose(cbr, (0, 2, 1)), jnp.transpose(cbi, (0, 2, 1))
    a_re = abar_re.reshape(nb, 1, S5_GPB * S5_STATE)
    a_im = abar_im.reshape(nb, 1, S5_GPB * S5_STATE)

    h = _rms_fwd(x2d, ln_w)
    proj = _mm(h, w_full_t, tb=True, b_rows=n_main, tm=1024, tn=512, after=late_token, name="proj")
    proj_gates = _mm(h, w_gates_t, tb=True, tm=1024, tn=1024, name="proj_gates")
    y1, car_r, car_i, states_r, states_i = _s5_fwd(proj, bbr, bbi, a_re, a_im, ctr, cti, s5_d, d_s5)
    a_log_row = jnp.pad(dn_a_log, ((0, 0), (DN_HEADS, 128 - 2 * DN_HEADS)))
    dt_row = jnp.pad(dn_dt_bias, ((0, 0), (DN_HEADS, 128 - 2 * DN_HEADS)))
    qkv = _dn_prep_fwd(proj, off_qkv, conv_full)
    gb = _dn_gates_fwd(proj, off_ba, a_log_row, dt_row)
    o_dn, states = _dn_chunk_fwd(qkv, gb)

    late_lands = _split_exchange_wait(late_plan, late_sems, late_shards, late_lands, o_dn,
                                      name="gather_late_wait")
    g_glu, g_sup, g_dup, g_wout = [
        lax.dynamic_update_slice(land, shard[None], (my_index, 0, 0))
        for land, shard in zip(late_lands, late_shards)]
    wglu_full = g_glu.reshape(d_s5, d_s5)
    wsup_full = jnp.transpose(g_sup, (1, 0, 2)).reshape(d_s5, d)
    wdup_full = jnp.transpose(g_dup, (1, 0, 2)).reshape(d_dn, d)
    wout_full = g_wout.reshape(d, d)

    out_s = _s5_glu_fwd(y1, proj, off_zs, wglu_full)
    y_s = _mm(out_s, wsup_full, name="s5_up")
    out_d = _dn_out_fwd(o_dn, proj, off_zd, dn_norm_w)
    y_d = _mm(out_d, wdup_full, name="dn_up")

    mixed = _merge_fwd(proj_gates, off_gs, off_gd, y_s, y_d)
    branch = _mm(mixed, wout_full, name="w_out")
    dx2, dx2_bf, loss_dev, d_final_w = _final(x2d, branch, final_norm_w.reshape(1, d), tgt2d)

    g_wout_full = _mm(mixed, dx2_bf, ta=True, out_dtype=BF16, name="grad_w_out")
    dmixed = _mm(dx2_bf, wout_full, tb=True, name="d_mixed")
    dgs, dgd, dys, dyd = _merge_bwd(proj_gates, off_gs, off_gd, y_s, y_d, dmixed)

    g_dup_full = _mm(out_d, dyd, ta=True, out_dtype=BF16, name="grad_dn_up")
    dout_d = _mm(dyd, wdup_full, tb=True, name="d_out_d")
    do_dn, dzd, d_norm_w = _dn_out_bwd(o_dn, proj, off_zd, dn_norm_w, dout_d)
    dqkv, dgb_heads = _dn_chunk_bwd(qkv, gb, states, do_dn)
    dba, d_a_log_row, d_dt_row = _dn_gates_bwd(proj, off_ba, a_log_row, dt_row, dgb_heads)
    dqkv_pre, d_conv_full = _dn_prep_bwd(proj, off_qkv, conv_full, dqkv)

    g_sup_full = _mm(out_s, dys, ta=True, out_dtype=BF16, name="grad_s5_up")
    dout_s = _mm(dys, wsup_full, tb=True, name="d_out_s")
    dy1, dzs, g_glu_full = _s5_glu_bwd(y1, proj, off_zs, wglu_full, dout_s)

    def by_dest(t, axis=0):
        if axis == 1:
            return t.reshape(t.shape[0], 4, 2, t.shape[1] // N_DEV).transpose(1, 2, 0, 3)
        return t.reshape(4, 2, t.shape[0] // N_DEV, t.shape[1])

    core = lax.axis_index("c").astype(jnp.int32).reshape(1)
    chip = (2 * lax.axis_index("x") + lax.axis_index("y")).astype(jnp.int32).reshape(1)

    def chip_sums_of(which, parts, tag):
        from_sibling = _sibling_swap(parts, name="swap_grads_" + tag)
        return [_pair_sum(p, got, core, name="pair_sum_" + nm)
                for nm, p, got in zip(which, parts, from_sibling)]

    early = ["s5_w_glu", "s5_w_up", "dn_w_up", "w_out"]
    sums_a = chip_sums_of(early, [by_dest(g_glu_full.astype(BF16)), by_dest(g_sup_full, 1),
                                  by_dest(g_dup_full, 1), by_dest(g_wout_full)], "a")
    plan_a = _chip_slices_plan(len(sums_a))
    sems_a, src_a, land_a, token_a = _split_exchange_start(
        plan_a, sums_a, [t.shape for t in sums_a], name="exchange_start_a")

    (du, d_a_re, d_a_im, d_bbr, d_bbi, d_cbr, d_cbi, d_s5_d) = _s5_bwd(
        proj, dy1, bbr, bbi, a_re, a_im, cbr, cbi, s5_d + token_a[:1, :1], car_r, car_i,
        states_r, states_i)

    def from_bb_blocks(t):
        t = _diag_blocks(t, S5_GROUP, S5_STATE).transpose(0, 1, 3, 2)
        return t.reshape(groups * S5_STATE, S5_GROUP)

    d_f_re, d_f_im, d_b_re, d_b_im = _s5_bbar_bwd(f_re_col, f_im_col, b_re, b_im,
                                                 from_bb_blocks(d_bbr), from_bb_blocks(d_bbi))
    d_lam_re, d_lam_im, d_log_step = _s5_disc_bwd(
        lam_re, lam_im, log_step, d_a_re.reshape(groups, S5_STATE), d_a_im.reshape(groups, S5_STATE),
        d_f_re.reshape(groups, S5_STATE), d_f_im.reshape(groups, S5_STATE))
    d_c_re = _diag_blocks(d_cbr, S5_GROUP, S5_STATE).reshape(groups, S5_GROUP, S5_STATE)
    d_c_im = _diag_blocks(d_cbi, S5_GROUP, S5_STATE).reshape(groups, S5_GROUP, S5_STATE)

    dproj = jnp.concatenate([du, dzs, dqkv_pre, dzd, jnp.pad(dba, ((0, 0), (0, BA_PAD - 128)))], axis=1)
    dproj_gates = jnp.concatenate([dgs, dgd], axis=1)
    g_main_t = _mm(dproj, h, ta=True, out_dtype=BF16, tm=512, tn=d, name="grad_w_in")
    g_gates_t = _mm(dproj_gates, h, ta=True, out_dtype=BF16, tm=1024, tn=d, name="grad_w_in_gates")
    g_win_full_t = jnp.concatenate([g_main_t[:ba_end], g_gates_t], axis=0)
    sums_b = chip_sums_of(["w_in"], [by_dest(g_win_full_t)], "b")
    plan_b = _chip_slices_plan(1)
    sems_b, src_b, land_b, token_b = _split_exchange_start(
        plan_b, sums_b, [t.shape for t in sums_b], name="exchange_start_b")
    dh_main = _mm(dproj, w_full_t, b_rows=n_main, tm=1024, tn=1024, tk=n_main // 4, after=token_b,
                  name="d_h_main")
    dh = _mm(dproj_gates, w_gates_t, tm=1024, tn=1024, tk=2048, addend=dh_main, name="d_h")
    grad_x, d_ln_w = _rms_bwd(x2d, ln_w, dh, dx2)
    big = ["w_in"] + early
    results = {}

    small = [nm for nm in names if nm not in big]
    small_grads = dict(
        ln_w=d_ln_w, s5_lam_re=d_lam_re, s5_lam_im=d_lam_im, s5_log_step=d_log_step,
        s5_b_re=d_b_re, s5_b_im=d_b_im, s5_c_re=d_c_re, s5_c_im=d_c_im, s5_d=d_s5_d,
        dn_conv_w=d_conv_full, dn_a_log=d_a_log_row[:, DN_HEADS:2 * DN_HEADS],
        dn_dt_bias=d_dt_row[:, DN_HEADS:2 * DN_HEADS], dn_norm_w=d_norm_w, final_norm_w=d_final_w)
    (all_small,) = _all_gather([_pack_rows([small_grads[nm] for nm in small])], name="gather_small_grads")
    summed = _slot_sum(all_small, name="sum_small_grads")
    full_shapes = [(CONV_K, 3 * d_dn) if nm == "dn_conv_w" else weights[nm].shape for nm in small]
    g_small = dict(zip(small, _unpack_rows(summed, full_shapes)))
    conv_cols = dn_conv_w.shape[2]
    g_small["dn_conv_w"] = lax.dynamic_slice_in_dim(
        g_small["dn_conv_w"], my_index * conv_cols, conv_cols, axis=1).reshape(dn_conv_w.shape)
    packed = [_pack_rows([t[nm] for nm in small]) for t in (weights, mom_m, mom_v, g_small)]
    small_out = _adamw(packed[0], packed[1], packed[2], packed[3][None], name="adamw_small")
    small_shapes = [weights[nm].shape for nm in small]
    for kind, packed_out in enumerate(small_out):
        for nm, val in zip(small, _unpack_rows(packed_out, small_shapes)):
            results.setdefault(nm, [None] * 4)[kind] = val

    land_a = _split_exchange_wait(plan_a, sems_a, src_a, land_a, small_out[0], name="exchange_wait_a")
    for nm, own, landed in zip(early, src_a, land_a):
        results[nm] = _adamw_exchanged(weights[nm][0], mom_m[nm][0], mom_v[nm][0], own, landed, chip,
                                       name="adamw_" + nm)
    (land_b,) = _split_exchange_wait(plan_b, sems_b, src_b, land_b, results[early[-1]][0],
                                     name="exchange_wait_b")
    res = _adamw_exchanged(jnp.transpose(w_in[0]), jnp.transpose(m_w_in[0]), jnp.transpose(v_w_in[0]),
                           src_b[0], land_b, chip, name="adamw_w_in")
    results["w_in"] = [jnp.transpose(t) for t in res]

    loss = lax.psum(loss_dev[0, 0], ("x", "y", "c"))
    outs = [loss, grad_x[None]]
    for kind in range(4):
        outs += [results[nm][kind].reshape(weights[nm].shape) for nm in names]
    return tuple(outs)
```

```python
import functools
import math

import jax
import jax.numpy as jnp
from jax import lax
from jax.experimental import pallas as pl
from jax.experimental.pallas import tpu as pltpu

F32 = jnp.float32
BF16 = jnp.bfloat16
HIGHEST = lax.Precision.HIGHEST
MESH = pl.DeviceIdType.MESH
N_DEV = 8

EPS = 1e-6
S5_GROUP = 16
S5_STATE = 64
S5_GPB = 8
S5_T = 256
DN_HEADS = 8
DN_HEAD_DIM = 128
CHUNK = 64
DN_HEADS_PER_STEP = 8
CONV_K = 4
BA_PAD = 512

ADAM_LR = 0.001
ADAM_B1 = 0.9
ADAM_B2 = 0.999
ADAM_EPS = 1e-08
ADAM_WD = 0.01
ADAM_STEP = 10

VMEM_LIMIT_BYTES = 48 * 1024 * 1024
ROW_TILE = 256


def _cparams(*sem):
    return pltpu.CompilerParams(dimension_semantics=sem if sem else None,
                                vmem_limit_bytes=VMEM_LIMIT_BYTES)


def _sigmoid(x):
    return 1.0 / (1.0 + jnp.exp(-x))


def _silu(x):
    return x * _sigmoid(x)


def _gelu(x):
    return 0.5 * x * (1.0 + jnp.tanh(0.7978845608028654 * (x + 0.044715 * x * x * x)))


def _softplus(x):
    return jnp.maximum(x, 0.0) + jnp.log(1.0 + jnp.exp(-jnp.abs(x)))


def _rmsnorm(x, w):
    return x * lax.rsqrt(jnp.mean(x * x, axis=-1, keepdims=True) + EPS) * w


def _dot(a, b, dims=((1,), (0,)), precision=None):
    return lax.dot_general(a, b, (dims, ((), ())), precision=precision,
                           preferred_element_type=F32)


def _bdot(a, b, dims=((1,), (0,))):
    return _dot(a.astype(BF16), b.astype(BF16), dims)


def _split_bf16(a):
    hi = a.astype(BF16)
    return hi, (a - hi.astype(F32)).astype(BF16)


def _dot3_dims(a, b, dims):
    ah, al = _split_bf16(a)
    bh, bl = _split_bf16(b)
    return _dot(ah, bh, dims) + (_dot(ah, bl, dims) + _dot(al, bh, dims))


@jax.custom_vjp
def _dot3(a, b):
    return _dot3_dims(a, b, ((1,), (0,)))


def _dot3_fwd(a, b):
    return _dot3(a, b), (a, b)


def _dot3_bwd(res, g):
    a, b = res
    return _dot3_dims(g, b, ((1,), (1,))), _dot3_dims(a, g, ((0,), (0,)))


_dot3.defvjp(_dot3_fwd, _dot3_bwd)


def _mm(a, b, *, ta=False, tb=False, out_dtype=F32, tm=512, tn=512, tk=None, after=None, b_rows=None,
        addend=None, name):
    k_dim, m_dim = (a.shape if ta else a.shape[::-1])
    b_rows = b.shape[0] if b_rows is None else b_rows
    n_dim = b_rows if tb else b.shape[1]
    assert (b.shape[1] if tb else b_rows) == k_dim and b_rows <= b.shape[0]
    tm, tn = min(tm, m_dim), min(tn, n_dim)
    tk = k_dim if tk is None else tk
    assert m_dim % tm == 0 and n_dim % tn == 0 and k_dim % tk == 0
    nk = k_dim // tk
    a_spec = (pl.BlockSpec((tk, tm), lambda i, j, k: (k, i)) if ta
              else pl.BlockSpec((tm, tk), lambda i, j, k: (i, k)))
    b_spec = (pl.BlockSpec((tn, tk), lambda i, j, k: (j, k)) if tb
              else pl.BlockSpec((tk, tn), lambda i, j, k: (k, j)))
    dims = ((0 if ta else 1,), (1 if tb else 0,))

    extras = ([after] if after is not None else []) + ([addend] if addend is not None else [])
    extra_specs = ([pl.BlockSpec((8, 128), lambda i, j, k: (0, 0))] if after is not None else []) + (
        [pl.BlockSpec((tm, tn), lambda i, j, k: (i, j))] if addend is not None else [])

    def body(a_ref, b_ref, *rest):
        o_ref, *scratch = rest[len(extras):]
        p = _bdot(a_ref[...], b_ref[...], dims)
        finish = (lambda v: v + rest[len(extras) - 1][...]) if addend is not None else (lambda v: v)
        if nk == 1:
            o_ref[...] = finish(p).astype(o_ref.dtype)
        else:
            acc = scratch[0]
            k = pl.program_id(2)

            @pl.when(k == 0)
            def _():
                acc[...] = p

            @pl.when(k > 0)
            def _():
                acc[...] += p

            @pl.when(k == nk - 1)
            def _():
                o_ref[...] = finish(acc[...]).astype(o_ref.dtype)

    return pl.pallas_call(
        body, name=name,
        out_shape=jax.ShapeDtypeStruct((m_dim, n_dim), out_dtype),
        grid=(m_dim // tm, n_dim // tn, nk),
        in_specs=[a_spec, b_spec] + extra_specs,
        out_specs=pl.BlockSpec((tm, tn), lambda i, j, k: (i, j)),
        scratch_shapes=[pltpu.VMEM((tm, tn), F32)] if nk > 1 else [],
        compiler_params=_cparams("parallel", "parallel", "arbitrary"),
    )(a, b, *extras)


def _rms_fwd(x, w):
    l, d = x.shape

    def body(x_ref, w_ref, h_ref):
        h_ref[...] = _rmsnorm(x_ref[...], w_ref[...]).astype(BF16)

    return pl.pallas_call(
        body, name="rms_fwd",
        out_shape=jax.ShapeDtypeStruct((l, d), BF16),
        grid=(l // ROW_TILE,),
        in_specs=[pl.BlockSpec((ROW_TILE, d), lambda i: (i, 0)),
                  pl.BlockSpec((1, d), lambda i: (0, 0))],
        out_specs=pl.BlockSpec((ROW_TILE, d), lambda i: (i, 0)),
        compiler_params=_cparams("parallel"),
    )(x, w)


def _rms_bwd(x, w, dh, dres):
    l, d = x.shape

    def body(x_ref, w_ref, dh_ref, dres_ref, dx_ref, dw_ref):
        _, vjp = jax.vjp(_rmsnorm, x_ref[...], w_ref[...])
        dx, dw = vjp(dh_ref[...])
        dx_ref[...] = dx + dres_ref[...]

        @pl.when(pl.program_id(0) == 0)
        def _():
            dw_ref[...] = jnp.zeros_like(dw_ref)

        dw_ref[...] += dw

    row = pl.BlockSpec((ROW_TILE, d), lambda i: (i, 0))
    vec = pl.BlockSpec((1, d), lambda i: (0, 0))
    return pl.pallas_call(
        body, name="rms_bwd",
        out_shape=(jax.ShapeDtypeStruct((l, d), F32), jax.ShapeDtypeStruct((1, d), F32)),
        grid=(l // ROW_TILE,),
        in_specs=[row, vec, row, row],
        out_specs=(row, vec),
        compiler_params=_cparams("arbitrary"),
    )(x, w, dh, dres)


def _final(x, r, fw, target):
    l, d = x.shape

    def per_row_loss(x2, w, tgt):
        err = _rmsnorm(x2, w) - tgt
        return 0.5 * jnp.mean(err * err, axis=-1, keepdims=True)

    def body(x_ref, r_ref, w_ref, t_ref, dx_ref, dxb_ref, loss_ref, dw_ref):
        x2 = x_ref[...] + r_ref[...]
        rows, vjp = jax.vjp(functools.partial(per_row_loss, tgt=t_ref[...]), x2, w_ref[...])
        dx2, dw = vjp(jnp.ones_like(rows))
        dx_ref[...] = dx2
        dxb_ref[...] = dx2.astype(BF16)

        @pl.when(pl.program_id(0) == 0)
        def _():
            dw_ref[...] = jnp.zeros_like(dw_ref)
            loss_ref[...] = jnp.zeros_like(loss_ref)

        dw_ref[...] += dw
        loss_ref[...] += jnp.sum(rows, axis=0, keepdims=True)

    row = pl.BlockSpec((ROW_TILE, d), lambda i: (i, 0))
    vec = pl.BlockSpec((1, d), lambda i: (0, 0))
    return pl.pallas_call(
        body, name="final_norm_loss",
        out_shape=(jax.ShapeDtypeStruct((l, d), F32), jax.ShapeDtypeStruct((l, d), BF16),
                   jax.ShapeDtypeStruct((1, 1), F32), jax.ShapeDtypeStruct((1, d), F32)),
        grid=(l // ROW_TILE,),
        in_specs=[row, row, vec, row],
        out_specs=(row, row, pl.BlockSpec((1, 1), lambda i: (0, 0)), vec),
        compiler_params=_cparams("arbitrary"),
    )(x, r, fw, target)


def _merge_fn(gs, gd, ys, yd):
    return _sigmoid(gs) * ys + _sigmoid(gd) * yd


def _merge_fwd(proj, off_gs, off_gd, ys, yd):
    l, d = ys.shape
    cw = 512
    blk = lambda off: pl.BlockSpec((ROW_TILE, cw), lambda i, j: (i, off // cw + j))

    def body(gs_ref, gd_ref, ys_ref, yd_ref, o_ref):
        o_ref[...] = _merge_fn(gs_ref[...], gd_ref[...], ys_ref[...], yd_ref[...]).astype(BF16)

    return pl.pallas_call(
        body, name="merge_fwd",
        out_shape=jax.ShapeDtypeStruct((l, d), BF16),
        grid=(l // ROW_TILE, d // cw),
        in_specs=[blk(off_gs), blk(off_gd), blk(0), blk(0)],
        out_specs=blk(0),
        compiler_params=_cparams("parallel", "parallel"),
    )(proj, proj, ys, yd)


def _merge_bwd(proj, off_gs, off_gd, ys, yd, dmixed):
    l, d = ys.shape
    cw = 512
    blk = lambda off: pl.BlockSpec((ROW_TILE, cw), lambda i, j: (i, off // cw + j))

    def body(gs_ref, gd_ref, ys_ref, yd_ref, dm_ref, dgs_ref, dgd_ref, dys_ref, dyd_ref):
        _, vjp = jax.vjp(_merge_fn, gs_ref[...], gd_ref[...], ys_ref[...], yd_ref[...])
        dgs, dgd, dys, dyd = vjp(dm_ref[...])
        dgs_ref[...] = dgs.astype(BF16)
        dgd_ref[...] = dgd.astype(BF16)
        dys_ref[...] = dys.astype(BF16)
        dyd_ref[...] = dyd.astype(BF16)

    out = jax.ShapeDtypeStruct((l, d), BF16)
    return pl.pallas_call(
        body, name="merge_bwd",
        out_shape=(out, out, out, out),
        grid=(l // ROW_TILE, d // cw),
        in_specs=[blk(off_gs), blk(off_gd), blk(0), blk(0), blk(0)],
        out_specs=(blk(0), blk(0), blk(0), blk(0)),
        compiler_params=_cparams("parallel", "parallel"),
    )(proj, proj, ys, yd, dmixed)


def _s5_disc_fn(lam_re, lam_im, log_step):
    step = jnp.exp(log_step)
    mag = jnp.exp(lam_re * step)
    abar_re = mag * jnp.cos(lam_im * step)
    abar_im = mag * jnp.sin(lam_im * step)
    den = lam_re * lam_re + lam_im * lam_im
    xr = abar_re - 1.0
    f_re = (xr * lam_re + abar_im * lam_im) / den
    f_im = (abar_im * lam_re - xr * lam_im) / den
    return abar_re, abar_im, f_re, f_im


def _s5_disc_fwd(lam_re, lam_im, log_step):
    g, p = lam_re.shape

    def body(lr_ref, li_ref, ls_ref, ar_ref, ai_ref, fr_ref, fi_ref):
        ar, ai, fr, fi = _s5_disc_fn(lr_ref[...], li_ref[...], ls_ref[...])
        ar_ref[...] = ar
        ai_ref[...] = ai
        fr_ref[...] = fr
        fi_ref[...] = fi

    o = jax.ShapeDtypeStruct((g, p), F32)
    return pl.pallas_call(body, name="s5_disc_fwd", out_shape=(o, o, o, o),
                          compiler_params=_cparams())(lam_re, lam_im, log_step)


def _s5_disc_bwd(lam_re, lam_im, log_step, dar, dai, dfr, dfi):
    g, p = lam_re.shape

    def body(lr_ref, li_ref, ls_ref, dar_ref, dai_ref, dfr_ref, dfi_ref, dlr_ref, dli_ref, dls_ref):
        _, vjp = jax.vjp(_s5_disc_fn, lr_ref[...], li_ref[...], ls_ref[...])
        dlr, dli, dls = vjp((dar_ref[...], dai_ref[...], dfr_ref[...], dfi_ref[...]))
        dlr_ref[...] = dlr
        dli_ref[...] = dli
        dls_ref[...] = dls

    o = jax.ShapeDtypeStruct((g, p), F32)
    return pl.pallas_call(body, name="s5_disc_bwd",
                          out_shape=(o, o, jax.ShapeDtypeStruct((g, 1), F32)),
                          compiler_params=_cparams())(lam_re, lam_im, log_step, dar, dai, dfr, dfi)


def _s5_bbar_fwd(f_re, f_im, b_re, b_im):
    n, c = b_re.shape

    def body(fr_ref, fi_ref, br_ref, bi_ref, or_ref, oi_ref):
        fr, fi, br, bi = fr_ref[...], fi_ref[...], br_ref[...], bi_ref[...]
        or_ref[...] = fr * br - fi * bi
        oi_ref[...] = fr * bi + fi * br

    o = jax.ShapeDtypeStruct((n, c), F32)
    return pl.pallas_call(body, name="s5_bbar_fwd", out_shape=(o, o),
                          compiler_params=_cparams())(f_re, f_im, b_re, b_im)


def _s5_bbar_bwd(f_re, f_im, b_re, b_im, dbr, dbi):
    n, c = b_re.shape

    def body(fr_ref, fi_ref, br_ref, bi_ref, dor_ref, doi_ref, dfr_ref, dfi_ref, dbr_ref, dbi_ref):
        fr, fi, br, bi = fr_ref[...], fi_ref[...], br_ref[...], bi_ref[...]
        dor, doi = dor_ref[...], doi_ref[...]
        dfr_ref[...] = jnp.sum(dor * br + doi * bi, axis=-1, keepdims=True)
        dfi_ref[...] = jnp.sum(doi * br - dor * bi, axis=-1, keepdims=True)
        dbr_ref[...] = fr * dor + fi * doi
        dbi_ref[...] = fr * doi - fi * dor

    col = jax.ShapeDtypeStruct((n, 1), F32)
    o = jax.ShapeDtypeStruct((n, c), F32)
    return pl.pallas_call(body, name="s5_bbar_bwd", out_shape=(col, col, o, o),
                          compiler_params=_cparams())(f_re, f_im, b_re, b_im, dbr, dbi)


SUBLANES = 8


def _scan_groups(xr, xi, ar, ai, reverse):
    t = xr.shape[0]
    sub = lax.broadcasted_iota(jnp.int32, (t, 1), 0) & (SUBLANES - 1)
    pr, pi = ar, ai
    for sh in (1, 2, 4):
        if reverse:
            keep = sub < SUBLANES - sh
            sr, si = pltpu.roll(xr, t - sh, 0), pltpu.roll(xi, t - sh, 0)
        else:
            keep = sub >= sh
            sr, si = pltpu.roll(xr, sh, 0), pltpu.roll(xi, sh, 0)
        sr = jnp.where(keep, sr, 0.0)
        si = jnp.where(keep, si, 0.0)
        xr, xi = xr + pr * sr - pi * si, xi + pr * si + pi * sr
        pr, pi = pr * pr - pi * pi, 2.0 * pr * pi
    return xr, xi


def _scan_rows(xr, xi, ar, ai, cr, ci, sr_ref, si_ref, reverse):
    t, n = xr.shape
    xr, xi = _scan_groups(xr, xi, ar, ai, reverse)
    sr_ref[...] = xr
    si_ref[...] = xi
    sub = lax.broadcasted_iota(jnp.int32, (SUBLANES, n), 0)
    seed = sub == (SUBLANES - 1 if reverse else 0)
    pwr, pwi = _scan_groups(jnp.where(seed, ar, 0.0), jnp.where(seed, ai, 0.0), ar, ai, reverse)
    groups = range(t // SUBLANES)
    edge = 0 if reverse else SUBLANES - 1
    for g in (reversed(groups) if reverse else groups):
        rows = slice(g * SUBLANES, (g + 1) * SUBLANES)
        vr = sr_ref[rows, :] + (pwr * cr - pwi * ci)
        vi = si_ref[rows, :] + (pwr * ci + pwi * cr)
        sr_ref[rows, :] = vr
        si_ref[rows, :] = vi
        cr, ci = vr[edge:edge + 1, :], vi[edge:edge + 1, :]
    return cr, ci


def _s5_states(u_bf, bbr, bbi, ar, ai, cr, ci, sr_ref, si_ref):
    return _scan_rows(_dot(u_bf, bbr), _dot(u_bf, bbi), ar, ai, cr, ci, sr_ref, si_ref, reverse=False)


def _s5_fwd(proj, bbr, bbi, a_re, a_im, ctr, cti, d_skip, d_s5):
    l = proj.shape[0]
    nb, uc, ns = bbr.shape
    t = min(S5_T, l)
    nt = l // t

    def body(u_ref, bbr_ref, bbi_ref, ar_ref, ai_ref, ctr_ref, cti_ref, d_ref,
             y_ref, car_r_ref, car_i_ref, sr_ref, si_ref, cr, ci):
        @pl.when(pl.program_id(1) == 0)
        def _():
            cr[...] = jnp.zeros_like(cr)
            ci[...] = jnp.zeros_like(ci)

        car_r_ref[...] = cr[...]
        car_i_ref[...] = ci[...]
        u = u_ref[...]
        cr[...], ci[...] = _s5_states(u.astype(BF16), bbr_ref[...], bbi_ref[...], ar_ref[...],
                                      ai_ref[...], cr[...], ci[...], sr_ref, si_ref)
        y_ref[...] = (_bdot(sr_ref[...], ctr_ref[...]) - _bdot(si_ref[...], cti_ref[...])
                      + d_ref[...] * u)

    per_block = lambda shape: pl.BlockSpec((None,) + shape, lambda b, n: (b, 0, 0))
    return pl.pallas_call(
        body, name="s5_fwd",
        out_shape=(jax.ShapeDtypeStruct((l, d_s5), F32),
                   jax.ShapeDtypeStruct((nt, 1, nb * ns), F32),
                   jax.ShapeDtypeStruct((nt, 1, nb * ns), F32),
                   jax.ShapeDtypeStruct((l, nb * ns), F32),
                   jax.ShapeDtypeStruct((l, nb * ns), F32)),
        grid=(nb, nt),
        in_specs=[pl.BlockSpec((t, uc), lambda b, n: (n, b)),
                  per_block((uc, ns)), per_block((uc, ns)),
                  per_block((1, ns)), per_block((1, ns)),
                  per_block((ns, uc)), per_block((ns, uc)),
                  pl.BlockSpec((1, uc), lambda b, n: (0, b))],
        out_specs=(pl.BlockSpec((t, uc), lambda b, n: (n, b)),
                   pl.BlockSpec((None, 1, ns), lambda b, n: (n, 0, b)),
                   pl.BlockSpec((None, 1, ns), lambda b, n: (n, 0, b)),
                   pl.BlockSpec((t, ns), lambda b, n: (n, b)),
                   pl.BlockSpec((t, ns), lambda b, n: (n, b))),
        scratch_shapes=[pltpu.VMEM((1, ns), F32), pltpu.VMEM((1, ns), F32)],
        compiler_params=_cparams("parallel", "arbitrary"),
    )(proj, bbr, bbi, a_re, a_im, ctr, cti, d_skip)


def _s5_bwd(proj, dy, bbr, bbi, a_re, a_im, cbr, cbi, d_skip, car_r, car_i, states_r, states_i):
    l, d_s5 = dy.shape
    nb, uc, ns = bbr.shape
    t = min(S5_T, l)
    nt = l // t

    def body(u_ref, dy_ref, bbr_ref, bbi_ref, ar_ref, ai_ref, cbr_ref, cbi_ref, d_ref,
             car_r_ref, car_i_ref, sr_ref, si_ref,
             du_ref, dar_ref, dai_ref, dbbr_ref, dbbi_ref, dcbr_ref, dcbi_ref, dd_ref, gcr, gci,
             gr_ref, gi_ref):
        @pl.when(pl.program_id(1) == 0)
        def _():
            gcr[...] = jnp.zeros_like(gcr)
            gci[...] = jnp.zeros_like(gci)
            for ref in (dar_ref, dai_ref, dbbr_ref, dbbi_ref, dcbr_ref, dcbi_ref, dd_ref):
                ref[...] = jnp.zeros_like(ref)

        row = lax.broadcasted_iota(jnp.int32, (t, 1), 0)
        u, dy = u_ref[...], dy_ref[...]
        u_bf, dy_bf = u.astype(BF16), dy.astype(BF16)
        ar, ai = ar_ref[...], ai_ref[...]
        cr, ci = car_r_ref[...], car_i_ref[...]
        sr, si = sr_ref[...], si_ref[...]
        first = row == 0
        pr = jnp.where(first, cr, pltpu.roll(sr, 1, 0))
        pi = jnp.where(first, ci, pltpu.roll(si, 1, 0))
        gcr[...], gci[...] = _scan_rows(_dot(dy_bf, cbr_ref[...]), -_dot(dy_bf, cbi_ref[...]), ar, -ai,
                                        gcr[...], gci[...], gr_ref, gi_ref, reverse=True)
        gr, gi = gr_ref[...], gi_ref[...]
        dar_ref[...] += jnp.sum(gr * pr + gi * pi, axis=0, keepdims=True)
        dai_ref[...] += jnp.sum(gi * pr - gr * pi, axis=0, keepdims=True)
        gr_bf, gi_bf = gr.astype(BF16), gi.astype(BF16)
        tn = ((0,), (0,))
        dbbr_ref[...] += _dot(u_bf, gr_bf, tn)
        dbbi_ref[...] += _dot(u_bf, gi_bf, tn)
        dcbr_ref[...] += _dot(dy_bf, sr.astype(BF16), tn)
        dcbi_ref[...] -= _dot(dy_bf, si.astype(BF16), tn)
        nt_dims = ((1,), (1,))
        du = _dot(gr_bf, bbr_ref[...], nt_dims) + _dot(gi_bf, bbi_ref[...], nt_dims) + dy * d_ref[...]
        du_ref[...] = du.astype(BF16)
        dd_ref[...] += jnp.sum(dy * u, axis=0, keepdims=True)

    rev = lambda n: nt - 1 - n
    per_block = lambda shape: pl.BlockSpec((None,) + shape, lambda b, n: (b, 0, 0))
    acc = jax.ShapeDtypeStruct((nb, uc, ns), F32)
    vec = jax.ShapeDtypeStruct((nb, 1, ns), F32)
    return pl.pallas_call(
        body, name="s5_bwd",
        out_shape=(jax.ShapeDtypeStruct((l, d_s5), BF16), vec, vec, acc, acc, acc, acc,
                   jax.ShapeDtypeStruct((1, d_s5), F32)),
        grid=(nb, nt),
        in_specs=[pl.BlockSpec((t, uc), lambda b, n: (rev(n), b)),
                  pl.BlockSpec((t, uc), lambda b, n: (rev(n), b)),
                  per_block((uc, ns)), per_block((uc, ns)),
                  per_block((1, ns)), per_block((1, ns)),
                  per_block((uc, ns)), per_block((uc, ns)),
                  pl.BlockSpec((1, uc), lambda b, n: (0, b)),
                  pl.BlockSpec((None, 1, ns), lambda b, n: (rev(n), 0, b)),
                  pl.BlockSpec((None, 1, ns), lambda b, n: (rev(n), 0, b)),
                  pl.BlockSpec((t, ns), lambda b, n: (rev(n), b)),
                  pl.BlockSpec((t, ns), lambda b, n: (rev(n), b))],
        out_specs=(pl.BlockSpec((t, uc), lambda b, n: (rev(n), b)),
                   per_block((1, ns)), per_block((1, ns)),
                   per_block((uc, ns)), per_block((uc, ns)),
                   per_block((uc, ns)), per_block((uc, ns)),
                   pl.BlockSpec((1, uc), lambda b, n: (0, b))),
        scratch_shapes=[pltpu.VMEM((1, ns), F32), pltpu.VMEM((1, ns), F32)]
        + [pltpu.VMEM((t, ns), F32)] * 2,
        compiler_params=_cparams("parallel", "arbitrary"),
    )(proj, dy, bbr, bbi, a_re, a_im, cbr, cbi, d_skip, car_r, car_i, states_r, states_i)


def _s5_glu_fwd(y1, proj, off_z, wglu):
    l, d = y1.shape

    def body(y_ref, z_ref, w_ref, o_ref):
        y2 = _gelu(y_ref[...])
        y3 = y2 * _sigmoid(_bdot(y2, w_ref[...]))
        o_ref[...] = (y3 * _silu(z_ref[...])).astype(BF16)

    return pl.pallas_call(
        body, name="s5_glu_fwd",
        out_shape=jax.ShapeDtypeStruct((l, d), BF16),
        grid=(l // ROW_TILE,),
        in_specs=[pl.BlockSpec((ROW_TILE, d), lambda i: (i, 0)),
                  pl.BlockSpec((ROW_TILE, d), lambda i: (i, off_z // d)),
                  pl.BlockSpec((d, d), lambda i: (0, 0))],
        out_specs=pl.BlockSpec((ROW_TILE, d), lambda i: (i, 0)),
        compiler_params=_cparams("parallel"),
    )(y1, proj, wglu)


def _s5_glu_bwd(y1, proj, off_z, wglu, dout):
    l, d = y1.shape

    def body(y_ref, z_ref, w_ref, do_ref, dy_ref, dz_ref, dw_ref):
        y2, gelu_vjp = jax.vjp(_gelu, y_ref[...])
        z = z_ref[...]
        sz, silu_vjp = jax.vjp(_silu, z)
        y2_bf = y2.astype(BF16)
        sg = _sigmoid(_dot(y2_bf, w_ref[...]))
        dout = do_ref[...]
        dy3 = dout * sz
        dz_ref[...] = silu_vjp(dout * (y2 * sg))[0].astype(BF16)
        dgl = (dy3 * y2 * sg * (1.0 - sg)).astype(BF16)
        dy2 = dy3 * sg + _dot(dgl, w_ref[...], ((1,), (1,)))
        dy_ref[...] = gelu_vjp(dy2)[0]

        @pl.when(pl.program_id(0) == 0)
        def _():
            dw_ref[...] = jnp.zeros_like(dw_ref)

        dw_ref[...] += _dot(y2_bf, dgl, ((0,), (0,)))

    row = pl.BlockSpec((ROW_TILE, d), lambda i: (i, 0))
    full = pl.BlockSpec((d, d), lambda i: (0, 0))
    return pl.pallas_call(
        body, name="s5_glu_bwd",
        out_shape=(jax.ShapeDtypeStruct((l, d), F32), jax.ShapeDtypeStruct((l, d), BF16),
                   jax.ShapeDtypeStruct((d, d), F32)),
        grid=(l // ROW_TILE,),
        in_specs=[row, pl.BlockSpec((ROW_TILE, d), lambda i: (i, off_z // d)), full, row],
        out_specs=(row, row, full),
        compiler_params=_cparams("arbitrary"),
    )(y1, proj, wglu, dout)


def _shift_rows(x, k, back=False):
    if k == 0:
        return x
    t = x.shape[0]
    row = lax.broadcasted_iota(jnp.int32, (t, 1), 0)
    if back:
        return jnp.where(row < t - k, pltpu.roll(x, t - k, 0), 0.0)
    return jnp.where(row >= k, pltpu.roll(x, k, 0), 0.0)


def _dn_conv(x, w_ref):
    return sum(w_ref[CONV_K - 1 - k:CONV_K - k, :] * _shift_rows(x, k) for k in range(CONV_K))


def _dn_post_conv(c, j):
    y = _silu(c)
    n = y * lax.rsqrt(jnp.sum(y * y, axis=-1, keepdims=True) + EPS)
    n = n * jnp.where(j < DN_HEADS, DN_HEAD_DIM ** -0.5, 1.0)
    return jnp.where(j < 2 * DN_HEADS, n, y)


def _dn_prep_fwd(proj, off_qkv, conv_w):
    l = proj.shape[0]
    hd = DN_HEAD_DIM
    nblk = 3 * DN_HEADS

    def body(x_ref, w_ref, o_ref):
        o_ref[...] = _dn_post_conv(_dn_conv(x_ref[...], w_ref), pl.program_id(0))

    return pl.pallas_call(
        body, name="dn_prep_fwd",
        out_shape=jax.ShapeDtypeStruct((l, nblk * hd), F32),
        grid=(nblk,),
        in_specs=[pl.BlockSpec((l, hd), lambda j: (0, off_qkv // hd + j)),
                  pl.BlockSpec((CONV_K, hd), lambda j: (0, j))],
        out_specs=pl.BlockSpec((l, hd), lambda j: (0, j)),
        compiler_params=_cparams("parallel"),
    )(proj, conv_w)


def _dn_prep_bwd(proj, off_qkv, conv_w, dqkv):
    l = proj.shape[0]
    hd = DN_HEAD_DIM
    nblk = 3 * DN_HEADS

    def body(x_ref, w_ref, do_ref, dx_ref, dw_ref):
        x = x_ref[...]
        j = pl.program_id(0)
        _, vjp = jax.vjp(functools.partial(_dn_post_conv, j=j), _dn_conv(x, w_ref))
        dc = vjp(do_ref[...])[0]
        dx = sum(w_ref[CONV_K - 1 - k:CONV_K - k, :] * _shift_rows(dc, k, back=True)
                 for k in range(CONV_K))
        dx_ref[...] = dx.astype(BF16)
        for k in range(CONV_K):
            dw_ref[CONV_K - 1 - k:CONV_K - k, :] = jnp.sum(dc * _shift_rows(x, k), axis=0,
                                                           keepdims=True)

    return pl.pallas_call(
        body, name="dn_prep_bwd",
        out_shape=(jax.ShapeDtypeStruct((l, nblk * hd), BF16),
                   jax.ShapeDtypeStruct((CONV_K, nblk * hd), F32)),
        grid=(nblk,),
        in_specs=[pl.BlockSpec((l, hd), lambda j: (0, off_qkv // hd + j)),
                  pl.BlockSpec((CONV_K, hd), lambda j: (0, j)),
                  pl.BlockSpec((None, l, hd), lambda j: (j // DN_HEADS, 0, j % DN_HEADS))],
        out_specs=(pl.BlockSpec((l, hd), lambda j: (0, j)),
                   pl.BlockSpec((CONV_K, hd), lambda j: (0, j))),
        compiler_params=_cparams("parallel"),
    )(proj, conv_w, dqkv)


def _dn_gate_fn(ba, a_log_row, dt_row):
    lane = lax.broadcasted_iota(jnp.int32, ba.shape, 1)
    beta = _sigmoid(ba)
    g = -jnp.exp(a_log_row) * _softplus(ba + dt_row)
    return jnp.where(lane < DN_HEADS, beta, jnp.where(lane < 2 * DN_HEADS, g, 0.0))


def _dn_gates_fwd(proj, off_ba, a_log_row, dt_row):
    l = proj.shape[0]
    row = pl.BlockSpec((ROW_TILE, 128), lambda i: (i, off_ba // 128))
    vec = pl.BlockSpec((1, 128), lambda i: (0, 0))

    def body(ba_ref, al_ref, dt_ref, o_ref):
        o_ref[...] = _dn_gate_fn(ba_ref[...], al_ref[...], dt_ref[...])

    return pl.pallas_call(
        body, name="dn_gates_fwd",
        out_shape=jax.ShapeDtypeStruct((l, 128), F32),
        grid=(l // ROW_TILE,),
        in_specs=[row, vec, vec],
        out_specs=pl.BlockSpec((ROW_TILE, 128), lambda i: (i, 0)),
        compiler_params=_cparams("parallel"),
    )(proj, a_log_row, dt_row)


def _dn_gates_bwd(proj, off_ba, a_log_row, dt_row, dgb_heads):
    l = proj.shape[0]
    nh = dgb_heads.shape[0]
    row = pl.BlockSpec((ROW_TILE, 128), lambda i: (i, off_ba // 128))
    vec = pl.BlockSpec((1, 128), lambda i: (0, 0))

    def body(ba_ref, al_ref, dt_ref, dg_ref, dba_ref, dal_ref, ddt_ref):
        _, vjp = jax.vjp(_dn_gate_fn, ba_ref[...], al_ref[...], dt_ref[...])
        dgb = dg_ref[0]
        for h in range(1, nh):
            dgb = dgb + dg_ref[h]
        dba, dal, ddt = vjp(dgb)
        dba_ref[...] = dba.astype(BF16)

        @pl.when(pl.program_id(0) == 0)
        def _():
            dal_ref[...] = jnp.zeros_like(dal_ref)
            ddt_ref[...] = jnp.zeros_like(ddt_ref)

        dal_ref[...] += dal
        ddt_ref[...] += ddt

    return pl.pallas_call(
        body, name="dn_gates_bwd",
        out_shape=(jax.ShapeDtypeStruct((l, 128), BF16), jax.ShapeDtypeStruct((1, 128), F32),
                   jax.ShapeDtypeStruct((1, 128), F32)),
        grid=(l // ROW_TILE,),
        in_specs=[row, vec, vec, pl.BlockSpec((nh, ROW_TILE, 128), lambda i: (0, i, 0))],
        out_specs=(pl.BlockSpec((ROW_TILE, 128), lambda i: (i, 0)), vec, vec),
        compiler_params=_cparams("arbitrary"),
    )(proj, a_log_row, dt_row, dgb_heads)


@jax.custom_vjp
def _unit_lower_inverses(a_mats):
    c = a_mats[0].shape[0]
    eye = (lax.broadcasted_iota(jnp.int32, (c, c), 0) == lax.broadcasted_iota(jnp.int32, (c, c), 1)).astype(F32)
    t_inv = [eye - a for a in a_mats]
    power = a_mats
    for _ in range(int(math.log2(c)) - 1):
        power = [_bdot(p, p) for p in power]
        t_inv = [t + _bdot(t, p) for t, p in zip(t_inv, power)]
    return t_inv


def _unit_lower_inverses_fwd(a_mats):
    t_inv = _unit_lower_inverses(a_mats)
    return t_inv, t_inv


def _unit_lower_inverses_bwd(t_inv, grads):
    right = [_dot3_dims(g, t, ((1,), (1,))) for g, t in zip(grads, t_inv)]
    return ([-_dot3_dims(t, r, ((0,), (0,))) for t, r in zip(t_inv, right)],)


_unit_lower_inverses.defvjp(_unit_lower_inverses_fwd, _unit_lower_inverses_bwd)


def _dn_chunk_fn(states, qs, ks, vs, gb, heads):
    c = qs[0].shape[0]
    each = lambda f, *lists: [f(*args) for args in zip(*lists)]
    lane = lax.broadcasted_iota(jnp.int32, gb.shape, 1)
    ri = lax.broadcasted_iota(jnp.int32, (c, c), 0)
    ci = lax.broadcasted_iota(jnp.int32, (c, c), 1)
    causal, strict = ri >= ci, ri > ci
    eye = (ri == ci).astype(F32)
    rowi = lax.broadcasted_iota(jnp.int32, (c, 1), 0)
    nt_dims = ((1,), (1,))
    hdot = functools.partial(_dot, precision=HIGHEST)

    pick = lambda m, at: jnp.sum(jnp.where(lane == at, m, 0.0), axis=1, keepdims=True)
    gb_cum = hdot(causal.astype(F32), gb)
    beta = [pick(gb, h) for h in heads]
    gc = [pick(gb_cum, h + DN_HEADS) for h in heads]
    gc_row = each(lambda g: jnp.sum(eye * g, axis=0, keepdims=True), gc)
    decay = each(lambda g, gr: jnp.where(causal, jnp.exp(jnp.where(causal, g - gr, 0.0)), 0.0),
                 gc, gc_row)
    kk = each(lambda k: _bdot(k, k, nt_dims), ks)
    a_mat = each(lambda b, m, dc: jnp.where(strict, b * m * dc, 0.0), beta, kk, decay)

    t_inv = _unit_lower_inverses(a_mat)
    egc = each(jnp.exp, gc)
    u_c = each(lambda t, v, b: _dot3(t, v * b), t_inv, vs, beta)
    w_c = each(lambda t, k, b, e: _dot3(t, k * (b * e)), t_inv, ks, beta, egc)
    qk = each(lambda q, k, dc: _bdot(q, k, nt_dims) * dc, qs, ks, decay)
    g_end = each(lambda g: jnp.sum(jnp.where(rowi == c - 1, g, 0.0), axis=0, keepdims=True), gc)
    v_new = each(lambda u, w, s: u - _bdot(w, s), u_c, w_c, states)
    o = each(lambda q, e, s, m, vn: _bdot(q * e, s) + _bdot(m, vn), qs, egc, states, qk, v_new)
    new_states = each(
        lambda s, ge, k, g, vn: s * jnp.exp(ge) + _bdot(k * jnp.exp(ge - g), vn, ((0,), (0,))),
        states, g_end, ks, gc, v_new)
    return o, new_states


def _dn_chunk_specs(order):
    hd, nh, hps = DN_HEAD_DIM, DN_HEADS, DN_HEADS_PER_STEP
    qkv = lambda part: pl.BlockSpec((CHUNK, hps * hd), lambda h, n: (order(n), part * (nh // hps) + h))
    gb = pl.BlockSpec((CHUNK, 128), lambda h, n: (order(n), 0))
    state = pl.BlockSpec((hps, None, hd, hd), lambda h, n: (h, order(n), 0, 0))
    return qkv, gb, state


def _dn_chunk_fwd(qkv, gb):
    l = qkv.shape[0]
    hd, nh, hps = DN_HEAD_DIM, DN_HEADS, DN_HEADS_PER_STEP
    n_chunks = l // CHUNK
    qkv_spec, gb_spec, state_spec = _dn_chunk_specs(lambda n: n)

    def body(q_ref, k_ref, v_ref, gb_ref, o_ref, s_ref, state):
        @pl.when(pl.program_id(1) == 0)
        def _():
            state[...] = jnp.zeros_like(state)

        cols = [slice(i * hd, (i + 1) * hd) for i in range(hps)]
        states = [state[i] for i in range(hps)]
        for i in range(hps):
            s_ref[i] = states[i]
        o, new_states = _dn_chunk_fn(
            states, [q_ref[:, cs] for cs in cols], [k_ref[:, cs] for cs in cols],
            [v_ref[:, cs] for cs in cols], gb_ref[...],
            [pl.program_id(0) * hps + i for i in range(hps)])
        for i in range(hps):
            o_ref[:, cols[i]] = o[i]
            state[i] = new_states[i]

    return pl.pallas_call(
        body, name="dn_chunk_fwd",
        out_shape=(jax.ShapeDtypeStruct((l, nh * hd), F32),
                   jax.ShapeDtypeStruct((nh, n_chunks, hd, hd), F32)),
        grid=(nh // hps, n_chunks),
        in_specs=[qkv_spec(0), qkv_spec(1), qkv_spec(2), gb_spec],
        out_specs=(pl.BlockSpec((CHUNK, hps * hd), lambda h, n: (n, h)), state_spec),
        scratch_shapes=[pltpu.VMEM((hps, hd, hd), F32)],
        compiler_params=_cparams("parallel", "arbitrary"),
    )(qkv, qkv, qkv, gb)


def _dn_chunk_bwd(qkv, gb, states, do):
    l = qkv.shape[0]
    hd, nh, hps = DN_HEAD_DIM, DN_HEADS, DN_HEADS_PER_STEP
    n_chunks = l // CHUNK
    rev = lambda n: n_chunks - 1 - n
    qkv_spec, gb_spec, state_spec = _dn_chunk_specs(rev)

    def body(q_ref, k_ref, v_ref, gb_ref, s_ref, do_ref, dqkv_ref, dgb_ref, dstate):
        @pl.when(pl.program_id(1) == 0)
        def _():
            dstate[...] = jnp.zeros_like(dstate)

        cols = [slice(i * hd, (i + 1) * hd) for i in range(hps)]
        fn = functools.partial(_dn_chunk_fn, heads=[pl.program_id(0) * hps + i for i in range(hps)])
        _, vjp = jax.vjp(fn, [s_ref[i] for i in range(hps)], [q_ref[:, cs] for cs in cols],
                         [k_ref[:, cs] for cs in cols], [v_ref[:, cs] for cs in cols], gb_ref[...])
        ds, dq, dk, dv, dgb = vjp(([do_ref[:, cs] for cs in cols], [dstate[i] for i in range(hps)]))
        for i in range(hps):
            dstate[i] = ds[i]
            dqkv_ref[0, :, cols[i]] = dq[i]
            dqkv_ref[1, :, cols[i]] = dk[i]
            dqkv_ref[2, :, cols[i]] = dv[i]
        dgb_ref[...] = dgb

    head_out = pl.BlockSpec((CHUNK, hps * hd), lambda h, n: (rev(n), h))
    return pl.pallas_call(
        body, name="dn_chunk_bwd",
        out_shape=(jax.ShapeDtypeStruct((3, l, nh * hd), F32),
                   jax.ShapeDtypeStruct((nh // hps, l, 128), F32)),
        grid=(nh // hps, n_chunks),
        in_specs=[qkv_spec(0), qkv_spec(1), qkv_spec(2), gb_spec, state_spec, head_out],
        out_specs=(pl.BlockSpec((3, CHUNK, hps * hd), lambda h, n: (0, rev(n), h)),
                   pl.BlockSpec((None, CHUNK, 128), lambda h, n: (h, rev(n), 0))),
        scratch_shapes=[pltpu.VMEM((hps, hd, hd), F32)],
        compiler_params=_cparams("parallel", "arbitrary"),
    )(qkv, qkv, qkv, gb, states, do)


def _dn_out_fn(o, z, w):
    return _rmsnorm(o, w) * _silu(z)


def _dn_out_fwd(o, proj, off_z, w):
    l, d = o.shape
    hd = DN_HEAD_DIM
    blk = lambda off: pl.BlockSpec((ROW_TILE, hd), lambda i, h: (i, off // hd + h))

    def body(o_ref, z_ref, w_ref, out_ref):
        out_ref[...] = _dn_out_fn(o_ref[...], z_ref[...], w_ref[...]).astype(BF16)

    return pl.pallas_call(
        body, name="dn_out_fwd",
        out_shape=jax.ShapeDtypeStruct((l, d), BF16),
        grid=(l // ROW_TILE, d // hd),
        in_specs=[blk(0), blk(off_z), pl.BlockSpec((1, hd), lambda i, h: (0, 0))],
        out_specs=blk(0),
        compiler_params=_cparams("parallel", "parallel"),
    )(o, proj, w)


def _dn_out_bwd(o, proj, off_z, w, dout):
    l, d = o.shape
    hd = DN_HEAD_DIM
    blk = lambda off: pl.BlockSpec((ROW_TILE, hd), lambda i, h: (i, off // hd + h))
    vec = pl.BlockSpec((1, hd), lambda i, h: (0, 0))

    def body(o_ref, z_ref, w_ref, dout_ref, do_ref, dz_ref, dw_ref):
        _, vjp = jax.vjp(_dn_out_fn, o_ref[...], z_ref[...], w_ref[...])
        do, dz, dw = vjp(dout_ref[...])
        do_ref[...] = do
        dz_ref[...] = dz.astype(BF16)

        @pl.when((pl.program_id(0) == 0) & (pl.program_id(1) == 0))
        def _():
            dw_ref[...] = jnp.zeros_like(dw_ref)

        dw_ref[...] += dw

    return pl.pallas_call(
        body, name="dn_out_bwd",
        out_shape=(jax.ShapeDtypeStruct((l, d), F32), jax.ShapeDtypeStruct((l, d), BF16),
                   jax.ShapeDtypeStruct((1, hd), F32)),
        grid=(l // ROW_TILE, d // hd),
        in_specs=[blk(0), blk(off_z), vec, blk(0)],
        out_specs=(blk(0), blk(0), vec),
        compiler_params=_cparams("arbitrary", "arbitrary"),
    )(o, proj, w, dout)


def _tile_2d(rows, cols, budget_bytes=1 << 20):
    for tr in (rows, 4096, 2048, 1024, 512, 256, 128, 64, 32, 16):
        if tr <= rows and rows % tr == 0 and tr * cols * 4 <= budget_bytes:
            return tr, cols
    for tc in (2048, 1024, 512, 256, 128):
        if cols % tc == 0 and rows * tc * 4 <= 2 * budget_bytes:
            return rows, tc
    raise ValueError((rows, cols))


def _adamw_update(g, w_ref, m_ref, v_ref, go_ref, d_ref, mo_ref, vo_ref):
    c1 = 1.0 / (1.0 - ADAM_B1 ** ADAM_STEP)
    c2 = 1.0 / (1.0 - ADAM_B2 ** ADAM_STEP)
    m_new = ADAM_B1 * m_ref[...] + (1.0 - ADAM_B1) * g
    v_new = ADAM_B2 * v_ref[...] + (1.0 - ADAM_B2) * (g * g)
    go_ref[...] = g
    mo_ref[...] = m_new
    vo_ref[...] = v_new
    d_ref[...] = -ADAM_LR * ((m_new * c1) / (jnp.sqrt(v_new * c2) + ADAM_EPS) + ADAM_WD * w_ref[...])


def _adamw(w, m, v, gslots, name):
    rows, cols = w.shape
    ns = gslots.shape[0]
    tr, tc = _tile_2d(rows, cols)

    def body(w_ref, m_ref, v_ref, g_ref, go_ref, d_ref, mo_ref, vo_ref):
        g = g_ref[0].astype(F32)
        for s in range(1, ns):
            g = g + g_ref[s].astype(F32)
        _adamw_update(g, w_ref, m_ref, v_ref, go_ref, d_ref, mo_ref, vo_ref)

    blk = pl.BlockSpec((tr, tc), lambda i, j: (i, j))
    o = jax.ShapeDtypeStruct((rows, cols), F32)
    return pl.pallas_call(
        body, name=name, out_shape=(o, o, o, o),
        grid=(rows // tr, cols // tc),
        in_specs=[blk, blk, blk, pl.BlockSpec((ns, tr, tc), lambda i, j: (0, i, j))],
        out_specs=(blk, blk, blk, blk),
        compiler_params=_cparams("parallel", "parallel"),
    )(w, m, v, gslots)


def _slot_sum(gslots, name):
    ns, rows, cols = gslots.shape
    tr, tc = _tile_2d(rows, cols)

    def body(g_ref, o_ref):
        g = g_ref[0]
        for s in range(1, ns):
            g = g + g_ref[s]
        o_ref[...] = g

    return pl.pallas_call(
        body, name=name, out_shape=jax.ShapeDtypeStruct((rows, cols), F32),
        grid=(rows // tr, cols // tc),
        in_specs=[pl.BlockSpec((ns, tr, tc), lambda i, j: (0, i, j))],
        out_specs=pl.BlockSpec((tr, tc), lambda i, j: (i, j)),
        compiler_params=_cparams("parallel", "parallel"),
    )(gslots)


HBM_SPEC = pl.BlockSpec(memory_space=pl.ANY)


def _all_gather(arrs, name, relayed=(), after=None):
    n = len(arrs)
    n_sems = 13
    n_in = n + (after is not None)

    def body(*refs):
        ins, outs = refs[:n], refs[n_in:n_in + n]
        send_sems, recv_sems, local_sems = refs[n_in + n:]
        x, y, c = lax.axis_index("x"), lax.axis_index("y"), lax.axis_index("c")
        me, sibling = (x, y, c), (x, y, 1 - c)
        chips = [(1 - x, y), (x, 1 - y), (1 - x, 1 - y)]
        index = lambda px, py, pc: 4 * px + 2 * py + pc

        def copy(a, k, block, to, src=None, cols=None):
            dst = outs[a].at[index(*block)]
            src = dst if src is None else src
            if cols is not None:
                dst, src = dst.at[:, cols], src.at[:, cols]
            return pltpu.make_async_remote_copy(
                src_ref=src, dst_ref=dst, send_sem=send_sems.at[a, k], recv_sem=recv_sems.at[a, k],
                device_id=to, device_id_type=MESH)

        mine = [pltpu.make_async_copy(ins[a], outs[a].at[index(*me)], local_sems.at[a])
                for a in range(n)]
        for cp in mine:
            cp.start()
        sends = []

        def start(cp):
            cp.start()
            sends.append(cp)

        halves = {a: (pl.ds(0, arrs[a].shape[1] // 2), pl.ds(arrs[a].shape[1] // 2, arrs[a].shape[1] // 2))
                  for a in relayed}
        near_x, near_y, far = [(*chip, c) for chip in chips]
        for a in range(n):
            start(copy(a, 0, me, sibling, src=ins[a]))
            if a in relayed:
                left, right = halves[a]
                for k, to, cols in ((1, near_x, left), (3, near_y, right), (2, near_x, right), (4, near_y, left)):
                    start(copy(a, k, me, to, src=ins[a], cols=cols))
            else:
                for j, chip in enumerate(chips):
                    start(copy(a, 1 + j, me, (*chip, c), src=ins[a]))
        for a in relayed:
            left, right = halves[a]
            for k, block, cols, onward, to_sibling in (
                    (1, near_x, left, (5, near_y), 7), (3, near_y, right, (6, near_x), 10),
                    (2, near_x, right, None, 8), (4, near_y, left, None, 9),
                    (5, far, left, None, 11), (6, far, right, None, 12)):
                copy(a, k, block, me, cols=cols).wait_recv()
                if onward is not None:
                    start(copy(a, onward[0], block, onward[1], cols=cols))
                start(copy(a, to_sibling, block, sibling, cols=cols))
        for j, chip in enumerate(chips):
            for a in range(n):
                if a not in relayed:
                    copy(a, 1 + j, (*chip, c), me).wait_recv()
                    start(copy(a, 4 + j, (*chip, c), sibling))
        for a in range(n):
            copy(a, 0, sibling, me).wait_recv()
            if a in relayed:
                left, right = halves[a]
                for k, chip, cols in ((7, chips[0], left), (8, chips[0], right), (9, chips[1], left),
                                      (10, chips[1], right), (11, chips[2], left), (12, chips[2], right)):
                    copy(a, k, (*chip, 1 - c), me, cols=cols).wait_recv()
            else:
                for j, chip in enumerate(chips):
                    copy(a, 4 + j, (*chip, 1 - c), me).wait_recv()
        for cp in sends:
            cp.wait_send()
        for cp in mine:
            cp.wait()

    return pl.pallas_call(
        body, name=name,
        out_shape=[jax.ShapeDtypeStruct((N_DEV,) + a.shape, a.dtype) for a in arrs],
        in_specs=[HBM_SPEC] * n_in, out_specs=[HBM_SPEC] * n,
        scratch_shapes=[pltpu.SemaphoreType.DMA((n, n_sems)), pltpu.SemaphoreType.DMA((n, n_sems)),
                        pltpu.SemaphoreType.DMA((n,))],
    )(*arrs, *([after] if after is not None else []))


def _sibling_swap(arrs, name):
    n = len(arrs)

    def body(*refs):
        ins, outs = refs[:n], refs[n:2 * n]
        send_sems, recv_sems = refs[2 * n:]
        x, y, c = lax.axis_index("x"), lax.axis_index("y"), lax.axis_index("c")
        copies = [pltpu.make_async_remote_copy(
            src_ref=ins[a].at[:, 1 - c], dst_ref=outs[a],
            send_sem=send_sems.at[a], recv_sem=recv_sems.at[a],
            device_id=(x, y, 1 - c), device_id_type=MESH) for a in range(n)]
        for cp in copies:
            cp.start()
        for cp in copies:
            cp.wait()

    return pl.pallas_call(
        body, name=name,
        out_shape=[jax.ShapeDtypeStruct(a.shape[:1] + a.shape[2:], a.dtype) for a in arrs],
        in_specs=[HBM_SPEC] * n, out_specs=[HBM_SPEC] * n,
        scratch_shapes=[pltpu.SemaphoreType.DMA((n,)), pltpu.SemaphoreType.DMA((n,))],
    )(*arrs)


def _pair_sum(mine, theirs, core, name):
    chips, _, rows, cols = mine.shape
    tr, tc = _tile_2d(rows, cols, budget_bytes=2 << 20)

    def body(core_ref, a_ref, b_ref, o_ref):
        o_ref[...] = (a_ref[...].astype(F32) + b_ref[...].astype(F32)).astype(o_ref.dtype)

    slab = pl.BlockSpec((None, tr, tc), lambda ch, i, j, core_ref: (ch, i, j))
    return pl.pallas_call(
        body, name=name, out_shape=jax.ShapeDtypeStruct((chips, rows, cols), mine.dtype),
        grid_spec=pltpu.PrefetchScalarGridSpec(
            num_scalar_prefetch=1, grid=(chips, rows // tr, cols // tc),
            in_specs=[pl.BlockSpec((None, None, tr, tc),
                                   lambda ch, i, j, core_ref: (ch, core_ref[0], i, j)), slab],
            out_specs=slab),
        compiler_params=_cparams("parallel", "parallel", "parallel"),
    )(core, mine, theirs)


HBM_ONLY = pl.BlockSpec(memory_space=pltpu.HBM)
SEM_SPEC = pl.BlockSpec(memory_space=pltpu.SEMAPHORE)
SPLIT_COPY_EFFECT = pltpu.SideEffectType.DATAFLOW_SIDE_EFFECTING


def _flip(v, bit):
    return 1 - v if bit else v


def _chip_slices_plan(n):
    def plan():
        x, y, c = lax.axis_index("x"), lax.axis_index("y"), lax.axis_index("c")
        copies = []
        for k in range(1, 4):
            px, py = _flip(x, k & 2), _flip(y, k & 1)
            copies += [(a, 2 * px + py, 2 * x + y, (px, py, c)) for a in range(n)]
        return copies
    return plan, 3 * n


def _gather_plan(n):
    def plan():
        x, y, c = lax.axis_index("x"), lax.axis_index("y"), lax.axis_index("c")
        copies = []
        for k in range(1, N_DEV):
            peer = (_flip(x, k & 4), _flip(y, k & 2), _flip(c, k & 1))
            copies += [(a, None, 4 * x + 2 * y + c, peer) for a in range(n)]
        return copies
    return plan, 7 * n


def _planned_copies(plan, srcs, lands, send_sems, recv_sems):
    return [pltpu.make_async_remote_copy(
        src_ref=srcs[a] if src_at is None else srcs[a].at[src_at], dst_ref=lands[a].at[land_at],
        send_sem=send_sems[i], recv_sem=recv_sems[i], device_id=peer, device_id_type=MESH)
        for i, (a, src_at, land_at, peer) in enumerate(plan())]


def _split_exchange_start(plan_and_count, arrs, land_shapes, name, after=None):
    plan, n_sems = plan_and_count
    n = len(arrs)

    n_in = 2 * n + (after is not None)

    def body(*refs):
        srcs, lands = refs[:n], refs[n:2 * n]
        send_sems, recv_sems = refs[n_in:n_in + n_sems], refs[n_in + n_sems:n_in + 2 * n_sems]
        token = refs[-1]
        for copy in _planned_copies(plan, srcs, lands, send_sems, recv_sems):
            copy.start()
        token[...] = jnp.zeros_like(token)

    hbm = lambda a: pltpu.HBM(a.shape, a.dtype)
    operands = [pltpu.with_memory_space_constraint(a, pltpu.HBM) for a in arrs]
    operands += [pltpu.with_memory_space_constraint(lax.empty(shape, a.dtype), pltpu.HBM)
                 for a, shape in zip(arrs, land_shapes)]
    out = pl.pallas_call(
        body, name=name,
        out_shape=(*[pltpu.SemaphoreType.DMA(())] * (2 * n_sems),
                   *[hbm(a) for a in operands],
                   jax.ShapeDtypeStruct((8, 128), F32)),
        in_specs=[HBM_ONLY] * (2 * n) + [pl.BlockSpec(memory_space=pl.ANY)] * (after is not None),
        out_specs=(*[SEM_SPEC] * (2 * n_sems), *[HBM_ONLY] * (2 * n),
                   pl.BlockSpec(memory_space=pltpu.VMEM)),
        input_output_aliases={i: 2 * n_sems + i for i in range(2 * n)},
        compiler_params=pltpu.CompilerParams(has_side_effects=SPLIT_COPY_EFFECT),
    )(*operands, *([after] if after is not None else []))
    sems, rest = list(out[:2 * n_sems]), out[2 * n_sems:]
    return sems, list(rest[:n]), list(rest[n:2 * n]), rest[-1]


def _split_exchange_wait(plan_and_count, sems, srcs, lands, after, name):
    plan, n_sems = plan_and_count
    n = len(srcs)

    def body(*refs):
        src_refs, land_refs = refs[:n], refs[n:2 * n]
        send_sems, recv_sems = refs[2 * n:2 * n + n_sems], refs[2 * n + n_sems:2 * n + 2 * n_sems]
        for copy in _planned_copies(plan, src_refs, land_refs, send_sems, recv_sems):
            copy.wait_send()
            copy.wait_recv()

    hbm = lambda a: pltpu.HBM(a.shape, a.dtype)
    out = pl.pallas_call(
        body, name=name,
        out_shape=(*[hbm(a) for a in srcs], *[hbm(a) for a in lands]),
        in_specs=[HBM_ONLY] * (2 * n) + [SEM_SPEC] * (2 * n_sems) + [pl.BlockSpec(memory_space=pl.ANY)],
        out_specs=tuple([HBM_ONLY] * (2 * n)),
        input_output_aliases={i: i for i in range(2 * n)},
        compiler_params=pltpu.CompilerParams(has_side_effects=SPLIT_COPY_EFFECT),
    )(*srcs, *lands, *sems, after)
    return list(out[n:])


def _adamw_exchanged(w, m, v, own, landed, chip, name):
    rows, cols = w.shape
    tr, tc = _tile_2d(rows, cols)

    def body(chip_ref, w_ref, m_ref, v_ref, own_ref, l1_ref, l2_ref, l3_ref, go_ref, d_ref, mo_ref, vo_ref):
        g = own_ref[...].astype(F32)
        for ref in (l1_ref, l2_ref, l3_ref):
            g = g + ref[...].astype(F32)
        _adamw_update(g, w_ref, m_ref, v_ref, go_ref, d_ref, mo_ref, vo_ref)

    blk = pl.BlockSpec((tr, tc), lambda i, j, chip_ref: (i, j))
    slot = lambda k: pl.BlockSpec((None, tr, tc), lambda i, j, chip_ref: (chip_ref[0] ^ k, i, j))
    o = jax.ShapeDtypeStruct((rows, cols), F32)
    return pl.pallas_call(
        body, name=name, out_shape=(o, o, o, o),
        grid_spec=pltpu.PrefetchScalarGridSpec(
            num_scalar_prefetch=1, grid=(rows // tr, cols // tc),
            in_specs=[blk, blk, blk, slot(0), slot(1), slot(2), slot(3)],
            out_specs=(blk, blk, blk, blk)),
        compiler_params=_cparams("parallel", "parallel"),
    )(chip, w, m, v, own, landed, landed, landed)


def _block_diag(t):
    nb, gpb, r, c = t.shape
    eye = jnp.eye(gpb, dtype=t.dtype)
    return jnp.einsum("ngrc,gh->ngrhc", t, eye).reshape(nb, gpb * r, gpb * c)


def _diag_blocks(t, r, c):
    nb = t.shape[0]
    gpb = t.shape[1] // r
    t = t.reshape(nb, gpb, r, gpb, c)
    return jnp.einsum("ngrhc,gh->ngrc", t, jnp.eye(gpb, dtype=t.dtype))


def _pack_rows(parts):
    flat = jnp.concatenate([p.reshape(-1).astype(F32) for p in parts])
    pad = (-flat.shape[0]) % (256 * 128)
    return jnp.pad(flat, (0, pad)).reshape(-1, 128)


def _unpack_rows(packed, shapes):
    flat = packed.reshape(-1)
    out, at = [], 0
    for shape in shapes:
        size = math.prod(shape)
        out.append(flat[at:at + size].reshape(shape))
        at += size
    return out


def kernel(x, ln_w, w_in, s5_lam_re, s5_lam_im, s5_log_step, s5_b_re, s5_b_im, s5_c_re, s5_c_im, s5_d, s5_w_glu, s5_w_up, dn_conv_w, dn_a_log, dn_dt_bias, dn_norm_w, dn_w_up, w_out, final_norm_w, loss_target, m_ln_w, m_w_in, m_s5_lam_re, m_s5_lam_im, m_s5_log_step, m_s5_b_re, m_s5_b_im, m_s5_c_re, m_s5_c_im, m_s5_d, m_s5_w_glu, m_s5_w_up, m_dn_conv_w, m_dn_a_log, m_dn_dt_bias, m_dn_norm_w, m_dn_w_up, m_w_out, m_final_norm_w, v_ln_w, v_w_in, v_s5_lam_re, v_s5_lam_im, v_s5_log_step, v_s5_b_re, v_s5_b_im, v_s5_c_re, v_s5_c_im, v_s5_d, v_s5_w_glu, v_s5_w_up, v_dn_conv_w, v_dn_a_log, v_dn_dt_bias, v_dn_norm_w, v_dn_w_up, v_w_out, v_final_norm_w):
    weights = dict(ln_w=ln_w, w_in=w_in, s5_lam_re=s5_lam_re, s5_lam_im=s5_lam_im,
                   s5_log_step=s5_log_step, s5_b_re=s5_b_re, s5_b_im=s5_b_im, s5_c_re=s5_c_re,
                   s5_c_im=s5_c_im, s5_d=s5_d, s5_w_glu=s5_w_glu, s5_w_up=s5_w_up,
                   dn_conv_w=dn_conv_w, dn_a_log=dn_a_log, dn_dt_bias=dn_dt_bias,
                   dn_norm_w=dn_norm_w, dn_w_up=dn_w_up, w_out=w_out, final_norm_w=final_norm_w)
    mom_m = dict(ln_w=m_ln_w, w_in=m_w_in, s5_lam_re=m_s5_lam_re, s5_lam_im=m_s5_lam_im,
                 s5_log_step=m_s5_log_step, s5_b_re=m_s5_b_re, s5_b_im=m_s5_b_im,
                 s5_c_re=m_s5_c_re, s5_c_im=m_s5_c_im, s5_d=m_s5_d, s5_w_glu=m_s5_w_glu,
                 s5_w_up=m_s5_w_up, dn_conv_w=m_dn_conv_w, dn_a_log=m_dn_a_log,
                 dn_dt_bias=m_dn_dt_bias, dn_norm_w=m_dn_norm_w, dn_w_up=m_dn_w_up,
                 w_out=m_w_out, final_norm_w=m_final_norm_w)
    mom_v = dict(ln_w=v_ln_w, w_in=v_w_in, s5_lam_re=v_s5_lam_re, s5_lam_im=v_s5_lam_im,
                 s5_log_step=v_s5_log_step, s5_b_re=v_s5_b_re, s5_b_im=v_s5_b_im,
                 s5_c_re=v_s5_c_re, s5_c_im=v_s5_c_im, s5_d=v_s5_d, s5_w_glu=v_s5_w_glu,
                 s5_w_up=v_s5_w_up, dn_conv_w=v_dn_conv_w, dn_a_log=v_dn_a_log,
                 dn_dt_bias=v_dn_dt_bias, dn_norm_w=v_dn_norm_w, dn_w_up=v_dn_w_up,
                 w_out=v_w_out, final_norm_w=v_final_norm_w)
    names = list(weights)

    l, d = x.shape[1], x.shape[2]
    d_s5 = d // 2
    groups = d_s5 // S5_GROUP
    nb = groups // S5_GPB
    d_dn = DN_HEADS * DN_HEAD_DIM
    w_in_cols = w_in.shape[2]
    d_in = N_DEV * w_in_cols
    off_ba_src = 2 * d_s5 + 4 * d_dn
    off_u, off_zs, off_qkv, off_zd = 0, d_s5, 2 * d_s5, 2 * d_s5 + 3 * d_dn
    off_ba = off_zd + d_dn
    n_main = off_ba + BA_PAD
    off_gs, off_gd = 0, d
    x2d, tgt2d = x[0], loss_target[0]
    my_index = 4 * lax.axis_index("x") + 2 * lax.axis_index("y") + lax.axis_index("c")

    g_win, g_conv = _all_gather([jnp.transpose(w_in[0]).astype(BF16), dn_conv_w[0]], name="gather_weights",
                                relayed=(0,))
    late_plan = _gather_plan(4)
    late_shards = [s5_w_glu[0].astype(BF16), s5_w_up[0].astype(BF16), dn_w_up[0].astype(BF16),
                   w_out[0].astype(BF16)]
    late_sems, late_shards, late_lands, late_token = _split_exchange_start(
        late_plan, late_shards, [(N_DEV,) + s.shape for s in late_shards], name="gather_late_start",
        after=g_conv)
    ba_end = off_ba_src + 2 * DN_HEADS
    w_full_t = g_win.reshape(d_in, d)
    w_gates_t = w_full_t[ba_end:]
    conv_full = jnp.transpose(g_conv, (1, 0, 2)).reshape(CONV_K, 3 * d_dn)

    lam_re, lam_im = s5_lam_re[0], s5_lam_im[0]
    log_step = s5_log_step[0].reshape(groups, 1)
    b_re = s5_b_re[0].reshape(groups * S5_STATE, S5_GROUP)
    b_im = s5_b_im[0].reshape(groups * S5_STATE, S5_GROUP)
    abar_re, abar_im, f_re, f_im = _s5_disc_fwd(lam_re, lam_im, log_step)
    f_re_col, f_im_col = f_re.reshape(-1, 1), f_im.reshape(-1, 1)
    bb_re, bb_im = _s5_bbar_fwd(f_re_col, f_im_col, b_re, b_im)

    def bb_blocks(t):
        t = t.reshape(nb, S5_GPB, S5_STATE, S5_GROUP).transpose(0, 1, 3, 2)
        return _block_diag(t).astype(BF16)

    def c_blocks(t):
        return _block_diag(t.reshape(nb, S5_GPB, S5_GROUP, S5_STATE)).astype(BF16)

    bbr, bbi = bb_blocks(bb_re), bb_blocks(bb_im)
    cbr, cbi = c_blocks(s5_c_re[0]), c_blocks(s5_c_im[0])
    ctr, cti = jnp.transpose(cbr, (0, 2, 1)), jnp.transpose(cbi, (0, 2, 1))
    a_re = abar_re.reshape(nb, 1, S5_GPB * S5_STATE)
    a_im = abar_im.reshape(nb, 1, S5_GPB * S5_STATE)

    h = _rms_fwd(x2d, ln_w)
    proj = _mm(h, w_full_t, tb=True, b_rows=n_main, tm=1024, tn=512, after=late_token, name="proj")
    proj_gates = _mm(h, w_gates_t, tb=True, tm=1024, tn=1024, name="proj_gates")
    y1, car_r, car_i, states_r, states_i = _s5_fwd(proj, bbr, bbi, a_re, a_im, ctr, cti, s5_d, d_s5)
    a_log_row = jnp.pad(dn_a_log, ((0, 0), (DN_HEADS, 128 - 2 * DN_HEADS)))
    dt_row = jnp.pad(dn_dt_bias, ((0, 0), (DN_HEADS, 128 - 2 * DN_HEADS)))
    qkv = _dn_prep_fwd(proj, off_qkv, conv_full)
    gb = _dn_gates_fwd(proj, off_ba, a_log_row, dt_row)
    o_dn, states = _dn_chunk_fwd(qkv, gb)

    late_lands = _split_exchange_wait(late_plan, late_sems, late_shards, late_lands, o_dn,
                                      name="gather_late_wait")
    g_glu, g_sup, g_dup, g_wout = [
        lax.dynamic_update_slice(land, shard[None], (my_index, 0, 0))
        for land, shard in zip(late_lands, late_shards)]
    wglu_full = g_glu.reshape(d_s5, d_s5)
    wsup_full = jnp.transpose(g_sup, (1, 0, 2)).reshape(d_s5, d)
    wdup_full = jnp.transpose(g_dup, (1, 0, 2)).reshape(d_dn, d)
    wout_full = g_wout.reshape(d, d)

    out_s = _s5_glu_fwd(y1, proj, off_zs, wglu_full)
    y_s = _mm(out_s, wsup_full, name="s5_up")
    out_d = _dn_out_fwd(o_dn, proj, off_zd, dn_norm_w)
    y_d = _mm(out_d, wdup_full, name="dn_up")

    mixed = _merge_fwd(proj_gates, off_gs, off_gd, y_s, y_d)
    branch = _mm(mixed, wout_full, name="w_out")
    dx2, dx2_bf, loss_dev, d_final_w = _final(x2d, branch, final_norm_w.reshape(1, d), tgt2d)

    g_wout_full = _mm(mixed, dx2_bf, ta=True, out_dtype=BF16, name="grad_w_out")
    dmixed = _mm(dx2_bf, wout_full, tb=True, name="d_mixed")
    dgs, dgd, dys, dyd = _merge_bwd(proj_gates, off_gs, off_gd, y_s, y_d, dmixed)

    g_dup_full = _mm(out_d, dyd, ta=True, out_dtype=BF16, name="grad_dn_up")
    dout_d = _mm(dyd, wdup_full, tb=True, name="d_out_d")
    do_dn, dzd, d_norm_w = _dn_out_bwd(o_dn, proj, off_zd, dn_norm_w, dout_d)
    dqkv, dgb_heads = _dn_chunk_bwd(qkv, gb, states, do_dn)
    dba, d_a_log_row, d_dt_row = _dn_gates_bwd(proj, off_ba, a_log_row, dt_row, dgb_heads)
    dqkv_pre, d_conv_full = _dn_prep_bwd(proj, off_qkv, conv_full, dqkv)

    g_sup_full = _mm(out_s, dys, ta=True, out_dtype=BF16, name="grad_s5_up")
    dout_s = _mm(dys, wsup_full, tb=True, name="d_out_s")
    dy1, dzs, g_glu_full = _s5_glu_bwd(y1, proj, off_zs, wglu_full, dout_s)

    def by_dest(t, axis=0):
        if axis == 1:
            return t.reshape(t.shape[0], 4, 2, t.shape[1] // N_DEV).transpose(1, 2, 0, 3)
        return t.reshape(4, 2, t.shape[0] // N_DEV, t.shape[1])

    core = lax.axis_index("c").astype(jnp.int32).reshape(1)
    chip = (2 * lax.axis_index("x") + lax.axis_index("y")).astype(jnp.int32).reshape(1)

    def chip_sums_of(which, parts, tag):
        from_sibling = _sibling_swap(parts, name="swap_grads_" + tag)
        return [_pair_sum(p, got, core, name="pair_sum_" + nm)
                for nm, p, got in zip(which, parts, from_sibling)]

    early = ["s5_w_glu", "s5_w_up", "dn_w_up", "w_out"]
    sums_a = chip_sums_of(early, [by_dest(g_glu_full.astype(BF16)), by_dest(g_sup_full, 1),
                                  by_dest(g_dup_full, 1), by_dest(g_wout_full)], "a")
    plan_a = _chip_slices_plan(len(sums_a))
    sems_a, src_a, land_a, token_a = _split_exchange_start(
        plan_a, sums_a, [t.shape for t in sums_a], name="exchange_start_a")

    (du, d_a_re, d_a_im, d_bbr, d_bbi, d_cbr, d_cbi, d_s5_d) = _s5_bwd(
        proj, dy1, bbr, bbi, a_re, a_im, cbr, cbi, s5_d + token_a[:1, :1], car_r, car_i,
        states_r, states_i)

    def from_bb_blocks(t):
        t = _diag_blocks(t, S5_GROUP, S5_STATE).transpose(0, 1, 3, 2)
        return t.reshape(groups * S5_STATE, S5_GROUP)

    d_f_re, d_f_im, d_b_re, d_b_im = _s5_bbar_bwd(f_re_col, f_im_col, b_re, b_im,
                                                 from_bb_blocks(d_bbr), from_bb_blocks(d_bbi))
    d_lam_re, d_lam_im, d_log_step = _s5_disc_bwd(
        lam_re, lam_im, log_step, d_a_re.reshape(groups, S5_STATE), d_a_im.reshape(groups, S5_STATE),
        d_f_re.reshape(groups, S5_STATE), d_f_im.reshape(groups, S5_STATE))
    d_c_re = _diag_blocks(d_cbr, S5_GROUP, S5_STATE).reshape(groups, S5_GROUP, S5_STATE)
    d_c_im = _diag_blocks(d_cbi, S5_GROUP, S5_STATE).reshape(groups, S5_GROUP, S5_STATE)

    dproj = jnp.concatenate([du, dzs, dqkv_pre, dzd, jnp.pad(dba, ((0, 0), (0, BA_PAD - 128)))], axis=1)
    dproj_gates = jnp.concatenate([dgs, dgd], axis=1)
    g_main_t = _mm(dproj, h, ta=True, out_dtype=BF16, tm=512, tn=d, name="grad_w_in")
    g_gates_t = _mm(dproj_gates, h, ta=True, out_dtype=BF16, tm=1024, tn=d, name="grad_w_in_gates")
    g_win_full_t = jnp.concatenate([g_main_t[:ba_end], g_gates_t], axis=0)
    sums_b = chip_sums_of(["w_in"], [by_dest(g_win_full_t)], "b")
    plan_b = _chip_slices_plan(1)
    sems_b, src_b, land_b, token_b = _split_exchange_start(
        plan_b, sums_b, [t.shape for t in sums_b], name="exchange_start_b")
    dh_main = _mm(dproj, w_full_t, b_rows=n_main, tm=1024, tn=1024, tk=n_main // 4, after=token_b,
                  name="d_h_main")
    dh = _mm(dproj_gates, w_gates_t, tm=1024, tn=1024, tk=2048, addend=dh_main, name="d_h")
    grad_x, d_ln_w = _rms_bwd(x2d, ln_w, dh, dx2)
    big = ["w_in"] + early
    results = {}

    land_a = _split_exchange_wait(plan_a, sems_a, src_a, land_a, grad_x, name="exchange_wait_a")
    for nm, own, landed in zip(early, src_a, land_a):
        results[nm] = _adamw_exchanged(weights[nm][0], mom_m[nm][0], mom_v[nm][0], own, landed, chip,
                                       name="adamw_" + nm)

    small = [nm for nm in names if nm not in big]
    small_grads = dict(
        ln_w=d_ln_w, s5_lam_re=d_lam_re, s5_lam_im=d_lam_im, s5_log_step=d_log_step,
        s5_b_re=d_b_re, s5_b_im=d_b_im, s5_c_re=d_c_re, s5_c_im=d_c_im, s5_d=d_s5_d,
        dn_conv_w=d_conv_full, dn_a_log=d_a_log_row[:, DN_HEADS:2 * DN_HEADS],
        dn_dt_bias=d_dt_row[:, DN_HEADS:2 * DN_HEADS], dn_norm_w=d_norm_w, final_norm_w=d_final_w)
    (all_small,) = _all_gather([_pack_rows([small_grads[nm] for nm in small])], name="gather_small_grads",
                               after=results[early[-1]][0])
    summed = _slot_sum(all_small, name="sum_small_grads")
    full_shapes = [(CONV_K, 3 * d_dn) if nm == "dn_conv_w" else weights[nm].shape for nm in small]
    g_small = dict(zip(small, _unpack_rows(summed, full_shapes)))
    conv_cols = dn_conv_w.shape[2]
    g_small["dn_conv_w"] = lax.dynamic_slice_in_dim(
        g_small["dn_conv_w"], my_index * conv_cols, conv_cols, axis=1).reshape(dn_conv_w.shape)
    packed = [_pack_rows([t[nm] for nm in small]) for t in (weights, mom_m, mom_v, g_small)]
    small_out = _adamw(packed[0], packed[1], packed[2], packed[3][None], name="adamw_small")
    small_shapes = [weights[nm].shape for nm in small]
    for kind, packed_out in enumerate(small_out):
        for nm, val in zip(small, _unpack_rows(packed_out, small_shapes)):
            results.setdefault(nm, [None] * 4)[kind] = val

    (land_b,) = _split_exchange_wait(plan_b, sems_b, src_b, land_b, small_out[0], name="exchange_wait_b")
    res = _adamw_exchanged(jnp.transpose(w_in[0]), jnp.transpose(m_w_in[0]), jnp.transpose(v_w_in[0]),
                           src_b[0], land_b, chip, name="adamw_w_in")
    results["w_in"] = [jnp.transpose(t) for t in res]

    loss = lax.psum(loss_dev[0, 0], ("x", "y", "c"))
    outs = [loss, grad_x[None]]
    for kind in range(4):
        outs += [results[nm][kind].reshape(weights[nm].shape) for nm in names]
    return tuple(outs)
```

```python
import functools
import math

import jax
import jax.numpy as jnp
from jax import lax
from jax.experimental import pallas as pl
from jax.experimental.pallas import tpu as pltpu

F32 = jnp.float32
BF16 = jnp.bfloat16
HIGHEST = lax.Precision.HIGHEST
MESH = pl.DeviceIdType.MESH
N_DEV = 8

EPS = 1e-6
S5_GROUP = 16
S5_STATE = 64
S5_GPB = 8
S5_T = 256
DN_HEADS = 8
DN_HEAD_DIM = 128
CHUNK = 64
DN_HEADS_PER_STEP = 8
CONV_K = 4
BA_PAD = 512

ADAM_LR = 0.001
ADAM_B1 = 0.9
ADAM_B2 = 0.999
ADAM_EPS = 1e-08
ADAM_WD = 0.01
ADAM_STEP = 10

VMEM_LIMIT_BYTES = 48 * 1024 * 1024
ROW_TILE = 256


def _cparams(*sem):
    return pltpu.CompilerParams(dimension_semantics=sem if sem else None,
                                vmem_limit_bytes=VMEM_LIMIT_BYTES)


def _sigmoid(x):
    return 1.0 / (1.0 + jnp.exp(-x))


def _silu(x):
    return x * _sigmoid(x)


def _gelu(x):
    return 0.5 * x * (1.0 + jnp.tanh(0.7978845608028654 * (x + 0.044715 * x * x * x)))


def _softplus(x):
    return jnp.maximum(x, 0.0) + jnp.log(1.0 + jnp.exp(-jnp.abs(x)))


def _rmsnorm(x, w):
    return x * lax.rsqrt(jnp.mean(x * x, axis=-1, keepdims=True) + EPS) * w


def _dot(a, b, dims=((1,), (0,)), precision=None):
    return lax.dot_general(a, b, (dims, ((), ())), precision=precision,
                           preferred_element_type=F32)


def _bdot(a, b, dims=((1,), (0,))):
    return _dot(a.astype(BF16), b.astype(BF16), dims)


def _split_bf16(a):
    hi = a.astype(BF16)
    return hi, (a - hi.astype(F32)).astype(BF16)


def _dot3_dims(a, b, dims):
    ah, al = _split_bf16(a)
    bh, bl = _split_bf16(b)
    return _dot(ah, bh, dims) + (_dot(ah, bl, dims) + _dot(al, bh, dims))


@jax.custom_vjp
def _dot3(a, b):
    return _dot3_dims(a, b, ((1,), (0,)))


def _dot3_fwd(a, b):
    return _dot3(a, b), (a, b)


def _dot3_bwd(res, g):
    a, b = res
    return _dot3_dims(g, b, ((1,), (1,))), _dot3_dims(a, g, ((0,), (0,)))


_dot3.defvjp(_dot3_fwd, _dot3_bwd)


def _mm(a, b, *, ta=False, tb=False, out_dtype=F32, tm=512, tn=512, tk=None, after=None, b_rows=None,
        addend=None, name):
    k_dim, m_dim = (a.shape if ta else a.shape[::-1])
    b_rows = b.shape[0] if b_rows is None else b_rows
    n_dim = b_rows if tb else b.shape[1]
    assert (b.shape[1] if tb else b_rows) == k_dim and b_rows <= b.shape[0]
    tm, tn = min(tm, m_dim), min(tn, n_dim)
    tk = k_dim if tk is None else tk
    assert m_dim % tm == 0 and n_dim % tn == 0 and k_dim % tk == 0
    nk = k_dim // tk
    a_spec = (pl.BlockSpec((tk, tm), lambda i, j, k: (k, i)) if ta
              else pl.BlockSpec((tm, tk), lambda i, j, k: (i, k)))
    b_spec = (pl.BlockSpec((tn, tk), lambda i, j, k: (j, k)) if tb
              else pl.BlockSpec((tk, tn), lambda i, j, k: (k, j)))
    dims = ((0 if ta else 1,), (1 if tb else 0,))

    extras = ([after] if after is not None else []) + ([addend] if addend is not None else [])
    extra_specs = ([pl.BlockSpec((8, 128), lambda i, j, k: (0, 0))] if after is not None else []) + (
        [pl.BlockSpec((tm, tn), lambda i, j, k: (i, j))] if addend is not None else [])

    def body(a_ref, b_ref, *rest):
        o_ref, *scratch = rest[len(extras):]
        p = _bdot(a_ref[...], b_ref[...], dims)
        finish = (lambda v: v + rest[len(extras) - 1][...]) if addend is not None else (lambda v: v)
        if nk == 1:
            o_ref[...] = finish(p).astype(o_ref.dtype)
        else:
            acc = scratch[0]
            k = pl.program_id(2)

            @pl.when(k == 0)
            def _():
                acc[...] = p

            @pl.when(k > 0)
            def _():
                acc[...] += p

            @pl.when(k == nk - 1)
            def _():
                o_ref[...] = finish(acc[...]).astype(o_ref.dtype)

    return pl.pallas_call(
        body, name=name,
        out_shape=jax.ShapeDtypeStruct((m_dim, n_dim), out_dtype),
        grid=(m_dim // tm, n_dim // tn, nk),
        in_specs=[a_spec, b_spec] + extra_specs,
        out_specs=pl.BlockSpec((tm, tn), lambda i, j, k: (i, j)),
        scratch_shapes=[pltpu.VMEM((tm, tn), F32)] if nk > 1 else [],
        compiler_params=_cparams("parallel", "parallel", "arbitrary"),
    )(a, b, *extras)


def _rms_fwd(x, w):
    l, d = x.shape

    def body(x_ref, w_ref, h_ref):
        h_ref[...] = _rmsnorm(x_ref[...], w_ref[...]).astype(BF16)

    return pl.pallas_call(
        body, name="rms_fwd",
        out_shape=jax.ShapeDtypeStruct((l, d), BF16),
        grid=(l // ROW_TILE,),
        in_specs=[pl.BlockSpec((ROW_TILE, d), lambda i: (i, 0)),
                  pl.BlockSpec((1, d), lambda i: (0, 0))],
        out_specs=pl.BlockSpec((ROW_TILE, d), lambda i: (i, 0)),
        compiler_params=_cparams("parallel"),
    )(x, w)


def _rms_bwd(x, w, dh, dres):
    l, d = x.shape

    def body(x_ref, w_ref, dh_ref, dres_ref, dx_ref, dw_ref):
        _, vjp = jax.vjp(_rmsnorm, x_ref[...], w_ref[...])
        dx, dw = vjp(dh_ref[...])
        dx_ref[...] = dx + dres_ref[...]

        @pl.when(pl.program_id(0) == 0)
        def _():
            dw_ref[...] = jnp.zeros_like(dw_ref)

        dw_ref[...] += dw

    row = pl.BlockSpec((ROW_TILE, d), lambda i: (i, 0))
    vec = pl.BlockSpec((1, d), lambda i: (0, 0))
    return pl.pallas_call(
        body, name="rms_bwd",
        out_shape=(jax.ShapeDtypeStruct((l, d), F32), jax.ShapeDtypeStruct((1, d), F32)),
        grid=(l // ROW_TILE,),
        in_specs=[row, vec, row, row],
        out_specs=(row, vec),
        compiler_params=_cparams("arbitrary"),
    )(x, w, dh, dres)


def _final(x, r, fw, target):
    l, d = x.shape

    def per_row_loss(x2, w, tgt):
        err = _rmsnorm(x2, w) - tgt
        return 0.5 * jnp.mean(err * err, axis=-1, keepdims=True)

    def body(x_ref, r_ref, w_ref, t_ref, dx_ref, dxb_ref, loss_ref, dw_ref):
        x2 = x_ref[...] + r_ref[...]
        rows, vjp = jax.vjp(functools.partial(per_row_loss, tgt=t_ref[...]), x2, w_ref[...])
        dx2, dw = vjp(jnp.ones_like(rows))
        dx_ref[...] = dx2
        dxb_ref[...] = dx2.astype(BF16)

        @pl.when(pl.program_id(0) == 0)
        def _():
            dw_ref[...] = jnp.zeros_like(dw_ref)
            loss_ref[...] = jnp.zeros_like(loss_ref)

        dw_ref[...] += dw
        loss_ref[...] += jnp.sum(rows, axis=0, keepdims=True)

    row = pl.BlockSpec((ROW_TILE, d), lambda i: (i, 0))
    vec = pl.BlockSpec((1, d), lambda i: (0, 0))
    return pl.pallas_call(
        body, name="final_norm_loss",
        out_shape=(jax.ShapeDtypeStruct((l, d), F32), jax.ShapeDtypeStruct((l, d), BF16),
                   jax.ShapeDtypeStruct((1, 1), F32), jax.ShapeDtypeStruct((1, d), F32)),
        grid=(l // ROW_TILE,),
        in_specs=[row, row, vec, row],
        out_specs=(row, row, pl.BlockSpec((1, 1), lambda i: (0, 0)), vec),
        compiler_params=_cparams("arbitrary"),
    )(x, r, fw, target)


def _merge_fn(gs, gd, ys, yd):
    return _sigmoid(gs) * ys + _sigmoid(gd) * yd


def _merge_fwd(proj, off_gs, off_gd, ys, yd):
    l, d = ys.shape
    cw = min(1024, d)
    blk = lambda off: pl.BlockSpec((ROW_TILE, cw), lambda i, j: (i, off // cw + j))

    def body(gs_ref, gd_ref, ys_ref, yd_ref, o_ref):
        o_ref[...] = _merge_fn(gs_ref[...], gd_ref[...], ys_ref[...], yd_ref[...]).astype(BF16)

    return pl.pallas_call(
        body, name="merge_fwd",
        out_shape=jax.ShapeDtypeStruct((l, d), BF16),
        grid=(l // ROW_TILE, d // cw),
        in_specs=[blk(off_gs), blk(off_gd), blk(0), blk(0)],
        out_specs=blk(0),
        compiler_params=_cparams("parallel", "parallel"),
    )(proj, proj, ys, yd)


def _merge_bwd(proj, off_gs, off_gd, ys, yd, dmixed):
    l, d = ys.shape
    cw = min(1024, d)
    blk = lambda off: pl.BlockSpec((ROW_TILE, cw), lambda i, j: (i, off // cw + j))

    def body(gs_ref, gd_ref, ys_ref, yd_ref, dm_ref, dgs_ref, dgd_ref, dys_ref, dyd_ref):
        _, vjp = jax.vjp(_merge_fn, gs_ref[...], gd_ref[...], ys_ref[...], yd_ref[...])
        dgs, dgd, dys, dyd = vjp(dm_ref[...])
        dgs_ref[...] = dgs.astype(BF16)
        dgd_ref[...] = dgd.astype(BF16)
        dys_ref[...] = dys.astype(BF16)
        dyd_ref[...] = dyd.astype(BF16)

    out = jax.ShapeDtypeStruct((l, d), BF16)
    return pl.pallas_call(
        body, name="merge_bwd",
        out_shape=(out, out, out, out),
        grid=(l // ROW_TILE, d // cw),
        in_specs=[blk(off_gs), blk(off_gd), blk(0), blk(0), blk(0)],
        out_specs=(blk(0), blk(0), blk(0), blk(0)),
        compiler_params=_cparams("parallel", "parallel"),
    )(proj, proj, ys, yd, dmixed)


def _s5_disc_fn(lam_re, lam_im, log_step):
    step = jnp.exp(log_step)
    mag = jnp.exp(lam_re * step)
    abar_re = mag * jnp.cos(lam_im * step)
    abar_im = mag * jnp.sin(lam_im * step)
    den = lam_re * lam_re + lam_im * lam_im
    xr = abar_re - 1.0
    f_re = (xr * lam_re + abar_im * lam_im) / den
    f_im = (abar_im * lam_re - xr * lam_im) / den
    return abar_re, abar_im, f_re, f_im


def _s5_disc_fwd(lam_re, lam_im, log_step):
    g, p = lam_re.shape

    def body(lr_ref, li_ref, ls_ref, ar_ref, ai_ref, fr_ref, fi_ref):
        ar, ai, fr, fi = _s5_disc_fn(lr_ref[...], li_ref[...], ls_ref[...])
        ar_ref[...] = ar
        ai_ref[...] = ai
        fr_ref[...] = fr
        fi_ref[...] = fi

    o = jax.ShapeDtypeStruct((g, p), F32)
    return pl.pallas_call(body, name="s5_disc_fwd", out_shape=(o, o, o, o),
                          compiler_params=_cparams())(lam_re, lam_im, log_step)


def _s5_disc_bwd(lam_re, lam_im, log_step, dar, dai, dfr, dfi):
    g, p = lam_re.shape

    def body(lr_ref, li_ref, ls_ref, dar_ref, dai_ref, dfr_ref, dfi_ref, dlr_ref, dli_ref, dls_ref):
        _, vjp = jax.vjp(_s5_disc_fn, lr_ref[...], li_ref[...], ls_ref[...])
        dlr, dli, dls = vjp((dar_ref[...], dai_ref[...], dfr_ref[...], dfi_ref[...]))
        dlr_ref[...] = dlr
        dli_ref[...] = dli
        dls_ref[...] = dls

    o = jax.ShapeDtypeStruct((g, p), F32)
    return pl.pallas_call(body, name="s5_disc_bwd",
                          out_shape=(o, o, jax.ShapeDtypeStruct((g, 1), F32)),
                          compiler_params=_cparams())(lam_re, lam_im, log_step, dar, dai, dfr, dfi)


def _s5_bbar_fwd(f_re, f_im, b_re, b_im):
    n, c = b_re.shape

    def body(fr_ref, fi_ref, br_ref, bi_ref, or_ref, oi_ref):
        fr, fi, br, bi = fr_ref[...], fi_ref[...], br_ref[...], bi_ref[...]
        or_ref[...] = fr * br - fi * bi
        oi_ref[...] = fr * bi + fi * br

    o = jax.ShapeDtypeStruct((n, c), F32)
    return pl.pallas_call(body, name="s5_bbar_fwd", out_shape=(o, o),
                          compiler_params=_cparams())(f_re, f_im, b_re, b_im)


def _s5_bbar_bwd(f_re, f_im, b_re, b_im, dbr, dbi):
    n, c = b_re.shape

    def body(fr_ref, fi_ref, br_ref, bi_ref, dor_ref, doi_ref, dfr_ref, dfi_ref, dbr_ref, dbi_ref):
        fr, fi, br, bi = fr_ref[...], fi_ref[...], br_ref[...], bi_ref[...]
        dor, doi = dor_ref[...], doi_ref[...]
        dfr_ref[...] = jnp.sum(dor * br + doi * bi, axis=-1, keepdims=True)
        dfi_ref[...] = jnp.sum(doi * br - dor * bi, axis=-1, keepdims=True)
        dbr_ref[...] = fr * dor + fi * doi
        dbi_ref[...] = fr * doi - fi * dor

    col = jax.ShapeDtypeStruct((n, 1), F32)
    o = jax.ShapeDtypeStruct((n, c), F32)
    return pl.pallas_call(body, name="s5_bbar_bwd", out_shape=(col, col, o, o),
                          compiler_params=_cparams())(f_re, f_im, b_re, b_im, dbr, dbi)


SUBLANES = 8


def _scan_groups(xr, xi, ar, ai, reverse):
    t = xr.shape[0]
    sub = lax.broadcasted_iota(jnp.int32, (t, 1), 0) & (SUBLANES - 1)
    pr, pi = ar, ai
    for sh in (1, 2, 4):
        if reverse:
            keep = sub < SUBLANES - sh
            sr, si = pltpu.roll(xr, t - sh, 0), pltpu.roll(xi, t - sh, 0)
        else:
            keep = sub >= sh
            sr, si = pltpu.roll(xr, sh, 0), pltpu.roll(xi, sh, 0)
        sr = jnp.where(keep, sr, 0.0)
        si = jnp.where(keep, si, 0.0)
        xr, xi = xr + pr * sr - pi * si, xi + pr * si + pi * sr
        pr, pi = pr * pr - pi * pi, 2.0 * pr * pi
    return xr, xi


def _scan_rows(xr, xi, ar, ai, cr, ci, sr_ref, si_ref, reverse):
    t, n = xr.shape
    xr, xi = _scan_groups(xr, xi, ar, ai, reverse)
    sr_ref[...] = xr
    si_ref[...] = xi
    sub = lax.broadcasted_iota(jnp.int32, (SUBLANES, n), 0)
    seed = sub == (SUBLANES - 1 if reverse else 0)
    pwr, pwi = _scan_groups(jnp.where(seed, ar, 0.0), jnp.where(seed, ai, 0.0), ar, ai, reverse)
    groups = range(t // SUBLANES)
    edge = 0 if reverse else SUBLANES - 1
    for g in (reversed(groups) if reverse else groups):
        rows = slice(g * SUBLANES, (g + 1) * SUBLANES)
        vr = sr_ref[rows, :] + (pwr * cr - pwi * ci)
        vi = si_ref[rows, :] + (pwr * ci + pwi * cr)
        sr_ref[rows, :] = vr
        si_ref[rows, :] = vi
        cr, ci = vr[edge:edge + 1, :], vi[edge:edge + 1, :]
    return cr, ci


def _s5_states(u_bf, bbr, bbi, ar, ai, cr, ci, sr_ref, si_ref):
    return _scan_rows(_dot(u_bf, bbr), _dot(u_bf, bbi), ar, ai, cr, ci, sr_ref, si_ref, reverse=False)


def _s5_fwd(proj, bbr, bbi, a_re, a_im, ctr, cti, d_skip, d_s5):
    l = proj.shape[0]
    nb, uc, ns = bbr.shape
    t = min(S5_T, l)
    nt = l // t

    def body(u_ref, bbr_ref, bbi_ref, ar_ref, ai_ref, ctr_ref, cti_ref, d_ref,
             y_ref, car_r_ref, car_i_ref, sr_ref, si_ref, cr, ci):
        @pl.when(pl.program_id(1) == 0)
        def _():
            cr[...] = jnp.zeros_like(cr)
            ci[...] = jnp.zeros_like(ci)

        car_r_ref[...] = cr[...]
        car_i_ref[...] = ci[...]
        u = u_ref[...]
        cr[...], ci[...] = _s5_states(u.astype(BF16), bbr_ref[...], bbi_ref[...], ar_ref[...],
                                      ai_ref[...], cr[...], ci[...], sr_ref, si_ref)
        y_ref[...] = (_bdot(sr_ref[...], ctr_ref[...]) - _bdot(si_ref[...], cti_ref[...])
                      + d_ref[...] * u)

    per_block = lambda shape: pl.BlockSpec((None,) + shape, lambda b, n: (b, 0, 0))
    return pl.pallas_call(
        body, name="s5_fwd",
        out_shape=(jax.ShapeDtypeStruct((l, d_s5), F32),
                   jax.ShapeDtypeStruct((nt, 1, nb * ns), F32),
                   jax.ShapeDtypeStruct((nt, 1, nb * ns), F32),
                   jax.ShapeDtypeStruct((l, nb * ns), F32),
                   jax.ShapeDtypeStruct((l, nb * ns), F32)),
        grid=(nb, nt),
        in_specs=[pl.BlockSpec((t, uc), lambda b, n: (n, b)),
                  per_block((uc, ns)), per_block((uc, ns)),
                  per_block((1, ns)), per_block((1, ns)),
                  per_block((ns, uc)), per_block((ns, uc)),
                  pl.BlockSpec((1, uc), lambda b, n: (0, b))],
        out_specs=(pl.BlockSpec((t, uc), lambda b, n: (n, b)),
                   pl.BlockSpec((None, 1, ns), lambda b, n: (n, 0, b)),
                   pl.BlockSpec((None, 1, ns), lambda b, n: (n, 0, b)),
                   pl.BlockSpec((t, ns), lambda b, n: (n, b)),
                   pl.BlockSpec((t, ns), lambda b, n: (n, b))),
        scratch_shapes=[pltpu.VMEM((1, ns), F32), pltpu.VMEM((1, ns), F32)],
        compiler_params=_cparams("parallel", "arbitrary"),
    )(proj, bbr, bbi, a_re, a_im, ctr, cti, d_skip)


def _s5_bwd(proj, dy, bbr, bbi, a_re, a_im, cbr, cbi, d_skip, car_r, car_i, states_r, states_i):
    l, d_s5 = dy.shape
    nb, uc, ns = bbr.shape
    t = min(S5_T, l)
    nt = l // t

    def body(u_ref, dy_ref, bbr_ref, bbi_ref, ar_ref, ai_ref, cbr_ref, cbi_ref, d_ref,
             car_r_ref, car_i_ref, sr_ref, si_ref,
             du_ref, dar_ref, dai_ref, dbbr_ref, dbbi_ref, dcbr_ref, dcbi_ref, dd_ref, gcr, gci,
             gr_ref, gi_ref):
        @pl.when(pl.program_id(1) == 0)
        def _():
            gcr[...] = jnp.zeros_like(gcr)
            gci[...] = jnp.zeros_like(gci)
            for ref in (dar_ref, dai_ref, dbbr_ref, dbbi_ref, dcbr_ref, dcbi_ref, dd_ref):
                ref[...] = jnp.zeros_like(ref)

        row = lax.broadcasted_iota(jnp.int32, (t, 1), 0)
        u, dy = u_ref[...], dy_ref[...]
        u_bf, dy_bf = u.astype(BF16), dy.astype(BF16)
        ar, ai = ar_ref[...], ai_ref[...]
        cr, ci = car_r_ref[...], car_i_ref[...]
        sr, si = sr_ref[...], si_ref[...]
        first = row == 0
        pr = jnp.where(first, cr, pltpu.roll(sr, 1, 0))
        pi = jnp.where(first, ci, pltpu.roll(si, 1, 0))
        gcr[...], gci[...] = _scan_rows(_dot(dy_bf, cbr_ref[...]), -_dot(dy_bf, cbi_ref[...]), ar, -ai,
                                        gcr[...], gci[...], gr_ref, gi_ref, reverse=True)
        gr, gi = gr_ref[...], gi_ref[...]
        dar_ref[...] += jnp.sum(gr * pr + gi * pi, axis=0, keepdims=True)
        dai_ref[...] += jnp.sum(gi * pr - gr * pi, axis=0, keepdims=True)
        gr_bf, gi_bf = gr.astype(BF16), gi.astype(BF16)
        tn = ((0,), (0,))
        dbbr_ref[...] += _dot(u_bf, gr_bf, tn)
        dbbi_ref[...] += _dot(u_bf, gi_bf, tn)
        dcbr_ref[...] += _dot(dy_bf, sr.astype(BF16), tn)
        dcbi_ref[...] -= _dot(dy_bf, si.astype(BF16), tn)
        nt_dims = ((1,), (1,))
        du = _dot(gr_bf, bbr_ref[...], nt_dims) + _dot(gi_bf, bbi_ref[...], nt_dims) + dy * d_ref[...]
        du_ref[...] = du.astype(BF16)
        dd_ref[...] += jnp.sum(dy * u, axis=0, keepdims=True)

    rev = lambda n: nt - 1 - n
    per_block = lambda shape: pl.BlockSpec((None,) + shape, lambda b, n: (b, 0, 0))
    acc = jax.ShapeDtypeStruct((nb, uc, ns), F32)
    vec = jax.ShapeDtypeStruct((nb, 1, ns), F32)
    return pl.pallas_call(
        body, name="s5_bwd",
        out_shape=(jax.ShapeDtypeStruct((l, d_s5), BF16), vec, vec, acc, acc, acc, acc,
                   jax.ShapeDtypeStruct((1, d_s5), F32)),
        grid=(nb, nt),
        in_specs=[pl.BlockSpec((t, uc), lambda b, n: (rev(n), b)),
                  pl.BlockSpec((t, uc), lambda b, n: (rev(n), b)),
                  per_block((uc, ns)), per_block((uc, ns)),
                  per_block((1, ns)), per_block((1, ns)),
                  per_block((uc, ns)), per_block((uc, ns)),
                  pl.BlockSpec((1, uc), lambda b, n: (0, b)),
                  pl.BlockSpec((None, 1, ns), lambda b, n: (rev(n), 0, b)),
                  pl.BlockSpec((None, 1, ns), lambda b, n: (rev(n), 0, b)),
                  pl.BlockSpec((t, ns), lambda b, n: (rev(n), b)),
                  pl.BlockSpec((t, ns), lambda b, n: (rev(n), b))],
        out_specs=(pl.BlockSpec((t, uc), lambda b, n: (rev(n), b)),
                   per_block((1, ns)), per_block((1, ns)),
                   per_block((uc, ns)), per_block((uc, ns)),
                   per_block((uc, ns)), per_block((uc, ns)),
                   pl.BlockSpec((1, uc), lambda b, n: (0, b))),
        scratch_shapes=[pltpu.VMEM((1, ns), F32), pltpu.VMEM((1, ns), F32)]
        + [pltpu.VMEM((t, ns), F32)] * 2,
        compiler_params=_cparams("parallel", "arbitrary"),
    )(proj, dy, bbr, bbi, a_re, a_im, cbr, cbi, d_skip, car_r, car_i, states_r, states_i)


def _s5_glu_fwd(y1, proj, off_z, wglu):
    l, d = y1.shape

    def body(y_ref, z_ref, w_ref, o_ref):
        y2 = _gelu(y_ref[...])
        y3 = y2 * _sigmoid(_bdot(y2, w_ref[...]))
        o_ref[...] = (y3 * _silu(z_ref[...])).astype(BF16)

    return pl.pallas_call(
        body, name="s5_glu_fwd",
        out_shape=jax.ShapeDtypeStruct((l, d), BF16),
        grid=(l // ROW_TILE,),
        in_specs=[pl.BlockSpec((ROW_TILE, d), lambda i: (i, 0)),
                  pl.BlockSpec((ROW_TILE, d), lambda i: (i, off_z // d)),
                  pl.BlockSpec((d, d), lambda i: (0, 0))],
        out_specs=pl.BlockSpec((ROW_TILE, d), lambda i: (i, 0)),
        compiler_params=_cparams("parallel"),
    )(y1, proj, wglu)


def _s5_glu_bwd(y1, proj, off_z, wglu, dout):
    l, d = y1.shape

    def body(y_ref, z_ref, w_ref, do_ref, dy_ref, dz_ref, dw_ref):
        y2, gelu_vjp = jax.vjp(_gelu, y_ref[...])
        z = z_ref[...]
        sz, silu_vjp = jax.vjp(_silu, z)
        y2_bf = y2.astype(BF16)
        sg = _sigmoid(_dot(y2_bf, w_ref[...]))
        dout = do_ref[...]
        dy3 = dout * sz
        dz_ref[...] = silu_vjp(dout * (y2 * sg))[0].astype(BF16)
        dgl = (dy3 * y2 * sg * (1.0 - sg)).astype(BF16)
        dy2 = dy3 * sg + _dot(dgl, w_ref[...], ((1,), (1,)))
        dy_ref[...] = gelu_vjp(dy2)[0]

        @pl.when(pl.program_id(0) == 0)
        def _():
            dw_ref[...] = jnp.zeros_like(dw_ref)

        dw_ref[...] += _dot(y2_bf, dgl, ((0,), (0,)))

    row = pl.BlockSpec((ROW_TILE, d), lambda i: (i, 0))
    full = pl.BlockSpec((d, d), lambda i: (0, 0))
    return pl.pallas_call(
        body, name="s5_glu_bwd",
        out_shape=(jax.ShapeDtypeStruct((l, d), F32), jax.ShapeDtypeStruct((l, d), BF16),
                   jax.ShapeDtypeStruct((d, d), F32)),
        grid=(l // ROW_TILE,),
        in_specs=[row, pl.BlockSpec((ROW_TILE, d), lambda i: (i, off_z // d)), full, row],
        out_specs=(row, row, full),
        compiler_params=_cparams("arbitrary"),
    )(y1, proj, wglu, dout)


def _shift_rows(x, k, back=False):
    if k == 0:
        return x
    t = x.shape[0]
    row = lax.broadcasted_iota(jnp.int32, (t, 1), 0)
    if back:
        return jnp.where(row < t - k, pltpu.roll(x, t - k, 0), 0.0)
    return jnp.where(row >= k, pltpu.roll(x, k, 0), 0.0)


def _dn_conv(x, w_ref):
    return sum(w_ref[CONV_K - 1 - k:CONV_K - k, :] * _shift_rows(x, k) for k in range(CONV_K))


def _dn_post_conv(c, j):
    y = _silu(c)
    n = y * lax.rsqrt(jnp.sum(y * y, axis=-1, keepdims=True) + EPS)
    n = n * jnp.where(j < DN_HEADS, DN_HEAD_DIM ** -0.5, 1.0)
    return jnp.where(j < 2 * DN_HEADS, n, y)


def _dn_prep_fwd(proj, off_qkv, conv_w):
    l = proj.shape[0]
    hd = DN_HEAD_DIM
    nblk = 3 * DN_HEADS

    def body(x_ref, w_ref, o_ref):
        o_ref[...] = _dn_post_conv(_dn_conv(x_ref[...], w_ref), pl.program_id(0))

    return pl.pallas_call(
        body, name="dn_prep_fwd",
        out_shape=jax.ShapeDtypeStruct((l, nblk * hd), F32),
        grid=(nblk,),
        in_specs=[pl.BlockSpec((l, hd), lambda j: (0, off_qkv // hd + j)),
                  pl.BlockSpec((CONV_K, hd), lambda j: (0, j))],
        out_specs=pl.BlockSpec((l, hd), lambda j: (0, j)),
        compiler_params=_cparams("parallel"),
    )(proj, conv_w)


def _dn_prep_bwd(proj, off_qkv, conv_w, dqkv):
    l = proj.shape[0]
    hd = DN_HEAD_DIM
    nblk = 3 * DN_HEADS

    def body(x_ref, w_ref, do_ref, dx_ref, dw_ref):
        x = x_ref[...]
        j = pl.program_id(0)
        _, vjp = jax.vjp(functools.partial(_dn_post_conv, j=j), _dn_conv(x, w_ref))
        dc = vjp(do_ref[...])[0]
        dx = sum(w_ref[CONV_K - 1 - k:CONV_K - k, :] * _shift_rows(dc, k, back=True)
                 for k in range(CONV_K))
        dx_ref[...] = dx.astype(BF16)
        for k in range(CONV_K):
            dw_ref[CONV_K - 1 - k:CONV_K - k, :] = jnp.sum(dc * _shift_rows(x, k), axis=0,
                                                           keepdims=True)

    return pl.pallas_call(
        body, name="dn_prep_bwd",
        out_shape=(jax.ShapeDtypeStruct((l, nblk * hd), BF16),
                   jax.ShapeDtypeStruct((CONV_K, nblk * hd), F32)),
        grid=(nblk,),
        in_specs=[pl.BlockSpec((l, hd), lambda j: (0, off_qkv // hd + j)),
                  pl.BlockSpec((CONV_K, hd), lambda j: (0, j)),
                  pl.BlockSpec((None, l, hd), lambda j: (j // DN_HEADS, 0, j % DN_HEADS))],
        out_specs=(pl.BlockSpec((l, hd), lambda j: (0, j)),
                   pl.BlockSpec((CONV_K, hd), lambda j: (0, j))),
        compiler_params=_cparams("parallel"),
    )(proj, conv_w, dqkv)


def _dn_gate_fn(ba, a_log_row, dt_row):
    lane = lax.broadcasted_iota(jnp.int32, ba.shape, 1)
    beta = _sigmoid(ba)
    g = -jnp.exp(a_log_row) * _softplus(ba + dt_row)
    return jnp.where(lane < DN_HEADS, beta, jnp.where(lane < 2 * DN_HEADS, g, 0.0))


def _dn_gates_fwd(proj, off_ba, a_log_row, dt_row):
    l = proj.shape[0]
    row = pl.BlockSpec((ROW_TILE, 128), lambda i: (i, off_ba // 128))
    vec = pl.BlockSpec((1, 128), lambda i: (0, 0))

    def body(ba_ref, al_ref, dt_ref, o_ref):
        o_ref[...] = _dn_gate_fn(ba_ref[...], al_ref[...], dt_ref[...])

    return pl.pallas_call(
        body, name="dn_gates_fwd",
        out_shape=jax.ShapeDtypeStruct((l, 128), F32),
        grid=(l // ROW_TILE,),
        in_specs=[row, vec, vec],
        out_specs=pl.BlockSpec((ROW_TILE, 128), lambda i: (i, 0)),
        compiler_params=_cparams("parallel"),
    )(proj, a_log_row, dt_row)


def _dn_gates_bwd(proj, off_ba, a_log_row, dt_row, dgb_heads):
    l = proj.shape[0]
    nh = dgb_heads.shape[0]
    row = pl.BlockSpec((ROW_TILE, 128), lambda i: (i, off_ba // 128))
    vec = pl.BlockSpec((1, 128), lambda i: (0, 0))

    def body(ba_ref, al_ref, dt_ref, dg_ref, dba_ref, dal_ref, ddt_ref):
        _, vjp = jax.vjp(_dn_gate_fn, ba_ref[...], al_ref[...], dt_ref[...])
        dgb = dg_ref[0]
        for h in range(1, nh):
            dgb = dgb + dg_ref[h]
        dba, dal, ddt = vjp(dgb)
        dba_ref[...] = dba.astype(BF16)

        @pl.when(pl.program_id(0) == 0)
        def _():
            dal_ref[...] = jnp.zeros_like(dal_ref)
            ddt_ref[...] = jnp.zeros_like(ddt_ref)

        dal_ref[...] += dal
        ddt_ref[...] += ddt

    return pl.pallas_call(
        body, name="dn_gates_bwd",
        out_shape=(jax.ShapeDtypeStruct((l, 128), BF16), jax.ShapeDtypeStruct((1, 128), F32),
                   jax.ShapeDtypeStruct((1, 128), F32)),
        grid=(l // ROW_TILE,),
        in_specs=[row, vec, vec, pl.BlockSpec((nh, ROW_TILE, 128), lambda i: (0, i, 0))],
        out_specs=(pl.BlockSpec((ROW_TILE, 128), lambda i: (i, 0)), vec, vec),
        compiler_params=_cparams("arbitrary"),
    )(proj, a_log_row, dt_row, dgb_heads)


@jax.custom_vjp
def _unit_lower_inverses(a_mats):
    c = a_mats[0].shape[0]
    eye = (lax.broadcasted_iota(jnp.int32, (c, c), 0) == lax.broadcasted_iota(jnp.int32, (c, c), 1)).astype(F32)
    t_inv = [eye - a for a in a_mats]
    power = a_mats
    for _ in range(int(math.log2(c)) - 1):
        power = [_bdot(p, p) for p in power]
        t_inv = [t + _bdot(t, p) for t, p in zip(t_inv, power)]
    return t_inv


def _unit_lower_inverses_fwd(a_mats):
    t_inv = _unit_lower_inverses(a_mats)
    return t_inv, t_inv


def _inverse_cotangents(t_inv, grads):
    right = [_dot3_dims(g, t, ((1,), (1,))) for g, t in zip(grads, t_inv)]
    return [-_dot3_dims(t, r, ((0,), (0,))) for t, r in zip(t_inv, right)]


_unit_lower_inverses.defvjp(_unit_lower_inverses_fwd,
                            lambda t_inv, grads: (_inverse_cotangents(t_inv, grads),))


@jax.custom_vjp
def _kept_inverses(a_mats, t_inv):
    return t_inv


_kept_inverses.defvjp(
    lambda a_mats, t_inv: (t_inv, t_inv),
    lambda t_inv, grads: (_inverse_cotangents(t_inv, grads), [jnp.zeros_like(t) for t in t_inv]))


def _dn_chunk_fn(states, qs, ks, vs, gb, heads, kept_inverses=None, return_inverses=False):
    c = qs[0].shape[0]
    each = lambda f, *lists: [f(*args) for args in zip(*lists)]
    lane = lax.broadcasted_iota(jnp.int32, gb.shape, 1)
    ri = lax.broadcasted_iota(jnp.int32, (c, c), 0)
    ci = lax.broadcasted_iota(jnp.int32, (c, c), 1)
    causal, strict = ri >= ci, ri > ci
    eye = (ri == ci).astype(F32)
    rowi = lax.broadcasted_iota(jnp.int32, (c, 1), 0)
    nt_dims = ((1,), (1,))
    hdot = functools.partial(_dot, precision=HIGHEST)

    pick = lambda m, at: jnp.sum(jnp.where(lane == at, m, 0.0), axis=1, keepdims=True)
    gb_cum = hdot(causal.astype(F32), gb)
    beta = [pick(gb, h) for h in heads]
    gc = [pick(gb_cum, h + DN_HEADS) for h in heads]
    gc_row = each(lambda g: jnp.sum(eye * g, axis=0, keepdims=True), gc)
    decay = each(lambda g, gr: jnp.where(causal, jnp.exp(jnp.where(causal, g - gr, 0.0)), 0.0),
                 gc, gc_row)
    kk = each(lambda k: _bdot(k, k, nt_dims), ks)
    a_mat = each(lambda b, m, dc: jnp.where(strict, b * m * dc, 0.0), beta, kk, decay)

    t_inv = (_unit_lower_inverses(a_mat) if kept_inverses is None
             else _kept_inverses(a_mat, kept_inverses))
    egc = each(jnp.exp, gc)
    u_c = each(lambda t, v, b: _dot3(t, v * b), t_inv, vs, beta)
    w_c = each(lambda t, k, b, e: _dot3(t, k * (b * e)), t_inv, ks, beta, egc)
    qk = each(lambda q, k, dc: _bdot(q, k, nt_dims) * dc, qs, ks, decay)
    g_end = each(lambda g: jnp.sum(jnp.where(rowi == c - 1, g, 0.0), axis=0, keepdims=True), gc)
    v_new = each(lambda u, w, s: u - _bdot(w, s), u_c, w_c, states)
    o = each(lambda q, e, s, m, vn: _bdot(q * e, s) + _bdot(m, vn), qs, egc, states, qk, v_new)
    new_states = each(
        lambda s, ge, k, g, vn: s * jnp.exp(ge) + _bdot(k * jnp.exp(ge - g), vn, ((0,), (0,))),
        states, g_end, ks, gc, v_new)
    return (o, new_states, t_inv) if return_inverses else (o, new_states)


def _dn_chunk_specs(order):
    hd, nh, hps = DN_HEAD_DIM, DN_HEADS, DN_HEADS_PER_STEP
    qkv = lambda part: pl.BlockSpec((CHUNK, hps * hd), lambda h, n: (order(n), part * (nh // hps) + h))
    gb = pl.BlockSpec((CHUNK, 128), lambda h, n: (order(n), 0))
    state = pl.BlockSpec((hps, None, hd, hd), lambda h, n: (h, order(n), 0, 0))
    inverse = pl.BlockSpec((hps, None, CHUNK, CHUNK), lambda h, n: (h, order(n), 0, 0))
    return qkv, gb, state, inverse


def _dn_chunk_fwd(qkv, gb):
    l = qkv.shape[0]
    hd, nh, hps = DN_HEAD_DIM, DN_HEADS, DN_HEADS_PER_STEP
    n_chunks = l // CHUNK
    qkv_spec, gb_spec, state_spec, inverse_spec = _dn_chunk_specs(lambda n: n)

    def body(q_ref, k_ref, v_ref, gb_ref, o_ref, s_ref, t_ref, state):
        @pl.when(pl.program_id(1) == 0)
        def _():
            state[...] = jnp.zeros_like(state)

        cols = [slice(i * hd, (i + 1) * hd) for i in range(hps)]
        states = [state[i] for i in range(hps)]
        for i in range(hps):
            s_ref[i] = states[i]
        o, new_states, t_inv = _dn_chunk_fn(
            states, [q_ref[:, cs] for cs in cols], [k_ref[:, cs] for cs in cols],
            [v_ref[:, cs] for cs in cols], gb_ref[...],
            [pl.program_id(0) * hps + i for i in range(hps)], return_inverses=True)
        for i in range(hps):
            o_ref[:, cols[i]] = o[i]
            t_ref[i] = t_inv[i]
            state[i] = new_states[i]

    return pl.pallas_call(
        body, name="dn_chunk_fwd",
        out_shape=(jax.ShapeDtypeStruct((l, nh * hd), F32),
                   jax.ShapeDtypeStruct((nh, n_chunks, hd, hd), F32),
                   jax.ShapeDtypeStruct((nh, n_chunks, CHUNK, CHUNK), F32)),
        grid=(nh // hps, n_chunks),
        in_specs=[qkv_spec(0), qkv_spec(1), qkv_spec(2), gb_spec],
        out_specs=(pl.BlockSpec((CHUNK, hps * hd), lambda h, n: (n, h)), state_spec, inverse_spec),
        scratch_shapes=[pltpu.VMEM((hps, hd, hd), F32)],
        compiler_params=_cparams("parallel", "arbitrary"),
    )(qkv, qkv, qkv, gb)


def _dn_chunk_bwd(qkv, gb, states, inverses, do):
    l = qkv.shape[0]
    hd, nh, hps = DN_HEAD_DIM, DN_HEADS, DN_HEADS_PER_STEP
    n_chunks = l // CHUNK
    rev = lambda n: n_chunks - 1 - n
    qkv_spec, gb_spec, state_spec, inverse_spec = _dn_chunk_specs(rev)

    def body(q_ref, k_ref, v_ref, gb_ref, s_ref, t_ref, do_ref, dqkv_ref, dgb_ref, dstate):
        @pl.when(pl.program_id(1) == 0)
        def _():
            dstate[...] = jnp.zeros_like(dstate)

        cols = [slice(i * hd, (i + 1) * hd) for i in range(hps)]
        fn = functools.partial(_dn_chunk_fn, heads=[pl.program_id(0) * hps + i for i in range(hps)],
                               kept_inverses=[t_ref[i] for i in range(hps)])
        _, vjp = jax.vjp(fn, [s_ref[i] for i in range(hps)], [q_ref[:, cs] for cs in cols],
                         [k_ref[:, cs] for cs in cols], [v_ref[:, cs] for cs in cols], gb_ref[...])
        ds, dq, dk, dv, dgb = vjp(([do_ref[:, cs] for cs in cols], [dstate[i] for i in range(hps)]))
        for i in range(hps):
            dstate[i] = ds[i]
            dqkv_ref[0, :, cols[i]] = dq[i]
            dqkv_ref[1, :, cols[i]] = dk[i]
            dqkv_ref[2, :, cols[i]] = dv[i]
        dgb_ref[...] = dgb

    head_out = pl.BlockSpec((CHUNK, hps * hd), lambda h, n: (rev(n), h))
    return pl.pallas_call(
        body, name="dn_chunk_bwd",
        out_shape=(jax.ShapeDtypeStruct((3, l, nh * hd), F32),
                   jax.ShapeDtypeStruct((nh // hps, l, 128), F32)),
        grid=(nh // hps, n_chunks),
        in_specs=[qkv_spec(0), qkv_spec(1), qkv_spec(2), gb_spec, state_spec, inverse_spec, head_out],
        out_specs=(pl.BlockSpec((3, CHUNK, hps * hd), lambda h, n: (0, rev(n), h)),
                   pl.BlockSpec((None, CHUNK, 128), lambda h, n: (h, rev(n), 0))),
        scratch_shapes=[pltpu.VMEM((hps, hd, hd), F32)],
        compiler_params=_cparams("parallel", "arbitrary"),
    )(qkv, qkv, qkv, gb, states, inverses, do)


def _dn_out_fn(o, z, w):
    return _rmsnorm(o, w) * _silu(z)


def _dn_out_fwd(o, proj, off_z, w):
    l, d = o.shape
    hd = DN_HEAD_DIM
    tr = min(4 * ROW_TILE, l)
    blk = lambda off: pl.BlockSpec((tr, hd), lambda i, h: (i, off // hd + h))

    def body(o_ref, z_ref, w_ref, out_ref):
        out_ref[...] = _dn_out_fn(o_ref[...], z_ref[...], w_ref[...]).astype(BF16)

    return pl.pallas_call(
        body, name="dn_out_fwd",
        out_shape=jax.ShapeDtypeStruct((l, d), BF16),
        grid=(l // tr, d // hd),
        in_specs=[blk(0), blk(off_z), pl.BlockSpec((1, hd), lambda i, h: (0, 0))],
        out_specs=blk(0),
        compiler_params=_cparams("parallel", "parallel"),
    )(o, proj, w)


def _dn_out_bwd(o, proj, off_z, w, dout):
    l, d = o.shape
    hd = DN_HEAD_DIM
    tr = min(4 * ROW_TILE, l)
    blk = lambda off: pl.BlockSpec((tr, hd), lambda i, h: (i, off // hd + h))
    vec = pl.BlockSpec((1, hd), lambda i, h: (0, 0))

    def body(o_ref, z_ref, w_ref, dout_ref, do_ref, dz_ref, dw_ref):
        _, vjp = jax.vjp(_dn_out_fn, o_ref[...], z_ref[...], w_ref[...])
        do, dz, dw = vjp(dout_ref[...])
        do_ref[...] = do
        dz_ref[...] = dz.astype(BF16)

        @pl.when((pl.program_id(0) == 0) & (pl.program_id(1) == 0))
        def _():
            dw_ref[...] = jnp.zeros_like(dw_ref)

        dw_ref[...] += dw

    return pl.pallas_call(
        body, name="dn_out_bwd",
        out_shape=(jax.ShapeDtypeStruct((l, d), F32), jax.ShapeDtypeStruct((l, d), BF16),
                   jax.ShapeDtypeStruct((1, hd), F32)),
        grid=(l // tr, d // hd),
        in_specs=[blk(0), blk(off_z), vec, blk(0)],
        out_specs=(blk(0), blk(0), vec),
        compiler_params=_cparams("arbitrary", "arbitrary"),
    )(o, proj, w, dout)


def _tile_2d(rows, cols, budget_bytes=1 << 20):
    for tr in (rows, 4096, 2048, 1024, 512, 256, 128, 64, 32, 16):
        if tr <= rows and rows % tr == 0 and tr * cols * 4 <= budget_bytes:
            return tr, cols
    for tc in (2048, 1024, 512, 256, 128):
        if cols % tc == 0 and rows * tc * 4 <= 2 * budget_bytes:
            return rows, tc
    raise ValueError((rows, cols))


def _adamw_update(g, w_ref, m_ref, v_ref, go_ref, d_ref, mo_ref, vo_ref):
    c1 = 1.0 / (1.0 - ADAM_B1 ** ADAM_STEP)
    c2 = 1.0 / (1.0 - ADAM_B2 ** ADAM_STEP)
    m_new = ADAM_B1 * m_ref[...] + (1.0 - ADAM_B1) * g
    v_new = ADAM_B2 * v_ref[...] + (1.0 - ADAM_B2) * (g * g)
    go_ref[...] = g
    mo_ref[...] = m_new
    vo_ref[...] = v_new
    d_ref[...] = -ADAM_LR * ((m_new * c1) / (jnp.sqrt(v_new * c2) + ADAM_EPS) + ADAM_WD * w_ref[...])


def _adamw(w, m, v, gslots, name):
    rows, cols = w.shape
    ns = gslots.shape[0]
    tr, tc = _tile_2d(rows, cols)

    def body(w_ref, m_ref, v_ref, g_ref, go_ref, d_ref, mo_ref, vo_ref):
        g = g_ref[0].astype(F32)
        for s in range(1, ns):
            g = g + g_ref[s].astype(F32)
        _adamw_update(g, w_ref, m_ref, v_ref, go_ref, d_ref, mo_ref, vo_ref)

    blk = pl.BlockSpec((tr, tc), lambda i, j: (i, j))
    o = jax.ShapeDtypeStruct((rows, cols), F32)
    return pl.pallas_call(
        body, name=name, out_shape=(o, o, o, o),
        grid=(rows // tr, cols // tc),
        in_specs=[blk, blk, blk, pl.BlockSpec((ns, tr, tc), lambda i, j: (0, i, j))],
        out_specs=(blk, blk, blk, blk),
        compiler_params=_cparams("parallel", "parallel"),
    )(w, m, v, gslots)


def _slot_sum(gslots, name):
    ns, rows, cols = gslots.shape
    tr, tc = _tile_2d(rows, cols)

    def body(g_ref, o_ref):
        g = g_ref[0]
        for s in range(1, ns):
            g = g + g_ref[s]
        o_ref[...] = g

    return pl.pallas_call(
        body, name=name, out_shape=jax.ShapeDtypeStruct((rows, cols), F32),
        grid=(rows // tr, cols // tc),
        in_specs=[pl.BlockSpec((ns, tr, tc), lambda i, j: (0, i, j))],
        out_specs=pl.BlockSpec((tr, tc), lambda i, j: (i, j)),
        compiler_params=_cparams("parallel", "parallel"),
    )(gslots)


HBM_SPEC = pl.BlockSpec(memory_space=pl.ANY)


def _all_gather(arrs, name, relayed=(), after=None):
    n = len(arrs)
    n_sems = 13
    n_in = n + (after is not None)

    def body(*refs):
        ins, outs = refs[:n], refs[n_in:n_in + n]
        send_sems, recv_sems, local_sems = refs[n_in + n:]
        x, y, c = lax.axis_index("x"), lax.axis_index("y"), lax.axis_index("c")
        me, sibling = (x, y, c), (x, y, 1 - c)
        chips = [(1 - x, y), (x, 1 - y), (1 - x, 1 - y)]
        index = lambda px, py, pc: 4 * px + 2 * py + pc

        def copy(a, k, block, to, src=None, cols=None):
            dst = outs[a].at[index(*block)]
            src = dst if src is None else src
            if cols is not None:
                dst, src = dst.at[:, cols], src.at[:, cols]
            return pltpu.make_async_remote_copy(
                src_ref=src, dst_ref=dst, send_sem=send_sems.at[a, k], recv_sem=recv_sems.at[a, k],
                device_id=to, device_id_type=MESH)

        mine = [pltpu.make_async_copy(ins[a], outs[a].at[index(*me)], local_sems.at[a])
                for a in range(n)]
        for cp in mine:
            cp.start()
        sends = []

        def start(cp):
            cp.start()
            sends.append(cp)

        halves = {a: (pl.ds(0, arrs[a].shape[1] // 2), pl.ds(arrs[a].shape[1] // 2, arrs[a].shape[1] // 2))
                  for a in relayed}
        near_x, near_y, far = [(*chip, c) for chip in chips]
        for a in range(n):
            start(copy(a, 0, me, sibling, src=ins[a]))
            if a in relayed:
                left, right = halves[a]
                for k, to, cols in ((1, near_x, left), (3, near_y, right), (2, near_x, right), (4, near_y, left)):
                    start(copy(a, k, me, to, src=ins[a], cols=cols))
            else:
                for j, chip in enumerate(chips):
                    start(copy(a, 1 + j, me, (*chip, c), src=ins[a]))
        for a in relayed:
            left, right = halves[a]
            for k, block, cols, onward, to_sibling in (
                    (1, near_x, left, (5, near_y), 7), (3, near_y, right, (6, near_x), 10),
                    (2, near_x, right, None, 8), (4, near_y, left, None, 9),
                    (5, far, left, None, 11), (6, far, right, None, 12)):
                copy(a, k, block, me, cols=cols).wait_recv()
                if onward is not None:
                    start(copy(a, onward[0], block, onward[1], cols=cols))
                start(copy(a, to_sibling, block, sibling, cols=cols))
        for j, chip in enumerate(chips):
            for a in range(n):
                if a not in relayed:
                    copy(a, 1 + j, (*chip, c), me).wait_recv()
                    start(copy(a, 4 + j, (*chip, c), sibling))
        for a in range(n):
            copy(a, 0, sibling, me).wait_recv()
            if a in relayed:
                left, right = halves[a]
                for k, chip, cols in ((7, chips[0], left), (8, chips[0], right), (9, chips[1], left),
                                      (10, chips[1], right), (11, chips[2], left), (12, chips[2], right)):
                    copy(a, k, (*chip, 1 - c), me, cols=cols).wait_recv()
            else:
                for j, chip in enumerate(chips):
                    copy(a, 4 + j, (*chip, 1 - c), me).wait_recv()
        for cp in sends:
            cp.wait_send()
        for cp in mine:
            cp.wait()

    return pl.pallas_call(
        body, name=name,
        out_shape=[jax.ShapeDtypeStruct((N_DEV,) + a.shape, a.dtype) for a in arrs],
        in_specs=[HBM_SPEC] * n_in, out_specs=[HBM_SPEC] * n,
        scratch_shapes=[pltpu.SemaphoreType.DMA((n, n_sems)), pltpu.SemaphoreType.DMA((n, n_sems)),
                        pltpu.SemaphoreType.DMA((n,))],
    )(*arrs, *([after] if after is not None else []))


def _sibling_swap(arrs, name):
    n = len(arrs)

    def body(*refs):
        ins, outs = refs[:n], refs[n:2 * n]
        send_sems, recv_sems = refs[2 * n:]
        x, y, c = lax.axis_index("x"), lax.axis_index("y"), lax.axis_index("c")
        copies = [pltpu.make_async_remote_copy(
            src_ref=ins[a].at[:, 1 - c], dst_ref=outs[a],
            send_sem=send_sems.at[a], recv_sem=recv_sems.at[a],
            device_id=(x, y, 1 - c), device_id_type=MESH) for a in range(n)]
        for cp in copies:
            cp.start()
        for cp in copies:
            cp.wait()

    return pl.pallas_call(
        body, name=name,
        out_shape=[jax.ShapeDtypeStruct(a.shape[:1] + a.shape[2:], a.dtype) for a in arrs],
        in_specs=[HBM_SPEC] * n, out_specs=[HBM_SPEC] * n,
        scratch_shapes=[pltpu.SemaphoreType.DMA((n,)), pltpu.SemaphoreType.DMA((n,))],
    )(*arrs)


def _pair_sum(mine, theirs, core, name):
    chips, _, rows, cols = mine.shape
    tr, tc = _tile_2d(rows, cols, budget_bytes=2 << 20)

    def body(core_ref, a_ref, b_ref, o_ref):
        o_ref[...] = (a_ref[...].astype(F32) + b_ref[...].astype(F32)).astype(o_ref.dtype)

    slab = pl.BlockSpec((None, tr, tc), lambda ch, i, j, core_ref: (ch, i, j))
    return pl.pallas_call(
        body, name=name, out_shape=jax.ShapeDtypeStruct((chips, rows, cols), mine.dtype),
        grid_spec=pltpu.PrefetchScalarGridSpec(
            num_scalar_prefetch=1, grid=(chips, rows // tr, cols // tc),
            in_specs=[pl.BlockSpec((None, None, tr, tc),
                                   lambda ch, i, j, core_ref: (ch, core_ref[0], i, j)), slab],
            out_specs=slab),
        compiler_params=_cparams("parallel", "parallel", "parallel"),
    )(core, mine, theirs)


HBM_ONLY = pl.BlockSpec(memory_space=pltpu.HBM)
SEM_SPEC = pl.BlockSpec(memory_space=pltpu.SEMAPHORE)
SPLIT_COPY_EFFECT = pltpu.SideEffectType.DATAFLOW_SIDE_EFFECTING


def _flip(v, bit):
    return 1 - v if bit else v


def _chip_slices_plan(n):
    def plan():
        x, y, c = lax.axis_index("x"), lax.axis_index("y"), lax.axis_index("c")
        copies = []
        for k in range(1, 4):
            px, py = _flip(x, k & 2), _flip(y, k & 1)
            copies += [(a, 2 * px + py, 2 * x + y, (px, py, c)) for a in range(n)]
        return copies
    return plan, 3 * n


def _gather_plan(n):
    def plan():
        x, y, c = lax.axis_index("x"), lax.axis_index("y"), lax.axis_index("c")
        copies = []
        for k in range(1, N_DEV):
            peer = (_flip(x, k & 4), _flip(y, k & 2), _flip(c, k & 1))
            copies += [(a, None, 4 * x + 2 * y + c, peer) for a in range(n)]
        return copies
    return plan, 7 * n


def _planned_copies(plan, srcs, lands, send_sems, recv_sems):
    return [pltpu.make_async_remote_copy(
        src_ref=srcs[a] if src_at is None else srcs[a].at[src_at], dst_ref=lands[a].at[land_at],
        send_sem=send_sems[i], recv_sem=recv_sems[i], device_id=peer, device_id_type=MESH)
        for i, (a, src_at, land_at, peer) in enumerate(plan())]


def _split_exchange_start(plan_and_count, arrs, land_shapes, name, after=None):
    plan, n_sems = plan_and_count
    n = len(arrs)

    n_in = 2 * n + (after is not None)

    def body(*refs):
        srcs, lands = refs[:n], refs[n:2 * n]
        send_sems, recv_sems = refs[n_in:n_in + n_sems], refs[n_in + n_sems:n_in + 2 * n_sems]
        token = refs[-1]
        for copy in _planned_copies(plan, srcs, lands, send_sems, recv_sems):
            copy.start()
        token[...] = jnp.zeros_like(token)

    hbm = lambda a: pltpu.HBM(a.shape, a.dtype)
    operands = [pltpu.with_memory_space_constraint(a, pltpu.HBM) for a in arrs]
    operands += [pltpu.with_memory_space_constraint(lax.empty(shape, a.dtype), pltpu.HBM)
                 for a, shape in zip(arrs, land_shapes)]
    out = pl.pallas_call(
        body, name=name,
        out_shape=(*[pltpu.SemaphoreType.DMA(())] * (2 * n_sems),
                   *[hbm(a) for a in operands],
                   jax.ShapeDtypeStruct((8, 128), F32)),
        in_specs=[HBM_ONLY] * (2 * n) + [pl.BlockSpec(memory_space=pl.ANY)] * (after is not None),
        out_specs=(*[SEM_SPEC] * (2 * n_sems), *[HBM_ONLY] * (2 * n),
                   pl.BlockSpec(memory_space=pltpu.VMEM)),
        input_output_aliases={i: 2 * n_sems + i for i in range(2 * n)},
        compiler_params=pltpu.CompilerParams(has_side_effects=SPLIT_COPY_EFFECT),
    )(*operands, *([after] if after is not None else []))
    sems, rest = list(out[:2 * n_sems]), out[2 * n_sems:]
    return sems, list(rest[:n]), list(rest[n:2 * n]), rest[-1]


def _split_exchange_wait(plan_and_count, sems, srcs, lands, after, name):
    plan, n_sems = plan_and_count
    n = len(srcs)

    def body(*refs):
        src_refs, land_refs = refs[:n], refs[n:2 * n]
        send_sems, recv_sems = refs[2 * n:2 * n + n_sems], refs[2 * n + n_sems:2 * n + 2 * n_sems]
        for copy in _planned_copies(plan, src_refs, land_refs, send_sems, recv_sems):
            copy.wait_send()
            copy.wait_recv()

    hbm = lambda a: pltpu.HBM(a.shape, a.dtype)
    out = pl.pallas_call(
        body, name=name,
        out_shape=(*[hbm(a) for a in srcs], *[hbm(a) for a in lands]),
        in_specs=[HBM_ONLY] * (2 * n) + [SEM_SPEC] * (2 * n_sems) + [pl.BlockSpec(memory_space=pl.ANY)],
        out_specs=tuple([HBM_ONLY] * (2 * n)),
        input_output_aliases={i: i for i in range(2 * n)},
        compiler_params=pltpu.CompilerParams(has_side_effects=SPLIT_COPY_EFFECT),
    )(*srcs, *lands, *sems, after)
    return list(out[n:])


def _adamw_exchanged(w, m, v, own, landed, chip, name):
    rows, cols = w.shape
    tr, tc = _tile_2d(rows, cols)

    def body(chip_ref, w_ref, m_ref, v_ref, own_ref, l1_ref, l2_ref, l3_ref, go_ref, d_ref, mo_ref, vo_ref):
        g = own_ref[...].astype(F32)
        for ref in (l1_ref, l2_ref, l3_ref):
            g = g + ref[...].astype(F32)
        _adamw_update(g, w_ref, m_ref, v_ref, go_ref, d_ref, mo_ref, vo_ref)

    blk = pl.BlockSpec((tr, tc), lambda i, j, chip_ref: (i, j))
    slot = lambda k: pl.BlockSpec((None, tr, tc), lambda i, j, chip_ref: (chip_ref[0] ^ k, i, j))
    o = jax.ShapeDtypeStruct((rows, cols), F32)
    return pl.pallas_call(
        body, name=name, out_shape=(o, o, o, o),
        grid_spec=pltpu.PrefetchScalarGridSpec(
            num_scalar_prefetch=1, grid=(rows // tr, cols // tc),
            in_specs=[blk, blk, blk, slot(0), slot(1), slot(2), slot(3)],
            out_specs=(blk, blk, blk, blk)),
        compiler_params=_cparams("parallel", "parallel"),
    )(chip, w, m, v, own, landed, landed, landed)


def _block_diag(t):
    nb, gpb, r, c = t.shape
    eye = jnp.eye(gpb, dtype=t.dtype)
    return jnp.einsum("ngrc,gh->ngrhc", t, eye).reshape(nb, gpb * r, gpb * c)


def _diag_blocks(t, r, c):
    nb = t.shape[0]
    gpb = t.shape[1] // r
    t = t.reshape(nb, gpb, r, gpb, c)
    return jnp.einsum("ngrhc,gh->ngrc", t, jnp.eye(gpb, dtype=t.dtype))


def _pack_rows(parts):
    flat = jnp.concatenate([p.reshape(-1).astype(F32) for p in parts])
    pad = (-flat.shape[0]) % (256 * 128)
    return jnp.pad(flat, (0, pad)).reshape(-1, 128)


def _unpack_rows(packed, shapes):
    flat = packed.reshape(-1)
    out, at = [], 0
    for shape in shapes:
        size = math.prod(shape)
        out.append(flat[at:at + size].reshape(shape))
        at += size
    return out


def kernel(x, ln_w, w_in, s5_lam_re, s5_lam_im, s5_log_step, s5_b_re, s5_b_im, s5_c_re, s5_c_im, s5_d, s5_w_glu, s5_w_up, dn_conv_w, dn_a_log, dn_dt_bias, dn_norm_w, dn_w_up, w_out, final_norm_w, loss_target, m_ln_w, m_w_in, m_s5_lam_re, m_s5_lam_im, m_s5_log_step, m_s5_b_re, m_s5_b_im, m_s5_c_re, m_s5_c_im, m_s5_d, m_s5_w_glu, m_s5_w_up, m_dn_conv_w, m_dn_a_log, m_dn_dt_bias, m_dn_norm_w, m_dn_w_up, m_w_out, m_final_norm_w, v_ln_w, v_w_in, v_s5_lam_re, v_s5_lam_im, v_s5_log_step, v_s5_b_re, v_s5_b_im, v_s5_c_re, v_s5_c_im, v_s5_d, v_s5_w_glu, v_s5_w_up, v_dn_conv_w, v_dn_a_log, v_dn_dt_bias, v_dn_norm_w, v_dn_w_up, v_w_out, v_final_norm_w):
    weights = dict(ln_w=ln_w, w_in=w_in, s5_lam_re=s5_lam_re, s5_lam_im=s5_lam_im,
                   s5_log_step=s5_log_step, s5_b_re=s5_b_re, s5_b_im=s5_b_im, s5_c_re=s5_c_re,
                   s5_c_im=s5_c_im, s5_d=s5_d, s5_w_glu=s5_w_glu, s5_w_up=s5_w_up,
                   dn_conv_w=dn_conv_w, dn_a_log=dn_a_log, dn_dt_bias=dn_dt_bias,
                   dn_norm_w=dn_norm_w, dn_w_up=dn_w_up, w_out=w_out, final_norm_w=final_norm_w)
    mom_m = dict(ln_w=m_ln_w, w_in=m_w_in, s5_lam_re=m_s5_lam_re, s5_lam_im=m_s5_lam_im,
                 s5_log_step=m_s5_log_step, s5_b_re=m_s5_b_re, s5_b_im=m_s5_b_im,
                 s5_c_re=m_s5_c_re, s5_c_im=m_s5_c_im, s5_d=m_s5_d, s5_w_glu=m_s5_w_glu,
                 s5_w_up=m_s5_w_up, dn_conv_w=m_dn_conv_w, dn_a_log=m_dn_a_log,
                 dn_dt_bias=m_dn_dt_bias, dn_norm_w=m_dn_norm_w, dn_w_up=m_dn_w_up,
                 w_out=m_w_out, final_norm_w=m_final_norm_w)
    mom_v = dict(ln_w=v_ln_w, w_in=v_w_in, s5_lam_re=v_s5_lam_re, s5_lam_im=v_s5_lam_im,
                 s5_log_step=v_s5_log_step, s5_b_re=v_s5_b_re, s5_b_im=v_s5_b_im,
                 s5_c_re=v_s5_c_re, s5_c_im=v_s5_c_im, s5_d=v_s5_d, s5_w_glu=v_s5_w_glu,
                 s5_w_up=v_s5_w_up, dn_conv_w=v_dn_conv_w, dn_a_log=v_dn_a_log,
                 dn_dt_bias=v_dn_dt_bias, dn_norm_w=v_dn_norm_w, dn_w_up=v_dn_w_up,
                 w_out=v_w_out, final_norm_w=v_final_norm_w)
    names = list(weights)

    l, d = x.shape[1], x.shape[2]
    d_s5 = d // 2
    groups = d_s5 // S5_GROUP
    nb = groups // S5_GPB
    d_dn = DN_HEADS * DN_HEAD_DIM
    w_in_cols = w_in.shape[2]
    d_in = N_DEV * w_in_cols
    off_ba_src = 2 * d_s5 + 4 * d_dn
    off_u, off_zs, off_qkv, off_zd = 0, d_s5, 2 * d_s5, 2 * d_s5 + 3 * d_dn
    off_ba = off_zd + d_dn
    n_main = off_ba + BA_PAD
    off_gs, off_gd = 0, d
    x2d, tgt2d = x[0], loss_target[0]
    my_index = 4 * lax.axis_index("x") + 2 * lax.axis_index("y") + lax.axis_index("c")

    g_win, g_conv = _all_gather([jnp.transpose(w_in[0]).astype(BF16), dn_conv_w[0]], name="gather_weights",
                                relayed=(0,))
    late_plan = _gather_plan(4)
    late_shards = [s5_w_glu[0].astype(BF16), s5_w_up[0].astype(BF16), dn_w_up[0].astype(BF16),
                   w_out[0].astype(BF16)]
    late_sems, late_shards, late_lands, late_token = _split_exchange_start(
        late_plan, late_shards, [(N_DEV,) + s.shape for s in late_shards], name="gather_late_start",
        after=g_conv)
    ba_end = off_ba_src + 2 * DN_HEADS
    w_full_t = g_win.reshape(d_in, d)
    w_gates_t = w_full_t[ba_end:]
    conv_full = jnp.transpose(g_conv, (1, 0, 2)).reshape(CONV_K, 3 * d_dn)

    lam_re, lam_im = s5_lam_re[0], s5_lam_im[0]
    log_step = s5_log_step[0].reshape(groups, 1)
    b_re = s5_b_re[0].reshape(groups * S5_STATE, S5_GROUP)
    b_im = s5_b_im[0].reshape(groups * S5_STATE, S5_GROUP)
    abar_re, abar_im, f_re, f_im = _s5_disc_fwd(lam_re, lam_im, log_step)
    f_re_col, f_im_col = f_re.reshape(-1, 1), f_im.reshape(-1, 1)
    bb_re, bb_im = _s5_bbar_fwd(f_re_col, f_im_col, b_re, b_im)

    def bb_blocks(t):
        t = t.reshape(nb, S5_GPB, S5_STATE, S5_GROUP).transpose(0, 1, 3, 2)
        return _block_diag(t).astype(BF16)

    def c_blocks(t):
        return _block_diag(t.reshape(nb, S5_GPB, S5_GROUP, S5_STATE)).astype(BF16)

    bbr, bbi = bb_blocks(bb_re), bb_blocks(bb_im)
    cbr, cbi = c_blocks(s5_c_re[0]), c_blocks(s5_c_im[0])
    ctr, cti = jnp.transpose(cbr, (0, 2, 1)), jnp.transpose(cbi, (0, 2, 1))
    a_re = abar_re.reshape(nb, 1, S5_GPB * S5_STATE)
    a_im = abar_im.reshape(nb, 1, S5_GPB * S5_STATE)

    h = _rms_fwd(x2d, ln_w)
    proj = _mm(h, w_full_t, tb=True, b_rows=n_main, tm=1024, tn=512, after=late_token, name="proj")
    proj_gates = _mm(h, w_gates_t, tb=True, tm=1024, tn=1024, name="proj_gates")
    y1, car_r, car_i, states_r, states_i = _s5_fwd(proj, bbr, bbi, a_re, a_im, ctr, cti, s5_d, d_s5)
    a_log_row = jnp.pad(dn_a_log, ((0, 0), (DN_HEADS, 128 - 2 * DN_HEADS)))
    dt_row = jnp.pad(dn_dt_bias, ((0, 0), (DN_HEADS, 128 - 2 * DN_HEADS)))
    qkv = _dn_prep_fwd(proj, off_qkv, conv_full)
    gb = _dn_gates_fwd(proj, off_ba, a_log_row, dt_row)
    o_dn, states, inverses = _dn_chunk_fwd(qkv, gb)

    late_lands = _split_exchange_wait(late_plan, late_sems, late_shards, late_lands, o_dn,
                                      name="gather_late_wait")
    g_glu, g_sup, g_dup, g_wout = [
        lax.dynamic_update_slice(land, shard[None], (my_index, 0, 0))
        for land, shard in zip(late_lands, late_shards)]
    wglu_full = g_glu.reshape(d_s5, d_s5)
    wsup_full = jnp.transpose(g_sup, (1, 0, 2)).reshape(d_s5, d)
    wdup_full = jnp.transpose(g_dup, (1, 0, 2)).reshape(d_dn, d)
    wout_full = g_wout.reshape(d, d)

    out_s = _s5_glu_fwd(y1, proj, off_zs, wglu_full)
    y_s = _mm(out_s, wsup_full, name="s5_up")
    out_d = _dn_out_fwd(o_dn, proj, off_zd, dn_norm_w)
    y_d = _mm(out_d, wdup_full, name="dn_up")

    mixed = _merge_fwd(proj_gates, off_gs, off_gd, y_s, y_d)
    branch = _mm(mixed, wout_full, name="w_out")
    dx2, dx2_bf, loss_dev, d_final_w = _final(x2d, branch, final_norm_w.reshape(1, d), tgt2d)

    g_wout_full = _mm(mixed, dx2_bf, ta=True, out_dtype=BF16, name="grad_w_out")
    dmixed = _mm(dx2_bf, wout_full, tb=True, name="d_mixed")
    dgs, dgd, dys, dyd = _merge_bwd(proj_gates, off_gs, off_gd, y_s, y_d, dmixed)

    g_dup_full = _mm(out_d, dyd, ta=True, out_dtype=BF16, name="grad_dn_up")
    dout_d = _mm(dyd, wdup_full, tb=True, name="d_out_d")
    do_dn, dzd, d_norm_w = _dn_out_bwd(o_dn, proj, off_zd, dn_norm_w, dout_d)
    dqkv, dgb_heads = _dn_chunk_bwd(qkv, gb, states, inverses, do_dn)
    dba, d_a_log_row, d_dt_row = _dn_gates_bwd(proj, off_ba, a_log_row, dt_row, dgb_heads)
    dqkv_pre, d_conv_full = _dn_prep_bwd(proj, off_qkv, conv_full, dqkv)

    g_sup_full = _mm(out_s, dys, ta=True, out_dtype=BF16, name="grad_s5_up")
    dout_s = _mm(dys, wsup_full, tb=True, name="d_out_s")
    dy1, dzs, g_glu_full = _s5_glu_bwd(y1, proj, off_zs, wglu_full, dout_s)

    def by_dest(t, axis=0):
        if axis == 1:
            return t.reshape(t.shape[0], 4, 2, t.shape[1] // N_DEV).transpose(1, 2, 0, 3)
        return t.reshape(4, 2, t.shape[0] // N_DEV, t.shape[1])

    core = lax.axis_index("c").astype(jnp.int32).reshape(1)
    chip = (2 * lax.axis_index("x") + lax.axis_index("y")).astype(jnp.int32).reshape(1)

    def chip_sums_of(which, parts, tag):
        from_sibling = _sibling_swap(parts, name="swap_grads_" + tag)
        return [_pair_sum(p, got, core, name="pair_sum_" + nm)
                for nm, p, got in zip(which, parts, from_sibling)]

    early = ["s5_w_glu", "s5_w_up", "dn_w_up", "w_out"]
    sums_a = chip_sums_of(early, [by_dest(g_glu_full.astype(BF16)), by_dest(g_sup_full, 1),
                                  by_dest(g_dup_full, 1), by_dest(g_wout_full)], "a")
    plan_a = _chip_slices_plan(len(sums_a))
    sems_a, src_a, land_a, token_a = _split_exchange_start(
        plan_a, sums_a, [t.shape for t in sums_a], name="exchange_start_a")

    (du, d_a_re, d_a_im, d_bbr, d_bbi, d_cbr, d_cbi, d_s5_d) = _s5_bwd(
        proj, dy1, bbr, bbi, a_re, a_im, cbr, cbi, s5_d + token_a[:1, :1], car_r, car_i,
        states_r, states_i)

    def from_bb_blocks(t):
        t = _diag_blocks(t, S5_GROUP, S5_STATE).transpose(0, 1, 3, 2)
        return t.reshape(groups * S5_STATE, S5_GROUP)

    d_f_re, d_f_im, d_b_re, d_b_im = _s5_bbar_bwd(f_re_col, f_im_col, b_re, b_im,
                                                 from_bb_blocks(d_bbr), from_bb_blocks(d_bbi))
    d_lam_re, d_lam_im, d_log_step = _s5_disc_bwd(
        lam_re, lam_im, log_step, d_a_re.reshape(groups, S5_STATE), d_a_im.reshape(groups, S5_STATE),
        d_f_re.reshape(groups, S5_STATE), d_f_im.reshape(groups, S5_STATE))
    d_c_re = _diag_blocks(d_cbr, S5_GROUP, S5_STATE).reshape(groups, S5_GROUP, S5_STATE)
    d_c_im = _diag_blocks(d_cbi, S5_GROUP, S5_STATE).reshape(groups, S5_GROUP, S5_STATE)

    dproj = jnp.concatenate([du, dzs, dqkv_pre, dzd, jnp.pad(dba, ((0, 0), (0, BA_PAD - 128)))], axis=1)
    dproj_gates = jnp.concatenate([dgs, dgd], axis=1)
    g_main_t = _mm(dproj, h, ta=True, out_dtype=BF16, tm=512, tn=d, name="grad_w_in")
    g_gates_t = _mm(dproj_gates, h, ta=True, out_dtype=BF16, tm=1024, tn=d, name="grad_w_in_gates")
    g_win_full_t = jnp.concatenate([g_main_t[:ba_end], g_gates_t], axis=0)
    sums_b = chip_sums_of(["w_in"], [by_dest(g_win_full_t)], "b")
    plan_b = _chip_slices_plan(1)
    sems_b, src_b, land_b, token_b = _split_exchange_start(
        plan_b, sums_b, [t.shape for t in sums_b], name="exchange_start_b")
    dh_main = _mm(dproj, w_full_t, b_rows=n_main, tm=1024, tn=1024, tk=n_main // 4, after=token_b,
                  name="d_h_main")
    dh = _mm(dproj_gates, w_gates_t, tm=1024, tn=1024, tk=2048, addend=dh_main, name="d_h")
    grad_x, d_ln_w = _rms_bwd(x2d, ln_w, dh, dx2)
    big = ["w_in"] + early
    results = {}

    land_a = _split_exchange_wait(plan_a, sems_a, src_a, land_a, grad_x, name="exchange_wait_a")
    for nm, own, landed in zip(early, src_a, land_a):
        results[nm] = _adamw_exchanged(weights[nm][0], mom_m[nm][0], mom_v[nm][0], own, landed, chip,
                                       name="adamw_" + nm)

    small = [nm for nm in names if nm not in big]
    small_grads = dict(
        ln_w=d_ln_w, s5_lam_re=d_lam_re, s5_lam_im=d_lam_im, s5_log_step=d_log_step,
        s5_b_re=d_b_re, s5_b_im=d_b_im, s5_c_re=d_c_re, s5_c_im=d_c_im, s5_d=d_s5_d,
        dn_conv_w=d_conv_full, dn_a_log=d_a_log_row[:, DN_HEADS:2 * DN_HEADS],
        dn_dt_bias=d_dt_row[:, DN_HEADS:2 * DN_HEADS], dn_norm_w=d_norm_w, final_norm_w=d_final_w)
    (all_small,) = _all_gather([_pack_rows([small_grads[nm] for nm in small])], name="gather_small_grads",
                               after=results[early[-1]][0])
    summed = _slot_sum(all_small, name="sum_small_grads")
    full_shapes = [(CONV_K, 3 * d_dn) if nm == "dn_conv_w" else weights[nm].shape for nm in small]
    g_small = dict(zip(small, _unpack_rows(summed, full_shapes)))
    conv_cols = dn_conv_w.shape[2]
    g_small["dn_conv_w"] = lax.dynamic_slice_in_dim(
        g_small["dn_conv_w"], my_index * conv_cols, conv_cols, axis=1).reshape(dn_conv_w.shape)
    packed = [_pack_rows([t[nm] for nm in small]) for t in (weights, mom_m, mom_v, g_small)]
    small_out = _adamw(packed[0], packed[1], packed[2], packed[3][None], name="adamw_small")
    small_shapes = [weights[nm].shape for nm in small]
    for kind, packed_out in enumerate(small_out):
        for nm, val in zip(small, _unpack_rows(packed_out, small_shapes)):
            results.setdefault(nm, [None] * 4)[kind] = val

    (land_b,) = _split_exchange_wait(plan_b, sems_b, src_b, land_b, small_out[0], name="exchange_wait_b")
    res = _adamw_exchanged(jnp.transpose(w_in[0]), jnp.transpose(m_w_in[0]), jnp.transpose(v_w_in[0]),
                           src_b[0], land_b, chip, name="adamw_w_in")
    results["w_in"] = [jnp.transpose(t) for t in res]

    loss = lax.psum(loss_dev[0, 0], ("x", "y", "c"))
    outs = [loss, grad_x[None]]
    for kind in range(4):
        outs += [results[nm][kind].reshape(weights[nm].shape) for nm in names]
    return tuple(outs)
```

```python
import functools
import math

import jax
import jax.numpy as jnp
from jax import lax
from jax.experimental import pallas as pl
from jax.experimental.pallas import tpu as pltpu

F32 = jnp.float32
BF16 = jnp.bfloat16
HIGHEST = lax.Precision.HIGHEST
MESH = pl.DeviceIdType.MESH
N_DEV = 8

EPS = 1e-6
S5_GROUP = 16
S5_STATE = 64
S5_GPB = 8
S5_T = 256
DN_HEADS = 8
DN_HEAD_DIM = 128
CHUNK = 64
DN_HEADS_PER_STEP = 8
CONV_K = 4
BA_PAD = 512

ADAM_LR = 0.001
ADAM_B1 = 0.9
ADAM_B2 = 0.999
ADAM_EPS = 1e-08
ADAM_WD = 0.01
ADAM_STEP = 10

VMEM_LIMIT_BYTES = 48 * 1024 * 1024
ROW_TILE = 256


def _cparams(*sem):
    return pltpu.CompilerParams(dimension_semantics=sem if sem else None,
                                vmem_limit_bytes=VMEM_LIMIT_BYTES)


def _sigmoid(x):
    return 1.0 / (1.0 + jnp.exp(-x))


def _silu(x):
    return x * _sigmoid(x)


def _gelu(x):
    return 0.5 * x * (1.0 + jnp.tanh(0.7978845608028654 * (x + 0.044715 * x * x * x)))


def _softplus(x):
    return jnp.maximum(x, 0.0) + jnp.log(1.0 + jnp.exp(-jnp.abs(x)))


def _rmsnorm(x, w):
    return x * lax.rsqrt(jnp.mean(x * x, axis=-1, keepdims=True) + EPS) * w


def _dot(a, b, dims=((1,), (0,)), precision=None):
    return lax.dot_general(a, b, (dims, ((), ())), precision=precision,
                           preferred_element_type=F32)


def _bdot(a, b, dims=((1,), (0,))):
    return _dot(a.astype(BF16), b.astype(BF16), dims)


def _split_bf16(a):
    hi = a.astype(BF16)
    return hi, (a - hi.astype(F32)).astype(BF16)


def _dot3_dims(a, b, dims):
    ah, al = _split_bf16(a)
    bh, bl = _split_bf16(b)
    return _dot(ah, bh, dims) + (_dot(ah, bl, dims) + _dot(al, bh, dims))


@jax.custom_vjp
def _dot3(a, b):
    return _dot3_dims(a, b, ((1,), (0,)))


def _dot3_fwd(a, b):
    return _dot3(a, b), (a, b)


def _dot3_bwd(res, g):
    a, b = res
    return _dot3_dims(g, b, ((1,), (1,))), _dot3_dims(a, g, ((0,), (0,)))


_dot3.defvjp(_dot3_fwd, _dot3_bwd)


def _mm(a, b, *, ta=False, tb=False, out_dtype=F32, tm=512, tn=512, tk=None, after=None, b_rows=None,
        addend=None, name):
    k_dim, m_dim = (a.shape if ta else a.shape[::-1])
    b_rows = b.shape[0] if b_rows is None else b_rows
    n_dim = b_rows if tb else b.shape[1]
    assert (b.shape[1] if tb else b_rows) == k_dim and b_rows <= b.shape[0]
    tm, tn = min(tm, m_dim), min(tn, n_dim)
    tk = k_dim if tk is None else tk
    assert m_dim % tm == 0 and n_dim % tn == 0 and k_dim % tk == 0
    nk = k_dim // tk
    a_spec = (pl.BlockSpec((tk, tm), lambda i, j, k: (k, i)) if ta
              else pl.BlockSpec((tm, tk), lambda i, j, k: (i, k)))
    b_spec = (pl.BlockSpec((tn, tk), lambda i, j, k: (j, k)) if tb
              else pl.BlockSpec((tk, tn), lambda i, j, k: (k, j)))
    dims = ((0 if ta else 1,), (1 if tb else 0,))

    extras = ([after] if after is not None else []) + ([addend] if addend is not None else [])
    extra_specs = ([pl.BlockSpec((8, 128), lambda i, j, k: (0, 0))] if after is not None else []) + (
        [pl.BlockSpec((tm, tn), lambda i, j, k: (i, j))] if addend is not None else [])

    def body(a_ref, b_ref, *rest):
        o_ref, *scratch = rest[len(extras):]
        p = _bdot(a_ref[...], b_ref[...], dims)
        finish = (lambda v: v + rest[len(extras) - 1][...]) if addend is not None else (lambda v: v)
        if nk == 1:
            o_ref[...] = finish(p).astype(o_ref.dtype)
        else:
            acc = scratch[0]
            k = pl.program_id(2)

            @pl.when(k == 0)
            def _():
                acc[...] = p

            @pl.when(k > 0)
            def _():
                acc[...] += p

            @pl.when(k == nk - 1)
            def _():
                o_ref[...] = finish(acc[...]).astype(o_ref.dtype)

    return pl.pallas_call(
        body, name=name,
        out_shape=jax.ShapeDtypeStruct((m_dim, n_dim), out_dtype),
        grid=(m_dim // tm, n_dim // tn, nk),
        in_specs=[a_spec, b_spec] + extra_specs,
        out_specs=pl.BlockSpec((tm, tn), lambda i, j, k: (i, j)),
        scratch_shapes=[pltpu.VMEM((tm, tn), F32)] if nk > 1 else [],
        compiler_params=_cparams("parallel", "parallel", "arbitrary"),
    )(a, b, *extras)


def _rms_fwd(x, w):
    l, d = x.shape

    def body(x_ref, w_ref, h_ref):
        h_ref[...] = _rmsnorm(x_ref[...], w_ref[...]).astype(BF16)

    return pl.pallas_call(
        body, name="rms_fwd",
        out_shape=jax.ShapeDtypeStruct((l, d), BF16),
        grid=(l // ROW_TILE,),
        in_specs=[pl.BlockSpec((ROW_TILE, d), lambda i: (i, 0)),
                  pl.BlockSpec((1, d), lambda i: (0, 0))],
        out_specs=pl.BlockSpec((ROW_TILE, d), lambda i: (i, 0)),
        compiler_params=_cparams("parallel"),
    )(x, w)


def _rms_bwd(x, w, dh, dres):
    l, d = x.shape

    def body(x_ref, w_ref, dh_ref, dres_ref, dx_ref, dw_ref):
        _, vjp = jax.vjp(_rmsnorm, x_ref[...], w_ref[...])
        dx, dw = vjp(dh_ref[...])
        dx_ref[...] = dx + dres_ref[...]

        @pl.when(pl.program_id(0) == 0)
        def _():
            dw_ref[...] = jnp.zeros_like(dw_ref)

        dw_ref[...] += dw

    row = pl.BlockSpec((ROW_TILE, d), lambda i: (i, 0))
    vec = pl.BlockSpec((1, d), lambda i: (0, 0))
    return pl.pallas_call(
        body, name="rms_bwd",
        out_shape=(jax.ShapeDtypeStruct((l, d), F32), jax.ShapeDtypeStruct((1, d), F32)),
        grid=(l // ROW_TILE,),
        in_specs=[row, vec, row, row],
        out_specs=(row, vec),
        compiler_params=_cparams("arbitrary"),
    )(x, w, dh, dres)


def _final(x, r, fw, target):
    l, d = x.shape

    def per_row_loss(x2, w, tgt):
        err = _rmsnorm(x2, w) - tgt
        return 0.5 * jnp.mean(err * err, axis=-1, keepdims=True)

    def body(x_ref, r_ref, w_ref, t_ref, dx_ref, dxb_ref, loss_ref, dw_ref):
        x2 = x_ref[...] + r_ref[...]
        rows, vjp = jax.vjp(functools.partial(per_row_loss, tgt=t_ref[...]), x2, w_ref[...])
        dx2, dw = vjp(jnp.ones_like(rows))
        dx_ref[...] = dx2
        dxb_ref[...] = dx2.astype(BF16)

        @pl.when(pl.program_id(0) == 0)
        def _():
            dw_ref[...] = jnp.zeros_like(dw_ref)
            loss_ref[...] = jnp.zeros_like(loss_ref)

        dw_ref[...] += dw
        loss_ref[...] += jnp.sum(rows, axis=0, keepdims=True)

    row = pl.BlockSpec((ROW_TILE, d), lambda i: (i, 0))
    vec = pl.BlockSpec((1, d), lambda i: (0, 0))
    return pl.pallas_call(
        body, name="final_norm_loss",
        out_shape=(jax.ShapeDtypeStruct((l, d), F32), jax.ShapeDtypeStruct((l, d), BF16),
                   jax.ShapeDtypeStruct((1, 1), F32), jax.ShapeDtypeStruct((1, d), F32)),
        grid=(l // ROW_TILE,),
        in_specs=[row, row, vec, row],
        out_specs=(row, row, pl.BlockSpec((1, 1), lambda i: (0, 0)), vec),
        compiler_params=_cparams("arbitrary"),
    )(x, r, fw, target)


def _merge_fn(gs, gd, ys, yd):
    return _sigmoid(gs) * ys + _sigmoid(gd) * yd


def _merge_fwd(proj, off_gs, off_gd, ys, yd):
    l, d = ys.shape
    cw = min(1024, d)
    blk = lambda off: pl.BlockSpec((ROW_TILE, cw), lambda i, j: (i, off // cw + j))

    def body(gs_ref, gd_ref, ys_ref, yd_ref, o_ref):
        o_ref[...] = _merge_fn(gs_ref[...], gd_ref[...], ys_ref[...], yd_ref[...]).astype(BF16)

    return pl.pallas_call(
        body, name="merge_fwd",
        out_shape=jax.ShapeDtypeStruct((l, d), BF16),
        grid=(l // ROW_TILE, d // cw),
        in_specs=[blk(off_gs), blk(off_gd), blk(0), blk(0)],
        out_specs=blk(0),
        compiler_params=_cparams("parallel", "parallel"),
    )(proj, proj, ys, yd)


def _merge_bwd(proj, off_gs, off_gd, ys, yd, dmixed):
    l, d = ys.shape
    cw = min(1024, d)
    blk = lambda off: pl.BlockSpec((ROW_TILE, cw), lambda i, j: (i, off // cw + j))

    def body(gs_ref, gd_ref, ys_ref, yd_ref, dm_ref, dgs_ref, dgd_ref, dys_ref, dyd_ref):
        _, vjp = jax.vjp(_merge_fn, gs_ref[...], gd_ref[...], ys_ref[...], yd_ref[...])
        dgs, dgd, dys, dyd = vjp(dm_ref[...])
        dgs_ref[...] = dgs.astype(BF16)
        dgd_ref[...] = dgd.astype(BF16)
        dys_ref[...] = dys.astype(BF16)
        dyd_ref[...] = dyd.astype(BF16)

    out = jax.ShapeDtypeStruct((l, d), BF16)
    return pl.pallas_call(
        body, name="merge_bwd",
        out_shape=(out, out, out, out),
        grid=(l // ROW_TILE, d // cw),
        in_specs=[blk(off_gs), blk(off_gd), blk(0), blk(0), blk(0)],
        out_specs=(blk(0), blk(0), blk(0), blk(0)),
        compiler_params=_cparams("parallel", "parallel"),
    )(proj, proj, ys, yd, dmixed)


def _s5_disc_fn(lam_re, lam_im, log_step):
    step = jnp.exp(log_step)
    mag = jnp.exp(lam_re * step)
    abar_re = mag * jnp.cos(lam_im * step)
    abar_im = mag * jnp.sin(lam_im * step)
    den = lam_re * lam_re + lam_im * lam_im
    xr = abar_re - 1.0
    f_re = (xr * lam_re + abar_im * lam_im) / den
    f_im = (abar_im * lam_re - xr * lam_im) / den
    return abar_re, abar_im, f_re, f_im


def _s5_disc_fwd(lam_re, lam_im, log_step):
    g, p = lam_re.shape

    def body(lr_ref, li_ref, ls_ref, ar_ref, ai_ref, fr_ref, fi_ref):
        ar, ai, fr, fi = _s5_disc_fn(lr_ref[...], li_ref[...], ls_ref[...])
        ar_ref[...] = ar
        ai_ref[...] = ai
        fr_ref[...] = fr
        fi_ref[...] = fi

    o = jax.ShapeDtypeStruct((g, p), F32)
    return pl.pallas_call(body, name="s5_disc_fwd", out_shape=(o, o, o, o),
                          compiler_params=_cparams())(lam_re, lam_im, log_step)


def _s5_disc_bwd(lam_re, lam_im, log_step, dar, dai, dfr, dfi):
    g, p = lam_re.shape

    def body(lr_ref, li_ref, ls_ref, dar_ref, dai_ref, dfr_ref, dfi_ref, dlr_ref, dli_ref, dls_ref):
        _, vjp = jax.vjp(_s5_disc_fn, lr_ref[...], li_ref[...], ls_ref[...])
        dlr, dli, dls = vjp((dar_ref[...], dai_ref[...], dfr_ref[...], dfi_ref[...]))
        dlr_ref[...] = dlr
        dli_ref[...] = dli
        dls_ref[...] = dls

    o = jax.ShapeDtypeStruct((g, p), F32)
    return pl.pallas_call(body, name="s5_disc_bwd",
                          out_shape=(o, o, jax.ShapeDtypeStruct((g, 1), F32)),
                          compiler_params=_cparams())(lam_re, lam_im, log_step, dar, dai, dfr, dfi)


def _s5_bbar_fwd(f_re, f_im, b_re, b_im):
    n, c = b_re.shape

    def body(fr_ref, fi_ref, br_ref, bi_ref, or_ref, oi_ref):
        fr, fi, br, bi = fr_ref[...], fi_ref[...], br_ref[...], bi_ref[...]
        or_ref[...] = fr * br - fi * bi
        oi_ref[...] = fr * bi + fi * br

    o = jax.ShapeDtypeStruct((n, c), F32)
    return pl.pallas_call(body, name="s5_bbar_fwd", out_shape=(o, o),
                          compiler_params=_cparams())(f_re, f_im, b_re, b_im)


def _s5_bbar_bwd(f_re, f_im, b_re, b_im, dbr, dbi):
    n, c = b_re.shape

    def body(fr_ref, fi_ref, br_ref, bi_ref, dor_ref, doi_ref, dfr_ref, dfi_ref, dbr_ref, dbi_ref):
        fr, fi, br, bi = fr_ref[...], fi_ref[...], br_ref[...], bi_ref[...]
        dor, doi = dor_ref[...], doi_ref[...]
        dfr_ref[...] = jnp.sum(dor * br + doi * bi, axis=-1, keepdims=True)
        dfi_ref[...] = jnp.sum(doi * br - dor * bi, axis=-1, keepdims=True)
        dbr_ref[...] = fr * dor + fi * doi
        dbi_ref[...] = fr * doi - fi * dor

    col = jax.ShapeDtypeStruct((n, 1), F32)
    o = jax.ShapeDtypeStruct((n, c), F32)
    return pl.pallas_call(body, name="s5_bbar_bwd", out_shape=(col, col, o, o),
                          compiler_params=_cparams())(f_re, f_im, b_re, b_im, dbr, dbi)


SUBLANES = 8


def _scan_groups(xr, xi, ar, ai, reverse):
    t, n = xr.shape
    xr, xi = xr.reshape(t // SUBLANES, SUBLANES, n), xi.reshape(t // SUBLANES, SUBLANES, n)
    sub = lax.broadcasted_iota(jnp.int32, (1, SUBLANES, 1), 1)
    pr, pi = ar.reshape(1, 1, n), ai.reshape(1, 1, n)
    for sh in (1, 2, 4):
        keep = (sub < SUBLANES - sh) if reverse else (sub >= sh)
        cr, ci = jnp.where(keep, pr, 0.0), jnp.where(keep, pi, 0.0)
        shift = SUBLANES - sh if reverse else sh
        sr, si = pltpu.roll(xr, shift, 1), pltpu.roll(xi, shift, 1)
        xr, xi = xr + cr * sr - ci * si, xi + cr * si + ci * sr
        pr, pi = pr * pr - pi * pi, 2.0 * pr * pi
    return xr.reshape(t, n), xi.reshape(t, n)


def _scan_rows(xr, xi, ar, ai, cr, ci, sr_ref, si_ref, reverse):
    t, n = xr.shape
    xr, xi = _scan_groups(xr, xi, ar, ai, reverse)
    sr_ref[...] = xr
    si_ref[...] = xi
    sub = lax.broadcasted_iota(jnp.int32, (SUBLANES, n), 0)
    seed = sub == (SUBLANES - 1 if reverse else 0)
    pwr, pwi = _scan_groups(jnp.where(seed, ar, 0.0), jnp.where(seed, ai, 0.0), ar, ai, reverse)
    groups = range(t // SUBLANES)
    edge = 0 if reverse else SUBLANES - 1
    for g in (reversed(groups) if reverse else groups):
        rows = slice(g * SUBLANES, (g + 1) * SUBLANES)
        vr = sr_ref[rows, :] + (pwr * cr - pwi * ci)
        vi = si_ref[rows, :] + (pwr * ci + pwi * cr)
        sr_ref[rows, :] = vr
        si_ref[rows, :] = vi
        cr, ci = vr[edge:edge + 1, :], vi[edge:edge + 1, :]
    return cr, ci


def _s5_states(u_bf, bbr, bbi, ar, ai, cr, ci, sr_ref, si_ref):
    return _scan_rows(_dot(u_bf, bbr), _dot(u_bf, bbi), ar, ai, cr, ci, sr_ref, si_ref, reverse=False)


def _s5_fwd(proj, bbr, bbi, a_re, a_im, ctr, cti, d_skip, d_s5):
    l = proj.shape[0]
    nb, uc, ns = bbr.shape
    t = min(S5_T, l)
    nt = l // t

    def body(u_ref, bbr_ref, bbi_ref, ar_ref, ai_ref, ctr_ref, cti_ref, d_ref,
             y_ref, car_r_ref, car_i_ref, sr_ref, si_ref, cr, ci):
        @pl.when(pl.program_id(1) == 0)
        def _():
            cr[...] = jnp.zeros_like(cr)
            ci[...] = jnp.zeros_like(ci)

        car_r_ref[...] = cr[...]
        car_i_ref[...] = ci[...]
        u = u_ref[...]
        cr[...], ci[...] = _s5_states(u.astype(BF16), bbr_ref[...], bbi_ref[...], ar_ref[...],
                                      ai_ref[...], cr[...], ci[...], sr_ref, si_ref)
        y_ref[...] = (_bdot(sr_ref[...], ctr_ref[...]) - _bdot(si_ref[...], cti_ref[...])
                      + d_ref[...] * u)

    per_block = lambda shape: pl.BlockSpec((None,) + shape, lambda b, n: (b, 0, 0))
    return pl.pallas_call(
        body, name="s5_fwd",
        out_shape=(jax.ShapeDtypeStruct((l, d_s5), F32),
                   jax.ShapeDtypeStruct((nt, 1, nb * ns), F32),
                   jax.ShapeDtypeStruct((nt, 1, nb * ns), F32),
                   jax.ShapeDtypeStruct((l, nb * ns), F32),
                   jax.ShapeDtypeStruct((l, nb * ns), F32)),
        grid=(nb, nt),
        in_specs=[pl.BlockSpec((t, uc), lambda b, n: (n, b)),
                  per_block((uc, ns)), per_block((uc, ns)),
                  per_block((1, ns)), per_block((1, ns)),
                  per_block((ns, uc)), per_block((ns, uc)),
                  pl.BlockSpec((1, uc), lambda b, n: (0, b))],
        out_specs=(pl.BlockSpec((t, uc), lambda b, n: (n, b)),
                   pl.BlockSpec((None, 1, ns), lambda b, n: (n, 0, b)),
                   pl.BlockSpec((None, 1, ns), lambda b, n: (n, 0, b)),
                   pl.BlockSpec((t, ns), lambda b, n: (n, b)),
                   pl.BlockSpec((t, ns), lambda b, n: (n, b))),
        scratch_shapes=[pltpu.VMEM((1, ns), F32), pltpu.VMEM((1, ns), F32)],
        compiler_params=_cparams("parallel", "arbitrary"),
    )(proj, bbr, bbi, a_re, a_im, ctr, cti, d_skip)


def _s5_bwd(proj, dy, bbr, bbi, a_re, a_im, cbr, cbi, d_skip, car_r, car_i, states_r, states_i):
    l, d_s5 = dy.shape
    nb, uc, ns = bbr.shape
    t = min(S5_T, l)
    nt = l // t

    def body(u_ref, dy_ref, bbr_ref, bbi_ref, ar_ref, ai_ref, cbr_ref, cbi_ref, d_ref,
             car_r_ref, car_i_ref, sr_ref, si_ref,
             du_ref, dar_ref, dai_ref, dbbr_ref, dbbi_ref, dcbr_ref, dcbi_ref, dd_ref, gcr, gci,
             gr_ref, gi_ref):
        @pl.when(pl.program_id(1) == 0)
        def _():
            gcr[...] = jnp.zeros_like(gcr)
            gci[...] = jnp.zeros_like(gci)
            for ref in (dar_ref, dai_ref, dbbr_ref, dbbi_ref, dcbr_ref, dcbi_ref, dd_ref):
                ref[...] = jnp.zeros_like(ref)

        row = lax.broadcasted_iota(jnp.int32, (t, 1), 0)
        u, dy = u_ref[...], dy_ref[...]
        u_bf, dy_bf = u.astype(BF16), dy.astype(BF16)
        ar, ai = ar_ref[...], ai_ref[...]
        cr, ci = car_r_ref[...], car_i_ref[...]
        sr, si = sr_ref[...], si_ref[...]
        first = row == 0
        pr = jnp.where(first, cr, pltpu.roll(sr, 1, 0))
        pi = jnp.where(first, ci, pltpu.roll(si, 1, 0))
        gcr[...], gci[...] = _scan_rows(_dot(dy_bf, cbr_ref[...]), -_dot(dy_bf, cbi_ref[...]), ar, -ai,
                                        gcr[...], gci[...], gr_ref, gi_ref, reverse=True)
        gr, gi = gr_ref[...], gi_ref[...]
        dar_ref[...] += jnp.sum(gr * pr + gi * pi, axis=0, keepdims=True)
        dai_ref[...] += jnp.sum(gi * pr - gr * pi, axis=0, keepdims=True)
        gr_bf, gi_bf = gr.astype(BF16), gi.astype(BF16)
        tn = ((0,), (0,))
        dbbr_ref[...] += _dot(u_bf, gr_bf, tn)
        dbbi_ref[...] += _dot(u_bf, gi_bf, tn)
        dcbr_ref[...] += _dot(dy_bf, sr.astype(BF16), tn)
        dcbi_ref[...] -= _dot(dy_bf, si.astype(BF16), tn)
        nt_dims = ((1,), (1,))
        du = _dot(gr_bf, bbr_ref[...], nt_dims) + _dot(gi_bf, bbi_ref[...], nt_dims) + dy * d_ref[...]
        du_ref[...] = du.astype(BF16)
        dd_ref[...] += jnp.sum(dy * u, axis=0, keepdims=True)

    rev = lambda n: nt - 1 - n
    per_block = lambda shape: pl.BlockSpec((None,) + shape, lambda b, n: (b, 0, 0))
    acc = jax.ShapeDtypeStruct((nb, uc, ns), F32)
    vec = jax.ShapeDtypeStruct((nb, 1, ns), F32)
    return pl.pallas_call(
        body, name="s5_bwd",
        out_shape=(jax.ShapeDtypeStruct((l, d_s5), BF16), vec, vec, acc, acc, acc, acc,
                   jax.ShapeDtypeStruct((1, d_s5), F32)),
        grid=(nb, nt),
        in_specs=[pl.BlockSpec((t, uc), lambda b, n: (rev(n), b)),
                  pl.BlockSpec((t, uc), lambda b, n: (rev(n), b)),
                  per_block((uc, ns)), per_block((uc, ns)),
                  per_block((1, ns)), per_block((1, ns)),
                  per_block((uc, ns)), per_block((uc, ns)),
                  pl.BlockSpec((1, uc), lambda b, n: (0, b)),
                  pl.BlockSpec((None, 1, ns), lambda b, n: (rev(n), 0, b)),
                  pl.BlockSpec((None, 1, ns), lambda b, n: (rev(n), 0, b)),
                  pl.BlockSpec((t, ns), lambda b, n: (rev(n), b)),
                  pl.BlockSpec((t, ns), lambda b, n: (rev(n), b))],
        out_specs=(pl.BlockSpec((t, uc), lambda b, n: (rev(n), b)),
                   per_block((1, ns)), per_block((1, ns)),
                   per_block((uc, ns)), per_block((uc, ns)),
                   per_block((uc, ns)), per_block((uc, ns)),
                   pl.BlockSpec((1, uc), lambda b, n: (0, b))),
        scratch_shapes=[pltpu.VMEM((1, ns), F32), pltpu.VMEM((1, ns), F32)]
        + [pltpu.VMEM((t, ns), F32)] * 2,
        compiler_params=_cparams("parallel", "arbitrary"),
    )(proj, dy, bbr, bbi, a_re, a_im, cbr, cbi, d_skip, car_r, car_i, states_r, states_i)


def _s5_glu_fwd(y1, proj, off_z, wglu):
    l, d = y1.shape

    def body(y_ref, z_ref, w_ref, o_ref):
        y2 = _gelu(y_ref[...])
        y3 = y2 * _sigmoid(_bdot(y2, w_ref[...]))
        o_ref[...] = (y3 * _silu(z_ref[...])).astype(BF16)

    return pl.pallas_call(
        body, name="s5_glu_fwd",
        out_shape=jax.ShapeDtypeStruct((l, d), BF16),
        grid=(l // ROW_TILE,),
        in_specs=[pl.BlockSpec((ROW_TILE, d), lambda i: (i, 0)),
                  pl.BlockSpec((ROW_TILE, d), lambda i: (i, off_z // d)),
                  pl.BlockSpec((d, d), lambda i: (0, 0))],
        out_specs=pl.BlockSpec((ROW_TILE, d), lambda i: (i, 0)),
        compiler_params=_cparams("parallel"),
    )(y1, proj, wglu)


def _s5_glu_bwd(y1, proj, off_z, wglu, dout):
    l, d = y1.shape

    def body(y_ref, z_ref, w_ref, do_ref, dy_ref, dz_ref, dw_ref):
        y2, gelu_vjp = jax.vjp(_gelu, y_ref[...])
        z = z_ref[...]
        sz, silu_vjp = jax.vjp(_silu, z)
        y2_bf = y2.astype(BF16)
        sg = _sigmoid(_dot(y2_bf, w_ref[...]))
        dout = do_ref[...]
        dy3 = dout * sz
        dz_ref[...] = silu_vjp(dout * (y2 * sg))[0].astype(BF16)
        dgl = (dy3 * y2 * sg * (1.0 - sg)).astype(BF16)
        dy2 = dy3 * sg + _dot(dgl, w_ref[...], ((1,), (1,)))
        dy_ref[...] = gelu_vjp(dy2)[0]

        @pl.when(pl.program_id(0) == 0)
        def _():
            dw_ref[...] = jnp.zeros_like(dw_ref)

        dw_ref[...] += _dot(y2_bf, dgl, ((0,), (0,)))

    row = pl.BlockSpec((ROW_TILE, d), lambda i: (i, 0))
    full = pl.BlockSpec((d, d), lambda i: (0, 0))
    return pl.pallas_call(
        body, name="s5_glu_bwd",
        out_shape=(jax.ShapeDtypeStruct((l, d), F32), jax.ShapeDtypeStruct((l, d), BF16),
                   jax.ShapeDtypeStruct((d, d), F32)),
        grid=(l // ROW_TILE,),
        in_specs=[row, pl.BlockSpec((ROW_TILE, d), lambda i: (i, off_z // d)), full, row],
        out_specs=(row, row, full),
        compiler_params=_cparams("arbitrary"),
    )(y1, proj, wglu, dout)


def _shift_rows(x, k, back=False):
    if k == 0:
        return x
    t = x.shape[0]
    row = lax.broadcasted_iota(jnp.int32, (t, 1), 0)
    if back:
        return jnp.where(row < t - k, pltpu.roll(x, t - k, 0), 0.0)
    return jnp.where(row >= k, pltpu.roll(x, k, 0), 0.0)


def _dn_conv(x, w_ref):
    return sum(w_ref[CONV_K - 1 - k:CONV_K - k, :] * _shift_rows(x, k) for k in range(CONV_K))


def _dn_post_conv(c, j):
    y = _silu(c)
    n = y * lax.rsqrt(jnp.sum(y * y, axis=-1, keepdims=True) + EPS)
    n = n * jnp.where(j < DN_HEADS, DN_HEAD_DIM ** -0.5, 1.0)
    return jnp.where(j < 2 * DN_HEADS, n, y)


def _dn_prep_fwd(proj, off_qkv, conv_w):
    l = proj.shape[0]
    hd = DN_HEAD_DIM
    nblk = 3 * DN_HEADS

    def body(x_ref, w_ref, o_ref):
        o_ref[...] = _dn_post_conv(_dn_conv(x_ref[...], w_ref), pl.program_id(0))

    return pl.pallas_call(
        body, name="dn_prep_fwd",
        out_shape=jax.ShapeDtypeStruct((l, nblk * hd), F32),
        grid=(nblk,),
        in_specs=[pl.BlockSpec((l, hd), lambda j: (0, off_qkv // hd + j)),
                  pl.BlockSpec((CONV_K, hd), lambda j: (0, j))],
        out_specs=pl.BlockSpec((l, hd), lambda j: (0, j)),
        compiler_params=_cparams("parallel"),
    )(proj, conv_w)


def _dn_prep_bwd(proj, off_qkv, conv_w, dqkv):
    l = proj.shape[0]
    hd = DN_HEAD_DIM
    nblk = 3 * DN_HEADS

    def body(x_ref, w_ref, do_ref, dx_ref, dw_ref):
        x = x_ref[...]
        j = pl.program_id(0)
        _, vjp = jax.vjp(functools.partial(_dn_post_conv, j=j), _dn_conv(x, w_ref))
        dc = vjp(do_ref[...])[0]
        dx = sum(w_ref[CONV_K - 1 - k:CONV_K - k, :] * _shift_rows(dc, k, back=True)
                 for k in range(CONV_K))
        dx_ref[...] = dx.astype(BF16)
        for k in range(CONV_K):
            dw_ref[CONV_K - 1 - k:CONV_K - k, :] = jnp.sum(dc * _shift_rows(x, k), axis=0,
                                                           keepdims=True)

    return pl.pallas_call(
        body, name="dn_prep_bwd",
        out_shape=(jax.ShapeDtypeStruct((l, nblk * hd), BF16),
                   jax.ShapeDtypeStruct((CONV_K, nblk * hd), F32)),
        grid=(nblk,),
        in_specs=[pl.BlockSpec((l, hd), lambda j: (0, off_qkv // hd + j)),
                  pl.BlockSpec((CONV_K, hd), lambda j: (0, j)),
                  pl.BlockSpec((None, l, hd), lambda j: (j // DN_HEADS, 0, j % DN_HEADS))],
        out_specs=(pl.BlockSpec((l, hd), lambda j: (0, j)),
                   pl.BlockSpec((CONV_K, hd), lambda j: (0, j))),
        compiler_params=_cparams("parallel"),
    )(proj, conv_w, dqkv)


def _dn_gate_fn(ba, a_log_row, dt_row):
    lane = lax.broadcasted_iota(jnp.int32, ba.shape, 1)
    beta = _sigmoid(ba)
    g = -jnp.exp(a_log_row) * _softplus(ba + dt_row)
    return jnp.where(lane < DN_HEADS, beta, jnp.where(lane < 2 * DN_HEADS, g, 0.0))


def _dn_gates_fwd(proj, off_ba, a_log_row, dt_row):
    l = proj.shape[0]
    row = pl.BlockSpec((ROW_TILE, 128), lambda i: (i, off_ba // 128))
    vec = pl.BlockSpec((1, 128), lambda i: (0, 0))

    def body(ba_ref, al_ref, dt_ref, o_ref):
        o_ref[...] = _dn_gate_fn(ba_ref[...], al_ref[...], dt_ref[...])

    return pl.pallas_call(
        body, name="dn_gates_fwd",
        out_shape=jax.ShapeDtypeStruct((l, 128), F32),
        grid=(l // ROW_TILE,),
        in_specs=[row, vec, vec],
        out_specs=pl.BlockSpec((ROW_TILE, 128), lambda i: (i, 0)),
        compiler_params=_cparams("parallel"),
    )(proj, a_log_row, dt_row)


def _dn_gates_bwd(proj, off_ba, a_log_row, dt_row, dgb_heads):
    l = proj.shape[0]
    nh = dgb_heads.shape[0]
    row = pl.BlockSpec((ROW_TILE, 128), lambda i: (i, off_ba // 128))
    vec = pl.BlockSpec((1, 128), lambda i: (0, 0))

    def body(ba_ref, al_ref, dt_ref, dg_ref, dba_ref, dal_ref, ddt_ref):
        _, vjp = jax.vjp(_dn_gate_fn, ba_ref[...], al_ref[...], dt_ref[...])
        dgb = dg_ref[0]
        for h in range(1, nh):
            dgb = dgb + dg_ref[h]
        dba, dal, ddt = vjp(dgb)
        dba_ref[...] = dba.astype(BF16)

        @pl.when(pl.program_id(0) == 0)
        def _():
            dal_ref[...] = jnp.zeros_like(dal_ref)
            ddt_ref[...] = jnp.zeros_like(ddt_ref)

        dal_ref[...] += dal
        ddt_ref[...] += ddt

    return pl.pallas_call(
        body, name="dn_gates_bwd",
        out_shape=(jax.ShapeDtypeStruct((l, 128), BF16), jax.ShapeDtypeStruct((1, 128), F32),
                   jax.ShapeDtypeStruct((1, 128), F32)),
        grid=(l // ROW_TILE,),
        in_specs=[row, vec, vec, pl.BlockSpec((nh, ROW_TILE, 128), lambda i: (0, i, 0))],
        out_specs=(pl.BlockSpec((ROW_TILE, 128), lambda i: (i, 0)), vec, vec),
        compiler_params=_cparams("arbitrary"),
    )(proj, a_log_row, dt_row, dgb_heads)


@jax.custom_vjp
def _unit_lower_inverses(a_mats):
    c = a_mats[0].shape[0]
    eye = (lax.broadcasted_iota(jnp.int32, (c, c), 0) == lax.broadcasted_iota(jnp.int32, (c, c), 1)).astype(F32)
    t_inv = [eye - a for a in a_mats]
    power = a_mats
    for _ in range(int(math.log2(c)) - 1):
        power = [_bdot(p, p) for p in power]
        t_inv = [t + _bdot(t, p) for t, p in zip(t_inv, power)]
    return t_inv


def _unit_lower_inverses_fwd(a_mats):
    t_inv = _unit_lower_inverses(a_mats)
    return t_inv, t_inv


def _inverse_cotangents(t_inv, grads):
    right = [_dot3_dims(g, t, ((1,), (1,))) for g, t in zip(grads, t_inv)]
    return [-_dot3_dims(t, r, ((0,), (0,))) for t, r in zip(t_inv, right)]


_unit_lower_inverses.defvjp(_unit_lower_inverses_fwd,
                            lambda t_inv, grads: (_inverse_cotangents(t_inv, grads),))


@jax.custom_vjp
def _kept_inverses(a_mats, t_inv):
    return t_inv


_kept_inverses.defvjp(
    lambda a_mats, t_inv: (t_inv, t_inv),
    lambda t_inv, grads: (_inverse_cotangents(t_inv, grads), [jnp.zeros_like(t) for t in t_inv]))


def _dn_chunk_fn(states, qs, ks, vs, gb, heads, kept_inverses=None, return_inverses=False):
    c = qs[0].shape[0]
    each = lambda f, *lists: [f(*args) for args in zip(*lists)]
    lane = lax.broadcasted_iota(jnp.int32, gb.shape, 1)
    ri = lax.broadcasted_iota(jnp.int32, (c, c), 0)
    ci = lax.broadcasted_iota(jnp.int32, (c, c), 1)
    causal, strict = ri >= ci, ri > ci
    eye = (ri == ci).astype(F32)
    rowi = lax.broadcasted_iota(jnp.int32, (c, 1), 0)
    nt_dims = ((1,), (1,))
    hdot = functools.partial(_dot, precision=HIGHEST)

    pick = lambda m, at: jnp.sum(jnp.where(lane == at, m, 0.0), axis=1, keepdims=True)
    gb_cum = hdot(causal.astype(F32), gb)
    beta = [pick(gb, h) for h in heads]
    gc = [pick(gb_cum, h + DN_HEADS) for h in heads]
    gc_row = each(lambda g: jnp.sum(eye * g, axis=0, keepdims=True), gc)
    decay = each(lambda g, gr: jnp.where(causal, jnp.exp(jnp.where(causal, g - gr, 0.0)), 0.0),
                 gc, gc_row)
    kk = each(lambda k: _bdot(k, k, nt_dims), ks)
    a_mat = each(lambda b, m, dc: jnp.where(strict, b * m * dc, 0.0), beta, kk, decay)

    t_inv = (_unit_lower_inverses(a_mat) if kept_inverses is None
             else _kept_inverses(a_mat, kept_inverses))
    egc = each(jnp.exp, gc)
    u_c = each(lambda t, v, b: _dot3(t, v * b), t_inv, vs, beta)
    w_c = each(lambda t, k, b, e: _dot3(t, k * (b * e)), t_inv, ks, beta, egc)
    qk = each(lambda q, k, dc: _bdot(q, k, nt_dims) * dc, qs, ks, decay)
    g_end = each(lambda g: jnp.sum(jnp.where(rowi == c - 1, g, 0.0), axis=0, keepdims=True), gc)
    v_new = each(lambda u, w, s: u - _bdot(w, s), u_c, w_c, states)
    o = each(lambda q, e, s, m, vn: _bdot(q * e, s) + _bdot(m, vn), qs, egc, states, qk, v_new)
    new_states = each(
        lambda s, ge, k, g, vn: s * jnp.exp(ge) + _bdot(k * jnp.exp(ge - g), vn, ((0,), (0,))),
        states, g_end, ks, gc, v_new)
    return (o, new_states, t_inv) if return_inverses else (o, new_states)


def _dn_chunk_specs(order):
    hd, nh, hps = DN_HEAD_DIM, DN_HEADS, DN_HEADS_PER_STEP
    qkv = lambda part: pl.BlockSpec((CHUNK, hps * hd), lambda h, n: (order(n), part * (nh // hps) + h))
    gb = pl.BlockSpec((CHUNK, 128), lambda h, n: (order(n), 0))
    state = pl.BlockSpec((hps, None, hd, hd), lambda h, n: (h, order(n), 0, 0))
    inverse = pl.BlockSpec((hps, None, CHUNK, CHUNK), lambda h, n: (h, order(n), 0, 0))
    return qkv, gb, state, inverse


def _dn_chunk_fwd(qkv, gb):
    l = qkv.shape[0]
    hd, nh, hps = DN_HEAD_DIM, DN_HEADS, DN_HEADS_PER_STEP
    n_chunks = l // CHUNK
    qkv_spec, gb_spec, state_spec, inverse_spec = _dn_chunk_specs(lambda n: n)

    def body(q_ref, k_ref, v_ref, gb_ref, o_ref, s_ref, t_ref, state):
        @pl.when(pl.program_id(1) == 0)
        def _():
            state[...] = jnp.zeros_like(state)

        cols = [slice(i * hd, (i + 1) * hd) for i in range(hps)]
        states = [state[i] for i in range(hps)]
        for i in range(hps):
            s_ref[i] = states[i]
        o, new_states, t_inv = _dn_chunk_fn(
            states, [q_ref[:, cs] for cs in cols], [k_ref[:, cs] for cs in cols],
            [v_ref[:, cs] for cs in cols], gb_ref[...],
            [pl.program_id(0) * hps + i for i in range(hps)], return_inverses=True)
        for i in range(hps):
            o_ref[:, cols[i]] = o[i]
            t_ref[i] = t_inv[i]
            state[i] = new_states[i]

    return pl.pallas_call(
        body, name="dn_chunk_fwd",
        out_shape=(jax.ShapeDtypeStruct((l, nh * hd), F32),
                   jax.ShapeDtypeStruct((nh, n_chunks, hd, hd), F32),
                   jax.ShapeDtypeStruct((nh, n_chunks, CHUNK, CHUNK), F32)),
        grid=(nh // hps, n_chunks),
        in_specs=[qkv_spec(0), qkv_spec(1), qkv_spec(2), gb_spec],
        out_specs=(pl.BlockSpec((CHUNK, hps * hd), lambda h, n: (n, h)), state_spec, inverse_spec),
        scratch_shapes=[pltpu.VMEM((hps, hd, hd), F32)],
        compiler_params=_cparams("parallel", "arbitrary"),
    )(qkv, qkv, qkv, gb)


def _dn_chunk_bwd(qkv, gb, states, inverses, do):
    l = qkv.shape[0]
    hd, nh, hps = DN_HEAD_DIM, DN_HEADS, DN_HEADS_PER_STEP
    n_chunks = l // CHUNK
    rev = lambda n: n_chunks - 1 - n
    qkv_spec, gb_spec, state_spec, inverse_spec = _dn_chunk_specs(rev)

    def body(q_ref, k_ref, v_ref, gb_ref, s_ref, t_ref, do_ref, dqkv_ref, dgb_ref, dstate):
        @pl.when(pl.program_id(1) == 0)
        def _():
            dstate[...] = jnp.zeros_like(dstate)

        cols = [slice(i * hd, (i + 1) * hd) for i in range(hps)]
        fn = functools.partial(_dn_chunk_fn, heads=[pl.program_id(0) * hps + i for i in range(hps)],
                               kept_inverses=[t_ref[i] for i in range(hps)])
        _, vjp = jax.vjp(fn, [s_ref[i] for i in range(hps)], [q_ref[:, cs] for cs in cols],
                         [k_ref[:, cs] for cs in cols], [v_ref[:, cs] for cs in cols], gb_ref[...])
        ds, dq, dk, dv, dgb = vjp(([do_ref[:, cs] for cs in cols], [dstate[i] for i in range(hps)]))
        for i in range(hps):
            dstate[i] = ds[i]
            dqkv_ref[0, :, cols[i]] = dq[i]
            dqkv_ref[1, :, cols[i]] = dk[i]
            dqkv_ref[2, :, cols[i]] = dv[i]
        dgb_ref[...] = dgb

    head_out = pl.BlockSpec((CHUNK, hps * hd), lambda h, n: (rev(n), h))
    return pl.pallas_call(
        body, name="dn_chunk_bwd",
        out_shape=(jax.ShapeDtypeStruct((3, l, nh * hd), F32),
                   jax.ShapeDtypeStruct((nh // hps, l, 128), F32)),
        grid=(nh // hps, n_chunks),
        in_specs=[qkv_spec(0), qkv_spec(1), qkv_spec(2), gb_spec, state_spec, inverse_spec, head_out],
        out_specs=(pl.BlockSpec((3, CHUNK, hps * hd), lambda h, n: (0, rev(n), h)),
                   pl.BlockSpec((None, CHUNK, 128), lambda h, n: (h, rev(n), 0))),
        scratch_shapes=[pltpu.VMEM((hps, hd, hd), F32)],
        compiler_params=_cparams("parallel", "arbitrary"),
    )(qkv, qkv, qkv, gb, states, inverses, do)


def _dn_out_fn(o, z, w):
    return _rmsnorm(o, w) * _silu(z)


def _dn_out_fwd(o, proj, off_z, w):
    l, d = o.shape
    hd = DN_HEAD_DIM
    tr = min(4 * ROW_TILE, l)
    blk = lambda off: pl.BlockSpec((tr, hd), lambda i, h: (i, off // hd + h))

    def body(o_ref, z_ref, w_ref, out_ref):
        out_ref[...] = _dn_out_fn(o_ref[...], z_ref[...], w_ref[...]).astype(BF16)

    return pl.pallas_call(
        body, name="dn_out_fwd",
        out_shape=jax.ShapeDtypeStruct((l, d), BF16),
        grid=(l // tr, d // hd),
        in_specs=[blk(0), blk(off_z), pl.BlockSpec((1, hd), lambda i, h: (0, 0))],
        out_specs=blk(0),
        compiler_params=_cparams("parallel", "parallel"),
    )(o, proj, w)


def _dn_out_bwd(o, proj, off_z, w, dout):
    l, d = o.shape
    hd = DN_HEAD_DIM
    tr = min(4 * ROW_TILE, l)
    blk = lambda off: pl.BlockSpec((tr, hd), lambda i, h: (i, off // hd + h))
    vec = pl.BlockSpec((1, hd), lambda i, h: (0, 0))

    def body(o_ref, z_ref, w_ref, dout_ref, do_ref, dz_ref, dw_ref):
        _, vjp = jax.vjp(_dn_out_fn, o_ref[...], z_ref[...], w_ref[...])
        do, dz, dw = vjp(dout_ref[...])
        do_ref[...] = do
        dz_ref[...] = dz.astype(BF16)

        @pl.when((pl.program_id(0) == 0) & (pl.program_id(1) == 0))
        def _():
            dw_ref[...] = jnp.zeros_like(dw_ref)

        dw_ref[...] += dw

    return pl.pallas_call(
        body, name="dn_out_bwd",
        out_shape=(jax.ShapeDtypeStruct((l, d), F32), jax.ShapeDtypeStruct((l, d), BF16),
                   jax.ShapeDtypeStruct((1, hd), F32)),
        grid=(l // tr, d // hd),
        in_specs=[blk(0), blk(off_z), vec, blk(0)],
        out_specs=(blk(0), blk(0), vec),
        compiler_params=_cparams("arbitrary", "arbitrary"),
    )(o, proj, w, dout)


def _tile_2d(rows, cols, budget_bytes=1 << 20):
    for tr in (rows, 4096, 2048, 1024, 512, 256, 128, 64, 32, 16):
        if tr <= rows and rows % tr == 0 and tr * cols * 4 <= budget_bytes:
            return tr, cols
    for tc in (2048, 1024, 512, 256, 128):
        if cols % tc == 0 and rows * tc * 4 <= 2 * budget_bytes:
            return rows, tc
    raise ValueError((rows, cols))


def _adamw_update(g, w_ref, m_ref, v_ref, go_ref, d_ref, mo_ref, vo_ref):
    c1 = 1.0 / (1.0 - ADAM_B1 ** ADAM_STEP)
    c2 = 1.0 / (1.0 - ADAM_B2 ** ADAM_STEP)
    m_new = ADAM_B1 * m_ref[...] + (1.0 - ADAM_B1) * g
    v_new = ADAM_B2 * v_ref[...] + (1.0 - ADAM_B2) * (g * g)
    go_ref[...] = g
    mo_ref[...] = m_new
    vo_ref[...] = v_new
    d_ref[...] = -ADAM_LR * ((m_new * c1) / (jnp.sqrt(v_new * c2) + ADAM_EPS) + ADAM_WD * w_ref[...])


def _adamw(w, m, v, gslots, name):
    rows, cols = w.shape
    ns = gslots.shape[0]
    tr, tc = _tile_2d(rows, cols)

    def body(w_ref, m_ref, v_ref, g_ref, go_ref, d_ref, mo_ref, vo_ref):
        g = g_ref[0].astype(F32)
        for s in range(1, ns):
            g = g + g_ref[s].astype(F32)
        _adamw_update(g, w_ref, m_ref, v_ref, go_ref, d_ref, mo_ref, vo_ref)

    blk = pl.BlockSpec((tr, tc), lambda i, j: (i, j))
    o = jax.ShapeDtypeStruct((rows, cols), F32)
    return pl.pallas_call(
        body, name=name, out_shape=(o, o, o, o),
        grid=(rows // tr, cols // tc),
        in_specs=[blk, blk, blk, pl.BlockSpec((ns, tr, tc), lambda i, j: (0, i, j))],
        out_specs=(blk, blk, blk, blk),
        compiler_params=_cparams("parallel", "parallel"),
    )(w, m, v, gslots)


def _slot_sum(gslots, name):
    ns, rows, cols = gslots.shape
    tr, tc = _tile_2d(rows, cols)

    def body(g_ref, o_ref):
        g = g_ref[0]
        for s in range(1, ns):
            g = g + g_ref[s]
        o_ref[...] = g

    return pl.pallas_call(
        body, name=name, out_shape=jax.ShapeDtypeStruct((rows, cols), F32),
        grid=(rows // tr, cols // tc),
        in_specs=[pl.BlockSpec((ns, tr, tc), lambda i, j: (0, i, j))],
        out_specs=pl.BlockSpec((tr, tc), lambda i, j: (i, j)),
        compiler_params=_cparams("parallel", "parallel"),
    )(gslots)


HBM_SPEC = pl.BlockSpec(memory_space=pl.ANY)


def _all_gather(arrs, name, relayed=(), after=None):
    n = len(arrs)
    n_sems = 13
    n_in = n + (after is not None)

    def body(*refs):
        ins, outs = refs[:n], refs[n_in:n_in + n]
        send_sems, recv_sems, local_sems = refs[n_in + n:]
        x, y, c = lax.axis_index("x"), lax.axis_index("y"), lax.axis_index("c")
        me, sibling = (x, y, c), (x, y, 1 - c)
        chips = [(1 - x, y), (x, 1 - y), (1 - x, 1 - y)]
        index = lambda px, py, pc: 4 * px + 2 * py + pc

        def copy(a, k, block, to, src=None, cols=None):
            dst = outs[a].at[index(*block)]
            src = dst if src is None else src
            if cols is not None:
                dst, src = dst.at[:, cols], src.at[:, cols]
            return pltpu.make_async_remote_copy(
                src_ref=src, dst_ref=dst, send_sem=send_sems.at[a, k], recv_sem=recv_sems.at[a, k],
                device_id=to, device_id_type=MESH)

        mine = [pltpu.make_async_copy(ins[a], outs[a].at[index(*me)], local_sems.at[a])
                for a in range(n)]
        for cp in mine:
            cp.start()
        sends = []

        def start(cp):
            cp.start()
            sends.append(cp)

        halves = {a: (pl.ds(0, arrs[a].shape[1] // 2), pl.ds(arrs[a].shape[1] // 2, arrs[a].shape[1] // 2))
                  for a in relayed}
        near_x, near_y, far = [(*chip, c) for chip in chips]
        for a in range(n):
            start(copy(a, 0, me, sibling, src=ins[a]))
            if a in relayed:
                left, right = halves[a]
                for k, to, cols in ((1, near_x, left), (3, near_y, right), (2, near_x, right), (4, near_y, left)):
                    start(copy(a, k, me, to, src=ins[a], cols=cols))
            else:
                for j, chip in enumerate(chips):
                    start(copy(a, 1 + j, me, (*chip, c), src=ins[a]))
        for a in relayed:
            left, right = halves[a]
            for k, block, cols, onward, to_sibling in (
                    (1, near_x, left, (5, near_y), 7), (3, near_y, right, (6, near_x), 10),
                    (2, near_x, right, None, 8), (4, near_y, left, None, 9),
                    (5, far, left, None, 11), (6, far, right, None, 12)):
                copy(a, k, block, me, cols=cols).wait_recv()
                if onward is not None:
                    start(copy(a, onward[0], block, onward[1], cols=cols))
                start(copy(a, to_sibling, block, sibling, cols=cols))
        for j, chip in enumerate(chips):
            for a in range(n):
                if a not in relayed:
                    copy(a, 1 + j, (*chip, c), me).wait_recv()
                    start(copy(a, 4 + j, (*chip, c), sibling))
        for a in range(n):
            copy(a, 0, sibling, me).wait_recv()
            if a in relayed:
                left, right = halves[a]
                for k, chip, cols in ((7, chips[0], left), (8, chips[0], right), (9, chips[1], left),
                                      (10, chips[1], right), (11, chips[2], left), (12, chips[2], right)):
                    copy(a, k, (*chip, 1 - c), me, cols=cols).wait_recv()
            else:
                for j, chip in enumerate(chips):
                    copy(a, 4 + j, (*chip, 1 - c), me).wait_recv()
        for cp in sends:
            cp.wait_send()
        for cp in mine:
            cp.wait()

    return pl.pallas_call(
        body, name=name,
        out_shape=[jax.ShapeDtypeStruct((N_DEV,) + a.shape, a.dtype) for a in arrs],
        in_specs=[HBM_SPEC] * n_in, out_specs=[HBM_SPEC] * n,
        scratch_shapes=[pltpu.SemaphoreType.DMA((n, n_sems)), pltpu.SemaphoreType.DMA((n, n_sems)),
                        pltpu.SemaphoreType.DMA((n,))],
    )(*arrs, *([after] if after is not None else []))


def _sibling_swap(arrs, name):
    n = len(arrs)

    def body(*refs):
        ins, outs = refs[:n], refs[n:2 * n]
        send_sems, recv_sems = refs[2 * n:]
        x, y, c = lax.axis_index("x"), lax.axis_index("y"), lax.axis_index("c")
        copies = [pltpu.make_async_remote_copy(
            src_ref=ins[a].at[:, 1 - c], dst_ref=outs[a],
            send_sem=send_sems.at[a], recv_sem=recv_sems.at[a],
            device_id=(x, y, 1 - c), device_id_type=MESH) for a in range(n)]
        for cp in copies:
            cp.start()
        for cp in copies:
            cp.wait()

    return pl.pallas_call(
        body, name=name,
        out_shape=[jax.ShapeDtypeStruct(a.shape[:1] + a.shape[2:], a.dtype) for a in arrs],
        in_specs=[HBM_SPEC] * n, out_specs=[HBM_SPEC] * n,
        scratch_shapes=[pltpu.SemaphoreType.DMA((n,)), pltpu.SemaphoreType.DMA((n,))],
    )(*arrs)


def _pair_sum(mine, theirs, core, name):
    chips, _, rows, cols = mine.shape
    tr, tc = _tile_2d(rows, cols, budget_bytes=2 << 20)

    def body(core_ref, a_ref, b_ref, o_ref):
        o_ref[...] = (a_ref[...].astype(F32) + b_ref[...].astype(F32)).astype(o_ref.dtype)

    slab = pl.BlockSpec((None, tr, tc), lambda ch, i, j, core_ref: (ch, i, j))
    return pl.pallas_call(
        body, name=name, out_shape=jax.ShapeDtypeStruct((chips, rows, cols), mine.dtype),
        grid_spec=pltpu.PrefetchScalarGridSpec(
            num_scalar_prefetch=1, grid=(chips, rows // tr, cols // tc),
            in_specs=[pl.BlockSpec((None, None, tr, tc),
                                   lambda ch, i, j, core_ref: (ch, core_ref[0], i, j)), slab],
            out_specs=slab),
        compiler_params=_cparams("parallel", "parallel", "parallel"),
    )(core, mine, theirs)


HBM_ONLY = pl.BlockSpec(memory_space=pltpu.HBM)
SEM_SPEC = pl.BlockSpec(memory_space=pltpu.SEMAPHORE)
SPLIT_COPY_EFFECT = pltpu.SideEffectType.DATAFLOW_SIDE_EFFECTING


def _flip(v, bit):
    return 1 - v if bit else v


def _chip_slices_plan(n):
    def plan():
        x, y, c = lax.axis_index("x"), lax.axis_index("y"), lax.axis_index("c")
        copies = []
        for k in range(1, 4):
            px, py = _flip(x, k & 2), _flip(y, k & 1)
            copies += [(a, 2 * px + py, 2 * x + y, (px, py, c)) for a in range(n)]
        return copies
    return plan, 3 * n


def _gather_plan(n):
    def plan():
        x, y, c = lax.axis_index("x"), lax.axis_index("y"), lax.axis_index("c")
        copies = []
        for k in range(1, N_DEV):
            peer = (_flip(x, k & 4), _flip(y, k & 2), _flip(c, k & 1))
            copies += [(a, None, 4 * x + 2 * y + c, peer) for a in range(n)]
        return copies
    return plan, 7 * n


def _planned_copies(plan, srcs, lands, send_sems, recv_sems):
    return [pltpu.make_async_remote_copy(
        src_ref=srcs[a] if src_at is None else srcs[a].at[src_at], dst_ref=lands[a].at[land_at],
        send_sem=send_sems[i], recv_sem=recv_sems[i], device_id=peer, device_id_type=MESH)
        for i, (a, src_at, land_at, peer) in enumerate(plan())]


def _split_exchange_start(plan_and_count, arrs, land_shapes, name, after=None):
    plan, n_sems = plan_and_count
    n = len(arrs)

    n_in = 2 * n + (after is not None)

    def body(*refs):
        srcs, lands = refs[:n], refs[n:2 * n]
        send_sems, recv_sems = refs[n_in:n_in + n_sems], refs[n_in + n_sems:n_in + 2 * n_sems]
        token = refs[-1]
        for copy in _planned_copies(plan, srcs, lands, send_sems, recv_sems):
            copy.start()
        token[...] = jnp.zeros_like(token)

    hbm = lambda a: pltpu.HBM(a.shape, a.dtype)
    operands = [pltpu.with_memory_space_constraint(a, pltpu.HBM) for a in arrs]
    operands += [pltpu.with_memory_space_constraint(lax.empty(shape, a.dtype), pltpu.HBM)
                 for a, shape in zip(arrs, land_shapes)]
    out = pl.pallas_call(
        body, name=name,
        out_shape=(*[pltpu.SemaphoreType.DMA(())] * (2 * n_sems),
                   *[hbm(a) for a in operands],
                   jax.ShapeDtypeStruct((8, 128), F32)),
        in_specs=[HBM_ONLY] * (2 * n) + [pl.BlockSpec(memory_space=pl.ANY)] * (after is not None),
        out_specs=(*[SEM_SPEC] * (2 * n_sems), *[HBM_ONLY] * (2 * n),
                   pl.BlockSpec(memory_space=pltpu.VMEM)),
        input_output_aliases={i: 2 * n_sems + i for i in range(2 * n)},
        compiler_params=pltpu.CompilerParams(has_side_effects=SPLIT_COPY_EFFECT),
    )(*operands, *([after] if after is not None else []))
    sems, rest = list(out[:2 * n_sems]), out[2 * n_sems:]
    return sems, list(rest[:n]), list(rest[n:2 * n]), rest[-1]


def _split_exchange_wait(plan_and_count, sems, srcs, lands, after, name):
    plan, n_sems = plan_and_count
    n = len(srcs)

    def body(*refs):
        src_refs, land_refs = refs[:n], refs[n:2 * n]
        send_sems, recv_sems = refs[2 * n:2 * n + n_sems], refs[2 * n + n_sems:2 * n + 2 * n_sems]
        for copy in _planned_copies(plan, src_refs, land_refs, send_sems, recv_sems):
            copy.wait_send()
            copy.wait_recv()

    hbm = lambda a: pltpu.HBM(a.shape, a.dtype)
    out = pl.pallas_call(
        body, name=name,
        out_shape=(*[hbm(a) for a in srcs], *[hbm(a) for a in lands]),
        in_specs=[HBM_ONLY] * (2 * n) + [SEM_SPEC] * (2 * n_sems) + [pl.BlockSpec(memory_space=pl.ANY)],
        out_specs=tuple([HBM_ONLY] * (2 * n)),
        input_output_aliases={i: i for i in range(2 * n)},
        compiler_params=pltpu.CompilerParams(has_side_effects=SPLIT_COPY_EFFECT),
    )(*srcs, *lands, *sems, after)
    return list(out[n:])


def _adamw_exchanged(w, m, v, own, landed, chip, name):
    rows, cols = w.shape
    tr, tc = _tile_2d(rows, cols)

    def body(chip_ref, w_ref, m_ref, v_ref, own_ref, l1_ref, l2_ref, l3_ref, go_ref, d_ref, mo_ref, vo_ref):
        g = own_ref[...].astype(F32)
        for ref in (l1_ref, l2_ref, l3_ref):
            g = g + ref[...].astype(F32)
        _adamw_update(g, w_ref, m_ref, v_ref, go_ref, d_ref, mo_ref, vo_ref)

    blk = pl.BlockSpec((tr, tc), lambda i, j, chip_ref: (i, j))
    slot = lambda k: pl.BlockSpec((None, tr, tc), lambda i, j, chip_ref: (chip_ref[0] ^ k, i, j))
    o = jax.ShapeDtypeStruct((rows, cols), F32)
    return pl.pallas_call(
        body, name=name, out_shape=(o, o, o, o),
        grid_spec=pltpu.PrefetchScalarGridSpec(
            num_scalar_prefetch=1, grid=(rows // tr, cols // tc),
            in_specs=[blk, blk, blk, slot(0), slot(1), slot(2), slot(3)],
            out_specs=(blk, blk, blk, blk)),
        compiler_params=_cparams("parallel", "parallel"),
    )(chip, w, m, v, own, landed, landed, landed)


def _block_diag(t):
    nb, gpb, r, c = t.shape
    eye = jnp.eye(gpb, dtype=t.dtype)
    return jnp.einsum("ngrc,gh->ngrhc", t, eye).reshape(nb, gpb * r, gpb * c)


def _diag_blocks(t, r, c):
    nb = t.shape[0]
    gpb = t.shape[1] // r
    t = t.reshape(nb, gpb, r, gpb, c)
    return jnp.einsum("ngrhc,gh->ngrc", t, jnp.eye(gpb, dtype=t.dtype))


def _pack_rows(parts):
    flat = jnp.concatenate([p.reshape(-1).astype(F32) for p in parts])
    pad = (-flat.shape[0]) % (256 * 128)
    return jnp.pad(flat, (0, pad)).reshape(-1, 128)


def _unpack_rows(packed, shapes):
    flat = packed.reshape(-1)
    out, at = [], 0
    for shape in shapes:
        size = math.prod(shape)
        out.append(flat[at:at + size].reshape(shape))
        at += size
    return out


def kernel(x, ln_w, w_in, s5_lam_re, s5_lam_im, s5_log_step, s5_b_re, s5_b_im, s5_c_re, s5_c_im, s5_d, s5_w_glu, s5_w_up, dn_conv_w, dn_a_log, dn_dt_bias, dn_norm_w, dn_w_up, w_out, final_norm_w, loss_target, m_ln_w, m_w_in, m_s5_lam_re, m_s5_lam_im, m_s5_log_step, m_s5_b_re, m_s5_b_im, m_s5_c_re, m_s5_c_im, m_s5_d, m_s5_w_glu, m_s5_w_up, m_dn_conv_w, m_dn_a_log, m_dn_dt_bias, m_dn_norm_w, m_dn_w_up, m_w_out, m_final_norm_w, v_ln_w, v_w_in, v_s5_lam_re, v_s5_lam_im, v_s5_log_step, v_s5_b_re, v_s5_b_im, v_s5_c_re, v_s5_c_im, v_s5_d, v_s5_w_glu, v_s5_w_up, v_dn_conv_w, v_dn_a_log, v_dn_dt_bias, v_dn_norm_w, v_dn_w_up, v_w_out, v_final_norm_w):
    weights = dict(ln_w=ln_w, w_in=w_in, s5_lam_re=s5_lam_re, s5_lam_im=s5_lam_im,
                   s5_log_step=s5_log_step, s5_b_re=s5_b_re, s5_b_im=s5_b_im, s5_c_re=s5_c_re,
                   s5_c_im=s5_c_im, s5_d=s5_d, s5_w_glu=s5_w_glu, s5_w_up=s5_w_up,
                   dn_conv_w=dn_conv_w, dn_a_log=dn_a_log, dn_dt_bias=dn_dt_bias,
                   dn_norm_w=dn_norm_w, dn_w_up=dn_w_up, w_out=w_out, final_norm_w=final_norm_w)
    mom_m = dict(ln_w=m_ln_w, w_in=m_w_in, s5_lam_re=m_s5_lam_re, s5_lam_im=m_s5_lam_im,
                 s5_log_step=m_s5_log_step, s5_b_re=m_s5_b_re, s5_b_im=m_s5_b_im,
                 s5_c_re=m_s5_c_re, s5_c_im=m_s5_c_im, s5_d=m_s5_d, s5_w_glu=m_s5_w_glu,
                 s5_w_up=m_s5_w_up, dn_conv_w=m_dn_conv_w, dn_a_log=m_dn_a_log,
                 dn_dt_bias=m_dn_dt_bias, dn_norm_w=m_dn_norm_w, dn_w_up=m_dn_w_up,
                 w_out=m_w_out, final_norm_w=m_final_norm_w)
    mom_v = dict(ln_w=v_ln_w, w_in=v_w_in, s5_lam_re=v_s5_lam_re, s5_lam_im=v_s5_lam_im,
                 s5_log_step=v_s5_log_step, s5_b_re=v_s5_b_re, s5_b_im=v_s5_b_im,
                 s5_c_re=v_s5_c_re, s5_c_im=v_s5_c_im, s5_d=v_s5_d, s5_w_glu=v_s5_w_glu,
                 s5_w_up=v_s5_w_up, dn_conv_w=v_dn_conv_w, dn_a_log=v_dn_a_log,
                 dn_dt_bias=v_dn_dt_bias, dn_norm_w=v_dn_norm_w, dn_w_up=v_dn_w_up,
                 w_out=v_w_out, final_norm_w=v_final_norm_w)
    names = list(weights)

    l, d = x.shape[1], x.shape[2]
    d_s5 = d // 2
    groups = d_s5 // S5_GROUP
    nb = groups // S5_GPB
    d_dn = DN_HEADS * DN_HEAD_DIM
    w_in_cols = w_in.shape[2]
    d_in = N_DEV * w_in_cols
    off_ba_src = 2 * d_s5 + 4 * d_dn
    off_u, off_zs, off_qkv, off_zd = 0, d_s5, 2 * d_s5, 2 * d_s5 + 3 * d_dn
    off_ba = off_zd + d_dn
    n_main = off_ba + BA_PAD
    off_gs, off_gd = 0, d
    x2d, tgt2d = x[0], loss_target[0]
    my_index = 4 * lax.axis_index("x") + 2 * lax.axis_index("y") + lax.axis_index("c")

    g_win, g_conv = _all_gather([jnp.transpose(w_in[0]).astype(BF16), dn_conv_w[0]], name="gather_weights",
                                relayed=(0,))
    late_plan = _gather_plan(4)
    late_shards = [s5_w_glu[0].astype(BF16), s5_w_up[0].astype(BF16), dn_w_up[0].astype(BF16),
                   w_out[0].astype(BF16)]
    late_sems, late_shards, late_lands, late_token = _split_exchange_start(
        late_plan, late_shards, [(N_DEV,) + s.shape for s in late_shards], name="gather_late_start",
        after=g_conv)
    ba_end = off_ba_src + 2 * DN_HEADS
    w_full_t = g_win.reshape(d_in, d)
    w_gates_t = w_full_t[ba_end:]
    conv_full = jnp.transpose(g_conv, (1, 0, 2)).reshape(CONV_K, 3 * d_dn)

    lam_re, lam_im = s5_lam_re[0], s5_lam_im[0]
    log_step = s5_log_step[0].reshape(groups, 1)
    b_re = s5_b_re[0].reshape(groups * S5_STATE, S5_GROUP)
    b_im = s5_b_im[0].reshape(groups * S5_STATE, S5_GROUP)
    abar_re, abar_im, f_re, f_im = _s5_disc_fwd(lam_re, lam_im, log_step)
    f_re_col, f_im_col = f_re.reshape(-1, 1), f_im.reshape(-1, 1)
    bb_re, bb_im = _s5_bbar_fwd(f_re_col, f_im_col, b_re, b_im)

    def bb_blocks(t):
        t = t.reshape(nb, S5_GPB, S5_STATE, S5_GROUP).transpose(0, 1, 3, 2)
        return _block_diag(t).astype(BF16)

    def c_blocks(t):
        return _block_diag(t.reshape(nb, S5_GPB, S5_GROUP, S5_STATE)).astype(BF16)

    bbr, bbi = bb_blocks(bb_re), bb_blocks(bb_im)
    cbr, cbi = c_blocks(s5_c_re[0]), c_blocks(s5_c_im[0])
    ctr, cti = jnp.transpose(cbr, (0, 2, 1)), jnp.transpose(cbi, (0, 2, 1))
    a_re = abar_re.reshape(nb, 1, S5_GPB * S5_STATE)
    a_im = abar_im.reshape(nb, 1, S5_GPB * S5_STATE)

    h = _rms_fwd(x2d, ln_w)
    proj = _mm(h, w_full_t, tb=True, b_rows=n_main, tm=1024, tn=512, after=late_token, name="proj")
    proj_gates = _mm(h, w_gates_t, tb=True, tm=1024, tn=1024, name="proj_gates")
    y1, car_r, car_i, states_r, states_i = _s5_fwd(proj, bbr, bbi, a_re, a_im, ctr, cti, s5_d, d_s5)
    a_log_row = jnp.pad(dn_a_log, ((0, 0), (DN_HEADS, 128 - 2 * DN_HEADS)))
    dt_row = jnp.pad(dn_dt_bias, ((0, 0), (DN_HEADS, 128 - 2 * DN_HEADS)))
    qkv = _dn_prep_fwd(proj, off_qkv, conv_full)
    gb = _dn_gates_fwd(proj, off_ba, a_log_row, dt_row)
    o_dn, states, inverses = _dn_chunk_fwd(qkv, gb)

    late_lands = _split_exchange_wait(late_plan, late_sems, late_shards, late_lands, o_dn,
                                      name="gather_late_wait")
    g_glu, g_sup, g_dup, g_wout = [
        lax.dynamic_update_slice(land, shard[None], (my_index, 0, 0))
        for land, shard in zip(late_lands, late_shards)]
    wglu_full = g_glu.reshape(d_s5, d_s5)
    wsup_full = jnp.transpose(g_sup, (1, 0, 2)).reshape(d_s5, d)
    wdup_full = jnp.transpose(g_dup, (1, 0, 2)).reshape(d_dn, d)
    wout_full = g_wout.reshape(d, d)

    out_s = _s5_glu_fwd(y1, proj, off_zs, wglu_full)
    y_s = _mm(out_s, wsup_full, name="s5_up")
    out_d = _dn_out_fwd(o_dn, proj, off_zd, dn_norm_w)
    y_d = _mm(out_d, wdup_full, name="dn_up")

    mixed = _merge_fwd(proj_gates, off_gs, off_gd, y_s, y_d)
    branch = _mm(mixed, wout_full, name="w_out")
    dx2, dx2_bf, loss_dev, d_final_w = _final(x2d, branch, final_norm_w.reshape(1, d), tgt2d)

    g_wout_full = _mm(mixed, dx2_bf, ta=True, out_dtype=BF16, name="grad_w_out")
    dmixed = _mm(dx2_bf, wout_full, tb=True, name="d_mixed")
    dgs, dgd, dys, dyd = _merge_bwd(proj_gates, off_gs, off_gd, y_s, y_d, dmixed)

    g_dup_full = _mm(out_d, dyd, ta=True, out_dtype=BF16, name="grad_dn_up")
    dout_d = _mm(dyd, wdup_full, tb=True, name="d_out_d")
    do_dn, dzd, d_norm_w = _dn_out_bwd(o_dn, proj, off_zd, dn_norm_w, dout_d)
    dqkv, dgb_heads = _dn_chunk_bwd(qkv, gb, states, inverses, do_dn)
    dba, d_a_log_row, d_dt_row = _dn_gates_bwd(proj, off_ba, a_log_row, dt_row, dgb_heads)
    dqkv_pre, d_conv_full = _dn_prep_bwd(proj, off_qkv, conv_full, dqkv)

    g_sup_full = _mm(out_s, dys, ta=True, out_dtype=BF16, name="grad_s5_up")
    dout_s = _mm(dys, wsup_full, tb=True, name="d_out_s")
    dy1, dzs, g_glu_full = _s5_glu_bwd(y1, proj, off_zs, wglu_full, dout_s)

    def by_dest(t, axis=0):
        if axis == 1:
            return t.reshape(t.shape[0], 4, 2, t.shape[1] // N_DEV).transpose(1, 2, 0, 3)
        return t.reshape(4, 2, t.shape[0] // N_DEV, t.shape[1])

    core = lax.axis_index("c").astype(jnp.int32).reshape(1)
    chip = (2 * lax.axis_index("x") + lax.axis_index("y")).astype(jnp.int32).reshape(1)

    def chip_sums_of(which, parts, tag):
        from_sibling = _sibling_swap(parts, name="swap_grads_" + tag)
        return [_pair_sum(p, got, core, name="pair_sum_" + nm)
                for nm, p, got in zip(which, parts, from_sibling)]

    early = ["s5_w_glu", "s5_w_up", "dn_w_up", "w_out"]
    sums_a = chip_sums_of(early, [by_dest(g_glu_full.astype(BF16)), by_dest(g_sup_full, 1),
                                  by_dest(g_dup_full, 1), by_dest(g_wout_full)], "a")
    plan_a = _chip_slices_plan(len(sums_a))
    sems_a, src_a, land_a, token_a = _split_exchange_start(
        plan_a, sums_a, [t.shape for t in sums_a], name="exchange_start_a")

    (du, d_a_re, d_a_im, d_bbr, d_bbi, d_cbr, d_cbi, d_s5_d) = _s5_bwd(
        proj, dy1, bbr, bbi, a_re, a_im, cbr, cbi, s5_d + token_a[:1, :1], car_r, car_i,
        states_r, states_i)

    def from_bb_blocks(t):
        t = _diag_blocks(t, S5_GROUP, S5_STATE).transpose(0, 1, 3, 2)
        return t.reshape(groups * S5_STATE, S5_GROUP)

    d_f_re, d_f_im, d_b_re, d_b_im = _s5_bbar_bwd(f_re_col, f_im_col, b_re, b_im,
                                                 from_bb_blocks(d_bbr), from_bb_blocks(d_bbi))
    d_lam_re, d_lam_im, d_log_step = _s5_disc_bwd(
        lam_re, lam_im, log_step, d_a_re.reshape(groups, S5_STATE), d_a_im.reshape(groups, S5_STATE),
        d_f_re.reshape(groups, S5_STATE), d_f_im.reshape(groups, S5_STATE))
    d_c_re = _diag_blocks(d_cbr, S5_GROUP, S5_STATE).reshape(groups, S5_GROUP, S5_STATE)
    d_c_im = _diag_blocks(d_cbi, S5_GROUP, S5_STATE).reshape(groups, S5_GROUP, S5_STATE)

    dproj = jnp.concatenate([du, dzs, dqkv_pre, dzd, jnp.pad(dba, ((0, 0), (0, BA_PAD - 128)))], axis=1)
    dproj_gates = jnp.concatenate([dgs, dgd], axis=1)
    g_main_t = _mm(dproj, h, ta=True, out_dtype=BF16, tm=512, tn=d, name="grad_w_in")
    g_gates_t = _mm(dproj_gates, h, ta=True, out_dtype=BF16, tm=512, tn=d, name="grad_w_in_gates")
    g_win_full_t = jnp.concatenate([g_main_t[:ba_end], g_gates_t], axis=0)
    sums_b = chip_sums_of(["w_in"], [by_dest(g_win_full_t)], "b")
    plan_b = _chip_slices_plan(1)
    sems_b, src_b, land_b, token_b = _split_exchange_start(
        plan_b, sums_b, [t.shape for t in sums_b], name="exchange_start_b")
    dh_main = _mm(dproj, w_full_t, b_rows=n_main, tm=1024, tn=1024, tk=n_main // 4, after=token_b,
                  name="d_h_main")
    dh = _mm(dproj_gates, w_gates_t, tm=1024, tn=1024, tk=2048, addend=dh_main, name="d_h")
    grad_x, d_ln_w = _rms_bwd(x2d, ln_w, dh, dx2)
    big = ["w_in"] + early
    results = {}

    land_a = _split_exchange_wait(plan_a, sems_a, src_a, land_a, grad_x, name="exchange_wait_a")
    for nm, own, landed in zip(early, src_a, land_a):
        results[nm] = _adamw_exchanged(weights[nm][0], mom_m[nm][0], mom_v[nm][0], own, landed, chip,
                                       name="adamw_" + nm)

    small = [nm for nm in names if nm not in big]
    small_grads = dict(
        ln_w=d_ln_w, s5_lam_re=d_lam_re, s5_lam_im=d_lam_im, s5_log_step=d_log_step,
        s5_b_re=d_b_re, s5_b_im=d_b_im, s5_c_re=d_c_re, s5_c_im=d_c_im, s5_d=d_s5_d,
        dn_conv_w=d_conv_full, dn_a_log=d_a_log_row[:, DN_HEADS:2 * DN_HEADS],
        dn_dt_bias=d_dt_row[:, DN_HEADS:2 * DN_HEADS], dn_norm_w=d_norm_w, final_norm_w=d_final_w)
    (all_small,) = _all_gather([_pack_rows([small_grads[nm] for nm in small])], name="gather_small_grads",
                               after=results[early[-1]][0])
    summed = _slot_sum(all_small, name="sum_small_grads")
    full_shapes = [(CONV_K, 3 * d_dn) if nm == "dn_conv_w" else weights[nm].shape for nm in small]
    g_small = dict(zip(small, _unpack_rows(summed, full_shapes)))
    conv_cols = dn_conv_w.shape[2]
    g_small["dn_conv_w"] = lax.dynamic_slice_in_dim(
        g_small["dn_conv_w"], my_index * conv_cols, conv_cols, axis=1).reshape(dn_conv_w.shape)
    packed = [_pack_rows([t[nm] for nm in small]) for t in (weights, mom_m, mom_v, g_small)]
    small_out = _adamw(packed[0], packed[1], packed[2], packed[3][None], name="adamw_small")
    small_shapes = [weights[nm].shape for nm in small]
    for kind, packed_out in enumerate(small_out):
        for nm, val in zip(small, _unpack_rows(packed_out, small_shapes)):
            results.setdefault(nm, [None] * 4)[kind] = val

    (land_b,) = _split_exchange_wait(plan_b, sems_b, src_b, land_b, small_out[0], name="exchange_wait_b")
    res = _adamw_exchanged(jnp.transpose(w_in[0]), jnp.transpose(m_w_in[0]), jnp.transpose(v_w_in[0]),
                           src_b[0], land_b, chip, name="adamw_w_in")
    results["w_in"] = [jnp.transpose(t) for t in res]

    loss = lax.psum(loss_dev[0, 0], ("x", "y", "c"))
    outs = [loss, grad_x[None]]
    for kind in range(4):
        outs += [results[nm][kind].reshape(weights[nm].shape) for nm in names]
    return tuple(outs)
```

```python
import functools
import math

import jax
import jax.numpy as jnp
from jax import lax
from jax.experimental import pallas as pl
from jax.experimental.pallas import tpu as pltpu

F32 = jnp.float32
BF16 = jnp.bfloat16
HIGHEST = lax.Precision.HIGHEST
MESH = pl.DeviceIdType.MESH
N_DEV = 8

EPS = 1e-6
S5_GROUP = 16
S5_STATE = 64
S5_GPB = 8
S5_T = 256
DN_HEADS = 8
DN_HEAD_DIM = 128
CHUNK = 64
DN_HEADS_PER_STEP = 8
CONV_K = 4
BA_PAD = 512

ADAM_LR = 0.001
ADAM_B1 = 0.9
ADAM_B2 = 0.999
ADAM_EPS = 1e-08
ADAM_WD = 0.01
ADAM_STEP = 10

VMEM_LIMIT_BYTES = 48 * 1024 * 1024
ROW_TILE = 256


def _cparams(*sem):
    return pltpu.CompilerParams(dimension_semantics=sem if sem else None,
                                vmem_limit_bytes=VMEM_LIMIT_BYTES)


@jax.custom_jvp
def _sigmoid(x):
    return 1.0 / (1.0 + jnp.exp(-x))


@_sigmoid.defjvp
def _sigmoid_jvp(primals, tangents):
    s = _sigmoid(primals[0])
    return s, tangents[0] * (s * (1.0 - s))


def _silu(x):
    return x * _sigmoid(x)


def _gelu(x):
    return 0.5 * x * (1.0 + jnp.tanh(0.7978845608028654 * (x + 0.044715 * x * x * x)))


def _softplus(x):
    return jnp.maximum(x, 0.0) + jnp.log(1.0 + jnp.exp(-jnp.abs(x)))


def _rmsnorm(x, w):
    return x * lax.rsqrt(jnp.mean(x * x, axis=-1, keepdims=True) + EPS) * w


def _dot(a, b, dims=((1,), (0,)), precision=None):
    return lax.dot_general(a, b, (dims, ((), ())), precision=precision,
                           preferred_element_type=F32)


def _bdot(a, b, dims=((1,), (0,))):
    return _dot(a.astype(BF16), b.astype(BF16), dims)


def _split_bf16(a):
    hi = a.astype(BF16)
    return hi, (a - hi.astype(F32)).astype(BF16)


def _dot3_dims(a, b, dims):
    ah, al = _split_bf16(a)
    bh, bl = _split_bf16(b)
    return _dot(ah, bh, dims) + (_dot(ah, bl, dims) + _dot(al, bh, dims))


@jax.custom_vjp
def _dot3(a, b):
    return _dot3_dims(a, b, ((1,), (0,)))


def _dot3_fwd(a, b):
    return _dot3(a, b), (a, b)


def _dot3_bwd(res, g):
    a, b = res
    return _dot3_dims(g, b, ((1,), (1,))), _dot3_dims(a, g, ((0,), (0,)))


_dot3.defvjp(_dot3_fwd, _dot3_bwd)


def _mm(a, b, *, ta=False, tb=False, out_dtype=F32, tm=512, tn=512, tk=None, after=None, b_rows=None,
        addend=None, name):
    k_dim, m_dim = (a.shape if ta else a.shape[::-1])
    b_rows = b.shape[0] if b_rows is None else b_rows
    n_dim = b_rows if tb else b.shape[1]
    assert (b.shape[1] if tb else b_rows) == k_dim and b_rows <= b.shape[0]
    tm, tn = min(tm, m_dim), min(tn, n_dim)
    tk = k_dim if tk is None else tk
    assert m_dim % tm == 0 and n_dim % tn == 0 and k_dim % tk == 0
    nk = k_dim // tk
    a_spec = (pl.BlockSpec((tk, tm), lambda i, j, k: (k, i)) if ta
              else pl.BlockSpec((tm, tk), lambda i, j, k: (i, k)))
    b_spec = (pl.BlockSpec((tn, tk), lambda i, j, k: (j, k)) if tb
              else pl.BlockSpec((tk, tn), lambda i, j, k: (k, j)))
    dims = ((0 if ta else 1,), (1 if tb else 0,))

    extras = ([after] if after is not None else []) + ([addend] if addend is not None else [])
    extra_specs = ([pl.BlockSpec((8, 128), lambda i, j, k: (0, 0))] if after is not None else []) + (
        [pl.BlockSpec((tm, tn), lambda i, j, k: (i, j))] if addend is not None else [])

    def body(a_ref, b_ref, *rest):
        o_ref, *scratch = rest[len(extras):]
        p = _bdot(a_ref[...], b_ref[...], dims)
        finish = (lambda v: v + rest[len(extras) - 1][...]) if addend is not None else (lambda v: v)
        if nk == 1:
            o_ref[...] = finish(p).astype(o_ref.dtype)
        else:
            acc = scratch[0]
            k = pl.program_id(2)

            @pl.when(k == 0)
            def _():
                acc[...] = p

            @pl.when(k > 0)
            def _():
                acc[...] += p

            @pl.when(k == nk - 1)
            def _():
                o_ref[...] = finish(acc[...]).astype(o_ref.dtype)

    return pl.pallas_call(
        body, name=name,
        out_shape=jax.ShapeDtypeStruct((m_dim, n_dim), out_dtype),
        grid=(m_dim // tm, n_dim // tn, nk),
        in_specs=[a_spec, b_spec] + extra_specs,
        out_specs=pl.BlockSpec((tm, tn), lambda i, j, k: (i, j)),
        scratch_shapes=[pltpu.VMEM((tm, tn), F32)] if nk > 1 else [],
        compiler_params=_cparams("parallel", "parallel", "arbitrary"),
    )(a, b, *extras)


def _rms_fwd(x, w):
    l, d = x.shape

    def body(x_ref, w_ref, h_ref):
        h_ref[...] = _rmsnorm(x_ref[...], w_ref[...]).astype(BF16)

    return pl.pallas_call(
        body, name="rms_fwd",
        out_shape=jax.ShapeDtypeStruct((l, d), BF16),
        grid=(l // ROW_TILE,),
        in_specs=[pl.BlockSpec((ROW_TILE, d), lambda i: (i, 0)),
                  pl.BlockSpec((1, d), lambda i: (0, 0))],
        out_specs=pl.BlockSpec((ROW_TILE, d), lambda i: (i, 0)),
        compiler_params=_cparams("parallel"),
    )(x, w)


def _rms_bwd(x, w, dh, dres):
    l, d = x.shape

    def body(x_ref, w_ref, dh_ref, dres_ref, dx_ref, dw_ref):
        _, vjp = jax.vjp(_rmsnorm, x_ref[...], w_ref[...])
        dx, dw = vjp(dh_ref[...])
        dx_ref[...] = dx + dres_ref[...]

        @pl.when(pl.program_id(0) == 0)
        def _():
            dw_ref[...] = jnp.zeros_like(dw_ref)

        dw_ref[...] += dw

    row = pl.BlockSpec((ROW_TILE, d), lambda i: (i, 0))
    vec = pl.BlockSpec((1, d), lambda i: (0, 0))
    return pl.pallas_call(
        body, name="rms_bwd",
        out_shape=(jax.ShapeDtypeStruct((l, d), F32), jax.ShapeDtypeStruct((1, d), F32)),
        grid=(l // ROW_TILE,),
        in_specs=[row, vec, row, row],
        out_specs=(row, vec),
        compiler_params=_cparams("arbitrary"),
    )(x, w, dh, dres)


def _final(x, r, fw, target):
    l, d = x.shape

    def per_row_loss(x2, w, tgt):
        err = _rmsnorm(x2, w) - tgt
        return 0.5 * jnp.mean(err * err, axis=-1, keepdims=True)

    def body(x_ref, r_ref, w_ref, t_ref, dx_ref, dxb_ref, loss_ref, dw_ref):
        x2 = x_ref[...] + r_ref[...]
        rows, vjp = jax.vjp(functools.partial(per_row_loss, tgt=t_ref[...]), x2, w_ref[...])
        dx2, dw = vjp(jnp.ones_like(rows))
        dx_ref[...] = dx2
        dxb_ref[...] = dx2.astype(BF16)

        @pl.when(pl.program_id(0) == 0)
        def _():
            dw_ref[...] = jnp.zeros_like(dw_ref)
            loss_ref[...] = jnp.zeros_like(loss_ref)

        dw_ref[...] += dw
        loss_ref[...] += jnp.sum(rows, axis=0, keepdims=True)

    row = pl.BlockSpec((ROW_TILE, d), lambda i: (i, 0))
    vec = pl.BlockSpec((1, d), lambda i: (0, 0))
    return pl.pallas_call(
        body, name="final_norm_loss",
        out_shape=(jax.ShapeDtypeStruct((l, d), F32), jax.ShapeDtypeStruct((l, d), BF16),
                   jax.ShapeDtypeStruct((1, 1), F32), jax.ShapeDtypeStruct((1, d), F32)),
        grid=(l // ROW_TILE,),
        in_specs=[row, row, vec, row],
        out_specs=(row, row, pl.BlockSpec((1, 1), lambda i: (0, 0)), vec),
        compiler_params=_cparams("arbitrary"),
    )(x, r, fw, target)


def _merge_fn(gs, gd, ys, yd):
    return _sigmoid(gs) * ys + _sigmoid(gd) * yd


def _merge_fwd(proj, off_gs, off_gd, ys, yd):
    l, d = ys.shape
    cw = min(1024, d)
    blk = lambda off: pl.BlockSpec((ROW_TILE, cw), lambda i, j: (i, off // cw + j))

    def body(gs_ref, gd_ref, ys_ref, yd_ref, o_ref):
        o_ref[...] = _merge_fn(gs_ref[...], gd_ref[...], ys_ref[...], yd_ref[...]).astype(BF16)

    return pl.pallas_call(
        body, name="merge_fwd",
        out_shape=jax.ShapeDtypeStruct((l, d), BF16),
        grid=(l // ROW_TILE, d // cw),
        in_specs=[blk(off_gs), blk(off_gd), blk(0), blk(0)],
        out_specs=blk(0),
        compiler_params=_cparams("parallel", "parallel"),
    )(proj, proj, ys, yd)


def _merge_bwd(proj, off_gs, off_gd, ys, yd, dmixed):
    l, d = ys.shape
    cw = min(1024, d)
    blk = lambda off: pl.BlockSpec((ROW_TILE, cw), lambda i, j: (i, off // cw + j))

    def body(gs_ref, gd_ref, ys_ref, yd_ref, dm_ref, dgs_ref, dgd_ref, dys_ref, dyd_ref):
        _, vjp = jax.vjp(_merge_fn, gs_ref[...], gd_ref[...], ys_ref[...], yd_ref[...])
        dgs, dgd, dys, dyd = vjp(dm_ref[...])
        dgs_ref[...] = dgs.astype(BF16)
        dgd_ref[...] = dgd.astype(BF16)
        dys_ref[...] = dys.astype(BF16)
        dyd_ref[...] = dyd.astype(BF16)

    out = jax.ShapeDtypeStruct((l, d), BF16)
    return pl.pallas_call(
        body, name="merge_bwd",
        out_shape=(out, out, out, out),
        grid=(l // ROW_TILE, d // cw),
        in_specs=[blk(off_gs), blk(off_gd), blk(0), blk(0), blk(0)],
        out_specs=(blk(0), blk(0), blk(0), blk(0)),
        compiler_params=_cparams("parallel", "parallel"),
    )(proj, proj, ys, yd, dmixed)


def _s5_disc_fn(lam_re, lam_im, log_step):
    step = jnp.exp(log_step)
    mag = jnp.exp(lam_re * step)
    abar_re = mag * jnp.cos(lam_im * step)
    abar_im = mag * jnp.sin(lam_im * step)
    den = lam_re * lam_re + lam_im * lam_im
    xr = abar_re - 1.0
    f_re = (xr * lam_re + abar_im * lam_im) / den
    f_im = (abar_im * lam_re - xr * lam_im) / den
    return abar_re, abar_im, f_re, f_im


def _s5_disc_fwd(lam_re, lam_im, log_step):
    g, p = lam_re.shape

    def body(lr_ref, li_ref, ls_ref, ar_ref, ai_ref, fr_ref, fi_ref):
        ar, ai, fr, fi = _s5_disc_fn(lr_ref[...], li_ref[...], ls_ref[...])
        ar_ref[...] = ar
        ai_ref[...] = ai
        fr_ref[...] = fr
        fi_ref[...] = fi

    o = jax.ShapeDtypeStruct((g, p), F32)
    return pl.pallas_call(body, name="s5_disc_fwd", out_shape=(o, o, o, o),
                          compiler_params=_cparams())(lam_re, lam_im, log_step)


def _s5_disc_bwd(lam_re, lam_im, log_step, dar, dai, dfr, dfi):
    g, p = lam_re.shape

    def body(lr_ref, li_ref, ls_ref, dar_ref, dai_ref, dfr_ref, dfi_ref, dlr_ref, dli_ref, dls_ref):
        _, vjp = jax.vjp(_s5_disc_fn, lr_ref[...], li_ref[...], ls_ref[...])
        dlr, dli, dls = vjp((dar_ref[...], dai_ref[...], dfr_ref[...], dfi_ref[...]))
        dlr_ref[...] = dlr
        dli_ref[...] = dli
        dls_ref[...] = dls

    o = jax.ShapeDtypeStruct((g, p), F32)
    return pl.pallas_call(body, name="s5_disc_bwd",
                          out_shape=(o, o, jax.ShapeDtypeStruct((g, 1), F32)),
                          compiler_params=_cparams())(lam_re, lam_im, log_step, dar, dai, dfr, dfi)


def _s5_bbar_fwd(f_re, f_im, b_re, b_im):
    n, c = b_re.shape

    def body(fr_ref, fi_ref, br_ref, bi_ref, or_ref, oi_ref):
        fr, fi, br, bi = fr_ref[...], fi_ref[...], br_ref[...], bi_ref[...]
        or_ref[...] = fr * br - fi * bi
        oi_ref[...] = fr * bi + fi * br

    o = jax.ShapeDtypeStruct((n, c), F32)
    return pl.pallas_call(body, name="s5_bbar_fwd", out_shape=(o, o),
                          compiler_params=_cparams())(f_re, f_im, b_re, b_im)


def _s5_bbar_bwd(f_re, f_im, b_re, b_im, dbr, dbi):
    n, c = b_re.shape

    def body(fr_ref, fi_ref, br_ref, bi_ref, dor_ref, doi_ref, dfr_ref, dfi_ref, dbr_ref, dbi_ref):
        fr, fi, br, bi = fr_ref[...], fi_ref[...], br_ref[...], bi_ref[...]
        dor, doi = dor_ref[...], doi_ref[...]
        dfr_ref[...] = jnp.sum(dor * br + doi * bi, axis=-1, keepdims=True)
        dfi_ref[...] = jnp.sum(doi * br - dor * bi, axis=-1, keepdims=True)
        dbr_ref[...] = fr * dor + fi * doi
        dbi_ref[...] = fr * doi - fi * dor

    col = jax.ShapeDtypeStruct((n, 1), F32)
    o = jax.ShapeDtypeStruct((n, c), F32)
    return pl.pallas_call(body, name="s5_bbar_bwd", out_shape=(col, col, o, o),
                          compiler_params=_cparams())(f_re, f_im, b_re, b_im, dbr, dbi)


SUBLANES = 8


def _scan_groups(xr, xi, ar, ai, reverse):
    t, n = xr.shape
    xr, xi = xr.reshape(t // SUBLANES, SUBLANES, n), xi.reshape(t // SUBLANES, SUBLANES, n)
    sub = lax.broadcasted_iota(jnp.int32, (1, SUBLANES, 1), 1)
    pr, pi = ar.reshape(1, 1, n), ai.reshape(1, 1, n)
    for sh in (1, 2, 4):
        keep = (sub < SUBLANES - sh) if reverse else (sub >= sh)
        cr, ci = jnp.where(keep, pr, 0.0), jnp.where(keep, pi, 0.0)
        shift = SUBLANES - sh if reverse else sh
        sr, si = pltpu.roll(xr, shift, 1), pltpu.roll(xi, shift, 1)
        xr, xi = xr + cr * sr - ci * si, xi + cr * si + ci * sr
        pr, pi = pr * pr - pi * pi, 2.0 * pr * pi
    return xr.reshape(t, n), xi.reshape(t, n)


def _scan_rows(xr, xi, ar, ai, cr, ci, sr_ref, si_ref, reverse):
    t, n = xr.shape
    xr, xi = _scan_groups(xr, xi, ar, ai, reverse)
    sr_ref[...] = xr
    si_ref[...] = xi
    sub = lax.broadcasted_iota(jnp.int32, (SUBLANES, n), 0)
    seed = sub == (SUBLANES - 1 if reverse else 0)
    pwr, pwi = _scan_groups(jnp.where(seed, ar, 0.0), jnp.where(seed, ai, 0.0), ar, ai, reverse)
    groups = range(t // SUBLANES)
    edge = 0 if reverse else SUBLANES - 1
    for g in (reversed(groups) if reverse else groups):
        rows = slice(g * SUBLANES, (g + 1) * SUBLANES)
        vr = sr_ref[rows, :] + (pwr * cr - pwi * ci)
        vi = si_ref[rows, :] + (pwr * ci + pwi * cr)
        sr_ref[rows, :] = vr
        si_ref[rows, :] = vi
        cr, ci = vr[edge:edge + 1, :], vi[edge:edge + 1, :]
    return cr, ci


def _s5_states(u_bf, bbr, bbi, ar, ai, cr, ci, sr_ref, si_ref):
    return _scan_rows(_dot(u_bf, bbr), _dot(u_bf, bbi), ar, ai, cr, ci, sr_ref, si_ref, reverse=False)


def _s5_fwd(proj, bbr, bbi, a_re, a_im, ctr, cti, d_skip, d_s5):
    l = proj.shape[0]
    nb, uc, ns = bbr.shape
    t = min(S5_T, l)
    nt = l // t

    def body(u_ref, bbr_ref, bbi_ref, ar_ref, ai_ref, ctr_ref, cti_ref, d_ref,
             y_ref, car_r_ref, car_i_ref, sr_ref, si_ref, cr, ci):
        @pl.when(pl.program_id(1) == 0)
        def _():
            cr[...] = jnp.zeros_like(cr)
            ci[...] = jnp.zeros_like(ci)

        car_r_ref[...] = cr[...]
        car_i_ref[...] = ci[...]
        u = u_ref[...]
        cr[...], ci[...] = _s5_states(u.astype(BF16), bbr_ref[...], bbi_ref[...], ar_ref[...],
                                      ai_ref[...], cr[...], ci[...], sr_ref, si_ref)
        y_ref[...] = (_bdot(sr_ref[...], ctr_ref[...]) - _bdot(si_ref[...], cti_ref[...])
                      + d_ref[...] * u)

    per_block = lambda shape: pl.BlockSpec((None,) + shape, lambda b, n: (b, 0, 0))
    return pl.pallas_call(
        body, name="s5_fwd",
        out_shape=(jax.ShapeDtypeStruct((l, d_s5), F32),
                   jax.ShapeDtypeStruct((nt, 1, nb * ns), F32),
                   jax.ShapeDtypeStruct((nt, 1, nb * ns), F32),
                   jax.ShapeDtypeStruct((l, nb * ns), F32),
                   jax.ShapeDtypeStruct((l, nb * ns), F32)),
        grid=(nb, nt),
        in_specs=[pl.BlockSpec((t, uc), lambda b, n: (n, b)),
                  per_block((uc, ns)), per_block((uc, ns)),
                  per_block((1, ns)), per_block((1, ns)),
                  per_block((ns, uc)), per_block((ns, uc)),
                  pl.BlockSpec((1, uc), lambda b, n: (0, b))],
        out_specs=(pl.BlockSpec((t, uc), lambda b, n: (n, b)),
                   pl.BlockSpec((None, 1, ns), lambda b, n: (n, 0, b)),
                   pl.BlockSpec((None, 1, ns), lambda b, n: (n, 0, b)),
                   pl.BlockSpec((t, ns), lambda b, n: (n, b)),
                   pl.BlockSpec((t, ns), lambda b, n: (n, b))),
        scratch_shapes=[pltpu.VMEM((1, ns), F32), pltpu.VMEM((1, ns), F32)],
        compiler_params=_cparams("parallel", "arbitrary"),
    )(proj, bbr, bbi, a_re, a_im, ctr, cti, d_skip)


def _s5_bwd(proj, dy, bbr, bbi, a_re, a_im, cbr, cbi, d_skip, car_r, car_i, states_r, states_i):
    l, d_s5 = dy.shape
    nb, uc, ns = bbr.shape
    t = min(S5_T, l)
    nt = l // t

    def body(u_ref, dy_ref, bbr_ref, bbi_ref, ar_ref, ai_ref, cbr_ref, cbi_ref, d_ref,
             car_r_ref, car_i_ref, sr_ref, si_ref,
             du_ref, dar_ref, dai_ref, dbbr_ref, dbbi_ref, dcbr_ref, dcbi_ref, dd_ref, gcr, gci,
             gr_ref, gi_ref):
        @pl.when(pl.program_id(1) == 0)
        def _():
            gcr[...] = jnp.zeros_like(gcr)
            gci[...] = jnp.zeros_like(gci)
            for ref in (dar_ref, dai_ref, dbbr_ref, dbbi_ref, dcbr_ref, dcbi_ref, dd_ref):
                ref[...] = jnp.zeros_like(ref)

        row = lax.broadcasted_iota(jnp.int32, (t, 1), 0)
        u, dy = u_ref[...], dy_ref[...]
        u_bf, dy_bf = u.astype(BF16), dy.astype(BF16)
        ar, ai = ar_ref[...], ai_ref[...]
        cr, ci = car_r_ref[...], car_i_ref[...]
        sr, si = sr_ref[...], si_ref[...]
        first = row == 0
        pr = jnp.where(first, cr, pltpu.roll(sr, 1, 0))
        pi = jnp.where(first, ci, pltpu.roll(si, 1, 0))
        gcr[...], gci[...] = _scan_rows(_dot(dy_bf, cbr_ref[...]), -_dot(dy_bf, cbi_ref[...]), ar, -ai,
                                        gcr[...], gci[...], gr_ref, gi_ref, reverse=True)
        gr, gi = gr_ref[...], gi_ref[...]
        dar_ref[...] += jnp.sum(gr * pr + gi * pi, axis=0, keepdims=True)
        dai_ref[...] += jnp.sum(gi * pr - gr * pi, axis=0, keepdims=True)
        gr_bf, gi_bf = gr.astype(BF16), gi.astype(BF16)
        tn = ((0,), (0,))
        dbbr_ref[...] += _dot(u_bf, gr_bf, tn)
        dbbi_ref[...] += _dot(u_bf, gi_bf, tn)
        dcbr_ref[...] += _dot(dy_bf, sr.astype(BF16), tn)
        dcbi_ref[...] -= _dot(dy_bf, si.astype(BF16), tn)
        nt_dims = ((1,), (1,))
        du = _dot(gr_bf, bbr_ref[...], nt_dims) + _dot(gi_bf, bbi_ref[...], nt_dims) + dy * d_ref[...]
        du_ref[...] = du.astype(BF16)
        dd_ref[...] += jnp.sum(dy * u, axis=0, keepdims=True)

    rev = lambda n: nt - 1 - n
    per_block = lambda shape: pl.BlockSpec((None,) + shape, lambda b, n: (b, 0, 0))
    acc = jax.ShapeDtypeStruct((nb, uc, ns), F32)
    vec = jax.ShapeDtypeStruct((nb, 1, ns), F32)
    return pl.pallas_call(
        body, name="s5_bwd",
        out_shape=(jax.ShapeDtypeStruct((l, d_s5), BF16), vec, vec, acc, acc, acc, acc,
                   jax.ShapeDtypeStruct((1, d_s5), F32)),
        grid=(nb, nt),
        in_specs=[pl.BlockSpec((t, uc), lambda b, n: (rev(n), b)),
                  pl.BlockSpec((t, uc), lambda b, n: (rev(n), b)),
                  per_block((uc, ns)), per_block((uc, ns)),
                  per_block((1, ns)), per_block((1, ns)),
                  per_block((uc, ns)), per_block((uc, ns)),
                  pl.BlockSpec((1, uc), lambda b, n: (0, b)),
                  pl.BlockSpec((None, 1, ns), lambda b, n: (rev(n), 0, b)),
                  pl.BlockSpec((None, 1, ns), lambda b, n: (rev(n), 0, b)),
                  pl.BlockSpec((t, ns), lambda b, n: (rev(n), b)),
                  pl.BlockSpec((t, ns), lambda b, n: (rev(n), b))],
        out_specs=(pl.BlockSpec((t, uc), lambda b, n: (rev(n), b)),
                   per_block((1, ns)), per_block((1, ns)),
                   per_block((uc, ns)), per_block((uc, ns)),
                   per_block((uc, ns)), per_block((uc, ns)),
                   pl.BlockSpec((1, uc), lambda b, n: (0, b))),
        scratch_shapes=[pltpu.VMEM((1, ns), F32), pltpu.VMEM((1, ns), F32)]
        + [pltpu.VMEM((t, ns), F32)] * 2,
        compiler_params=_cparams("parallel", "arbitrary"),
    )(proj, dy, bbr, bbi, a_re, a_im, cbr, cbi, d_skip, car_r, car_i, states_r, states_i)


def _s5_glu_fwd(y1, proj, off_z, wglu):
    l, d = y1.shape

    def body(y_ref, z_ref, w_ref, o_ref):
        y2 = _gelu(y_ref[...])
        y3 = y2 * _sigmoid(_bdot(y2, w_ref[...]))
        o_ref[...] = (y3 * _silu(z_ref[...])).astype(BF16)

    return pl.pallas_call(
        body, name="s5_glu_fwd",
        out_shape=jax.ShapeDtypeStruct((l, d), BF16),
        grid=(l // ROW_TILE,),
        in_specs=[pl.BlockSpec((ROW_TILE, d), lambda i: (i, 0)),
                  pl.BlockSpec((ROW_TILE, d), lambda i: (i, off_z // d)),
                  pl.BlockSpec((d, d), lambda i: (0, 0))],
        out_specs=pl.BlockSpec((ROW_TILE, d), lambda i: (i, 0)),
        compiler_params=_cparams("parallel"),
    )(y1, proj, wglu)


def _s5_glu_bwd(y1, proj, off_z, wglu, dout):
    l, d = y1.shape

    def body(y_ref, z_ref, w_ref, do_ref, dy_ref, dz_ref, dw_ref):
        y2, gelu_vjp = jax.vjp(_gelu, y_ref[...])
        z = z_ref[...]
        sz, silu_vjp = jax.vjp(_silu, z)
        y2_bf = y2.astype(BF16)
        sg = _sigmoid(_dot(y2_bf, w_ref[...]))
        dout = do_ref[...]
        dy3 = dout * sz
        dz_ref[...] = silu_vjp(dout * (y2 * sg))[0].astype(BF16)
        dgl = (dy3 * y2 * sg * (1.0 - sg)).astype(BF16)
        dy2 = dy3 * sg + _dot(dgl, w_ref[...], ((1,), (1,)))
        dy_ref[...] = gelu_vjp(dy2)[0]

        @pl.when(pl.program_id(0) == 0)
        def _():
            dw_ref[...] = jnp.zeros_like(dw_ref)

        dw_ref[...] += _dot(y2_bf, dgl, ((0,), (0,)))

    row = pl.BlockSpec((ROW_TILE, d), lambda i: (i, 0))
    full = pl.BlockSpec((d, d), lambda i: (0, 0))
    return pl.pallas_call(
        body, name="s5_glu_bwd",
        out_shape=(jax.ShapeDtypeStruct((l, d), F32), jax.ShapeDtypeStruct((l, d), BF16),
                   jax.ShapeDtypeStruct((d, d), F32)),
        grid=(l // ROW_TILE,),
        in_specs=[row, pl.BlockSpec((ROW_TILE, d), lambda i: (i, off_z // d)), full, row],
        out_specs=(row, row, full),
        compiler_params=_cparams("arbitrary"),
    )(y1, proj, wglu, dout)


def _shift_rows(x, k, back=False):
    if k == 0:
        return x
    t = x.shape[0]
    row = lax.broadcasted_iota(jnp.int32, (t, 1), 0)
    if back:
        return jnp.where(row < t - k, pltpu.roll(x, t - k, 0), 0.0)
    return jnp.where(row >= k, pltpu.roll(x, k, 0), 0.0)


def _dn_conv(x, w_ref):
    return sum(w_ref[CONV_K - 1 - k:CONV_K - k, :] * _shift_rows(x, k) for k in range(CONV_K))


def _dn_post_conv(c, j):
    y = _silu(c)
    n = y * lax.rsqrt(jnp.sum(y * y, axis=-1, keepdims=True) + EPS)
    n = n * jnp.where(j < DN_HEADS, DN_HEAD_DIM ** -0.5, 1.0)
    return jnp.where(j < 2 * DN_HEADS, n, y)


def _dn_prep_fwd(proj, off_qkv, conv_w):
    l = proj.shape[0]
    hd = DN_HEAD_DIM
    nblk = 3 * DN_HEADS

    def body(x_ref, w_ref, o_ref):
        o_ref[...] = _dn_post_conv(_dn_conv(x_ref[...], w_ref), pl.program_id(0))

    return pl.pallas_call(
        body, name="dn_prep_fwd",
        out_shape=jax.ShapeDtypeStruct((l, nblk * hd), F32),
        grid=(nblk,),
        in_specs=[pl.BlockSpec((l, hd), lambda j: (0, off_qkv // hd + j)),
                  pl.BlockSpec((CONV_K, hd), lambda j: (0, j))],
        out_specs=pl.BlockSpec((l, hd), lambda j: (0, j)),
        compiler_params=_cparams("parallel"),
    )(proj, conv_w)


def _dn_prep_bwd(proj, off_qkv, conv_w, dqkv):
    l = proj.shape[0]
    hd = DN_HEAD_DIM
    nblk = 3 * DN_HEADS

    def body(x_ref, w_ref, do_ref, dx_ref, dw_ref):
        x = x_ref[...]
        j = pl.program_id(0)
        _, vjp = jax.vjp(functools.partial(_dn_post_conv, j=j), _dn_conv(x, w_ref))
        dc = vjp(do_ref[...])[0]
        dx = sum(w_ref[CONV_K - 1 - k:CONV_K - k, :] * _shift_rows(dc, k, back=True)
                 for k in range(CONV_K))
        dx_ref[...] = dx.astype(BF16)
        for k in range(CONV_K):
            dw_ref[CONV_K - 1 - k:CONV_K - k, :] = jnp.sum(dc * _shift_rows(x, k), axis=0,
                                                           keepdims=True)

    return pl.pallas_call(
        body, name="dn_prep_bwd",
        out_shape=(jax.ShapeDtypeStruct((l, nblk * hd), BF16),
                   jax.ShapeDtypeStruct((CONV_K, nblk * hd), F32)),
        grid=(nblk,),
        in_specs=[pl.BlockSpec((l, hd), lambda j: (0, off_qkv // hd + j)),
                  pl.BlockSpec((CONV_K, hd), lambda j: (0, j)),
                  pl.BlockSpec((None, l, hd), lambda j: (j // DN_HEADS, 0, j % DN_HEADS))],
        out_specs=(pl.BlockSpec((l, hd), lambda j: (0, j)),
                   pl.BlockSpec((CONV_K, hd), lambda j: (0, j))),
        compiler_params=_cparams("parallel"),
    )(proj, conv_w, dqkv)


def _dn_gate_fn(ba, a_log_row, dt_row):
    lane = lax.broadcasted_iota(jnp.int32, ba.shape, 1)
    beta = _sigmoid(ba)
    g = -jnp.exp(a_log_row) * _softplus(ba + dt_row)
    return jnp.where(lane < DN_HEADS, beta, jnp.where(lane < 2 * DN_HEADS, g, 0.0))


def _dn_gates_fwd(proj, off_ba, a_log_row, dt_row):
    l = proj.shape[0]
    row = pl.BlockSpec((ROW_TILE, 128), lambda i: (i, off_ba // 128))
    vec = pl.BlockSpec((1, 128), lambda i: (0, 0))

    def body(ba_ref, al_ref, dt_ref, o_ref):
        o_ref[...] = _dn_gate_fn(ba_ref[...], al_ref[...], dt_ref[...])

    return pl.pallas_call(
        body, name="dn_gates_fwd",
        out_shape=jax.ShapeDtypeStruct((l, 128), F32),
        grid=(l // ROW_TILE,),
        in_specs=[row, vec, vec],
        out_specs=pl.BlockSpec((ROW_TILE, 128), lambda i: (i, 0)),
        compiler_params=_cparams("parallel"),
    )(proj, a_log_row, dt_row)


def _dn_gates_bwd(proj, off_ba, a_log_row, dt_row, dgb_heads):
    l = proj.shape[0]
    nh = dgb_heads.shape[0]
    row = pl.BlockSpec((ROW_TILE, 128), lambda i: (i, off_ba // 128))
    vec = pl.BlockSpec((1, 128), lambda i: (0, 0))

    def body(ba_ref, al_ref, dt_ref, dg_ref, dba_ref, dal_ref, ddt_ref):
        _, vjp = jax.vjp(_dn_gate_fn, ba_ref[...], al_ref[...], dt_ref[...])
        dgb = dg_ref[0]
        for h in range(1, nh):
            dgb = dgb + dg_ref[h]
        dba, dal, ddt = vjp(dgb)
        dba_ref[...] = dba.astype(BF16)

        @pl.when(pl.program_id(0) == 0)
        def _():
            dal_ref[...] = jnp.zeros_like(dal_ref)
            ddt_ref[...] = jnp.zeros_like(ddt_ref)

        dal_ref[...] += dal
        ddt_ref[...] += ddt

    return pl.pallas_call(
        body, name="dn_gates_bwd",
        out_shape=(jax.ShapeDtypeStruct((l, 128), BF16), jax.ShapeDtypeStruct((1, 128), F32),
                   jax.ShapeDtypeStruct((1, 128), F32)),
        grid=(l // ROW_TILE,),
        in_specs=[row, vec, vec, pl.BlockSpec((nh, ROW_TILE, 128), lambda i: (0, i, 0))],
        out_specs=(pl.BlockSpec((ROW_TILE, 128), lambda i: (i, 0)), vec, vec),
        compiler_params=_cparams("arbitrary"),
    )(proj, a_log_row, dt_row, dgb_heads)


@jax.custom_vjp
def _unit_lower_inverses(a_mats):
    c = a_mats[0].shape[0]
    eye = (lax.broadcasted_iota(jnp.int32, (c, c), 0) == lax.broadcasted_iota(jnp.int32, (c, c), 1)).astype(F32)
    t_inv = [eye - a for a in a_mats]
    power = a_mats
    for _ in range(int(math.log2(c)) - 1):
        power = [_bdot(p, p) for p in power]
        t_inv = [t + _bdot(t, p) for t, p in zip(t_inv, power)]
    return t_inv


def _unit_lower_inverses_fwd(a_mats):
    t_inv = _unit_lower_inverses(a_mats)
    return t_inv, t_inv


def _inverse_cotangents(t_inv, grads):
    right = [_dot3_dims(g, t, ((1,), (1,))) for g, t in zip(grads, t_inv)]
    return [-_dot3_dims(t, r, ((0,), (0,))) for t, r in zip(t_inv, right)]


_unit_lower_inverses.defvjp(_unit_lower_inverses_fwd,
                            lambda t_inv, grads: (_inverse_cotangents(t_inv, grads),))


@jax.custom_vjp
def _kept_inverses(a_mats, t_inv):
    return t_inv


_kept_inverses.defvjp(
    lambda a_mats, t_inv: (t_inv, t_inv),
    lambda t_inv, grads: (_inverse_cotangents(t_inv, grads), [jnp.zeros_like(t) for t in t_inv]))


def _dn_chunk_fn(states, qs, ks, vs, gb, heads, kept_inverses=None, return_inverses=False):
    c = qs[0].shape[0]
    each = lambda f, *lists: [f(*args) for args in zip(*lists)]
    lane = lax.broadcasted_iota(jnp.int32, gb.shape, 1)
    ri = lax.broadcasted_iota(jnp.int32, (c, c), 0)
    ci = lax.broadcasted_iota(jnp.int32, (c, c), 1)
    causal, strict = ri >= ci, ri > ci
    eye = (ri == ci).astype(F32)
    rowi = lax.broadcasted_iota(jnp.int32, (c, 1), 0)
    nt_dims = ((1,), (1,))
    hdot = functools.partial(_dot, precision=HIGHEST)

    pick = lambda m, at: jnp.sum(jnp.where(lane == at, m, 0.0), axis=1, keepdims=True)
    gb_cum = hdot(causal.astype(F32), gb)
    beta = [pick(gb, h) for h in heads]
    gc = [pick(gb_cum, h + DN_HEADS) for h in heads]
    gc_row = each(lambda g: jnp.sum(eye * g, axis=0, keepdims=True), gc)
    decay = each(lambda g, gr: jnp.where(causal, jnp.exp(jnp.where(causal, g - gr, 0.0)), 0.0),
                 gc, gc_row)
    kk = each(lambda k: _bdot(k, k, nt_dims), ks)
    a_mat = each(lambda b, m, dc: jnp.where(strict, b * m * dc, 0.0), beta, kk, decay)

    t_inv = (_unit_lower_inverses(a_mat) if kept_inverses is None
             else _kept_inverses(a_mat, kept_inverses))
    egc = each(jnp.exp, gc)
    u_c = each(lambda t, v, b: _dot3(t, v * b), t_inv, vs, beta)
    w_c = each(lambda t, k, b, e: _dot3(t, k * (b * e)), t_inv, ks, beta, egc)
    qk = each(lambda q, k, dc: _bdot(q, k, nt_dims) * dc, qs, ks, decay)
    g_end = each(lambda g: jnp.sum(jnp.where(rowi == c - 1, g, 0.0), axis=0, keepdims=True), gc)
    v_new = each(lambda u, w, s: u - _bdot(w, s), u_c, w_c, states)
    o = each(lambda q, e, s, m, vn: _bdot(q * e, s) + _bdot(m, vn), qs, egc, states, qk, v_new)
    new_states = each(
        lambda s, ge, k, g, vn: s * jnp.exp(ge) + _bdot(k * jnp.exp(ge - g), vn, ((0,), (0,))),
        states, g_end, ks, gc, v_new)
    return (o, new_states, t_inv) if return_inverses else (o, new_states)


def _dn_chunk_specs(order):
    hd, nh, hps = DN_HEAD_DIM, DN_HEADS, DN_HEADS_PER_STEP
    qkv = lambda part: pl.BlockSpec((CHUNK, hps * hd), lambda h, n: (order(n), part * (nh // hps) + h))
    gb = pl.BlockSpec((CHUNK, 128), lambda h, n: (order(n), 0))
    state = pl.BlockSpec((hps, None, hd, hd), lambda h, n: (h, order(n), 0, 0))
    inverse = pl.BlockSpec((hps, None, CHUNK, CHUNK), lambda h, n: (h, order(n), 0, 0))
    return qkv, gb, state, inverse


def _dn_chunk_fwd(qkv, gb):
    l = qkv.shape[0]
    hd, nh, hps = DN_HEAD_DIM, DN_HEADS, DN_HEADS_PER_STEP
    n_chunks = l // CHUNK
    qkv_spec, gb_spec, state_spec, inverse_spec = _dn_chunk_specs(lambda n: n)

    def body(q_ref, k_ref, v_ref, gb_ref, o_ref, s_ref, t_ref, state):
        @pl.when(pl.program_id(1) == 0)
        def _():
            state[...] = jnp.zeros_like(state)

        cols = [slice(i * hd, (i + 1) * hd) for i in range(hps)]
        states = [state[i] for i in range(hps)]
        for i in range(hps):
            s_ref[i] = states[i]
        o, new_states, t_inv = _dn_chunk_fn(
            states, [q_ref[:, cs] for cs in cols], [k_ref[:, cs] for cs in cols],
            [v_ref[:, cs] for cs in cols], gb_ref[...],
            [pl.program_id(0) * hps + i for i in range(hps)], return_inverses=True)
        for i in range(hps):
            o_ref[:, cols[i]] = o[i]
            t_ref[i] = t_inv[i]
            state[i] = new_states[i]

    return pl.pallas_call(
        body, name="dn_chunk_fwd",
        out_shape=(jax.ShapeDtypeStruct((l, nh * hd), F32),
                   jax.ShapeDtypeStruct((nh, n_chunks, hd, hd), F32),
                   jax.ShapeDtypeStruct((nh, n_chunks, CHUNK, CHUNK), F32)),
        grid=(nh // hps, n_chunks),
        in_specs=[qkv_spec(0), qkv_spec(1), qkv_spec(2), gb_spec],
        out_specs=(pl.BlockSpec((CHUNK, hps * hd), lambda h, n: (n, h)), state_spec, inverse_spec),
        scratch_shapes=[pltpu.VMEM((hps, hd, hd), F32)],
        compiler_params=_cparams("parallel", "arbitrary"),
    )(qkv, qkv, qkv, gb)


def _dn_chunk_bwd(qkv, gb, states, inverses, do):
    l = qkv.shape[0]
    hd, nh, hps = DN_HEAD_DIM, DN_HEADS, DN_HEADS_PER_STEP
    n_chunks = l // CHUNK
    rev = lambda n: n_chunks - 1 - n
    qkv_spec, gb_spec, state_spec, inverse_spec = _dn_chunk_specs(rev)

    def body(q_ref, k_ref, v_ref, gb_ref, s_ref, t_ref, do_ref, dqkv_ref, dgb_ref, dstate):
        @pl.when(pl.program_id(1) == 0)
        def _():
            dstate[...] = jnp.zeros_like(dstate)

        cols = [slice(i * hd, (i + 1) * hd) for i in range(hps)]
        fn = functools.partial(_dn_chunk_fn, heads=[pl.program_id(0) * hps + i for i in range(hps)],
                               kept_inverses=[t_ref[i] for i in range(hps)])
        _, vjp = jax.vjp(fn, [s_ref[i] for i in range(hps)], [q_ref[:, cs] for cs in cols],
                         [k_ref[:, cs] for cs in cols], [v_ref[:, cs] for cs in cols], gb_ref[...])
        ds, dq, dk, dv, dgb = vjp(([do_ref[:, cs] for cs in cols], [dstate[i] for i in range(hps)]))
        for i in range(hps):
            dstate[i] = ds[i]
            dqkv_ref[0, :, cols[i]] = dq[i]
            dqkv_ref[1, :, cols[i]] = dk[i]
            dqkv_ref[2, :, cols[i]] = dv[i]
        dgb_ref[...] = dgb

    head_out = pl.BlockSpec((CHUNK, hps * hd), lambda h, n: (rev(n), h))
    return pl.pallas_call(
        body, name="dn_chunk_bwd",
        out_shape=(jax.ShapeDtypeStruct((3, l, nh * hd), F32),
                   jax.ShapeDtypeStruct((nh // hps, l, 128), F32)),
        grid=(nh // hps, n_chunks),
        in_specs=[qkv_spec(0), qkv_spec(1), qkv_spec(2), gb_spec, state_spec, inverse_spec, head_out],
        out_specs=(pl.BlockSpec((3, CHUNK, hps * hd), lambda h, n: (0, rev(n), h)),
                   pl.BlockSpec((None, CHUNK, 128), lambda h, n: (h, rev(n), 0))),
        scratch_shapes=[pltpu.VMEM((hps, hd, hd), F32)],
        compiler_params=_cparams("parallel", "arbitrary"),
    )(qkv, qkv, qkv, gb, states, inverses, do)


def _dn_out_fn(o, z, w):
    return _rmsnorm(o, w) * _silu(z)


def _dn_out_fwd(o, proj, off_z, w):
    l, d = o.shape
    hd = DN_HEAD_DIM
    tr = min(4 * ROW_TILE, l)
    blk = lambda off: pl.BlockSpec((tr, hd), lambda i, h: (i, off // hd + h))

    def body(o_ref, z_ref, w_ref, out_ref):
        out_ref[...] = _dn_out_fn(o_ref[...], z_ref[...], w_ref[...]).astype(BF16)

    return pl.pallas_call(
        body, name="dn_out_fwd",
        out_shape=jax.ShapeDtypeStruct((l, d), BF16),
        grid=(l // tr, d // hd),
        in_specs=[blk(0), blk(off_z), pl.BlockSpec((1, hd), lambda i, h: (0, 0))],
        out_specs=blk(0),
        compiler_params=_cparams("parallel", "parallel"),
    )(o, proj, w)


def _dn_out_bwd(o, proj, off_z, w, dout):
    l, d = o.shape
    hd = DN_HEAD_DIM
    tr = min(4 * ROW_TILE, l)
    blk = lambda off: pl.BlockSpec((tr, hd), lambda i, h: (i, off // hd + h))
    vec = pl.BlockSpec((1, hd), lambda i, h: (0, 0))

    def body(o_ref, z_ref, w_ref, dout_ref, do_ref, dz_ref, dw_ref):
        _, vjp = jax.vjp(_dn_out_fn, o_ref[...], z_ref[...], w_ref[...])
        do, dz, dw = vjp(dout_ref[...])
        do_ref[...] = do
        dz_ref[...] = dz.astype(BF16)

        @pl.when((pl.program_id(0) == 0) & (pl.program_id(1) == 0))
        def _():
            dw_ref[...] = jnp.zeros_like(dw_ref)

        dw_ref[...] += dw

    return pl.pallas_call(
        body, name="dn_out_bwd",
        out_shape=(jax.ShapeDtypeStruct((l, d), F32), jax.ShapeDtypeStruct((l, d), BF16),
                   jax.ShapeDtypeStruct((1, hd), F32)),
        grid=(l // tr, d // hd),
        in_specs=[blk(0), blk(off_z), vec, blk(0)],
        out_specs=(blk(0), blk(0), vec),
        compiler_params=_cparams("arbitrary", "arbitrary"),
    )(o, proj, w, dout)


def _tile_2d(rows, cols, budget_bytes=1 << 20):
    for tr in (rows, 4096, 2048, 1024, 512, 256, 128, 64, 32, 16):
        if tr <= rows and rows % tr == 0 and tr * cols * 4 <= budget_bytes:
            return tr, cols
    for tc in (2048, 1024, 512, 256, 128):
        if cols % tc == 0 and rows * tc * 4 <= 2 * budget_bytes:
            return rows, tc
    raise ValueError((rows, cols))


def _adamw_update(g, w_ref, m_ref, v_ref, go_ref, d_ref, mo_ref, vo_ref):
    c1 = 1.0 / (1.0 - ADAM_B1 ** ADAM_STEP)
    c2 = 1.0 / (1.0 - ADAM_B2 ** ADAM_STEP)
    m_new = ADAM_B1 * m_ref[...] + (1.0 - ADAM_B1) * g
    v_new = ADAM_B2 * v_ref[...] + (1.0 - ADAM_B2) * (g * g)
    go_ref[...] = g
    mo_ref[...] = m_new
    vo_ref[...] = v_new
    d_ref[...] = -ADAM_LR * ((m_new * c1) / (jnp.sqrt(v_new * c2) + ADAM_EPS) + ADAM_WD * w_ref[...])


def _adamw(w, m, v, gslots, name):
    rows, cols = w.shape
    ns = gslots.shape[0]
    tr, tc = _tile_2d(rows, cols)

    def body(w_ref, m_ref, v_ref, g_ref, go_ref, d_ref, mo_ref, vo_ref):
        g = g_ref[0].astype(F32)
        for s in range(1, ns):
            g = g + g_ref[s].astype(F32)
        _adamw_update(g, w_ref, m_ref, v_ref, go_ref, d_ref, mo_ref, vo_ref)

    blk = pl.BlockSpec((tr, tc), lambda i, j: (i, j))
    o = jax.ShapeDtypeStruct((rows, cols), F32)
    return pl.pallas_call(
        body, name=name, out_shape=(o, o, o, o),
        grid=(rows // tr, cols // tc),
        in_specs=[blk, blk, blk, pl.BlockSpec((ns, tr, tc), lambda i, j: (0, i, j))],
        out_specs=(blk, blk, blk, blk),
        compiler_params=_cparams("parallel", "parallel"),
    )(w, m, v, gslots)


def _slot_sum(gslots, name):
    ns, rows, cols = gslots.shape
    tr, tc = _tile_2d(rows, cols)

    def body(g_ref, o_ref):
        g = g_ref[0]
        for s in range(1, ns):
            g = g + g_ref[s]
        o_ref[...] = g

    return pl.pallas_call(
        body, name=name, out_shape=jax.ShapeDtypeStruct((rows, cols), F32),
        grid=(rows // tr, cols // tc),
        in_specs=[pl.BlockSpec((ns, tr, tc), lambda i, j: (0, i, j))],
        out_specs=pl.BlockSpec((tr, tc), lambda i, j: (i, j)),
        compiler_params=_cparams("parallel", "parallel"),
    )(gslots)


HBM_SPEC = pl.BlockSpec(memory_space=pl.ANY)


def _all_gather(arrs, name, relayed=(), after=None):
    n = len(arrs)
    n_sems = 13
    n_in = n + (after is not None)

    def body(*refs):
        ins, outs = refs[:n], refs[n_in:n_in + n]
        send_sems, recv_sems, local_sems = refs[n_in + n:]
        x, y, c = lax.axis_index("x"), lax.axis_index("y"), lax.axis_index("c")
        me, sibling = (x, y, c), (x, y, 1 - c)
        chips = [(1 - x, y), (x, 1 - y), (1 - x, 1 - y)]
        index = lambda px, py, pc: 4 * px + 2 * py + pc

        def copy(a, k, block, to, src=None, cols=None):
            dst = outs[a].at[index(*block)]
            src = dst if src is None else src
            if cols is not None:
                dst, src = dst.at[:, cols], src.at[:, cols]
            return pltpu.make_async_remote_copy(
                src_ref=src, dst_ref=dst, send_sem=send_sems.at[a, k], recv_sem=recv_sems.at[a, k],
                device_id=to, device_id_type=MESH)

        mine = [pltpu.make_async_copy(ins[a], outs[a].at[index(*me)], local_sems.at[a])
                for a in range(n)]
        for cp in mine:
            cp.start()
        sends = []

        def start(cp):
            cp.start()
            sends.append(cp)

        halves = {a: (pl.ds(0, arrs[a].shape[1] // 2), pl.ds(arrs[a].shape[1] // 2, arrs[a].shape[1] // 2))
                  for a in relayed}
        near_x, near_y, far = [(*chip, c) for chip in chips]
        for a in range(n):
            start(copy(a, 0, me, sibling, src=ins[a]))
            if a in relayed:
                left, right = halves[a]
                for k, to, cols in ((1, near_x, left), (3, near_y, right), (2, near_x, right), (4, near_y, left)):
                    start(copy(a, k, me, to, src=ins[a], cols=cols))
            else:
                for j, chip in enumerate(chips):
                    start(copy(a, 1 + j, me, (*chip, c), src=ins[a]))
        for a in relayed:
            left, right = halves[a]
            for k, block, cols, onward, to_sibling in (
                    (1, near_x, left, (5, near_y), 7), (3, near_y, right, (6, near_x), 10),
                    (2, near_x, right, None, 8), (4, near_y, left, None, 9),
                    (5, far, left, None, 11), (6, far, right, None, 12)):
                copy(a, k, block, me, cols=cols).wait_recv()
                if onward is not None:
                    start(copy(a, onward[0], block, onward[1], cols=cols))
                start(copy(a, to_sibling, block, sibling, cols=cols))
        for j, chip in enumerate(chips):
            for a in range(n):
                if a not in relayed:
                    copy(a, 1 + j, (*chip, c), me).wait_recv()
                    start(copy(a, 4 + j, (*chip, c), sibling))
        for a in range(n):
            copy(a, 0, sibling, me).wait_recv()
            if a in relayed:
                left, right = halves[a]
                for k, chip, cols in ((7, chips[0], left), (8, chips[0], right), (9, chips[1], left),
                                      (10, chips[1], right), (11, chips[2], left), (12, chips[2], right)):
                    copy(a, k, (*chip, 1 - c), me, cols=cols).wait_recv()
            else:
                for j, chip in enumerate(chips):
                    copy(a, 4 + j, (*chip, 1 - c), me).wait_recv()
        for cp in sends:
            cp.wait_send()
        for cp in mine:
            cp.wait()

    return pl.pallas_call(
        body, name=name,
        out_shape=[jax.ShapeDtypeStruct((N_DEV,) + a.shape, a.dtype) for a in arrs],
        in_specs=[HBM_SPEC] * n_in, out_specs=[HBM_SPEC] * n,
        scratch_shapes=[pltpu.SemaphoreType.DMA((n, n_sems)), pltpu.SemaphoreType.DMA((n, n_sems)),
                        pltpu.SemaphoreType.DMA((n,))],
    )(*arrs, *([after] if after is not None else []))


def _sibling_swap(arrs, name):
    n = len(arrs)

    def body(*refs):
        ins, outs = refs[:n], refs[n:2 * n]
        send_sems, recv_sems = refs[2 * n:]
        x, y, c = lax.axis_index("x"), lax.axis_index("y"), lax.axis_index("c")
        copies = [pltpu.make_async_remote_copy(
            src_ref=ins[a].at[:, 1 - c], dst_ref=outs[a],
            send_sem=send_sems.at[a], recv_sem=recv_sems.at[a],
            device_id=(x, y, 1 - c), device_id_type=MESH) for a in range(n)]
        for cp in copies:
            cp.start()
        for cp in copies:
            cp.wait()

    return pl.pallas_call(
        body, name=name,
        out_shape=[jax.ShapeDtypeStruct(a.shape[:1] + a.shape[2:], a.dtype) for a in arrs],
        in_specs=[HBM_SPEC] * n, out_specs=[HBM_SPEC] * n,
        scratch_shapes=[pltpu.SemaphoreType.DMA((n,)), pltpu.SemaphoreType.DMA((n,))],
    )(*arrs)


def _pair_sum(mine, theirs, core, name):
    chips, _, rows, cols = mine.shape
    tr, tc = _tile_2d(rows, cols, budget_bytes=2 << 20)

    def body(core_ref, a_ref, b_ref, o_ref):
        o_ref[...] = (a_ref[...].astype(F32) + b_ref[...].astype(F32)).astype(o_ref.dtype)

    slab = pl.BlockSpec((None, tr, tc), lambda ch, i, j, core_ref: (ch, i, j))
    return pl.pallas_call(
        body, name=name, out_shape=jax.ShapeDtypeStruct((chips, rows, cols), mine.dtype),
        grid_spec=pltpu.PrefetchScalarGridSpec(
            num_scalar_prefetch=1, grid=(chips, rows // tr, cols // tc),
            in_specs=[pl.BlockSpec((None, None, tr, tc),
                                   lambda ch, i, j, core_ref: (ch, core_ref[0], i, j)), slab],
            out_specs=slab),
        compiler_params=_cparams("parallel", "parallel", "parallel"),
    )(core, mine, theirs)


HBM_ONLY = pl.BlockSpec(memory_space=pltpu.HBM)
SEM_SPEC = pl.BlockSpec(memory_space=pltpu.SEMAPHORE)
SPLIT_COPY_EFFECT = pltpu.SideEffectType.DATAFLOW_SIDE_EFFECTING


def _flip(v, bit):
    return 1 - v if bit else v


def _chip_slices_plan(n):
    def plan():
        x, y, c = lax.axis_index("x"), lax.axis_index("y"), lax.axis_index("c")
        copies = []
        for k in range(1, 4):
            px, py = _flip(x, k & 2), _flip(y, k & 1)
            copies += [(a, 2 * px + py, 2 * x + y, (px, py, c)) for a in range(n)]
        return copies
    return plan, 3 * n


def _gather_plan(n):
    def plan():
        x, y, c = lax.axis_index("x"), lax.axis_index("y"), lax.axis_index("c")
        copies = []
        for k in range(1, N_DEV):
            peer = (_flip(x, k & 4), _flip(y, k & 2), _flip(c, k & 1))
            copies += [(a, None, 4 * x + 2 * y + c, peer) for a in range(n)]
        return copies + [(a, None, 4 * x + 2 * y + c, None) for a in range(n)]
    return plan, 8 * n


def _planned_copies(plan, srcs, lands, send_sems, recv_sems):
    copies = []
    for i, (a, src_at, land_at, peer) in enumerate(plan()):
        src, dst = srcs[a] if src_at is None else srcs[a].at[src_at], lands[a].at[land_at]
        if peer is None:
            local = pltpu.make_async_copy(src, dst, send_sems[i])
            copies.append((local, local.wait))
        else:
            remote = pltpu.make_async_remote_copy(src_ref=src, dst_ref=dst, send_sem=send_sems[i],
                                                  recv_sem=recv_sems[i], device_id=peer, device_id_type=MESH)
            copies.append((remote, remote.wait))
    return copies


def _split_exchange_start(plan_and_count, arrs, land_shapes, name, after=None):
    plan, n_sems = plan_and_count
    n = len(arrs)

    n_in = 2 * n + (after is not None)

    def body(*refs):
        srcs, lands = refs[:n], refs[n:2 * n]
        send_sems, recv_sems = refs[n_in:n_in + n_sems], refs[n_in + n_sems:n_in + 2 * n_sems]
        token = refs[-1]
        for copy, _ in _planned_copies(plan, srcs, lands, send_sems, recv_sems):
            copy.start()
        token[...] = jnp.zeros_like(token)

    hbm = lambda a: pltpu.HBM(a.shape, a.dtype)
    operands = [pltpu.with_memory_space_constraint(a, pltpu.HBM) for a in arrs]
    operands += [pltpu.with_memory_space_constraint(lax.empty(shape, a.dtype), pltpu.HBM)
                 for a, shape in zip(arrs, land_shapes)]
    out = pl.pallas_call(
        body, name=name,
        out_shape=(*[pltpu.SemaphoreType.DMA(())] * (2 * n_sems),
                   *[hbm(a) for a in operands],
                   jax.ShapeDtypeStruct((8, 128), F32)),
        in_specs=[HBM_ONLY] * (2 * n) + [pl.BlockSpec(memory_space=pl.ANY)] * (after is not None),
        out_specs=(*[SEM_SPEC] * (2 * n_sems), *[HBM_ONLY] * (2 * n),
                   pl.BlockSpec(memory_space=pltpu.VMEM)),
        input_output_aliases={i: 2 * n_sems + i for i in range(2 * n)},
        compiler_params=pltpu.CompilerParams(has_side_effects=SPLIT_COPY_EFFECT),
    )(*operands, *([after] if after is not None else []))
    sems, rest = list(out[:2 * n_sems]), out[2 * n_sems:]
    return sems, list(rest[:n]), list(rest[n:2 * n]), rest[-1]


def _split_exchange_wait(plan_and_count, sems, srcs, lands, after, name):
    plan, n_sems = plan_and_count
    n = len(srcs)

    def body(*refs):
        src_refs, land_refs = refs[:n], refs[n:2 * n]
        send_sems, recv_sems = refs[2 * n:2 * n + n_sems], refs[2 * n + n_sems:2 * n + 2 * n_sems]
        for _, wait in _planned_copies(plan, src_refs, land_refs, send_sems, recv_sems):
            wait()

    hbm = lambda a: pltpu.HBM(a.shape, a.dtype)
    out = pl.pallas_call(
        body, name=name,
        out_shape=(*[hbm(a) for a in srcs], *[hbm(a) for a in lands]),
        in_specs=[HBM_ONLY] * (2 * n) + [SEM_SPEC] * (2 * n_sems) + [pl.BlockSpec(memory_space=pl.ANY)],
        out_specs=tuple([HBM_ONLY] * (2 * n)),
        input_output_aliases={i: i for i in range(2 * n)},
        compiler_params=pltpu.CompilerParams(has_side_effects=SPLIT_COPY_EFFECT),
    )(*srcs, *lands, *sems, after)
    return list(out[n:])


def _adamw_exchanged(w, m, v, own, landed, chip, name):
    rows, cols = w.shape
    tr, tc = _tile_2d(rows, cols)

    def body(chip_ref, w_ref, m_ref, v_ref, own_ref, l1_ref, l2_ref, l3_ref, go_ref, d_ref, mo_ref, vo_ref):
        g = own_ref[...].astype(F32)
        for ref in (l1_ref, l2_ref, l3_ref):
            g = g + ref[...].astype(F32)
        _adamw_update(g, w_ref, m_ref, v_ref, go_ref, d_ref, mo_ref, vo_ref)

    blk = pl.BlockSpec((tr, tc), lambda i, j, chip_ref: (i, j))
    slot = lambda k: pl.BlockSpec((None, tr, tc), lambda i, j, chip_ref: (chip_ref[0] ^ k, i, j))
    o = jax.ShapeDtypeStruct((rows, cols), F32)
    return pl.pallas_call(
        body, name=name, out_shape=(o, o, o, o),
        grid_spec=pltpu.PrefetchScalarGridSpec(
            num_scalar_prefetch=1, grid=(rows // tr, cols // tc),
            in_specs=[blk, blk, blk, slot(0), slot(1), slot(2), slot(3)],
            out_specs=(blk, blk, blk, blk)),
        compiler_params=_cparams("parallel", "parallel"),
    )(chip, w, m, v, own, landed, landed, landed)


def _block_diag(t):
    nb, gpb, r, c = t.shape
    eye = jnp.eye(gpb, dtype=t.dtype)
    return jnp.einsum("ngrc,gh->ngrhc", t, eye).reshape(nb, gpb * r, gpb * c)


def _diag_blocks(t, r, c):
    nb = t.shape[0]
    gpb = t.shape[1] // r
    t = t.reshape(nb, gpb, r, gpb, c)
    return jnp.einsum("ngrhc,gh->ngrc", t, jnp.eye(gpb, dtype=t.dtype))


def _pack_rows(parts):
    flat = jnp.concatenate([p.reshape(-1).astype(F32) for p in parts])
    pad = (-flat.shape[0]) % (256 * 128)
    return jnp.pad(flat, (0, pad)).reshape(-1, 128)


def _unpack_rows(packed, shapes):
    flat = packed.reshape(-1)
    out, at = [], 0
    for shape in shapes:
        size = math.prod(shape)
        out.append(flat[at:at + size].reshape(shape))
        at += size
    return out


def kernel(x, ln_w, w_in, s5_lam_re, s5_lam_im, s5_log_step, s5_b_re, s5_b_im, s5_c_re, s5_c_im, s5_d, s5_w_glu, s5_w_up, dn_conv_w, dn_a_log, dn_dt_bias, dn_norm_w, dn_w_up, w_out, final_norm_w, loss_target, m_ln_w, m_w_in, m_s5_lam_re, m_s5_lam_im, m_s5_log_step, m_s5_b_re, m_s5_b_im, m_s5_c_re, m_s5_c_im, m_s5_d, m_s5_w_glu, m_s5_w_up, m_dn_conv_w, m_dn_a_log, m_dn_dt_bias, m_dn_norm_w, m_dn_w_up, m_w_out, m_final_norm_w, v_ln_w, v_w_in, v_s5_lam_re, v_s5_lam_im, v_s5_log_step, v_s5_b_re, v_s5_b_im, v_s5_c_re, v_s5_c_im, v_s5_d, v_s5_w_glu, v_s5_w_up, v_dn_conv_w, v_dn_a_log, v_dn_dt_bias, v_dn_norm_w, v_dn_w_up, v_w_out, v_final_norm_w):
    weights = dict(ln_w=ln_w, w_in=w_in, s5_lam_re=s5_lam_re, s5_lam_im=s5_lam_im,
                   s5_log_step=s5_log_step, s5_b_re=s5_b_re, s5_b_im=s5_b_im, s5_c_re=s5_c_re,
                   s5_c_im=s5_c_im, s5_d=s5_d, s5_w_glu=s5_w_glu, s5_w_up=s5_w_up,
                   dn_conv_w=dn_conv_w, dn_a_log=dn_a_log, dn_dt_bias=dn_dt_bias,
                   dn_norm_w=dn_norm_w, dn_w_up=dn_w_up, w_out=w_out, final_norm_w=final_norm_w)
    mom_m = dict(ln_w=m_ln_w, w_in=m_w_in, s5_lam_re=m_s5_lam_re, s5_lam_im=m_s5_lam_im,
                 s5_log_step=m_s5_log_step, s5_b_re=m_s5_b_re, s5_b_im=m_s5_b_im,
                 s5_c_re=m_s5_c_re, s5_c_im=m_s5_c_im, s5_d=m_s5_d, s5_w_glu=m_s5_w_glu,
                 s5_w_up=m_s5_w_up, dn_conv_w=m_dn_conv_w, dn_a_log=m_dn_a_log,
                 dn_dt_bias=m_dn_dt_bias, dn_norm_w=m_dn_norm_w, dn_w_up=m_dn_w_up,
                 w_out=m_w_out, final_norm_w=m_final_norm_w)
    mom_v = dict(ln_w=v_ln_w, w_in=v_w_in, s5_lam_re=v_s5_lam_re, s5_lam_im=v_s5_lam_im,
                 s5_log_step=v_s5_log_step, s5_b_re=v_s5_b_re, s5_b_im=v_s5_b_im,
                 s5_c_re=v_s5_c_re, s5_c_im=v_s5_c_im, s5_d=v_s5_d, s5_w_glu=v_s5_w_glu,
                 s5_w_up=v_s5_w_up, dn_conv_w=v_dn_conv_w, dn_a_log=v_dn_a_log,
                 dn_dt_bias=v_dn_dt_bias, dn_norm_w=v_dn_norm_w, dn_w_up=v_dn_w_up,
                 w_out=v_w_out, final_norm_w=v_final_norm_w)
    names = list(weights)

    l, d = x.shape[1], x.shape[2]
    d_s5 = d // 2
    groups = d_s5 // S5_GROUP
    nb = groups // S5_GPB
    d_dn = DN_HEADS * DN_HEAD_DIM
    w_in_cols = w_in.shape[2]
    d_in = N_DEV * w_in_cols
    off_ba_src = 2 * d_s5 + 4 * d_dn
    off_u, off_zs, off_qkv, off_zd = 0, d_s5, 2 * d_s5, 2 * d_s5 + 3 * d_dn
    off_ba = off_zd + d_dn
    n_main = off_ba + BA_PAD
    off_gs, off_gd = 0, d
    x2d, tgt2d = x[0], loss_target[0]
    my_index = 4 * lax.axis_index("x") + 2 * lax.axis_index("y") + lax.axis_index("c")

    g_win, g_conv = _all_gather([jnp.transpose(w_in[0]).astype(BF16), dn_conv_w[0]], name="gather_weights",
                                relayed=(0,))
    late_plan = _gather_plan(4)
    late_shards = [s5_w_glu[0].astype(BF16), s5_w_up[0].astype(BF16), dn_w_up[0].astype(BF16),
                   w_out[0].astype(BF16)]
    late_sems, late_shards, late_lands, late_token = _split_exchange_start(
        late_plan, late_shards, [(N_DEV,) + s.shape for s in late_shards], name="gather_late_start",
        after=g_conv)
    ba_end = off_ba_src + 2 * DN_HEADS
    w_full_t = g_win.reshape(d_in, d)
    w_gates_t = w_full_t[ba_end:]
    conv_full = jnp.transpose(g_conv, (1, 0, 2)).reshape(CONV_K, 3 * d_dn)

    lam_re, lam_im = s5_lam_re[0], s5_lam_im[0]
    log_step = s5_log_step[0].reshape(groups, 1)
    b_re = s5_b_re[0].reshape(groups * S5_STATE, S5_GROUP)
    b_im = s5_b_im[0].reshape(groups * S5_STATE, S5_GROUP)
    abar_re, abar_im, f_re, f_im = _s5_disc_fwd(lam_re, lam_im, log_step)
    f_re_col, f_im_col = f_re.reshape(-1, 1), f_im.reshape(-1, 1)
    bb_re, bb_im = _s5_bbar_fwd(f_re_col, f_im_col, b_re, b_im)

    def bb_blocks(t):
        t = t.reshape(nb, S5_GPB, S5_STATE, S5_GROUP).transpose(0, 1, 3, 2)
        return _block_diag(t).astype(BF16)

    def c_blocks(t):
        return _block_diag(t.reshape(nb, S5_GPB, S5_GROUP, S5_STATE)).astype(BF16)

    bbr, bbi = bb_blocks(bb_re), bb_blocks(bb_im)
    cbr, cbi = c_blocks(s5_c_re[0]), c_blocks(s5_c_im[0])
    ctr, cti = jnp.transpose(cbr, (0, 2, 1)), jnp.transpose(cbi, (0, 2, 1))
    a_re = abar_re.reshape(nb, 1, S5_GPB * S5_STATE)
    a_im = abar_im.reshape(nb, 1, S5_GPB * S5_STATE)

    h = _rms_fwd(x2d, ln_w)
    proj = _mm(h, w_full_t, tb=True, b_rows=n_main, tm=1024, tn=512, after=late_token, name="proj")
    proj_gates = _mm(h, w_gates_t, tb=True, tm=1024, tn=1024, name="proj_gates")
    y1, car_r, car_i, states_r, states_i = _s5_fwd(proj, bbr, bbi, a_re, a_im, ctr, cti, s5_d, d_s5)
    a_log_row = jnp.pad(dn_a_log, ((0, 0), (DN_HEADS, 128 - 2 * DN_HEADS)))
    dt_row = jnp.pad(dn_dt_bias, ((0, 0), (DN_HEADS, 128 - 2 * DN_HEADS)))
    qkv = _dn_prep_fwd(proj, off_qkv, conv_full)
    gb = _dn_gates_fwd(proj, off_ba, a_log_row, dt_row)
    o_dn, states, inverses = _dn_chunk_fwd(qkv, gb)

    g_glu, g_sup, g_dup, g_wout = _split_exchange_wait(late_plan, late_sems, late_shards, late_lands, o_dn,
                                                       name="gather_late_wait")
    wglu_full = g_glu.reshape(d_s5, d_s5)
    wsup_full = jnp.transpose(g_sup, (1, 0, 2)).reshape(d_s5, d)
    wdup_full = jnp.transpose(g_dup, (1, 0, 2)).reshape(d_dn, d)
    wout_full = g_wout.reshape(d, d)

    out_s = _s5_glu_fwd(y1, proj, off_zs, wglu_full)
    y_s = _mm(out_s, wsup_full, name="s5_up")
    out_d = _dn_out_fwd(o_dn, proj, off_zd, dn_norm_w)
    y_d = _mm(out_d, wdup_full, name="dn_up")

    mixed = _merge_fwd(proj_gates, off_gs, off_gd, y_s, y_d)
    branch = _mm(mixed, wout_full, name="w_out")
    dx2, dx2_bf, loss_dev, d_final_w = _final(x2d, branch, final_norm_w.reshape(1, d), tgt2d)

    g_wout_full = _mm(mixed, dx2_bf, ta=True, out_dtype=BF16, name="grad_w_out")
    dmixed = _mm(dx2_bf, wout_full, tb=True, name="d_mixed")
    dgs, dgd, dys, dyd = _merge_bwd(proj_gates, off_gs, off_gd, y_s, y_d, dmixed)

    g_dup_full = _mm(out_d, dyd, ta=True, out_dtype=BF16, name="grad_dn_up")
    dout_d = _mm(dyd, wdup_full, tb=True, name="d_out_d")
    do_dn, dzd, d_norm_w = _dn_out_bwd(o_dn, proj, off_zd, dn_norm_w, dout_d)
    dqkv, dgb_heads = _dn_chunk_bwd(qkv, gb, states, inverses, do_dn)
    dba, d_a_log_row, d_dt_row = _dn_gates_bwd(proj, off_ba, a_log_row, dt_row, dgb_heads)
    dqkv_pre, d_conv_full = _dn_prep_bwd(proj, off_qkv, conv_full, dqkv)

    g_sup_full = _mm(out_s, dys, ta=True, out_dtype=BF16, name="grad_s5_up")
    dout_s = _mm(dys, wsup_full, tb=True, name="d_out_s")
    dy1, dzs, g_glu_full = _s5_glu_bwd(y1, proj, off_zs, wglu_full, dout_s)

    def by_dest(t, axis=0):
        if axis == 1:
            return t.reshape(t.shape[0], 4, 2, t.shape[1] // N_DEV).transpose(1, 2, 0, 3)
        return t.reshape(4, 2, t.shape[0] // N_DEV, t.shape[1])

    core = lax.axis_index("c").astype(jnp.int32).reshape(1)
    chip = (2 * lax.axis_index("x") + lax.axis_index("y")).astype(jnp.int32).reshape(1)

    def chip_sums_of(which, parts, tag):
        from_sibling = _sibling_swap(parts, name="swap_grads_" + tag)
        return [_pair_sum(p, got, core, name="pair_sum_" + nm)
                for nm, p, got in zip(which, parts, from_sibling)]

    early = ["s5_w_glu", "s5_w_up", "dn_w_up", "w_out"]
    sums_a = chip_sums_of(early, [by_dest(g_glu_full.astype(BF16)), by_dest(g_sup_full, 1),
                                  by_dest(g_dup_full, 1), by_dest(g_wout_full)], "a")
    plan_a = _chip_slices_plan(len(sums_a))
    sems_a, src_a, land_a, token_a = _split_exchange_start(
        plan_a, sums_a, [t.shape for t in sums_a], name="exchange_start_a")

    (du, d_a_re, d_a_im, d_bbr, d_bbi, d_cbr, d_cbi, d_s5_d) = _s5_bwd(
        proj, dy1, bbr, bbi, a_re, a_im, cbr, cbi, s5_d + token_a[:1, :1], car_r, car_i,
        states_r, states_i)

    def from_bb_blocks(t):
        t = _diag_blocks(t, S5_GROUP, S5_STATE).transpose(0, 1, 3, 2)
        return t.reshape(groups * S5_STATE, S5_GROUP)

    d_f_re, d_f_im, d_b_re, d_b_im = _s5_bbar_bwd(f_re_col, f_im_col, b_re, b_im,
                                                 from_bb_blocks(d_bbr), from_bb_blocks(d_bbi))
    d_lam_re, d_lam_im, d_log_step = _s5_disc_bwd(
        lam_re, lam_im, log_step, d_a_re.reshape(groups, S5_STATE), d_a_im.reshape(groups, S5_STATE),
        d_f_re.reshape(groups, S5_STATE), d_f_im.reshape(groups, S5_STATE))
    d_c_re = _diag_blocks(d_cbr, S5_GROUP, S5_STATE).reshape(groups, S5_GROUP, S5_STATE)
    d_c_im = _diag_blocks(d_cbi, S5_GROUP, S5_STATE).reshape(groups, S5_GROUP, S5_STATE)

    dproj = jnp.concatenate([du, dzs, dqkv_pre, dzd, jnp.pad(dba, ((0, 0), (0, BA_PAD - 128)))], axis=1)
    dproj_gates = jnp.concatenate([dgs, dgd], axis=1)
    g_main_t = _mm(dproj, h, ta=True, out_dtype=BF16, tm=512, tn=d, name="grad_w_in")
    g_gates_t = _mm(dproj_gates, h, ta=True, out_dtype=BF16, tm=512, tn=d, name="grad_w_in_gates")
    g_win_full_t = jnp.concatenate([g_main_t[:ba_end], g_gates_t], axis=0)
    sums_b = chip_sums_of(["w_in"], [by_dest(g_win_full_t)], "b")
    plan_b = _chip_slices_plan(1)
    sems_b, src_b, land_b, token_b = _split_exchange_start(
        plan_b, sums_b, [t.shape for t in sums_b], name="exchange_start_b")
    dh_main = _mm(dproj, w_full_t, b_rows=n_main, tm=1024, tn=1024, tk=n_main // 4, after=token_b,
                  name="d_h_main")
    dh = _mm(dproj_gates, w_gates_t, tm=1024, tn=1024, tk=2048, addend=dh_main, name="d_h")
    grad_x, d_ln_w = _rms_bwd(x2d, ln_w, dh, dx2)
    big = ["w_in"] + early
    results = {}

    land_a = _split_exchange_wait(plan_a, sems_a, src_a, land_a, grad_x, name="exchange_wait_a")
    for nm, own, landed in zip(early, src_a, land_a):
        results[nm] = _adamw_exchanged(weights[nm][0], mom_m[nm][0], mom_v[nm][0], own, landed, chip,
                                       name="adamw_" + nm)

    small = [nm for nm in names if nm not in big]
    small_grads = dict(
        ln_w=d_ln_w, s5_lam_re=d_lam_re, s5_lam_im=d_lam_im, s5_log_step=d_log_step,
        s5_b_re=d_b_re, s5_b_im=d_b_im, s5_c_re=d_c_re, s5_c_im=d_c_im, s5_d=d_s5_d,
        dn_conv_w=d_conv_full, dn_a_log=d_a_log_row[:, DN_HEADS:2 * DN_HEADS],
        dn_dt_bias=d_dt_row[:, DN_HEADS:2 * DN_HEADS], dn_norm_w=d_norm_w, final_norm_w=d_final_w)
    (all_small,) = _all_gather([_pack_rows([small_grads[nm] for nm in small])], name="gather_small_grads",
                               after=results[early[-1]][0])
    summed = _slot_sum(all_small, name="sum_small_grads")
    full_shapes = [(CONV_K, 3 * d_dn) if nm == "dn_conv_w" else weights[nm].shape for nm in small]
    g_small = dict(zip(small, _unpack_rows(summed, full_shapes)))
    conv_cols = dn_conv_w.shape[2]
    g_small["dn_conv_w"] = lax.dynamic_slice_in_dim(
        g_small["dn_conv_w"], my_index * conv_cols, conv_cols, axis=1).reshape(dn_conv_w.shape)
    packed = [_pack_rows([t[nm] for nm in small]) for t in (weights, mom_m, mom_v, g_small)]
    small_out = _adamw(packed[0], packed[1], packed[2], packed[3][None], name="adamw_small")
    small_shapes = [weights[nm].shape for nm in small]
    for kind, packed_out in enumerate(small_out):
        for nm, val in zip(small, _unpack_rows(packed_out, small_shapes)):
            results.setdefault(nm, [None] * 4)[kind] = val

    (land_b,) = _split_exchange_wait(plan_b, sems_b, src_b, land_b, small_out[0], name="exchange_wait_b")
    res = _adamw_exchanged(jnp.transpose(w_in[0]), jnp.transpose(m_w_in[0]), jnp.transpose(v_w_in[0]),
                           src_b[0], land_b, chip, name="adamw_w_in")
    results["w_in"] = [jnp.transpose(t) for t in res]

    loss = lax.psum(loss_dev[0, 0], ("x", "y", "c"))
    outs = [loss, grad_x[None]]
    for kind in range(4):
        outs += [results[nm][kind].reshape(weights[nm].shape) for nm in names]
    return tuple(outs)
```

```python
import functools
import math

import jax
import jax.numpy as jnp
from jax import lax
from jax.experimental import pallas as pl
from jax.experimental.pallas import tpu as pltpu

F32 = jnp.float32
BF16 = jnp.bfloat16
HIGHEST = lax.Precision.HIGHEST
MESH = pl.DeviceIdType.MESH
N_DEV = 8

EPS = 1e-6
S5_GROUP = 16
S5_STATE = 64
S5_GPB = 8
S5_T = 1024
DN_HEADS = 8
DN_HEAD_DIM = 128
CHUNK = 64
DN_HEADS_PER_STEP = 8
CONV_K = 4
BA_PAD = 512

ADAM_LR = 0.001
ADAM_B1 = 0.9
ADAM_B2 = 0.999
ADAM_EPS = 1e-08
ADAM_WD = 0.01
ADAM_STEP = 10

VMEM_LIMIT_BYTES = 48 * 1024 * 1024
ROW_TILE = 256


def _cparams(*sem):
    return pltpu.CompilerParams(dimension_semantics=sem if sem else None,
                                vmem_limit_bytes=VMEM_LIMIT_BYTES)


@jax.custom_jvp
def _sigmoid(x):
    return 1.0 / (1.0 + jnp.exp(-x))


@_sigmoid.defjvp
def _sigmoid_jvp(primals, tangents):
    s = _sigmoid(primals[0])
    return s, tangents[0] * (s * (1.0 - s))


def _silu(x):
    return x * _sigmoid(x)


def _gelu(x):
    return 0.5 * x * (1.0 + jnp.tanh(0.7978845608028654 * (x + 0.044715 * x * x * x)))


def _softplus(x):
    return jnp.maximum(x, 0.0) + jnp.log(1.0 + jnp.exp(-jnp.abs(x)))


def _rmsnorm(x, w):
    return x * lax.rsqrt(jnp.mean(x * x, axis=-1, keepdims=True) + EPS) * w


def _dot(a, b, dims=((1,), (0,)), precision=None):
    return lax.dot_general(a, b, (dims, ((), ())), precision=precision,
                           preferred_element_type=F32)


def _bdot(a, b, dims=((1,), (0,))):
    return _dot(a.astype(BF16), b.astype(BF16), dims)


def _split_bf16(a):
    hi = a.astype(BF16)
    return hi, (a - hi.astype(F32)).astype(BF16)


def _dot3_dims(a, b, dims):
    ah, al = _split_bf16(a)
    bh, bl = _split_bf16(b)
    return _dot(ah, bh, dims) + (_dot(ah, bl, dims) + _dot(al, bh, dims))


@jax.custom_vjp
def _dot3(a, b):
    return _dot3_dims(a, b, ((1,), (0,)))


def _dot3_fwd(a, b):
    return _dot3(a, b), (a, b)


def _dot3_bwd(res, g):
    a, b = res
    return _dot3_dims(g, b, ((1,), (1,))), _dot3_dims(a, g, ((0,), (0,)))


_dot3.defvjp(_dot3_fwd, _dot3_bwd)


def _mm(a, b, *, ta=False, tb=False, out_dtype=F32, tm=512, tn=512, tk=None, after=None, b_rows=None,
        addend=None, name):
    k_dim, m_dim = (a.shape if ta else a.shape[::-1])
    b_rows = b.shape[0] if b_rows is None else b_rows
    n_dim = b_rows if tb else b.shape[1]
    assert (b.shape[1] if tb else b_rows) == k_dim and b_rows <= b.shape[0]
    tm, tn = min(tm, m_dim), min(tn, n_dim)
    tk = k_dim if tk is None else tk
    assert m_dim % tm == 0 and n_dim % tn == 0 and k_dim % tk == 0
    nk = k_dim // tk
    a_spec = (pl.BlockSpec((tk, tm), lambda i, j, k: (k, i)) if ta
              else pl.BlockSpec((tm, tk), lambda i, j, k: (i, k)))
    b_spec = (pl.BlockSpec((tn, tk), lambda i, j, k: (j, k)) if tb
              else pl.BlockSpec((tk, tn), lambda i, j, k: (k, j)))
    dims = ((0 if ta else 1,), (1 if tb else 0,))

    extras = ([after] if after is not None else []) + ([addend] if addend is not None else [])
    extra_specs = ([pl.BlockSpec((8, 128), lambda i, j, k: (0, 0))] if after is not None else []) + (
        [pl.BlockSpec((tm, tn), lambda i, j, k: (i, j))] if addend is not None else [])

    def body(a_ref, b_ref, *rest):
        o_ref, *scratch = rest[len(extras):]
        p = _bdot(a_ref[...], b_ref[...], dims)
        finish = (lambda v: v + rest[len(extras) - 1][...]) if addend is not None else (lambda v: v)
        if nk == 1:
            o_ref[...] = finish(p).astype(o_ref.dtype)
        else:
            acc = scratch[0]
            k = pl.program_id(2)

            @pl.when(k == 0)
            def _():
                acc[...] = p

            @pl.when(k > 0)
            def _():
                acc[...] += p

            @pl.when(k == nk - 1)
            def _():
                o_ref[...] = finish(acc[...]).astype(o_ref.dtype)

    return pl.pallas_call(
        body, name=name,
        out_shape=jax.ShapeDtypeStruct((m_dim, n_dim), out_dtype),
        grid=(m_dim // tm, n_dim // tn, nk),
        in_specs=[a_spec, b_spec] + extra_specs,
        out_specs=pl.BlockSpec((tm, tn), lambda i, j, k: (i, j)),
        scratch_shapes=[pltpu.VMEM((tm, tn), F32)] if nk > 1 else [],
        compiler_params=_cparams("parallel", "parallel", "arbitrary"),
    )(a, b, *extras)


def _rms_fwd(x, w):
    l, d = x.shape

    def body(x_ref, w_ref, h_ref):
        h_ref[...] = _rmsnorm(x_ref[...], w_ref[...]).astype(BF16)

    return pl.pallas_call(
        body, name="rms_fwd",
        out_shape=jax.ShapeDtypeStruct((l, d), BF16),
        grid=(l // ROW_TILE,),
        in_specs=[pl.BlockSpec((ROW_TILE, d), lambda i: (i, 0)),
                  pl.BlockSpec((1, d), lambda i: (0, 0))],
        out_specs=pl.BlockSpec((ROW_TILE, d), lambda i: (i, 0)),
        compiler_params=_cparams("parallel"),
    )(x, w)


def _rms_bwd(x, w, dh, dres):
    l, d = x.shape

    def body(x_ref, w_ref, dh_ref, dres_ref, dx_ref, dw_ref):
        _, vjp = jax.vjp(_rmsnorm, x_ref[...], w_ref[...])
        dx, dw = vjp(dh_ref[...])
        dx_ref[...] = dx + dres_ref[...]

        @pl.when(pl.program_id(0) == 0)
        def _():
            dw_ref[...] = jnp.zeros_like(dw_ref)

        dw_ref[...] += dw

    row = pl.BlockSpec((ROW_TILE, d), lambda i: (i, 0))
    vec = pl.BlockSpec((1, d), lambda i: (0, 0))
    return pl.pallas_call(
        body, name="rms_bwd",
        out_shape=(jax.ShapeDtypeStruct((l, d), F32), jax.ShapeDtypeStruct((1, d), F32)),
        grid=(l // ROW_TILE,),
        in_specs=[row, vec, row, row],
        out_specs=(row, vec),
        compiler_params=_cparams("arbitrary"),
    )(x, w, dh, dres)


def _final(x, r, fw, target):
    l, d = x.shape

    def per_row_loss(x2, w, tgt):
        err = _rmsnorm(x2, w) - tgt
        return 0.5 * jnp.mean(err * err, axis=-1, keepdims=True)

    def body(x_ref, r_ref, w_ref, t_ref, dx_ref, dxb_ref, loss_ref, dw_ref):
        x2 = x_ref[...] + r_ref[...]
        rows, vjp = jax.vjp(functools.partial(per_row_loss, tgt=t_ref[...]), x2, w_ref[...])
        dx2, dw = vjp(jnp.ones_like(rows))
        dx_ref[...] = dx2
        dxb_ref[...] = dx2.astype(BF16)

        @pl.when(pl.program_id(0) == 0)
        def _():
            dw_ref[...] = jnp.zeros_like(dw_ref)
            loss_ref[...] = jnp.zeros_like(loss_ref)

        dw_ref[...] += dw
        loss_ref[...] += jnp.sum(rows, axis=0, keepdims=True)

    row = pl.BlockSpec((ROW_TILE, d), lambda i: (i, 0))
    vec = pl.BlockSpec((1, d), lambda i: (0, 0))
    return pl.pallas_call(
        body, name="final_norm_loss",
        out_shape=(jax.ShapeDtypeStruct((l, d), F32), jax.ShapeDtypeStruct((l, d), BF16),
                   jax.ShapeDtypeStruct((1, 1), F32), jax.ShapeDtypeStruct((1, d), F32)),
        grid=(l // ROW_TILE,),
        in_specs=[row, row, vec, row],
        out_specs=(row, row, pl.BlockSpec((1, 1), lambda i: (0, 0)), vec),
        compiler_params=_cparams("arbitrary"),
    )(x, r, fw, target)


def _merge_fn(gs, gd, ys, yd):
    return _sigmoid(gs) * ys + _sigmoid(gd) * yd


def _merge_fwd(proj, off_gs, off_gd, ys, yd):
    l, d = ys.shape
    cw = min(1024, d)
    blk = lambda off: pl.BlockSpec((ROW_TILE, cw), lambda i, j: (i, off // cw + j))

    def body(gs_ref, gd_ref, ys_ref, yd_ref, o_ref):
        o_ref[...] = _merge_fn(gs_ref[...], gd_ref[...], ys_ref[...], yd_ref[...]).astype(BF16)

    return pl.pallas_call(
        body, name="merge_fwd",
        out_shape=jax.ShapeDtypeStruct((l, d), BF16),
        grid=(l // ROW_TILE, d // cw),
        in_specs=[blk(off_gs), blk(off_gd), blk(0), blk(0)],
        out_specs=blk(0),
        compiler_params=_cparams("parallel", "parallel"),
    )(proj, proj, ys, yd)


def _merge_bwd(proj, off_gs, off_gd, ys, yd, dmixed):
    l, d = ys.shape
    cw = min(1024, d)
    blk = lambda off: pl.BlockSpec((ROW_TILE, cw), lambda i, j: (i, off // cw + j))

    def body(gs_ref, gd_ref, ys_ref, yd_ref, dm_ref, dgs_ref, dgd_ref, dys_ref, dyd_ref):
        _, vjp = jax.vjp(_merge_fn, gs_ref[...], gd_ref[...], ys_ref[...], yd_ref[...])
        dgs, dgd, dys, dyd = vjp(dm_ref[...])
        dgs_ref[...] = dgs.astype(BF16)
        dgd_ref[...] = dgd.astype(BF16)
        dys_ref[...] = dys.astype(BF16)
        dyd_ref[...] = dyd.astype(BF16)

    out = jax.ShapeDtypeStruct((l, d), BF16)
    return pl.pallas_call(
        body, name="merge_bwd",
        out_shape=(out, out, out, out),
        grid=(l // ROW_TILE, d // cw),
        in_specs=[blk(off_gs), blk(off_gd), blk(0), blk(0), blk(0)],
        out_specs=(blk(0), blk(0), blk(0), blk(0)),
        compiler_params=_cparams("parallel", "parallel"),
    )(proj, proj, ys, yd, dmixed)


def _s5_disc_fn(lam_re, lam_im, log_step):
    step = jnp.exp(log_step)
    mag = jnp.exp(lam_re * step)
    abar_re = mag * jnp.cos(lam_im * step)
    abar_im = mag * jnp.sin(lam_im * step)
    den = lam_re * lam_re + lam_im * lam_im
    xr = abar_re - 1.0
    f_re = (xr * lam_re + abar_im * lam_im) / den
    f_im = (abar_im * lam_re - xr * lam_im) / den
    return abar_re, abar_im, f_re, f_im


def _s5_disc_fwd(lam_re, lam_im, log_step):
    g, p = lam_re.shape

    def body(lr_ref, li_ref, ls_ref, ar_ref, ai_ref, fr_ref, fi_ref):
        ar, ai, fr, fi = _s5_disc_fn(lr_ref[...], li_ref[...], ls_ref[...])
        ar_ref[...] = ar
        ai_ref[...] = ai
        fr_ref[...] = fr
        fi_ref[...] = fi

    o = jax.ShapeDtypeStruct((g, p), F32)
    return pl.pallas_call(body, name="s5_disc_fwd", out_shape=(o, o, o, o),
                          compiler_params=_cparams())(lam_re, lam_im, log_step)


def _s5_disc_bwd(lam_re, lam_im, log_step, dar, dai, dfr, dfi):
    g, p = lam_re.shape

    def body(lr_ref, li_ref, ls_ref, dar_ref, dai_ref, dfr_ref, dfi_ref, dlr_ref, dli_ref, dls_ref):
        _, vjp = jax.vjp(_s5_disc_fn, lr_ref[...], li_ref[...], ls_ref[...])
        dlr, dli, dls = vjp((dar_ref[...], dai_ref[...], dfr_ref[...], dfi_ref[...]))
        dlr_ref[...] = dlr
        dli_ref[...] = dli
        dls_ref[...] = dls

    o = jax.ShapeDtypeStruct((g, p), F32)
    return pl.pallas_call(body, name="s5_disc_bwd",
                          out_shape=(o, o, jax.ShapeDtypeStruct((g, 1), F32)),
                          compiler_params=_cparams())(lam_re, lam_im, log_step, dar, dai, dfr, dfi)


def _s5_bbar_fwd(f_re, f_im, b_re, b_im):
    n, c = b_re.shape

    def body(fr_ref, fi_ref, br_ref, bi_ref, or_ref, oi_ref):
        fr, fi, br, bi = fr_ref[...], fi_ref[...], br_ref[...], bi_ref[...]
        or_ref[...] = fr * br - fi * bi
        oi_ref[...] = fr * bi + fi * br

    o = jax.ShapeDtypeStruct((n, c), F32)
    return pl.pallas_call(body, name="s5_bbar_fwd", out_shape=(o, o),
                          compiler_params=_cparams())(f_re, f_im, b_re, b_im)


def _s5_bbar_bwd(f_re, f_im, b_re, b_im, dbr, dbi):
    n, c = b_re.shape

    def body(fr_ref, fi_ref, br_ref, bi_ref, dor_ref, doi_ref, dfr_ref, dfi_ref, dbr_ref, dbi_ref):
        fr, fi, br, bi = fr_ref[...], fi_ref[...], br_ref[...], bi_ref[...]
        dor, doi = dor_ref[...], doi_ref[...]
        dfr_ref[...] = jnp.sum(dor * br + doi * bi, axis=-1, keepdims=True)
        dfi_ref[...] = jnp.sum(doi * br - dor * bi, axis=-1, keepdims=True)
        dbr_ref[...] = fr * dor + fi * doi
        dbi_ref[...] = fr * doi - fi * dor

    col = jax.ShapeDtypeStruct((n, 1), F32)
    o = jax.ShapeDtypeStruct((n, c), F32)
    return pl.pallas_call(body, name="s5_bbar_bwd", out_shape=(col, col, o, o),
                          compiler_params=_cparams())(f_re, f_im, b_re, b_im, dbr, dbi)


SUBLANES = 8


def _scan_groups(xr, xi, ar, ai, reverse):
    t, n = xr.shape
    xr, xi = xr.reshape(t // SUBLANES, SUBLANES, n), xi.reshape(t // SUBLANES, SUBLANES, n)
    sub = lax.broadcasted_iota(jnp.int32, (1, SUBLANES, 1), 1)
    pr, pi = ar.reshape(1, 1, n), ai.reshape(1, 1, n)
    for sh in (1, 2, 4):
        keep = (sub < SUBLANES - sh) if reverse else (sub >= sh)
        cr, ci = jnp.where(keep, pr, 0.0), jnp.where(keep, pi, 0.0)
        shift = SUBLANES - sh if reverse else sh
        sr, si = pltpu.roll(xr, shift, 1), pltpu.roll(xi, shift, 1)
        xr, xi = xr + cr * sr - ci * si, xi + cr * si + ci * sr
        pr, pi = pr * pr - pi * pi, 2.0 * pr * pi
    return xr.reshape(t, n), xi.reshape(t, n)


def _scan_rows(xr, xi, ar, ai, cr, ci, sr_ref, si_ref, reverse):
    t, n = xr.shape
    xr, xi = _scan_groups(xr, xi, ar, ai, reverse)
    sr_ref[...] = xr
    si_ref[...] = xi
    sub = lax.broadcasted_iota(jnp.int32, (SUBLANES, n), 0)
    seed = sub == (SUBLANES - 1 if reverse else 0)
    pwr, pwi = _scan_groups(jnp.where(seed, ar, 0.0), jnp.where(seed, ai, 0.0), ar, ai, reverse)
    groups = range(t // SUBLANES)
    edge = 0 if reverse else SUBLANES - 1
    for g in (reversed(groups) if reverse else groups):
        rows = slice(g * SUBLANES, (g + 1) * SUBLANES)
        vr = sr_ref[rows, :] + (pwr * cr - pwi * ci)
        vi = si_ref[rows, :] + (pwr * ci + pwi * cr)
        sr_ref[rows, :] = vr
        si_ref[rows, :] = vi
        cr, ci = vr[edge:edge + 1, :], vi[edge:edge + 1, :]
    return cr, ci


def _s5_states(u_bf, bbr, bbi, ar, ai, cr, ci, sr_ref, si_ref):
    return _scan_rows(_dot(u_bf, bbr), _dot(u_bf, bbi), ar, ai, cr, ci, sr_ref, si_ref, reverse=False)


def _s5_fwd(proj, bbr, bbi, a_re, a_im, ctr, cti, d_skip, d_s5):
    l = proj.shape[0]
    nb, uc, ns = bbr.shape
    t = min(S5_T, l)
    nt = l // t

    def body(u_ref, bbr_ref, bbi_ref, ar_ref, ai_ref, ctr_ref, cti_ref, d_ref,
             y_ref, car_r_ref, car_i_ref, sr_ref, si_ref, cr, ci):
        @pl.when(pl.program_id(1) == 0)
        def _():
            cr[...] = jnp.zeros_like(cr)
            ci[...] = jnp.zeros_like(ci)

        car_r_ref[...] = cr[...]
        car_i_ref[...] = ci[...]
        u = u_ref[...]
        cr[...], ci[...] = _s5_states(u.astype(BF16), bbr_ref[...], bbi_ref[...], ar_ref[...],
                                      ai_ref[...], cr[...], ci[...], sr_ref, si_ref)
        y_ref[...] = (_bdot(sr_ref[...], ctr_ref[...]) - _bdot(si_ref[...], cti_ref[...])
                      + d_ref[...] * u)

    per_block = lambda shape: pl.BlockSpec((None,) + shape, lambda b, n: (b, 0, 0))
    return pl.pallas_call(
        body, name="s5_fwd",
        out_shape=(jax.ShapeDtypeStruct((l, d_s5), F32),
                   jax.ShapeDtypeStruct((nt, 1, nb * ns), F32),
                   jax.ShapeDtypeStruct((nt, 1, nb * ns), F32),
                   jax.ShapeDtypeStruct((l, nb * ns), F32),
                   jax.ShapeDtypeStruct((l, nb * ns), F32)),
        grid=(nb, nt),
        in_specs=[pl.BlockSpec((t, uc), lambda b, n: (n, b)),
                  per_block((uc, ns)), per_block((uc, ns)),
                  per_block((1, ns)), per_block((1, ns)),
                  per_block((ns, uc)), per_block((ns, uc)),
                  pl.BlockSpec((1, uc), lambda b, n: (0, b))],
        out_specs=(pl.BlockSpec((t, uc), lambda b, n: (n, b)),
                   pl.BlockSpec((None, 1, ns), lambda b, n: (n, 0, b)),
                   pl.BlockSpec((None, 1, ns), lambda b, n: (n, 0, b)),
                   pl.BlockSpec((t, ns), lambda b, n: (n, b)),
                   pl.BlockSpec((t, ns), lambda b, n: (n, b))),
        scratch_shapes=[pltpu.VMEM((1, ns), F32), pltpu.VMEM((1, ns), F32)],
        compiler_params=_cparams("parallel", "arbitrary"),
    )(proj, bbr, bbi, a_re, a_im, ctr, cti, d_skip)


def _s5_bwd(proj, dy, bbr, bbi, a_re, a_im, cbr, cbi, d_skip, car_r, car_i, states_r, states_i):
    l, d_s5 = dy.shape
    nb, uc, ns = bbr.shape
    t = min(S5_T, l)
    nt = l // t

    def body(u_ref, dy_ref, bbr_ref, bbi_ref, ar_ref, ai_ref, cbr_ref, cbi_ref, d_ref,
             car_r_ref, car_i_ref, sr_ref, si_ref,
             du_ref, dar_ref, dai_ref, dbbr_ref, dbbi_ref, dcbr_ref, dcbi_ref, dd_ref, gcr, gci,
             gr_ref, gi_ref):
        @pl.when(pl.program_id(1) == 0)
        def _():
            gcr[...] = jnp.zeros_like(gcr)
            gci[...] = jnp.zeros_like(gci)
            for ref in (dar_ref, dai_ref, dbbr_ref, dbbi_ref, dcbr_ref, dcbi_ref, dd_ref):
                ref[...] = jnp.zeros_like(ref)

        row = lax.broadcasted_iota(jnp.int32, (t, 1), 0)
        u, dy = u_ref[...], dy_ref[...]
        u_bf, dy_bf = u.astype(BF16), dy.astype(BF16)
        ar, ai = ar_ref[...], ai_ref[...]
        cr, ci = car_r_ref[...], car_i_ref[...]
        sr, si = sr_ref[...], si_ref[...]
        first = row == 0
        pr = jnp.where(first, cr, pltpu.roll(sr, 1, 0))
        pi = jnp.where(first, ci, pltpu.roll(si, 1, 0))
        gcr[...], gci[...] = _scan_rows(_dot(dy_bf, cbr_ref[...]), -_dot(dy_bf, cbi_ref[...]), ar, -ai,
                                        gcr[...], gci[...], gr_ref, gi_ref, reverse=True)
        gr, gi = gr_ref[...], gi_ref[...]
        dar_ref[...] += jnp.sum(gr * pr + gi * pi, axis=0, keepdims=True)
        dai_ref[...] += jnp.sum(gi * pr - gr * pi, axis=0, keepdims=True)
        gr_bf, gi_bf = gr.astype(BF16), gi.astype(BF16)
        tn = ((0,), (0,))
        dbbr_ref[...] += _dot(u_bf, gr_bf, tn)
        dbbi_ref[...] += _dot(u_bf, gi_bf, tn)
        dcbr_ref[...] += _dot(dy_bf, sr.astype(BF16), tn)
        dcbi_ref[...] -= _dot(dy_bf, si.astype(BF16), tn)
        nt_dims = ((1,), (1,))
        du = _dot(gr_bf, bbr_ref[...], nt_dims) + _dot(gi_bf, bbi_ref[...], nt_dims) + dy * d_ref[...]
        du_ref[...] = du.astype(BF16)
        dd_ref[...] += jnp.sum(dy * u, axis=0, keepdims=True)

    rev = lambda n: nt - 1 - n
    per_block = lambda shape: pl.BlockSpec((None,) + shape, lambda b, n: (b, 0, 0))
    acc = jax.ShapeDtypeStruct((nb, uc, ns), F32)
    vec = jax.ShapeDtypeStruct((nb, 1, ns), F32)
    return pl.pallas_call(
        body, name="s5_bwd",
        out_shape=(jax.ShapeDtypeStruct((l, d_s5), BF16), vec, vec, acc, acc, acc, acc,
                   jax.ShapeDtypeStruct((1, d_s5), F32)),
        grid=(nb, nt),
        in_specs=[pl.BlockSpec((t, uc), lambda b, n: (rev(n), b)),
                  pl.BlockSpec((t, uc), lambda b, n: (rev(n), b)),
                  per_block((uc, ns)), per_block((uc, ns)),
                  per_block((1, ns)), per_block((1, ns)),
                  per_block((uc, ns)), per_block((uc, ns)),
                  pl.BlockSpec((1, uc), lambda b, n: (0, b)),
                  pl.BlockSpec((None, 1, ns), lambda b, n: (rev(n), 0, b)),
                  pl.BlockSpec((None, 1, ns), lambda b, n: (rev(n), 0, b)),
                  pl.BlockSpec((t, ns), lambda b, n: (rev(n), b)),
                  pl.BlockSpec((t, ns), lambda b, n: (rev(n), b))],
        out_specs=(pl.BlockSpec((t, uc), lambda b, n: (rev(n), b)),
                   per_block((1, ns)), per_block((1, ns)),
                   per_block((uc, ns)), per_block((uc, ns)),
                   per_block((uc, ns)), per_block((uc, ns)),
                   pl.BlockSpec((1, uc), lambda b, n: (0, b))),
        scratch_shapes=[pltpu.VMEM((1, ns), F32), pltpu.VMEM((1, ns), F32)]
        + [pltpu.VMEM((t, ns), F32)] * 2,
        compiler_params=_cparams("parallel", "arbitrary"),
    )(proj, dy, bbr, bbi, a_re, a_im, cbr, cbi, d_skip, car_r, car_i, states_r, states_i)


def _s5_glu_fwd(y1, proj, off_z, wglu):
    l, d = y1.shape

    def body(y_ref, z_ref, w_ref, o_ref):
        y2 = _gelu(y_ref[...])
        y3 = y2 * _sigmoid(_bdot(y2, w_ref[...]))
        o_ref[...] = (y3 * _silu(z_ref[...])).astype(BF16)

    return pl.pallas_call(
        body, name="s5_glu_fwd",
        out_shape=jax.ShapeDtypeStruct((l, d), BF16),
        grid=(l // ROW_TILE,),
        in_specs=[pl.BlockSpec((ROW_TILE, d), lambda i: (i, 0)),
                  pl.BlockSpec((ROW_TILE, d), lambda i: (i, off_z // d)),
                  pl.BlockSpec((d, d), lambda i: (0, 0))],
        out_specs=pl.BlockSpec((ROW_TILE, d), lambda i: (i, 0)),
        compiler_params=_cparams("parallel"),
    )(y1, proj, wglu)


def _s5_glu_bwd(y1, proj, off_z, wglu, dout):
    l, d = y1.shape

    def body(y_ref, z_ref, w_ref, do_ref, dy_ref, dz_ref, dw_ref):
        y2, gelu_vjp = jax.vjp(_gelu, y_ref[...])
        z = z_ref[...]
        sz, silu_vjp = jax.vjp(_silu, z)
        y2_bf = y2.astype(BF16)
        sg = _sigmoid(_dot(y2_bf, w_ref[...]))
        dout = do_ref[...]
        dy3 = dout * sz
        dz_ref[...] = silu_vjp(dout * (y2 * sg))[0].astype(BF16)
        dgl = (dy3 * y2 * sg * (1.0 - sg)).astype(BF16)
        dy2 = dy3 * sg + _dot(dgl, w_ref[...], ((1,), (1,)))
        dy_ref[...] = gelu_vjp(dy2)[0]

        @pl.when(pl.program_id(0) == 0)
        def _():
            dw_ref[...] = jnp.zeros_like(dw_ref)

        dw_ref[...] += _dot(y2_bf, dgl, ((0,), (0,)))

    row = pl.BlockSpec((ROW_TILE, d), lambda i: (i, 0))
    full = pl.BlockSpec((d, d), lambda i: (0, 0))
    return pl.pallas_call(
        body, name="s5_glu_bwd",
        out_shape=(jax.ShapeDtypeStruct((l, d), F32), jax.ShapeDtypeStruct((l, d), BF16),
                   jax.ShapeDtypeStruct((d, d), F32)),
        grid=(l // ROW_TILE,),
        in_specs=[row, pl.BlockSpec((ROW_TILE, d), lambda i: (i, off_z // d)), full, row],
        out_specs=(row, row, full),
        compiler_params=_cparams("arbitrary"),
    )(y1, proj, wglu, dout)


def _shift_rows(x, k, back=False):
    if k == 0:
        return x
    t = x.shape[0]
    row = lax.broadcasted_iota(jnp.int32, (t, 1), 0)
    if back:
        return jnp.where(row < t - k, pltpu.roll(x, t - k, 0), 0.0)
    return jnp.where(row >= k, pltpu.roll(x, k, 0), 0.0)


def _dn_conv(x, w_ref):
    return sum(w_ref[CONV_K - 1 - k:CONV_K - k, :] * _shift_rows(x, k) for k in range(CONV_K))


def _dn_post_conv(c, j):
    y = _silu(c)
    n = y * lax.rsqrt(jnp.sum(y * y, axis=-1, keepdims=True) + EPS)
    n = n * jnp.where(j < DN_HEADS, DN_HEAD_DIM ** -0.5, 1.0)
    return jnp.where(j < 2 * DN_HEADS, n, y)


def _dn_prep_fwd(proj, off_qkv, conv_w):
    l = proj.shape[0]
    hd = DN_HEAD_DIM
    nblk = 3 * DN_HEADS

    def body(x_ref, w_ref, o_ref):
        o_ref[...] = _dn_post_conv(_dn_conv(x_ref[...], w_ref), pl.program_id(0))

    return pl.pallas_call(
        body, name="dn_prep_fwd",
        out_shape=jax.ShapeDtypeStruct((l, nblk * hd), F32),
        grid=(nblk,),
        in_specs=[pl.BlockSpec((l, hd), lambda j: (0, off_qkv // hd + j)),
                  pl.BlockSpec((CONV_K, hd), lambda j: (0, j))],
        out_specs=pl.BlockSpec((l, hd), lambda j: (0, j)),
        compiler_params=_cparams("parallel"),
    )(proj, conv_w)


def _dn_prep_bwd(proj, off_qkv, conv_w, dqkv):
    l = proj.shape[0]
    hd = DN_HEAD_DIM
    nblk = 3 * DN_HEADS

    def body(x_ref, w_ref, do_ref, dx_ref, dw_ref):
        x = x_ref[...]
        j = pl.program_id(0)
        _, vjp = jax.vjp(functools.partial(_dn_post_conv, j=j), _dn_conv(x, w_ref))
        dc = vjp(do_ref[...])[0]
        dx = sum(w_ref[CONV_K - 1 - k:CONV_K - k, :] * _shift_rows(dc, k, back=True)
                 for k in range(CONV_K))
        dx_ref[...] = dx.astype(BF16)
        for k in range(CONV_K):
            dw_ref[CONV_K - 1 - k:CONV_K - k, :] = jnp.sum(dc * _shift_rows(x, k), axis=0,
                                                           keepdims=True)

    return pl.pallas_call(
        body, name="dn_prep_bwd",
        out_shape=(jax.ShapeDtypeStruct((l, nblk * hd), BF16),
                   jax.ShapeDtypeStruct((CONV_K, nblk * hd), F32)),
        grid=(nblk,),
        in_specs=[pl.BlockSpec((l, hd), lambda j: (0, off_qkv // hd + j)),
                  pl.BlockSpec((CONV_K, hd), lambda j: (0, j)),
                  pl.BlockSpec((None, l, hd), lambda j: (j // DN_HEADS, 0, j % DN_HEADS))],
        out_specs=(pl.BlockSpec((l, hd), lambda j: (0, j)),
                   pl.BlockSpec((CONV_K, hd), lambda j: (0, j))),
        compiler_params=_cparams("parallel"),
    )(proj, conv_w, dqkv)


def _dn_gate_fn(ba, a_log_row, dt_row):
    lane = lax.broadcasted_iota(jnp.int32, ba.shape, 1)
    beta = _sigmoid(ba)
    g = -jnp.exp(a_log_row) * _softplus(ba + dt_row)
    return jnp.where(lane < DN_HEADS, beta, jnp.where(lane < 2 * DN_HEADS, g, 0.0))


def _dn_gates_fwd(proj, off_ba, a_log_row, dt_row):
    l = proj.shape[0]
    row = pl.BlockSpec((ROW_TILE, 128), lambda i: (i, off_ba // 128))
    vec = pl.BlockSpec((1, 128), lambda i: (0, 0))

    def body(ba_ref, al_ref, dt_ref, o_ref):
        o_ref[...] = _dn_gate_fn(ba_ref[...], al_ref[...], dt_ref[...])

    return pl.pallas_call(
        body, name="dn_gates_fwd",
        out_shape=jax.ShapeDtypeStruct((l, 128), F32),
        grid=(l // ROW_TILE,),
        in_specs=[row, vec, vec],
        out_specs=pl.BlockSpec((ROW_TILE, 128), lambda i: (i, 0)),
        compiler_params=_cparams("parallel"),
    )(proj, a_log_row, dt_row)


def _dn_gates_bwd(proj, off_ba, a_log_row, dt_row, dgb_heads):
    l = proj.shape[0]
    nh = dgb_heads.shape[0]
    row = pl.BlockSpec((ROW_TILE, 128), lambda i: (i, off_ba // 128))
    vec = pl.BlockSpec((1, 128), lambda i: (0, 0))

    def body(ba_ref, al_ref, dt_ref, dg_ref, dba_ref, dal_ref, ddt_ref):
        _, vjp = jax.vjp(_dn_gate_fn, ba_ref[...], al_ref[...], dt_ref[...])
        dgb = dg_ref[0]
        for h in range(1, nh):
            dgb = dgb + dg_ref[h]
        dba, dal, ddt = vjp(dgb)
        dba_ref[...] = dba.astype(BF16)

        @pl.when(pl.program_id(0) == 0)
        def _():
            dal_ref[...] = jnp.zeros_like(dal_ref)
            ddt_ref[...] = jnp.zeros_like(ddt_ref)

        dal_ref[...] += dal
        ddt_ref[...] += ddt

    return pl.pallas_call(
        body, name="dn_gates_bwd",
        out_shape=(jax.ShapeDtypeStruct((l, 128), BF16), jax.ShapeDtypeStruct((1, 128), F32),
                   jax.ShapeDtypeStruct((1, 128), F32)),
        grid=(l // ROW_TILE,),
        in_specs=[row, vec, vec, pl.BlockSpec((nh, ROW_TILE, 128), lambda i: (0, i, 0))],
        out_specs=(pl.BlockSpec((ROW_TILE, 128), lambda i: (i, 0)), vec, vec),
        compiler_params=_cparams("arbitrary"),
    )(proj, a_log_row, dt_row, dgb_heads)


@jax.custom_vjp
def _unit_lower_inverses(a_mats):
    c = a_mats[0].shape[0]
    eye = (lax.broadcasted_iota(jnp.int32, (c, c), 0) == lax.broadcasted_iota(jnp.int32, (c, c), 1)).astype(F32)
    t_inv = [eye - a for a in a_mats]
    power = a_mats
    for _ in range(int(math.log2(c)) - 1):
        power = [_bdot(p, p) for p in power]
        t_inv = [t + _bdot(t, p) for t, p in zip(t_inv, power)]
    return t_inv


def _unit_lower_inverses_fwd(a_mats):
    t_inv = _unit_lower_inverses(a_mats)
    return t_inv, t_inv


def _inverse_cotangents(t_inv, grads):
    right = [_dot3_dims(g, t, ((1,), (1,))) for g, t in zip(grads, t_inv)]
    return [-_dot3_dims(t, r, ((0,), (0,))) for t, r in zip(t_inv, right)]


_unit_lower_inverses.defvjp(_unit_lower_inverses_fwd,
                            lambda t_inv, grads: (_inverse_cotangents(t_inv, grads),))


@jax.custom_vjp
def _kept_inverses(a_mats, t_inv):
    return t_inv


_kept_inverses.defvjp(
    lambda a_mats, t_inv: (t_inv, t_inv),
    lambda t_inv, grads: (_inverse_cotangents(t_inv, grads), [jnp.zeros_like(t) for t in t_inv]))


def _dn_chunk_fn(states, qs, ks, vs, gb, heads, kept_inverses=None, return_inverses=False):
    c = qs[0].shape[0]
    each = lambda f, *lists: [f(*args) for args in zip(*lists)]
    lane = lax.broadcasted_iota(jnp.int32, gb.shape, 1)
    ri = lax.broadcasted_iota(jnp.int32, (c, c), 0)
    ci = lax.broadcasted_iota(jnp.int32, (c, c), 1)
    causal, strict = ri >= ci, ri > ci
    eye = (ri == ci).astype(F32)
    rowi = lax.broadcasted_iota(jnp.int32, (c, 1), 0)
    nt_dims = ((1,), (1,))
    hdot = functools.partial(_dot, precision=HIGHEST)

    pick = lambda m, at: jnp.sum(jnp.where(lane == at, m, 0.0), axis=1, keepdims=True)
    gb_cum = hdot(causal.astype(F32), gb)
    beta = [pick(gb, h) for h in heads]
    gc = [pick(gb_cum, h + DN_HEADS) for h in heads]
    gc_row = each(lambda g: jnp.sum(eye * g, axis=0, keepdims=True), gc)
    decay = each(lambda g, gr: jnp.where(causal, jnp.exp(jnp.where(causal, g - gr, 0.0)), 0.0),
                 gc, gc_row)
    kk = each(lambda k: _bdot(k, k, nt_dims), ks)
    a_mat = each(lambda b, m, dc: jnp.where(strict, b * m * dc, 0.0), beta, kk, decay)

    t_inv = (_unit_lower_inverses(a_mat) if kept_inverses is None
             else _kept_inverses(a_mat, kept_inverses))
    egc = each(jnp.exp, gc)
    u_c = each(lambda t, v, b: _dot3(t, v * b), t_inv, vs, beta)
    w_c = each(lambda t, k, b, e: _dot3(t, k * (b * e)), t_inv, ks, beta, egc)
    qk = each(lambda q, k, dc: _bdot(q, k, nt_dims) * dc, qs, ks, decay)
    g_end = each(lambda g: jnp.sum(jnp.where(rowi == c - 1, g, 0.0), axis=0, keepdims=True), gc)
    v_new = each(lambda u, w, s: u - _bdot(w, s), u_c, w_c, states)
    o = each(lambda q, e, s, m, vn: _bdot(q * e, s) + _bdot(m, vn), qs, egc, states, qk, v_new)
    new_states = each(
        lambda s, ge, k, g, vn: s * jnp.exp(ge) + _bdot(k * jnp.exp(ge - g), vn, ((0,), (0,))),
        states, g_end, ks, gc, v_new)
    return (o, new_states, t_inv) if return_inverses else (o, new_states)


def _dn_chunk_specs(order):
    hd, nh, hps = DN_HEAD_DIM, DN_HEADS, DN_HEADS_PER_STEP
    qkv = lambda part: pl.BlockSpec((CHUNK, hps * hd), lambda h, n: (order(n), part * (nh // hps) + h))
    gb = pl.BlockSpec((CHUNK, 128), lambda h, n: (order(n), 0))
    state = pl.BlockSpec((hps, None, hd, hd), lambda h, n: (h, order(n), 0, 0))
    inverse = pl.BlockSpec((hps, None, CHUNK, CHUNK), lambda h, n: (h, order(n), 0, 0))
    return qkv, gb, state, inverse


def _dn_chunk_fwd(qkv, gb):
    l = qkv.shape[0]
    hd, nh, hps = DN_HEAD_DIM, DN_HEADS, DN_HEADS_PER_STEP
    n_chunks = l // CHUNK
    qkv_spec, gb_spec, state_spec, inverse_spec = _dn_chunk_specs(lambda n: n)

    def body(q_ref, k_ref, v_ref, gb_ref, o_ref, s_ref, t_ref, state):
        @pl.when(pl.program_id(1) == 0)
        def _():
            state[...] = jnp.zeros_like(state)

        cols = [slice(i * hd, (i + 1) * hd) for i in range(hps)]
        states = [state[i] for i in range(hps)]
        for i in range(hps):
            s_ref[i] = states[i]
        o, new_states, t_inv = _dn_chunk_fn(
            states, [q_ref[:, cs] for cs in cols], [k_ref[:, cs] for cs in cols],
            [v_ref[:, cs] for cs in cols], gb_ref[...],
            [pl.program_id(0) * hps + i for i in range(hps)], return_inverses=True)
        for i in range(hps):
            o_ref[:, cols[i]] = o[i]
            t_ref[i] = t_inv[i]
            state[i] = new_states[i]

    return pl.pallas_call(
        body, name="dn_chunk_fwd",
        out_shape=(jax.ShapeDtypeStruct((l, nh * hd), F32),
                   jax.ShapeDtypeStruct((nh, n_chunks, hd, hd), F32),
                   jax.ShapeDtypeStruct((nh, n_chunks, CHUNK, CHUNK), F32)),
        grid=(nh // hps, n_chunks),
        in_specs=[qkv_spec(0), qkv_spec(1), qkv_spec(2), gb_spec],
        out_specs=(pl.BlockSpec((CHUNK, hps * hd), lambda h, n: (n, h)), state_spec, inverse_spec),
        scratch_shapes=[pltpu.VMEM((hps, hd, hd), F32)],
        compiler_params=_cparams("parallel", "arbitrary"),
    )(qkv, qkv, qkv, gb)


def _dn_chunk_bwd(qkv, gb, states, inverses, do):
    l = qkv.shape[0]
    hd, nh, hps = DN_HEAD_DIM, DN_HEADS, DN_HEADS_PER_STEP
    n_chunks = l // CHUNK
    rev = lambda n: n_chunks - 1 - n
    qkv_spec, gb_spec, state_spec, inverse_spec = _dn_chunk_specs(rev)

    def body(q_ref, k_ref, v_ref, gb_ref, s_ref, t_ref, do_ref, dqkv_ref, dgb_ref, dstate):
        @pl.when(pl.program_id(1) == 0)
        def _():
            dstate[...] = jnp.zeros_like(dstate)

        cols = [slice(i * hd, (i + 1) * hd) for i in range(hps)]
        fn = functools.partial(_dn_chunk_fn, heads=[pl.program_id(0) * hps + i for i in range(hps)],
                               kept_inverses=[t_ref[i] for i in range(hps)])
        _, vjp = jax.vjp(fn, [s_ref[i] for i in range(hps)], [q_ref[:, cs] for cs in cols],
                         [k_ref[:, cs] for cs in cols], [v_ref[:, cs] for cs in cols], gb_ref[...])
        ds, dq, dk, dv, dgb = vjp(([do_ref[:, cs] for cs in cols], [dstate[i] for i in range(hps)]))
        for i in range(hps):
            dstate[i] = ds[i]
            dqkv_ref[0, :, cols[i]] = dq[i]
            dqkv_ref[1, :, cols[i]] = dk[i]
            dqkv_ref[2, :, cols[i]] = dv[i]
        dgb_ref[...] = dgb

    head_out = pl.BlockSpec((CHUNK, hps * hd), lambda h, n: (rev(n), h))
    return pl.pallas_call(
        body, name="dn_chunk_bwd",
        out_shape=(jax.ShapeDtypeStruct((3, l, nh * hd), F32),
                   jax.ShapeDtypeStruct((nh // hps, l, 128), F32)),
        grid=(nh // hps, n_chunks),
        in_specs=[qkv_spec(0), qkv_spec(1), qkv_spec(2), gb_spec, state_spec, inverse_spec, head_out],
        out_specs=(pl.BlockSpec((3, CHUNK, hps * hd), lambda h, n: (0, rev(n), h)),
                   pl.BlockSpec((None, CHUNK, 128), lambda h, n: (h, rev(n), 0))),
        scratch_shapes=[pltpu.VMEM((hps, hd, hd), F32)],
        compiler_params=_cparams("parallel", "arbitrary"),
    )(qkv, qkv, qkv, gb, states, inverses, do)


def _dn_out_fn(o, z, w):
    return _rmsnorm(o, w) * _silu(z)


def _dn_out_fwd(o, proj, off_z, w):
    l, d = o.shape
    hd = DN_HEAD_DIM
    tr = min(4 * ROW_TILE, l)
    blk = lambda off: pl.BlockSpec((tr, hd), lambda i, h: (i, off // hd + h))

    def body(o_ref, z_ref, w_ref, out_ref):
        out_ref[...] = _dn_out_fn(o_ref[...], z_ref[...], w_ref[...]).astype(BF16)

    return pl.pallas_call(
        body, name="dn_out_fwd",
        out_shape=jax.ShapeDtypeStruct((l, d), BF16),
        grid=(l // tr, d // hd),
        in_specs=[blk(0), blk(off_z), pl.BlockSpec((1, hd), lambda i, h: (0, 0))],
        out_specs=blk(0),
        compiler_params=_cparams("parallel", "parallel"),
    )(o, proj, w)


def _dn_out_bwd(o, proj, off_z, w, dout):
    l, d = o.shape
    hd = DN_HEAD_DIM
    tr = min(4 * ROW_TILE, l)
    blk = lambda off: pl.BlockSpec((tr, hd), lambda i, h: (i, off // hd + h))
    vec = pl.BlockSpec((1, hd), lambda i, h: (0, 0))

    def body(o_ref, z_ref, w_ref, dout_ref, do_ref, dz_ref, dw_ref):
        _, vjp = jax.vjp(_dn_out_fn, o_ref[...], z_ref[...], w_ref[...])
        do, dz, dw = vjp(dout_ref[...])
        do_ref[...] = do
        dz_ref[...] = dz.astype(BF16)

        @pl.when((pl.program_id(0) == 0) & (pl.program_id(1) == 0))
        def _():
            dw_ref[...] = jnp.zeros_like(dw_ref)

        dw_ref[...] += dw

    return pl.pallas_call(
        body, name="dn_out_bwd",
        out_shape=(jax.ShapeDtypeStruct((l, d), F32), jax.ShapeDtypeStruct((l, d), BF16),
                   jax.ShapeDtypeStruct((1, hd), F32)),
        grid=(l // tr, d // hd),
        in_specs=[blk(0), blk(off_z), vec, blk(0)],
        out_specs=(blk(0), blk(0), vec),
        compiler_params=_cparams("arbitrary", "arbitrary"),
    )(o, proj, w, dout)


def _tile_2d(rows, cols, budget_bytes=1 << 20):
    for tr in (rows, 4096, 2048, 1024, 512, 256, 128, 64, 32, 16):
        if tr <= rows and rows % tr == 0 and tr * cols * 4 <= budget_bytes:
            return tr, cols
    for tc in (2048, 1024, 512, 256, 128):
        if cols % tc == 0 and rows * tc * 4 <= 2 * budget_bytes:
            return rows, tc
    raise ValueError((rows, cols))


def _adamw_update(g, w_ref, m_ref, v_ref, go_ref, d_ref, mo_ref, vo_ref):
    c1 = 1.0 / (1.0 - ADAM_B1 ** ADAM_STEP)
    c2 = 1.0 / (1.0 - ADAM_B2 ** ADAM_STEP)
    m_new = ADAM_B1 * m_ref[...] + (1.0 - ADAM_B1) * g
    v_new = ADAM_B2 * v_ref[...] + (1.0 - ADAM_B2) * (g * g)
    go_ref[...] = g
    mo_ref[...] = m_new
    vo_ref[...] = v_new
    d_ref[...] = -ADAM_LR * ((m_new * c1) / (jnp.sqrt(v_new * c2) + ADAM_EPS) + ADAM_WD * w_ref[...])


def _adamw(w, m, v, gslots, name):
    rows, cols = w.shape
    ns = gslots.shape[0]
    tr, tc = _tile_2d(rows, cols)

    def body(w_ref, m_ref, v_ref, g_ref, go_ref, d_ref, mo_ref, vo_ref):
        g = g_ref[0].astype(F32)
        for s in range(1, ns):
            g = g + g_ref[s].astype(F32)
        _adamw_update(g, w_ref, m_ref, v_ref, go_ref, d_ref, mo_ref, vo_ref)

    blk = pl.BlockSpec((tr, tc), lambda i, j: (i, j))
    o = jax.ShapeDtypeStruct((rows, cols), F32)
    return pl.pallas_call(
        body, name=name, out_shape=(o, o, o, o),
        grid=(rows // tr, cols // tc),
        in_specs=[blk, blk, blk, pl.BlockSpec((ns, tr, tc), lambda i, j: (0, i, j))],
        out_specs=(blk, blk, blk, blk),
        compiler_params=_cparams("parallel", "parallel"),
    )(w, m, v, gslots)


def _slot_sum(gslots, name):
    ns, rows, cols = gslots.shape
    tr, tc = _tile_2d(rows, cols)

    def body(g_ref, o_ref):
        g = g_ref[0]
        for s in range(1, ns):
            g = g + g_ref[s]
        o_ref[...] = g

    return pl.pallas_call(
        body, name=name, out_shape=jax.ShapeDtypeStruct((rows, cols), F32),
        grid=(rows // tr, cols // tc),
        in_specs=[pl.BlockSpec((ns, tr, tc), lambda i, j: (0, i, j))],
        out_specs=pl.BlockSpec((tr, tc), lambda i, j: (i, j)),
        compiler_params=_cparams("parallel", "parallel"),
    )(gslots)


HBM_SPEC = pl.BlockSpec(memory_space=pl.ANY)


def _all_gather(arrs, name, relayed=(), after=None):
    n = len(arrs)
    n_sems = 13
    n_in = n + (after is not None)

    def body(*refs):
        ins, outs = refs[:n], refs[n_in:n_in + n]
        send_sems, recv_sems, local_sems = refs[n_in + n:]
        x, y, c = lax.axis_index("x"), lax.axis_index("y"), lax.axis_index("c")
        me, sibling = (x, y, c), (x, y, 1 - c)
        chips = [(1 - x, y), (x, 1 - y), (1 - x, 1 - y)]
        index = lambda px, py, pc: 4 * px + 2 * py + pc

        def copy(a, k, block, to, src=None, cols=None):
            dst = outs[a].at[index(*block)]
            src = dst if src is None else src
            if cols is not None:
                dst, src = dst.at[:, cols], src.at[:, cols]
            return pltpu.make_async_remote_copy(
                src_ref=src, dst_ref=dst, send_sem=send_sems.at[a, k], recv_sem=recv_sems.at[a, k],
                device_id=to, device_id_type=MESH)

        mine = [pltpu.make_async_copy(ins[a], outs[a].at[index(*me)], local_sems.at[a])
                for a in range(n)]
        for cp in mine:
            cp.start()
        sends = []

        def start(cp):
            cp.start()
            sends.append(cp)

        halves = {a: (pl.ds(0, arrs[a].shape[1] // 2), pl.ds(arrs[a].shape[1] // 2, arrs[a].shape[1] // 2))
                  for a in relayed}
        near_x, near_y, far = [(*chip, c) for chip in chips]
        for a in range(n):
            start(copy(a, 0, me, sibling, src=ins[a]))
            if a in relayed:
                left, right = halves[a]
                for k, to, cols in ((1, near_x, left), (3, near_y, right), (2, near_x, right), (4, near_y, left)):
                    start(copy(a, k, me, to, src=ins[a], cols=cols))
            else:
                for j, chip in enumerate(chips):
                    start(copy(a, 1 + j, me, (*chip, c), src=ins[a]))
        for a in relayed:
            left, right = halves[a]
            for k, block, cols, onward, to_sibling in (
                    (1, near_x, left, (5, near_y), 7), (3, near_y, right, (6, near_x), 10),
                    (2, near_x, right, None, 8), (4, near_y, left, None, 9),
                    (5, far, left, None, 11), (6, far, right, None, 12)):
                copy(a, k, block, me, cols=cols).wait_recv()
                if onward is not None:
                    start(copy(a, onward[0], block, onward[1], cols=cols))
                start(copy(a, to_sibling, block, sibling, cols=cols))
        for j, chip in enumerate(chips):
            for a in range(n):
                if a not in relayed:
                    copy(a, 1 + j, (*chip, c), me).wait_recv()
                    start(copy(a, 4 + j, (*chip, c), sibling))
        for a in range(n):
            copy(a, 0, sibling, me).wait_recv()
            if a in relayed:
                left, right = halves[a]
                for k, chip, cols in ((7, chips[0], left), (8, chips[0], right), (9, chips[1], left),
                                      (10, chips[1], right), (11, chips[2], left), (12, chips[2], right)):
                    copy(a, k, (*chip, 1 - c), me, cols=cols).wait_recv()
            else:
                for j, chip in enumerate(chips):
                    copy(a, 4 + j, (*chip, 1 - c), me).wait_recv()
        for cp in sends:
            cp.wait_send()
        for cp in mine:
            cp.wait()

    return pl.pallas_call(
        body, name=name,
        out_shape=[jax.ShapeDtypeStruct((N_DEV,) + a.shape, a.dtype) for a in arrs],
        in_specs=[HBM_SPEC] * n_in, out_specs=[HBM_SPEC] * n,
        scratch_shapes=[pltpu.SemaphoreType.DMA((n, n_sems)), pltpu.SemaphoreType.DMA((n, n_sems)),
                        pltpu.SemaphoreType.DMA((n,))],
    )(*arrs, *([after] if after is not None else []))


def _sibling_swap(arrs, name):
    n = len(arrs)

    def body(*refs):
        ins, outs = refs[:n], refs[n:2 * n]
        send_sems, recv_sems = refs[2 * n:]
        x, y, c = lax.axis_index("x"), lax.axis_index("y"), lax.axis_index("c")
        copies = [pltpu.make_async_remote_copy(
            src_ref=ins[a].at[:, 1 - c], dst_ref=outs[a],
            send_sem=send_sems.at[a], recv_sem=recv_sems.at[a],
            device_id=(x, y, 1 - c), device_id_type=MESH) for a in range(n)]
        for cp in copies:
            cp.start()
        for cp in copies:
            cp.wait()

    return pl.pallas_call(
        body, name=name,
        out_shape=[jax.ShapeDtypeStruct(a.shape[:1] + a.shape[2:], a.dtype) for a in arrs],
        in_specs=[HBM_SPEC] * n, out_specs=[HBM_SPEC] * n,
        scratch_shapes=[pltpu.SemaphoreType.DMA((n,)), pltpu.SemaphoreType.DMA((n,))],
    )(*arrs)


def _pair_sum(mine, theirs, core, name):
    chips, _, rows, cols = mine.shape
    tr, tc = _tile_2d(rows, cols, budget_bytes=2 << 20)

    def body(core_ref, a_ref, b_ref, o_ref):
        o_ref[...] = (a_ref[...].astype(F32) + b_ref[...].astype(F32)).astype(o_ref.dtype)

    slab = pl.BlockSpec((None, tr, tc), lambda ch, i, j, core_ref: (ch, i, j))
    return pl.pallas_call(
        body, name=name, out_shape=jax.ShapeDtypeStruct((chips, rows, cols), mine.dtype),
        grid_spec=pltpu.PrefetchScalarGridSpec(
            num_scalar_prefetch=1, grid=(chips, rows // tr, cols // tc),
            in_specs=[pl.BlockSpec((None, None, tr, tc),
                                   lambda ch, i, j, core_ref: (ch, core_ref[0], i, j)), slab],
            out_specs=slab),
        compiler_params=_cparams("parallel", "parallel", "parallel"),
    )(core, mine, theirs)


HBM_ONLY = pl.BlockSpec(memory_space=pltpu.HBM)
SEM_SPEC = pl.BlockSpec(memory_space=pltpu.SEMAPHORE)
SPLIT_COPY_EFFECT = pltpu.SideEffectType.DATAFLOW_SIDE_EFFECTING


def _flip(v, bit):
    return 1 - v if bit else v


def _chip_slices_plan(n):
    def plan():
        x, y, c = lax.axis_index("x"), lax.axis_index("y"), lax.axis_index("c")
        copies = []
        for k in range(1, 4):
            px, py = _flip(x, k & 2), _flip(y, k & 1)
            copies += [(a, 2 * px + py, 2 * x + y, (px, py, c)) for a in range(n)]
        return copies
    return plan, 3 * n


def _gather_plan(n):
    def plan():
        x, y, c = lax.axis_index("x"), lax.axis_index("y"), lax.axis_index("c")
        copies = []
        for k in range(1, N_DEV):
            peer = (_flip(x, k & 4), _flip(y, k & 2), _flip(c, k & 1))
            copies += [(a, None, 4 * x + 2 * y + c, peer) for a in range(n)]
        return copies + [(a, None, 4 * x + 2 * y + c, None) for a in range(n)]
    return plan, 8 * n


def _planned_copies(plan, srcs, lands, send_sems, recv_sems):
    copies = []
    for i, (a, src_at, land_at, peer) in enumerate(plan()):
        src, dst = srcs[a] if src_at is None else srcs[a].at[src_at], lands[a].at[land_at]
        if peer is None:
            local = pltpu.make_async_copy(src, dst, send_sems[i])
            copies.append((local, local.wait))
        else:
            remote = pltpu.make_async_remote_copy(src_ref=src, dst_ref=dst, send_sem=send_sems[i],
                                                  recv_sem=recv_sems[i], device_id=peer, device_id_type=MESH)
            copies.append((remote, remote.wait))
    return copies


def _split_exchange_start(plan_and_count, arrs, land_shapes, name, after=None):
    plan, n_sems = plan_and_count
    n = len(arrs)

    n_in = 2 * n + (after is not None)

    def body(*refs):
        srcs, lands = refs[:n], refs[n:2 * n]
        send_sems, recv_sems = refs[n_in:n_in + n_sems], refs[n_in + n_sems:n_in + 2 * n_sems]
        token = refs[-1]
        for copy, _ in _planned_copies(plan, srcs, lands, send_sems, recv_sems):
            copy.start()
        token[...] = jnp.zeros_like(token)

    hbm = lambda a: pltpu.HBM(a.shape, a.dtype)
    operands = [pltpu.with_memory_space_constraint(a, pltpu.HBM) for a in arrs]
    operands += [pltpu.with_memory_space_constraint(lax.empty(shape, a.dtype), pltpu.HBM)
                 for a, shape in zip(arrs, land_shapes)]
    out = pl.pallas_call(
        body, name=name,
        out_shape=(*[pltpu.SemaphoreType.DMA(())] * (2 * n_sems),
                   *[hbm(a) for a in operands],
                   jax.ShapeDtypeStruct((8, 128), F32)),
        in_specs=[HBM_ONLY] * (2 * n) + [pl.BlockSpec(memory_space=pl.ANY)] * (after is not None),
        out_specs=(*[SEM_SPEC] * (2 * n_sems), *[HBM_ONLY] * (2 * n),
                   pl.BlockSpec(memory_space=pltpu.VMEM)),
        input_output_aliases={i: 2 * n_sems + i for i in range(2 * n)},
        compiler_params=pltpu.CompilerParams(has_side_effects=SPLIT_COPY_EFFECT),
    )(*operands, *([after] if after is not None else []))
    sems, rest = list(out[:2 * n_sems]), out[2 * n_sems:]
    return sems, list(rest[:n]), list(rest[n:2 * n]), rest[-1]


def _split_exchange_wait(plan_and_count, sems, srcs, lands, after, name):
    plan, n_sems = plan_and_count
    n = len(srcs)

    def body(*refs):
        src_refs, land_refs = refs[:n], refs[n:2 * n]
        send_sems, recv_sems = refs[2 * n:2 * n + n_sems], refs[2 * n + n_sems:2 * n + 2 * n_sems]
        for _, wait in _planned_copies(plan, src_refs, land_refs, send_sems, recv_sems):
            wait()

    hbm = lambda a: pltpu.HBM(a.shape, a.dtype)
    out = pl.pallas_call(
        body, name=name,
        out_shape=(*[hbm(a) for a in srcs], *[hbm(a) for a in lands]),
        in_specs=[HBM_ONLY] * (2 * n) + [SEM_SPEC] * (2 * n_sems) + [pl.BlockSpec(memory_space=pl.ANY)],
        out_specs=tuple([HBM_ONLY] * (2 * n)),
        input_output_aliases={i: i for i in range(2 * n)},
        compiler_params=pltpu.CompilerParams(has_side_effects=SPLIT_COPY_EFFECT),
    )(*srcs, *lands, *sems, after)
    return list(out[n:])


def _adamw_exchanged(w, m, v, own, landed, chip, name):
    rows, cols = w.shape
    tr, tc = _tile_2d(rows, cols)

    def body(chip_ref, w_ref, m_ref, v_ref, own_ref, l1_ref, l2_ref, l3_ref, go_ref, d_ref, mo_ref, vo_ref):
        g = own_ref[...].astype(F32)
        for ref in (l1_ref, l2_ref, l3_ref):
            g = g + ref[...].astype(F32)
        _adamw_update(g, w_ref, m_ref, v_ref, go_ref, d_ref, mo_ref, vo_ref)

    blk = pl.BlockSpec((tr, tc), lambda i, j, chip_ref: (i, j))
    slot = lambda k: pl.BlockSpec((None, tr, tc), lambda i, j, chip_ref: (chip_ref[0] ^ k, i, j))
    o = jax.ShapeDtypeStruct((rows, cols), F32)
    return pl.pallas_call(
        body, name=name, out_shape=(o, o, o, o),
        grid_spec=pltpu.PrefetchScalarGridSpec(
            num_scalar_prefetch=1, grid=(rows // tr, cols // tc),
            in_specs=[blk, blk, blk, slot(0), slot(1), slot(2), slot(3)],
            out_specs=(blk, blk, blk, blk)),
        compiler_params=_cparams("parallel", "parallel"),
    )(chip, w, m, v, own, landed, landed, landed)


def _block_diag(t):
    nb, gpb, r, c = t.shape
    eye = jnp.eye(gpb, dtype=t.dtype)
    return jnp.einsum("ngrc,gh->ngrhc", t, eye).reshape(nb, gpb * r, gpb * c)


def _diag_blocks(t, r, c):
    nb = t.shape[0]
    gpb = t.shape[1] // r
    t = t.reshape(nb, gpb, r, gpb, c)
    return jnp.einsum("ngrhc,gh->ngrc", t, jnp.eye(gpb, dtype=t.dtype))


def _pack_rows(parts):
    flat = jnp.concatenate([p.reshape(-1).astype(F32) for p in parts])
    pad = (-flat.shape[0]) % (256 * 128)
    return jnp.pad(flat, (0, pad)).reshape(-1, 128)


def _unpack_rows(packed, shapes):
    flat = packed.reshape(-1)
    out, at = [], 0
    for shape in shapes:
        size = math.prod(shape)
        out.append(flat[at:at + size].reshape(shape))
        at += size
    return out


def kernel(x, ln_w, w_in, s5_lam_re, s5_lam_im, s5_log_step, s5_b_re, s5_b_im, s5_c_re, s5_c_im, s5_d, s5_w_glu, s5_w_up, dn_conv_w, dn_a_log, dn_dt_bias, dn_norm_w, dn_w_up, w_out, final_norm_w, loss_target, m_ln_w, m_w_in, m_s5_lam_re, m_s5_lam_im, m_s5_log_step, m_s5_b_re, m_s5_b_im, m_s5_c_re, m_s5_c_im, m_s5_d, m_s5_w_glu, m_s5_w_up, m_dn_conv_w, m_dn_a_log, m_dn_dt_bias, m_dn_norm_w, m_dn_w_up, m_w_out, m_final_norm_w, v_ln_w, v_w_in, v_s5_lam_re, v_s5_lam_im, v_s5_log_step, v_s5_b_re, v_s5_b_im, v_s5_c_re, v_s5_c_im, v_s5_d, v_s5_w_glu, v_s5_w_up, v_dn_conv_w, v_dn_a_log, v_dn_dt_bias, v_dn_norm_w, v_dn_w_up, v_w_out, v_final_norm_w):
    weights = dict(ln_w=ln_w, w_in=w_in, s5_lam_re=s5_lam_re, s5_lam_im=s5_lam_im,
                   s5_log_step=s5_log_step, s5_b_re=s5_b_re, s5_b_im=s5_b_im, s5_c_re=s5_c_re,
                   s5_c_im=s5_c_im, s5_d=s5_d, s5_w_glu=s5_w_glu, s5_w_up=s5_w_up,
                   dn_conv_w=dn_conv_w, dn_a_log=dn_a_log, dn_dt_bias=dn_dt_bias,
                   dn_norm_w=dn_norm_w, dn_w_up=dn_w_up, w_out=w_out, final_norm_w=final_norm_w)
    mom_m = dict(ln_w=m_ln_w, w_in=m_w_in, s5_lam_re=m_s5_lam_re, s5_lam_im=m_s5_lam_im,
                 s5_log_step=m_s5_log_step, s5_b_re=m_s5_b_re, s5_b_im=m_s5_b_im,
                 s5_c_re=m_s5_c_re, s5_c_im=m_s5_c_im, s5_d=m_s5_d, s5_w_glu=m_s5_w_glu,
                 s5_w_up=m_s5_w_up, dn_conv_w=m_dn_conv_w, dn_a_log=m_dn_a_log,
                 dn_dt_bias=m_dn_dt_bias, dn_norm_w=m_dn_norm_w, dn_w_up=m_dn_w_up,
                 w_out=m_w_out, final_norm_w=m_final_norm_w)
    mom_v = dict(ln_w=v_ln_w, w_in=v_w_in, s5_lam_re=v_s5_lam_re, s5_lam_im=v_s5_lam_im,
                 s5_log_step=v_s5_log_step, s5_b_re=v_s5_b_re, s5_b_im=v_s5_b_im,
                 s5_c_re=v_s5_c_re, s5_c_im=v_s5_c_im, s5_d=v_s5_d, s5_w_glu=v_s5_w_glu,
                 s5_w_up=v_s5_w_up, dn_conv_w=v_dn_conv_w, dn_a_log=v_dn_a_log,
                 dn_dt_bias=v_dn_dt_bias, dn_norm_w=v_dn_norm_w, dn_w_up=v_dn_w_up,
                 w_out=v_w_out, final_norm_w=v_final_norm_w)
    names = list(weights)

    l, d = x.shape[1], x.shape[2]
    d_s5 = d // 2
    groups = d_s5 // S5_GROUP
    nb = groups // S5_GPB
    d_dn = DN_HEADS * DN_HEAD_DIM
    w_in_cols = w_in.shape[2]
    d_in = N_DEV * w_in_cols
    off_ba_src = 2 * d_s5 + 4 * d_dn
    off_u, off_zs, off_qkv, off_zd = 0, d_s5, 2 * d_s5, 2 * d_s5 + 3 * d_dn
    off_ba = off_zd + d_dn
    n_main = off_ba + BA_PAD
    off_gs, off_gd = 0, d
    x2d, tgt2d = x[0], loss_target[0]
    my_index = 4 * lax.axis_index("x") + 2 * lax.axis_index("y") + lax.axis_index("c")

    g_win, g_conv = _all_gather([jnp.transpose(w_in[0]).astype(BF16), dn_conv_w[0]], name="gather_weights",
                                relayed=(0,))
    late_plan = _gather_plan(4)
    late_shards = [s5_w_glu[0].astype(BF16), s5_w_up[0].astype(BF16), dn_w_up[0].astype(BF16),
                   w_out[0].astype(BF16)]
    late_sems, late_shards, late_lands, late_token = _split_exchange_start(
        late_plan, late_shards, [(N_DEV,) + s.shape for s in late_shards], name="gather_late_start",
        after=g_conv)
    ba_end = off_ba_src + 2 * DN_HEADS
    w_full_t = g_win.reshape(d_in, d)
    w_gates_t = w_full_t[ba_end:]
    conv_full = jnp.transpose(g_conv, (1, 0, 2)).reshape(CONV_K, 3 * d_dn)

    lam_re, lam_im = s5_lam_re[0], s5_lam_im[0]
    log_step = s5_log_step[0].reshape(groups, 1)
    b_re = s5_b_re[0].reshape(groups * S5_STATE, S5_GROUP)
    b_im = s5_b_im[0].reshape(groups * S5_STATE, S5_GROUP)
    abar_re, abar_im, f_re, f_im = _s5_disc_fwd(lam_re, lam_im, log_step)
    f_re_col, f_im_col = f_re.reshape(-1, 1), f_im.reshape(-1, 1)
    bb_re, bb_im = _s5_bbar_fwd(f_re_col, f_im_col, b_re, b_im)

    def bb_blocks(t):
        t = t.reshape(nb, S5_GPB, S5_STATE, S5_GROUP).transpose(0, 1, 3, 2)
        return _block_diag(t).astype(BF16)

    def c_blocks(t):
        return _block_diag(t.reshape(nb, S5_GPB, S5_GROUP, S5_STATE)).astype(BF16)

    bbr, bbi = bb_blocks(bb_re), bb_blocks(bb_im)
    cbr, cbi = c_blocks(s5_c_re[0]), c_blocks(s5_c_im[0])
    ctr, cti = jnp.transpose(cbr, (0, 2, 1)), jnp.transpose(cbi, (0, 2, 1))
    a_re = abar_re.reshape(nb, 1, S5_GPB * S5_STATE)
    a_im = abar_im.reshape(nb, 1, S5_GPB * S5_STATE)

    h = _rms_fwd(x2d, ln_w)
    proj = _mm(h, w_full_t, tb=True, b_rows=n_main, tm=1024, tn=512, after=late_token, name="proj")
    proj_gates = _mm(h, w_gates_t, tb=True, tm=1024, tn=1024, name="proj_gates")
    y1, car_r, car_i, states_r, states_i = _s5_fwd(proj, bbr, bbi, a_re, a_im, ctr, cti, s5_d, d_s5)
    a_log_row = jnp.pad(dn_a_log, ((0, 0), (DN_HEADS, 128 - 2 * DN_HEADS)))
    dt_row = jnp.pad(dn_dt_bias, ((0, 0), (DN_HEADS, 128 - 2 * DN_HEADS)))
    qkv = _dn_prep_fwd(proj, off_qkv, conv_full)
    gb = _dn_gates_fwd(proj, off_ba, a_log_row, dt_row)
    o_dn, states, inverses = _dn_chunk_fwd(qkv, gb)

    g_glu, g_sup, g_dup, g_wout = _split_exchange_wait(late_plan, late_sems, late_shards, late_lands, o_dn,
                                                       name="gather_late_wait")
    wglu_full = g_glu.reshape(d_s5, d_s5)
    wsup_full = jnp.transpose(g_sup, (1, 0, 2)).reshape(d_s5, d)
    wdup_full = jnp.transpose(g_dup, (1, 0, 2)).reshape(d_dn, d)
    wout_full = g_wout.reshape(d, d)

    out_s = _s5_glu_fwd(y1, proj, off_zs, wglu_full)
    y_s = _mm(out_s, wsup_full, name="s5_up")
    out_d = _dn_out_fwd(o_dn, proj, off_zd, dn_norm_w)
    y_d = _mm(out_d, wdup_full, name="dn_up")

    mixed = _merge_fwd(proj_gates, off_gs, off_gd, y_s, y_d)
    branch = _mm(mixed, wout_full, name="w_out")
    dx2, dx2_bf, loss_dev, d_final_w = _final(x2d, branch, final_norm_w.reshape(1, d), tgt2d)

    g_wout_full = _mm(mixed, dx2_bf, ta=True, out_dtype=BF16, name="grad_w_out")
    dmixed = _mm(dx2_bf, wout_full, tb=True, name="d_mixed")
    dgs, dgd, dys, dyd = _merge_bwd(proj_gates, off_gs, off_gd, y_s, y_d, dmixed)

    g_dup_full = _mm(out_d, dyd, ta=True, out_dtype=BF16, name="grad_dn_up")
    dout_d = _mm(dyd, wdup_full, tb=True, name="d_out_d")
    do_dn, dzd, d_norm_w = _dn_out_bwd(o_dn, proj, off_zd, dn_norm_w, dout_d)
    dqkv, dgb_heads = _dn_chunk_bwd(qkv, gb, states, inverses, do_dn)
    dba, d_a_log_row, d_dt_row = _dn_gates_bwd(proj, off_ba, a_log_row, dt_row, dgb_heads)
    dqkv_pre, d_conv_full = _dn_prep_bwd(proj, off_qkv, conv_full, dqkv)

    g_sup_full = _mm(out_s, dys, ta=True, out_dtype=BF16, name="grad_s5_up")
    dout_s = _mm(dys, wsup_full, tb=True, name="d_out_s")
    dy1, dzs, g_glu_full = _s5_glu_bwd(y1, proj, off_zs, wglu_full, dout_s)

    def by_dest(t, axis=0):
        if axis == 1:
            return t.reshape(t.shape[0], 4, 2, t.shape[1] // N_DEV).transpose(1, 2, 0, 3)
        return t.reshape(4, 2, t.shape[0] // N_DEV, t.shape[1])

    core = lax.axis_index("c").astype(jnp.int32).reshape(1)
    chip = (2 * lax.axis_index("x") + lax.axis_index("y")).astype(jnp.int32).reshape(1)

    def chip_sums_of(which, parts, tag):
        from_sibling = _sibling_swap(parts, name="swap_grads_" + tag)
        return [_pair_sum(p, got, core, name="pair_sum_" + nm)
                for nm, p, got in zip(which, parts, from_sibling)]

    early = ["s5_w_glu", "s5_w_up", "dn_w_up", "w_out"]
    sums_a = chip_sums_of(early, [by_dest(g_glu_full.astype(BF16)), by_dest(g_sup_full, 1),
                                  by_dest(g_dup_full, 1), by_dest(g_wout_full)], "a")
    plan_a = _chip_slices_plan(len(sums_a))
    sems_a, src_a, land_a, token_a = _split_exchange_start(
        plan_a, sums_a, [t.shape for t in sums_a], name="exchange_start_a")

    (du, d_a_re, d_a_im, d_bbr, d_bbi, d_cbr, d_cbi, d_s5_d) = _s5_bwd(
        proj, dy1, bbr, bbi, a_re, a_im, cbr, cbi, s5_d + token_a[:1, :1], car_r, car_i,
        states_r, states_i)

    def from_bb_blocks(t):
        t = _diag_blocks(t, S5_GROUP, S5_STATE).transpose(0, 1, 3, 2)
        return t.reshape(groups * S5_STATE, S5_GROUP)

    d_f_re, d_f_im, d_b_re, d_b_im = _s5_bbar_bwd(f_re_col, f_im_col, b_re, b_im,
                                                 from_bb_blocks(d_bbr), from_bb_blocks(d_bbi))
    d_lam_re, d_lam_im, d_log_step = _s5_disc_bwd(
        lam_re, lam_im, log_step, d_a_re.reshape(groups, S5_STATE), d_a_im.reshape(groups, S5_STATE),
        d_f_re.reshape(groups, S5_STATE), d_f_im.reshape(groups, S5_STATE))
    d_c_re = _diag_blocks(d_cbr, S5_GROUP, S5_STATE).reshape(groups, S5_GROUP, S5_STATE)
    d_c_im = _diag_blocks(d_cbi, S5_GROUP, S5_STATE).reshape(groups, S5_GROUP, S5_STATE)

    dproj = jnp.concatenate([du, dzs, dqkv_pre, dzd, jnp.pad(dba, ((0, 0), (0, BA_PAD - 128)))], axis=1)
    dproj_gates = jnp.concatenate([dgs, dgd], axis=1)
    g_main_t = _mm(dproj, h, ta=True, out_dtype=BF16, tm=512, tn=d, name="grad_w_in")
    g_gates_t = _mm(dproj_gates, h, ta=True, out_dtype=BF16, tm=512, tn=d, name="grad_w_in_gates")
    g_win_full_t = jnp.concatenate([g_main_t[:ba_end], g_gates_t], axis=0)
    sums_b = chip_sums_of(["w_in"], [by_dest(g_win_full_t)], "b")
    plan_b = _chip_slices_plan(1)
    sems_b, src_b, land_b, token_b = _split_exchange_start(
        plan_b, sums_b, [t.shape for t in sums_b], name="exchange_start_b")
    dh_main = _mm(dproj, w_full_t, b_rows=n_main, tm=1024, tn=1024, tk=n_main // 4, after=token_b,
                  name="d_h_main")
    dh = _mm(dproj_gates, w_gates_t, tm=1024, tn=1024, tk=2048, addend=dh_main, name="d_h")
    grad_x, d_ln_w = _rms_bwd(x2d, ln_w, dh, dx2)
    big = ["w_in"] + early
    results = {}

    land_a = _split_exchange_wait(plan_a, sems_a, src_a, land_a, grad_x, name="exchange_wait_a")
    for nm, own, landed in zip(early, src_a, land_a):
        results[nm] = _adamw_exchanged(weights[nm][0], mom_m[nm][0], mom_v[nm][0], own, landed, chip,
                                       name="adamw_" + nm)

    small = [nm for nm in names if nm not in big]
    small_grads = dict(
        ln_w=d_ln_w, s5_lam_re=d_lam_re, s5_lam_im=d_lam_im, s5_log_step=d_log_step,
        s5_b_re=d_b_re, s5_b_im=d_b_im, s5_c_re=d_c_re, s5_c_im=d_c_im, s5_d=d_s5_d,
        dn_conv_w=d_conv_full, dn_a_log=d_a_log_row[:, DN_HEADS:2 * DN_HEADS],
        dn_dt_bias=d_dt_row[:, DN_HEADS:2 * DN_HEADS], dn_norm_w=d_norm_w, final_norm_w=d_final_w)
    (all_small,) = _all_gather([_pack_rows([small_grads[nm] for nm in small])], name="gather_small_grads",
                               after=results[early[-1]][0])
    summed = _slot_sum(all_small, name="sum_small_grads")
    full_shapes = [(CONV_K, 3 * d_dn) if nm == "dn_conv_w" else weights[nm].shape for nm in small]
    g_small = dict(zip(small, _unpack_rows(summed, full_shapes)))
    conv_cols = dn_conv_w.shape[2]
    g_small["dn_conv_w"] = lax.dynamic_slice_in_dim(
        g_small["dn_conv_w"], my_index * conv_cols, conv_cols, axis=1).reshape(dn_conv_w.shape)
    packed = [_pack_rows([t[nm] for nm in small]) for t in (weights, mom_m, mom_v, g_small)]
    small_out = _adamw(packed[0], packed[1], packed[2], packed[3][None], name="adamw_small")
    small_shapes = [weights[nm].shape for nm in small]
    for kind, packed_out in enumerate(small_out):
        for nm, val in zip(small, _unpack_rows(packed_out, small_shapes)):
            results.setdefault(nm, [None] * 4)[kind] = val

    (land_b,) = _split_exchange_wait(plan_b, sems_b, src_b, land_b, small_out[0], name="exchange_wait_b")
    res = _adamw_exchanged(jnp.transpose(w_in[0]), jnp.transpose(m_w_in[0]), jnp.transpose(v_w_in[0]),
                           src_b[0], land_b, chip, name="adamw_w_in")
    results["w_in"] = [jnp.transpose(t) for t in res]

    loss = lax.psum(loss_dev[0, 0], ("x", "y", "c"))
    outs = [loss, grad_x[None]]
    for kind in range(4):
        outs += [results[nm][kind].reshape(weights[nm].shape) for nm in names]
    return tuple(outs)
```

```python
import functools
import math

import jax
import jax.numpy as jnp
from jax import lax
from jax.experimental import pallas as pl
from jax.experimental.pallas import tpu as pltpu

F32 = jnp.float32
BF16 = jnp.bfloat16
HIGHEST = lax.Precision.HIGHEST
MESH = pl.DeviceIdType.MESH
N_DEV = 8

EPS = 1e-6
S5_GROUP = 16
S5_STATE = 64
S5_GPB = 8
S5_T = 1024
DN_HEADS = 8
DN_HEAD_DIM = 128
CHUNK = 64
DN_HEADS_PER_STEP = 8
DN_CHUNKS_PER_STEP = 4
CONV_K = 4
BA_PAD = 512

ADAM_LR = 0.001
ADAM_B1 = 0.9
ADAM_B2 = 0.999
ADAM_EPS = 1e-08
ADAM_WD = 0.01
ADAM_STEP = 10

VMEM_LIMIT_BYTES = 48 * 1024 * 1024
ROW_TILE = 256


def _cparams(*sem):
    return pltpu.CompilerParams(dimension_semantics=sem if sem else None,
                                vmem_limit_bytes=VMEM_LIMIT_BYTES)


@jax.custom_jvp
def _sigmoid(x):
    return 1.0 / (1.0 + jnp.exp(-x))


@_sigmoid.defjvp
def _sigmoid_jvp(primals, tangents):
    s = _sigmoid(primals[0])
    return s, tangents[0] * (s * (1.0 - s))


def _silu(x):
    return x * _sigmoid(x)


def _gelu(x):
    return 0.5 * x * (1.0 + jnp.tanh(0.7978845608028654 * (x + 0.044715 * x * x * x)))


def _softplus(x):
    return jnp.maximum(x, 0.0) + jnp.log(1.0 + jnp.exp(-jnp.abs(x)))


def _rmsnorm(x, w):
    return x * lax.rsqrt(jnp.mean(x * x, axis=-1, keepdims=True) + EPS) * w


def _dot(a, b, dims=((1,), (0,)), precision=None):
    return lax.dot_general(a, b, (dims, ((), ())), precision=precision,
                           preferred_element_type=F32)


def _bdot(a, b, dims=((1,), (0,))):
    return _dot(a.astype(BF16), b.astype(BF16), dims)


def _split_bf16(a):
    hi = a.astype(BF16)
    return hi, (a - hi.astype(F32)).astype(BF16)


def _dot3_dims(a, b, dims):
    ah, al = _split_bf16(a)
    bh, bl = _split_bf16(b)
    return _dot(ah, bh, dims) + (_dot(ah, bl, dims) + _dot(al, bh, dims))


@jax.custom_vjp
def _dot3(a, b):
    return _dot3_dims(a, b, ((1,), (0,)))


def _dot3_fwd(a, b):
    return _dot3(a, b), (a, b)


def _dot3_bwd(res, g):
    a, b = res
    return _dot3_dims(g, b, ((1,), (1,))), _dot3_dims(a, g, ((0,), (0,)))


_dot3.defvjp(_dot3_fwd, _dot3_bwd)


def _mm(a, b, *, ta=False, tb=False, out_dtype=F32, tm=512, tn=512, tk=None, after=None, b_rows=None,
        addend=None, name):
    k_dim, m_dim = (a.shape if ta else a.shape[::-1])
    b_rows = b.shape[0] if b_rows is None else b_rows
    n_dim = b_rows if tb else b.shape[1]
    assert (b.shape[1] if tb else b_rows) == k_dim and b_rows <= b.shape[0]
    tm, tn = min(tm, m_dim), min(tn, n_dim)
    tk = k_dim if tk is None else tk
    assert m_dim % tm == 0 and n_dim % tn == 0 and k_dim % tk == 0
    nk = k_dim // tk
    a_spec = (pl.BlockSpec((tk, tm), lambda i, j, k: (k, i)) if ta
              else pl.BlockSpec((tm, tk), lambda i, j, k: (i, k)))
    b_spec = (pl.BlockSpec((tn, tk), lambda i, j, k: (j, k)) if tb
              else pl.BlockSpec((tk, tn), lambda i, j, k: (k, j)))
    dims = ((0 if ta else 1,), (1 if tb else 0,))

    extras = ([after] if after is not None else []) + ([addend] if addend is not None else [])
    extra_specs = ([pl.BlockSpec((8, 128), lambda i, j, k: (0, 0))] if after is not None else []) + (
        [pl.BlockSpec((tm, tn), lambda i, j, k: (i, j))] if addend is not None else [])

    def body(a_ref, b_ref, *rest):
        o_ref, *scratch = rest[len(extras):]
        p = _bdot(a_ref[...], b_ref[...], dims)
        finish = (lambda v: v + rest[len(extras) - 1][...]) if addend is not None else (lambda v: v)
        if nk == 1:
            o_ref[...] = finish(p).astype(o_ref.dtype)
        else:
            acc = scratch[0]
            k = pl.program_id(2)

            @pl.when(k == 0)
            def _():
                acc[...] = p

            @pl.when(k > 0)
            def _():
                acc[...] += p

            @pl.when(k == nk - 1)
            def _():
                o_ref[...] = finish(acc[...]).astype(o_ref.dtype)

    return pl.pallas_call(
        body, name=name,
        out_shape=jax.ShapeDtypeStruct((m_dim, n_dim), out_dtype),
        grid=(m_dim // tm, n_dim // tn, nk),
        in_specs=[a_spec, b_spec] + extra_specs,
        out_specs=pl.BlockSpec((tm, tn), lambda i, j, k: (i, j)),
        scratch_shapes=[pltpu.VMEM((tm, tn), F32)] if nk > 1 else [],
        compiler_params=_cparams("parallel", "parallel", "arbitrary"),
    )(a, b, *extras)


def _rms_fwd(x, w):
    l, d = x.shape

    def body(x_ref, w_ref, h_ref):
        h_ref[...] = _rmsnorm(x_ref[...], w_ref[...]).astype(BF16)

    return pl.pallas_call(
        body, name="rms_fwd",
        out_shape=jax.ShapeDtypeStruct((l, d), BF16),
        grid=(l // ROW_TILE,),
        in_specs=[pl.BlockSpec((ROW_TILE, d), lambda i: (i, 0)),
                  pl.BlockSpec((1, d), lambda i: (0, 0))],
        out_specs=pl.BlockSpec((ROW_TILE, d), lambda i: (i, 0)),
        compiler_params=_cparams("parallel"),
    )(x, w)


def _rms_bwd(x, w, dh, dres):
    l, d = x.shape

    def body(x_ref, w_ref, dh_ref, dres_ref, dx_ref, dw_ref):
        _, vjp = jax.vjp(_rmsnorm, x_ref[...], w_ref[...])
        dx, dw = vjp(dh_ref[...])
        dx_ref[...] = dx + dres_ref[...]

        @pl.when(pl.program_id(0) == 0)
        def _():
            dw_ref[...] = jnp.zeros_like(dw_ref)

        dw_ref[...] += dw

    row = pl.BlockSpec((ROW_TILE, d), lambda i: (i, 0))
    vec = pl.BlockSpec((1, d), lambda i: (0, 0))
    return pl.pallas_call(
        body, name="rms_bwd",
        out_shape=(jax.ShapeDtypeStruct((l, d), F32), jax.ShapeDtypeStruct((1, d), F32)),
        grid=(l // ROW_TILE,),
        in_specs=[row, vec, row, row],
        out_specs=(row, vec),
        compiler_params=_cparams("arbitrary"),
    )(x, w, dh, dres)


def _final(x, r, fw, target):
    l, d = x.shape

    def per_row_loss(x2, w, tgt):
        err = _rmsnorm(x2, w) - tgt
        return 0.5 * jnp.mean(err * err, axis=-1, keepdims=True)

    def body(x_ref, r_ref, w_ref, t_ref, dx_ref, dxb_ref, loss_ref, dw_ref):
        x2 = x_ref[...] + r_ref[...]
        rows, vjp = jax.vjp(functools.partial(per_row_loss, tgt=t_ref[...]), x2, w_ref[...])
        dx2, dw = vjp(jnp.ones_like(rows))
        dx_ref[...] = dx2
        dxb_ref[...] = dx2.astype(BF16)

        @pl.when(pl.program_id(0) == 0)
        def _():
            dw_ref[...] = jnp.zeros_like(dw_ref)
            loss_ref[...] = jnp.zeros_like(loss_ref)

        dw_ref[...] += dw
        loss_ref[...] += jnp.sum(rows, axis=0, keepdims=True)

    row = pl.BlockSpec((ROW_TILE, d), lambda i: (i, 0))
    vec = pl.BlockSpec((1, d), lambda i: (0, 0))
    return pl.pallas_call(
        body, name="final_norm_loss",
        out_shape=(jax.ShapeDtypeStruct((l, d), F32), jax.ShapeDtypeStruct((l, d), BF16),
                   jax.ShapeDtypeStruct((1, 1), F32), jax.ShapeDtypeStruct((1, d), F32)),
        grid=(l // ROW_TILE,),
        in_specs=[row, row, vec, row],
        out_specs=(row, row, pl.BlockSpec((1, 1), lambda i: (0, 0)), vec),
        compiler_params=_cparams("arbitrary"),
    )(x, r, fw, target)


def _merge_fn(gs, gd, ys, yd):
    return _sigmoid(gs) * ys + _sigmoid(gd) * yd


def _merge_fwd(proj, off_gs, off_gd, ys, yd):
    l, d = ys.shape
    cw = min(1024, d)
    blk = lambda off: pl.BlockSpec((ROW_TILE, cw), lambda i, j: (i, off // cw + j))

    def body(gs_ref, gd_ref, ys_ref, yd_ref, o_ref):
        o_ref[...] = _merge_fn(gs_ref[...], gd_ref[...], ys_ref[...], yd_ref[...]).astype(BF16)

    return pl.pallas_call(
        body, name="merge_fwd",
        out_shape=jax.ShapeDtypeStruct((l, d), BF16),
        grid=(l // ROW_TILE, d // cw),
        in_specs=[blk(off_gs), blk(off_gd), blk(0), blk(0)],
        out_specs=blk(0),
        compiler_params=_cparams("parallel", "parallel"),
    )(proj, proj, ys, yd)


def _merge_bwd(proj, off_gs, off_gd, ys, yd, dmixed):
    l, d = ys.shape
    cw = min(1024, d)
    blk = lambda off: pl.BlockSpec((ROW_TILE, cw), lambda i, j: (i, off // cw + j))

    def body(gs_ref, gd_ref, ys_ref, yd_ref, dm_ref, dgs_ref, dgd_ref, dys_ref, dyd_ref):
        _, vjp = jax.vjp(_merge_fn, gs_ref[...], gd_ref[...], ys_ref[...], yd_ref[...])
        dgs, dgd, dys, dyd = vjp(dm_ref[...])
        dgs_ref[...] = dgs.astype(BF16)
        dgd_ref[...] = dgd.astype(BF16)
        dys_ref[...] = dys.astype(BF16)
        dyd_ref[...] = dyd.astype(BF16)

    out = jax.ShapeDtypeStruct((l, d), BF16)
    return pl.pallas_call(
        body, name="merge_bwd",
        out_shape=(out, out, out, out),
        grid=(l // ROW_TILE, d // cw),
        in_specs=[blk(off_gs), blk(off_gd), blk(0), blk(0), blk(0)],
        out_specs=(blk(0), blk(0), blk(0), blk(0)),
        compiler_params=_cparams("parallel", "parallel"),
    )(proj, proj, ys, yd, dmixed)


def _s5_disc_fn(lam_re, lam_im, log_step):
    step = jnp.exp(log_step)
    mag = jnp.exp(lam_re * step)
    abar_re = mag * jnp.cos(lam_im * step)
    abar_im = mag * jnp.sin(lam_im * step)
    den = lam_re * lam_re + lam_im * lam_im
    xr = abar_re - 1.0
    f_re = (xr * lam_re + abar_im * lam_im) / den
    f_im = (abar_im * lam_re - xr * lam_im) / den
    return abar_re, abar_im, f_re, f_im


def _s5_disc_fwd(lam_re, lam_im, log_step):
    g, p = lam_re.shape

    def body(lr_ref, li_ref, ls_ref, ar_ref, ai_ref, fr_ref, fi_ref):
        ar, ai, fr, fi = _s5_disc_fn(lr_ref[...], li_ref[...], ls_ref[...])
        ar_ref[...] = ar
        ai_ref[...] = ai
        fr_ref[...] = fr
        fi_ref[...] = fi

    o = jax.ShapeDtypeStruct((g, p), F32)
    return pl.pallas_call(body, name="s5_disc_fwd", out_shape=(o, o, o, o),
                          compiler_params=_cparams())(lam_re, lam_im, log_step)


def _s5_disc_bwd(lam_re, lam_im, log_step, dar, dai, dfr, dfi):
    g, p = lam_re.shape

    def body(lr_ref, li_ref, ls_ref, dar_ref, dai_ref, dfr_ref, dfi_ref, dlr_ref, dli_ref, dls_ref):
        _, vjp = jax.vjp(_s5_disc_fn, lr_ref[...], li_ref[...], ls_ref[...])
        dlr, dli, dls = vjp((dar_ref[...], dai_ref[...], dfr_ref[...], dfi_ref[...]))
        dlr_ref[...] = dlr
        dli_ref[...] = dli
        dls_ref[...] = dls

    o = jax.ShapeDtypeStruct((g, p), F32)
    return pl.pallas_call(body, name="s5_disc_bwd",
                          out_shape=(o, o, jax.ShapeDtypeStruct((g, 1), F32)),
                          compiler_params=_cparams())(lam_re, lam_im, log_step, dar, dai, dfr, dfi)


def _s5_bbar_fwd(f_re, f_im, b_re, b_im):
    n, c = b_re.shape

    def body(fr_ref, fi_ref, br_ref, bi_ref, or_ref, oi_ref):
        fr, fi, br, bi = fr_ref[...], fi_ref[...], br_ref[...], bi_ref[...]
        or_ref[...] = fr * br - fi * bi
        oi_ref[...] = fr * bi + fi * br

    o = jax.ShapeDtypeStruct((n, c), F32)
    return pl.pallas_call(body, name="s5_bbar_fwd", out_shape=(o, o),
                          compiler_params=_cparams())(f_re, f_im, b_re, b_im)


def _s5_bbar_bwd(f_re, f_im, b_re, b_im, dbr, dbi):
    n, c = b_re.shape

    def body(fr_ref, fi_ref, br_ref, bi_ref, dor_ref, doi_ref, dfr_ref, dfi_ref, dbr_ref, dbi_ref):
        fr, fi, br, bi = fr_ref[...], fi_ref[...], br_ref[...], bi_ref[...]
        dor, doi = dor_ref[...], doi_ref[...]
        dfr_ref[...] = jnp.sum(dor * br + doi * bi, axis=-1, keepdims=True)
        dfi_ref[...] = jnp.sum(doi * br - dor * bi, axis=-1, keepdims=True)
        dbr_ref[...] = fr * dor + fi * doi
        dbi_ref[...] = fr * doi - fi * dor

    col = jax.ShapeDtypeStruct((n, 1), F32)
    o = jax.ShapeDtypeStruct((n, c), F32)
    return pl.pallas_call(body, name="s5_bbar_bwd", out_shape=(col, col, o, o),
                          compiler_params=_cparams())(f_re, f_im, b_re, b_im, dbr, dbi)


SUBLANES = 8


def _scan_groups(xr, xi, ar, ai, reverse):
    t, n = xr.shape
    xr, xi = xr.reshape(t // SUBLANES, SUBLANES, n), xi.reshape(t // SUBLANES, SUBLANES, n)
    sub = lax.broadcasted_iota(jnp.int32, (1, SUBLANES, 1), 1)
    pr, pi = ar.reshape(1, 1, n), ai.reshape(1, 1, n)
    for sh in (1, 2, 4):
        keep = (sub < SUBLANES - sh) if reverse else (sub >= sh)
        cr, ci = jnp.where(keep, pr, 0.0), jnp.where(keep, pi, 0.0)
        shift = SUBLANES - sh if reverse else sh
        sr, si = pltpu.roll(xr, shift, 1), pltpu.roll(xi, shift, 1)
        xr, xi = xr + cr * sr - ci * si, xi + cr * si + ci * sr
        pr, pi = pr * pr - pi * pi, 2.0 * pr * pi
    return xr.reshape(t, n), xi.reshape(t, n)


def _scan_rows(xr, xi, ar, ai, cr, ci, sr_ref, si_ref, reverse):
    t, n = xr.shape
    xr, xi = _scan_groups(xr, xi, ar, ai, reverse)
    sr_ref[...] = xr
    si_ref[...] = xi
    sub = lax.broadcasted_iota(jnp.int32, (SUBLANES, n), 0)
    seed = sub == (SUBLANES - 1 if reverse else 0)
    pwr, pwi = _scan_groups(jnp.where(seed, ar, 0.0), jnp.where(seed, ai, 0.0), ar, ai, reverse)
    groups = range(t // SUBLANES)
    edge = 0 if reverse else SUBLANES - 1
    for g in (reversed(groups) if reverse else groups):
        rows = slice(g * SUBLANES, (g + 1) * SUBLANES)
        vr = sr_ref[rows, :] + (pwr * cr - pwi * ci)
        vi = si_ref[rows, :] + (pwr * ci + pwi * cr)
        sr_ref[rows, :] = vr
        si_ref[rows, :] = vi
        cr, ci = vr[edge:edge + 1, :], vi[edge:edge + 1, :]
    return cr, ci


def _s5_states(u_bf, bbr, bbi, ar, ai, cr, ci, sr_ref, si_ref):
    return _scan_rows(_dot(u_bf, bbr), _dot(u_bf, bbi), ar, ai, cr, ci, sr_ref, si_ref, reverse=False)


def _s5_fwd(proj, bbr, bbi, a_re, a_im, ctr, cti, d_skip, d_s5):
    l = proj.shape[0]
    nb, uc, ns = bbr.shape
    t = min(S5_T, l)
    nt = l // t

    def body(u_ref, bbr_ref, bbi_ref, ar_ref, ai_ref, ctr_ref, cti_ref, d_ref,
             y_ref, car_r_ref, car_i_ref, sr_ref, si_ref, cr, ci):
        @pl.when(pl.program_id(1) == 0)
        def _():
            cr[...] = jnp.zeros_like(cr)
            ci[...] = jnp.zeros_like(ci)

        car_r_ref[...] = cr[...]
        car_i_ref[...] = ci[...]
        u = u_ref[...]
        cr[...], ci[...] = _s5_states(u.astype(BF16), bbr_ref[...], bbi_ref[...], ar_ref[...],
                                      ai_ref[...], cr[...], ci[...], sr_ref, si_ref)
        y_ref[...] = (_bdot(sr_ref[...], ctr_ref[...]) - _bdot(si_ref[...], cti_ref[...])
                      + d_ref[...] * u)

    per_block = lambda shape: pl.BlockSpec((None,) + shape, lambda b, n: (b, 0, 0))
    return pl.pallas_call(
        body, name="s5_fwd",
        out_shape=(jax.ShapeDtypeStruct((l, d_s5), F32),
                   jax.ShapeDtypeStruct((nt, 1, nb * ns), F32),
                   jax.ShapeDtypeStruct((nt, 1, nb * ns), F32),
                   jax.ShapeDtypeStruct((l, nb * ns), F32),
                   jax.ShapeDtypeStruct((l, nb * ns), F32)),
        grid=(nb, nt),
        in_specs=[pl.BlockSpec((t, uc), lambda b, n: (n, b)),
                  per_block((uc, ns)), per_block((uc, ns)),
                  per_block((1, ns)), per_block((1, ns)),
                  per_block((ns, uc)), per_block((ns, uc)),
                  pl.BlockSpec((1, uc), lambda b, n: (0, b))],
        out_specs=(pl.BlockSpec((t, uc), lambda b, n: (n, b)),
                   pl.BlockSpec((None, 1, ns), lambda b, n: (n, 0, b)),
                   pl.BlockSpec((None, 1, ns), lambda b, n: (n, 0, b)),
                   pl.BlockSpec((t, ns), lambda b, n: (n, b)),
                   pl.BlockSpec((t, ns), lambda b, n: (n, b))),
        scratch_shapes=[pltpu.VMEM((1, ns), F32), pltpu.VMEM((1, ns), F32)],
        compiler_params=_cparams("parallel", "arbitrary"),
    )(proj, bbr, bbi, a_re, a_im, ctr, cti, d_skip)


def _s5_bwd(proj, dy, bbr, bbi, a_re, a_im, cbr, cbi, d_skip, car_r, car_i, states_r, states_i):
    l, d_s5 = dy.shape
    nb, uc, ns = bbr.shape
    t = min(S5_T, l)
    nt = l // t

    def body(u_ref, dy_ref, bbr_ref, bbi_ref, ar_ref, ai_ref, cbr_ref, cbi_ref, d_ref,
             car_r_ref, car_i_ref, sr_ref, si_ref,
             du_ref, dar_ref, dai_ref, dbbr_ref, dbbi_ref, dcbr_ref, dcbi_ref, dd_ref, gcr, gci,
             gr_ref, gi_ref):
        @pl.when(pl.program_id(1) == 0)
        def _():
            gcr[...] = jnp.zeros_like(gcr)
            gci[...] = jnp.zeros_like(gci)
            for ref in (dar_ref, dai_ref, dbbr_ref, dbbi_ref, dcbr_ref, dcbi_ref, dd_ref):
                ref[...] = jnp.zeros_like(ref)

        row = lax.broadcasted_iota(jnp.int32, (t, 1), 0)
        u, dy = u_ref[...], dy_ref[...]
        u_bf, dy_bf = u.astype(BF16), dy.astype(BF16)
        ar, ai = ar_ref[...], ai_ref[...]
        cr, ci = car_r_ref[...], car_i_ref[...]
        sr, si = sr_ref[...], si_ref[...]
        first = row == 0
        pr = jnp.where(first, cr, pltpu.roll(sr, 1, 0))
        pi = jnp.where(first, ci, pltpu.roll(si, 1, 0))
        gcr[...], gci[...] = _scan_rows(_dot(dy_bf, cbr_ref[...]), -_dot(dy_bf, cbi_ref[...]), ar, -ai,
                                        gcr[...], gci[...], gr_ref, gi_ref, reverse=True)
        gr, gi = gr_ref[...], gi_ref[...]
        dar_ref[...] += jnp.sum(gr * pr + gi * pi, axis=0, keepdims=True)
        dai_ref[...] += jnp.sum(gi * pr - gr * pi, axis=0, keepdims=True)
        gr_bf, gi_bf = gr.astype(BF16), gi.astype(BF16)
        tn = ((0,), (0,))
        dbbr_ref[...] += _dot(u_bf, gr_bf, tn)
        dbbi_ref[...] += _dot(u_bf, gi_bf, tn)
        dcbr_ref[...] += _dot(dy_bf, sr.astype(BF16), tn)
        dcbi_ref[...] -= _dot(dy_bf, si.astype(BF16), tn)
        nt_dims = ((1,), (1,))
        du = _dot(gr_bf, bbr_ref[...], nt_dims) + _dot(gi_bf, bbi_ref[...], nt_dims) + dy * d_ref[...]
        du_ref[...] = du.astype(BF16)
        dd_ref[...] += jnp.sum(dy * u, axis=0, keepdims=True)

    rev = lambda n: nt - 1 - n
    per_block = lambda shape: pl.BlockSpec((None,) + shape, lambda b, n: (b, 0, 0))
    acc = jax.ShapeDtypeStruct((nb, uc, ns), F32)
    vec = jax.ShapeDtypeStruct((nb, 1, ns), F32)
    return pl.pallas_call(
        body, name="s5_bwd",
        out_shape=(jax.ShapeDtypeStruct((l, d_s5), BF16), vec, vec, acc, acc, acc, acc,
                   jax.ShapeDtypeStruct((1, d_s5), F32)),
        grid=(nb, nt),
        in_specs=[pl.BlockSpec((t, uc), lambda b, n: (rev(n), b)),
                  pl.BlockSpec((t, uc), lambda b, n: (rev(n), b)),
                  per_block((uc, ns)), per_block((uc, ns)),
                  per_block((1, ns)), per_block((1, ns)),
                  per_block((uc, ns)), per_block((uc, ns)),
                  pl.BlockSpec((1, uc), lambda b, n: (0, b)),
                  pl.BlockSpec((None, 1, ns), lambda b, n: (rev(n), 0, b)),
                  pl.BlockSpec((None, 1, ns), lambda b, n: (rev(n), 0, b)),
                  pl.BlockSpec((t, ns), lambda b, n: (rev(n), b)),
                  pl.BlockSpec((t, ns), lambda b, n: (rev(n), b))],
        out_specs=(pl.BlockSpec((t, uc), lambda b, n: (rev(n), b)),
                   per_block((1, ns)), per_block((1, ns)),
                   per_block((uc, ns)), per_block((uc, ns)),
                   per_block((uc, ns)), per_block((uc, ns)),
                   pl.BlockSpec((1, uc), lambda b, n: (0, b))),
        scratch_shapes=[pltpu.VMEM((1, ns), F32), pltpu.VMEM((1, ns), F32)]
        + [pltpu.VMEM((t, ns), F32)] * 2,
        compiler_params=_cparams("parallel", "arbitrary"),
    )(proj, dy, bbr, bbi, a_re, a_im, cbr, cbi, d_skip, car_r, car_i, states_r, states_i)


def _s5_glu_fwd(y1, proj, off_z, wglu):
    l, d = y1.shape

    def body(y_ref, z_ref, w_ref, o_ref):
        y2 = _gelu(y_ref[...])
        y3 = y2 * _sigmoid(_bdot(y2, w_ref[...]))
        o_ref[...] = (y3 * _silu(z_ref[...])).astype(BF16)

    return pl.pallas_call(
        body, name="s5_glu_fwd",
        out_shape=jax.ShapeDtypeStruct((l, d), BF16),
        grid=(l // ROW_TILE,),
        in_specs=[pl.BlockSpec((ROW_TILE, d), lambda i: (i, 0)),
                  pl.BlockSpec((ROW_TILE, d), lambda i: (i, off_z // d)),
                  pl.BlockSpec((d, d), lambda i: (0, 0))],
        out_specs=pl.BlockSpec((ROW_TILE, d), lambda i: (i, 0)),
        compiler_params=_cparams("parallel"),
    )(y1, proj, wglu)


def _s5_glu_bwd(y1, proj, off_z, wglu, dout):
    l, d = y1.shape

    def body(y_ref, z_ref, w_ref, do_ref, dy_ref, dz_ref, dw_ref):
        y2, gelu_vjp = jax.vjp(_gelu, y_ref[...])
        z = z_ref[...]
        sz, silu_vjp = jax.vjp(_silu, z)
        y2_bf = y2.astype(BF16)
        sg = _sigmoid(_dot(y2_bf, w_ref[...]))
        dout = do_ref[...]
        dy3 = dout * sz
        dz_ref[...] = silu_vjp(dout * (y2 * sg))[0].astype(BF16)
        dgl = (dy3 * y2 * sg * (1.0 - sg)).astype(BF16)
        dy2 = dy3 * sg + _dot(dgl, w_ref[...], ((1,), (1,)))
        dy_ref[...] = gelu_vjp(dy2)[0]

        @pl.when(pl.program_id(0) == 0)
        def _():
            dw_ref[...] = jnp.zeros_like(dw_ref)

        dw_ref[...] += _dot(y2_bf, dgl, ((0,), (0,)))

    row = pl.BlockSpec((ROW_TILE, d), lambda i: (i, 0))
    full = pl.BlockSpec((d, d), lambda i: (0, 0))
    return pl.pallas_call(
        body, name="s5_glu_bwd",
        out_shape=(jax.ShapeDtypeStruct((l, d), F32), jax.ShapeDtypeStruct((l, d), BF16),
                   jax.ShapeDtypeStruct((d, d), F32)),
        grid=(l // ROW_TILE,),
        in_specs=[row, pl.BlockSpec((ROW_TILE, d), lambda i: (i, off_z // d)), full, row],
        out_specs=(row, row, full),
        compiler_params=_cparams("arbitrary"),
    )(y1, proj, wglu, dout)


def _shift_rows(x, k, back=False):
    if k == 0:
        return x
    t = x.shape[0]
    row = lax.broadcasted_iota(jnp.int32, (t, 1), 0)
    if back:
        return jnp.where(row < t - k, pltpu.roll(x, t - k, 0), 0.0)
    return jnp.where(row >= k, pltpu.roll(x, k, 0), 0.0)


def _dn_conv(x, w_ref):
    return sum(w_ref[CONV_K - 1 - k:CONV_K - k, :] * _shift_rows(x, k) for k in range(CONV_K))


def _dn_post_conv(c, j):
    y = _silu(c)
    n = y * lax.rsqrt(jnp.sum(y * y, axis=-1, keepdims=True) + EPS)
    n = n * jnp.where(j < DN_HEADS, DN_HEAD_DIM ** -0.5, 1.0)
    return jnp.where(j < 2 * DN_HEADS, n, y)


def _dn_prep_fwd(proj, off_qkv, conv_w):
    l = proj.shape[0]
    hd = DN_HEAD_DIM
    nblk = 3 * DN_HEADS

    def body(x_ref, w_ref, o_ref):
        o_ref[...] = _dn_post_conv(_dn_conv(x_ref[...], w_ref), pl.program_id(0))

    return pl.pallas_call(
        body, name="dn_prep_fwd",
        out_shape=jax.ShapeDtypeStruct((l, nblk * hd), F32),
        grid=(nblk,),
        in_specs=[pl.BlockSpec((l, hd), lambda j: (0, off_qkv // hd + j)),
                  pl.BlockSpec((CONV_K, hd), lambda j: (0, j))],
        out_specs=pl.BlockSpec((l, hd), lambda j: (0, j)),
        compiler_params=_cparams("parallel"),
    )(proj, conv_w)


def _dn_prep_bwd(proj, off_qkv, conv_w, dqkv):
    l = proj.shape[0]
    hd = DN_HEAD_DIM
    nblk = 3 * DN_HEADS

    def body(x_ref, w_ref, do_ref, dx_ref, dw_ref):
        x = x_ref[...]
        j = pl.program_id(0)
        _, vjp = jax.vjp(functools.partial(_dn_post_conv, j=j), _dn_conv(x, w_ref))
        dc = vjp(do_ref[...])[0]
        dx = sum(w_ref[CONV_K - 1 - k:CONV_K - k, :] * _shift_rows(dc, k, back=True)
                 for k in range(CONV_K))
        dx_ref[...] = dx.astype(BF16)
        for k in range(CONV_K):
            dw_ref[CONV_K - 1 - k:CONV_K - k, :] = jnp.sum(dc * _shift_rows(x, k), axis=0,
                                                           keepdims=True)

    return pl.pallas_call(
        body, name="dn_prep_bwd",
        out_shape=(jax.ShapeDtypeStruct((l, nblk * hd), BF16),
                   jax.ShapeDtypeStruct((CONV_K, nblk * hd), F32)),
        grid=(nblk,),
        in_specs=[pl.BlockSpec((l, hd), lambda j: (0, off_qkv // hd + j)),
                  pl.BlockSpec((CONV_K, hd), lambda j: (0, j)),
                  pl.BlockSpec((None, l, hd), lambda j: (j // DN_HEADS, 0, j % DN_HEADS))],
        out_specs=(pl.BlockSpec((l, hd), lambda j: (0, j)),
                   pl.BlockSpec((CONV_K, hd), lambda j: (0, j))),
        compiler_params=_cparams("parallel"),
    )(proj, conv_w, dqkv)


def _dn_gate_fn(ba, a_log_row, dt_row):
    lane = lax.broadcasted_iota(jnp.int32, ba.shape, 1)
    beta = _sigmoid(ba)
    g = -jnp.exp(a_log_row) * _softplus(ba + dt_row)
    return jnp.where(lane < DN_HEADS, beta, jnp.where(lane < 2 * DN_HEADS, g, 0.0))


def _dn_gates_fwd(proj, off_ba, a_log_row, dt_row):
    l = proj.shape[0]
    row = pl.BlockSpec((ROW_TILE, 128), lambda i: (i, off_ba // 128))
    vec = pl.BlockSpec((1, 128), lambda i: (0, 0))

    def body(ba_ref, al_ref, dt_ref, o_ref):
        o_ref[...] = _dn_gate_fn(ba_ref[...], al_ref[...], dt_ref[...])

    return pl.pallas_call(
        body, name="dn_gates_fwd",
        out_shape=jax.ShapeDtypeStruct((l, 128), F32),
        grid=(l // ROW_TILE,),
        in_specs=[row, vec, vec],
        out_specs=pl.BlockSpec((ROW_TILE, 128), lambda i: (i, 0)),
        compiler_params=_cparams("parallel"),
    )(proj, a_log_row, dt_row)


def _dn_gates_bwd(proj, off_ba, a_log_row, dt_row, dgb_heads):
    l = proj.shape[0]
    nh = dgb_heads.shape[0]
    row = pl.BlockSpec((ROW_TILE, 128), lambda i: (i, off_ba // 128))
    vec = pl.BlockSpec((1, 128), lambda i: (0, 0))

    def body(ba_ref, al_ref, dt_ref, dg_ref, dba_ref, dal_ref, ddt_ref):
        _, vjp = jax.vjp(_dn_gate_fn, ba_ref[...], al_ref[...], dt_ref[...])
        dgb = dg_ref[0]
        for h in range(1, nh):
            dgb = dgb + dg_ref[h]
        dba, dal, ddt = vjp(dgb)
        dba_ref[...] = dba.astype(BF16)

        @pl.when(pl.program_id(0) == 0)
        def _():
            dal_ref[...] = jnp.zeros_like(dal_ref)
            ddt_ref[...] = jnp.zeros_like(ddt_ref)

        dal_ref[...] += dal
        ddt_ref[...] += ddt

    return pl.pallas_call(
        body, name="dn_gates_bwd",
        out_shape=(jax.ShapeDtypeStruct((l, 128), BF16), jax.ShapeDtypeStruct((1, 128), F32),
                   jax.ShapeDtypeStruct((1, 128), F32)),
        grid=(l // ROW_TILE,),
        in_specs=[row, vec, vec, pl.BlockSpec((nh, ROW_TILE, 128), lambda i: (0, i, 0))],
        out_specs=(pl.BlockSpec((ROW_TILE, 128), lambda i: (i, 0)), vec, vec),
        compiler_params=_cparams("arbitrary"),
    )(proj, a_log_row, dt_row, dgb_heads)


@jax.custom_vjp
def _unit_lower_inverses(a_mats):
    c = a_mats[0].shape[0]
    eye = (lax.broadcasted_iota(jnp.int32, (c, c), 0) == lax.broadcasted_iota(jnp.int32, (c, c), 1)).astype(F32)
    t_inv = [eye - a for a in a_mats]
    power = a_mats
    for _ in range(int(math.log2(c)) - 1):
        power = [_bdot(p, p) for p in power]
        t_inv = [t + _bdot(t, p) for t, p in zip(t_inv, power)]
    return t_inv


def _unit_lower_inverses_fwd(a_mats):
    t_inv = _unit_lower_inverses(a_mats)
    return t_inv, t_inv


def _inverse_cotangents(t_inv, grads):
    right = [_dot3_dims(g, t, ((1,), (1,))) for g, t in zip(grads, t_inv)]
    return [-_dot3_dims(t, r, ((0,), (0,))) for t, r in zip(t_inv, right)]


_unit_lower_inverses.defvjp(_unit_lower_inverses_fwd,
                            lambda t_inv, grads: (_inverse_cotangents(t_inv, grads),))


@jax.custom_vjp
def _kept_inverses(a_mats, t_inv):
    return t_inv


_kept_inverses.defvjp(
    lambda a_mats, t_inv: (t_inv, t_inv),
    lambda t_inv, grads: (_inverse_cotangents(t_inv, grads), [jnp.zeros_like(t) for t in t_inv]))


def _dn_chunk_fn(states, qs, ks, vs, gb, heads, kept_inverses=None, return_inverses=False):
    c = qs[0].shape[0]
    each = lambda f, *lists: [f(*args) for args in zip(*lists)]
    lane = lax.broadcasted_iota(jnp.int32, gb.shape, 1)
    ri = lax.broadcasted_iota(jnp.int32, (c, c), 0)
    ci = lax.broadcasted_iota(jnp.int32, (c, c), 1)
    causal, strict = ri >= ci, ri > ci
    eye = (ri == ci).astype(F32)
    rowi = lax.broadcasted_iota(jnp.int32, (c, 1), 0)
    nt_dims = ((1,), (1,))
    hdot = functools.partial(_dot, precision=HIGHEST)

    pick = lambda m, at: jnp.sum(jnp.where(lane == at, m, 0.0), axis=1, keepdims=True)
    gb_cum = hdot(causal.astype(F32), gb)
    beta = [pick(gb, h) for h in heads]
    gc = [pick(gb_cum, h + DN_HEADS) for h in heads]
    gc_row = each(lambda g: jnp.sum(eye * g, axis=0, keepdims=True), gc)
    decay = each(lambda g, gr: jnp.where(causal, jnp.exp(jnp.where(causal, g - gr, 0.0)), 0.0),
                 gc, gc_row)
    kk = each(lambda k: _bdot(k, k, nt_dims), ks)
    a_mat = each(lambda b, m, dc: jnp.where(strict, b * m * dc, 0.0), beta, kk, decay)

    t_inv = (_unit_lower_inverses(a_mat) if kept_inverses is None
             else _kept_inverses(a_mat, kept_inverses))
    egc = each(jnp.exp, gc)
    u_c = each(lambda t, v, b: _dot3(t, v * b), t_inv, vs, beta)
    w_c = each(lambda t, k, b, e: _dot3(t, k * (b * e)), t_inv, ks, beta, egc)
    qk = each(lambda q, k, dc: _bdot(q, k, nt_dims) * dc, qs, ks, decay)
    g_end = each(lambda g: jnp.sum(jnp.where(rowi == c - 1, g, 0.0), axis=0, keepdims=True), gc)
    v_new = each(lambda u, w, s: u - _bdot(w, s), u_c, w_c, states)
    o = each(lambda q, e, s, m, vn: _bdot(q * e, s) + _bdot(m, vn), qs, egc, states, qk, v_new)
    new_states = each(
        lambda s, ge, k, g, vn: s * jnp.exp(ge) + _bdot(k * jnp.exp(ge - g), vn, ((0,), (0,))),
        states, g_end, ks, gc, v_new)
    return (o, new_states, t_inv) if return_inverses else (o, new_states)


def _dn_chunk_specs(order):
    hd, nh, hps, cps = DN_HEAD_DIM, DN_HEADS, DN_HEADS_PER_STEP, DN_CHUNKS_PER_STEP
    rows = cps * CHUNK
    qkv = lambda part: pl.BlockSpec((rows, hps * hd), lambda h, n: (order(n), part * (nh // hps) + h))
    gb = pl.BlockSpec((rows, 128), lambda h, n: (order(n), 0))
    state = pl.BlockSpec((hps, cps, hd, hd), lambda h, n: (h, order(n), 0, 0))
    inverse = pl.BlockSpec((hps, cps, CHUNK, CHUNK), lambda h, n: (h, order(n), 0, 0))
    return qkv, gb, state, inverse


def _dn_chunk_fwd(qkv, gb):
    l = qkv.shape[0]
    hd, nh, hps, cps = DN_HEAD_DIM, DN_HEADS, DN_HEADS_PER_STEP, DN_CHUNKS_PER_STEP
    n_chunks = l // CHUNK
    qkv_spec, gb_spec, state_spec, inverse_spec = _dn_chunk_specs(lambda n: n)

    def body(q_ref, k_ref, v_ref, gb_ref, o_ref, s_ref, t_ref, state):
        @pl.when(pl.program_id(1) == 0)
        def _():
            state[...] = jnp.zeros_like(state)

        cols = [slice(i * hd, (i + 1) * hd) for i in range(hps)]
        heads = [pl.program_id(0) * hps + i for i in range(hps)]
        states = [state[i] for i in range(hps)]
        for j in range(cps):
            rows = slice(j * CHUNK, (j + 1) * CHUNK)
            for i in range(hps):
                s_ref[i, j] = states[i]
            o, states, t_inv = _dn_chunk_fn(
                states, [q_ref[rows, cs] for cs in cols], [k_ref[rows, cs] for cs in cols],
                [v_ref[rows, cs] for cs in cols], gb_ref[rows, :], heads, return_inverses=True)
            for i in range(hps):
                o_ref[rows, cols[i]] = o[i]
                t_ref[i, j] = t_inv[i]
        for i in range(hps):
            state[i] = states[i]

    return pl.pallas_call(
        body, name="dn_chunk_fwd",
        out_shape=(jax.ShapeDtypeStruct((l, nh * hd), F32),
                   jax.ShapeDtypeStruct((nh, n_chunks, hd, hd), F32),
                   jax.ShapeDtypeStruct((nh, n_chunks, CHUNK, CHUNK), F32)),
        grid=(nh // hps, n_chunks // cps),
        in_specs=[qkv_spec(0), qkv_spec(1), qkv_spec(2), gb_spec],
        out_specs=(pl.BlockSpec((cps * CHUNK, hps * hd), lambda h, n: (n, h)), state_spec, inverse_spec),
        scratch_shapes=[pltpu.VMEM((hps, hd, hd), F32)],
        compiler_params=_cparams("parallel", "arbitrary"),
    )(qkv, qkv, qkv, gb)


def _dn_chunk_bwd(qkv, gb, states, inverses, do):
    l = qkv.shape[0]
    hd, nh, hps, cps = DN_HEAD_DIM, DN_HEADS, DN_HEADS_PER_STEP, DN_CHUNKS_PER_STEP
    n_steps = l // (cps * CHUNK)
    rev = lambda n: n_steps - 1 - n
    qkv_spec, gb_spec, state_spec, inverse_spec = _dn_chunk_specs(rev)

    def body(q_ref, k_ref, v_ref, gb_ref, s_ref, t_ref, do_ref, dqkv_ref, dgb_ref, dstate):
        @pl.when(pl.program_id(1) == 0)
        def _():
            dstate[...] = jnp.zeros_like(dstate)

        cols = [slice(i * hd, (i + 1) * hd) for i in range(hps)]
        heads = [pl.program_id(0) * hps + i for i in range(hps)]
        dstates = [dstate[i] for i in range(hps)]
        for j in reversed(range(cps)):
            rows = slice(j * CHUNK, (j + 1) * CHUNK)
            fn = functools.partial(_dn_chunk_fn, heads=heads, kept_inverses=[t_ref[i, j] for i in range(hps)])
            _, vjp = jax.vjp(fn, [s_ref[i, j] for i in range(hps)], [q_ref[rows, cs] for cs in cols],
                             [k_ref[rows, cs] for cs in cols], [v_ref[rows, cs] for cs in cols],
                             gb_ref[rows, :])
            dstates, dq, dk, dv, dgb = vjp(([do_ref[rows, cs] for cs in cols], dstates))
            for i in range(hps):
                dqkv_ref[0, rows, cols[i]] = dq[i]
                dqkv_ref[1, rows, cols[i]] = dk[i]
                dqkv_ref[2, rows, cols[i]] = dv[i]
            dgb_ref[rows, :] = dgb
        for i in range(hps):
            dstate[i] = dstates[i]

    head_out = pl.BlockSpec((cps * CHUNK, hps * hd), lambda h, n: (rev(n), h))
    return pl.pallas_call(
        body, name="dn_chunk_bwd",
        out_shape=(jax.ShapeDtypeStruct((3, l, nh * hd), F32),
                   jax.ShapeDtypeStruct((nh // hps, l, 128), F32)),
        grid=(nh // hps, n_steps),
        in_specs=[qkv_spec(0), qkv_spec(1), qkv_spec(2), gb_spec, state_spec, inverse_spec, head_out],
        out_specs=(pl.BlockSpec((3, cps * CHUNK, hps * hd), lambda h, n: (0, rev(n), h)),
                   pl.BlockSpec((None, cps * CHUNK, 128), lambda h, n: (h, rev(n), 0))),
        scratch_shapes=[pltpu.VMEM((hps, hd, hd), F32)],
        compiler_params=_cparams("parallel", "arbitrary"),
    )(qkv, qkv, qkv, gb, states, inverses, do)


def _dn_out_fn(o, z, w):
    return _rmsnorm(o, w) * _silu(z)


def _dn_out_fwd(o, proj, off_z, w):
    l, d = o.shape
    hd = DN_HEAD_DIM
    tr = min(4 * ROW_TILE, l)
    blk = lambda off: pl.BlockSpec((tr, hd), lambda i, h: (i, off // hd + h))

    def body(o_ref, z_ref, w_ref, out_ref):
        out_ref[...] = _dn_out_fn(o_ref[...], z_ref[...], w_ref[...]).astype(BF16)

    return pl.pallas_call(
        body, name="dn_out_fwd",
        out_shape=jax.ShapeDtypeStruct((l, d), BF16),
        grid=(l // tr, d // hd),
        in_specs=[blk(0), blk(off_z), pl.BlockSpec((1, hd), lambda i, h: (0, 0))],
        out_specs=blk(0),
        compiler_params=_cparams("parallel", "parallel"),
    )(o, proj, w)


def _dn_out_bwd(o, proj, off_z, w, dout):
    l, d = o.shape
    hd = DN_HEAD_DIM
    tr = min(4 * ROW_TILE, l)
    blk = lambda off: pl.BlockSpec((tr, hd), lambda i, h: (i, off // hd + h))
    vec = pl.BlockSpec((1, hd), lambda i, h: (0, 0))

    def body(o_ref, z_ref, w_ref, dout_ref, do_ref, dz_ref, dw_ref):
        _, vjp = jax.vjp(_dn_out_fn, o_ref[...], z_ref[...], w_ref[...])
        do, dz, dw = vjp(dout_ref[...])
        do_ref[...] = do
        dz_ref[...] = dz.astype(BF16)

        @pl.when((pl.program_id(0) == 0) & (pl.program_id(1) == 0))
        def _():
            dw_ref[...] = jnp.zeros_like(dw_ref)

        dw_ref[...] += dw

    return pl.pallas_call(
        body, name="dn_out_bwd",
        out_shape=(jax.ShapeDtypeStruct((l, d), F32), jax.ShapeDtypeStruct((l, d), BF16),
                   jax.ShapeDtypeStruct((1, hd), F32)),
        grid=(l // tr, d // hd),
        in_specs=[blk(0), blk(off_z), vec, blk(0)],
        out_specs=(blk(0), blk(0), vec),
        compiler_params=_cparams("arbitrary", "arbitrary"),
    )(o, proj, w, dout)


def _tile_2d(rows, cols, budget_bytes=1 << 20):
    for tr in (rows, 4096, 2048, 1024, 512, 256, 128, 64, 32, 16):
        if tr <= rows and rows % tr == 0 and tr * cols * 4 <= budget_bytes:
            return tr, cols
    for tc in (2048, 1024, 512, 256, 128):
        if cols % tc == 0 and rows * tc * 4 <= 2 * budget_bytes:
            return rows, tc
    raise ValueError((rows, cols))


def _adamw_update(g, w_ref, m_ref, v_ref, go_ref, d_ref, mo_ref, vo_ref):
    c1 = 1.0 / (1.0 - ADAM_B1 ** ADAM_STEP)
    c2 = 1.0 / (1.0 - ADAM_B2 ** ADAM_STEP)
    m_new = ADAM_B1 * m_ref[...] + (1.0 - ADAM_B1) * g
    v_new = ADAM_B2 * v_ref[...] + (1.0 - ADAM_B2) * (g * g)
    go_ref[...] = g
    mo_ref[...] = m_new
    vo_ref[...] = v_new
    d_ref[...] = -ADAM_LR * ((m_new * c1) / (jnp.sqrt(v_new * c2) + ADAM_EPS) + ADAM_WD * w_ref[...])


def _adamw(w, m, v, gslots, name):
    rows, cols = w.shape
    ns = gslots.shape[0]
    tr, tc = _tile_2d(rows, cols)

    def body(w_ref, m_ref, v_ref, g_ref, go_ref, d_ref, mo_ref, vo_ref):
        g = g_ref[0].astype(F32)
        for s in range(1, ns):
            g = g + g_ref[s].astype(F32)
        _adamw_update(g, w_ref, m_ref, v_ref, go_ref, d_ref, mo_ref, vo_ref)

    blk = pl.BlockSpec((tr, tc), lambda i, j: (i, j))
    o = jax.ShapeDtypeStruct((rows, cols), F32)
    return pl.pallas_call(
        body, name=name, out_shape=(o, o, o, o),
        grid=(rows // tr, cols // tc),
        in_specs=[blk, blk, blk, pl.BlockSpec((ns, tr, tc), lambda i, j: (0, i, j))],
        out_specs=(blk, blk, blk, blk),
        compiler_params=_cparams("parallel", "parallel"),
    )(w, m, v, gslots)


def _slot_sum(gslots, name):
    ns, rows, cols = gslots.shape
    tr, tc = _tile_2d(rows, cols)

    def body(g_ref, o_ref):
        g = g_ref[0]
        for s in range(1, ns):
            g = g + g_ref[s]
        o_ref[...] = g

    return pl.pallas_call(
        body, name=name, out_shape=jax.ShapeDtypeStruct((rows, cols), F32),
        grid=(rows // tr, cols // tc),
        in_specs=[pl.BlockSpec((ns, tr, tc), lambda i, j: (0, i, j))],
        out_specs=pl.BlockSpec((tr, tc), lambda i, j: (i, j)),
        compiler_params=_cparams("parallel", "parallel"),
    )(gslots)


HBM_SPEC = pl.BlockSpec(memory_space=pl.ANY)


def _all_gather(arrs, name, relayed=(), after=None):
    n = len(arrs)
    n_sems = 13
    n_in = n + (after is not None)

    def body(*refs):
        ins, outs = refs[:n], refs[n_in:n_in + n]
        send_sems, recv_sems, local_sems = refs[n_in + n:]
        x, y, c = lax.axis_index("x"), lax.axis_index("y"), lax.axis_index("c")
        me, sibling = (x, y, c), (x, y, 1 - c)
        chips = [(1 - x, y), (x, 1 - y), (1 - x, 1 - y)]
        index = lambda px, py, pc: 4 * px + 2 * py + pc

        def copy(a, k, block, to, src=None, cols=None):
            dst = outs[a].at[index(*block)]
            src = dst if src is None else src
            if cols is not None:
                dst, src = dst.at[:, cols], src.at[:, cols]
            return pltpu.make_async_remote_copy(
                src_ref=src, dst_ref=dst, send_sem=send_sems.at[a, k], recv_sem=recv_sems.at[a, k],
                device_id=to, device_id_type=MESH)

        mine = [pltpu.make_async_copy(ins[a], outs[a].at[index(*me)], local_sems.at[a])
                for a in range(n)]
        for cp in mine:
            cp.start()
        sends = []

        def start(cp):
            cp.start()
            sends.append(cp)

        halves = {a: (pl.ds(0, arrs[a].shape[1] // 2), pl.ds(arrs[a].shape[1] // 2, arrs[a].shape[1] // 2))
                  for a in relayed}
        near_x, near_y, far = [(*chip, c) for chip in chips]
        for a in range(n):
            start(copy(a, 0, me, sibling, src=ins[a]))
            if a in relayed:
                left, right = halves[a]
                for k, to, cols in ((1, near_x, left), (3, near_y, right), (2, near_x, right), (4, near_y, left)):
                    start(copy(a, k, me, to, src=ins[a], cols=cols))
            else:
                for j, chip in enumerate(chips):
                    start(copy(a, 1 + j, me, (*chip, c), src=ins[a]))
        for a in relayed:
            left, right = halves[a]
            for k, block, cols, onward, to_sibling in (
                    (1, near_x, left, (5, near_y), 7), (3, near_y, right, (6, near_x), 10),
                    (2, near_x, right, None, 8), (4, near_y, left, None, 9),
                    (5, far, left, None, 11), (6, far, right, None, 12)):
                copy(a, k, block, me, cols=cols).wait_recv()
                if onward is not None:
                    start(copy(a, onward[0], block, onward[1], cols=cols))
                start(copy(a, to_sibling, block, sibling, cols=cols))
        for j, chip in enumerate(chips):
            for a in range(n):
                if a not in relayed:
                    copy(a, 1 + j, (*chip, c), me).wait_recv()
                    start(copy(a, 4 + j, (*chip, c), sibling))
        for a in range(n):
            copy(a, 0, sibling, me).wait_recv()
            if a in relayed:
                left, right = halves[a]
                for k, chip, cols in ((7, chips[0], left), (8, chips[0], right), (9, chips[1], left),
                                      (10, chips[1], right), (11, chips[2], left), (12, chips[2], right)):
                    copy(a, k, (*chip, 1 - c), me, cols=cols).wait_recv()
            else:
                for j, chip in enumerate(chips):
                    copy(a, 4 + j, (*chip, 1 - c), me).wait_recv()
        for cp in sends:
            cp.wait_send()
        for cp in mine:
            cp.wait()

    return pl.pallas_call(
        body, name=name,
        out_shape=[jax.ShapeDtypeStruct((N_DEV,) + a.shape, a.dtype) for a in arrs],
        in_specs=[HBM_SPEC] * n_in, out_specs=[HBM_SPEC] * n,
        scratch_shapes=[pltpu.SemaphoreType.DMA((n, n_sems)), pltpu.SemaphoreType.DMA((n, n_sems)),
                        pltpu.SemaphoreType.DMA((n,))],
    )(*arrs, *([after] if after is not None else []))


def _sibling_swap(arrs, name):
    n = len(arrs)

    def body(*refs):
        ins, outs = refs[:n], refs[n:2 * n]
        send_sems, recv_sems = refs[2 * n:]
        x, y, c = lax.axis_index("x"), lax.axis_index("y"), lax.axis_index("c")
        copies = [pltpu.make_async_remote_copy(
            src_ref=ins[a].at[:, 1 - c], dst_ref=outs[a],
            send_sem=send_sems.at[a], recv_sem=recv_sems.at[a],
            device_id=(x, y, 1 - c), device_id_type=MESH) for a in range(n)]
        for cp in copies:
            cp.start()
        for cp in copies:
            cp.wait()

    return pl.pallas_call(
        body, name=name,
        out_shape=[jax.ShapeDtypeStruct(a.shape[:1] + a.shape[2:], a.dtype) for a in arrs],
        in_specs=[HBM_SPEC] * n, out_specs=[HBM_SPEC] * n,
        scratch_shapes=[pltpu.SemaphoreType.DMA((n,)), pltpu.SemaphoreType.DMA((n,))],
    )(*arrs)


def _pair_sum(mine, theirs, core, name):
    chips, _, rows, cols = mine.shape
    tr, tc = _tile_2d(rows, cols, budget_bytes=2 << 20)

    def body(core_ref, a_ref, b_ref, o_ref):
        o_ref[...] = (a_ref[...].astype(F32) + b_ref[...].astype(F32)).astype(o_ref.dtype)

    slab = pl.BlockSpec((None, tr, tc), lambda ch, i, j, core_ref: (ch, i, j))
    return pl.pallas_call(
        body, name=name, out_shape=jax.ShapeDtypeStruct((chips, rows, cols), mine.dtype),
        grid_spec=pltpu.PrefetchScalarGridSpec(
            num_scalar_prefetch=1, grid=(chips, rows // tr, cols // tc),
            in_specs=[pl.BlockSpec((None, None, tr, tc),
                                   lambda ch, i, j, core_ref: (ch, core_ref[0], i, j)), slab],
            out_specs=slab),
        compiler_params=_cparams("parallel", "parallel", "parallel"),
    )(core, mine, theirs)


HBM_ONLY = pl.BlockSpec(memory_space=pltpu.HBM)
SEM_SPEC = pl.BlockSpec(memory_space=pltpu.SEMAPHORE)
SPLIT_COPY_EFFECT = pltpu.SideEffectType.DATAFLOW_SIDE_EFFECTING


def _flip(v, bit):
    return 1 - v if bit else v


def _chip_slices_plan(n):
    def plan():
        x, y, c = lax.axis_index("x"), lax.axis_index("y"), lax.axis_index("c")
        copies = []
        for k in range(1, 4):
            px, py = _flip(x, k & 2), _flip(y, k & 1)
            copies += [(a, 2 * px + py, 2 * x + y, (px, py, c)) for a in range(n)]
        return copies
    return plan, 3 * n


def _gather_plan(n):
    def plan():
        x, y, c = lax.axis_index("x"), lax.axis_index("y"), lax.axis_index("c")
        copies = []
        for k in range(1, N_DEV):
            peer = (_flip(x, k & 4), _flip(y, k & 2), _flip(c, k & 1))
            copies += [(a, None, 4 * x + 2 * y + c, peer) for a in range(n)]
        return copies + [(a, None, 4 * x + 2 * y + c, None) for a in range(n)]
    return plan, 8 * n


def _planned_copies(plan, srcs, lands, send_sems, recv_sems):
    copies = []
    for i, (a, src_at, land_at, peer) in enumerate(plan()):
        src, dst = srcs[a] if src_at is None else srcs[a].at[src_at], lands[a].at[land_at]
        if peer is None:
            local = pltpu.make_async_copy(src, dst, send_sems[i])
            copies.append((local, local.wait))
        else:
            remote = pltpu.make_async_remote_copy(src_ref=src, dst_ref=dst, send_sem=send_sems[i],
                                                  recv_sem=recv_sems[i], device_id=peer, device_id_type=MESH)
            copies.append((remote, remote.wait))
    return copies


def _split_exchange_start(plan_and_count, arrs, land_shapes, name, after=None):
    plan, n_sems = plan_and_count
    n = len(arrs)

    n_in = 2 * n + (after is not None)

    def body(*refs):
        srcs, lands = refs[:n], refs[n:2 * n]
        send_sems, recv_sems = refs[n_in:n_in + n_sems], refs[n_in + n_sems:n_in + 2 * n_sems]
        token = refs[-1]
        for copy, _ in _planned_copies(plan, srcs, lands, send_sems, recv_sems):
            copy.start()
        token[...] = jnp.zeros_like(token)

    hbm = lambda a: pltpu.HBM(a.shape, a.dtype)
    operands = [pltpu.with_memory_space_constraint(a, pltpu.HBM) for a in arrs]
    operands += [pltpu.with_memory_space_constraint(lax.empty(shape, a.dtype), pltpu.HBM)
                 for a, shape in zip(arrs, land_shapes)]
    out = pl.pallas_call(
        body, name=name,
        out_shape=(*[pltpu.SemaphoreType.DMA(())] * (2 * n_sems),
                   *[hbm(a) for a in operands],
                   jax.ShapeDtypeStruct((8, 128), F32)),
        in_specs=[HBM_ONLY] * (2 * n) + [pl.BlockSpec(memory_space=pl.ANY)] * (after is not None),
        out_specs=(*[SEM_SPEC] * (2 * n_sems), *[HBM_ONLY] * (2 * n),
                   pl.BlockSpec(memory_space=pltpu.VMEM)),
        input_output_aliases={i: 2 * n_sems + i for i in range(2 * n)},
        compiler_params=pltpu.CompilerParams(has_side_effects=SPLIT_COPY_EFFECT),
    )(*operands, *([after] if after is not None else []))
    sems, rest = list(out[:2 * n_sems]), out[2 * n_sems:]
    return sems, list(rest[:n]), list(rest[n:2 * n]), rest[-1]


def _split_exchange_wait(plan_and_count, sems, srcs, lands, after, name):
    plan, n_sems = plan_and_count
    n = len(srcs)

    def body(*refs):
        src_refs, land_refs = refs[:n], refs[n:2 * n]
        send_sems, recv_sems = refs[2 * n:2 * n + n_sems], refs[2 * n + n_sems:2 * n + 2 * n_sems]
        for _, wait in _planned_copies(plan, src_refs, land_refs, send_sems, recv_sems):
            wait()

    hbm = lambda a: pltpu.HBM(a.shape, a.dtype)
    out = pl.pallas_call(
        body, name=name,
        out_shape=(*[hbm(a) for a in srcs], *[hbm(a) for a in lands]),
        in_specs=[HBM_ONLY] * (2 * n) + [SEM_SPEC] * (2 * n_sems) + [pl.BlockSpec(memory_space=pl.ANY)],
        out_specs=tuple([HBM_ONLY] * (2 * n)),
        input_output_aliases={i: i for i in range(2 * n)},
        compiler_params=pltpu.CompilerParams(has_side_effects=SPLIT_COPY_EFFECT),
    )(*srcs, *lands, *sems, after)
    return list(out[n:])


def _adamw_exchanged(w, m, v, own, landed, chip, name):
    rows, cols = w.shape
    tr, tc = _tile_2d(rows, cols)

    def body(chip_ref, w_ref, m_ref, v_ref, own_ref, l1_ref, l2_ref, l3_ref, go_ref, d_ref, mo_ref, vo_ref):
        g = own_ref[...].astype(F32)
        for ref in (l1_ref, l2_ref, l3_ref):
            g = g + ref[...].astype(F32)
        _adamw_update(g, w_ref, m_ref, v_ref, go_ref, d_ref, mo_ref, vo_ref)

    blk = pl.BlockSpec((tr, tc), lambda i, j, chip_ref: (i, j))
    slot = lambda k: pl.BlockSpec((None, tr, tc), lambda i, j, chip_ref: (chip_ref[0] ^ k, i, j))
    o = jax.ShapeDtypeStruct((rows, cols), F32)
    return pl.pallas_call(
        body, name=name, out_shape=(o, o, o, o),
        grid_spec=pltpu.PrefetchScalarGridSpec(
            num_scalar_prefetch=1, grid=(rows // tr, cols // tc),
            in_specs=[blk, blk, blk, slot(0), slot(1), slot(2), slot(3)],
            out_specs=(blk, blk, blk, blk)),
        compiler_params=_cparams("parallel", "parallel"),
    )(chip, w, m, v, own, landed, landed, landed)


def _block_diag(t):
    nb, gpb, r, c = t.shape
    eye = jnp.eye(gpb, dtype=t.dtype)
    return jnp.einsum("ngrc,gh->ngrhc", t, eye).reshape(nb, gpb * r, gpb * c)


def _diag_blocks(t, r, c):
    nb = t.shape[0]
    gpb = t.shape[1] // r
    t = t.reshape(nb, gpb, r, gpb, c)
    return jnp.einsum("ngrhc,gh->ngrc", t, jnp.eye(gpb, dtype=t.dtype))


def _pack_rows(parts):
    flat = jnp.concatenate([p.reshape(-1).astype(F32) for p in parts])
    pad = (-flat.shape[0]) % (256 * 128)
    return jnp.pad(flat, (0, pad)).reshape(-1, 128)


def _unpack_rows(packed, shapes):
    flat = packed.reshape(-1)
    out, at = [], 0
    for shape in shapes:
        size = math.prod(shape)
        out.append(flat[at:at + size].reshape(shape))
        at += size
    return out


def kernel(x, ln_w, w_in, s5_lam_re, s5_lam_im, s5_log_step, s5_b_re, s5_b_im, s5_c_re, s5_c_im, s5_d, s5_w_glu, s5_w_up, dn_conv_w, dn_a_log, dn_dt_bias, dn_norm_w, dn_w_up, w_out, final_norm_w, loss_target, m_ln_w, m_w_in, m_s5_lam_re, m_s5_lam_im, m_s5_log_step, m_s5_b_re, m_s5_b_im, m_s5_c_re, m_s5_c_im, m_s5_d, m_s5_w_glu, m_s5_w_up, m_dn_conv_w, m_dn_a_log, m_dn_dt_bias, m_dn_norm_w, m_dn_w_up, m_w_out, m_final_norm_w, v_ln_w, v_w_in, v_s5_lam_re, v_s5_lam_im, v_s5_log_step, v_s5_b_re, v_s5_b_im, v_s5_c_re, v_s5_c_im, v_s5_d, v_s5_w_glu, v_s5_w_up, v_dn_conv_w, v_dn_a_log, v_dn_dt_bias, v_dn_norm_w, v_dn_w_up, v_w_out, v_final_norm_w):
    weights = dict(ln_w=ln_w, w_in=w_in, s5_lam_re=s5_lam_re, s5_lam_im=s5_lam_im,
                   s5_log_step=s5_log_step, s5_b_re=s5_b_re, s5_b_im=s5_b_im, s5_c_re=s5_c_re,
                   s5_c_im=s5_c_im, s5_d=s5_d, s5_w_glu=s5_w_glu, s5_w_up=s5_w_up,
                   dn_conv_w=dn_conv_w, dn_a_log=dn_a_log, dn_dt_bias=dn_dt_bias,
                   dn_norm_w=dn_norm_w, dn_w_up=dn_w_up, w_out=w_out, final_norm_w=final_norm_w)
    mom_m = dict(ln_w=m_ln_w, w_in=m_w_in, s5_lam_re=m_s5_lam_re, s5_lam_im=m_s5_lam_im,
                 s5_log_step=m_s5_log_step, s5_b_re=m_s5_b_re, s5_b_im=m_s5_b_im,
                 s5_c_re=m_s5_c_re, s5_c_im=m_s5_c_im, s5_d=m_s5_d, s5_w_glu=m_s5_w_glu,
                 s5_w_up=m_s5_w_up, dn_conv_w=m_dn_conv_w, dn_a_log=m_dn_a_log,
                 dn_dt_bias=m_dn_dt_bias, dn_norm_w=m_dn_norm_w, dn_w_up=m_dn_w_up,
                 w_out=m_w_out, final_norm_w=m_final_norm_w)
    mom_v = dict(ln_w=v_ln_w, w_in=v_w_in, s5_lam_re=v_s5_lam_re, s5_lam_im=v_s5_lam_im,
                 s5_log_step=v_s5_log_step, s5_b_re=v_s5_b_re, s5_b_im=v_s5_b_im,
                 s5_c_re=v_s5_c_re, s5_c_im=v_s5_c_im, s5_d=v_s5_d, s5_w_glu=v_s5_w_glu,
                 s5_w_up=v_s5_w_up, dn_conv_w=v_dn_conv_w, dn_a_log=v_dn_a_log,
                 dn_dt_bias=v_dn_dt_bias, dn_norm_w=v_dn_norm_w, dn_w_up=v_dn_w_up,
                 w_out=v_w_out, final_norm_w=v_final_norm_w)
    names = list(weights)

    l, d = x.shape[1], x.shape[2]
    d_s5 = d // 2
    groups = d_s5 // S5_GROUP
    nb = groups // S5_GPB
    d_dn = DN_HEADS * DN_HEAD_DIM
    w_in_cols = w_in.shape[2]
    d_in = N_DEV * w_in_cols
    off_ba_src = 2 * d_s5 + 4 * d_dn
    off_u, off_zs, off_qkv, off_zd = 0, d_s5, 2 * d_s5, 2 * d_s5 + 3 * d_dn
    off_ba = off_zd + d_dn
    n_main = off_ba + BA_PAD
    off_gs, off_gd = 0, d
    x2d, tgt2d = x[0], loss_target[0]
    my_index = 4 * lax.axis_index("x") + 2 * lax.axis_index("y") + lax.axis_index("c")

    g_win, g_conv = _all_gather([jnp.transpose(w_in[0]).astype(BF16), dn_conv_w[0]], name="gather_weights",
                                relayed=(0,))
    late_plan = _gather_plan(4)
    late_shards = [s5_w_glu[0].astype(BF16), s5_w_up[0].astype(BF16), dn_w_up[0].astype(BF16),
                   w_out[0].astype(BF16)]
    late_sems, late_shards, late_lands, late_token = _split_exchange_start(
        late_plan, late_shards, [(N_DEV,) + s.shape for s in late_shards], name="gather_late_start",
        after=g_conv)
    ba_end = off_ba_src + 2 * DN_HEADS
    w_full_t = g_win.reshape(d_in, d)
    w_gates_t = w_full_t[ba_end:]
    conv_full = jnp.transpose(g_conv, (1, 0, 2)).reshape(CONV_K, 3 * d_dn)

    lam_re, lam_im = s5_lam_re[0], s5_lam_im[0]
    log_step = s5_log_step[0].reshape(groups, 1)
    b_re = s5_b_re[0].reshape(groups * S5_STATE, S5_GROUP)
    b_im = s5_b_im[0].reshape(groups * S5_STATE, S5_GROUP)
    abar_re, abar_im, f_re, f_im = _s5_disc_fwd(lam_re, lam_im, log_step)
    f_re_col, f_im_col = f_re.reshape(-1, 1), f_im.reshape(-1, 1)
    bb_re, bb_im = _s5_bbar_fwd(f_re_col, f_im_col, b_re, b_im)

    def bb_blocks(t):
        t = t.reshape(nb, S5_GPB, S5_STATE, S5_GROUP).transpose(0, 1, 3, 2)
        return _block_diag(t).astype(BF16)

    def c_blocks(t):
        return _block_diag(t.reshape(nb, S5_GPB, S5_GROUP, S5_STATE)).astype(BF16)

    bbr, bbi = bb_blocks(bb_re), bb_blocks(bb_im)
    cbr, cbi = c_blocks(s5_c_re[0]), c_blocks(s5_c_im[0])
    ctr, cti = jnp.transpose(cbr, (0, 2, 1)), jnp.transpose(cbi, (0, 2, 1))
    a_re = abar_re.reshape(nb, 1, S5_GPB * S5_STATE)
    a_im = abar_im.reshape(nb, 1, S5_GPB * S5_STATE)

    h = _rms_fwd(x2d, ln_w)
    proj = _mm(h, w_full_t, tb=True, b_rows=n_main, tm=1024, tn=512, after=late_token, name="proj")
    proj_gates = _mm(h, w_gates_t, tb=True, tm=1024, tn=1024, name="proj_gates")
    y1, car_r, car_i, states_r, states_i = _s5_fwd(proj, bbr, bbi, a_re, a_im, ctr, cti, s5_d, d_s5)
    a_log_row = jnp.pad(dn_a_log, ((0, 0), (DN_HEADS, 128 - 2 * DN_HEADS)))
    dt_row = jnp.pad(dn_dt_bias, ((0, 0), (DN_HEADS, 128 - 2 * DN_HEADS)))
    qkv = _dn_prep_fwd(proj, off_qkv, conv_full)
    gb = _dn_gates_fwd(proj, off_ba, a_log_row, dt_row)
    o_dn, states, inverses = _dn_chunk_fwd(qkv, gb)

    g_glu, g_sup, g_dup, g_wout = _split_exchange_wait(late_plan, late_sems, late_shards, late_lands, o_dn,
                                                       name="gather_late_wait")
    wglu_full = g_glu.reshape(d_s5, d_s5)
    wsup_full = jnp.transpose(g_sup, (1, 0, 2)).reshape(d_s5, d)
    wdup_full = jnp.transpose(g_dup, (1, 0, 2)).reshape(d_dn, d)
    wout_full = g_wout.reshape(d, d)

    out_s = _s5_glu_fwd(y1, proj, off_zs, wglu_full)
    y_s = _mm(out_s, wsup_full, name="s5_up")
    out_d = _dn_out_fwd(o_dn, proj, off_zd, dn_norm_w)
    y_d = _mm(out_d, wdup_full, name="dn_up")

    mixed = _merge_fwd(proj_gates, off_gs, off_gd, y_s, y_d)
    branch = _mm(mixed, wout_full, name="w_out")
    dx2, dx2_bf, loss_dev, d_final_w = _final(x2d, branch, final_norm_w.reshape(1, d), tgt2d)

    g_wout_full = _mm(mixed, dx2_bf, ta=True, out_dtype=BF16, name="grad_w_out")
    dmixed = _mm(dx2_bf, wout_full, tb=True, name="d_mixed")
    dgs, dgd, dys, dyd = _merge_bwd(proj_gates, off_gs, off_gd, y_s, y_d, dmixed)

    g_dup_full = _mm(out_d, dyd, ta=True, out_dtype=BF16, name="grad_dn_up")
    dout_d = _mm(dyd, wdup_full, tb=True, name="d_out_d")
    do_dn, dzd, d_norm_w = _dn_out_bwd(o_dn, proj, off_zd, dn_norm_w, dout_d)
    dqkv, dgb_heads = _dn_chunk_bwd(qkv, gb, states, inverses, do_dn)
    dba, d_a_log_row, d_dt_row = _dn_gates_bwd(proj, off_ba, a_log_row, dt_row, dgb_heads)
    dqkv_pre, d_conv_full = _dn_prep_bwd(proj, off_qkv, conv_full, dqkv)

    g_sup_full = _mm(out_s, dys, ta=True, out_dtype=BF16, name="grad_s5_up")
    dout_s = _mm(dys, wsup_full, tb=True, name="d_out_s")
    dy1, dzs, g_glu_full = _s5_glu_bwd(y1, proj, off_zs, wglu_full, dout_s)

    def by_dest(t, axis=0):
        if axis == 1:
            return t.reshape(t.shape[0], 4, 2, t.shape[1] // N_DEV).transpose(1, 2, 0, 3)
        return t.reshape(4, 2, t.shape[0] // N_DEV, t.shape[1])

    core = lax.axis_index("c").astype(jnp.int32).reshape(1)
    chip = (2 * lax.axis_index("x") + lax.axis_index("y")).astype(jnp.int32).reshape(1)

    def chip_sums_of(which, parts, tag):
        from_sibling = _sibling_swap(parts, name="swap_grads_" + tag)
        return [_pair_sum(p, got, core, name="pair_sum_" + nm)
                for nm, p, got in zip(which, parts, from_sibling)]

    early = ["s5_w_glu", "s5_w_up", "dn_w_up", "w_out"]
    sums_a = chip_sums_of(early, [by_dest(g_glu_full.astype(BF16)), by_dest(g_sup_full, 1),
                                  by_dest(g_dup_full, 1), by_dest(g_wout_full)], "a")
    plan_a = _chip_slices_plan(len(sums_a))
    sems_a, src_a, land_a, token_a = _split_exchange_start(
        plan_a, sums_a, [t.shape for t in sums_a], name="exchange_start_a")

    (du, d_a_re, d_a_im, d_bbr, d_bbi, d_cbr, d_cbi, d_s5_d) = _s5_bwd(
        proj, dy1, bbr, bbi, a_re, a_im, cbr, cbi, s5_d + token_a[:1, :1], car_r, car_i,
        states_r, states_i)

    def from_bb_blocks(t):
        t = _diag_blocks(t, S5_GROUP, S5_STATE).transpose(0, 1, 3, 2)
        return t.reshape(groups * S5_STATE, S5_GROUP)

    d_f_re, d_f_im, d_b_re, d_b_im = _s5_bbar_bwd(f_re_col, f_im_col, b_re, b_im,
                                                 from_bb_blocks(d_bbr), from_bb_blocks(d_bbi))
    d_lam_re, d_lam_im, d_log_step = _s5_disc_bwd(
        lam_re, lam_im, log_step, d_a_re.reshape(groups, S5_STATE), d_a_im.reshape(groups, S5_STATE),
        d_f_re.reshape(groups, S5_STATE), d_f_im.reshape(groups, S5_STATE))
    d_c_re = _diag_blocks(d_cbr, S5_GROUP, S5_STATE).reshape(groups, S5_GROUP, S5_STATE)
    d_c_im = _diag_blocks(d_cbi, S5_GROUP, S5_STATE).reshape(groups, S5_GROUP, S5_STATE)

    dproj = jnp.concatenate([du, dzs, dqkv_pre, dzd, jnp.pad(dba, ((0, 0), (0, BA_PAD - 128)))], axis=1)
    dproj_gates = jnp.concatenate([dgs, dgd], axis=1)
    g_main_t = _mm(dproj, h, ta=True, out_dtype=BF16, tm=512, tn=d, name="grad_w_in")
    g_gates_t = _mm(dproj_gates, h, ta=True, out_dtype=BF16, tm=512, tn=d, name="grad_w_in_gates")
    g_win_full_t = jnp.concatenate([g_main_t[:ba_end], g_gates_t], axis=0)
    sums_b = chip_sums_of(["w_in"], [by_dest(g_win_full_t)], "b")
    plan_b = _chip_slices_plan(1)
    sems_b, src_b, land_b, token_b = _split_exchange_start(
        plan_b, sums_b, [t.shape for t in sums_b], name="exchange_start_b")
    dh_main = _mm(dproj, w_full_t, b_rows=n_main, tm=1024, tn=1024, tk=n_main // 4, after=token_b,
                  name="d_h_main")
    dh = _mm(dproj_gates, w_gates_t, tm=1024, tn=1024, tk=2048, addend=dh_main, name="d_h")
    grad_x, d_ln_w = _rms_bwd(x2d, ln_w, dh, dx2)
    big = ["w_in"] + early
    results = {}

    land_a = _split_exchange_wait(plan_a, sems_a, src_a, land_a, grad_x, name="exchange_wait_a")
    for nm, own, landed in zip(early, src_a, land_a):
        results[nm] = _adamw_exchanged(weights[nm][0], mom_m[nm][0], mom_v[nm][0], own, landed, chip,
                                       name="adamw_" + nm)

    small = [nm for nm in names if nm not in big]
    small_grads = dict(
        ln_w=d_ln_w, s5_lam_re=d_lam_re, s5_lam_im=d_lam_im, s5_log_step=d_log_step,
        s5_b_re=d_b_re, s5_b_im=d_b_im, s5_c_re=d_c_re, s5_c_im=d_c_im, s5_d=d_s5_d,
        dn_conv_w=d_conv_full, dn_a_log=d_a_log_row[:, DN_HEADS:2 * DN_HEADS],
        dn_dt_bias=d_dt_row[:, DN_HEADS:2 * DN_HEADS], dn_norm_w=d_norm_w, final_norm_w=d_final_w)
    (all_small,) = _all_gather([_pack_rows([small_grads[nm] for nm in small])], name="gather_small_grads",
                               after=results[early[-1]][0])
    summed = _slot_sum(all_small, name="sum_small_grads")
    full_shapes = [(CONV_K, 3 * d_dn) if nm == "dn_conv_w" else weights[nm].shape for nm in small]
    g_small = dict(zip(small, _unpack_rows(summed, full_shapes)))
    conv_cols = dn_conv_w.shape[2]
    g_small["dn_conv_w"] = lax.dynamic_slice_in_dim(
        g_small["dn_conv_w"], my_index * conv_cols, conv_cols, axis=1).reshape(dn_conv_w.shape)
    packed = [_pack_rows([t[nm] for nm in small]) for t in (weights, mom_m, mom_v, g_small)]
    small_out = _adamw(packed[0], packed[1], packed[2], packed[3][None], name="adamw_small")
    small_shapes = [weights[nm].shape for nm in small]
    for kind, packed_out in enumerate(small_out):
        for nm, val in zip(small, _unpack_rows(packed_out, small_shapes)):
            results.setdefault(nm, [None] * 4)[kind] = val

    (land_b,) = _split_exchange_wait(plan_b, sems_b, src_b, land_b, small_out[0], name="exchange_wait_b")
    res = _adamw_exchanged(jnp.transpose(w_in[0]), jnp.transpose(m_w_in[0]), jnp.transpose(v_w_in[0]),
                           src_b[0], land_b, chip, name="adamw_w_in")
    results["w_in"] = [jnp.transpose(t) for t in res]

    loss = lax.psum(loss_dev[0, 0], ("x", "y", "c"))
    outs = [loss, grad_x[None]]
    for kind in range(4):
        outs += [results[nm][kind].reshape(weights[nm].shape) for nm in names]
    return tuple(outs)
```

```python
import functools
import math

import jax
import jax.numpy as jnp
from jax import lax
from jax.experimental import pallas as pl
from jax.experimental.pallas import tpu as pltpu

F32 = jnp.float32
BF16 = jnp.bfloat16
HIGHEST = lax.Precision.HIGHEST
MESH = pl.DeviceIdType.MESH
N_DEV = 8

EPS = 1e-6
S5_GROUP = 16
S5_STATE = 64
S5_GPB = 8
S5_T = 1024
DN_HEADS = 8
DN_HEAD_DIM = 128
CHUNK = 64
DN_HEADS_PER_STEP = 8
DN_CHUNKS_PER_STEP = 4
CONV_K = 4
BA_PAD = 512

ADAM_LR = 0.001
ADAM_B1 = 0.9
ADAM_B2 = 0.999
ADAM_EPS = 1e-08
ADAM_WD = 0.01
ADAM_STEP = 10

VMEM_LIMIT_BYTES = 48 * 1024 * 1024
ROW_TILE = 256


def _cparams(*sem):
    return pltpu.CompilerParams(dimension_semantics=sem if sem else None,
                                vmem_limit_bytes=VMEM_LIMIT_BYTES)


@jax.custom_jvp
def _sigmoid(x):
    return 1.0 / (1.0 + jnp.exp(-x))


@_sigmoid.defjvp
def _sigmoid_jvp(primals, tangents):
    s = _sigmoid(primals[0])
    return s, tangents[0] * (s * (1.0 - s))


def _silu(x):
    return x * _sigmoid(x)


def _gelu(x):
    return 0.5 * x * (1.0 + jnp.tanh(0.7978845608028654 * (x + 0.044715 * x * x * x)))


def _softplus(x):
    return jnp.maximum(x, 0.0) + jnp.log(1.0 + jnp.exp(-jnp.abs(x)))


def _rmsnorm(x, w):
    return x * lax.rsqrt(jnp.mean(x * x, axis=-1, keepdims=True) + EPS) * w


def _dot(a, b, dims=((1,), (0,)), precision=None):
    return lax.dot_general(a, b, (dims, ((), ())), precision=precision,
                           preferred_element_type=F32)


def _bdot(a, b, dims=((1,), (0,))):
    return _dot(a.astype(BF16), b.astype(BF16), dims)


def _split_bf16(a):
    hi = a.astype(BF16)
    return hi, (a - hi.astype(F32)).astype(BF16)


def _dot3_dims(a, b, dims):
    ah, al = _split_bf16(a)
    bh, bl = _split_bf16(b)
    return _dot(ah, bh, dims) + (_dot(ah, bl, dims) + _dot(al, bh, dims))


@jax.custom_vjp
def _dot3(a, b):
    return _dot3_dims(a, b, ((1,), (0,)))


def _dot3_fwd(a, b):
    return _dot3(a, b), (a, b)


def _dot3_bwd(res, g):
    a, b = res
    return _dot3_dims(g, b, ((1,), (1,))), _dot3_dims(a, g, ((0,), (0,)))


_dot3.defvjp(_dot3_fwd, _dot3_bwd)


def _mm(a, b, *, ta=False, tb=False, out_dtype=F32, tm=512, tn=512, tk=None, after=None, b_rows=None,
        addend=None, name):
    k_dim, m_dim = (a.shape if ta else a.shape[::-1])
    b_rows = b.shape[0] if b_rows is None else b_rows
    n_dim = b_rows if tb else b.shape[1]
    assert (b.shape[1] if tb else b_rows) == k_dim and b_rows <= b.shape[0]
    tm, tn = min(tm, m_dim), min(tn, n_dim)
    tk = k_dim if tk is None else tk
    assert m_dim % tm == 0 and n_dim % tn == 0 and k_dim % tk == 0
    nk = k_dim // tk
    a_spec = (pl.BlockSpec((tk, tm), lambda i, j, k: (k, i)) if ta
              else pl.BlockSpec((tm, tk), lambda i, j, k: (i, k)))
    b_spec = (pl.BlockSpec((tn, tk), lambda i, j, k: (j, k)) if tb
              else pl.BlockSpec((tk, tn), lambda i, j, k: (k, j)))
    dims = ((0 if ta else 1,), (1 if tb else 0,))

    extras = ([after] if after is not None else []) + ([addend] if addend is not None else [])
    extra_specs = ([pl.BlockSpec((8, 128), lambda i, j, k: (0, 0))] if after is not None else []) + (
        [pl.BlockSpec((tm, tn), lambda i, j, k: (i, j))] if addend is not None else [])

    def body(a_ref, b_ref, *rest):
        o_ref, *scratch = rest[len(extras):]
        p = _bdot(a_ref[...], b_ref[...], dims)
        finish = (lambda v: v + rest[len(extras) - 1][...]) if addend is not None else (lambda v: v)
        if nk == 1:
            o_ref[...] = finish(p).astype(o_ref.dtype)
        else:
            acc = scratch[0]
            k = pl.program_id(2)

            @pl.when(k == 0)
            def _():
                acc[...] = p

            @pl.when(k > 0)
            def _():
                acc[...] += p

            @pl.when(k == nk - 1)
            def _():
                o_ref[...] = finish(acc[...]).astype(o_ref.dtype)

    return pl.pallas_call(
        body, name=name,
        out_shape=jax.ShapeDtypeStruct((m_dim, n_dim), out_dtype),
        grid=(m_dim // tm, n_dim // tn, nk),
        in_specs=[a_spec, b_spec] + extra_specs,
        out_specs=pl.BlockSpec((tm, tn), lambda i, j, k: (i, j)),
        scratch_shapes=[pltpu.VMEM((tm, tn), F32)] if nk > 1 else [],
        compiler_params=_cparams("parallel", "parallel", "arbitrary"),
    )(a, b, *extras)


def _rms_fwd(x, w):
    l, d = x.shape

    def body(x_ref, w_ref, h_ref):
        h_ref[...] = _rmsnorm(x_ref[...], w_ref[...]).astype(BF16)

    return pl.pallas_call(
        body, name="rms_fwd",
        out_shape=jax.ShapeDtypeStruct((l, d), BF16),
        grid=(l // ROW_TILE,),
        in_specs=[pl.BlockSpec((ROW_TILE, d), lambda i: (i, 0)),
                  pl.BlockSpec((1, d), lambda i: (0, 0))],
        out_specs=pl.BlockSpec((ROW_TILE, d), lambda i: (i, 0)),
        compiler_params=_cparams("parallel"),
    )(x, w)


def _rms_bwd(x, w, dh, dres):
    l, d = x.shape

    def body(x_ref, w_ref, dh_ref, dres_ref, dx_ref, dw_ref):
        _, vjp = jax.vjp(_rmsnorm, x_ref[...], w_ref[...])
        dx, dw = vjp(dh_ref[...])
        dx_ref[...] = dx + dres_ref[...]

        @pl.when(pl.program_id(0) == 0)
        def _():
            dw_ref[...] = jnp.zeros_like(dw_ref)

        dw_ref[...] += dw

    row = pl.BlockSpec((ROW_TILE, d), lambda i: (i, 0))
    vec = pl.BlockSpec((1, d), lambda i: (0, 0))
    return pl.pallas_call(
        body, name="rms_bwd",
        out_shape=(jax.ShapeDtypeStruct((l, d), F32), jax.ShapeDtypeStruct((1, d), F32)),
        grid=(l // ROW_TILE,),
        in_specs=[row, vec, row, row],
        out_specs=(row, vec),
        compiler_params=_cparams("arbitrary"),
    )(x, w, dh, dres)


def _final(x, r, fw, target):
    l, d = x.shape

    def per_row_loss(x2, w, tgt):
        err = _rmsnorm(x2, w) - tgt
        return 0.5 * jnp.mean(err * err, axis=-1, keepdims=True)

    def body(x_ref, r_ref, w_ref, t_ref, dx_ref, dxb_ref, loss_ref, dw_ref):
        x2 = x_ref[...] + r_ref[...]
        rows, vjp = jax.vjp(functools.partial(per_row_loss, tgt=t_ref[...]), x2, w_ref[...])
        dx2, dw = vjp(jnp.ones_like(rows))
        dx_ref[...] = dx2
        dxb_ref[...] = dx2.astype(BF16)

        @pl.when(pl.program_id(0) == 0)
        def _():
            dw_ref[...] = jnp.zeros_like(dw_ref)
            loss_ref[...] = jnp.zeros_like(loss_ref)

        dw_ref[...] += dw
        loss_ref[...] += jnp.sum(rows, axis=0, keepdims=True)

    row = pl.BlockSpec((ROW_TILE, d), lambda i: (i, 0))
    vec = pl.BlockSpec((1, d), lambda i: (0, 0))
    return pl.pallas_call(
        body, name="final_norm_loss",
        out_shape=(jax.ShapeDtypeStruct((l, d), F32), jax.ShapeDtypeStruct((l, d), BF16),
                   jax.ShapeDtypeStruct((1, 1), F32), jax.ShapeDtypeStruct((1, d), F32)),
        grid=(l // ROW_TILE,),
        in_specs=[row, row, vec, row],
        out_specs=(row, row, pl.BlockSpec((1, 1), lambda i: (0, 0)), vec),
        compiler_params=_cparams("arbitrary"),
    )(x, r, fw, target)


def _merge_fn(gs, gd, ys, yd):
    return _sigmoid(gs) * ys + _sigmoid(gd) * yd


def _merge_fwd(proj, off_gs, off_gd, ys, yd):
    l, d = ys.shape
    cw = min(1024, d)
    blk = lambda off: pl.BlockSpec((ROW_TILE, cw), lambda i, j: (i, off // cw + j))

    def body(gs_ref, gd_ref, ys_ref, yd_ref, o_ref):
        o_ref[...] = _merge_fn(gs_ref[...], gd_ref[...], ys_ref[...], yd_ref[...]).astype(BF16)

    return pl.pallas_call(
        body, name="merge_fwd",
        out_shape=jax.ShapeDtypeStruct((l, d), BF16),
        grid=(l // ROW_TILE, d // cw),
        in_specs=[blk(off_gs), blk(off_gd), blk(0), blk(0)],
        out_specs=blk(0),
        compiler_params=_cparams("parallel", "parallel"),
    )(proj, proj, ys, yd)


def _merge_bwd(proj, off_gs, off_gd, ys, yd, dmixed):
    l, d = ys.shape
    cw = min(1024, d)
    blk = lambda off: pl.BlockSpec((ROW_TILE, cw), lambda i, j: (i, off // cw + j))

    def body(gs_ref, gd_ref, ys_ref, yd_ref, dm_ref, dgs_ref, dgd_ref, dys_ref, dyd_ref):
        _, vjp = jax.vjp(_merge_fn, gs_ref[...], gd_ref[...], ys_ref[...], yd_ref[...])
        dgs, dgd, dys, dyd = vjp(dm_ref[...])
        dgs_ref[...] = dgs.astype(BF16)
        dgd_ref[...] = dgd.astype(BF16)
        dys_ref[...] = dys.astype(BF16)
        dyd_ref[...] = dyd.astype(BF16)

    out = jax.ShapeDtypeStruct((l, d), BF16)
    return pl.pallas_call(
        body, name="merge_bwd",
        out_shape=(out, out, out, out),
        grid=(l // ROW_TILE, d // cw),
        in_specs=[blk(off_gs), blk(off_gd), blk(0), blk(0), blk(0)],
        out_specs=(blk(0), blk(0), blk(0), blk(0)),
        compiler_params=_cparams("parallel", "parallel"),
    )(proj, proj, ys, yd, dmixed)


def _s5_disc_fn(lam_re, lam_im, log_step):
    step = jnp.exp(log_step)
    mag = jnp.exp(lam_re * step)
    abar_re = mag * jnp.cos(lam_im * step)
    abar_im = mag * jnp.sin(lam_im * step)
    den = lam_re * lam_re + lam_im * lam_im
    xr = abar_re - 1.0
    f_re = (xr * lam_re + abar_im * lam_im) / den
    f_im = (abar_im * lam_re - xr * lam_im) / den
    return abar_re, abar_im, f_re, f_im


def _s5_disc_fwd(lam_re, lam_im, log_step):
    g, p = lam_re.shape

    def body(lr_ref, li_ref, ls_ref, ar_ref, ai_ref, fr_ref, fi_ref):
        ar, ai, fr, fi = _s5_disc_fn(lr_ref[...], li_ref[...], ls_ref[...])
        ar_ref[...] = ar
        ai_ref[...] = ai
        fr_ref[...] = fr
        fi_ref[...] = fi

    o = jax.ShapeDtypeStruct((g, p), F32)
    return pl.pallas_call(body, name="s5_disc_fwd", out_shape=(o, o, o, o),
                          compiler_params=_cparams())(lam_re, lam_im, log_step)


def _s5_disc_bwd(lam_re, lam_im, log_step, dar, dai, dfr, dfi):
    g, p = lam_re.shape

    def body(lr_ref, li_ref, ls_ref, dar_ref, dai_ref, dfr_ref, dfi_ref, dlr_ref, dli_ref, dls_ref):
        _, vjp = jax.vjp(_s5_disc_fn, lr_ref[...], li_ref[...], ls_ref[...])
        dlr, dli, dls = vjp((dar_ref[...], dai_ref[...], dfr_ref[...], dfi_ref[...]))
        dlr_ref[...] = dlr
        dli_ref[...] = dli
        dls_ref[...] = dls

    o = jax.ShapeDtypeStruct((g, p), F32)
    return pl.pallas_call(body, name="s5_disc_bwd",
                          out_shape=(o, o, jax.ShapeDtypeStruct((g, 1), F32)),
                          compiler_params=_cparams())(lam_re, lam_im, log_step, dar, dai, dfr, dfi)


def _s5_bbar_fwd(f_re, f_im, b_re, b_im):
    n, c = b_re.shape

    def body(fr_ref, fi_ref, br_ref, bi_ref, or_ref, oi_ref):
        fr, fi, br, bi = fr_ref[...], fi_ref[...], br_ref[...], bi_ref[...]
        or_ref[...] = fr * br - fi * bi
        oi_ref[...] = fr * bi + fi * br

    o = jax.ShapeDtypeStruct((n, c), F32)
    return pl.pallas_call(body, name="s5_bbar_fwd", out_shape=(o, o),
                          compiler_params=_cparams())(f_re, f_im, b_re, b_im)


def _s5_bbar_bwd(f_re, f_im, b_re, b_im, dbr, dbi):
    n, c = b_re.shape

    def body(fr_ref, fi_ref, br_ref, bi_ref, dor_ref, doi_ref, dfr_ref, dfi_ref, dbr_ref, dbi_ref):
        fr, fi, br, bi = fr_ref[...], fi_ref[...], br_ref[...], bi_ref[...]
        dor, doi = dor_ref[...], doi_ref[...]
        dfr_ref[...] = jnp.sum(dor * br + doi * bi, axis=-1, keepdims=True)
        dfi_ref[...] = jnp.sum(doi * br - dor * bi, axis=-1, keepdims=True)
        dbr_ref[...] = fr * dor + fi * doi
        dbi_ref[...] = fr * doi - fi * dor

    col = jax.ShapeDtypeStruct((n, 1), F32)
    o = jax.ShapeDtypeStruct((n, c), F32)
    return pl.pallas_call(body, name="s5_bbar_bwd", out_shape=(col, col, o, o),
                          compiler_params=_cparams())(f_re, f_im, b_re, b_im, dbr, dbi)


SUBLANES = 8


def _scan_groups(xr, xi, ar, ai, reverse):
    t, n = xr.shape
    xr, xi = xr.reshape(t // SUBLANES, SUBLANES, n), xi.reshape(t // SUBLANES, SUBLANES, n)
    sub = lax.broadcasted_iota(jnp.int32, (1, SUBLANES, 1), 1)
    pr, pi = ar.reshape(1, 1, n), ai.reshape(1, 1, n)
    for sh in (1, 2, 4):
        keep = (sub < SUBLANES - sh) if reverse else (sub >= sh)
        cr, ci = jnp.where(keep, pr, 0.0), jnp.where(keep, pi, 0.0)
        shift = SUBLANES - sh if reverse else sh
        sr, si = pltpu.roll(xr, shift, 1), pltpu.roll(xi, shift, 1)
        xr, xi = xr + cr * sr - ci * si, xi + cr * si + ci * sr
        pr, pi = pr * pr - pi * pi, 2.0 * pr * pi
    return xr.reshape(t, n), xi.reshape(t, n)


def _scan_rows(xr, xi, ar, ai, cr, ci, sr_ref, si_ref, reverse):
    t, n = xr.shape
    xr, xi = _scan_groups(xr, xi, ar, ai, reverse)
    sr_ref[...] = xr
    si_ref[...] = xi
    sub = lax.broadcasted_iota(jnp.int32, (SUBLANES, n), 0)
    seed = sub == (SUBLANES - 1 if reverse else 0)
    pwr, pwi = _scan_groups(jnp.where(seed, ar, 0.0), jnp.where(seed, ai, 0.0), ar, ai, reverse)
    groups = range(t // SUBLANES)
    edge = 0 if reverse else SUBLANES - 1
    for g in (reversed(groups) if reverse else groups):
        rows = slice(g * SUBLANES, (g + 1) * SUBLANES)
        vr = sr_ref[rows, :] + (pwr * cr - pwi * ci)
        vi = si_ref[rows, :] + (pwr * ci + pwi * cr)
        sr_ref[rows, :] = vr
        si_ref[rows, :] = vi
        cr, ci = vr[edge:edge + 1, :], vi[edge:edge + 1, :]
    return cr, ci


def _s5_states(u_bf, bbr, bbi, ar, ai, cr, ci, sr_ref, si_ref):
    return _scan_rows(_dot(u_bf, bbr), _dot(u_bf, bbi), ar, ai, cr, ci, sr_ref, si_ref, reverse=False)


def _s5_fwd(proj, bbr, bbi, a_re, a_im, ctr, cti, d_skip, d_s5):
    l = proj.shape[0]
    nb, uc, ns = bbr.shape
    t = min(S5_T, l)
    nt = l // t

    def body(u_ref, bbr_ref, bbi_ref, ar_ref, ai_ref, ctr_ref, cti_ref, d_ref,
             y_ref, car_r_ref, car_i_ref, sr_ref, si_ref, cr, ci):
        @pl.when(pl.program_id(1) == 0)
        def _():
            cr[...] = jnp.zeros_like(cr)
            ci[...] = jnp.zeros_like(ci)

        car_r_ref[...] = cr[...]
        car_i_ref[...] = ci[...]
        u = u_ref[...]
        cr[...], ci[...] = _s5_states(u.astype(BF16), bbr_ref[...], bbi_ref[...], ar_ref[...],
                                      ai_ref[...], cr[...], ci[...], sr_ref, si_ref)
        y_ref[...] = (_bdot(sr_ref[...], ctr_ref[...]) - _bdot(si_ref[...], cti_ref[...])
                      + d_ref[...] * u)

    per_block = lambda shape: pl.BlockSpec((None,) + shape, lambda b, n: (b, 0, 0))
    return pl.pallas_call(
        body, name="s5_fwd",
        out_shape=(jax.ShapeDtypeStruct((l, d_s5), F32),
                   jax.ShapeDtypeStruct((nt, 1, nb * ns), F32),
                   jax.ShapeDtypeStruct((nt, 1, nb * ns), F32),
                   jax.ShapeDtypeStruct((l, nb * ns), F32),
                   jax.ShapeDtypeStruct((l, nb * ns), F32)),
        grid=(nb, nt),
        in_specs=[pl.BlockSpec((t, uc), lambda b, n: (n, b)),
                  per_block((uc, ns)), per_block((uc, ns)),
                  per_block((1, ns)), per_block((1, ns)),
                  per_block((ns, uc)), per_block((ns, uc)),
                  pl.BlockSpec((1, uc), lambda b, n: (0, b))],
        out_specs=(pl.BlockSpec((t, uc), lambda b, n: (n, b)),
                   pl.BlockSpec((None, 1, ns), lambda b, n: (n, 0, b)),
                   pl.BlockSpec((None, 1, ns), lambda b, n: (n, 0, b)),
                   pl.BlockSpec((t, ns), lambda b, n: (n, b)),
                   pl.BlockSpec((t, ns), lambda b, n: (n, b))),
        scratch_shapes=[pltpu.VMEM((1, ns), F32), pltpu.VMEM((1, ns), F32)],
        compiler_params=_cparams("parallel", "arbitrary"),
    )(proj, bbr, bbi, a_re, a_im, ctr, cti, d_skip)


def _s5_bwd(proj, dy, bbr, bbi, a_re, a_im, cbr, cbi, d_skip, car_r, car_i, states_r, states_i):
    l, d_s5 = dy.shape
    nb, uc, ns = bbr.shape
    t = min(S5_T, l)
    nt = l // t

    def body(u_ref, dy_ref, bbr_ref, bbi_ref, ar_ref, ai_ref, cbr_ref, cbi_ref, d_ref,
             car_r_ref, car_i_ref, sr_ref, si_ref,
             du_ref, dar_ref, dai_ref, dbbr_ref, dbbi_ref, dcbr_ref, dcbi_ref, dd_ref, gcr, gci,
             gr_ref, gi_ref):
        @pl.when(pl.program_id(1) == 0)
        def _():
            gcr[...] = jnp.zeros_like(gcr)
            gci[...] = jnp.zeros_like(gci)
            for ref in (dar_ref, dai_ref, dbbr_ref, dbbi_ref, dcbr_ref, dcbi_ref, dd_ref):
                ref[...] = jnp.zeros_like(ref)

        row = lax.broadcasted_iota(jnp.int32, (t, 1), 0)
        u, dy = u_ref[...], dy_ref[...]
        u_bf, dy_bf = u.astype(BF16), dy.astype(BF16)
        ar, ai = ar_ref[...], ai_ref[...]
        cr, ci = car_r_ref[...], car_i_ref[...]
        sr, si = sr_ref[...], si_ref[...]
        first = row == 0
        pr = jnp.where(first, cr, pltpu.roll(sr, 1, 0))
        pi = jnp.where(first, ci, pltpu.roll(si, 1, 0))
        gcr[...], gci[...] = _scan_rows(_dot(dy_bf, cbr_ref[...]), -_dot(dy_bf, cbi_ref[...]), ar, -ai,
                                        gcr[...], gci[...], gr_ref, gi_ref, reverse=True)
        gr, gi = gr_ref[...], gi_ref[...]
        dar_ref[...] += jnp.sum(gr * pr + gi * pi, axis=0, keepdims=True)
        dai_ref[...] += jnp.sum(gi * pr - gr * pi, axis=0, keepdims=True)
        gr_bf, gi_bf = gr.astype(BF16), gi.astype(BF16)
        tn = ((0,), (0,))
        dbbr_ref[...] += _dot(u_bf, gr_bf, tn)
        dbbi_ref[...] += _dot(u_bf, gi_bf, tn)
        dcbr_ref[...] += _dot(dy_bf, sr.astype(BF16), tn)
        dcbi_ref[...] -= _dot(dy_bf, si.astype(BF16), tn)
        nt_dims = ((1,), (1,))
        du = _dot(gr_bf, bbr_ref[...], nt_dims) + _dot(gi_bf, bbi_ref[...], nt_dims) + dy * d_ref[...]
        du_ref[...] = du.astype(BF16)
        dd_ref[...] += jnp.sum(dy * u, axis=0, keepdims=True)

    rev = lambda n: nt - 1 - n
    per_block = lambda shape: pl.BlockSpec((None,) + shape, lambda b, n: (b, 0, 0))
    acc = jax.ShapeDtypeStruct((nb, uc, ns), F32)
    vec = jax.ShapeDtypeStruct((nb, 1, ns), F32)
    return pl.pallas_call(
        body, name="s5_bwd",
        out_shape=(jax.ShapeDtypeStruct((l, d_s5), BF16), vec, vec, acc, acc, acc, acc,
                   jax.ShapeDtypeStruct((1, d_s5), F32)),
        grid=(nb, nt),
        in_specs=[pl.BlockSpec((t, uc), lambda b, n: (rev(n), b)),
                  pl.BlockSpec((t, uc), lambda b, n: (rev(n), b)),
                  per_block((uc, ns)), per_block((uc, ns)),
                  per_block((1, ns)), per_block((1, ns)),
                  per_block((uc, ns)), per_block((uc, ns)),
                  pl.BlockSpec((1, uc), lambda b, n: (0, b)),
                  pl.BlockSpec((None, 1, ns), lambda b, n: (rev(n), 0, b)),
                  pl.BlockSpec((None, 1, ns), lambda b, n: (rev(n), 0, b)),
                  pl.BlockSpec((t, ns), lambda b, n: (rev(n), b)),
                  pl.BlockSpec((t, ns), lambda b, n: (rev(n), b))],
        out_specs=(pl.BlockSpec((t, uc), lambda b, n: (rev(n), b)),
                   per_block((1, ns)), per_block((1, ns)),
                   per_block((uc, ns)), per_block((uc, ns)),
                   per_block((uc, ns)), per_block((uc, ns)),
                   pl.BlockSpec((1, uc), lambda b, n: (0, b))),
        scratch_shapes=[pltpu.VMEM((1, ns), F32), pltpu.VMEM((1, ns), F32)]
        + [pltpu.VMEM((t, ns), F32)] * 2,
        compiler_params=_cparams("parallel", "arbitrary"),
    )(proj, dy, bbr, bbi, a_re, a_im, cbr, cbi, d_skip, car_r, car_i, states_r, states_i)


def _s5_glu_fwd(y1, proj, off_z, wglu):
    l, d = y1.shape

    def body(y_ref, z_ref, w_ref, o_ref):
        y2 = _gelu(y_ref[...])
        y3 = y2 * _sigmoid(_bdot(y2, w_ref[...]))
        o_ref[...] = (y3 * _silu(z_ref[...])).astype(BF16)

    return pl.pallas_call(
        body, name="s5_glu_fwd",
        out_shape=jax.ShapeDtypeStruct((l, d), BF16),
        grid=(l // ROW_TILE,),
        in_specs=[pl.BlockSpec((ROW_TILE, d), lambda i: (i, 0)),
                  pl.BlockSpec((ROW_TILE, d), lambda i: (i, off_z // d)),
                  pl.BlockSpec((d, d), lambda i: (0, 0))],
        out_specs=pl.BlockSpec((ROW_TILE, d), lambda i: (i, 0)),
        compiler_params=_cparams("parallel"),
    )(y1, proj, wglu)


def _s5_glu_bwd(y1, proj, off_z, wglu, dout):
    l, d = y1.shape

    def body(y_ref, z_ref, w_ref, do_ref, dy_ref, dz_ref, dw_ref):
        y2, gelu_vjp = jax.vjp(_gelu, y_ref[...])
        z = z_ref[...]
        sz, silu_vjp = jax.vjp(_silu, z)
        y2_bf = y2.astype(BF16)
        sg = _sigmoid(_dot(y2_bf, w_ref[...]))
        dout = do_ref[...]
        dy3 = dout * sz
        dz_ref[...] = silu_vjp(dout * (y2 * sg))[0].astype(BF16)
        dgl = (dy3 * y2 * sg * (1.0 - sg)).astype(BF16)
        dy2 = dy3 * sg + _dot(dgl, w_ref[...], ((1,), (1,)))
        dy_ref[...] = gelu_vjp(dy2)[0]

        @pl.when(pl.program_id(0) == 0)
        def _():
            dw_ref[...] = jnp.zeros_like(dw_ref)

        dw_ref[...] += _dot(y2_bf, dgl, ((0,), (0,)))

    row = pl.BlockSpec((ROW_TILE, d), lambda i: (i, 0))
    full = pl.BlockSpec((d, d), lambda i: (0, 0))
    return pl.pallas_call(
        body, name="s5_glu_bwd",
        out_shape=(jax.ShapeDtypeStruct((l, d), F32), jax.ShapeDtypeStruct((l, d), BF16),
                   jax.ShapeDtypeStruct((d, d), F32)),
        grid=(l // ROW_TILE,),
        in_specs=[row, pl.BlockSpec((ROW_TILE, d), lambda i: (i, off_z // d)), full, row],
        out_specs=(row, row, full),
        compiler_params=_cparams("arbitrary"),
    )(y1, proj, wglu, dout)


def _shift_rows(x, k, back=False):
    if k == 0:
        return x
    t = x.shape[0]
    row = lax.broadcasted_iota(jnp.int32, (t, 1), 0)
    if back:
        return jnp.where(row < t - k, pltpu.roll(x, t - k, 0), 0.0)
    return jnp.where(row >= k, pltpu.roll(x, k, 0), 0.0)


def _dn_conv(x, w_ref):
    return sum(w_ref[CONV_K - 1 - k:CONV_K - k, :] * _shift_rows(x, k) for k in range(CONV_K))


def _dn_post_conv(c, j):
    y = _silu(c)
    n = y * lax.rsqrt(jnp.sum(y * y, axis=-1, keepdims=True) + EPS)
    n = n * jnp.where(j < DN_HEADS, DN_HEAD_DIM ** -0.5, 1.0)
    return jnp.where(j < 2 * DN_HEADS, n, y)


def _dn_prep_fwd(proj, off_qkv, conv_w):
    l = proj.shape[0]
    hd = DN_HEAD_DIM
    nblk = 3 * DN_HEADS

    def body(x_ref, w_ref, o_ref):
        o_ref[...] = _dn_post_conv(_dn_conv(x_ref[...], w_ref), pl.program_id(0))

    return pl.pallas_call(
        body, name="dn_prep_fwd",
        out_shape=jax.ShapeDtypeStruct((l, nblk * hd), F32),
        grid=(nblk,),
        in_specs=[pl.BlockSpec((l, hd), lambda j: (0, off_qkv // hd + j)),
                  pl.BlockSpec((CONV_K, hd), lambda j: (0, j))],
        out_specs=pl.BlockSpec((l, hd), lambda j: (0, j)),
        compiler_params=_cparams("parallel"),
    )(proj, conv_w)


def _dn_prep_bwd(proj, off_qkv, conv_w, dqkv):
    l = proj.shape[0]
    hd = DN_HEAD_DIM
    nblk = 3 * DN_HEADS

    def body(x_ref, w_ref, do_ref, dx_ref, dw_ref):
        x = x_ref[...]
        j = pl.program_id(0)
        _, vjp = jax.vjp(functools.partial(_dn_post_conv, j=j), _dn_conv(x, w_ref))
        dc = vjp(do_ref[...])[0]
        dx = sum(w_ref[CONV_K - 1 - k:CONV_K - k, :] * _shift_rows(dc, k, back=True)
                 for k in range(CONV_K))
        dx_ref[...] = dx.astype(BF16)
        for k in range(CONV_K):
            dw_ref[CONV_K - 1 - k:CONV_K - k, :] = jnp.sum(dc * _shift_rows(x, k), axis=0,
                                                           keepdims=True)

    return pl.pallas_call(
        body, name="dn_prep_bwd",
        out_shape=(jax.ShapeDtypeStruct((l, nblk * hd), BF16),
                   jax.ShapeDtypeStruct((CONV_K, nblk * hd), F32)),
        grid=(nblk,),
        in_specs=[pl.BlockSpec((l, hd), lambda j: (0, off_qkv // hd + j)),
                  pl.BlockSpec((CONV_K, hd), lambda j: (0, j)),
                  pl.BlockSpec((None, l, hd), lambda j: (j // DN_HEADS, 0, j % DN_HEADS))],
        out_specs=(pl.BlockSpec((l, hd), lambda j: (0, j)),
                   pl.BlockSpec((CONV_K, hd), lambda j: (0, j))),
        compiler_params=_cparams("parallel"),
    )(proj, conv_w, dqkv)


def _dn_gate_fn(ba, a_log_row, dt_row):
    lane = lax.broadcasted_iota(jnp.int32, ba.shape, 1)
    beta = _sigmoid(ba)
    g = -jnp.exp(a_log_row) * _softplus(ba + dt_row)
    return jnp.where(lane < DN_HEADS, beta, jnp.where(lane < 2 * DN_HEADS, g, 0.0))


def _dn_gates_fwd(proj, off_ba, a_log_row, dt_row):
    l = proj.shape[0]
    row = pl.BlockSpec((ROW_TILE, 128), lambda i: (i, off_ba // 128))
    vec = pl.BlockSpec((1, 128), lambda i: (0, 0))

    def body(ba_ref, al_ref, dt_ref, o_ref):
        o_ref[...] = _dn_gate_fn(ba_ref[...], al_ref[...], dt_ref[...])

    return pl.pallas_call(
        body, name="dn_gates_fwd",
        out_shape=jax.ShapeDtypeStruct((l, 128), F32),
        grid=(l // ROW_TILE,),
        in_specs=[row, vec, vec],
        out_specs=pl.BlockSpec((ROW_TILE, 128), lambda i: (i, 0)),
        compiler_params=_cparams("parallel"),
    )(proj, a_log_row, dt_row)


def _dn_gates_bwd(proj, off_ba, a_log_row, dt_row, dgb_heads):
    l = proj.shape[0]
    nh = dgb_heads.shape[0]
    row = pl.BlockSpec((ROW_TILE, 128), lambda i: (i, off_ba // 128))
    vec = pl.BlockSpec((1, 128), lambda i: (0, 0))

    def body(ba_ref, al_ref, dt_ref, dg_ref, dba_ref, dal_ref, ddt_ref):
        _, vjp = jax.vjp(_dn_gate_fn, ba_ref[...], al_ref[...], dt_ref[...])
        dgb = dg_ref[0]
        for h in range(1, nh):
            dgb = dgb + dg_ref[h]
        dba, dal, ddt = vjp(dgb)
        dba_ref[...] = dba.astype(BF16)

        @pl.when(pl.program_id(0) == 0)
        def _():
            dal_ref[...] = jnp.zeros_like(dal_ref)
            ddt_ref[...] = jnp.zeros_like(ddt_ref)

        dal_ref[...] += dal
        ddt_ref[...] += ddt

    return pl.pallas_call(
        body, name="dn_gates_bwd",
        out_shape=(jax.ShapeDtypeStruct((l, 128), BF16), jax.ShapeDtypeStruct((1, 128), F32),
                   jax.ShapeDtypeStruct((1, 128), F32)),
        grid=(l // ROW_TILE,),
        in_specs=[row, vec, vec, pl.BlockSpec((nh, ROW_TILE, 128), lambda i: (0, i, 0))],
        out_specs=(pl.BlockSpec((ROW_TILE, 128), lambda i: (i, 0)), vec, vec),
        compiler_params=_cparams("arbitrary"),
    )(proj, a_log_row, dt_row, dgb_heads)


@jax.custom_vjp
def _unit_lower_inverses(a_mats):
    c = a_mats[0].shape[0]
    eye = (lax.broadcasted_iota(jnp.int32, (c, c), 0) == lax.broadcasted_iota(jnp.int32, (c, c), 1)).astype(F32)
    t_inv = [eye - a for a in a_mats]
    power = a_mats
    for _ in range(int(math.log2(c)) - 1):
        power = [_bdot(p, p) for p in power]
        t_inv = [t + _bdot(t, p) for t, p in zip(t_inv, power)]
    return t_inv


def _unit_lower_inverses_fwd(a_mats):
    t_inv = _unit_lower_inverses(a_mats)
    return t_inv, t_inv


def _inverse_cotangents(t_inv, grads):
    right = [_dot3_dims(g, t, ((1,), (1,))) for g, t in zip(grads, t_inv)]
    return [-_dot3_dims(t, r, ((0,), (0,))) for t, r in zip(t_inv, right)]


_unit_lower_inverses.defvjp(_unit_lower_inverses_fwd,
                            lambda t_inv, grads: (_inverse_cotangents(t_inv, grads),))


@jax.custom_vjp
def _kept_inverses(a_mats, t_inv):
    return t_inv


_kept_inverses.defvjp(
    lambda a_mats, t_inv: (t_inv, t_inv),
    lambda t_inv, grads: (_inverse_cotangents(t_inv, grads), [jnp.zeros_like(t) for t in t_inv]))


def _dn_chunk_fn(states, qs, ks, vs, gb, heads, kept_inverses=None, return_inverses=False):
    c = qs[0].shape[0]
    each = lambda f, *lists: [f(*args) for args in zip(*lists)]
    lane = lax.broadcasted_iota(jnp.int32, gb.shape, 1)
    ri = lax.broadcasted_iota(jnp.int32, (c, c), 0)
    ci = lax.broadcasted_iota(jnp.int32, (c, c), 1)
    causal, strict = ri >= ci, ri > ci
    eye = (ri == ci).astype(F32)
    rowi = lax.broadcasted_iota(jnp.int32, (c, 1), 0)
    nt_dims = ((1,), (1,))
    hdot = functools.partial(_dot, precision=HIGHEST)

    pick = lambda m, at: jnp.sum(jnp.where(lane == at, m, 0.0), axis=1, keepdims=True)
    gb_cum = hdot(causal.astype(F32), gb)
    beta = [pick(gb, h) for h in heads]
    gc = [pick(gb_cum, h + DN_HEADS) for h in heads]
    gc_row = each(lambda g: jnp.sum(eye * g, axis=0, keepdims=True), gc)
    decay = each(lambda g, gr: jnp.where(causal, jnp.exp(jnp.where(causal, g - gr, 0.0)), 0.0),
                 gc, gc_row)
    kk = each(lambda k: _bdot(k, k, nt_dims), ks)
    a_mat = each(lambda b, m, dc: jnp.where(strict, b * m * dc, 0.0), beta, kk, decay)

    t_inv = (_unit_lower_inverses(a_mat) if kept_inverses is None
             else _kept_inverses(a_mat, kept_inverses))
    egc = each(jnp.exp, gc)
    u_c = each(lambda t, v, b: _dot3(t, v * b), t_inv, vs, beta)
    w_c = each(lambda t, k, b, e: _dot3(t, k * (b * e)), t_inv, ks, beta, egc)
    qk = each(lambda q, k, dc: _bdot(q, k, nt_dims) * dc, qs, ks, decay)
    g_end = each(lambda g: jnp.sum(jnp.where(rowi == c - 1, g, 0.0), axis=0, keepdims=True), gc)
    v_new = each(lambda u, w, s: u - _bdot(w, s), u_c, w_c, states)
    o = each(lambda q, e, s, m, vn: _bdot(q * e, s) + _bdot(m, vn), qs, egc, states, qk, v_new)
    new_states = each(
        lambda s, ge, k, g, vn: s * jnp.exp(ge) + _bdot(k * jnp.exp(ge - g), vn, ((0,), (0,))),
        states, g_end, ks, gc, v_new)
    return (o, new_states, t_inv) if return_inverses else (o, new_states)


def _dn_chunk_specs(order):
    hd, nh, hps, cps = DN_HEAD_DIM, DN_HEADS, DN_HEADS_PER_STEP, DN_CHUNKS_PER_STEP
    rows = cps * CHUNK
    qkv = lambda part: pl.BlockSpec((rows, hps * hd), lambda h, n: (order(n), part * (nh // hps) + h))
    gb = pl.BlockSpec((rows, 128), lambda h, n: (order(n), 0))
    state = pl.BlockSpec((hps, cps, hd, hd), lambda h, n: (h, order(n), 0, 0))
    inverse = pl.BlockSpec((hps, cps, CHUNK, CHUNK), lambda h, n: (h, order(n), 0, 0))
    return qkv, gb, state, inverse


def _dn_chunk_fwd(qkv, gb):
    l = qkv.shape[0]
    hd, nh, hps, cps = DN_HEAD_DIM, DN_HEADS, DN_HEADS_PER_STEP, DN_CHUNKS_PER_STEP
    n_chunks = l // CHUNK
    qkv_spec, gb_spec, state_spec, inverse_spec = _dn_chunk_specs(lambda n: n)

    def body(q_ref, k_ref, v_ref, gb_ref, o_ref, s_ref, t_ref, state):
        @pl.when(pl.program_id(1) == 0)
        def _():
            state[...] = jnp.zeros_like(state)

        cols = [slice(i * hd, (i + 1) * hd) for i in range(hps)]
        heads = [pl.program_id(0) * hps + i for i in range(hps)]
        states = [state[i] for i in range(hps)]
        for j in range(cps):
            rows = slice(j * CHUNK, (j + 1) * CHUNK)
            for i in range(hps):
                s_ref[i, j] = states[i]
            o, states, t_inv = _dn_chunk_fn(
                states, [q_ref[rows, cs] for cs in cols], [k_ref[rows, cs] for cs in cols],
                [v_ref[rows, cs] for cs in cols], gb_ref[rows, :], heads, return_inverses=True)
            for i in range(hps):
                o_ref[rows, cols[i]] = o[i]
                t_ref[i, j] = t_inv[i]
        for i in range(hps):
            state[i] = states[i]

    return pl.pallas_call(
        body, name="dn_chunk_fwd",
        out_shape=(jax.ShapeDtypeStruct((l, nh * hd), F32),
                   jax.ShapeDtypeStruct((nh, n_chunks, hd, hd), F32),
                   jax.ShapeDtypeStruct((nh, n_chunks, CHUNK, CHUNK), F32)),
        grid=(nh // hps, n_chunks // cps),
        in_specs=[qkv_spec(0), qkv_spec(1), qkv_spec(2), gb_spec],
        out_specs=(pl.BlockSpec((cps * CHUNK, hps * hd), lambda h, n: (n, h)), state_spec, inverse_spec),
        scratch_shapes=[pltpu.VMEM((hps, hd, hd), F32)],
        compiler_params=_cparams("parallel", "arbitrary"),
    )(qkv, qkv, qkv, gb)


def _dn_chunk_bwd(qkv, gb, states, inverses, do):
    l = qkv.shape[0]
    hd, nh, hps, cps = DN_HEAD_DIM, DN_HEADS, DN_HEADS_PER_STEP, DN_CHUNKS_PER_STEP
    n_steps = l // (cps * CHUNK)
    rev = lambda n: n_steps - 1 - n
    qkv_spec, gb_spec, state_spec, inverse_spec = _dn_chunk_specs(rev)

    def body(q_ref, k_ref, v_ref, gb_ref, s_ref, t_ref, do_ref, dqkv_ref, dgb_ref, dstate):
        @pl.when(pl.program_id(1) == 0)
        def _():
            dstate[...] = jnp.zeros_like(dstate)

        cols = [slice(i * hd, (i + 1) * hd) for i in range(hps)]
        heads = [pl.program_id(0) * hps + i for i in range(hps)]
        dstates = [dstate[i] for i in range(hps)]
        for j in reversed(range(cps)):
            rows = slice(j * CHUNK, (j + 1) * CHUNK)
            fn = functools.partial(_dn_chunk_fn, heads=heads, kept_inverses=[t_ref[i, j] for i in range(hps)])
            _, vjp = jax.vjp(fn, [s_ref[i, j] for i in range(hps)], [q_ref[rows, cs] for cs in cols],
                             [k_ref[rows, cs] for cs in cols], [v_ref[rows, cs] for cs in cols],
                             gb_ref[rows, :])
            dstates, dq, dk, dv, dgb = vjp(([do_ref[rows, cs] for cs in cols], dstates))
            for i in range(hps):
                dqkv_ref[0, rows, cols[i]] = dq[i]
                dqkv_ref[1, rows, cols[i]] = dk[i]
                dqkv_ref[2, rows, cols[i]] = dv[i]
            dgb_ref[rows, :] = dgb
        for i in range(hps):
            dstate[i] = dstates[i]

    head_out = pl.BlockSpec((cps * CHUNK, hps * hd), lambda h, n: (rev(n), h))
    return pl.pallas_call(
        body, name="dn_chunk_bwd",
        out_shape=(jax.ShapeDtypeStruct((3, l, nh * hd), F32),
                   jax.ShapeDtypeStruct((nh // hps, l, 128), F32)),
        grid=(nh // hps, n_steps),
        in_specs=[qkv_spec(0), qkv_spec(1), qkv_spec(2), gb_spec, state_spec, inverse_spec, head_out],
        out_specs=(pl.BlockSpec((3, cps * CHUNK, hps * hd), lambda h, n: (0, rev(n), h)),
                   pl.BlockSpec((None, cps * CHUNK, 128), lambda h, n: (h, rev(n), 0))),
        scratch_shapes=[pltpu.VMEM((hps, hd, hd), F32)],
        compiler_params=_cparams("parallel", "arbitrary"),
    )(qkv, qkv, qkv, gb, states, inverses, do)


def _dn_out_fn(o, z, w):
    return _rmsnorm(o, w) * _silu(z)


def _dn_out_fwd(o, proj, off_z, w):
    l, d = o.shape
    hd = DN_HEAD_DIM
    tr = min(4 * ROW_TILE, l)
    blk = lambda off: pl.BlockSpec((tr, hd), lambda i, h: (i, off // hd + h))

    def body(o_ref, z_ref, w_ref, out_ref):
        out_ref[...] = _dn_out_fn(o_ref[...], z_ref[...], w_ref[...]).astype(BF16)

    return pl.pallas_call(
        body, name="dn_out_fwd",
        out_shape=jax.ShapeDtypeStruct((l, d), BF16),
        grid=(l // tr, d // hd),
        in_specs=[blk(0), blk(off_z), pl.BlockSpec((1, hd), lambda i, h: (0, 0))],
        out_specs=blk(0),
        compiler_params=_cparams("parallel", "parallel"),
    )(o, proj, w)


def _dn_out_bwd(o, proj, off_z, w, dout):
    l, d = o.shape
    hd = DN_HEAD_DIM
    tr = min(4 * ROW_TILE, l)
    blk = lambda off: pl.BlockSpec((tr, hd), lambda i, h: (i, off // hd + h))
    vec = pl.BlockSpec((1, hd), lambda i, h: (0, 0))

    def body(o_ref, z_ref, w_ref, dout_ref, do_ref, dz_ref, dw_ref):
        _, vjp = jax.vjp(_dn_out_fn, o_ref[...], z_ref[...], w_ref[...])
        do, dz, dw = vjp(dout_ref[...])
        do_ref[...] = do
        dz_ref[...] = dz.astype(BF16)

        @pl.when((pl.program_id(0) == 0) & (pl.program_id(1) == 0))
        def _():
            dw_ref[...] = jnp.zeros_like(dw_ref)

        dw_ref[...] += dw

    return pl.pallas_call(
        body, name="dn_out_bwd",
        out_shape=(jax.ShapeDtypeStruct((l, d), F32), jax.ShapeDtypeStruct((l, d), BF16),
                   jax.ShapeDtypeStruct((1, hd), F32)),
        grid=(l // tr, d // hd),
        in_specs=[blk(0), blk(off_z), vec, blk(0)],
        out_specs=(blk(0), blk(0), vec),
        compiler_params=_cparams("arbitrary", "arbitrary"),
    )(o, proj, w, dout)


def _tile_2d(rows, cols, budget_bytes=1 << 20):
    for tr in (rows, 4096, 2048, 1024, 512, 256, 128, 64, 32, 16):
        if tr <= rows and rows % tr == 0 and tr * cols * 4 <= budget_bytes:
            return tr, cols
    for tc in (2048, 1024, 512, 256, 128):
        if cols % tc == 0 and rows * tc * 4 <= 2 * budget_bytes:
            return rows, tc
    raise ValueError((rows, cols))


def _adamw_update(g, w_ref, m_ref, v_ref, go_ref, d_ref, mo_ref, vo_ref):
    c1 = 1.0 / (1.0 - ADAM_B1 ** ADAM_STEP)
    c2 = 1.0 / (1.0 - ADAM_B2 ** ADAM_STEP)
    m_new = ADAM_B1 * m_ref[...] + (1.0 - ADAM_B1) * g
    v_new = ADAM_B2 * v_ref[...] + (1.0 - ADAM_B2) * (g * g)
    go_ref[...] = g
    mo_ref[...] = m_new
    vo_ref[...] = v_new
    d_ref[...] = -ADAM_LR * ((m_new * c1) / (jnp.sqrt(v_new * c2) + ADAM_EPS) + ADAM_WD * w_ref[...])


def _adamw(w, m, v, gslots, name):
    rows, cols = w.shape
    ns = gslots.shape[0]
    tr, tc = _tile_2d(rows, cols)

    def body(w_ref, m_ref, v_ref, g_ref, go_ref, d_ref, mo_ref, vo_ref):
        g = g_ref[0].astype(F32)
        for s in range(1, ns):
            g = g + g_ref[s].astype(F32)
        _adamw_update(g, w_ref, m_ref, v_ref, go_ref, d_ref, mo_ref, vo_ref)

    blk = pl.BlockSpec((tr, tc), lambda i, j: (i, j))
    o = jax.ShapeDtypeStruct((rows, cols), F32)
    return pl.pallas_call(
        body, name=name, out_shape=(o, o, o, o),
        grid=(rows // tr, cols // tc),
        in_specs=[blk, blk, blk, pl.BlockSpec((ns, tr, tc), lambda i, j: (0, i, j))],
        out_specs=(blk, blk, blk, blk),
        compiler_params=_cparams("parallel", "parallel"),
    )(w, m, v, gslots)


def _slot_sum(gslots, name):
    ns, rows, cols = gslots.shape
    tr, tc = _tile_2d(rows, cols)

    def body(g_ref, o_ref):
        g = g_ref[0]
        for s in range(1, ns):
            g = g + g_ref[s]
        o_ref[...] = g

    return pl.pallas_call(
        body, name=name, out_shape=jax.ShapeDtypeStruct((rows, cols), F32),
        grid=(rows // tr, cols // tc),
        in_specs=[pl.BlockSpec((ns, tr, tc), lambda i, j: (0, i, j))],
        out_specs=pl.BlockSpec((tr, tc), lambda i, j: (i, j)),
        compiler_params=_cparams("parallel", "parallel"),
    )(gslots)


HBM_SPEC = pl.BlockSpec(memory_space=pl.ANY)


def _all_gather(arrs, name, relayed=(), after=None):
    n = len(arrs)
    n_sems = 13
    n_in = n + (after is not None)

    def body(*refs):
        ins, outs = refs[:n], refs[n_in:n_in + n]
        send_sems, recv_sems, local_sems = refs[n_in + n:]
        x, y, c = lax.axis_index("x"), lax.axis_index("y"), lax.axis_index("c")
        me, sibling = (x, y, c), (x, y, 1 - c)
        chips = [(1 - x, y), (x, 1 - y), (1 - x, 1 - y)]
        index = lambda px, py, pc: 4 * px + 2 * py + pc

        def copy(a, k, block, to, src=None, cols=None):
            dst = outs[a].at[index(*block)]
            src = dst if src is None else src
            if cols is not None:
                dst, src = dst.at[:, cols], src.at[:, cols]
            return pltpu.make_async_remote_copy(
                src_ref=src, dst_ref=dst, send_sem=send_sems.at[a, k], recv_sem=recv_sems.at[a, k],
                device_id=to, device_id_type=MESH)

        mine = [pltpu.make_async_copy(ins[a], outs[a].at[index(*me)], local_sems.at[a])
                for a in range(n)]
        for cp in mine:
            cp.start()
        sends = []

        def start(cp):
            cp.start()
            sends.append(cp)

        halves = {a: (pl.ds(0, arrs[a].shape[1] // 2), pl.ds(arrs[a].shape[1] // 2, arrs[a].shape[1] // 2))
                  for a in relayed}
        near_x, near_y, far = [(*chip, c) for chip in chips]
        for a in range(n):
            start(copy(a, 0, me, sibling, src=ins[a]))
            if a in relayed:
                left, right = halves[a]
                for k, to, cols in ((1, near_x, left), (3, near_y, right), (2, near_x, right), (4, near_y, left)):
                    start(copy(a, k, me, to, src=ins[a], cols=cols))
            else:
                for j, chip in enumerate(chips):
                    start(copy(a, 1 + j, me, (*chip, c), src=ins[a]))
        for a in relayed:
            left, right = halves[a]
            for k, block, cols, onward, to_sibling in (
                    (1, near_x, left, (5, near_y), 7), (3, near_y, right, (6, near_x), 10),
                    (2, near_x, right, None, 8), (4, near_y, left, None, 9),
                    (5, far, left, None, 11), (6, far, right, None, 12)):
                copy(a, k, block, me, cols=cols).wait_recv()
                if onward is not None:
                    start(copy(a, onward[0], block, onward[1], cols=cols))
                start(copy(a, to_sibling, block, sibling, cols=cols))
        for j, chip in enumerate(chips):
            for a in range(n):
                if a not in relayed:
                    copy(a, 1 + j, (*chip, c), me).wait_recv()
                    start(copy(a, 4 + j, (*chip, c), sibling))
        for a in range(n):
            copy(a, 0, sibling, me).wait_recv()
            if a in relayed:
                left, right = halves[a]
                for k, chip, cols in ((7, chips[0], left), (8, chips[0], right), (9, chips[1], left),
                                      (10, chips[1], right), (11, chips[2], left), (12, chips[2], right)):
                    copy(a, k, (*chip, 1 - c), me, cols=cols).wait_recv()
            else:
                for j, chip in enumerate(chips):
                    copy(a, 4 + j, (*chip, 1 - c), me).wait_recv()
        for cp in sends:
            cp.wait_send()
        for cp in mine:
            cp.wait()

    return pl.pallas_call(
        body, name=name,
        out_shape=[jax.ShapeDtypeStruct((N_DEV,) + a.shape, a.dtype) for a in arrs],
        in_specs=[HBM_SPEC] * n_in, out_specs=[HBM_SPEC] * n,
        scratch_shapes=[pltpu.SemaphoreType.DMA((n, n_sems)), pltpu.SemaphoreType.DMA((n, n_sems)),
                        pltpu.SemaphoreType.DMA((n,))],
    )(*arrs, *([after] if after is not None else []))


def _sibling_swap(arrs, name):
    n = len(arrs)

    def body(*refs):
        ins, outs = refs[:n], refs[n:2 * n]
        send_sems, recv_sems = refs[2 * n:]
        x, y, c = lax.axis_index("x"), lax.axis_index("y"), lax.axis_index("c")
        copies = [pltpu.make_async_remote_copy(
            src_ref=ins[a].at[:, 1 - c], dst_ref=outs[a],
            send_sem=send_sems.at[a], recv_sem=recv_sems.at[a],
            device_id=(x, y, 1 - c), device_id_type=MESH) for a in range(n)]
        for cp in copies:
            cp.start()
        for cp in copies:
            cp.wait()

    return pl.pallas_call(
        body, name=name,
        out_shape=[jax.ShapeDtypeStruct(a.shape[:1] + a.shape[2:], a.dtype) for a in arrs],
        in_specs=[HBM_SPEC] * n, out_specs=[HBM_SPEC] * n,
        scratch_shapes=[pltpu.SemaphoreType.DMA((n,)), pltpu.SemaphoreType.DMA((n,))],
    )(*arrs)


def _pair_sum(mine, theirs, core, name):
    chips, _, rows, cols = mine.shape
    tr, tc = _tile_2d(rows, cols, budget_bytes=2 << 20)

    def body(core_ref, a_ref, b_ref, o_ref):
        o_ref[...] = (a_ref[...].astype(F32) + b_ref[...].astype(F32)).astype(o_ref.dtype)

    slab = pl.BlockSpec((None, tr, tc), lambda ch, i, j, core_ref: (ch, i, j))
    return pl.pallas_call(
        body, name=name, out_shape=jax.ShapeDtypeStruct((chips, rows, cols), mine.dtype),
        grid_spec=pltpu.PrefetchScalarGridSpec(
            num_scalar_prefetch=1, grid=(chips, rows // tr, cols // tc),
            in_specs=[pl.BlockSpec((None, None, tr, tc),
                                   lambda ch, i, j, core_ref: (ch, core_ref[0], i, j)), slab],
            out_specs=slab),
        compiler_params=_cparams("parallel", "parallel", "parallel"),
    )(core, mine, theirs)


HBM_ONLY = pl.BlockSpec(memory_space=pltpu.HBM)
SEM_SPEC = pl.BlockSpec(memory_space=pltpu.SEMAPHORE)
SPLIT_COPY_EFFECT = pltpu.SideEffectType.DATAFLOW_SIDE_EFFECTING


def _flip(v, bit):
    return 1 - v if bit else v


def _chip_slices_plan(n):
    def plan():
        x, y, c = lax.axis_index("x"), lax.axis_index("y"), lax.axis_index("c")
        copies = []
        for k in range(1, 4):
            px, py = _flip(x, k & 2), _flip(y, k & 1)
            copies += [(a, 2 * px + py, 2 * x + y, (px, py, c)) for a in range(n)]
        return copies
    return plan, 3 * n


def _gather_plan(n):
    def plan():
        x, y, c = lax.axis_index("x"), lax.axis_index("y"), lax.axis_index("c")
        copies = []
        for k in range(1, N_DEV):
            peer = (_flip(x, k & 4), _flip(y, k & 2), _flip(c, k & 1))
            copies += [(a, None, 4 * x + 2 * y + c, peer) for a in range(n)]
        return copies + [(a, None, 4 * x + 2 * y + c, None) for a in range(n)]
    return plan, 8 * n


def _planned_copies(plan, srcs, lands, send_sems, recv_sems):
    copies = []
    for i, (a, src_at, land_at, peer) in enumerate(plan()):
        src, dst = srcs[a] if src_at is None else srcs[a].at[src_at], lands[a].at[land_at]
        if peer is None:
            local = pltpu.make_async_copy(src, dst, send_sems[i])
            copies.append((local, local.wait))
        else:
            remote = pltpu.make_async_remote_copy(src_ref=src, dst_ref=dst, send_sem=send_sems[i],
                                                  recv_sem=recv_sems[i], device_id=peer, device_id_type=MESH)
            copies.append((remote, remote.wait))
    return copies


def _split_exchange_start(plan_and_count, arrs, land_shapes, name, after=None):
    plan, n_sems = plan_and_count
    n = len(arrs)

    n_in = 2 * n + (after is not None)

    def body(*refs):
        srcs, lands = refs[:n], refs[n:2 * n]
        send_sems, recv_sems = refs[n_in:n_in + n_sems], refs[n_in + n_sems:n_in + 2 * n_sems]
        token = refs[-1]
        for copy, _ in _planned_copies(plan, srcs, lands, send_sems, recv_sems):
            copy.start()
        token[...] = jnp.zeros_like(token)

    hbm = lambda a: pltpu.HBM(a.shape, a.dtype)
    operands = [pltpu.with_memory_space_constraint(a, pltpu.HBM) for a in arrs]
    operands += [pltpu.with_memory_space_constraint(lax.empty(shape, a.dtype), pltpu.HBM)
                 for a, shape in zip(arrs, land_shapes)]
    out = pl.pallas_call(
        body, name=name,
        out_shape=(*[pltpu.SemaphoreType.DMA(())] * (2 * n_sems),
                   *[hbm(a) for a in operands],
                   jax.ShapeDtypeStruct((8, 128), F32)),
        in_specs=[HBM_ONLY] * (2 * n) + [pl.BlockSpec(memory_space=pl.ANY)] * (after is not None),
        out_specs=(*[SEM_SPEC] * (2 * n_sems), *[HBM_ONLY] * (2 * n),
                   pl.BlockSpec(memory_space=pltpu.VMEM)),
        input_output_aliases={i: 2 * n_sems + i for i in range(2 * n)},
        compiler_params=pltpu.CompilerParams(has_side_effects=SPLIT_COPY_EFFECT),
    )(*operands, *([after] if after is not None else []))
    sems, rest = list(out[:2 * n_sems]), out[2 * n_sems:]
    return sems, list(rest[:n]), list(rest[n:2 * n]), rest[-1]


def _split_exchange_wait(plan_and_count, sems, srcs, lands, after, name):
    plan, n_sems = plan_and_count
    n = len(srcs)

    def body(*refs):
        src_refs, land_refs = refs[:n], refs[n:2 * n]
        send_sems, recv_sems = refs[2 * n:2 * n + n_sems], refs[2 * n + n_sems:2 * n + 2 * n_sems]
        for _, wait in _planned_copies(plan, src_refs, land_refs, send_sems, recv_sems):
            wait()

    hbm = lambda a: pltpu.HBM(a.shape, a.dtype)
    out = pl.pallas_call(
        body, name=name,
        out_shape=(*[hbm(a) for a in srcs], *[hbm(a) for a in lands]),
        in_specs=[HBM_ONLY] * (2 * n) + [SEM_SPEC] * (2 * n_sems) + [pl.BlockSpec(memory_space=pl.ANY)],
        out_specs=tuple([HBM_ONLY] * (2 * n)),
        input_output_aliases={i: i for i in range(2 * n)},
        compiler_params=pltpu.CompilerParams(has_side_effects=SPLIT_COPY_EFFECT),
    )(*srcs, *lands, *sems, after)
    return list(out[n:])


def _adamw_exchanged(w, m, v, own, landed, chip, name):
    rows, cols = w.shape
    tr, tc = _tile_2d(rows, cols)

    def body(chip_ref, w_ref, m_ref, v_ref, own_ref, l1_ref, l2_ref, l3_ref, go_ref, d_ref, mo_ref, vo_ref):
        g = own_ref[...].astype(F32)
        for ref in (l1_ref, l2_ref, l3_ref):
            g = g + ref[...].astype(F32)
        _adamw_update(g, w_ref, m_ref, v_ref, go_ref, d_ref, mo_ref, vo_ref)

    blk = pl.BlockSpec((tr, tc), lambda i, j, chip_ref: (i, j))
    slot = lambda k: pl.BlockSpec((None, tr, tc), lambda i, j, chip_ref: (chip_ref[0] ^ k, i, j))
    o = jax.ShapeDtypeStruct((rows, cols), F32)
    return pl.pallas_call(
        body, name=name, out_shape=(o, o, o, o),
        grid_spec=pltpu.PrefetchScalarGridSpec(
            num_scalar_prefetch=1, grid=(rows // tr, cols // tc),
            in_specs=[blk, blk, blk, slot(0), slot(1), slot(2), slot(3)],
            out_specs=(blk, blk, blk, blk)),
        compiler_params=_cparams("parallel", "parallel"),
    )(chip, w, m, v, own, landed, landed, landed)


def _block_diag(t):
    nb, gpb, r, c = t.shape
    eye = jnp.eye(gpb, dtype=t.dtype)
    return jnp.einsum("ngrc,gh->ngrhc", t, eye).reshape(nb, gpb * r, gpb * c)


def _diag_blocks(t, r, c):
    nb = t.shape[0]
    gpb = t.shape[1] // r
    t = t.reshape(nb, gpb, r, gpb, c)
    return jnp.einsum("ngrhc,gh->ngrc", t, jnp.eye(gpb, dtype=t.dtype))


def _pack_rows(parts):
    flat = jnp.concatenate([p.reshape(-1).astype(F32) for p in parts])
    pad = (-flat.shape[0]) % (256 * 128)
    return jnp.pad(flat, (0, pad)).reshape(-1, 128)


def _unpack_rows(packed, shapes):
    flat = packed.reshape(-1)
    out, at = [], 0
    for shape in shapes:
        size = math.prod(shape)
        out.append(flat[at:at + size].reshape(shape))
        at += size
    return out


def kernel(x, ln_w, w_in, s5_lam_re, s5_lam_im, s5_log_step, s5_b_re, s5_b_im, s5_c_re, s5_c_im, s5_d, s5_w_glu, s5_w_up, dn_conv_w, dn_a_log, dn_dt_bias, dn_norm_w, dn_w_up, w_out, final_norm_w, loss_target, m_ln_w, m_w_in, m_s5_lam_re, m_s5_lam_im, m_s5_log_step, m_s5_b_re, m_s5_b_im, m_s5_c_re, m_s5_c_im, m_s5_d, m_s5_w_glu, m_s5_w_up, m_dn_conv_w, m_dn_a_log, m_dn_dt_bias, m_dn_norm_w, m_dn_w_up, m_w_out, m_final_norm_w, v_ln_w, v_w_in, v_s5_lam_re, v_s5_lam_im, v_s5_log_step, v_s5_b_re, v_s5_b_im, v_s5_c_re, v_s5_c_im, v_s5_d, v_s5_w_glu, v_s5_w_up, v_dn_conv_w, v_dn_a_log, v_dn_dt_bias, v_dn_norm_w, v_dn_w_up, v_w_out, v_final_norm_w):
    weights = dict(ln_w=ln_w, w_in=w_in, s5_lam_re=s5_lam_re, s5_lam_im=s5_lam_im,
                   s5_log_step=s5_log_step, s5_b_re=s5_b_re, s5_b_im=s5_b_im, s5_c_re=s5_c_re,
                   s5_c_im=s5_c_im, s5_d=s5_d, s5_w_glu=s5_w_glu, s5_w_up=s5_w_up,
                   dn_conv_w=dn_conv_w, dn_a_log=dn_a_log, dn_dt_bias=dn_dt_bias,
                   dn_norm_w=dn_norm_w, dn_w_up=dn_w_up, w_out=w_out, final_norm_w=final_norm_w)
    mom_m = dict(ln_w=m_ln_w, w_in=m_w_in, s5_lam_re=m_s5_lam_re, s5_lam_im=m_s5_lam_im,
                 s5_log_step=m_s5_log_step, s5_b_re=m_s5_b_re, s5_b_im=m_s5_b_im,
                 s5_c_re=m_s5_c_re, s5_c_im=m_s5_c_im, s5_d=m_s5_d, s5_w_glu=m_s5_w_glu,
                 s5_w_up=m_s5_w_up, dn_conv_w=m_dn_conv_w, dn_a_log=m_dn_a_log,
                 dn_dt_bias=m_dn_dt_bias, dn_norm_w=m_dn_norm_w, dn_w_up=m_dn_w_up,
                 w_out=m_w_out, final_norm_w=m_final_norm_w)
    mom_v = dict(ln_w=v_ln_w, w_in=v_w_in, s5_lam_re=v_s5_lam_re, s5_lam_im=v_s5_lam_im,
                 s5_log_step=v_s5_log_step, s5_b_re=v_s5_b_re, s5_b_im=v_s5_b_im,
                 s5_c_re=v_s5_c_re, s5_c_im=v_s5_c_im, s5_d=v_s5_d, s5_w_glu=v_s5_w_glu,
                 s5_w_up=v_s5_w_up, dn_conv_w=v_dn_conv_w, dn_a_log=v_dn_a_log,
                 dn_dt_bias=v_dn_dt_bias, dn_norm_w=v_dn_norm_w, dn_w_up=v_dn_w_up,
                 w_out=v_w_out, final_norm_w=v_final_norm_w)
    names = list(weights)

    l, d = x.shape[1], x.shape[2]
    d_s5 = d // 2
    groups = d_s5 // S5_GROUP
    nb = groups // S5_GPB
    d_dn = DN_HEADS * DN_HEAD_DIM
    w_in_cols = w_in.shape[2]
    d_in = N_DEV * w_in_cols
    off_ba_src = 2 * d_s5 + 4 * d_dn
    off_u, off_zs, off_qkv, off_zd = 0, d_s5, 2 * d_s5, 2 * d_s5 + 3 * d_dn
    off_ba = off_zd + d_dn
    n_main = off_ba + BA_PAD
    off_gs, off_gd = 0, d
    x2d, tgt2d = x[0], loss_target[0]
    my_index = 4 * lax.axis_index("x") + 2 * lax.axis_index("y") + lax.axis_index("c")

    g_win, g_conv = _all_gather([jnp.transpose(w_in[0]).astype(BF16), dn_conv_w[0]], name="gather_weights",
                                relayed=(0,))
    late_plan = _gather_plan(4)
    late_shards = [s5_w_glu[0].astype(BF16), s5_w_up[0].astype(BF16), dn_w_up[0].astype(BF16),
                   w_out[0].astype(BF16)]
    late_sems, late_shards, late_lands, late_token = _split_exchange_start(
        late_plan, late_shards, [(N_DEV,) + s.shape for s in late_shards], name="gather_late_start",
        after=g_conv)
    ba_end = off_ba_src + 2 * DN_HEADS
    w_full_t = g_win.reshape(d_in, d)
    w_gates_t = w_full_t[ba_end:]
    conv_full = jnp.transpose(g_conv, (1, 0, 2)).reshape(CONV_K, 3 * d_dn)

    lam_re, lam_im = s5_lam_re[0], s5_lam_im[0]
    log_step = s5_log_step[0].reshape(groups, 1)
    b_re = s5_b_re[0].reshape(groups * S5_STATE, S5_GROUP)
    b_im = s5_b_im[0].reshape(groups * S5_STATE, S5_GROUP)
    abar_re, abar_im, f_re, f_im = _s5_disc_fwd(lam_re, lam_im, log_step)
    f_re_col, f_im_col = f_re.reshape(-1, 1), f_im.reshape(-1, 1)
    bb_re, bb_im = _s5_bbar_fwd(f_re_col, f_im_col, b_re, b_im)

    def bb_blocks(t):
        t = t.reshape(nb, S5_GPB, S5_STATE, S5_GROUP).transpose(0, 1, 3, 2)
        return _block_diag(t).astype(BF16)

    def c_blocks(t):
        return _block_diag(t.reshape(nb, S5_GPB, S5_GROUP, S5_STATE)).astype(BF16)

    bbr, bbi = bb_blocks(bb_re), bb_blocks(bb_im)
    cbr, cbi = c_blocks(s5_c_re[0]), c_blocks(s5_c_im[0])
    ctr, cti = jnp.transpose(cbr, (0, 2, 1)), jnp.transpose(cbi, (0, 2, 1))
    a_re = abar_re.reshape(nb, 1, S5_GPB * S5_STATE)
    a_im = abar_im.reshape(nb, 1, S5_GPB * S5_STATE)

    h = _rms_fwd(x2d, ln_w)
    proj = _mm(h, w_full_t, tb=True, b_rows=n_main, tm=1024, tn=n_main // 4, after=late_token, name="proj")
    proj_gates = _mm(h, w_gates_t, tb=True, tm=1024, tn=1024, name="proj_gates")
    y1, car_r, car_i, states_r, states_i = _s5_fwd(proj, bbr, bbi, a_re, a_im, ctr, cti, s5_d, d_s5)
    a_log_row = jnp.pad(dn_a_log, ((0, 0), (DN_HEADS, 128 - 2 * DN_HEADS)))
    dt_row = jnp.pad(dn_dt_bias, ((0, 0), (DN_HEADS, 128 - 2 * DN_HEADS)))
    qkv = _dn_prep_fwd(proj, off_qkv, conv_full)
    gb = _dn_gates_fwd(proj, off_ba, a_log_row, dt_row)
    o_dn, states, inverses = _dn_chunk_fwd(qkv, gb)

    g_glu, g_sup, g_dup, g_wout = _split_exchange_wait(late_plan, late_sems, late_shards, late_lands, o_dn,
                                                       name="gather_late_wait")
    wglu_full = g_glu.reshape(d_s5, d_s5)
    wsup_full = jnp.transpose(g_sup, (1, 0, 2)).reshape(d_s5, d)
    wdup_full = jnp.transpose(g_dup, (1, 0, 2)).reshape(d_dn, d)
    wout_full = g_wout.reshape(d, d)

    out_s = _s5_glu_fwd(y1, proj, off_zs, wglu_full)
    y_s = _mm(out_s, wsup_full, name="s5_up")
    out_d = _dn_out_fwd(o_dn, proj, off_zd, dn_norm_w)
    y_d = _mm(out_d, wdup_full, name="dn_up")

    mixed = _merge_fwd(proj_gates, off_gs, off_gd, y_s, y_d)
    branch = _mm(mixed, wout_full, name="w_out")
    dx2, dx2_bf, loss_dev, d_final_w = _final(x2d, branch, final_norm_w.reshape(1, d), tgt2d)

    g_wout_full = _mm(mixed, dx2_bf, ta=True, out_dtype=BF16, name="grad_w_out")
    dmixed = _mm(dx2_bf, wout_full, tb=True, name="d_mixed")
    dgs, dgd, dys, dyd = _merge_bwd(proj_gates, off_gs, off_gd, y_s, y_d, dmixed)

    g_dup_full = _mm(out_d, dyd, ta=True, out_dtype=BF16, name="grad_dn_up")
    dout_d = _mm(dyd, wdup_full, tb=True, name="d_out_d")
    do_dn, dzd, d_norm_w = _dn_out_bwd(o_dn, proj, off_zd, dn_norm_w, dout_d)
    dqkv, dgb_heads = _dn_chunk_bwd(qkv, gb, states, inverses, do_dn)
    dba, d_a_log_row, d_dt_row = _dn_gates_bwd(proj, off_ba, a_log_row, dt_row, dgb_heads)
    dqkv_pre, d_conv_full = _dn_prep_bwd(proj, off_qkv, conv_full, dqkv)

    g_sup_full = _mm(out_s, dys, ta=True, out_dtype=BF16, name="grad_s5_up")
    dout_s = _mm(dys, wsup_full, tb=True, name="d_out_s")
    dy1, dzs, g_glu_full = _s5_glu_bwd(y1, proj, off_zs, wglu_full, dout_s)

    def by_dest(t, axis=0):
        if axis == 1:
            return t.reshape(t.shape[0], 4, 2, t.shape[1] // N_DEV).transpose(1, 2, 0, 3)
        return t.reshape(4, 2, t.shape[0] // N_DEV, t.shape[1])

    core = lax.axis_index("c").astype(jnp.int32).reshape(1)
    chip = (2 * lax.axis_index("x") + lax.axis_index("y")).astype(jnp.int32).reshape(1)

    def chip_sums_of(which, parts, tag):
        from_sibling = _sibling_swap(parts, name="swap_grads_" + tag)
        return [_pair_sum(p, got, core, name="pair_sum_" + nm)
                for nm, p, got in zip(which, parts, from_sibling)]

    early = ["s5_w_glu", "s5_w_up", "dn_w_up", "w_out"]
    sums_a = chip_sums_of(early, [by_dest(g_glu_full.astype(BF16)), by_dest(g_sup_full, 1),
                                  by_dest(g_dup_full, 1), by_dest(g_wout_full)], "a")
    plan_a = _chip_slices_plan(len(sums_a))
    sems_a, src_a, land_a, token_a = _split_exchange_start(
        plan_a, sums_a, [t.shape for t in sums_a], name="exchange_start_a")

    (du, d_a_re, d_a_im, d_bbr, d_bbi, d_cbr, d_cbi, d_s5_d) = _s5_bwd(
        proj, dy1, bbr, bbi, a_re, a_im, cbr, cbi, s5_d + token_a[:1, :1], car_r, car_i,
        states_r, states_i)

    def from_bb_blocks(t):
        t = _diag_blocks(t, S5_GROUP, S5_STATE).transpose(0, 1, 3, 2)
        return t.reshape(groups * S5_STATE, S5_GROUP)

    d_f_re, d_f_im, d_b_re, d_b_im = _s5_bbar_bwd(f_re_col, f_im_col, b_re, b_im,
                                                 from_bb_blocks(d_bbr), from_bb_blocks(d_bbi))
    d_lam_re, d_lam_im, d_log_step = _s5_disc_bwd(
        lam_re, lam_im, log_step, d_a_re.reshape(groups, S5_STATE), d_a_im.reshape(groups, S5_STATE),
        d_f_re.reshape(groups, S5_STATE), d_f_im.reshape(groups, S5_STATE))
    d_c_re = _diag_blocks(d_cbr, S5_GROUP, S5_STATE).reshape(groups, S5_GROUP, S5_STATE)
    d_c_im = _diag_blocks(d_cbi, S5_GROUP, S5_STATE).reshape(groups, S5_GROUP, S5_STATE)

    dproj = jnp.concatenate([du, dzs, dqkv_pre, dzd, jnp.pad(dba, ((0, 0), (0, BA_PAD - 128)))], axis=1)
    dproj_gates = jnp.concatenate([dgs, dgd], axis=1)
    g_main_t = _mm(dproj, h, ta=True, out_dtype=BF16, tm=512, tn=d, name="grad_w_in")
    g_gates_t = _mm(dproj_gates, h, ta=True, out_dtype=BF16, tm=512, tn=d, name="grad_w_in_gates")
    g_win_full_t = jnp.concatenate([g_main_t[:ba_end], g_gates_t], axis=0)
    sums_b = chip_sums_of(["w_in"], [by_dest(g_win_full_t)], "b")
    plan_b = _chip_slices_plan(1)
    sems_b, src_b, land_b, token_b = _split_exchange_start(
        plan_b, sums_b, [t.shape for t in sums_b], name="exchange_start_b")
    dh_main = _mm(dproj, w_full_t, b_rows=n_main, tm=1024, tn=1024, tk=n_main // 4, after=token_b,
                  name="d_h_main")
    dh = _mm(dproj_gates, w_gates_t, tm=1024, tn=1024, tk=2048, addend=dh_main, name="d_h")
    grad_x, d_ln_w = _rms_bwd(x2d, ln_w, dh, dx2)
    big = ["w_in"] + early
    results = {}

    land_a = _split_exchange_wait(plan_a, sems_a, src_a, land_a, grad_x, name="exchange_wait_a")
    for nm, own, landed in zip(early, src_a, land_a):
        results[nm] = _adamw_exchanged(weights[nm][0], mom_m[nm][0], mom_v[nm][0], own, landed, chip,
                                       name="adamw_" + nm)

    small = [nm for nm in names if nm not in big]
    small_grads = dict(
        ln_w=d_ln_w, s5_lam_re=d_lam_re, s5_lam_im=d_lam_im, s5_log_step=d_log_step,
        s5_b_re=d_b_re, s5_b_im=d_b_im, s5_c_re=d_c_re, s5_c_im=d_c_im, s5_d=d_s5_d,
        dn_conv_w=d_conv_full, dn_a_log=d_a_log_row[:, DN_HEADS:2 * DN_HEADS],
        dn_dt_bias=d_dt_row[:, DN_HEADS:2 * DN_HEADS], dn_norm_w=d_norm_w, final_norm_w=d_final_w)
    (all_small,) = _all_gather([_pack_rows([small_grads[nm] for nm in small])], name="gather_small_grads",
                               after=results[early[-1]][0])
    summed = _slot_sum(all_small, name="sum_small_grads")
    full_shapes = [(CONV_K, 3 * d_dn) if nm == "dn_conv_w" else weights[nm].shape for nm in small]
    g_small = dict(zip(small, _unpack_rows(summed, full_shapes)))
    conv_cols = dn_conv_w.shape[2]
    g_small["dn_conv_w"] = lax.dynamic_slice_in_dim(
        g_small["dn_conv_w"], my_index * conv_cols, conv_cols, axis=1).reshape(dn_conv_w.shape)
    packed = [_pack_rows([t[nm] for nm in small]) for t in (weights, mom_m, mom_v, g_small)]
    small_out = _adamw(packed[0], packed[1], packed[2], packed[3][None], name="adamw_small")
    small_shapes = [weights[nm].shape for nm in small]
    for kind, packed_out in enumerate(small_out):
        for nm, val in zip(small, _unpack_rows(packed_out, small_shapes)):
            results.setdefault(nm, [None] * 4)[kind] = val

    (land_b,) = _split_exchange_wait(plan_b, sems_b, src_b, land_b, small_out[0], name="exchange_wait_b")
    res = _adamw_exchanged(jnp.transpose(w_in[0]), jnp.transpose(m_w_in[0]), jnp.transpose(v_w_in[0]),
                           src_b[0], land_b, chip, name="adamw_w_in")
    results["w_in"] = [jnp.transpose(t) for t in res]

    loss = lax.psum(loss_dev[0, 0], ("x", "y", "c"))
    outs = [loss, grad_x[None]]
    for kind in range(4):
        outs += [results[nm][kind].reshape(weights[nm].shape) for nm in names]
    return tuple(outs)
```

```python
import functools
import math

import jax
import jax.numpy as jnp
from jax import lax
from jax.experimental import pallas as pl
from jax.experimental.pallas import tpu as pltpu

F32 = jnp.float32
BF16 = jnp.bfloat16
HIGHEST = lax.Precision.HIGHEST
MESH = pl.DeviceIdType.MESH
N_DEV = 8

EPS = 1e-6
S5_GROUP = 16
S5_STATE = 64
S5_GPB = 8
S5_T = 1024
DN_HEADS = 8
DN_HEAD_DIM = 128
CHUNK = 64
DN_HEADS_PER_STEP = 8
DN_CHUNKS_PER_STEP = 4
CONV_K = 4
BA_PAD = 512

ADAM_LR = 0.001
ADAM_B1 = 0.9
ADAM_B2 = 0.999
ADAM_EPS = 1e-08
ADAM_WD = 0.01
ADAM_STEP = 10

VMEM_LIMIT_BYTES = 48 * 1024 * 1024
ROW_TILE = 256


def _cparams(*sem):
    return pltpu.CompilerParams(dimension_semantics=sem if sem else None,
                                vmem_limit_bytes=VMEM_LIMIT_BYTES)


@jax.custom_jvp
def _sigmoid(x):
    return 1.0 / (1.0 + jnp.exp(-x))


@_sigmoid.defjvp
def _sigmoid_jvp(primals, tangents):
    s = _sigmoid(primals[0])
    return s, tangents[0] * (s * (1.0 - s))


def _silu(x):
    return x * _sigmoid(x)


def _gelu(x):
    return 0.5 * x * (1.0 + jnp.tanh(0.7978845608028654 * (x + 0.044715 * x * x * x)))


def _softplus(x):
    return jnp.maximum(x, 0.0) + jnp.log(1.0 + jnp.exp(-jnp.abs(x)))


def _rmsnorm(x, w):
    return x * lax.rsqrt(jnp.mean(x * x, axis=-1, keepdims=True) + EPS) * w


def _dot(a, b, dims=((1,), (0,)), precision=None):
    return lax.dot_general(a, b, (dims, ((), ())), precision=precision,
                           preferred_element_type=F32)


def _bdot(a, b, dims=((1,), (0,))):
    return _dot(a.astype(BF16), b.astype(BF16), dims)


def _split_bf16(a):
    hi = a.astype(BF16)
    return hi, (a - hi.astype(F32)).astype(BF16)


def _dot3_dims(a, b, dims):
    ah, al = _split_bf16(a)
    bh, bl = _split_bf16(b)
    return _dot(ah, bh, dims) + (_dot(ah, bl, dims) + _dot(al, bh, dims))


@jax.custom_vjp
def _dot3(a, b):
    return _dot3_dims(a, b, ((1,), (0,)))


def _dot3_fwd(a, b):
    return _dot3(a, b), (a, b)


def _dot3_bwd(res, g):
    a, b = res
    return _dot3_dims(g, b, ((1,), (1,))), _dot3_dims(a, g, ((0,), (0,)))


_dot3.defvjp(_dot3_fwd, _dot3_bwd)


def _mm(a, b, *, ta=False, tb=False, out_dtype=F32, tm=512, tn=512, tk=None, after=None, b_rows=None,
        addend=None, name):
    k_dim, m_dim = (a.shape if ta else a.shape[::-1])
    b_rows = b.shape[0] if b_rows is None else b_rows
    n_dim = b_rows if tb else b.shape[1]
    assert (b.shape[1] if tb else b_rows) == k_dim and b_rows <= b.shape[0]
    tm, tn = min(tm, m_dim), min(tn, n_dim)
    tk = k_dim if tk is None else tk
    assert m_dim % tm == 0 and n_dim % tn == 0 and k_dim % tk == 0
    nk = k_dim // tk
    a_spec = (pl.BlockSpec((tk, tm), lambda i, j, k: (k, i)) if ta
              else pl.BlockSpec((tm, tk), lambda i, j, k: (i, k)))
    b_spec = (pl.BlockSpec((tn, tk), lambda i, j, k: (j, k)) if tb
              else pl.BlockSpec((tk, tn), lambda i, j, k: (k, j)))
    dims = ((0 if ta else 1,), (1 if tb else 0,))

    extras = ([after] if after is not None else []) + ([addend] if addend is not None else [])
    extra_specs = ([pl.BlockSpec((8, 128), lambda i, j, k: (0, 0))] if after is not None else []) + (
        [pl.BlockSpec((tm, tn), lambda i, j, k: (i, j))] if addend is not None else [])

    def body(a_ref, b_ref, *rest):
        o_ref, *scratch = rest[len(extras):]
        p = _bdot(a_ref[...], b_ref[...], dims)
        finish = (lambda v: v + rest[len(extras) - 1][...]) if addend is not None else (lambda v: v)
        if nk == 1:
            o_ref[...] = finish(p).astype(o_ref.dtype)
        else:
            acc = scratch[0]
            k = pl.program_id(2)

            @pl.when(k == 0)
            def _():
                acc[...] = p

            @pl.when(k > 0)
            def _():
                acc[...] += p

            @pl.when(k == nk - 1)
            def _():
                o_ref[...] = finish(acc[...]).astype(o_ref.dtype)

    return pl.pallas_call(
        body, name=name,
        out_shape=jax.ShapeDtypeStruct((m_dim, n_dim), out_dtype),
        grid=(m_dim // tm, n_dim // tn, nk),
        in_specs=[a_spec, b_spec] + extra_specs,
        out_specs=pl.BlockSpec((tm, tn), lambda i, j, k: (i, j)),
        scratch_shapes=[pltpu.VMEM((tm, tn), F32)] if nk > 1 else [],
        compiler_params=_cparams("parallel", "parallel", "arbitrary"),
    )(a, b, *extras)


def _rms_fwd(x, w):
    l, d = x.shape

    def body(x_ref, w_ref, h_ref):
        h_ref[...] = _rmsnorm(x_ref[...], w_ref[...]).astype(BF16)

    return pl.pallas_call(
        body, name="rms_fwd",
        out_shape=jax.ShapeDtypeStruct((l, d), BF16),
        grid=(l // ROW_TILE,),
        in_specs=[pl.BlockSpec((ROW_TILE, d), lambda i: (i, 0)),
                  pl.BlockSpec((1, d), lambda i: (0, 0))],
        out_specs=pl.BlockSpec((ROW_TILE, d), lambda i: (i, 0)),
        compiler_params=_cparams("parallel"),
    )(x, w)


def _rms_bwd(x, w, dh, dres):
    l, d = x.shape

    def body(x_ref, w_ref, dh_ref, dres_ref, dx_ref, dw_ref):
        _, vjp = jax.vjp(_rmsnorm, x_ref[...], w_ref[...])
        dx, dw = vjp(dh_ref[...])
        dx_ref[...] = dx + dres_ref[...]

        @pl.when(pl.program_id(0) == 0)
        def _():
            dw_ref[...] = jnp.zeros_like(dw_ref)

        dw_ref[...] += dw

    row = pl.BlockSpec((ROW_TILE, d), lambda i: (i, 0))
    vec = pl.BlockSpec((1, d), lambda i: (0, 0))
    return pl.pallas_call(
        body, name="rms_bwd",
        out_shape=(jax.ShapeDtypeStruct((l, d), F32), jax.ShapeDtypeStruct((1, d), F32)),
        grid=(l // ROW_TILE,),
        in_specs=[row, vec, row, row],
        out_specs=(row, vec),
        compiler_params=_cparams("arbitrary"),
    )(x, w, dh, dres)


def _final(x, r, fw, target):
    l, d = x.shape

    def per_row_loss(x2, w, tgt):
        err = _rmsnorm(x2, w) - tgt
        return 0.5 * jnp.mean(err * err, axis=-1, keepdims=True)

    def body(x_ref, r_ref, w_ref, t_ref, dx_ref, dxb_ref, loss_ref, dw_ref):
        x2 = x_ref[...] + r_ref[...]
        rows, vjp = jax.vjp(functools.partial(per_row_loss, tgt=t_ref[...]), x2, w_ref[...])
        dx2, dw = vjp(jnp.ones_like(rows))
        dx_ref[...] = dx2
        dxb_ref[...] = dx2.astype(BF16)

        @pl.when(pl.program_id(0) == 0)
        def _():
            dw_ref[...] = jnp.zeros_like(dw_ref)
            loss_ref[...] = jnp.zeros_like(loss_ref)

        dw_ref[...] += dw
        loss_ref[...] += jnp.sum(rows, axis=0, keepdims=True)

    row = pl.BlockSpec((ROW_TILE, d), lambda i: (i, 0))
    vec = pl.BlockSpec((1, d), lambda i: (0, 0))
    return pl.pallas_call(
        body, name="final_norm_loss",
        out_shape=(jax.ShapeDtypeStruct((l, d), F32), jax.ShapeDtypeStruct((l, d), BF16),
                   jax.ShapeDtypeStruct((1, 1), F32), jax.ShapeDtypeStruct((1, d), F32)),
        grid=(l // ROW_TILE,),
        in_specs=[row, row, vec, row],
        out_specs=(row, row, pl.BlockSpec((1, 1), lambda i: (0, 0)), vec),
        compiler_params=_cparams("arbitrary"),
    )(x, r, fw, target)


def _merge_fn(gs, gd, ys, yd):
    return _sigmoid(gs) * ys + _sigmoid(gd) * yd


def _merge_fwd(proj, off_gs, off_gd, ys, yd):
    l, d = ys.shape
    cw = min(1024, d)
    blk = lambda off: pl.BlockSpec((ROW_TILE, cw), lambda i, j: (i, off // cw + j))

    def body(gs_ref, gd_ref, ys_ref, yd_ref, o_ref):
        o_ref[...] = _merge_fn(gs_ref[...], gd_ref[...], ys_ref[...], yd_ref[...]).astype(BF16)

    return pl.pallas_call(
        body, name="merge_fwd",
        out_shape=jax.ShapeDtypeStruct((l, d), BF16),
        grid=(l // ROW_TILE, d // cw),
        in_specs=[blk(off_gs), blk(off_gd), blk(0), blk(0)],
        out_specs=blk(0),
        compiler_params=_cparams("parallel", "parallel"),
    )(proj, proj, ys, yd)


def _merge_bwd(proj, off_gs, off_gd, ys, yd, dmixed):
    l, d = ys.shape
    cw = min(1024, d)
    blk = lambda off: pl.BlockSpec((ROW_TILE, cw), lambda i, j: (i, off // cw + j))

    def body(gs_ref, gd_ref, ys_ref, yd_ref, dm_ref, dgs_ref, dgd_ref, dys_ref, dyd_ref):
        _, vjp = jax.vjp(_merge_fn, gs_ref[...], gd_ref[...], ys_ref[...], yd_ref[...])
        dgs, dgd, dys, dyd = vjp(dm_ref[...])
        dgs_ref[...] = dgs.astype(BF16)
        dgd_ref[...] = dgd.astype(BF16)
        dys_ref[...] = dys.astype(BF16)
        dyd_ref[...] = dyd.astype(BF16)

    out = jax.ShapeDtypeStruct((l, d), BF16)
    return pl.pallas_call(
        body, name="merge_bwd",
        out_shape=(out, out, out, out),
        grid=(l // ROW_TILE, d // cw),
        in_specs=[blk(off_gs), blk(off_gd), blk(0), blk(0), blk(0)],
        out_specs=(blk(0), blk(0), blk(0), blk(0)),
        compiler_params=_cparams("parallel", "parallel"),
    )(proj, proj, ys, yd, dmixed)


def _s5_disc_fn(lam_re, lam_im, log_step):
    step = jnp.exp(log_step)
    mag = jnp.exp(lam_re * step)
    abar_re = mag * jnp.cos(lam_im * step)
    abar_im = mag * jnp.sin(lam_im * step)
    den = lam_re * lam_re + lam_im * lam_im
    xr = abar_re - 1.0
    f_re = (xr * lam_re + abar_im * lam_im) / den
    f_im = (abar_im * lam_re - xr * lam_im) / den
    return abar_re, abar_im, f_re, f_im


def _s5_disc_fwd(lam_re, lam_im, log_step):
    g, p = lam_re.shape

    def body(lr_ref, li_ref, ls_ref, ar_ref, ai_ref, fr_ref, fi_ref):
        ar, ai, fr, fi = _s5_disc_fn(lr_ref[...], li_ref[...], ls_ref[...])
        ar_ref[...] = ar
        ai_ref[...] = ai
        fr_ref[...] = fr
        fi_ref[...] = fi

    o = jax.ShapeDtypeStruct((g, p), F32)
    return pl.pallas_call(body, name="s5_disc_fwd", out_shape=(o, o, o, o),
                          compiler_params=_cparams())(lam_re, lam_im, log_step)


def _s5_disc_bwd(lam_re, lam_im, log_step, dar, dai, dfr, dfi):
    g, p = lam_re.shape

    def body(lr_ref, li_ref, ls_ref, dar_ref, dai_ref, dfr_ref, dfi_ref, dlr_ref, dli_ref, dls_ref):
        _, vjp = jax.vjp(_s5_disc_fn, lr_ref[...], li_ref[...], ls_ref[...])
        dlr, dli, dls = vjp((dar_ref[...], dai_ref[...], dfr_ref[...], dfi_ref[...]))
        dlr_ref[...] = dlr
        dli_ref[...] = dli
        dls_ref[...] = dls

    o = jax.ShapeDtypeStruct((g, p), F32)
    return pl.pallas_call(body, name="s5_disc_bwd",
                          out_shape=(o, o, jax.ShapeDtypeStruct((g, 1), F32)),
                          compiler_params=_cparams())(lam_re, lam_im, log_step, dar, dai, dfr, dfi)


def _s5_bbar_fwd(f_re, f_im, b_re, b_im):
    n, c = b_re.shape

    def body(fr_ref, fi_ref, br_ref, bi_ref, or_ref, oi_ref):
        fr, fi, br, bi = fr_ref[...], fi_ref[...], br_ref[...], bi_ref[...]
        or_ref[...] = fr * br - fi * bi
        oi_ref[...] = fr * bi + fi * br

    o = jax.ShapeDtypeStruct((n, c), F32)
    return pl.pallas_call(body, name="s5_bbar_fwd", out_shape=(o, o),
                          compiler_params=_cparams())(f_re, f_im, b_re, b_im)


def _s5_bbar_bwd(f_re, f_im, b_re, b_im, dbr, dbi):
    n, c = b_re.shape

    def body(fr_ref, fi_ref, br_ref, bi_ref, dor_ref, doi_ref, dfr_ref, dfi_ref, dbr_ref, dbi_ref):
        fr, fi, br, bi = fr_ref[...], fi_ref[...], br_ref[...], bi_ref[...]
        dor, doi = dor_ref[...], doi_ref[...]
        dfr_ref[...] = jnp.sum(dor * br + doi * bi, axis=-1, keepdims=True)
        dfi_ref[...] = jnp.sum(doi * br - dor * bi, axis=-1, keepdims=True)
        dbr_ref[...] = fr * dor + fi * doi
        dbi_ref[...] = fr * doi - fi * dor

    col = jax.ShapeDtypeStruct((n, 1), F32)
    o = jax.ShapeDtypeStruct((n, c), F32)
    return pl.pallas_call(body, name="s5_bbar_bwd", out_shape=(col, col, o, o),
                          compiler_params=_cparams())(f_re, f_im, b_re, b_im, dbr, dbi)


SUBLANES = 8


def _scan_groups(xr, xi, ar, ai, reverse):
    t, n = xr.shape
    xr, xi = xr.reshape(t // SUBLANES, SUBLANES, n), xi.reshape(t // SUBLANES, SUBLANES, n)
    sub = lax.broadcasted_iota(jnp.int32, (1, SUBLANES, 1), 1)
    pr, pi = ar.reshape(1, 1, n), ai.reshape(1, 1, n)
    for sh in (1, 2, 4):
        keep = (sub < SUBLANES - sh) if reverse else (sub >= sh)
        cr, ci = jnp.where(keep, pr, 0.0), jnp.where(keep, pi, 0.0)
        shift = SUBLANES - sh if reverse else sh
        sr, si = pltpu.roll(xr, shift, 1), pltpu.roll(xi, shift, 1)
        xr, xi = xr + cr * sr - ci * si, xi + cr * si + ci * sr
        pr, pi = pr * pr - pi * pi, 2.0 * pr * pi
    return xr.reshape(t, n), xi.reshape(t, n)


def _scan_rows(xr, xi, ar, ai, cr, ci, sr_ref, si_ref, reverse):
    t, n = xr.shape
    xr, xi = _scan_groups(xr, xi, ar, ai, reverse)
    sr_ref[...] = xr
    si_ref[...] = xi
    sub = lax.broadcasted_iota(jnp.int32, (SUBLANES, n), 0)
    seed = sub == (SUBLANES - 1 if reverse else 0)
    pwr, pwi = _scan_groups(jnp.where(seed, ar, 0.0), jnp.where(seed, ai, 0.0), ar, ai, reverse)
    groups = range(t // SUBLANES)
    edge = 0 if reverse else SUBLANES - 1
    for g in (reversed(groups) if reverse else groups):
        rows = slice(g * SUBLANES, (g + 1) * SUBLANES)
        vr = sr_ref[rows, :] + (pwr * cr - pwi * ci)
        vi = si_ref[rows, :] + (pwr * ci + pwi * cr)
        sr_ref[rows, :] = vr
        si_ref[rows, :] = vi
        cr, ci = vr[edge:edge + 1, :], vi[edge:edge + 1, :]
    return cr, ci


def _s5_states(u_bf, bbr, bbi, ar, ai, cr, ci, sr_ref, si_ref):
    return _scan_rows(_dot(u_bf, bbr), _dot(u_bf, bbi), ar, ai, cr, ci, sr_ref, si_ref, reverse=False)


def _s5_fwd(proj, bbr, bbi, a_re, a_im, ctr, cti, d_skip, d_s5):
    l = proj.shape[0]
    nb, uc, ns = bbr.shape
    t = min(S5_T, l)
    nt = l // t

    def body(u_ref, bbr_ref, bbi_ref, ar_ref, ai_ref, ctr_ref, cti_ref, d_ref,
             y_ref, car_r_ref, car_i_ref, sr_ref, si_ref, cr, ci):
        @pl.when(pl.program_id(1) == 0)
        def _():
            cr[...] = jnp.zeros_like(cr)
            ci[...] = jnp.zeros_like(ci)

        car_r_ref[...] = cr[...]
        car_i_ref[...] = ci[...]
        u = u_ref[...]
        cr[...], ci[...] = _s5_states(u.astype(BF16), bbr_ref[...], bbi_ref[...], ar_ref[...],
                                      ai_ref[...], cr[...], ci[...], sr_ref, si_ref)
        y_ref[...] = (_bdot(sr_ref[...], ctr_ref[...]) - _bdot(si_ref[...], cti_ref[...])
                      + d_ref[...] * u)

    per_block = lambda shape: pl.BlockSpec((None,) + shape, lambda b, n: (b, 0, 0))
    return pl.pallas_call(
        body, name="s5_fwd",
        out_shape=(jax.ShapeDtypeStruct((l, d_s5), F32),
                   jax.ShapeDtypeStruct((nt, 1, nb * ns), F32),
                   jax.ShapeDtypeStruct((nt, 1, nb * ns), F32),
                   jax.ShapeDtypeStruct((l, nb * ns), F32),
                   jax.ShapeDtypeStruct((l, nb * ns), F32)),
        grid=(nb, nt),
        in_specs=[pl.BlockSpec((t, uc), lambda b, n: (n, b)),
                  per_block((uc, ns)), per_block((uc, ns)),
                  per_block((1, ns)), per_block((1, ns)),
                  per_block((ns, uc)), per_block((ns, uc)),
                  pl.BlockSpec((1, uc), lambda b, n: (0, b))],
        out_specs=(pl.BlockSpec((t, uc), lambda b, n: (n, b)),
                   pl.BlockSpec((None, 1, ns), lambda b, n: (n, 0, b)),
                   pl.BlockSpec((None, 1, ns), lambda b, n: (n, 0, b)),
                   pl.BlockSpec((t, ns), lambda b, n: (n, b)),
                   pl.BlockSpec((t, ns), lambda b, n: (n, b))),
        scratch_shapes=[pltpu.VMEM((1, ns), F32), pltpu.VMEM((1, ns), F32)],
        compiler_params=_cparams("parallel", "arbitrary"),
    )(proj, bbr, bbi, a_re, a_im, ctr, cti, d_skip)


def _s5_bwd(proj, dy, bbr, bbi, a_re, a_im, cbr, cbi, d_skip, car_r, car_i, states_r, states_i):
    l, d_s5 = dy.shape
    nb, uc, ns = bbr.shape
    t = min(S5_T, l)
    nt = l // t

    def body(u_ref, dy_ref, bbr_ref, bbi_ref, ar_ref, ai_ref, cbr_ref, cbi_ref, d_ref,
             car_r_ref, car_i_ref, sr_ref, si_ref,
             du_ref, dar_ref, dai_ref, dbbr_ref, dbbi_ref, dcbr_ref, dcbi_ref, dd_ref, gcr, gci,
             gr_ref, gi_ref):
        @pl.when(pl.program_id(1) == 0)
        def _():
            gcr[...] = jnp.zeros_like(gcr)
            gci[...] = jnp.zeros_like(gci)
            for ref in (dar_ref, dai_ref, dbbr_ref, dbbi_ref, dcbr_ref, dcbi_ref, dd_ref):
                ref[...] = jnp.zeros_like(ref)

        row = lax.broadcasted_iota(jnp.int32, (t, 1), 0)
        u, dy = u_ref[...], dy_ref[...]
        u_bf, dy_bf = u.astype(BF16), dy.astype(BF16)
        ar, ai = ar_ref[...], ai_ref[...]
        cr, ci = car_r_ref[...], car_i_ref[...]
        sr, si = sr_ref[...], si_ref[...]
        first = row == 0
        pr = jnp.where(first, cr, pltpu.roll(sr, 1, 0))
        pi = jnp.where(first, ci, pltpu.roll(si, 1, 0))
        gcr[...], gci[...] = _scan_rows(_dot(dy_bf, cbr_ref[...]), -_dot(dy_bf, cbi_ref[...]), ar, -ai,
                                        gcr[...], gci[...], gr_ref, gi_ref, reverse=True)
        gr, gi = gr_ref[...], gi_ref[...]
        dar_ref[...] += jnp.sum(gr * pr + gi * pi, axis=0, keepdims=True)
        dai_ref[...] += jnp.sum(gi * pr - gr * pi, axis=0, keepdims=True)
        gr_bf, gi_bf = gr.astype(BF16), gi.astype(BF16)
        tn = ((0,), (0,))
        dbbr_ref[...] += _dot(u_bf, gr_bf, tn)
        dbbi_ref[...] += _dot(u_bf, gi_bf, tn)
        dcbr_ref[...] += _dot(dy_bf, sr.astype(BF16), tn)
        dcbi_ref[...] -= _dot(dy_bf, si.astype(BF16), tn)
        nt_dims = ((1,), (1,))
        du = _dot(gr_bf, bbr_ref[...], nt_dims) + _dot(gi_bf, bbi_ref[...], nt_dims) + dy * d_ref[...]
        du_ref[...] = du.astype(BF16)
        dd_ref[...] += jnp.sum(dy * u, axis=0, keepdims=True)

    rev = lambda n: nt - 1 - n
    per_block = lambda shape: pl.BlockSpec((None,) + shape, lambda b, n: (b, 0, 0))
    acc = jax.ShapeDtypeStruct((nb, uc, ns), F32)
    vec = jax.ShapeDtypeStruct((nb, 1, ns), F32)
    return pl.pallas_call(
        body, name="s5_bwd",
        out_shape=(jax.ShapeDtypeStruct((l, d_s5), BF16), vec, vec, acc, acc, acc, acc,
                   jax.ShapeDtypeStruct((1, d_s5), F32)),
        grid=(nb, nt),
        in_specs=[pl.BlockSpec((t, uc), lambda b, n: (rev(n), b)),
                  pl.BlockSpec((t, uc), lambda b, n: (rev(n), b)),
                  per_block((uc, ns)), per_block((uc, ns)),
                  per_block((1, ns)), per_block((1, ns)),
                  per_block((uc, ns)), per_block((uc, ns)),
                  pl.BlockSpec((1, uc), lambda b, n: (0, b)),
                  pl.BlockSpec((None, 1, ns), lambda b, n: (rev(n), 0, b)),
                  pl.BlockSpec((None, 1, ns), lambda b, n: (rev(n), 0, b)),
                  pl.BlockSpec((t, ns), lambda b, n: (rev(n), b)),
                  pl.BlockSpec((t, ns), lambda b, n: (rev(n), b))],
        out_specs=(pl.BlockSpec((t, uc), lambda b, n: (rev(n), b)),
                   per_block((1, ns)), per_block((1, ns)),
                   per_block((uc, ns)), per_block((uc, ns)),
                   per_block((uc, ns)), per_block((uc, ns)),
                   pl.BlockSpec((1, uc), lambda b, n: (0, b))),
        scratch_shapes=[pltpu.VMEM((1, ns), F32), pltpu.VMEM((1, ns), F32)]
        + [pltpu.VMEM((t, ns), F32)] * 2,
        compiler_params=_cparams("parallel", "arbitrary"),
    )(proj, dy, bbr, bbi, a_re, a_im, cbr, cbi, d_skip, car_r, car_i, states_r, states_i)


def _s5_glu_fwd(y1, proj, off_z, wglu):
    l, d = y1.shape

    def body(y_ref, z_ref, w_ref, o_ref):
        y2 = _gelu(y_ref[...])
        y3 = y2 * _sigmoid(_bdot(y2, w_ref[...]))
        o_ref[...] = (y3 * _silu(z_ref[...])).astype(BF16)

    return pl.pallas_call(
        body, name="s5_glu_fwd",
        out_shape=jax.ShapeDtypeStruct((l, d), BF16),
        grid=(l // ROW_TILE,),
        in_specs=[pl.BlockSpec((ROW_TILE, d), lambda i: (i, 0)),
                  pl.BlockSpec((ROW_TILE, d), lambda i: (i, off_z // d)),
                  pl.BlockSpec((d, d), lambda i: (0, 0))],
        out_specs=pl.BlockSpec((ROW_TILE, d), lambda i: (i, 0)),
        compiler_params=_cparams("parallel"),
    )(y1, proj, wglu)


def _s5_glu_bwd(y1, proj, off_z, wglu, dout):
    l, d = y1.shape

    def body(y_ref, z_ref, w_ref, do_ref, dy_ref, dz_ref, dw_ref):
        y2, gelu_vjp = jax.vjp(_gelu, y_ref[...])
        z = z_ref[...]
        sz, silu_vjp = jax.vjp(_silu, z)
        y2_bf = y2.astype(BF16)
        sg = _sigmoid(_dot(y2_bf, w_ref[...]))
        dout = do_ref[...]
        dy3 = dout * sz
        dz_ref[...] = silu_vjp(dout * (y2 * sg))[0].astype(BF16)
        dgl = (dy3 * y2 * sg * (1.0 - sg)).astype(BF16)
        dy2 = dy3 * sg + _dot(dgl, w_ref[...], ((1,), (1,)))
        dy_ref[...] = gelu_vjp(dy2)[0]

        @pl.when(pl.program_id(0) == 0)
        def _():
            dw_ref[...] = jnp.zeros_like(dw_ref)

        dw_ref[...] += _dot(y2_bf, dgl, ((0,), (0,)))

    row = pl.BlockSpec((ROW_TILE, d), lambda i: (i, 0))
    full = pl.BlockSpec((d, d), lambda i: (0, 0))
    return pl.pallas_call(
        body, name="s5_glu_bwd",
        out_shape=(jax.ShapeDtypeStruct((l, d), F32), jax.ShapeDtypeStruct((l, d), BF16),
                   jax.ShapeDtypeStruct((d, d), F32)),
        grid=(l // ROW_TILE,),
        in_specs=[row, pl.BlockSpec((ROW_TILE, d), lambda i: (i, off_z // d)), full, row],
        out_specs=(row, row, full),
        compiler_params=_cparams("arbitrary"),
    )(y1, proj, wglu, dout)


def _shift_rows(x, k, back=False):
    if k == 0:
        return x
    t = x.shape[0]
    row = lax.broadcasted_iota(jnp.int32, (t, 1), 0)
    if back:
        return jnp.where(row < t - k, pltpu.roll(x, t - k, 0), 0.0)
    return jnp.where(row >= k, pltpu.roll(x, k, 0), 0.0)


def _dn_conv(x, w_ref):
    return sum(w_ref[CONV_K - 1 - k:CONV_K - k, :] * _shift_rows(x, k) for k in range(CONV_K))


def _dn_post_conv(c, j):
    y = _silu(c)
    n = y * lax.rsqrt(jnp.sum(y * y, axis=-1, keepdims=True) + EPS)
    n = n * jnp.where(j < DN_HEADS, DN_HEAD_DIM ** -0.5, 1.0)
    return jnp.where(j < 2 * DN_HEADS, n, y)


def _dn_prep_fwd(proj, off_qkv, conv_w):
    l = proj.shape[0]
    hd = DN_HEAD_DIM
    nblk = 3 * DN_HEADS

    def body(x_ref, w_ref, o_ref):
        o_ref[...] = _dn_post_conv(_dn_conv(x_ref[...], w_ref), pl.program_id(0))

    return pl.pallas_call(
        body, name="dn_prep_fwd",
        out_shape=jax.ShapeDtypeStruct((l, nblk * hd), F32),
        grid=(nblk,),
        in_specs=[pl.BlockSpec((l, hd), lambda j: (0, off_qkv // hd + j)),
                  pl.BlockSpec((CONV_K, hd), lambda j: (0, j))],
        out_specs=pl.BlockSpec((l, hd), lambda j: (0, j)),
        compiler_params=_cparams("parallel"),
    )(proj, conv_w)


def _dn_prep_bwd(proj, off_qkv, conv_w, dqkv):
    l = proj.shape[0]
    hd = DN_HEAD_DIM
    nblk = 3 * DN_HEADS

    def body(x_ref, w_ref, do_ref, dx_ref, dw_ref):
        x = x_ref[...]
        j = pl.program_id(0)
        _, vjp = jax.vjp(functools.partial(_dn_post_conv, j=j), _dn_conv(x, w_ref))
        dc = vjp(do_ref[...])[0]
        dx = sum(w_ref[CONV_K - 1 - k:CONV_K - k, :] * _shift_rows(dc, k, back=True)
                 for k in range(CONV_K))
        dx_ref[...] = dx.astype(BF16)
        for k in range(CONV_K):
            dw_ref[CONV_K - 1 - k:CONV_K - k, :] = jnp.sum(dc * _shift_rows(x, k), axis=0,
                                                           keepdims=True)

    return pl.pallas_call(
        body, name="dn_prep_bwd",
        out_shape=(jax.ShapeDtypeStruct((l, nblk * hd), BF16),
                   jax.ShapeDtypeStruct((CONV_K, nblk * hd), F32)),
        grid=(nblk,),
        in_specs=[pl.BlockSpec((l, hd), lambda j: (0, off_qkv // hd + j)),
                  pl.BlockSpec((CONV_K, hd), lambda j: (0, j)),
                  pl.BlockSpec((None, l, hd), lambda j: (j // DN_HEADS, 0, j % DN_HEADS))],
        out_specs=(pl.BlockSpec((l, hd), lambda j: (0, j)),
                   pl.BlockSpec((CONV_K, hd), lambda j: (0, j))),
        compiler_params=_cparams("parallel"),
    )(proj, conv_w, dqkv)


def _dn_gate_fn(ba, a_log_row, dt_row):
    lane = lax.broadcasted_iota(jnp.int32, ba.shape, 1)
    beta = _sigmoid(ba)
    g = -jnp.exp(a_log_row) * _softplus(ba + dt_row)
    return jnp.where(lane < DN_HEADS, beta, jnp.where(lane < 2 * DN_HEADS, g, 0.0))


def _dn_gates_fwd(proj, off_ba, a_log_row, dt_row):
    l = proj.shape[0]
    row = pl.BlockSpec((ROW_TILE, 128), lambda i: (i, off_ba // 128))
    vec = pl.BlockSpec((1, 128), lambda i: (0, 0))

    def body(ba_ref, al_ref, dt_ref, o_ref):
        o_ref[...] = _dn_gate_fn(ba_ref[...], al_ref[...], dt_ref[...])

    return pl.pallas_call(
        body, name="dn_gates_fwd",
        out_shape=jax.ShapeDtypeStruct((l, 128), F32),
        grid=(l // ROW_TILE,),
        in_specs=[row, vec, vec],
        out_specs=pl.BlockSpec((ROW_TILE, 128), lambda i: (i, 0)),
        compiler_params=_cparams("parallel"),
    )(proj, a_log_row, dt_row)


def _dn_gates_bwd(proj, off_ba, a_log_row, dt_row, dgb_heads):
    l = proj.shape[0]
    nh = dgb_heads.shape[0]
    row = pl.BlockSpec((ROW_TILE, 128), lambda i: (i, off_ba // 128))
    vec = pl.BlockSpec((1, 128), lambda i: (0, 0))

    def body(ba_ref, al_ref, dt_ref, dg_ref, dba_ref, dal_ref, ddt_ref):
        _, vjp = jax.vjp(_dn_gate_fn, ba_ref[...], al_ref[...], dt_ref[...])
        dgb = dg_ref[0]
        for h in range(1, nh):
            dgb = dgb + dg_ref[h]
        dba, dal, ddt = vjp(dgb)
        dba_ref[...] = dba.astype(BF16)

        @pl.when(pl.program_id(0) == 0)
        def _():
            dal_ref[...] = jnp.zeros_like(dal_ref)
            ddt_ref[...] = jnp.zeros_like(ddt_ref)

        dal_ref[...] += dal
        ddt_ref[...] += ddt

    return pl.pallas_call(
        body, name="dn_gates_bwd",
        out_shape=(jax.ShapeDtypeStruct((l, 128), BF16), jax.ShapeDtypeStruct((1, 128), F32),
                   jax.ShapeDtypeStruct((1, 128), F32)),
        grid=(l // ROW_TILE,),
        in_specs=[row, vec, vec, pl.BlockSpec((nh, ROW_TILE, 128), lambda i: (0, i, 0))],
        out_specs=(pl.BlockSpec((ROW_TILE, 128), lambda i: (i, 0)), vec, vec),
        compiler_params=_cparams("arbitrary"),
    )(proj, a_log_row, dt_row, dgb_heads)


@jax.custom_vjp
def _unit_lower_inverses(a_mats):
    c = a_mats[0].shape[0]
    eye = (lax.broadcasted_iota(jnp.int32, (c, c), 0) == lax.broadcasted_iota(jnp.int32, (c, c), 1)).astype(F32)
    t_inv = [eye - a for a in a_mats]
    power = a_mats
    for _ in range(int(math.log2(c)) - 1):
        power = [_bdot(p, p) for p in power]
        t_inv = [t + _bdot(t, p) for t, p in zip(t_inv, power)]
    return t_inv


def _unit_lower_inverses_fwd(a_mats):
    t_inv = _unit_lower_inverses(a_mats)
    return t_inv, t_inv


def _inverse_cotangents(t_inv, grads):
    right = [_dot3_dims(g, t, ((1,), (1,))) for g, t in zip(grads, t_inv)]
    return [-_dot3_dims(t, r, ((0,), (0,))) for t, r in zip(t_inv, right)]


_unit_lower_inverses.defvjp(_unit_lower_inverses_fwd,
                            lambda t_inv, grads: (_inverse_cotangents(t_inv, grads),))


@jax.custom_vjp
def _kept_inverses(a_mats, t_inv):
    return t_inv


_kept_inverses.defvjp(
    lambda a_mats, t_inv: (t_inv, t_inv),
    lambda t_inv, grads: (_inverse_cotangents(t_inv, grads), [jnp.zeros_like(t) for t in t_inv]))


def _dn_chunk_fn(states, qs, ks, vs, gb, heads, kept_inverses=None, return_inverses=False):
    c = qs[0].shape[0]
    each = lambda f, *lists: [f(*args) for args in zip(*lists)]
    lane = lax.broadcasted_iota(jnp.int32, gb.shape, 1)
    ri = lax.broadcasted_iota(jnp.int32, (c, c), 0)
    ci = lax.broadcasted_iota(jnp.int32, (c, c), 1)
    causal, strict = ri >= ci, ri > ci
    eye = (ri == ci).astype(F32)
    rowi = lax.broadcasted_iota(jnp.int32, (c, 1), 0)
    nt_dims = ((1,), (1,))
    hdot = functools.partial(_dot, precision=HIGHEST)

    pick = lambda m, at: jnp.sum(jnp.where(lane == at, m, 0.0), axis=1, keepdims=True)
    gb_cum = hdot(causal.astype(F32), gb)
    beta = [pick(gb, h) for h in heads]
    gc = [pick(gb_cum, h + DN_HEADS) for h in heads]
    gc_row = each(lambda g: jnp.sum(eye * g, axis=0, keepdims=True), gc)
    decay = each(lambda g, gr: jnp.where(causal, jnp.exp(jnp.where(causal, g - gr, 0.0)), 0.0),
                 gc, gc_row)
    kk = each(lambda k: _bdot(k, k, nt_dims), ks)
    a_mat = each(lambda b, m, dc: jnp.where(strict, b * m * dc, 0.0), beta, kk, decay)

    t_inv = (_unit_lower_inverses(a_mat) if kept_inverses is None
             else _kept_inverses(a_mat, kept_inverses))
    egc = each(jnp.exp, gc)
    u_c = each(lambda t, v, b: _dot3(t, v * b), t_inv, vs, beta)
    w_c = each(lambda t, k, b, e: _dot3(t, k * (b * e)), t_inv, ks, beta, egc)
    qk = each(lambda q, k, dc: _bdot(q, k, nt_dims) * dc, qs, ks, decay)
    g_end = each(lambda g: jnp.sum(jnp.where(rowi == c - 1, g, 0.0), axis=0, keepdims=True), gc)
    v_new = each(lambda u, w, s: u - _bdot(w, s), u_c, w_c, states)
    o = each(lambda q, e, s, m, vn: _bdot(q * e, s) + _bdot(m, vn), qs, egc, states, qk, v_new)
    new_states = each(
        lambda s, ge, k, g, vn: s * jnp.exp(ge) + _bdot(k * jnp.exp(ge - g), vn, ((0,), (0,))),
        states, g_end, ks, gc, v_new)
    return (o, new_states, t_inv) if return_inverses else (o, new_states)


def _dn_chunk_specs(order):
    hd, nh, hps, cps = DN_HEAD_DIM, DN_HEADS, DN_HEADS_PER_STEP, DN_CHUNKS_PER_STEP
    rows = cps * CHUNK
    qkv = lambda part: pl.BlockSpec((rows, hps * hd), lambda h, n: (order(n), part * (nh // hps) + h))
    gb = pl.BlockSpec((rows, 128), lambda h, n: (order(n), 0))
    state = pl.BlockSpec((hps, cps, hd, hd), lambda h, n: (h, order(n), 0, 0))
    inverse = pl.BlockSpec((hps, cps, CHUNK, CHUNK), lambda h, n: (h, order(n), 0, 0))
    return qkv, gb, state, inverse


def _dn_chunk_fwd(qkv, gb):
    l = qkv.shape[0]
    hd, nh, hps, cps = DN_HEAD_DIM, DN_HEADS, DN_HEADS_PER_STEP, DN_CHUNKS_PER_STEP
    n_chunks = l // CHUNK
    qkv_spec, gb_spec, state_spec, inverse_spec = _dn_chunk_specs(lambda n: n)

    def body(q_ref, k_ref, v_ref, gb_ref, o_ref, s_ref, t_ref, state):
        @pl.when(pl.program_id(1) == 0)
        def _():
            state[...] = jnp.zeros_like(state)

        cols = [slice(i * hd, (i + 1) * hd) for i in range(hps)]
        heads = [pl.program_id(0) * hps + i for i in range(hps)]
        states = [state[i] for i in range(hps)]
        for j in range(cps):
            rows = slice(j * CHUNK, (j + 1) * CHUNK)
            for i in range(hps):
                s_ref[i, j] = states[i]
            o, states, t_inv = _dn_chunk_fn(
                states, [q_ref[rows, cs] for cs in cols], [k_ref[rows, cs] for cs in cols],
                [v_ref[rows, cs] for cs in cols], gb_ref[rows, :], heads, return_inverses=True)
            for i in range(hps):
                o_ref[rows, cols[i]] = o[i]
                t_ref[i, j] = t_inv[i]
        for i in range(hps):
            state[i] = states[i]

    return pl.pallas_call(
        body, name="dn_chunk_fwd",
        out_shape=(jax.ShapeDtypeStruct((l, nh * hd), F32),
                   jax.ShapeDtypeStruct((nh, n_chunks, hd, hd), F32),
                   jax.ShapeDtypeStruct((nh, n_chunks, CHUNK, CHUNK), F32)),
        grid=(nh // hps, n_chunks // cps),
        in_specs=[qkv_spec(0), qkv_spec(1), qkv_spec(2), gb_spec],
        out_specs=(pl.BlockSpec((cps * CHUNK, hps * hd), lambda h, n: (n, h)), state_spec, inverse_spec),
        scratch_shapes=[pltpu.VMEM((hps, hd, hd), F32)],
        compiler_params=_cparams("parallel", "arbitrary"),
    )(qkv, qkv, qkv, gb)


def _dn_chunk_bwd(qkv, gb, states, inverses, do):
    l = qkv.shape[0]
    hd, nh, hps, cps = DN_HEAD_DIM, DN_HEADS, DN_HEADS_PER_STEP, DN_CHUNKS_PER_STEP
    n_steps = l // (cps * CHUNK)
    rev = lambda n: n_steps - 1 - n
    qkv_spec, gb_spec, state_spec, inverse_spec = _dn_chunk_specs(rev)

    def body(q_ref, k_ref, v_ref, gb_ref, s_ref, t_ref, do_ref, dqkv_ref, dgb_ref, dstate):
        @pl.when(pl.program_id(1) == 0)
        def _():
            dstate[...] = jnp.zeros_like(dstate)

        cols = [slice(i * hd, (i + 1) * hd) for i in range(hps)]
        heads = [pl.program_id(0) * hps + i for i in range(hps)]
        dstates = [dstate[i] for i in range(hps)]
        for j in reversed(range(cps)):
            rows = slice(j * CHUNK, (j + 1) * CHUNK)
            fn = functools.partial(_dn_chunk_fn, heads=heads, kept_inverses=[t_ref[i, j] for i in range(hps)])
            _, vjp = jax.vjp(fn, [s_ref[i, j] for i in range(hps)], [q_ref[rows, cs] for cs in cols],
                             [k_ref[rows, cs] for cs in cols], [v_ref[rows, cs] for cs in cols],
                             gb_ref[rows, :])
            dstates, dq, dk, dv, dgb = vjp(([do_ref[rows, cs] for cs in cols], dstates))
            for i in range(hps):
                dqkv_ref[0, rows, cols[i]] = dq[i]
                dqkv_ref[1, rows, cols[i]] = dk[i]
                dqkv_ref[2, rows, cols[i]] = dv[i]
            dgb_ref[rows, :] = dgb
        for i in range(hps):
            dstate[i] = dstates[i]

    head_out = pl.BlockSpec((cps * CHUNK, hps * hd), lambda h, n: (rev(n), h))
    return pl.pallas_call(
        body, name="dn_chunk_bwd",
        out_shape=(jax.ShapeDtypeStruct((3, l, nh * hd), F32),
                   jax.ShapeDtypeStruct((nh // hps, l, 128), F32)),
        grid=(nh // hps, n_steps),
        in_specs=[qkv_spec(0), qkv_spec(1), qkv_spec(2), gb_spec, state_spec, inverse_spec, head_out],
        out_specs=(pl.BlockSpec((3, cps * CHUNK, hps * hd), lambda h, n: (0, rev(n), h)),
                   pl.BlockSpec((None, cps * CHUNK, 128), lambda h, n: (h, rev(n), 0))),
        scratch_shapes=[pltpu.VMEM((hps, hd, hd), F32)],
        compiler_params=_cparams("parallel", "arbitrary"),
    )(qkv, qkv, qkv, gb, states, inverses, do)


def _dn_out_fn(o, z, w):
    return _rmsnorm(o, w) * _silu(z)


def _dn_out_fwd(o, proj, off_z, w):
    l, d = o.shape
    hd = DN_HEAD_DIM
    tr = min(4 * ROW_TILE, l)
    blk = lambda off: pl.BlockSpec((tr, hd), lambda i, h: (i, off // hd + h))

    def body(o_ref, z_ref, w_ref, out_ref):
        out_ref[...] = _dn_out_fn(o_ref[...], z_ref[...], w_ref[...]).astype(BF16)

    return pl.pallas_call(
        body, name="dn_out_fwd",
        out_shape=jax.ShapeDtypeStruct((l, d), BF16),
        grid=(l // tr, d // hd),
        in_specs=[blk(0), blk(off_z), pl.BlockSpec((1, hd), lambda i, h: (0, 0))],
        out_specs=blk(0),
        compiler_params=_cparams("parallel", "parallel"),
    )(o, proj, w)


def _dn_out_bwd(o, proj, off_z, w, dout):
    l, d = o.shape
    hd = DN_HEAD_DIM
    tr = min(4 * ROW_TILE, l)
    blk = lambda off: pl.BlockSpec((tr, hd), lambda i, h: (i, off // hd + h))
    vec = pl.BlockSpec((1, hd), lambda i, h: (0, 0))

    def body(o_ref, z_ref, w_ref, dout_ref, do_ref, dz_ref, dw_ref):
        _, vjp = jax.vjp(_dn_out_fn, o_ref[...], z_ref[...], w_ref[...])
        do, dz, dw = vjp(dout_ref[...])
        do_ref[...] = do
        dz_ref[...] = dz.astype(BF16)

        @pl.when((pl.program_id(0) == 0) & (pl.program_id(1) == 0))
        def _():
            dw_ref[...] = jnp.zeros_like(dw_ref)

        dw_ref[...] += dw

    return pl.pallas_call(
        body, name="dn_out_bwd",
        out_shape=(jax.ShapeDtypeStruct((l, d), F32), jax.ShapeDtypeStruct((l, d), BF16),
                   jax.ShapeDtypeStruct((1, hd), F32)),
        grid=(l // tr, d // hd),
        in_specs=[blk(0), blk(off_z), vec, blk(0)],
        out_specs=(blk(0), blk(0), vec),
        compiler_params=_cparams("arbitrary", "arbitrary"),
    )(o, proj, w, dout)


def _tile_2d(rows, cols, budget_bytes=1 << 20):
    for tr in (rows, 4096, 2048, 1024, 512, 256, 128, 64, 32, 16):
        if tr <= rows and rows % tr == 0 and tr * cols * 4 <= budget_bytes:
            return tr, cols
    for tc in (2048, 1024, 512, 256, 128):
        if cols % tc == 0 and rows * tc * 4 <= 2 * budget_bytes:
            return rows, tc
    raise ValueError((rows, cols))


def _adamw_update(g, w_ref, m_ref, v_ref, go_ref, d_ref, mo_ref, vo_ref):
    c1 = 1.0 / (1.0 - ADAM_B1 ** ADAM_STEP)
    c2 = 1.0 / (1.0 - ADAM_B2 ** ADAM_STEP)
    m_new = ADAM_B1 * m_ref[...] + (1.0 - ADAM_B1) * g
    v_new = ADAM_B2 * v_ref[...] + (1.0 - ADAM_B2) * (g * g)
    go_ref[...] = g
    mo_ref[...] = m_new
    vo_ref[...] = v_new
    d_ref[...] = -ADAM_LR * ((m_new * c1) / (jnp.sqrt(v_new * c2) + ADAM_EPS) + ADAM_WD * w_ref[...])


def _adamw(w, m, v, gslots, name):
    rows, cols = w.shape
    ns = gslots.shape[0]
    tr, tc = _tile_2d(rows, cols)

    def body(w_ref, m_ref, v_ref, g_ref, go_ref, d_ref, mo_ref, vo_ref):
        g = g_ref[0].astype(F32)
        for s in range(1, ns):
            g = g + g_ref[s].astype(F32)
        _adamw_update(g, w_ref, m_ref, v_ref, go_ref, d_ref, mo_ref, vo_ref)

    blk = pl.BlockSpec((tr, tc), lambda i, j: (i, j))
    o = jax.ShapeDtypeStruct((rows, cols), F32)
    return pl.pallas_call(
        body, name=name, out_shape=(o, o, o, o),
        grid=(rows // tr, cols // tc),
        in_specs=[blk, blk, blk, pl.BlockSpec((ns, tr, tc), lambda i, j: (0, i, j))],
        out_specs=(blk, blk, blk, blk),
        compiler_params=_cparams("parallel", "parallel"),
    )(w, m, v, gslots)


def _slot_sum(gslots, name):
    ns, rows, cols = gslots.shape
    tr, tc = _tile_2d(rows, cols)

    def body(g_ref, o_ref):
        g = g_ref[0]
        for s in range(1, ns):
            g = g + g_ref[s]
        o_ref[...] = g

    return pl.pallas_call(
        body, name=name, out_shape=jax.ShapeDtypeStruct((rows, cols), F32),
        grid=(rows // tr, cols // tc),
        in_specs=[pl.BlockSpec((ns, tr, tc), lambda i, j: (0, i, j))],
        out_specs=pl.BlockSpec((tr, tc), lambda i, j: (i, j)),
        compiler_params=_cparams("parallel", "parallel"),
    )(gslots)


HBM_SPEC = pl.BlockSpec(memory_space=pl.ANY)


def _all_gather(arrs, name, relayed=(), after=None):
    n = len(arrs)
    n_sems = 13
    n_in = n + (after is not None)

    def body(*refs):
        ins, outs = refs[:n], refs[n_in:n_in + n]
        send_sems, recv_sems, local_sems = refs[n_in + n:]
        x, y, c = lax.axis_index("x"), lax.axis_index("y"), lax.axis_index("c")
        me, sibling = (x, y, c), (x, y, 1 - c)
        chips = [(1 - x, y), (x, 1 - y), (1 - x, 1 - y)]
        index = lambda px, py, pc: 4 * px + 2 * py + pc

        def copy(a, k, block, to, src=None, cols=None):
            dst = outs[a].at[index(*block)]
            src = dst if src is None else src
            if cols is not None:
                dst, src = dst.at[:, cols], src.at[:, cols]
            return pltpu.make_async_remote_copy(
                src_ref=src, dst_ref=dst, send_sem=send_sems.at[a, k], recv_sem=recv_sems.at[a, k],
                device_id=to, device_id_type=MESH)

        mine = [pltpu.make_async_copy(ins[a], outs[a].at[index(*me)], local_sems.at[a])
                for a in range(n)]
        for cp in mine:
            cp.start()
        sends = []

        def start(cp):
            cp.start()
            sends.append(cp)

        halves = {a: (pl.ds(0, arrs[a].shape[1] // 2), pl.ds(arrs[a].shape[1] // 2, arrs[a].shape[1] // 2))
                  for a in relayed}
        near_x, near_y, far = [(*chip, c) for chip in chips]
        for a in range(n):
            start(copy(a, 0, me, sibling, src=ins[a]))
            if a in relayed:
                left, right = halves[a]
                for k, to, cols in ((1, near_x, left), (3, near_y, right), (2, near_x, right), (4, near_y, left)):
                    start(copy(a, k, me, to, src=ins[a], cols=cols))
            else:
                for j, chip in enumerate(chips):
                    start(copy(a, 1 + j, me, (*chip, c), src=ins[a]))
        for a in relayed:
            left, right = halves[a]
            for k, block, cols, onward, to_sibling in (
                    (1, near_x, left, (5, near_y), 7), (3, near_y, right, (6, near_x), 10),
                    (2, near_x, right, None, 8), (4, near_y, left, None, 9),
                    (5, far, left, None, 11), (6, far, right, None, 12)):
                copy(a, k, block, me, cols=cols).wait_recv()
                if onward is not None:
                    start(copy(a, onward[0], block, onward[1], cols=cols))
                start(copy(a, to_sibling, block, sibling, cols=cols))
        for j, chip in enumerate(chips):
            for a in range(n):
                if a not in relayed:
                    copy(a, 1 + j, (*chip, c), me).wait_recv()
                    start(copy(a, 4 + j, (*chip, c), sibling))
        for a in range(n):
            copy(a, 0, sibling, me).wait_recv()
            if a in relayed:
                left, right = halves[a]
                for k, chip, cols in ((7, chips[0], left), (8, chips[0], right), (9, chips[1], left),
                                      (10, chips[1], right), (11, chips[2], left), (12, chips[2], right)):
                    copy(a, k, (*chip, 1 - c), me, cols=cols).wait_recv()
            else:
                for j, chip in enumerate(chips):
                    copy(a, 4 + j, (*chip, 1 - c), me).wait_recv()
        for cp in sends:
            cp.wait_send()
        for cp in mine:
            cp.wait()

    return pl.pallas_call(
        body, name=name,
        out_shape=[jax.ShapeDtypeStruct((N_DEV,) + a.shape, a.dtype) for a in arrs],
        in_specs=[HBM_SPEC] * n_in, out_specs=[HBM_SPEC] * n,
        scratch_shapes=[pltpu.SemaphoreType.DMA((n, n_sems)), pltpu.SemaphoreType.DMA((n, n_sems)),
                        pltpu.SemaphoreType.DMA((n,))],
    )(*arrs, *([after] if after is not None else []))


def _sibling_swap(arrs, name):
    n = len(arrs)

    def body(*refs):
        ins, outs = refs[:n], refs[n:2 * n]
        send_sems, recv_sems = refs[2 * n:]
        x, y, c = lax.axis_index("x"), lax.axis_index("y"), lax.axis_index("c")
        copies = [pltpu.make_async_remote_copy(
            src_ref=ins[a].at[:, 1 - c], dst_ref=outs[a],
            send_sem=send_sems.at[a], recv_sem=recv_sems.at[a],
            device_id=(x, y, 1 - c), device_id_type=MESH) for a in range(n)]
        for cp in copies:
            cp.start()
        for cp in copies:
            cp.wait()

    return pl.pallas_call(
        body, name=name,
        out_shape=[jax.ShapeDtypeStruct(a.shape[:1] + a.shape[2:], a.dtype) for a in arrs],
        in_specs=[HBM_SPEC] * n, out_specs=[HBM_SPEC] * n,
        scratch_shapes=[pltpu.SemaphoreType.DMA((n,)), pltpu.SemaphoreType.DMA((n,))],
    )(*arrs)


def _pair_sum(mine, theirs, core, name):
    chips, _, rows, cols = mine.shape
    tr, tc = _tile_2d(rows, cols, budget_bytes=2 << 20)

    def body(core_ref, a_ref, b_ref, o_ref):
        o_ref[...] = (a_ref[...].astype(F32) + b_ref[...].astype(F32)).astype(o_ref.dtype)

    slab = pl.BlockSpec((None, tr, tc), lambda ch, i, j, core_ref: (ch, i, j))
    return pl.pallas_call(
        body, name=name, out_shape=jax.ShapeDtypeStruct((chips, rows, cols), mine.dtype),
        grid_spec=pltpu.PrefetchScalarGridSpec(
            num_scalar_prefetch=1, grid=(chips, rows // tr, cols // tc),
            in_specs=[pl.BlockSpec((None, None, tr, tc),
                                   lambda ch, i, j, core_ref: (ch, core_ref[0], i, j)), slab],
            out_specs=slab),
        compiler_params=_cparams("parallel", "parallel", "parallel"),
    )(core, mine, theirs)


HBM_ONLY = pl.BlockSpec(memory_space=pltpu.HBM)
SEM_SPEC = pl.BlockSpec(memory_space=pltpu.SEMAPHORE)
SPLIT_COPY_EFFECT = pltpu.SideEffectType.DATAFLOW_SIDE_EFFECTING


def _flip(v, bit):
    return 1 - v if bit else v


def _chip_slices_plan(n):
    def plan():
        x, y, c = lax.axis_index("x"), lax.axis_index("y"), lax.axis_index("c")
        copies = []
        for k in range(1, 4):
            px, py = _flip(x, k & 2), _flip(y, k & 1)
            copies += [(a, 2 * px + py, 2 * x + y, (px, py, c)) for a in range(n)]
        return copies
    return plan, 3 * n


def _gather_plan(n):
    def plan():
        x, y, c = lax.axis_index("x"), lax.axis_index("y"), lax.axis_index("c")
        copies = []
        for k in range(1, N_DEV):
            peer = (_flip(x, k & 4), _flip(y, k & 2), _flip(c, k & 1))
            copies += [(a, None, 4 * x + 2 * y + c, peer) for a in range(n)]
        return copies + [(a, None, 4 * x + 2 * y + c, None) for a in range(n)]
    return plan, 8 * n


def _planned_copies(plan, srcs, lands, send_sems, recv_sems):
    copies = []
    for i, (a, src_at, land_at, peer) in enumerate(plan()):
        src, dst = srcs[a] if src_at is None else srcs[a].at[src_at], lands[a].at[land_at]
        if peer is None:
            local = pltpu.make_async_copy(src, dst, send_sems[i])
            copies.append((local, local.wait))
        else:
            remote = pltpu.make_async_remote_copy(src_ref=src, dst_ref=dst, send_sem=send_sems[i],
                                                  recv_sem=recv_sems[i], device_id=peer, device_id_type=MESH)
            copies.append((remote, remote.wait))
    return copies


def _split_exchange_start(plan_and_count, arrs, land_shapes, name, after=None):
    plan, n_sems = plan_and_count
    n = len(arrs)

    n_in = 2 * n + (after is not None)

    def body(*refs):
        srcs, lands = refs[:n], refs[n:2 * n]
        send_sems, recv_sems = refs[n_in:n_in + n_sems], refs[n_in + n_sems:n_in + 2 * n_sems]
        token = refs[-1]
        for copy, _ in _planned_copies(plan, srcs, lands, send_sems, recv_sems):
            copy.start()
        token[...] = jnp.zeros_like(token)

    hbm = lambda a: pltpu.HBM(a.shape, a.dtype)
    operands = [pltpu.with_memory_space_constraint(a, pltpu.HBM) for a in arrs]
    operands += [pltpu.with_memory_space_constraint(lax.empty(shape, a.dtype), pltpu.HBM)
                 for a, shape in zip(arrs, land_shapes)]
    out = pl.pallas_call(
        body, name=name,
        out_shape=(*[pltpu.SemaphoreType.DMA(())] * (2 * n_sems),
                   *[hbm(a) for a in operands],
                   jax.ShapeDtypeStruct((8, 128), F32)),
        in_specs=[HBM_ONLY] * (2 * n) + [pl.BlockSpec(memory_space=pl.ANY)] * (after is not None),
        out_specs=(*[SEM_SPEC] * (2 * n_sems), *[HBM_ONLY] * (2 * n),
                   pl.BlockSpec(memory_space=pltpu.VMEM)),
        input_output_aliases={i: 2 * n_sems + i for i in range(2 * n)},
        compiler_params=pltpu.CompilerParams(has_side_effects=SPLIT_COPY_EFFECT),
    )(*operands, *([after] if after is not None else []))
    sems, rest = list(out[:2 * n_sems]), out[2 * n_sems:]
    return sems, list(rest[:n]), list(rest[n:2 * n]), rest[-1]


def _split_exchange_wait(plan_and_count, sems, srcs, lands, after, name):
    plan, n_sems = plan_and_count
    n = len(srcs)

    def body(*refs):
        src_refs, land_refs = refs[:n], refs[n:2 * n]
        send_sems, recv_sems = refs[2 * n:2 * n + n_sems], refs[2 * n + n_sems:2 * n + 2 * n_sems]
        for _, wait in _planned_copies(plan, src_refs, land_refs, send_sems, recv_sems):
            wait()

    hbm = lambda a: pltpu.HBM(a.shape, a.dtype)
    out = pl.pallas_call(
        body, name=name,
        out_shape=(*[hbm(a) for a in srcs], *[hbm(a) for a in lands]),
        in_specs=[HBM_ONLY] * (2 * n) + [SEM_SPEC] * (2 * n_sems) + [pl.BlockSpec(memory_space=pl.ANY)],
        out_specs=tuple([HBM_ONLY] * (2 * n)),
        input_output_aliases={i: i for i in range(2 * n)},
        compiler_params=pltpu.CompilerParams(has_side_effects=SPLIT_COPY_EFFECT),
    )(*srcs, *lands, *sems, after)
    return list(out[n:])


def _adamw_exchanged(w, m, v, own, landed, chip, name):
    rows, cols = w.shape
    tr, tc = _tile_2d(rows, cols)

    def body(chip_ref, w_ref, m_ref, v_ref, own_ref, l1_ref, l2_ref, l3_ref, go_ref, d_ref, mo_ref, vo_ref):
        g = own_ref[...].astype(F32)
        for ref in (l1_ref, l2_ref, l3_ref):
            g = g + ref[...].astype(F32)
        _adamw_update(g, w_ref, m_ref, v_ref, go_ref, d_ref, mo_ref, vo_ref)

    blk = pl.BlockSpec((tr, tc), lambda i, j, chip_ref: (i, j))
    slot = lambda k: pl.BlockSpec((None, tr, tc), lambda i, j, chip_ref: (chip_ref[0] ^ k, i, j))
    o = jax.ShapeDtypeStruct((rows, cols), F32)
    return pl.pallas_call(
        body, name=name, out_shape=(o, o, o, o),
        grid_spec=pltpu.PrefetchScalarGridSpec(
            num_scalar_prefetch=1, grid=(rows // tr, cols // tc),
            in_specs=[blk, blk, blk, slot(0), slot(1), slot(2), slot(3)],
            out_specs=(blk, blk, blk, blk)),
        compiler_params=_cparams("parallel", "parallel"),
    )(chip, w, m, v, own, landed, landed, landed)


def _block_diag(t):
    nb, gpb, r, c = t.shape
    eye = jnp.eye(gpb, dtype=t.dtype)
    return jnp.einsum("ngrc,gh->ngrhc", t, eye).reshape(nb, gpb * r, gpb * c)


def _diag_blocks(t, r, c):
    nb = t.shape[0]
    gpb = t.shape[1] // r
    t = t.reshape(nb, gpb, r, gpb, c)
    return jnp.einsum("ngrhc,gh->ngrc", t, jnp.eye(gpb, dtype=t.dtype))


def _pack_rows(parts):
    flat = jnp.concatenate([p.reshape(-1).astype(F32) for p in parts])
    pad = (-flat.shape[0]) % (256 * 128)
    return jnp.pad(flat, (0, pad)).reshape(-1, 128)


def _unpack_rows(packed, shapes):
    flat = packed.reshape(-1)
    out, at = [], 0
    for shape in shapes:
        size = math.prod(shape)
        out.append(flat[at:at + size].reshape(shape))
        at += size
    return out


def kernel(x, ln_w, w_in, s5_lam_re, s5_lam_im, s5_log_step, s5_b_re, s5_b_im, s5_c_re, s5_c_im, s5_d, s5_w_glu, s5_w_up, dn_conv_w, dn_a_log, dn_dt_bias, dn_norm_w, dn_w_up, w_out, final_norm_w, loss_target, m_ln_w, m_w_in, m_s5_lam_re, m_s5_lam_im, m_s5_log_step, m_s5_b_re, m_s5_b_im, m_s5_c_re, m_s5_c_im, m_s5_d, m_s5_w_glu, m_s5_w_up, m_dn_conv_w, m_dn_a_log, m_dn_dt_bias, m_dn_norm_w, m_dn_w_up, m_w_out, m_final_norm_w, v_ln_w, v_w_in, v_s5_lam_re, v_s5_lam_im, v_s5_log_step, v_s5_b_re, v_s5_b_im, v_s5_c_re, v_s5_c_im, v_s5_d, v_s5_w_glu, v_s5_w_up, v_dn_conv_w, v_dn_a_log, v_dn_dt_bias, v_dn_norm_w, v_dn_w_up, v_w_out, v_final_norm_w):
    weights = dict(ln_w=ln_w, w_in=w_in, s5_lam_re=s5_lam_re, s5_lam_im=s5_lam_im,
                   s5_log_step=s5_log_step, s5_b_re=s5_b_re, s5_b_im=s5_b_im, s5_c_re=s5_c_re,
                   s5_c_im=s5_c_im, s5_d=s5_d, s5_w_glu=s5_w_glu, s5_w_up=s5_w_up,
                   dn_conv_w=dn_conv_w, dn_a_log=dn_a_log, dn_dt_bias=dn_dt_bias,
                   dn_norm_w=dn_norm_w, dn_w_up=dn_w_up, w_out=w_out, final_norm_w=final_norm_w)
    mom_m = dict(ln_w=m_ln_w, w_in=m_w_in, s5_lam_re=m_s5_lam_re, s5_lam_im=m_s5_lam_im,
                 s5_log_step=m_s5_log_step, s5_b_re=m_s5_b_re, s5_b_im=m_s5_b_im,
                 s5_c_re=m_s5_c_re, s5_c_im=m_s5_c_im, s5_d=m_s5_d, s5_w_glu=m_s5_w_glu,
                 s5_w_up=m_s5_w_up, dn_conv_w=m_dn_conv_w, dn_a_log=m_dn_a_log,
                 dn_dt_bias=m_dn_dt_bias, dn_norm_w=m_dn_norm_w, dn_w_up=m_dn_w_up,
                 w_out=m_w_out, final_norm_w=m_final_norm_w)
    mom_v = dict(ln_w=v_ln_w, w_in=v_w_in, s5_lam_re=v_s5_lam_re, s5_lam_im=v_s5_lam_im,
                 s5_log_step=v_s5_log_step, s5_b_re=v_s5_b_re, s5_b_im=v_s5_b_im,
                 s5_c_re=v_s5_c_re, s5_c_im=v_s5_c_im, s5_d=v_s5_d, s5_w_glu=v_s5_w_glu,
                 s5_w_up=v_s5_w_up, dn_conv_w=v_dn_conv_w, dn_a_log=v_dn_a_log,
                 dn_dt_bias=v_dn_dt_bias, dn_norm_w=v_dn_norm_w, dn_w_up=v_dn_w_up,
                 w_out=v_w_out, final_norm_w=v_final_norm_w)
    names = list(weights)

    l, d = x.shape[1], x.shape[2]
    d_s5 = d // 2
    groups = d_s5 // S5_GROUP
    nb = groups // S5_GPB
    d_dn = DN_HEADS * DN_HEAD_DIM
    w_in_cols = w_in.shape[2]
    d_in = N_DEV * w_in_cols
    off_ba_src = 2 * d_s5 + 4 * d_dn
    off_u, off_zs, off_qkv, off_zd = 0, d_s5, 2 * d_s5, 2 * d_s5 + 3 * d_dn
    off_ba = off_zd + d_dn
    n_main = off_ba + BA_PAD
    off_gs, off_gd = 0, d
    x2d, tgt2d = x[0], loss_target[0]
    my_index = 4 * lax.axis_index("x") + 2 * lax.axis_index("y") + lax.axis_index("c")

    g_win, g_conv = _all_gather([jnp.transpose(w_in[0]).astype(BF16), dn_conv_w[0]], name="gather_weights",
                                relayed=(0,))
    late_plan = _gather_plan(4)
    late_shards = [s5_w_glu[0].astype(BF16), s5_w_up[0].astype(BF16), dn_w_up[0].astype(BF16),
                   w_out[0].astype(BF16)]
    late_sems, late_shards, late_lands, late_token = _split_exchange_start(
        late_plan, late_shards, [(N_DEV,) + s.shape for s in late_shards], name="gather_late_start",
        after=g_conv)
    ba_end = off_ba_src + 2 * DN_HEADS
    w_full_t = g_win.reshape(d_in, d)
    w_gates_t = w_full_t[ba_end:]
    conv_full = jnp.transpose(g_conv, (1, 0, 2)).reshape(CONV_K, 3 * d_dn)

    lam_re, lam_im = s5_lam_re[0], s5_lam_im[0]
    log_step = s5_log_step[0].reshape(groups, 1)
    b_re = s5_b_re[0].reshape(groups * S5_STATE, S5_GROUP)
    b_im = s5_b_im[0].reshape(groups * S5_STATE, S5_GROUP)
    abar_re, abar_im, f_re, f_im = _s5_disc_fwd(lam_re, lam_im, log_step)
    f_re_col, f_im_col = f_re.reshape(-1, 1), f_im.reshape(-1, 1)
    bb_re, bb_im = _s5_bbar_fwd(f_re_col, f_im_col, b_re, b_im)

    def bb_blocks(t):
        t = t.reshape(nb, S5_GPB, S5_STATE, S5_GROUP).transpose(0, 1, 3, 2)
        return _block_diag(t).astype(BF16)

    def c_blocks(t):
        return _block_diag(t.reshape(nb, S5_GPB, S5_GROUP, S5_STATE)).astype(BF16)

    bbr, bbi = bb_blocks(bb_re), bb_blocks(bb_im)
    cbr, cbi = c_blocks(s5_c_re[0]), c_blocks(s5_c_im[0])
    ctr, cti = jnp.transpose(cbr, (0, 2, 1)), jnp.transpose(cbi, (0, 2, 1))
    a_re = abar_re.reshape(nb, 1, S5_GPB * S5_STATE)
    a_im = abar_im.reshape(nb, 1, S5_GPB * S5_STATE)

    h = _rms_fwd(x2d, ln_w)
    proj = _mm(h, w_full_t, tb=True, b_rows=n_main, tm=1024, tn=n_main // 4, after=late_token, name="proj")
    proj_gates = _mm(h, w_gates_t, tb=True, tm=1024, tn=1024, name="proj_gates")
    y1, car_r, car_i, states_r, states_i = _s5_fwd(proj, bbr, bbi, a_re, a_im, ctr, cti, s5_d, d_s5)
    a_log_row = jnp.pad(dn_a_log, ((0, 0), (DN_HEADS, 128 - 2 * DN_HEADS)))
    dt_row = jnp.pad(dn_dt_bias, ((0, 0), (DN_HEADS, 128 - 2 * DN_HEADS)))
    qkv = _dn_prep_fwd(proj, off_qkv, conv_full)
    gb = _dn_gates_fwd(proj, off_ba, a_log_row, dt_row)
    o_dn, states, inverses = _dn_chunk_fwd(qkv, gb)

    g_glu, g_sup, g_dup, g_wout = _split_exchange_wait(late_plan, late_sems, late_shards, late_lands, o_dn,
                                                       name="gather_late_wait")
    wglu_full = g_glu.reshape(d_s5, d_s5)
    wsup_full = jnp.transpose(g_sup, (1, 0, 2)).reshape(d_s5, d)
    wdup_full = jnp.transpose(g_dup, (1, 0, 2)).reshape(d_dn, d)
    wout_full = g_wout.reshape(d, d)

    out_s = _s5_glu_fwd(y1, proj, off_zs, wglu_full)
    y_s = _mm(out_s, wsup_full, name="s5_up")
    out_d = _dn_out_fwd(o_dn, proj, off_zd, dn_norm_w)
    y_d = _mm(out_d, wdup_full, name="dn_up")

    mixed = _merge_fwd(proj_gates, off_gs, off_gd, y_s, y_d)
    branch = _mm(mixed, wout_full, name="w_out")
    dx2, dx2_bf, loss_dev, d_final_w = _final(x2d, branch, final_norm_w.reshape(1, d), tgt2d)

    g_wout_full = _mm(mixed, dx2_bf, ta=True, out_dtype=BF16, name="grad_w_out")
    dmixed = _mm(dx2_bf, wout_full, tb=True, name="d_mixed")
    dgs, dgd, dys, dyd = _merge_bwd(proj_gates, off_gs, off_gd, y_s, y_d, dmixed)

    g_dup_full = _mm(out_d, dyd, ta=True, out_dtype=BF16, name="grad_dn_up")
    dout_d = _mm(dyd, wdup_full, tb=True, name="d_out_d")
    do_dn, dzd, d_norm_w = _dn_out_bwd(o_dn, proj, off_zd, dn_norm_w, dout_d)
    dqkv, dgb_heads = _dn_chunk_bwd(qkv, gb, states, inverses, do_dn)
    dba, d_a_log_row, d_dt_row = _dn_gates_bwd(proj, off_ba, a_log_row, dt_row, dgb_heads)
    dqkv_pre, d_conv_full = _dn_prep_bwd(proj, off_qkv, conv_full, dqkv)

    g_sup_full = _mm(out_s, dys, ta=True, out_dtype=BF16, name="grad_s5_up")
    dout_s = _mm(dys, wsup_full, tb=True, name="d_out_s")
    dy1, dzs, g_glu_full = _s5_glu_bwd(y1, proj, off_zs, wglu_full, dout_s)

    def by_dest(t, axis=0):
        if axis == 1:
            return t.reshape(t.shape[0], 4, 2, t.shape[1] // N_DEV).transpose(1, 2, 0, 3)
        return t.reshape(4, 2, t.shape[0] // N_DEV, t.shape[1])

    core = lax.axis_index("c").astype(jnp.int32).reshape(1)
    chip = (2 * lax.axis_index("x") + lax.axis_index("y")).astype(jnp.int32).reshape(1)

    def chip_sums_of(which, parts, tag):
        from_sibling = _sibling_swap(parts, name="swap_grads_" + tag)
        return [_pair_sum(p, got, core, name="pair_sum_" + nm)
                for nm, p, got in zip(which, parts, from_sibling)]

    early = ["s5_w_glu", "s5_w_up", "dn_w_up", "w_out"]
    sums_a = chip_sums_of(early, [by_dest(g_glu_full.astype(BF16)), by_dest(g_sup_full, 1),
                                  by_dest(g_dup_full, 1), by_dest(g_wout_full)], "a")
    plan_a = _chip_slices_plan(len(sums_a))
    sems_a, src_a, land_a, token_a = _split_exchange_start(
        plan_a, sums_a, [t.shape for t in sums_a], name="exchange_start_a")

    (du, d_a_re, d_a_im, d_bbr, d_bbi, d_cbr, d_cbi, d_s5_d) = _s5_bwd(
        proj, dy1, bbr, bbi, a_re, a_im, cbr, cbi, s5_d + token_a[:1, :1], car_r, car_i,
        states_r, states_i)

    dproj = jnp.concatenate([du, dzs, dqkv_pre, dzd, jnp.pad(dba, ((0, 0), (0, BA_PAD - 128)))], axis=1)
    dproj_gates = jnp.concatenate([dgs, dgd], axis=1)
    g_main_t = _mm(dproj, h, ta=True, out_dtype=BF16, tm=512, tn=d, name="grad_w_in")
    g_gates_t = _mm(dproj_gates, h, ta=True, out_dtype=BF16, tm=512, tn=d, name="grad_w_in_gates")
    g_win_full_t = jnp.concatenate([g_main_t[:ba_end], g_gates_t], axis=0)
    sums_b = chip_sums_of(["w_in"], [by_dest(g_win_full_t)], "b")
    plan_b = _chip_slices_plan(1)
    sems_b, src_b, land_b, token_b = _split_exchange_start(
        plan_b, sums_b, [t.shape for t in sums_b], name="exchange_start_b")
    dh_main = _mm(dproj, w_full_t, b_rows=n_main, tm=1024, tn=1024, tk=n_main // 4, after=token_b,
                  name="d_h_main")
    dh = _mm(dproj_gates, w_gates_t, tm=1024, tn=1024, tk=2048, addend=dh_main, name="d_h")
    grad_x, d_ln_w = _rms_bwd(x2d, ln_w, dh, dx2)
    big = ["w_in"] + early
    results = {}

    def from_bb_blocks(t):
        t = _diag_blocks(t, S5_GROUP, S5_STATE).transpose(0, 1, 3, 2)
        return t.reshape(groups * S5_STATE, S5_GROUP)

    d_f_re, d_f_im, d_b_re, d_b_im = _s5_bbar_bwd(f_re_col + token_b[:1, :1], f_im_col, b_re, b_im,
                                                 from_bb_blocks(d_bbr), from_bb_blocks(d_bbi))
    d_lam_re, d_lam_im, d_log_step = _s5_disc_bwd(
        lam_re, lam_im, log_step, d_a_re.reshape(groups, S5_STATE), d_a_im.reshape(groups, S5_STATE),
        d_f_re.reshape(groups, S5_STATE), d_f_im.reshape(groups, S5_STATE))
    d_c_re = _diag_blocks(d_cbr + token_b[:1, :1], S5_GROUP, S5_STATE).reshape(groups, S5_GROUP, S5_STATE)
    d_c_im = _diag_blocks(d_cbi + token_b[:1, :1], S5_GROUP, S5_STATE).reshape(groups, S5_GROUP, S5_STATE)

    land_a = _split_exchange_wait(plan_a, sems_a, src_a, land_a, grad_x, name="exchange_wait_a")
    for nm, own, landed in zip(early, src_a, land_a):
        results[nm] = _adamw_exchanged(weights[nm][0], mom_m[nm][0], mom_v[nm][0], own, landed, chip,
                                       name="adamw_" + nm)

    small = [nm for nm in names if nm not in big]
    small_grads = dict(
        ln_w=d_ln_w, s5_lam_re=d_lam_re, s5_lam_im=d_lam_im, s5_log_step=d_log_step,
        s5_b_re=d_b_re, s5_b_im=d_b_im, s5_c_re=d_c_re, s5_c_im=d_c_im, s5_d=d_s5_d,
        dn_conv_w=d_conv_full, dn_a_log=d_a_log_row[:, DN_HEADS:2 * DN_HEADS],
        dn_dt_bias=d_dt_row[:, DN_HEADS:2 * DN_HEADS], dn_norm_w=d_norm_w, final_norm_w=d_final_w)
    (all_small,) = _all_gather([_pack_rows([small_grads[nm] for nm in small])], name="gather_small_grads",
                               after=results[early[-1]][0])
    summed = _slot_sum(all_small, name="sum_small_grads")
    full_shapes = [(CONV_K, 3 * d_dn) if nm == "dn_conv_w" else weights[nm].shape for nm in small]
    g_small = dict(zip(small, _unpack_rows(summed, full_shapes)))
    conv_cols = dn_conv_w.shape[2]
    g_small["dn_conv_w"] = lax.dynamic_slice_in_dim(
        g_small["dn_conv_w"], my_index * conv_cols, conv_cols, axis=1).reshape(dn_conv_w.shape)
    packed = [_pack_rows([t[nm] for nm in small]) for t in (weights, mom_m, mom_v, g_small)]
    small_out = _adamw(packed[0], packed[1], packed[2], packed[3][None], name="adamw_small")
    small_shapes = [weights[nm].shape for nm in small]
    for kind, packed_out in enumerate(small_out):
        for nm, val in zip(small, _unpack_rows(packed_out, small_shapes)):
            results.setdefault(nm, [None] * 4)[kind] = val

    (land_b,) = _split_exchange_wait(plan_b, sems_b, src_b, land_b, small_out[0], name="exchange_wait_b")
    res = _adamw_exchanged(jnp.transpose(w_in[0]), jnp.transpose(m_w_in[0]), jnp.transpose(v_w_in[0]),
                           src_b[0], land_b, chip, name="adamw_w_in")
    results["w_in"] = [jnp.transpose(t) for t in res]

    loss = lax.psum(loss_dev[0, 0], ("x", "y", "c"))
    outs = [loss, grad_x[None]]
    for kind in range(4):
        outs += [results[nm][kind].reshape(weights[nm].shape) for nm in names]
    return tuple(outs)
```

```python
import functools
import math

import jax
import jax.numpy as jnp
from jax import lax
from jax.experimental import pallas as pl
from jax.experimental.pallas import tpu as pltpu

F32 = jnp.float32
BF16 = jnp.bfloat16
HIGHEST = lax.Precision.HIGHEST
MESH = pl.DeviceIdType.MESH
N_DEV = 8

EPS = 1e-6
S5_GROUP = 16
S5_STATE = 64
S5_GPB = 8
S5_T = 1024
DN_HEADS = 8
DN_HEAD_DIM = 128
CHUNK = 64
DN_HEADS_PER_STEP = 8
DN_CHUNKS_PER_STEP = 4
CONV_K = 4
BA_PAD = 512

ADAM_LR = 0.001
ADAM_B1 = 0.9
ADAM_B2 = 0.999
ADAM_EPS = 1e-08
ADAM_WD = 0.01
ADAM_STEP = 10

VMEM_LIMIT_BYTES = 48 * 1024 * 1024
ROW_TILE = 256


def _cparams(*sem):
    return pltpu.CompilerParams(dimension_semantics=sem if sem else None,
                                vmem_limit_bytes=VMEM_LIMIT_BYTES)


@jax.custom_jvp
def _sigmoid(x):
    return 1.0 / (1.0 + jnp.exp(-x))


@_sigmoid.defjvp
def _sigmoid_jvp(primals, tangents):
    s = _sigmoid(primals[0])
    return s, tangents[0] * (s * (1.0 - s))


def _silu(x):
    return x * _sigmoid(x)


def _gelu(x):
    return 0.5 * x * (1.0 + jnp.tanh(0.7978845608028654 * (x + 0.044715 * x * x * x)))


def _softplus(x):
    return jnp.maximum(x, 0.0) + jnp.log(1.0 + jnp.exp(-jnp.abs(x)))


def _rmsnorm(x, w):
    return x * lax.rsqrt(jnp.mean(x * x, axis=-1, keepdims=True) + EPS) * w


def _dot(a, b, dims=((1,), (0,)), precision=None):
    return lax.dot_general(a, b, (dims, ((), ())), precision=precision,
                           preferred_element_type=F32)


def _bdot(a, b, dims=((1,), (0,))):
    return _dot(a.astype(BF16), b.astype(BF16), dims)


def _split_bf16(a):
    hi = a.astype(BF16)
    return hi, (a - hi.astype(F32)).astype(BF16)


def _dot3_dims(a, b, dims):
    ah, al = _split_bf16(a)
    bh, bl = _split_bf16(b)
    return _dot(ah, bh, dims) + (_dot(ah, bl, dims) + _dot(al, bh, dims))


@jax.custom_vjp
def _dot3(a, b):
    return _dot3_dims(a, b, ((1,), (0,)))


def _dot3_fwd(a, b):
    return _dot3(a, b), (a, b)


def _dot3_bwd(res, g):
    a, b = res
    return _dot3_dims(g, b, ((1,), (1,))), _dot3_dims(a, g, ((0,), (0,)))


_dot3.defvjp(_dot3_fwd, _dot3_bwd)


def _mm(a, b, *, ta=False, tb=False, out_dtype=F32, tm=512, tn=512, tk=None, after=None, b_rows=None,
        addend=None, name):
    k_dim, m_dim = (a.shape if ta else a.shape[::-1])
    b_rows = b.shape[0] if b_rows is None else b_rows
    n_dim = b_rows if tb else b.shape[1]
    assert (b.shape[1] if tb else b_rows) == k_dim and b_rows <= b.shape[0]
    tm, tn = min(tm, m_dim), min(tn, n_dim)
    tk = k_dim if tk is None else tk
    assert m_dim % tm == 0 and n_dim % tn == 0 and k_dim % tk == 0
    nk = k_dim // tk
    a_spec = (pl.BlockSpec((tk, tm), lambda i, j, k: (k, i)) if ta
              else pl.BlockSpec((tm, tk), lambda i, j, k: (i, k)))
    b_spec = (pl.BlockSpec((tn, tk), lambda i, j, k: (j, k)) if tb
              else pl.BlockSpec((tk, tn), lambda i, j, k: (k, j)))
    dims = ((0 if ta else 1,), (1 if tb else 0,))

    extras = ([after] if after is not None else []) + ([addend] if addend is not None else [])
    extra_specs = ([pl.BlockSpec((8, 128), lambda i, j, k: (0, 0))] if after is not None else []) + (
        [pl.BlockSpec((tm, tn), lambda i, j, k: (i, j))] if addend is not None else [])

    def body(a_ref, b_ref, *rest):
        o_ref, *scratch = rest[len(extras):]
        p = _bdot(a_ref[...], b_ref[...], dims)
        finish = (lambda v: v + rest[len(extras) - 1][...]) if addend is not None else (lambda v: v)
        if nk == 1:
            o_ref[...] = finish(p).astype(o_ref.dtype)
        else:
            acc = scratch[0]
            k = pl.program_id(2)

            @pl.when(k == 0)
            def _():
                acc[...] = p

            @pl.when(k > 0)
            def _():
                acc[...] += p

            @pl.when(k == nk - 1)
            def _():
                o_ref[...] = finish(acc[...]).astype(o_ref.dtype)

    return pl.pallas_call(
        body, name=name,
        out_shape=jax.ShapeDtypeStruct((m_dim, n_dim), out_dtype),
        grid=(m_dim // tm, n_dim // tn, nk),
        in_specs=[a_spec, b_spec] + extra_specs,
        out_specs=pl.BlockSpec((tm, tn), lambda i, j, k: (i, j)),
        scratch_shapes=[pltpu.VMEM((tm, tn), F32)] if nk > 1 else [],
        compiler_params=_cparams("parallel", "parallel", "arbitrary"),
    )(a, b, *extras)


def _rms_fwd(x, w):
    l, d = x.shape

    def body(x_ref, w_ref, h_ref):
        h_ref[...] = _rmsnorm(x_ref[...], w_ref[...]).astype(BF16)

    return pl.pallas_call(
        body, name="rms_fwd",
        out_shape=jax.ShapeDtypeStruct((l, d), BF16),
        grid=(l // ROW_TILE,),
        in_specs=[pl.BlockSpec((ROW_TILE, d), lambda i: (i, 0)),
                  pl.BlockSpec((1, d), lambda i: (0, 0))],
        out_specs=pl.BlockSpec((ROW_TILE, d), lambda i: (i, 0)),
        compiler_params=_cparams("parallel"),
    )(x, w)


def _rms_bwd(x, w, dh, dres):
    l, d = x.shape

    def body(x_ref, w_ref, dh_ref, dres_ref, dx_ref, dw_ref):
        _, vjp = jax.vjp(_rmsnorm, x_ref[...], w_ref[...])
        dx, dw = vjp(dh_ref[...])
        dx_ref[...] = dx + dres_ref[...]

        @pl.when(pl.program_id(0) == 0)
        def _():
            dw_ref[...] = jnp.zeros_like(dw_ref)

        dw_ref[...] += dw

    row = pl.BlockSpec((ROW_TILE, d), lambda i: (i, 0))
    vec = pl.BlockSpec((1, d), lambda i: (0, 0))
    return pl.pallas_call(
        body, name="rms_bwd",
        out_shape=(jax.ShapeDtypeStruct((l, d), F32), jax.ShapeDtypeStruct((1, d), F32)),
        grid=(l // ROW_TILE,),
        in_specs=[row, vec, row, row],
        out_specs=(row, vec),
        compiler_params=_cparams("arbitrary"),
    )(x, w, dh, dres)


def _final(x, r, fw, target):
    l, d = x.shape

    def per_row_loss(x2, w, tgt):
        err = _rmsnorm(x2, w) - tgt
        return 0.5 * jnp.mean(err * err, axis=-1, keepdims=True)

    def body(x_ref, r_ref, w_ref, t_ref, dx_ref, dxb_ref, loss_ref, dw_ref):
        x2 = x_ref[...] + r_ref[...]
        rows, vjp = jax.vjp(functools.partial(per_row_loss, tgt=t_ref[...]), x2, w_ref[...])
        dx2, dw = vjp(jnp.ones_like(rows))
        dx_ref[...] = dx2
        dxb_ref[...] = dx2.astype(BF16)

        @pl.when(pl.program_id(0) == 0)
        def _():
            dw_ref[...] = jnp.zeros_like(dw_ref)
            loss_ref[...] = jnp.zeros_like(loss_ref)

        dw_ref[...] += dw
        loss_ref[...] += jnp.sum(rows, axis=0, keepdims=True)

    row = pl.BlockSpec((ROW_TILE, d), lambda i: (i, 0))
    vec = pl.BlockSpec((1, d), lambda i: (0, 0))
    return pl.pallas_call(
        body, name="final_norm_loss",
        out_shape=(jax.ShapeDtypeStruct((l, d), F32), jax.ShapeDtypeStruct((l, d), BF16),
                   jax.ShapeDtypeStruct((1, 1), F32), jax.ShapeDtypeStruct((1, d), F32)),
        grid=(l // ROW_TILE,),
        in_specs=[row, row, vec, row],
        out_specs=(row, row, pl.BlockSpec((1, 1), lambda i: (0, 0)), vec),
        compiler_params=_cparams("arbitrary"),
    )(x, r, fw, target)


def _merge_fn(gs, gd, ys, yd):
    return _sigmoid(gs) * ys + _sigmoid(gd) * yd


def _merge_fwd(proj, off_gs, off_gd, ys, yd):
    l, d = ys.shape
    cw = min(1024, d)
    blk = lambda off: pl.BlockSpec((ROW_TILE, cw), lambda i, j: (i, off // cw + j))

    def body(gs_ref, gd_ref, ys_ref, yd_ref, o_ref):
        o_ref[...] = _merge_fn(gs_ref[...], gd_ref[...], ys_ref[...], yd_ref[...]).astype(BF16)

    return pl.pallas_call(
        body, name="merge_fwd",
        out_shape=jax.ShapeDtypeStruct((l, d), BF16),
        grid=(l // ROW_TILE, d // cw),
        in_specs=[blk(off_gs), blk(off_gd), blk(0), blk(0)],
        out_specs=blk(0),
        compiler_params=_cparams("parallel", "parallel"),
    )(proj, proj, ys, yd)


def _merge_bwd(proj, off_gs, off_gd, ys, yd, dmixed):
    l, d = ys.shape
    cw = min(1024, d)
    blk = lambda off: pl.BlockSpec((ROW_TILE, cw), lambda i, j: (i, off // cw + j))

    def body(gs_ref, gd_ref, ys_ref, yd_ref, dm_ref, dgs_ref, dgd_ref, dys_ref, dyd_ref):
        _, vjp = jax.vjp(_merge_fn, gs_ref[...], gd_ref[...], ys_ref[...], yd_ref[...])
        dgs, dgd, dys, dyd = vjp(dm_ref[...])
        dgs_ref[...] = dgs.astype(BF16)
        dgd_ref[...] = dgd.astype(BF16)
        dys_ref[...] = dys.astype(BF16)
        dyd_ref[...] = dyd.astype(BF16)

    out = jax.ShapeDtypeStruct((l, d), BF16)
    return pl.pallas_call(
        body, name="merge_bwd",
        out_shape=(out, out, out, out),
        grid=(l // ROW_TILE, d // cw),
        in_specs=[blk(off_gs), blk(off_gd), blk(0), blk(0), blk(0)],
        out_specs=(blk(0), blk(0), blk(0), blk(0)),
        compiler_params=_cparams("parallel", "parallel"),
    )(proj, proj, ys, yd, dmixed)


def _s5_disc_fn(lam_re, lam_im, log_step):
    step = jnp.exp(log_step)
    mag = jnp.exp(lam_re * step)
    abar_re = mag * jnp.cos(lam_im * step)
    abar_im = mag * jnp.sin(lam_im * step)
    den = lam_re * lam_re + lam_im * lam_im
    xr = abar_re - 1.0
    f_re = (xr * lam_re + abar_im * lam_im) / den
    f_im = (abar_im * lam_re - xr * lam_im) / den
    return abar_re, abar_im, f_re, f_im


def _s5_disc_fwd(lam_re, lam_im, log_step):
    g, p = lam_re.shape

    def body(lr_ref, li_ref, ls_ref, ar_ref, ai_ref, fr_ref, fi_ref):
        ar, ai, fr, fi = _s5_disc_fn(lr_ref[...], li_ref[...], ls_ref[...])
        ar_ref[...] = ar
        ai_ref[...] = ai
        fr_ref[...] = fr
        fi_ref[...] = fi

    o = jax.ShapeDtypeStruct((g, p), F32)
    return pl.pallas_call(body, name="s5_disc_fwd", out_shape=(o, o, o, o),
                          compiler_params=_cparams())(lam_re, lam_im, log_step)


def _s5_disc_bwd(lam_re, lam_im, log_step, dar, dai, dfr, dfi):
    g, p = lam_re.shape

    def body(lr_ref, li_ref, ls_ref, dar_ref, dai_ref, dfr_ref, dfi_ref, dlr_ref, dli_ref, dls_ref):
        _, vjp = jax.vjp(_s5_disc_fn, lr_ref[...], li_ref[...], ls_ref[...])
        dlr, dli, dls = vjp((dar_ref[...], dai_ref[...], dfr_ref[...], dfi_ref[...]))
        dlr_ref[...] = dlr
        dli_ref[...] = dli
        dls_ref[...] = dls

    o = jax.ShapeDtypeStruct((g, p), F32)
    return pl.pallas_call(body, name="s5_disc_bwd",
                          out_shape=(o, o, jax.ShapeDtypeStruct((g, 1), F32)),
                          compiler_params=_cparams())(lam_re, lam_im, log_step, dar, dai, dfr, dfi)


def _s5_bbar_fwd(f_re, f_im, b_re, b_im):
    n, c = b_re.shape

    def body(fr_ref, fi_ref, br_ref, bi_ref, or_ref, oi_ref):
        fr, fi, br, bi = fr_ref[...], fi_ref[...], br_ref[...], bi_ref[...]
        or_ref[...] = fr * br - fi * bi
        oi_ref[...] = fr * bi + fi * br

    o = jax.ShapeDtypeStruct((n, c), F32)
    return pl.pallas_call(body, name="s5_bbar_fwd", out_shape=(o, o),
                          compiler_params=_cparams())(f_re, f_im, b_re, b_im)


def _s5_bbar_bwd(f_re, f_im, b_re, b_im, dbr, dbi):
    n, c = b_re.shape

    def body(fr_ref, fi_ref, br_ref, bi_ref, dor_ref, doi_ref, dfr_ref, dfi_ref, dbr_ref, dbi_ref):
        fr, fi, br, bi = fr_ref[...], fi_ref[...], br_ref[...], bi_ref[...]
        dor, doi = dor_ref[...], doi_ref[...]
        dfr_ref[...] = jnp.sum(dor * br + doi * bi, axis=-1, keepdims=True)
        dfi_ref[...] = jnp.sum(doi * br - dor * bi, axis=-1, keepdims=True)
        dbr_ref[...] = fr * dor + fi * doi
        dbi_ref[...] = fr * doi - fi * dor

    col = jax.ShapeDtypeStruct((n, 1), F32)
    o = jax.ShapeDtypeStruct((n, c), F32)
    return pl.pallas_call(body, name="s5_bbar_bwd", out_shape=(col, col, o, o),
                          compiler_params=_cparams())(f_re, f_im, b_re, b_im, dbr, dbi)


SUBLANES = 8


def _scan_groups(xr, xi, ar, ai, reverse):
    t, n = xr.shape
    xr, xi = xr.reshape(t // SUBLANES, SUBLANES, n), xi.reshape(t // SUBLANES, SUBLANES, n)
    sub = lax.broadcasted_iota(jnp.int32, (1, SUBLANES, 1), 1)
    pr, pi = ar.reshape(1, 1, n), ai.reshape(1, 1, n)
    for sh in (1, 2, 4):
        keep = (sub < SUBLANES - sh) if reverse else (sub >= sh)
        cr, ci = jnp.where(keep, pr, 0.0), jnp.where(keep, pi, 0.0)
        shift = SUBLANES - sh if reverse else sh
        sr, si = pltpu.roll(xr, shift, 1), pltpu.roll(xi, shift, 1)
        xr, xi = xr + cr * sr - ci * si, xi + cr * si + ci * sr
        pr, pi = pr * pr - pi * pi, 2.0 * pr * pi
    return xr.reshape(t, n), xi.reshape(t, n)


def _scan_rows(xr, xi, ar, ai, cr, ci, sr_ref, si_ref, reverse):
    t, n = xr.shape
    xr, xi = _scan_groups(xr, xi, ar, ai, reverse)
    sr_ref[...] = xr
    si_ref[...] = xi
    sub = lax.broadcasted_iota(jnp.int32, (SUBLANES, n), 0)
    seed = sub == (SUBLANES - 1 if reverse else 0)
    pwr, pwi = _scan_groups(jnp.where(seed, ar, 0.0), jnp.where(seed, ai, 0.0), ar, ai, reverse)
    groups = range(t // SUBLANES)
    edge = 0 if reverse else SUBLANES - 1
    for g in (reversed(groups) if reverse else groups):
        rows = slice(g * SUBLANES, (g + 1) * SUBLANES)
        vr = sr_ref[rows, :] + (pwr * cr - pwi * ci)
        vi = si_ref[rows, :] + (pwr * ci + pwi * cr)
        sr_ref[rows, :] = vr
        si_ref[rows, :] = vi
        cr, ci = vr[edge:edge + 1, :], vi[edge:edge + 1, :]
    return cr, ci


def _s5_states(u_bf, bbr, bbi, ar, ai, cr, ci, sr_ref, si_ref):
    return _scan_rows(_dot(u_bf, bbr), _dot(u_bf, bbi), ar, ai, cr, ci, sr_ref, si_ref, reverse=False)


def _s5_fwd(proj, bbr, bbi, a_re, a_im, ctr, cti, d_skip, d_s5):
    l = proj.shape[0]
    nb, uc, ns = bbr.shape
    t = min(S5_T, l)
    nt = l // t

    def body(u_ref, bbr_ref, bbi_ref, ar_ref, ai_ref, ctr_ref, cti_ref, d_ref,
             y_ref, car_r_ref, car_i_ref, sr_ref, si_ref, cr, ci):
        @pl.when(pl.program_id(1) == 0)
        def _():
            cr[...] = jnp.zeros_like(cr)
            ci[...] = jnp.zeros_like(ci)

        car_r_ref[...] = cr[...]
        car_i_ref[...] = ci[...]
        u = u_ref[...]
        cr[...], ci[...] = _s5_states(u.astype(BF16), bbr_ref[...], bbi_ref[...], ar_ref[...],
                                      ai_ref[...], cr[...], ci[...], sr_ref, si_ref)
        y_ref[...] = (_bdot(sr_ref[...], ctr_ref[...]) - _bdot(si_ref[...], cti_ref[...])
                      + d_ref[...] * u)

    per_block = lambda shape: pl.BlockSpec((None,) + shape, lambda b, n: (b, 0, 0))
    return pl.pallas_call(
        body, name="s5_fwd",
        out_shape=(jax.ShapeDtypeStruct((l, d_s5), F32),
                   jax.ShapeDtypeStruct((nt, 1, nb * ns), F32),
                   jax.ShapeDtypeStruct((nt, 1, nb * ns), F32),
                   jax.ShapeDtypeStruct((l, nb * ns), F32),
                   jax.ShapeDtypeStruct((l, nb * ns), F32)),
        grid=(nb, nt),
        in_specs=[pl.BlockSpec((t, uc), lambda b, n: (n, b)),
                  per_block((uc, ns)), per_block((uc, ns)),
                  per_block((1, ns)), per_block((1, ns)),
                  per_block((ns, uc)), per_block((ns, uc)),
                  pl.BlockSpec((1, uc), lambda b, n: (0, b))],
        out_specs=(pl.BlockSpec((t, uc), lambda b, n: (n, b)),
                   pl.BlockSpec((None, 1, ns), lambda b, n: (n, 0, b)),
                   pl.BlockSpec((None, 1, ns), lambda b, n: (n, 0, b)),
                   pl.BlockSpec((t, ns), lambda b, n: (n, b)),
                   pl.BlockSpec((t, ns), lambda b, n: (n, b))),
        scratch_shapes=[pltpu.VMEM((1, ns), F32), pltpu.VMEM((1, ns), F32)],
        compiler_params=_cparams("parallel", "arbitrary"),
    )(proj, bbr, bbi, a_re, a_im, ctr, cti, d_skip)


def _s5_bwd(proj, dy, bbr, bbi, a_re, a_im, cbr, cbi, d_skip, car_r, car_i, states_r, states_i):
    l, d_s5 = dy.shape
    nb, uc, ns = bbr.shape
    t = min(S5_T, l)
    nt = l // t

    def body(u_ref, dy_ref, bbr_ref, bbi_ref, ar_ref, ai_ref, cbr_ref, cbi_ref, d_ref,
             car_r_ref, car_i_ref, sr_ref, si_ref,
             du_ref, dar_ref, dai_ref, dbbr_ref, dbbi_ref, dcbr_ref, dcbi_ref, dd_ref, gcr, gci,
             gr_ref, gi_ref):
        @pl.when(pl.program_id(1) == 0)
        def _():
            gcr[...] = jnp.zeros_like(gcr)
            gci[...] = jnp.zeros_like(gci)
            for ref in (dar_ref, dai_ref, dbbr_ref, dbbi_ref, dcbr_ref, dcbi_ref, dd_ref):
                ref[...] = jnp.zeros_like(ref)

        row = lax.broadcasted_iota(jnp.int32, (t, 1), 0)
        u, dy = u_ref[...], dy_ref[...]
        u_bf, dy_bf = u.astype(BF16), dy.astype(BF16)
        ar, ai = ar_ref[...], ai_ref[...]
        cr, ci = car_r_ref[...], car_i_ref[...]
        sr, si = sr_ref[...], si_ref[...]
        first = row == 0
        pr = jnp.where(first, cr, pltpu.roll(sr, 1, 0))
        pi = jnp.where(first, ci, pltpu.roll(si, 1, 0))
        gcr[...], gci[...] = _scan_rows(_dot(dy_bf, cbr_ref[...]), -_dot(dy_bf, cbi_ref[...]), ar, -ai,
                                        gcr[...], gci[...], gr_ref, gi_ref, reverse=True)
        gr, gi = gr_ref[...], gi_ref[...]
        dar_ref[...] += jnp.sum(gr * pr + gi * pi, axis=0, keepdims=True)
        dai_ref[...] += jnp.sum(gi * pr - gr * pi, axis=0, keepdims=True)
        gr_bf, gi_bf = gr.astype(BF16), gi.astype(BF16)
        tn = ((0,), (0,))
        dbbr_ref[...] += _dot(u_bf, gr_bf, tn)
        dbbi_ref[...] += _dot(u_bf, gi_bf, tn)
        dcbr_ref[...] += _dot(dy_bf, sr.astype(BF16), tn)
        dcbi_ref[...] -= _dot(dy_bf, si.astype(BF16), tn)
        nt_dims = ((1,), (1,))
        du = _dot(gr_bf, bbr_ref[...], nt_dims) + _dot(gi_bf, bbi_ref[...], nt_dims) + dy * d_ref[...]
        du_ref[...] = du.astype(BF16)
        dd_ref[...] += jnp.sum(dy * u, axis=0, keepdims=True)

    rev = lambda n: nt - 1 - n
    per_block = lambda shape: pl.BlockSpec((None,) + shape, lambda b, n: (b, 0, 0))
    acc = jax.ShapeDtypeStruct((nb, uc, ns), F32)
    vec = jax.ShapeDtypeStruct((nb, 1, ns), F32)
    return pl.pallas_call(
        body, name="s5_bwd",
        out_shape=(jax.ShapeDtypeStruct((l, d_s5), BF16), vec, vec, acc, acc, acc, acc,
                   jax.ShapeDtypeStruct((1, d_s5), F32)),
        grid=(nb, nt),
        in_specs=[pl.BlockSpec((t, uc), lambda b, n: (rev(n), b)),
                  pl.BlockSpec((t, uc), lambda b, n: (rev(n), b)),
                  per_block((uc, ns)), per_block((uc, ns)),
                  per_block((1, ns)), per_block((1, ns)),
                  per_block((uc, ns)), per_block((uc, ns)),
                  pl.BlockSpec((1, uc), lambda b, n: (0, b)),
                  pl.BlockSpec((None, 1, ns), lambda b, n: (rev(n), 0, b)),
                  pl.BlockSpec((None, 1, ns), lambda b, n: (rev(n), 0, b)),
                  pl.BlockSpec((t, ns), lambda b, n: (rev(n), b)),
                  pl.BlockSpec((t, ns), lambda b, n: (rev(n), b))],
        out_specs=(pl.BlockSpec((t, uc), lambda b, n: (rev(n), b)),
                   per_block((1, ns)), per_block((1, ns)),
                   per_block((uc, ns)), per_block((uc, ns)),
                   per_block((uc, ns)), per_block((uc, ns)),
                   pl.BlockSpec((1, uc), lambda b, n: (0, b))),
        scratch_shapes=[pltpu.VMEM((1, ns), F32), pltpu.VMEM((1, ns), F32)]
        + [pltpu.VMEM((t, ns), F32)] * 2,
        compiler_params=_cparams("parallel", "arbitrary"),
    )(proj, dy, bbr, bbi, a_re, a_im, cbr, cbi, d_skip, car_r, car_i, states_r, states_i)


def _s5_glu_fwd(y1, proj, off_z, wglu):
    l, d = y1.shape

    def body(y_ref, z_ref, w_ref, o_ref):
        y2 = _gelu(y_ref[...])
        y3 = y2 * _sigmoid(_bdot(y2, w_ref[...]))
        o_ref[...] = (y3 * _silu(z_ref[...])).astype(BF16)

    return pl.pallas_call(
        body, name="s5_glu_fwd",
        out_shape=jax.ShapeDtypeStruct((l, d), BF16),
        grid=(l // ROW_TILE,),
        in_specs=[pl.BlockSpec((ROW_TILE, d), lambda i: (i, 0)),
                  pl.BlockSpec((ROW_TILE, d), lambda i: (i, off_z // d)),
                  pl.BlockSpec((d, d), lambda i: (0, 0))],
        out_specs=pl.BlockSpec((ROW_TILE, d), lambda i: (i, 0)),
        compiler_params=_cparams("parallel"),
    )(y1, proj, wglu)


def _s5_glu_bwd(y1, proj, off_z, wglu, dout):
    l, d = y1.shape

    def body(y_ref, z_ref, w_ref, do_ref, dy_ref, dz_ref, dw_ref):
        y2, gelu_vjp = jax.vjp(_gelu, y_ref[...])
        z = z_ref[...]
        sz, silu_vjp = jax.vjp(_silu, z)
        y2_bf = y2.astype(BF16)
        sg = _sigmoid(_dot(y2_bf, w_ref[...]))
        dout = do_ref[...]
        dy3 = dout * sz
        dz_ref[...] = silu_vjp(dout * (y2 * sg))[0].astype(BF16)
        dgl = (dy3 * y2 * sg * (1.0 - sg)).astype(BF16)
        dy2 = dy3 * sg + _dot(dgl, w_ref[...], ((1,), (1,)))
        dy_ref[...] = gelu_vjp(dy2)[0]

        @pl.when(pl.program_id(0) == 0)
        def _():
            dw_ref[...] = jnp.zeros_like(dw_ref)

        dw_ref[...] += _dot(y2_bf, dgl, ((0,), (0,)))

    row = pl.BlockSpec((ROW_TILE, d), lambda i: (i, 0))
    full = pl.BlockSpec((d, d), lambda i: (0, 0))
    return pl.pallas_call(
        body, name="s5_glu_bwd",
        out_shape=(jax.ShapeDtypeStruct((l, d), F32), jax.ShapeDtypeStruct((l, d), BF16),
                   jax.ShapeDtypeStruct((d, d), F32)),
        grid=(l // ROW_TILE,),
        in_specs=[row, pl.BlockSpec((ROW_TILE, d), lambda i: (i, off_z // d)), full, row],
        out_specs=(row, row, full),
        compiler_params=_cparams("arbitrary"),
    )(y1, proj, wglu, dout)


def _shift_rows(x, k, back=False):
    if k == 0:
        return x
    t = x.shape[0]
    row = lax.broadcasted_iota(jnp.int32, (t, 1), 0)
    if back:
        return jnp.where(row < t - k, pltpu.roll(x, t - k, 0), 0.0)
    return jnp.where(row >= k, pltpu.roll(x, k, 0), 0.0)


def _dn_conv(x, w_ref):
    return sum(w_ref[CONV_K - 1 - k:CONV_K - k, :] * _shift_rows(x, k) for k in range(CONV_K))


def _dn_post_conv(c, j):
    y = _silu(c)
    n = y * lax.rsqrt(jnp.sum(y * y, axis=-1, keepdims=True) + EPS)
    n = n * jnp.where(j < DN_HEADS, DN_HEAD_DIM ** -0.5, 1.0)
    return jnp.where(j < 2 * DN_HEADS, n, y)


def _dn_prep_fwd(proj, off_qkv, conv_w):
    l = proj.shape[0]
    hd = DN_HEAD_DIM
    nblk = 3 * DN_HEADS

    def body(x_ref, w_ref, o_ref):
        o_ref[...] = _dn_post_conv(_dn_conv(x_ref[...], w_ref), pl.program_id(0))

    return pl.pallas_call(
        body, name="dn_prep_fwd",
        out_shape=jax.ShapeDtypeStruct((l, nblk * hd), F32),
        grid=(nblk,),
        in_specs=[pl.BlockSpec((l, hd), lambda j: (0, off_qkv // hd + j)),
                  pl.BlockSpec((CONV_K, hd), lambda j: (0, j))],
        out_specs=pl.BlockSpec((l, hd), lambda j: (0, j)),
        compiler_params=_cparams("parallel"),
    )(proj, conv_w)


def _dn_prep_bwd(proj, off_qkv, conv_w, dqkv):
    l = proj.shape[0]
    hd = DN_HEAD_DIM
    nblk = 3 * DN_HEADS

    def body(x_ref, w_ref, do_ref, dx_ref, dw_ref):
        x = x_ref[...]
        j = pl.program_id(0)
        _, vjp = jax.vjp(functools.partial(_dn_post_conv, j=j), _dn_conv(x, w_ref))
        dc = vjp(do_ref[...])[0]
        dx = sum(w_ref[CONV_K - 1 - k:CONV_K - k, :] * _shift_rows(dc, k, back=True)
                 for k in range(CONV_K))
        dx_ref[...] = dx.astype(BF16)
        for k in range(CONV_K):
            dw_ref[CONV_K - 1 - k:CONV_K - k, :] = jnp.sum(dc * _shift_rows(x, k), axis=0,
                                                           keepdims=True)

    return pl.pallas_call(
        body, name="dn_prep_bwd",
        out_shape=(jax.ShapeDtypeStruct((l, nblk * hd), BF16),
                   jax.ShapeDtypeStruct((CONV_K, nblk * hd), F32)),
        grid=(nblk,),
        in_specs=[pl.BlockSpec((l, hd), lambda j: (0, off_qkv // hd + j)),
                  pl.BlockSpec((CONV_K, hd), lambda j: (0, j)),
                  pl.BlockSpec((None, l, hd), lambda j: (j // DN_HEADS, 0, j % DN_HEADS))],
        out_specs=(pl.BlockSpec((l, hd), lambda j: (0, j)),
                   pl.BlockSpec((CONV_K, hd), lambda j: (0, j))),
        compiler_params=_cparams("parallel"),
    )(proj, conv_w, dqkv)


def _dn_gate_fn(ba, a_log_row, dt_row):
    lane = lax.broadcasted_iota(jnp.int32, ba.shape, 1)
    beta = _sigmoid(ba)
    g = -jnp.exp(a_log_row) * _softplus(ba + dt_row)
    return jnp.where(lane < DN_HEADS, beta, jnp.where(lane < 2 * DN_HEADS, g, 0.0))


def _dn_gates_fwd(proj, off_ba, a_log_row, dt_row):
    l = proj.shape[0]
    row = pl.BlockSpec((ROW_TILE, 128), lambda i: (i, off_ba // 128))
    vec = pl.BlockSpec((1, 128), lambda i: (0, 0))

    def body(ba_ref, al_ref, dt_ref, o_ref):
        o_ref[...] = _dn_gate_fn(ba_ref[...], al_ref[...], dt_ref[...])

    return pl.pallas_call(
        body, name="dn_gates_fwd",
        out_shape=jax.ShapeDtypeStruct((l, 128), F32),
        grid=(l // ROW_TILE,),
        in_specs=[row, vec, vec],
        out_specs=pl.BlockSpec((ROW_TILE, 128), lambda i: (i, 0)),
        compiler_params=_cparams("parallel"),
    )(proj, a_log_row, dt_row)


def _dn_gates_bwd(proj, off_ba, a_log_row, dt_row, dgb_heads):
    l = proj.shape[0]
    nh = dgb_heads.shape[0]
    row = pl.BlockSpec((ROW_TILE, 128), lambda i: (i, off_ba // 128))
    vec = pl.BlockSpec((1, 128), lambda i: (0, 0))

    def body(ba_ref, al_ref, dt_ref, dg_ref, dba_ref, dal_ref, ddt_ref):
        _, vjp = jax.vjp(_dn_gate_fn, ba_ref[...], al_ref[...], dt_ref[...])
        dgb = dg_ref[0]
        for h in range(1, nh):
            dgb = dgb + dg_ref[h]
        dba, dal, ddt = vjp(dgb)
        dba_ref[...] = dba.astype(BF16)

        @pl.when(pl.program_id(0) == 0)
        def _():
            dal_ref[...] = jnp.zeros_like(dal_ref)
            ddt_ref[...] = jnp.zeros_like(ddt_ref)

        dal_ref[...] += dal
        ddt_ref[...] += ddt

    return pl.pallas_call(
        body, name="dn_gates_bwd",
        out_shape=(jax.ShapeDtypeStruct((l, 128), BF16), jax.ShapeDtypeStruct((1, 128), F32),
                   jax.ShapeDtypeStruct((1, 128), F32)),
        grid=(l // ROW_TILE,),
        in_specs=[row, vec, vec, pl.BlockSpec((nh, ROW_TILE, 128), lambda i: (0, i, 0))],
        out_specs=(pl.BlockSpec((ROW_TILE, 128), lambda i: (i, 0)), vec, vec),
        compiler_params=_cparams("arbitrary"),
    )(proj, a_log_row, dt_row, dgb_heads)


@jax.custom_vjp
def _unit_lower_inverses(a_mats):
    c = a_mats[0].shape[0]
    eye = (lax.broadcasted_iota(jnp.int32, (c, c), 0) == lax.broadcasted_iota(jnp.int32, (c, c), 1)).astype(F32)
    t_inv = [eye - a for a in a_mats]
    power = a_mats
    for _ in range(int(math.log2(c)) - 1):
        power = [_bdot(p, p) for p in power]
        t_inv = [t + _bdot(t, p) for t, p in zip(t_inv, power)]
    return t_inv


def _unit_lower_inverses_fwd(a_mats):
    t_inv = _unit_lower_inverses(a_mats)
    return t_inv, t_inv


def _inverse_cotangents(t_inv, grads):
    right = [_dot3_dims(g, t, ((1,), (1,))) for g, t in zip(grads, t_inv)]
    return [-_dot3_dims(t, r, ((0,), (0,))) for t, r in zip(t_inv, right)]


_unit_lower_inverses.defvjp(_unit_lower_inverses_fwd,
                            lambda t_inv, grads: (_inverse_cotangents(t_inv, grads),))


@jax.custom_vjp
def _kept_inverses(a_mats, t_inv):
    return t_inv


_kept_inverses.defvjp(
    lambda a_mats, t_inv: (t_inv, t_inv),
    lambda t_inv, grads: (_inverse_cotangents(t_inv, grads), [jnp.zeros_like(t) for t in t_inv]))


def _dn_chunk_fn(states, qs, ks, vs, gb, heads, kept_inverses=None, return_inverses=False):
    c = qs[0].shape[0]
    each = lambda f, *lists: [f(*args) for args in zip(*lists)]
    lane = lax.broadcasted_iota(jnp.int32, gb.shape, 1)
    ri = lax.broadcasted_iota(jnp.int32, (c, c), 0)
    ci = lax.broadcasted_iota(jnp.int32, (c, c), 1)
    causal, strict = ri >= ci, ri > ci
    eye = (ri == ci).astype(F32)
    rowi = lax.broadcasted_iota(jnp.int32, (c, 1), 0)
    nt_dims = ((1,), (1,))
    hdot = functools.partial(_dot, precision=HIGHEST)

    pick = lambda m, at: jnp.sum(jnp.where(lane == at, m, 0.0), axis=1, keepdims=True)
    gb_cum = hdot(causal.astype(F32), gb)
    beta = [pick(gb, h) for h in heads]
    gc = [pick(gb_cum, h + DN_HEADS) for h in heads]
    gc_row = each(lambda g: jnp.sum(eye * g, axis=0, keepdims=True), gc)
    decay = each(lambda g, gr: jnp.where(causal, jnp.exp(jnp.where(causal, g - gr, 0.0)), 0.0),
                 gc, gc_row)
    kk = each(lambda k: _bdot(k, k, nt_dims), ks)
    a_mat = each(lambda b, m, dc: jnp.where(strict, b * m * dc, 0.0), beta, kk, decay)

    t_inv = (_unit_lower_inverses(a_mat) if kept_inverses is None
             else _kept_inverses(a_mat, kept_inverses))
    egc = each(jnp.exp, gc)
    u_c = each(lambda t, v, b: _dot3(t, v * b), t_inv, vs, beta)
    w_c = each(lambda t, k, b, e: _dot3(t, k * (b * e)), t_inv, ks, beta, egc)
    qk = each(lambda q, k, dc: _bdot(q, k, nt_dims) * dc, qs, ks, decay)
    g_end = each(lambda g: jnp.sum(jnp.where(rowi == c - 1, g, 0.0), axis=0, keepdims=True), gc)
    v_new = each(lambda u, w, s: u - _bdot(w, s), u_c, w_c, states)
    o = each(lambda q, e, s, m, vn: _bdot(q * e, s) + _bdot(m, vn), qs, egc, states, qk, v_new)
    new_states = each(
        lambda s, ge, k, g, vn: s * jnp.exp(ge) + _bdot(k * jnp.exp(ge - g), vn, ((0,), (0,))),
        states, g_end, ks, gc, v_new)
    return (o, new_states, t_inv) if return_inverses else (o, new_states)


def _dn_chunk_specs(order):
    hd, nh, hps, cps = DN_HEAD_DIM, DN_HEADS, DN_HEADS_PER_STEP, DN_CHUNKS_PER_STEP
    rows = cps * CHUNK
    qkv = lambda part: pl.BlockSpec((rows, hps * hd), lambda h, n: (order(n), part * (nh // hps) + h))
    gb = pl.BlockSpec((rows, 128), lambda h, n: (order(n), 0))
    state = pl.BlockSpec((hps, cps, hd, hd), lambda h, n: (h, order(n), 0, 0))
    inverse = pl.BlockSpec((hps, cps, CHUNK, CHUNK), lambda h, n: (h, order(n), 0, 0))
    return qkv, gb, state, inverse


def _dn_chunk_fwd(qkv, gb):
    l = qkv.shape[0]
    hd, nh, hps, cps = DN_HEAD_DIM, DN_HEADS, DN_HEADS_PER_STEP, DN_CHUNKS_PER_STEP
    n_chunks = l // CHUNK
    qkv_spec, gb_spec, state_spec, inverse_spec = _dn_chunk_specs(lambda n: n)

    def body(q_ref, k_ref, v_ref, gb_ref, o_ref, s_ref, t_ref, state):
        @pl.when(pl.program_id(1) == 0)
        def _():
            state[...] = jnp.zeros_like(state)

        cols = [slice(i * hd, (i + 1) * hd) for i in range(hps)]
        heads = [pl.program_id(0) * hps + i for i in range(hps)]
        states = [state[i] for i in range(hps)]
        for j in range(cps):
            rows = slice(j * CHUNK, (j + 1) * CHUNK)
            for i in range(hps):
                s_ref[i, j] = states[i]
            o, states, t_inv = _dn_chunk_fn(
                states, [q_ref[rows, cs] for cs in cols], [k_ref[rows, cs] for cs in cols],
                [v_ref[rows, cs] for cs in cols], gb_ref[rows, :], heads, return_inverses=True)
            for i in range(hps):
                o_ref[rows, cols[i]] = o[i]
                t_ref[i, j] = t_inv[i]
        for i in range(hps):
            state[i] = states[i]

    return pl.pallas_call(
        body, name="dn_chunk_fwd",
        out_shape=(jax.ShapeDtypeStruct((l, nh * hd), F32),
                   jax.ShapeDtypeStruct((nh, n_chunks, hd, hd), F32),
                   jax.ShapeDtypeStruct((nh, n_chunks, CHUNK, CHUNK), F32)),
        grid=(nh // hps, n_chunks // cps),
        in_specs=[qkv_spec(0), qkv_spec(1), qkv_spec(2), gb_spec],
        out_specs=(pl.BlockSpec((cps * CHUNK, hps * hd), lambda h, n: (n, h)), state_spec, inverse_spec),
        scratch_shapes=[pltpu.VMEM((hps, hd, hd), F32)],
        compiler_params=_cparams("parallel", "arbitrary"),
    )(qkv, qkv, qkv, gb)


def _dn_chunk_bwd(qkv, gb, states, inverses, do):
    l = qkv.shape[0]
    hd, nh, hps, cps = DN_HEAD_DIM, DN_HEADS, DN_HEADS_PER_STEP, DN_CHUNKS_PER_STEP
    n_steps = l // (cps * CHUNK)
    rev = lambda n: n_steps - 1 - n
    qkv_spec, gb_spec, state_spec, inverse_spec = _dn_chunk_specs(rev)

    def body(q_ref, k_ref, v_ref, gb_ref, s_ref, t_ref, do_ref, dqkv_ref, dgb_ref, dstate):
        @pl.when(pl.program_id(1) == 0)
        def _():
            dstate[...] = jnp.zeros_like(dstate)

        cols = [slice(i * hd, (i + 1) * hd) for i in range(hps)]
        heads = [pl.program_id(0) * hps + i for i in range(hps)]
        dstates = [dstate[i] for i in range(hps)]
        for j in reversed(range(cps)):
            rows = slice(j * CHUNK, (j + 1) * CHUNK)
            fn = functools.partial(_dn_chunk_fn, heads=heads, kept_inverses=[t_ref[i, j] for i in range(hps)])
            _, vjp = jax.vjp(fn, [s_ref[i, j] for i in range(hps)], [q_ref[rows, cs] for cs in cols],
                             [k_ref[rows, cs] for cs in cols], [v_ref[rows, cs] for cs in cols],
                             gb_ref[rows, :])
            dstates, dq, dk, dv, dgb = vjp(([do_ref[rows, cs] for cs in cols], dstates))
            for i in range(hps):
                dqkv_ref[0, rows, cols[i]] = dq[i]
                dqkv_ref[1, rows, cols[i]] = dk[i]
                dqkv_ref[2, rows, cols[i]] = dv[i]
            dgb_ref[rows, :] = dgb
        for i in range(hps):
            dstate[i] = dstates[i]

    head_out = pl.BlockSpec((cps * CHUNK, hps * hd), lambda h, n: (rev(n), h))
    return pl.pallas_call(
        body, name="dn_chunk_bwd",
        out_shape=(jax.ShapeDtypeStruct((3, l, nh * hd), F32),
                   jax.ShapeDtypeStruct((nh // hps, l, 128), F32)),
        grid=(nh // hps, n_steps),
        in_specs=[qkv_spec(0), qkv_spec(1), qkv_spec(2), gb_spec, state_spec, inverse_spec, head_out],
        out_specs=(pl.BlockSpec((3, cps * CHUNK, hps * hd), lambda h, n: (0, rev(n), h)),
                   pl.BlockSpec((None, cps * CHUNK, 128), lambda h, n: (h, rev(n), 0))),
        scratch_shapes=[pltpu.VMEM((hps, hd, hd), F32)],
        compiler_params=_cparams("parallel", "arbitrary"),
    )(qkv, qkv, qkv, gb, states, inverses, do)


def _dn_out_fn(o, z, w):
    return _rmsnorm(o, w) * _silu(z)


def _dn_out_fwd(o, proj, off_z, w):
    l, d = o.shape
    hd = DN_HEAD_DIM
    tr = min(4 * ROW_TILE, l)
    blk = lambda off: pl.BlockSpec((tr, hd), lambda i, h: (i, off // hd + h))

    def body(o_ref, z_ref, w_ref, out_ref):
        out_ref[...] = _dn_out_fn(o_ref[...], z_ref[...], w_ref[...]).astype(BF16)

    return pl.pallas_call(
        body, name="dn_out_fwd",
        out_shape=jax.ShapeDtypeStruct((l, d), BF16),
        grid=(l // tr, d // hd),
        in_specs=[blk(0), blk(off_z), pl.BlockSpec((1, hd), lambda i, h: (0, 0))],
        out_specs=blk(0),
        compiler_params=_cparams("parallel", "parallel"),
    )(o, proj, w)


def _dn_out_bwd(o, proj, off_z, w, dout):
    l, d = o.shape
    hd = DN_HEAD_DIM
    tr = min(4 * ROW_TILE, l)
    blk = lambda off: pl.BlockSpec((tr, hd), lambda i, h: (i, off // hd + h))
    vec = pl.BlockSpec((1, hd), lambda i, h: (0, 0))

    def body(o_ref, z_ref, w_ref, dout_ref, do_ref, dz_ref, dw_ref):
        _, vjp = jax.vjp(_dn_out_fn, o_ref[...], z_ref[...], w_ref[...])
        do, dz, dw = vjp(dout_ref[...])
        do_ref[...] = do
        dz_ref[...] = dz.astype(BF16)

        @pl.when((pl.program_id(0) == 0) & (pl.program_id(1) == 0))
        def _():
            dw_ref[...] = jnp.zeros_like(dw_ref)

        dw_ref[...] += dw

    return pl.pallas_call(
        body, name="dn_out_bwd",
        out_shape=(jax.ShapeDtypeStruct((l, d), F32), jax.ShapeDtypeStruct((l, d), BF16),
                   jax.ShapeDtypeStruct((1, hd), F32)),
        grid=(l // tr, d // hd),
        in_specs=[blk(0), blk(off_z), vec, blk(0)],
        out_specs=(blk(0), blk(0), vec),
        compiler_params=_cparams("arbitrary", "arbitrary"),
    )(o, proj, w, dout)


def _tile_2d(rows, cols, budget_bytes=1 << 20):
    for tr in (rows, 4096, 2048, 1024, 512, 256, 128, 64, 32, 16):
        if tr <= rows and rows % tr == 0 and tr * cols * 4 <= budget_bytes:
            return tr, cols
    for tc in (2048, 1024, 512, 256, 128):
        if cols % tc == 0 and rows * tc * 4 <= 2 * budget_bytes:
            return rows, tc
    raise ValueError((rows, cols))


def _adamw_update(g, w_ref, m_ref, v_ref, go_ref, d_ref, mo_ref, vo_ref):
    c1 = 1.0 / (1.0 - ADAM_B1 ** ADAM_STEP)
    c2 = 1.0 / (1.0 - ADAM_B2 ** ADAM_STEP)
    m_new = ADAM_B1 * m_ref[...] + (1.0 - ADAM_B1) * g
    v_new = ADAM_B2 * v_ref[...] + (1.0 - ADAM_B2) * (g * g)
    go_ref[...] = g
    mo_ref[...] = m_new
    vo_ref[...] = v_new
    d_ref[...] = -ADAM_LR * ((m_new * c1) / (jnp.sqrt(v_new * c2) + ADAM_EPS) + ADAM_WD * w_ref[...])


def _adamw(w, m, v, gslots, name):
    rows, cols = w.shape
    ns = gslots.shape[0]
    tr, tc = _tile_2d(rows, cols)

    def body(w_ref, m_ref, v_ref, g_ref, go_ref, d_ref, mo_ref, vo_ref):
        g = g_ref[0].astype(F32)
        for s in range(1, ns):
            g = g + g_ref[s].astype(F32)
        _adamw_update(g, w_ref, m_ref, v_ref, go_ref, d_ref, mo_ref, vo_ref)

    blk = pl.BlockSpec((tr, tc), lambda i, j: (i, j))
    o = jax.ShapeDtypeStruct((rows, cols), F32)
    return pl.pallas_call(
        body, name=name, out_shape=(o, o, o, o),
        grid=(rows // tr, cols // tc),
        in_specs=[blk, blk, blk, pl.BlockSpec((ns, tr, tc), lambda i, j: (0, i, j))],
        out_specs=(blk, blk, blk, blk),
        compiler_params=_cparams("parallel", "parallel"),
    )(w, m, v, gslots)


def _slot_sum(gslots, name):
    ns, rows, cols = gslots.shape
    tr, tc = _tile_2d(rows, cols)

    def body(g_ref, o_ref):
        g = g_ref[0]
        for s in range(1, ns):
            g = g + g_ref[s]
        o_ref[...] = g

    return pl.pallas_call(
        body, name=name, out_shape=jax.ShapeDtypeStruct((rows, cols), F32),
        grid=(rows // tr, cols // tc),
        in_specs=[pl.BlockSpec((ns, tr, tc), lambda i, j: (0, i, j))],
        out_specs=pl.BlockSpec((tr, tc), lambda i, j: (i, j)),
        compiler_params=_cparams("parallel", "parallel"),
    )(gslots)


HBM_SPEC = pl.BlockSpec(memory_space=pl.ANY)


def _all_gather(arrs, name, relayed=(), after=None):
    n = len(arrs)
    n_sems = 13
    n_in = n + (after is not None)

    def body(*refs):
        ins, outs = refs[:n], refs[n_in:n_in + n]
        send_sems, recv_sems, local_sems = refs[n_in + n:]
        x, y, c = lax.axis_index("x"), lax.axis_index("y"), lax.axis_index("c")
        me, sibling = (x, y, c), (x, y, 1 - c)
        chips = [(1 - x, y), (x, 1 - y), (1 - x, 1 - y)]
        index = lambda px, py, pc: 4 * px + 2 * py + pc

        def copy(a, k, block, to, src=None, cols=None):
            dst = outs[a].at[index(*block)]
            src = dst if src is None else src
            if cols is not None:
                dst, src = dst.at[:, cols], src.at[:, cols]
            return pltpu.make_async_remote_copy(
                src_ref=src, dst_ref=dst, send_sem=send_sems.at[a, k], recv_sem=recv_sems.at[a, k],
                device_id=to, device_id_type=MESH)

        mine = [pltpu.make_async_copy(ins[a], outs[a].at[index(*me)], local_sems.at[a])
                for a in range(n)]
        for cp in mine:
            cp.start()
        sends = []

        def start(cp):
            cp.start()
            sends.append(cp)

        halves = {a: (pl.ds(0, arrs[a].shape[1] // 2), pl.ds(arrs[a].shape[1] // 2, arrs[a].shape[1] // 2))
                  for a in relayed}
        near_x, near_y, far = [(*chip, c) for chip in chips]
        for a in range(n):
            start(copy(a, 0, me, sibling, src=ins[a]))
            if a in relayed:
                left, right = halves[a]
                for k, to, cols in ((1, near_x, left), (3, near_y, right), (2, near_x, right), (4, near_y, left)):
                    start(copy(a, k, me, to, src=ins[a], cols=cols))
            else:
                for j, chip in enumerate(chips):
                    start(copy(a, 1 + j, me, (*chip, c), src=ins[a]))
        for a in relayed:
            left, right = halves[a]
            for k, block, cols, onward, to_sibling in (
                    (1, near_x, left, (5, near_y), 7), (3, near_y, right, (6, near_x), 10),
                    (2, near_x, right, None, 8), (4, near_y, left, None, 9),
                    (5, far, left, None, 11), (6, far, right, None, 12)):
                copy(a, k, block, me, cols=cols).wait_recv()
                if onward is not None:
                    start(copy(a, onward[0], block, onward[1], cols=cols))
                start(copy(a, to_sibling, block, sibling, cols=cols))
        for j, chip in enumerate(chips):
            for a in range(n):
                if a not in relayed:
                    copy(a, 1 + j, (*chip, c), me).wait_recv()
                    start(copy(a, 4 + j, (*chip, c), sibling))
        for a in range(n):
            copy(a, 0, sibling, me).wait_recv()
            if a in relayed:
                left, right = halves[a]
                for k, chip, cols in ((7, chips[0], left), (8, chips[0], right), (9, chips[1], left),
                                      (10, chips[1], right), (11, chips[2], left), (12, chips[2], right)):
                    copy(a, k, (*chip, 1 - c), me, cols=cols).wait_recv()
            else:
                for j, chip in enumerate(chips):
                    copy(a, 4 + j, (*chip, 1 - c), me).wait_recv()
        for cp in sends:
            cp.wait_send()
        for cp in mine:
            cp.wait()

    return pl.pallas_call(
        body, name=name,
        out_shape=[jax.ShapeDtypeStruct((N_DEV,) + a.shape, a.dtype) for a in arrs],
        in_specs=[HBM_SPEC] * n_in, out_specs=[HBM_SPEC] * n,
        scratch_shapes=[pltpu.SemaphoreType.DMA((n, n_sems)), pltpu.SemaphoreType.DMA((n, n_sems)),
                        pltpu.SemaphoreType.DMA((n,))],
    )(*arrs, *([after] if after is not None else []))


def _sibling_swap(arrs, name):
    n = len(arrs)

    def body(*refs):
        ins, outs = refs[:n], refs[n:2 * n]
        send_sems, recv_sems = refs[2 * n:]
        x, y, c = lax.axis_index("x"), lax.axis_index("y"), lax.axis_index("c")
        copies = [pltpu.make_async_remote_copy(
            src_ref=ins[a].at[:, 1 - c], dst_ref=outs[a],
            send_sem=send_sems.at[a], recv_sem=recv_sems.at[a],
            device_id=(x, y, 1 - c), device_id_type=MESH) for a in range(n)]
        for cp in copies:
            cp.start()
        for cp in copies:
            cp.wait()

    return pl.pallas_call(
        body, name=name,
        out_shape=[jax.ShapeDtypeStruct(a.shape[:1] + a.shape[2:], a.dtype) for a in arrs],
        in_specs=[HBM_SPEC] * n, out_specs=[HBM_SPEC] * n,
        scratch_shapes=[pltpu.SemaphoreType.DMA((n,)), pltpu.SemaphoreType.DMA((n,))],
    )(*arrs)


def _pair_sum(mine, theirs, core, name):
    chips, _, rows, cols = mine.shape
    tr, tc = _tile_2d(rows, cols, budget_bytes=2 << 20)

    def body(core_ref, a_ref, b_ref, o_ref):
        o_ref[...] = (a_ref[...].astype(F32) + b_ref[...].astype(F32)).astype(o_ref.dtype)

    slab = pl.BlockSpec((None, tr, tc), lambda ch, i, j, core_ref: (ch, i, j))
    return pl.pallas_call(
        body, name=name, out_shape=jax.ShapeDtypeStruct((chips, rows, cols), mine.dtype),
        grid_spec=pltpu.PrefetchScalarGridSpec(
            num_scalar_prefetch=1, grid=(chips, rows // tr, cols // tc),
            in_specs=[pl.BlockSpec((None, None, tr, tc),
                                   lambda ch, i, j, core_ref: (ch, core_ref[0], i, j)), slab],
            out_specs=slab),
        compiler_params=_cparams("parallel", "parallel", "parallel"),
    )(core, mine, theirs)


HBM_ONLY = pl.BlockSpec(memory_space=pltpu.HBM)
SEM_SPEC = pl.BlockSpec(memory_space=pltpu.SEMAPHORE)
SPLIT_COPY_EFFECT = pltpu.SideEffectType.DATAFLOW_SIDE_EFFECTING


def _flip(v, bit):
    return 1 - v if bit else v


def _chip_slices_plan(n):
    def plan():
        x, y, c = lax.axis_index("x"), lax.axis_index("y"), lax.axis_index("c")
        copies = []
        for k in range(1, 4):
            px, py = _flip(x, k & 2), _flip(y, k & 1)
            copies += [(a, 2 * px + py, 2 * x + y, (px, py, c)) for a in range(n)]
        return copies
    return plan, 3 * n


def _gather_plan(n):
    def plan():
        x, y, c = lax.axis_index("x"), lax.axis_index("y"), lax.axis_index("c")
        copies = []
        for k in range(1, N_DEV):
            peer = (_flip(x, k & 4), _flip(y, k & 2), _flip(c, k & 1))
            copies += [(a, None, 4 * x + 2 * y + c, peer) for a in range(n)]
        return copies + [(a, None, 4 * x + 2 * y + c, None) for a in range(n)]
    return plan, 8 * n


def _planned_copies(plan, srcs, lands, send_sems, recv_sems):
    copies = []
    for i, (a, src_at, land_at, peer) in enumerate(plan()):
        src, dst = srcs[a] if src_at is None else srcs[a].at[src_at], lands[a].at[land_at]
        if peer is None:
            local = pltpu.make_async_copy(src, dst, send_sems[i])
            copies.append((local, local.wait))
        else:
            remote = pltpu.make_async_remote_copy(src_ref=src, dst_ref=dst, send_sem=send_sems[i],
                                                  recv_sem=recv_sems[i], device_id=peer, device_id_type=MESH)
            copies.append((remote, remote.wait))
    return copies


def _split_exchange_start(plan_and_count, arrs, land_shapes, name, after=None):
    plan, n_sems = plan_and_count
    n = len(arrs)

    n_in = 2 * n + (after is not None)

    def body(*refs):
        srcs, lands = refs[:n], refs[n:2 * n]
        send_sems, recv_sems = refs[n_in:n_in + n_sems], refs[n_in + n_sems:n_in + 2 * n_sems]
        token = refs[-1]
        for copy, _ in _planned_copies(plan, srcs, lands, send_sems, recv_sems):
            copy.start()
        token[...] = jnp.zeros_like(token)

    hbm = lambda a: pltpu.HBM(a.shape, a.dtype)
    operands = [pltpu.with_memory_space_constraint(a, pltpu.HBM) for a in arrs]
    operands += [pltpu.with_memory_space_constraint(lax.empty(shape, a.dtype), pltpu.HBM)
                 for a, shape in zip(arrs, land_shapes)]
    out = pl.pallas_call(
        body, name=name,
        out_shape=(*[pltpu.SemaphoreType.DMA(())] * (2 * n_sems),
                   *[hbm(a) for a in operands],
                   jax.ShapeDtypeStruct((8, 128), F32)),
        in_specs=[HBM_ONLY] * (2 * n) + [pl.BlockSpec(memory_space=pl.ANY)] * (after is not None),
        out_specs=(*[SEM_SPEC] * (2 * n_sems), *[HBM_ONLY] * (2 * n),
                   pl.BlockSpec(memory_space=pltpu.VMEM)),
        input_output_aliases={i: 2 * n_sems + i for i in range(2 * n)},
        compiler_params=pltpu.CompilerParams(has_side_effects=SPLIT_COPY_EFFECT),
    )(*operands, *([after] if after is not None else []))
    sems, rest = list(out[:2 * n_sems]), out[2 * n_sems:]
    return sems, list(rest[:n]), list(rest[n:2 * n]), rest[-1]


def _split_exchange_wait(plan_and_count, sems, srcs, lands, after, name):
    plan, n_sems = plan_and_count
    n = len(srcs)

    def body(*refs):
        src_refs, land_refs = refs[:n], refs[n:2 * n]
        send_sems, recv_sems = refs[2 * n:2 * n + n_sems], refs[2 * n + n_sems:2 * n + 2 * n_sems]
        for _, wait in _planned_copies(plan, src_refs, land_refs, send_sems, recv_sems):
            wait()

    hbm = lambda a: pltpu.HBM(a.shape, a.dtype)
    out = pl.pallas_call(
        body, name=name,
        out_shape=(*[hbm(a) for a in srcs], *[hbm(a) for a in lands]),
        in_specs=[HBM_ONLY] * (2 * n) + [SEM_SPEC] * (2 * n_sems) + [pl.BlockSpec(memory_space=pl.ANY)],
        out_specs=tuple([HBM_ONLY] * (2 * n)),
        input_output_aliases={i: i for i in range(2 * n)},
        compiler_params=pltpu.CompilerParams(has_side_effects=SPLIT_COPY_EFFECT),
    )(*srcs, *lands, *sems, after)
    return list(out[n:])


def _adamw_exchanged(w, m, v, own, landed, chip, name):
    rows, cols = w.shape
    tr, tc = _tile_2d(rows, cols)

    def body(chip_ref, w_ref, m_ref, v_ref, own_ref, l1_ref, l2_ref, l3_ref, go_ref, d_ref, mo_ref, vo_ref):
        g = own_ref[...].astype(F32)
        for ref in (l1_ref, l2_ref, l3_ref):
            g = g + ref[...].astype(F32)
        _adamw_update(g, w_ref, m_ref, v_ref, go_ref, d_ref, mo_ref, vo_ref)

    blk = pl.BlockSpec((tr, tc), lambda i, j, chip_ref: (i, j))
    slot = lambda k: pl.BlockSpec((None, tr, tc), lambda i, j, chip_ref: (chip_ref[0] ^ k, i, j))
    o = jax.ShapeDtypeStruct((rows, cols), F32)
    return pl.pallas_call(
        body, name=name, out_shape=(o, o, o, o),
        grid_spec=pltpu.PrefetchScalarGridSpec(
            num_scalar_prefetch=1, grid=(rows // tr, cols // tc),
            in_specs=[blk, blk, blk, slot(0), slot(1), slot(2), slot(3)],
            out_specs=(blk, blk, blk, blk)),
        compiler_params=_cparams("parallel", "parallel"),
    )(chip, w, m, v, own, landed, landed, landed)


def _block_diag(t):
    nb, gpb, r, c = t.shape
    eye = jnp.eye(gpb, dtype=t.dtype)
    return jnp.einsum("ngrc,gh->ngrhc", t, eye).reshape(nb, gpb * r, gpb * c)


def _diag_blocks(t, r, c):
    nb = t.shape[0]
    gpb = t.shape[1] // r
    t = t.reshape(nb, gpb, r, gpb, c)
    return jnp.einsum("ngrhc,gh->ngrc", t, jnp.eye(gpb, dtype=t.dtype))


def _pack_rows(parts):
    flat = jnp.concatenate([p.reshape(-1).astype(F32) for p in parts])
    pad = (-flat.shape[0]) % (256 * 128)
    return jnp.pad(flat, (0, pad)).reshape(-1, 128)


def _unpack_rows(packed, shapes):
    flat = packed.reshape(-1)
    out, at = [], 0
    for shape in shapes:
        size = math.prod(shape)
        out.append(flat[at:at + size].reshape(shape))
        at += size
    return out


def kernel(x, ln_w, w_in, s5_lam_re, s5_lam_im, s5_log_step, s5_b_re, s5_b_im, s5_c_re, s5_c_im, s5_d, s5_w_glu, s5_w_up, dn_conv_w, dn_a_log, dn_dt_bias, dn_norm_w, dn_w_up, w_out, final_norm_w, loss_target, m_ln_w, m_w_in, m_s5_lam_re, m_s5_lam_im, m_s5_log_step, m_s5_b_re, m_s5_b_im, m_s5_c_re, m_s5_c_im, m_s5_d, m_s5_w_glu, m_s5_w_up, m_dn_conv_w, m_dn_a_log, m_dn_dt_bias, m_dn_norm_w, m_dn_w_up, m_w_out, m_final_norm_w, v_ln_w, v_w_in, v_s5_lam_re, v_s5_lam_im, v_s5_log_step, v_s5_b_re, v_s5_b_im, v_s5_c_re, v_s5_c_im, v_s5_d, v_s5_w_glu, v_s5_w_up, v_dn_conv_w, v_dn_a_log, v_dn_dt_bias, v_dn_norm_w, v_dn_w_up, v_w_out, v_final_norm_w):
    weights = dict(ln_w=ln_w, w_in=w_in, s5_lam_re=s5_lam_re, s5_lam_im=s5_lam_im,
                   s5_log_step=s5_log_step, s5_b_re=s5_b_re, s5_b_im=s5_b_im, s5_c_re=s5_c_re,
                   s5_c_im=s5_c_im, s5_d=s5_d, s5_w_glu=s5_w_glu, s5_w_up=s5_w_up,
                   dn_conv_w=dn_conv_w, dn_a_log=dn_a_log, dn_dt_bias=dn_dt_bias,
                   dn_norm_w=dn_norm_w, dn_w_up=dn_w_up, w_out=w_out, final_norm_w=final_norm_w)
    mom_m = dict(ln_w=m_ln_w, w_in=m_w_in, s5_lam_re=m_s5_lam_re, s5_lam_im=m_s5_lam_im,
                 s5_log_step=m_s5_log_step, s5_b_re=m_s5_b_re, s5_b_im=m_s5_b_im,
                 s5_c_re=m_s5_c_re, s5_c_im=m_s5_c_im, s5_d=m_s5_d, s5_w_glu=m_s5_w_glu,
                 s5_w_up=m_s5_w_up, dn_conv_w=m_dn_conv_w, dn_a_log=m_dn_a_log,
                 dn_dt_bias=m_dn_dt_bias, dn_norm_w=m_dn_norm_w, dn_w_up=m_dn_w_up,
                 w_out=m_w_out, final_norm_w=m_final_norm_w)
    mom_v = dict(ln_w=v_ln_w, w_in=v_w_in, s5_lam_re=v_s5_lam_re, s5_lam_im=v_s5_lam_im,
                 s5_log_step=v_s5_log_step, s5_b_re=v_s5_b_re, s5_b_im=v_s5_b_im,
                 s5_c_re=v_s5_c_re, s5_c_im=v_s5_c_im, s5_d=v_s5_d, s5_w_glu=v_s5_w_glu,
                 s5_w_up=v_s5_w_up, dn_conv_w=v_dn_conv_w, dn_a_log=v_dn_a_log,
                 dn_dt_bias=v_dn_dt_bias, dn_norm_w=v_dn_norm_w, dn_w_up=v_dn_w_up,
                 w_out=v_w_out, final_norm_w=v_final_norm_w)
    names = list(weights)

    l, d = x.shape[1], x.shape[2]
    d_s5 = d // 2
    groups = d_s5 // S5_GROUP
    nb = groups // S5_GPB
    d_dn = DN_HEADS * DN_HEAD_DIM
    w_in_cols = w_in.shape[2]
    d_in = N_DEV * w_in_cols
    off_ba_src = 2 * d_s5 + 4 * d_dn
    off_u, off_zs, off_qkv, off_zd = 0, d_s5, 2 * d_s5, 2 * d_s5 + 3 * d_dn
    off_ba = off_zd + d_dn
    n_main = off_ba + BA_PAD
    off_gs, off_gd = 0, d
    x2d, tgt2d = x[0], loss_target[0]
    my_index = 4 * lax.axis_index("x") + 2 * lax.axis_index("y") + lax.axis_index("c")

    g_win, g_conv = _all_gather([jnp.transpose(w_in[0]).astype(BF16), dn_conv_w[0]], name="gather_weights",
                                relayed=(0,))
    late_plan = _gather_plan(4)
    late_shards = [s5_w_glu[0].astype(BF16), s5_w_up[0].astype(BF16), dn_w_up[0].astype(BF16),
                   w_out[0].astype(BF16)]
    late_sems, late_shards, late_lands, late_token = _split_exchange_start(
        late_plan, late_shards, [(N_DEV,) + s.shape for s in late_shards], name="gather_late_start",
        after=g_conv)
    ba_end = off_ba_src + 2 * DN_HEADS
    w_full_t = g_win.reshape(d_in, d)
    w_gates_t = w_full_t[ba_end:]
    conv_full = jnp.transpose(g_conv, (1, 0, 2)).reshape(CONV_K, 3 * d_dn)

    lam_re, lam_im = s5_lam_re[0], s5_lam_im[0]
    log_step = s5_log_step[0].reshape(groups, 1)
    b_re = s5_b_re[0].reshape(groups * S5_STATE, S5_GROUP)
    b_im = s5_b_im[0].reshape(groups * S5_STATE, S5_GROUP)
    abar_re, abar_im, f_re, f_im = _s5_disc_fwd(lam_re, lam_im, log_step)
    f_re_col, f_im_col = f_re.reshape(-1, 1), f_im.reshape(-1, 1)
    bb_re, bb_im = _s5_bbar_fwd(f_re_col, f_im_col, b_re, b_im)

    def bb_blocks(t):
        t = t.reshape(nb, S5_GPB, S5_STATE, S5_GROUP).transpose(0, 1, 3, 2)
        return _block_diag(t).astype(BF16)

    def c_blocks(t):
        return _block_diag(t.reshape(nb, S5_GPB, S5_GROUP, S5_STATE)).astype(BF16)

    bbr, bbi = bb_blocks(bb_re), bb_blocks(bb_im)
    cbr, cbi = c_blocks(s5_c_re[0]), c_blocks(s5_c_im[0])
    ctr, cti = jnp.transpose(cbr, (0, 2, 1)), jnp.transpose(cbi, (0, 2, 1))
    a_re = abar_re.reshape(nb, 1, S5_GPB * S5_STATE)
    a_im = abar_im.reshape(nb, 1, S5_GPB * S5_STATE)

    h = _rms_fwd(x2d, ln_w)
    proj = _mm(h, w_full_t, tb=True, b_rows=n_main, tm=1024, tn=512, after=late_token, name="proj")
    proj_gates = _mm(h, w_gates_t, tb=True, tm=1024, tn=1024, name="proj_gates")
    y1, car_r, car_i, states_r, states_i = _s5_fwd(proj, bbr, bbi, a_re, a_im, ctr, cti, s5_d, d_s5)
    a_log_row = jnp.pad(dn_a_log, ((0, 0), (DN_HEADS, 128 - 2 * DN_HEADS)))
    dt_row = jnp.pad(dn_dt_bias, ((0, 0), (DN_HEADS, 128 - 2 * DN_HEADS)))
    qkv = _dn_prep_fwd(proj, off_qkv, conv_full)
    gb = _dn_gates_fwd(proj, off_ba, a_log_row, dt_row)
    o_dn, states, inverses = _dn_chunk_fwd(qkv, gb)

    g_glu, g_sup, g_dup, g_wout = _split_exchange_wait(late_plan, late_sems, late_shards, late_lands, o_dn,
                                                       name="gather_late_wait")
    wglu_full = g_glu.reshape(d_s5, d_s5)
    wsup_full = jnp.transpose(g_sup, (1, 0, 2)).reshape(d_s5, d)
    wdup_full = jnp.transpose(g_dup, (1, 0, 2)).reshape(d_dn, d)
    wout_full = g_wout.reshape(d, d)

    out_s = _s5_glu_fwd(y1, proj, off_zs, wglu_full)
    y_s = _mm(out_s, wsup_full, name="s5_up")
    out_d = _dn_out_fwd(o_dn, proj, off_zd, dn_norm_w)
    y_d = _mm(out_d, wdup_full, name="dn_up")

    mixed = _merge_fwd(proj_gates, off_gs, off_gd, y_s, y_d)
    branch = _mm(mixed, wout_full, name="w_out")
    dx2, dx2_bf, loss_dev, d_final_w = _final(x2d, branch, final_norm_w.reshape(1, d), tgt2d)

    g_wout_full = _mm(mixed, dx2_bf, ta=True, out_dtype=BF16, name="grad_w_out")
    dmixed = _mm(dx2_bf, wout_full, tb=True, name="d_mixed")
    dgs, dgd, dys, dyd = _merge_bwd(proj_gates, off_gs, off_gd, y_s, y_d, dmixed)

    g_dup_full = _mm(out_d, dyd, ta=True, out_dtype=BF16, name="grad_dn_up")
    dout_d = _mm(dyd, wdup_full, tb=True, name="d_out_d")
    do_dn, dzd, d_norm_w = _dn_out_bwd(o_dn, proj, off_zd, dn_norm_w, dout_d)
    dqkv, dgb_heads = _dn_chunk_bwd(qkv, gb, states, inverses, do_dn)
    dba, d_a_log_row, d_dt_row = _dn_gates_bwd(proj, off_ba, a_log_row, dt_row, dgb_heads)
    dqkv_pre, d_conv_full = _dn_prep_bwd(proj, off_qkv, conv_full, dqkv)

    g_sup_full = _mm(out_s, dys, ta=True, out_dtype=BF16, name="grad_s5_up")
    dout_s = _mm(dys, wsup_full, tb=True, name="d_out_s")
    dy1, dzs, g_glu_full = _s5_glu_bwd(y1, proj, off_zs, wglu_full, dout_s)

    def by_dest(t, axis=0):
        if axis == 1:
            return t.reshape(t.shape[0], 4, 2, t.shape[1] // N_DEV).transpose(1, 2, 0, 3)
        return t.reshape(4, 2, t.shape[0] // N_DEV, t.shape[1])

    core = lax.axis_index("c").astype(jnp.int32).reshape(1)
    chip = (2 * lax.axis_index("x") + lax.axis_index("y")).astype(jnp.int32).reshape(1)

    def chip_sums_of(which, parts, tag):
        from_sibling = _sibling_swap(parts, name="swap_grads_" + tag)
        return [_pair_sum(p, got, core, name="pair_sum_" + nm)
                for nm, p, got in zip(which, parts, from_sibling)]

    early = ["s5_w_glu", "s5_w_up", "dn_w_up", "w_out"]
    sums_a = chip_sums_of(early, [by_dest(g_glu_full.astype(BF16)), by_dest(g_sup_full, 1),
                                  by_dest(g_dup_full, 1), by_dest(g_wout_full)], "a")
    plan_a = _chip_slices_plan(len(sums_a))
    sems_a, src_a, land_a, token_a = _split_exchange_start(
        plan_a, sums_a, [t.shape for t in sums_a], name="exchange_start_a")

    (du, d_a_re, d_a_im, d_bbr, d_bbi, d_cbr, d_cbi, d_s5_d) = _s5_bwd(
        proj, dy1, bbr, bbi, a_re, a_im, cbr, cbi, s5_d + token_a[:1, :1], car_r, car_i,
        states_r, states_i)

    def from_bb_blocks(t):
        t = _diag_blocks(t, S5_GROUP, S5_STATE).transpose(0, 1, 3, 2)
        return t.reshape(groups * S5_STATE, S5_GROUP)

    d_f_re, d_f_im, d_b_re, d_b_im = _s5_bbar_bwd(f_re_col, f_im_col, b_re, b_im,
                                                 from_bb_blocks(d_bbr), from_bb_blocks(d_bbi))
    d_lam_re, d_lam_im, d_log_step = _s5_disc_bwd(
        lam_re, lam_im, log_step, d_a_re.reshape(groups, S5_STATE), d_a_im.reshape(groups, S5_STATE),
        d_f_re.reshape(groups, S5_STATE), d_f_im.reshape(groups, S5_STATE))
    d_c_re = _diag_blocks(d_cbr, S5_GROUP, S5_STATE).reshape(groups, S5_GROUP, S5_STATE)
    d_c_im = _diag_blocks(d_cbi, S5_GROUP, S5_STATE).reshape(groups, S5_GROUP, S5_STATE)

    dproj = jnp.concatenate([du, dzs, dqkv_pre, dzd, jnp.pad(dba, ((0, 0), (0, BA_PAD - 128)))], axis=1)
    dproj_gates = jnp.concatenate([dgs, dgd], axis=1)
    g_main_t = _mm(dproj, h, ta=True, out_dtype=BF16, tm=512, tn=d, name="grad_w_in")
    g_gates_t = _mm(dproj_gates, h, ta=True, out_dtype=BF16, tm=512, tn=d, name="grad_w_in_gates")
    g_win_full_t = jnp.concatenate([g_main_t[:ba_end], g_gates_t], axis=0)
    sums_b = chip_sums_of(["w_in"], [by_dest(g_win_full_t)], "b")
    plan_b = _chip_slices_plan(1)
    sems_b, src_b, land_b, token_b = _split_exchange_start(
        plan_b, sums_b, [t.shape for t in sums_b], name="exchange_start_b")
    dh_main = _mm(dproj, w_full_t, b_rows=n_main, tm=1024, tn=1024, tk=n_main // 4, after=token_b,
                  name="d_h_main")
    dh = _mm(dproj_gates, w_gates_t, tm=1024, tn=1024, tk=2048, addend=dh_main, name="d_h")
    grad_x, d_ln_w = _rms_bwd(x2d, ln_w, dh, dx2)
    big = ["w_in"] + early
    results = {}

    land_a = _split_exchange_wait(plan_a, sems_a, src_a, land_a, grad_x, name="exchange_wait_a")
    for nm, own, landed in zip(early, src_a, land_a):
        results[nm] = _adamw_exchanged(weights[nm][0], mom_m[nm][0], mom_v[nm][0], own, landed, chip,
                                       name="adamw_" + nm)

    small = [nm for nm in names if nm not in big]
    small_grads = dict(
        ln_w=d_ln_w, s5_lam_re=d_lam_re, s5_lam_im=d_lam_im, s5_log_step=d_log_step,
        s5_b_re=d_b_re, s5_b_im=d_b_im, s5_c_re=d_c_re, s5_c_im=d_c_im, s5_d=d_s5_d,
        dn_conv_w=d_conv_full, dn_a_log=d_a_log_row[:, DN_HEADS:2 * DN_HEADS],
        dn_dt_bias=d_dt_row[:, DN_HEADS:2 * DN_HEADS], dn_norm_w=d_norm_w, final_norm_w=d_final_w)
    small_plan = _gather_plan(1)
    small_pack = _pack_rows([small_grads[nm] for nm in small])
    small_sems, small_src, small_land, _ = _split_exchange_start(
        small_plan, [small_pack], [(N_DEV,) + small_pack.shape], name="gather_small_start",
        after=results[early[-1]][0])
    w_in_t, m_w_in_t, v_w_in_t = [jnp.transpose(t[0]) for t in (w_in, m_w_in, v_w_in)]
    (all_small,) = _split_exchange_wait(small_plan, small_sems, small_src, small_land, v_w_in_t,
                                        name="gather_small_wait")
    summed = _slot_sum(all_small, name="sum_small_grads")
    full_shapes = [(CONV_K, 3 * d_dn) if nm == "dn_conv_w" else weights[nm].shape for nm in small]
    g_small = dict(zip(small, _unpack_rows(summed, full_shapes)))
    conv_cols = dn_conv_w.shape[2]
    g_small["dn_conv_w"] = lax.dynamic_slice_in_dim(
        g_small["dn_conv_w"], my_index * conv_cols, conv_cols, axis=1).reshape(dn_conv_w.shape)
    packed = [_pack_rows([t[nm] for nm in small]) for t in (weights, mom_m, mom_v, g_small)]
    small_out = _adamw(packed[0], packed[1], packed[2], packed[3][None], name="adamw_small")
    small_shapes = [weights[nm].shape for nm in small]
    for kind, packed_out in enumerate(small_out):
        for nm, val in zip(small, _unpack_rows(packed_out, small_shapes)):
            results.setdefault(nm, [None] * 4)[kind] = val

    (land_b,) = _split_exchange_wait(plan_b, sems_b, src_b, land_b, small_out[0], name="exchange_wait_b")
    res = _adamw_exchanged(w_in_t, m_w_in_t, v_w_in_t, src_b[0], land_b, chip,
                           name="adamw_w_in")
    results["w_in"] = [jnp.transpose(t) for t in res]

    loss = lax.psum(loss_dev[0, 0], ("x", "y", "c"))
    outs = [loss, grad_x[None]]
    for kind in range(4):
        outs += [results[nm][kind].reshape(weights[nm].shape) for nm in names]
    return tuple(outs)
```

```python
import functools
import math

import jax
import jax.numpy as jnp
from jax import lax
from jax.experimental import pallas as pl
from jax.experimental.pallas import tpu as pltpu

F32 = jnp.float32
BF16 = jnp.bfloat16
HIGHEST = lax.Precision.HIGHEST
MESH = pl.DeviceIdType.MESH
N_DEV = 8

EPS = 1e-6
S5_GROUP = 16
S5_STATE = 64
S5_GPB = 8
S5_T = 1024
DN_HEADS = 8
DN_HEAD_DIM = 128
CHUNK = 64
DN_HEADS_PER_STEP = 8
DN_CHUNKS_PER_STEP = 4
CONV_K = 4
BA_PAD = 512

ADAM_LR = 0.001
ADAM_B1 = 0.9
ADAM_B2 = 0.999
ADAM_EPS = 1e-08
ADAM_WD = 0.01
ADAM_STEP = 10

VMEM_LIMIT_BYTES = 48 * 1024 * 1024
ROW_TILE = 256


def _cparams(*sem):
    return pltpu.CompilerParams(dimension_semantics=sem if sem else None,
                                vmem_limit_bytes=VMEM_LIMIT_BYTES)


@jax.custom_jvp
def _sigmoid(x):
    return 1.0 / (1.0 + jnp.exp(-x))


@_sigmoid.defjvp
def _sigmoid_jvp(primals, tangents):
    s = _sigmoid(primals[0])
    return s, tangents[0] * (s * (1.0 - s))


def _silu(x):
    return x * _sigmoid(x)


def _gelu(x):
    return 0.5 * x * (1.0 + jnp.tanh(0.7978845608028654 * (x + 0.044715 * x * x * x)))


def _softplus(x):
    return jnp.maximum(x, 0.0) + jnp.log(1.0 + jnp.exp(-jnp.abs(x)))


def _rmsnorm(x, w):
    return x * lax.rsqrt(jnp.mean(x * x, axis=-1, keepdims=True) + EPS) * w


def _dot(a, b, dims=((1,), (0,)), precision=None):
    return lax.dot_general(a, b, (dims, ((), ())), precision=precision,
                           preferred_element_type=F32)


def _bdot(a, b, dims=((1,), (0,))):
    return _dot(a.astype(BF16), b.astype(BF16), dims)


def _split_bf16(a):
    hi = a.astype(BF16)
    return hi, (a - hi.astype(F32)).astype(BF16)


def _dot3_dims(a, b, dims):
    ah, al = _split_bf16(a)
    bh, bl = _split_bf16(b)
    return _dot(ah, bh, dims) + (_dot(ah, bl, dims) + _dot(al, bh, dims))


@jax.custom_vjp
def _dot3(a, b):
    return _dot3_dims(a, b, ((1,), (0,)))


def _dot3_fwd(a, b):
    return _dot3(a, b), (a, b)


def _dot3_bwd(res, g):
    a, b = res
    return _dot3_dims(g, b, ((1,), (1,))), _dot3_dims(a, g, ((0,), (0,)))


_dot3.defvjp(_dot3_fwd, _dot3_bwd)


def _mm(a, b, *, ta=False, tb=False, out_dtype=F32, tm=512, tn=512, tk=None, after=None, b_rows=None,
        addend=None, name):
    k_dim, m_dim = (a.shape if ta else a.shape[::-1])
    b_rows = b.shape[0] if b_rows is None else b_rows
    n_dim = b_rows if tb else b.shape[1]
    assert (b.shape[1] if tb else b_rows) == k_dim and b_rows <= b.shape[0]
    tm, tn = min(tm, m_dim), min(tn, n_dim)
    tk = k_dim if tk is None else tk
    assert m_dim % tm == 0 and n_dim % tn == 0 and k_dim % tk == 0
    nk = k_dim // tk
    a_spec = (pl.BlockSpec((tk, tm), lambda i, j, k: (k, i)) if ta
              else pl.BlockSpec((tm, tk), lambda i, j, k: (i, k)))
    b_spec = (pl.BlockSpec((tn, tk), lambda i, j, k: (j, k)) if tb
              else pl.BlockSpec((tk, tn), lambda i, j, k: (k, j)))
    dims = ((0 if ta else 1,), (1 if tb else 0,))

    extras = ([after] if after is not None else []) + ([addend] if addend is not None else [])
    extra_specs = ([pl.BlockSpec((8, 128), lambda i, j, k: (0, 0))] if after is not None else []) + (
        [pl.BlockSpec((tm, tn), lambda i, j, k: (i, j))] if addend is not None else [])

    def body(a_ref, b_ref, *rest):
        o_ref, *scratch = rest[len(extras):]
        p = _bdot(a_ref[...], b_ref[...], dims)
        finish = (lambda v: v + rest[len(extras) - 1][...]) if addend is not None else (lambda v: v)
        if nk == 1:
            o_ref[...] = finish(p).astype(o_ref.dtype)
        else:
            acc = scratch[0]
            k = pl.program_id(2)

            @pl.when(k == 0)
            def _():
                acc[...] = p

            @pl.when(k > 0)
            def _():
                acc[...] += p

            @pl.when(k == nk - 1)
            def _():
                o_ref[...] = finish(acc[...]).astype(o_ref.dtype)

    return pl.pallas_call(
        body, name=name,
        out_shape=jax.ShapeDtypeStruct((m_dim, n_dim), out_dtype),
        grid=(m_dim // tm, n_dim // tn, nk),
        in_specs=[a_spec, b_spec] + extra_specs,
        out_specs=pl.BlockSpec((tm, tn), lambda i, j, k: (i, j)),
        scratch_shapes=[pltpu.VMEM((tm, tn), F32)] if nk > 1 else [],
        compiler_params=_cparams("parallel", "parallel", "arbitrary"),
    )(a, b, *extras)


def _rms_fwd(x, w):
    l, d = x.shape

    def body(x_ref, w_ref, h_ref):
        h_ref[...] = _rmsnorm(x_ref[...], w_ref[...]).astype(BF16)

    return pl.pallas_call(
        body, name="rms_fwd",
        out_shape=jax.ShapeDtypeStruct((l, d), BF16),
        grid=(l // ROW_TILE,),
        in_specs=[pl.BlockSpec((ROW_TILE, d), lambda i: (i, 0)),
                  pl.BlockSpec((1, d), lambda i: (0, 0))],
        out_specs=pl.BlockSpec((ROW_TILE, d), lambda i: (i, 0)),
        compiler_params=_cparams("parallel"),
    )(x, w)


def _rms_bwd(x, w, dh, dres):
    l, d = x.shape

    def body(x_ref, w_ref, dh_ref, dres_ref, dx_ref, dw_ref):
        _, vjp = jax.vjp(_rmsnorm, x_ref[...], w_ref[...])
        dx, dw = vjp(dh_ref[...])
        dx_ref[...] = dx + dres_ref[...]

        @pl.when(pl.program_id(0) == 0)
        def _():
            dw_ref[...] = jnp.zeros_like(dw_ref)

        dw_ref[...] += dw

    row = pl.BlockSpec((ROW_TILE, d), lambda i: (i, 0))
    vec = pl.BlockSpec((1, d), lambda i: (0, 0))
    return pl.pallas_call(
        body, name="rms_bwd",
        out_shape=(jax.ShapeDtypeStruct((l, d), F32), jax.ShapeDtypeStruct((1, d), F32)),
        grid=(l // ROW_TILE,),
        in_specs=[row, vec, row, row],
        out_specs=(row, vec),
        compiler_params=_cparams("arbitrary"),
    )(x, w, dh, dres)


def _final(x, r, fw, target):
    l, d = x.shape

    def per_row_loss(x2, w, tgt):
        err = _rmsnorm(x2, w) - tgt
        return 0.5 * jnp.mean(err * err, axis=-1, keepdims=True)

    def body(x_ref, r_ref, w_ref, t_ref, dx_ref, dxb_ref, loss_ref, dw_ref):
        x2 = x_ref[...] + r_ref[...]
        rows, vjp = jax.vjp(functools.partial(per_row_loss, tgt=t_ref[...]), x2, w_ref[...])
        dx2, dw = vjp(jnp.ones_like(rows))
        dx_ref[...] = dx2
        dxb_ref[...] = dx2.astype(BF16)

        @pl.when(pl.program_id(0) == 0)
        def _():
            dw_ref[...] = jnp.zeros_like(dw_ref)
            loss_ref[...] = jnp.zeros_like(loss_ref)

        dw_ref[...] += dw
        loss_ref[...] += jnp.sum(rows, axis=0, keepdims=True)

    row = pl.BlockSpec((ROW_TILE, d), lambda i: (i, 0))
    vec = pl.BlockSpec((1, d), lambda i: (0, 0))
    return pl.pallas_call(
        body, name="final_norm_loss",
        out_shape=(jax.ShapeDtypeStruct((l, d), F32), jax.ShapeDtypeStruct((l, d), BF16),
                   jax.ShapeDtypeStruct((1, 1), F32), jax.ShapeDtypeStruct((1, d), F32)),
        grid=(l // ROW_TILE,),
        in_specs=[row, row, vec, row],
        out_specs=(row, row, pl.BlockSpec((1, 1), lambda i: (0, 0)), vec),
        compiler_params=_cparams("arbitrary"),
    )(x, r, fw, target)


def _merge_fn(gs, gd, ys, yd):
    return _sigmoid(gs) * ys + _sigmoid(gd) * yd


def _merge_w_out(proj, off_gs, off_gd, ys, yd, wout):
    l, d = ys.shape
    tn = min(512, d)
    blk = lambda off: pl.BlockSpec((ROW_TILE, d), lambda i, j: (i, off // d))

    def body(gs_ref, gd_ref, ys_ref, yd_ref, w_ref, mixed_ref, o_ref):
        @pl.when(pl.program_id(1) == 0)
        def _():
            mixed_ref[...] = _merge_fn(gs_ref[...], gd_ref[...], ys_ref[...], yd_ref[...]).astype(BF16)

        o_ref[...] = _dot(mixed_ref[...], w_ref[...])

    return pl.pallas_call(
        body, name="merge_w_out",
        out_shape=(jax.ShapeDtypeStruct((l, d), BF16), jax.ShapeDtypeStruct((l, d), F32)),
        grid=(l // ROW_TILE, d // tn),
        in_specs=[blk(off_gs), blk(off_gd), blk(0), blk(0), pl.BlockSpec((d, tn), lambda i, j: (0, j))],
        out_specs=(pl.BlockSpec((ROW_TILE, d), lambda i, j: (i, 0)),
                   pl.BlockSpec((ROW_TILE, tn), lambda i, j: (i, j))),
        compiler_params=_cparams("parallel", "arbitrary"),
    )(proj, proj, ys, yd, wout)


def _merge_bwd(proj, off_gs, off_gd, ys, yd, dmixed):
    l, d = ys.shape
    cw = min(1024, d)
    blk = lambda off: pl.BlockSpec((ROW_TILE, cw), lambda i, j: (i, off // cw + j))

    def body(gs_ref, gd_ref, ys_ref, yd_ref, dm_ref, dgs_ref, dgd_ref, dys_ref, dyd_ref):
        _, vjp = jax.vjp(_merge_fn, gs_ref[...], gd_ref[...], ys_ref[...], yd_ref[...])
        dgs, dgd, dys, dyd = vjp(dm_ref[...])
        dgs_ref[...] = dgs.astype(BF16)
        dgd_ref[...] = dgd.astype(BF16)
        dys_ref[...] = dys.astype(BF16)
        dyd_ref[...] = dyd.astype(BF16)

    out = jax.ShapeDtypeStruct((l, d), BF16)
    return pl.pallas_call(
        body, name="merge_bwd",
        out_shape=(out, out, out, out),
        grid=(l // ROW_TILE, d // cw),
        in_specs=[blk(off_gs), blk(off_gd), blk(0), blk(0), blk(0)],
        out_specs=(blk(0), blk(0), blk(0), blk(0)),
        compiler_params=_cparams("parallel", "parallel"),
    )(proj, proj, ys, yd, dmixed)


def _s5_disc_fn(lam_re, lam_im, log_step):
    step = jnp.exp(log_step)
    mag = jnp.exp(lam_re * step)
    abar_re = mag * jnp.cos(lam_im * step)
    abar_im = mag * jnp.sin(lam_im * step)
    den = lam_re * lam_re + lam_im * lam_im
    xr = abar_re - 1.0
    f_re = (xr * lam_re + abar_im * lam_im) / den
    f_im = (abar_im * lam_re - xr * lam_im) / den
    return abar_re, abar_im, f_re, f_im


def _s5_disc_fwd(lam_re, lam_im, log_step):
    g, p = lam_re.shape

    def body(lr_ref, li_ref, ls_ref, ar_ref, ai_ref, fr_ref, fi_ref):
        ar, ai, fr, fi = _s5_disc_fn(lr_ref[...], li_ref[...], ls_ref[...])
        ar_ref[...] = ar
        ai_ref[...] = ai
        fr_ref[...] = fr
        fi_ref[...] = fi

    o = jax.ShapeDtypeStruct((g, p), F32)
    return pl.pallas_call(body, name="s5_disc_fwd", out_shape=(o, o, o, o),
                          compiler_params=_cparams())(lam_re, lam_im, log_step)


def _s5_disc_bwd(lam_re, lam_im, log_step, dar, dai, dfr, dfi):
    g, p = lam_re.shape

    def body(lr_ref, li_ref, ls_ref, dar_ref, dai_ref, dfr_ref, dfi_ref, dlr_ref, dli_ref, dls_ref):
        _, vjp = jax.vjp(_s5_disc_fn, lr_ref[...], li_ref[...], ls_ref[...])
        dlr, dli, dls = vjp((dar_ref[...], dai_ref[...], dfr_ref[...], dfi_ref[...]))
        dlr_ref[...] = dlr
        dli_ref[...] = dli
        dls_ref[...] = dls

    o = jax.ShapeDtypeStruct((g, p), F32)
    return pl.pallas_call(body, name="s5_disc_bwd",
                          out_shape=(o, o, jax.ShapeDtypeStruct((g, 1), F32)),
                          compiler_params=_cparams())(lam_re, lam_im, log_step, dar, dai, dfr, dfi)


def _s5_bbar_fwd(f_re, f_im, b_re, b_im):
    n, c = b_re.shape

    def body(fr_ref, fi_ref, br_ref, bi_ref, or_ref, oi_ref):
        fr, fi, br, bi = fr_ref[...], fi_ref[...], br_ref[...], bi_ref[...]
        or_ref[...] = fr * br - fi * bi
        oi_ref[...] = fr * bi + fi * br

    o = jax.ShapeDtypeStruct((n, c), F32)
    return pl.pallas_call(body, name="s5_bbar_fwd", out_shape=(o, o),
                          compiler_params=_cparams())(f_re, f_im, b_re, b_im)


def _s5_bbar_bwd(f_re, f_im, b_re, b_im, dbr, dbi):
    n, c = b_re.shape

    def body(fr_ref, fi_ref, br_ref, bi_ref, dor_ref, doi_ref, dfr_ref, dfi_ref, dbr_ref, dbi_ref):
        fr, fi, br, bi = fr_ref[...], fi_ref[...], br_ref[...], bi_ref[...]
        dor, doi = dor_ref[...], doi_ref[...]
        dfr_ref[...] = jnp.sum(dor * br + doi * bi, axis=-1, keepdims=True)
        dfi_ref[...] = jnp.sum(doi * br - dor * bi, axis=-1, keepdims=True)
        dbr_ref[...] = fr * dor + fi * doi
        dbi_ref[...] = fr * doi - fi * dor

    col = jax.ShapeDtypeStruct((n, 1), F32)
    o = jax.ShapeDtypeStruct((n, c), F32)
    return pl.pallas_call(body, name="s5_bbar_bwd", out_shape=(col, col, o, o),
                          compiler_params=_cparams())(f_re, f_im, b_re, b_im, dbr, dbi)


SUBLANES = 8


def _scan_groups(xr, xi, ar, ai, reverse):
    t, n = xr.shape
    xr, xi = xr.reshape(t // SUBLANES, SUBLANES, n), xi.reshape(t // SUBLANES, SUBLANES, n)
    sub = lax.broadcasted_iota(jnp.int32, (1, SUBLANES, 1), 1)
    pr, pi = ar.reshape(1, 1, n), ai.reshape(1, 1, n)
    for sh in (1, 2, 4):
        keep = (sub < SUBLANES - sh) if reverse else (sub >= sh)
        cr, ci = jnp.where(keep, pr, 0.0), jnp.where(keep, pi, 0.0)
        shift = SUBLANES - sh if reverse else sh
        sr, si = pltpu.roll(xr, shift, 1), pltpu.roll(xi, shift, 1)
        xr, xi = xr + cr * sr - ci * si, xi + cr * si + ci * sr
        pr, pi = pr * pr - pi * pi, 2.0 * pr * pi
    return xr.reshape(t, n), xi.reshape(t, n)


def _scan_rows(xr, xi, ar, ai, cr, ci, sr_ref, si_ref, reverse):
    t, n = xr.shape
    xr, xi = _scan_groups(xr, xi, ar, ai, reverse)
    sr_ref[...] = xr
    si_ref[...] = xi
    sub = lax.broadcasted_iota(jnp.int32, (SUBLANES, n), 0)
    seed = sub == (SUBLANES - 1 if reverse else 0)
    pwr, pwi = _scan_groups(jnp.where(seed, ar, 0.0), jnp.where(seed, ai, 0.0), ar, ai, reverse)
    groups = range(t // SUBLANES)
    edge = 0 if reverse else SUBLANES - 1
    for g in (reversed(groups) if reverse else groups):
        rows = slice(g * SUBLANES, (g + 1) * SUBLANES)
        vr = sr_ref[rows, :] + (pwr * cr - pwi * ci)
        vi = si_ref[rows, :] + (pwr * ci + pwi * cr)
        sr_ref[rows, :] = vr
        si_ref[rows, :] = vi
        cr, ci = vr[edge:edge + 1, :], vi[edge:edge + 1, :]
    return cr, ci


def _s5_states(u_bf, bbr, bbi, ar, ai, cr, ci, sr_ref, si_ref):
    return _scan_rows(_dot(u_bf, bbr), _dot(u_bf, bbi), ar, ai, cr, ci, sr_ref, si_ref, reverse=False)


def _s5_fwd(proj, bbr, bbi, a_re, a_im, ctr, cti, d_skip, d_s5):
    l = proj.shape[0]
    nb, uc, ns = bbr.shape
    t = min(S5_T, l)
    nt = l // t

    def body(u_ref, bbr_ref, bbi_ref, ar_ref, ai_ref, ctr_ref, cti_ref, d_ref,
             y_ref, car_r_ref, car_i_ref, sr_ref, si_ref, cr, ci):
        @pl.when(pl.program_id(1) == 0)
        def _():
            cr[...] = jnp.zeros_like(cr)
            ci[...] = jnp.zeros_like(ci)

        car_r_ref[...] = cr[...]
        car_i_ref[...] = ci[...]
        u = u_ref[...]
        cr[...], ci[...] = _s5_states(u.astype(BF16), bbr_ref[...], bbi_ref[...], ar_ref[...],
                                      ai_ref[...], cr[...], ci[...], sr_ref, si_ref)
        y_ref[...] = (_bdot(sr_ref[...], ctr_ref[...]) - _bdot(si_ref[...], cti_ref[...])
                      + d_ref[...] * u)

    per_block = lambda shape: pl.BlockSpec((None,) + shape, lambda b, n: (b, 0, 0))
    return pl.pallas_call(
        body, name="s5_fwd",
        out_shape=(jax.ShapeDtypeStruct((l, d_s5), F32),
                   jax.ShapeDtypeStruct((nt, 1, nb * ns), F32),
                   jax.ShapeDtypeStruct((nt, 1, nb * ns), F32),
                   jax.ShapeDtypeStruct((l, nb * ns), F32),
                   jax.ShapeDtypeStruct((l, nb * ns), F32)),
        grid=(nb, nt),
        in_specs=[pl.BlockSpec((t, uc), lambda b, n: (n, b)),
                  per_block((uc, ns)), per_block((uc, ns)),
                  per_block((1, ns)), per_block((1, ns)),
                  per_block((ns, uc)), per_block((ns, uc)),
                  pl.BlockSpec((1, uc), lambda b, n: (0, b))],
        out_specs=(pl.BlockSpec((t, uc), lambda b, n: (n, b)),
                   pl.BlockSpec((None, 1, ns), lambda b, n: (n, 0, b)),
                   pl.BlockSpec((None, 1, ns), lambda b, n: (n, 0, b)),
                   pl.BlockSpec((t, ns), lambda b, n: (n, b)),
                   pl.BlockSpec((t, ns), lambda b, n: (n, b))),
        scratch_shapes=[pltpu.VMEM((1, ns), F32), pltpu.VMEM((1, ns), F32)],
        compiler_params=_cparams("parallel", "arbitrary"),
    )(proj, bbr, bbi, a_re, a_im, ctr, cti, d_skip)


def _s5_bwd(proj, dy, bbr, bbi, a_re, a_im, cbr, cbi, d_skip, car_r, car_i, states_r, states_i):
    l, d_s5 = dy.shape
    nb, uc, ns = bbr.shape
    t = min(S5_T, l)
    nt = l // t

    def body(u_ref, dy_ref, bbr_ref, bbi_ref, ar_ref, ai_ref, cbr_ref, cbi_ref, d_ref,
             car_r_ref, car_i_ref, sr_ref, si_ref,
             du_ref, dar_ref, dai_ref, dbbr_ref, dbbi_ref, dcbr_ref, dcbi_ref, dd_ref, gcr, gci,
             gr_ref, gi_ref):
        @pl.when(pl.program_id(1) == 0)
        def _():
            gcr[...] = jnp.zeros_like(gcr)
            gci[...] = jnp.zeros_like(gci)
            for ref in (dar_ref, dai_ref, dbbr_ref, dbbi_ref, dcbr_ref, dcbi_ref, dd_ref):
                ref[...] = jnp.zeros_like(ref)

        row = lax.broadcasted_iota(jnp.int32, (t, 1), 0)
        u, dy = u_ref[...], dy_ref[...]
        u_bf, dy_bf = u.astype(BF16), dy.astype(BF16)
        ar, ai = ar_ref[...], ai_ref[...]
        cr, ci = car_r_ref[...], car_i_ref[...]
        sr, si = sr_ref[...], si_ref[...]
        first = row == 0
        pr = jnp.where(first, cr, pltpu.roll(sr, 1, 0))
        pi = jnp.where(first, ci, pltpu.roll(si, 1, 0))
        gcr[...], gci[...] = _scan_rows(_dot(dy_bf, cbr_ref[...]), -_dot(dy_bf, cbi_ref[...]), ar, -ai,
                                        gcr[...], gci[...], gr_ref, gi_ref, reverse=True)
        gr, gi = gr_ref[...], gi_ref[...]
        dar_ref[...] += jnp.sum(gr * pr + gi * pi, axis=0, keepdims=True)
        dai_ref[...] += jnp.sum(gi * pr - gr * pi, axis=0, keepdims=True)
        gr_bf, gi_bf = gr.astype(BF16), gi.astype(BF16)
        tn = ((0,), (0,))
        dbbr_ref[...] += _dot(u_bf, gr_bf, tn)
        dbbi_ref[...] += _dot(u_bf, gi_bf, tn)
        dcbr_ref[...] += _dot(dy_bf, sr.astype(BF16), tn)
        dcbi_ref[...] -= _dot(dy_bf, si.astype(BF16), tn)
        nt_dims = ((1,), (1,))
        du = _dot(gr_bf, bbr_ref[...], nt_dims) + _dot(gi_bf, bbi_ref[...], nt_dims) + dy * d_ref[...]
        du_ref[...] = du.astype(BF16)
        dd_ref[...] += jnp.sum(dy * u, axis=0, keepdims=True)

    rev = lambda n: nt - 1 - n
    per_block = lambda shape: pl.BlockSpec((None,) + shape, lambda b, n: (b, 0, 0))
    acc = jax.ShapeDtypeStruct((nb, uc, ns), F32)
    vec = jax.ShapeDtypeStruct((nb, 1, ns), F32)
    return pl.pallas_call(
        body, name="s5_bwd",
        out_shape=(jax.ShapeDtypeStruct((l, d_s5), BF16), vec, vec, acc, acc, acc, acc,
                   jax.ShapeDtypeStruct((1, d_s5), F32)),
        grid=(nb, nt),
        in_specs=[pl.BlockSpec((t, uc), lambda b, n: (rev(n), b)),
                  pl.BlockSpec((t, uc), lambda b, n: (rev(n), b)),
                  per_block((uc, ns)), per_block((uc, ns)),
                  per_block((1, ns)), per_block((1, ns)),
                  per_block((uc, ns)), per_block((uc, ns)),
                  pl.BlockSpec((1, uc), lambda b, n: (0, b)),
                  pl.BlockSpec((None, 1, ns), lambda b, n: (rev(n), 0, b)),
                  pl.BlockSpec((None, 1, ns), lambda b, n: (rev(n), 0, b)),
                  pl.BlockSpec((t, ns), lambda b, n: (rev(n), b)),
                  pl.BlockSpec((t, ns), lambda b, n: (rev(n), b))],
        out_specs=(pl.BlockSpec((t, uc), lambda b, n: (rev(n), b)),
                   per_block((1, ns)), per_block((1, ns)),
                   per_block((uc, ns)), per_block((uc, ns)),
                   per_block((uc, ns)), per_block((uc, ns)),
                   pl.BlockSpec((1, uc), lambda b, n: (0, b))),
        scratch_shapes=[pltpu.VMEM((1, ns), F32), pltpu.VMEM((1, ns), F32)]
        + [pltpu.VMEM((t, ns), F32)] * 2,
        compiler_params=_cparams("parallel", "arbitrary"),
    )(proj, dy, bbr, bbi, a_re, a_im, cbr, cbi, d_skip, car_r, car_i, states_r, states_i)


def _s5_glu_fwd(y1, proj, off_z, wglu):
    l, d = y1.shape

    def body(y_ref, z_ref, w_ref, o_ref):
        y2 = _gelu(y_ref[...])
        y3 = y2 * _sigmoid(_bdot(y2, w_ref[...]))
        o_ref[...] = (y3 * _silu(z_ref[...])).astype(BF16)

    return pl.pallas_call(
        body, name="s5_glu_fwd",
        out_shape=jax.ShapeDtypeStruct((l, d), BF16),
        grid=(l // ROW_TILE,),
        in_specs=[pl.BlockSpec((ROW_TILE, d), lambda i: (i, 0)),
                  pl.BlockSpec((ROW_TILE, d), lambda i: (i, off_z // d)),
                  pl.BlockSpec((d, d), lambda i: (0, 0))],
        out_specs=pl.BlockSpec((ROW_TILE, d), lambda i: (i, 0)),
        compiler_params=_cparams("parallel"),
    )(y1, proj, wglu)


def _s5_glu_bwd(y1, proj, off_z, wglu, dout):
    l, d = y1.shape

    def body(y_ref, z_ref, w_ref, do_ref, dy_ref, dz_ref, dw_ref):
        y2, gelu_vjp = jax.vjp(_gelu, y_ref[...])
        z = z_ref[...]
        sz, silu_vjp = jax.vjp(_silu, z)
        y2_bf = y2.astype(BF16)
        sg = _sigmoid(_dot(y2_bf, w_ref[...]))
        dout = do_ref[...]
        dy3 = dout * sz
        dz_ref[...] = silu_vjp(dout * (y2 * sg))[0].astype(BF16)
        dgl = (dy3 * y2 * sg * (1.0 - sg)).astype(BF16)
        dy2 = dy3 * sg + _dot(dgl, w_ref[...], ((1,), (1,)))
        dy_ref[...] = gelu_vjp(dy2)[0]

        @pl.when(pl.program_id(0) == 0)
        def _():
            dw_ref[...] = jnp.zeros_like(dw_ref)

        dw_ref[...] += _dot(y2_bf, dgl, ((0,), (0,)))

    row = pl.BlockSpec((ROW_TILE, d), lambda i: (i, 0))
    full = pl.BlockSpec((d, d), lambda i: (0, 0))
    return pl.pallas_call(
        body, name="s5_glu_bwd",
        out_shape=(jax.ShapeDtypeStruct((l, d), F32), jax.ShapeDtypeStruct((l, d), BF16),
                   jax.ShapeDtypeStruct((d, d), F32)),
        grid=(l // ROW_TILE,),
        in_specs=[row, pl.BlockSpec((ROW_TILE, d), lambda i: (i, off_z // d)), full, row],
        out_specs=(row, row, full),
        compiler_params=_cparams("arbitrary"),
    )(y1, proj, wglu, dout)


def _shift_rows(x, k, back=False):
    if k == 0:
        return x
    t = x.shape[0]
    row = lax.broadcasted_iota(jnp.int32, (t, 1), 0)
    if back:
        return jnp.where(row < t - k, pltpu.roll(x, t - k, 0), 0.0)
    return jnp.where(row >= k, pltpu.roll(x, k, 0), 0.0)


def _dn_conv(x, w_ref):
    return sum(w_ref[CONV_K - 1 - k:CONV_K - k, :] * _shift_rows(x, k) for k in range(CONV_K))


def _dn_post_conv(c, j):
    y = _silu(c)
    n = y * lax.rsqrt(jnp.sum(y * y, axis=-1, keepdims=True) + EPS)
    n = n * jnp.where(j < DN_HEADS, DN_HEAD_DIM ** -0.5, 1.0)
    return jnp.where(j < 2 * DN_HEADS, n, y)


def _dn_prep_fwd(proj, off_qkv, conv_w):
    l = proj.shape[0]
    hd = DN_HEAD_DIM
    nblk = 3 * DN_HEADS

    def body(x_ref, w_ref, o_ref):
        o_ref[...] = _dn_post_conv(_dn_conv(x_ref[...], w_ref), pl.program_id(0))

    return pl.pallas_call(
        body, name="dn_prep_fwd",
        out_shape=jax.ShapeDtypeStruct((l, nblk * hd), F32),
        grid=(nblk,),
        in_specs=[pl.BlockSpec((l, hd), lambda j: (0, off_qkv // hd + j)),
                  pl.BlockSpec((CONV_K, hd), lambda j: (0, j))],
        out_specs=pl.BlockSpec((l, hd), lambda j: (0, j)),
        compiler_params=_cparams("parallel"),
    )(proj, conv_w)


def _dn_prep_bwd(proj, off_qkv, conv_w, dqkv):
    l = proj.shape[0]
    hd = DN_HEAD_DIM
    nblk = 3 * DN_HEADS

    def body(x_ref, w_ref, do_ref, dx_ref, dw_ref):
        x = x_ref[...]
        j = pl.program_id(0)
        _, vjp = jax.vjp(functools.partial(_dn_post_conv, j=j), _dn_conv(x, w_ref))
        dc = vjp(do_ref[...])[0]
        dx = sum(w_ref[CONV_K - 1 - k:CONV_K - k, :] * _shift_rows(dc, k, back=True)
                 for k in range(CONV_K))
        dx_ref[...] = dx.astype(BF16)
        for k in range(CONV_K):
            dw_ref[CONV_K - 1 - k:CONV_K - k, :] = jnp.sum(dc * _shift_rows(x, k), axis=0,
                                                           keepdims=True)

    return pl.pallas_call(
        body, name="dn_prep_bwd",
        out_shape=(jax.ShapeDtypeStruct((l, nblk * hd), BF16),
                   jax.ShapeDtypeStruct((CONV_K, nblk * hd), F32)),
        grid=(nblk,),
        in_specs=[pl.BlockSpec((l, hd), lambda j: (0, off_qkv // hd + j)),
                  pl.BlockSpec((CONV_K, hd), lambda j: (0, j)),
                  pl.BlockSpec((None, l, hd), lambda j: (j // DN_HEADS, 0, j % DN_HEADS))],
        out_specs=(pl.BlockSpec((l, hd), lambda j: (0, j)),
                   pl.BlockSpec((CONV_K, hd), lambda j: (0, j))),
        compiler_params=_cparams("parallel"),
    )(proj, conv_w, dqkv)


def _dn_gate_fn(ba, a_log_row, dt_row):
    lane = lax.broadcasted_iota(jnp.int32, ba.shape, 1)
    beta = _sigmoid(ba)
    g = -jnp.exp(a_log_row) * _softplus(ba + dt_row)
    return jnp.where(lane < DN_HEADS, beta, jnp.where(lane < 2 * DN_HEADS, g, 0.0))


def _dn_gates_fwd(proj, off_ba, a_log_row, dt_row):
    l = proj.shape[0]
    row = pl.BlockSpec((ROW_TILE, 128), lambda i: (i, off_ba // 128))
    vec = pl.BlockSpec((1, 128), lambda i: (0, 0))

    def body(ba_ref, al_ref, dt_ref, o_ref):
        o_ref[...] = _dn_gate_fn(ba_ref[...], al_ref[...], dt_ref[...])

    return pl.pallas_call(
        body, name="dn_gates_fwd",
        out_shape=jax.ShapeDtypeStruct((l, 128), F32),
        grid=(l // ROW_TILE,),
        in_specs=[row, vec, vec],
        out_specs=pl.BlockSpec((ROW_TILE, 128), lambda i: (i, 0)),
        compiler_params=_cparams("parallel"),
    )(proj, a_log_row, dt_row)


def _dn_gates_bwd(proj, off_ba, a_log_row, dt_row, dgb_heads):
    l = proj.shape[0]
    nh = dgb_heads.shape[0]
    row = pl.BlockSpec((ROW_TILE, 128), lambda i: (i, off_ba // 128))
    vec = pl.BlockSpec((1, 128), lambda i: (0, 0))

    def body(ba_ref, al_ref, dt_ref, dg_ref, dba_ref, dal_ref, ddt_ref):
        _, vjp = jax.vjp(_dn_gate_fn, ba_ref[...], al_ref[...], dt_ref[...])
        dgb = dg_ref[0]
        for h in range(1, nh):
            dgb = dgb + dg_ref[h]
        dba, dal, ddt = vjp(dgb)
        dba_ref[...] = dba.astype(BF16)

        @pl.when(pl.program_id(0) == 0)
        def _():
            dal_ref[...] = jnp.zeros_like(dal_ref)
            ddt_ref[...] = jnp.zeros_like(ddt_ref)

        dal_ref[...] += dal
        ddt_ref[...] += ddt

    return pl.pallas_call(
        body, name="dn_gates_bwd",
        out_shape=(jax.ShapeDtypeStruct((l, 128), BF16), jax.ShapeDtypeStruct((1, 128), F32),
                   jax.ShapeDtypeStruct((1, 128), F32)),
        grid=(l // ROW_TILE,),
        in_specs=[row, vec, vec, pl.BlockSpec((nh, ROW_TILE, 128), lambda i: (0, i, 0))],
        out_specs=(pl.BlockSpec((ROW_TILE, 128), lambda i: (i, 0)), vec, vec),
        compiler_params=_cparams("arbitrary"),
    )(proj, a_log_row, dt_row, dgb_heads)


@jax.custom_vjp
def _unit_lower_inverses(a_mats):
    c = a_mats[0].shape[0]
    eye = (lax.broadcasted_iota(jnp.int32, (c, c), 0) == lax.broadcasted_iota(jnp.int32, (c, c), 1)).astype(F32)
    t_inv = [eye - a for a in a_mats]
    power = a_mats
    for _ in range(int(math.log2(c)) - 1):
        power = [_bdot(p, p) for p in power]
        t_inv = [t + _bdot(t, p) for t, p in zip(t_inv, power)]
    return t_inv


def _unit_lower_inverses_fwd(a_mats):
    t_inv = _unit_lower_inverses(a_mats)
    return t_inv, t_inv


def _inverse_cotangents(t_inv, grads):
    right = [_dot3_dims(g, t, ((1,), (1,))) for g, t in zip(grads, t_inv)]
    return [-_dot3_dims(t, r, ((0,), (0,))) for t, r in zip(t_inv, right)]


_unit_lower_inverses.defvjp(_unit_lower_inverses_fwd,
                            lambda t_inv, grads: (_inverse_cotangents(t_inv, grads),))


@jax.custom_vjp
def _kept_inverses(a_mats, t_inv):
    return t_inv


_kept_inverses.defvjp(
    lambda a_mats, t_inv: (t_inv, t_inv),
    lambda t_inv, grads: (_inverse_cotangents(t_inv, grads), [jnp.zeros_like(t) for t in t_inv]))


def _dn_chunk_fn(states, qs, ks, vs, gb, heads, kept_inverses=None, return_inverses=False):
    c = qs[0].shape[0]
    each = lambda f, *lists: [f(*args) for args in zip(*lists)]
    lane = lax.broadcasted_iota(jnp.int32, gb.shape, 1)
    ri = lax.broadcasted_iota(jnp.int32, (c, c), 0)
    ci = lax.broadcasted_iota(jnp.int32, (c, c), 1)
    causal, strict = ri >= ci, ri > ci
    eye = (ri == ci).astype(F32)
    rowi = lax.broadcasted_iota(jnp.int32, (c, 1), 0)
    nt_dims = ((1,), (1,))
    hdot = functools.partial(_dot, precision=HIGHEST)

    pick = lambda m, at: jnp.sum(jnp.where(lane == at, m, 0.0), axis=1, keepdims=True)
    gb_cum = hdot(causal.astype(F32), gb)
    beta = [pick(gb, h) for h in heads]
    gc = [pick(gb_cum, h + DN_HEADS) for h in heads]
    gc_row = each(lambda g: jnp.sum(eye * g, axis=0, keepdims=True), gc)
    decay = each(lambda g, gr: jnp.where(causal, jnp.exp(jnp.where(causal, g - gr, 0.0)), 0.0),
                 gc, gc_row)
    kk = each(lambda k: _bdot(k, k, nt_dims), ks)
    a_mat = each(lambda b, m, dc: jnp.where(strict, b * m * dc, 0.0), beta, kk, decay)

    t_inv = (_unit_lower_inverses(a_mat) if kept_inverses is None
             else _kept_inverses(a_mat, kept_inverses))
    egc = each(jnp.exp, gc)
    u_c = each(lambda t, v, b: _dot3(t, v * b), t_inv, vs, beta)
    w_c = each(lambda t, k, b, e: _dot3(t, k * (b * e)), t_inv, ks, beta, egc)
    qk = each(lambda q, k, dc: _bdot(q, k, nt_dims) * dc, qs, ks, decay)
    g_end = each(lambda g: jnp.sum(jnp.where(rowi == c - 1, g, 0.0), axis=0, keepdims=True), gc)
    v_new = each(lambda u, w, s: u - _bdot(w, s), u_c, w_c, states)
    o = each(lambda q, e, s, m, vn: _bdot(q * e, s) + _bdot(m, vn), qs, egc, states, qk, v_new)
    new_states = each(
        lambda s, ge, k, g, vn: s * jnp.exp(ge) + _bdot(k * jnp.exp(ge - g), vn, ((0,), (0,))),
        states, g_end, ks, gc, v_new)
    return (o, new_states, t_inv) if return_inverses else (o, new_states)


def _dn_chunk_specs(order):
    hd, nh, hps, cps = DN_HEAD_DIM, DN_HEADS, DN_HEADS_PER_STEP, DN_CHUNKS_PER_STEP
    rows = cps * CHUNK
    qkv = lambda part: pl.BlockSpec((rows, hps * hd), lambda h, n: (order(n), part * (nh // hps) + h))
    gb = pl.BlockSpec((rows, 128), lambda h, n: (order(n), 0))
    state = pl.BlockSpec((hps, cps, hd, hd), lambda h, n: (h, order(n), 0, 0))
    inverse = pl.BlockSpec((hps, cps, CHUNK, CHUNK), lambda h, n: (h, order(n), 0, 0))
    return qkv, gb, state, inverse


def _dn_chunk_fwd(qkv, gb):
    l = qkv.shape[0]
    hd, nh, hps, cps = DN_HEAD_DIM, DN_HEADS, DN_HEADS_PER_STEP, DN_CHUNKS_PER_STEP
    n_chunks = l // CHUNK
    qkv_spec, gb_spec, state_spec, inverse_spec = _dn_chunk_specs(lambda n: n)

    def body(q_ref, k_ref, v_ref, gb_ref, o_ref, s_ref, t_ref, state):
        @pl.when(pl.program_id(1) == 0)
        def _():
            state[...] = jnp.zeros_like(state)

        cols = [slice(i * hd, (i + 1) * hd) for i in range(hps)]
        heads = [pl.program_id(0) * hps + i for i in range(hps)]
        states = [state[i] for i in range(hps)]
        for j in range(cps):
            rows = slice(j * CHUNK, (j + 1) * CHUNK)
            for i in range(hps):
                s_ref[i, j] = states[i]
            o, states, t_inv = _dn_chunk_fn(
                states, [q_ref[rows, cs] for cs in cols], [k_ref[rows, cs] for cs in cols],
                [v_ref[rows, cs] for cs in cols], gb_ref[rows, :], heads, return_inverses=True)
            for i in range(hps):
                o_ref[rows, cols[i]] = o[i]
                t_ref[i, j] = t_inv[i]
        for i in range(hps):
            state[i] = states[i]

    return pl.pallas_call(
        body, name="dn_chunk_fwd",
        out_shape=(jax.ShapeDtypeStruct((l, nh * hd), F32),
                   jax.ShapeDtypeStruct((nh, n_chunks, hd, hd), F32),
                   jax.ShapeDtypeStruct((nh, n_chunks, CHUNK, CHUNK), F32)),
        grid=(nh // hps, n_chunks // cps),
        in_specs=[qkv_spec(0), qkv_spec(1), qkv_spec(2), gb_spec],
        out_specs=(pl.BlockSpec((cps * CHUNK, hps * hd), lambda h, n: (n, h)), state_spec, inverse_spec),
        scratch_shapes=[pltpu.VMEM((hps, hd, hd), F32)],
        compiler_params=_cparams("parallel", "arbitrary"),
    )(qkv, qkv, qkv, gb)


def _dn_chunk_bwd(qkv, gb, states, inverses, do):
    l = qkv.shape[0]
    hd, nh, hps, cps = DN_HEAD_DIM, DN_HEADS, DN_HEADS_PER_STEP, DN_CHUNKS_PER_STEP
    n_steps = l // (cps * CHUNK)
    rev = lambda n: n_steps - 1 - n
    qkv_spec, gb_spec, state_spec, inverse_spec = _dn_chunk_specs(rev)

    def body(q_ref, k_ref, v_ref, gb_ref, s_ref, t_ref, do_ref, dqkv_ref, dgb_ref, dstate):
        @pl.when(pl.program_id(1) == 0)
        def _():
            dstate[...] = jnp.zeros_like(dstate)

        cols = [slice(i * hd, (i + 1) * hd) for i in range(hps)]
        heads = [pl.program_id(0) * hps + i for i in range(hps)]
        dstates = [dstate[i] for i in range(hps)]
        for j in reversed(range(cps)):
            rows = slice(j * CHUNK, (j + 1) * CHUNK)
            fn = functools.partial(_dn_chunk_fn, heads=heads, kept_inverses=[t_ref[i, j] for i in range(hps)])
            _, vjp = jax.vjp(fn, [s_ref[i, j] for i in range(hps)], [q_ref[rows, cs] for cs in cols],
                             [k_ref[rows, cs] for cs in cols], [v_ref[rows, cs] for cs in cols],
                             gb_ref[rows, :])
            dstates, dq, dk, dv, dgb = vjp(([do_ref[rows, cs] for cs in cols], dstates))
            for i in range(hps):
                dqkv_ref[0, rows, cols[i]] = dq[i]
                dqkv_ref[1, rows, cols[i]] = dk[i]
                dqkv_ref[2, rows, cols[i]] = dv[i]
            dgb_ref[rows, :] = dgb
        for i in range(hps):
            dstate[i] = dstates[i]

    head_out = pl.BlockSpec((cps * CHUNK, hps * hd), lambda h, n: (rev(n), h))
    return pl.pallas_call(
        body, name="dn_chunk_bwd",
        out_shape=(jax.ShapeDtypeStruct((3, l, nh * hd), F32),
                   jax.ShapeDtypeStruct((nh // hps, l, 128), F32)),
        grid=(nh // hps, n_steps),
        in_specs=[qkv_spec(0), qkv_spec(1), qkv_spec(2), gb_spec, state_spec, inverse_spec, head_out],
        out_specs=(pl.BlockSpec((3, cps * CHUNK, hps * hd), lambda h, n: (0, rev(n), h)),
                   pl.BlockSpec((None, cps * CHUNK, 128), lambda h, n: (h, rev(n), 0))),
        scratch_shapes=[pltpu.VMEM((hps, hd, hd), F32)],
        compiler_params=_cparams("parallel", "arbitrary"),
    )(qkv, qkv, qkv, gb, states, inverses, do)


def _dn_out_fn(o, z, w):
    return _rmsnorm(o, w) * _silu(z)


def _dn_out_fwd(o, proj, off_z, w):
    l, d = o.shape
    hd = DN_HEAD_DIM
    tr = min(4 * ROW_TILE, l)
    blk = lambda off: pl.BlockSpec((tr, hd), lambda i, h: (i, off // hd + h))

    def body(o_ref, z_ref, w_ref, out_ref):
        out_ref[...] = _dn_out_fn(o_ref[...], z_ref[...], w_ref[...]).astype(BF16)

    return pl.pallas_call(
        body, name="dn_out_fwd",
        out_shape=jax.ShapeDtypeStruct((l, d), BF16),
        grid=(l // tr, d // hd),
        in_specs=[blk(0), blk(off_z), pl.BlockSpec((1, hd), lambda i, h: (0, 0))],
        out_specs=blk(0),
        compiler_params=_cparams("parallel", "parallel"),
    )(o, proj, w)


def _dn_out_bwd(o, proj, off_z, w, dout):
    l, d = o.shape
    hd = DN_HEAD_DIM
    tr = min(4 * ROW_TILE, l)
    blk = lambda off: pl.BlockSpec((tr, hd), lambda i, h: (i, off // hd + h))
    vec = pl.BlockSpec((1, hd), lambda i, h: (0, 0))

    def body(o_ref, z_ref, w_ref, dout_ref, do_ref, dz_ref, dw_ref):
        _, vjp = jax.vjp(_dn_out_fn, o_ref[...], z_ref[...], w_ref[...])
        do, dz, dw = vjp(dout_ref[...])
        do_ref[...] = do
        dz_ref[...] = dz.astype(BF16)

        @pl.when((pl.program_id(0) == 0) & (pl.program_id(1) == 0))
        def _():
            dw_ref[...] = jnp.zeros_like(dw_ref)

        dw_ref[...] += dw

    return pl.pallas_call(
        body, name="dn_out_bwd",
        out_shape=(jax.ShapeDtypeStruct((l, d), F32), jax.ShapeDtypeStruct((l, d), BF16),
                   jax.ShapeDtypeStruct((1, hd), F32)),
        grid=(l // tr, d // hd),
        in_specs=[blk(0), blk(off_z), vec, blk(0)],
        out_specs=(blk(0), blk(0), vec),
        compiler_params=_cparams("arbitrary", "arbitrary"),
    )(o, proj, w, dout)


def _tile_2d(rows, cols, budget_bytes=1 << 20):
    for tr in (rows, 4096, 2048, 1024, 512, 256, 128, 64, 32, 16):
        if tr <= rows and rows % tr == 0 and tr * cols * 4 <= budget_bytes:
            return tr, cols
    for tc in (2048, 1024, 512, 256, 128):
        if cols % tc == 0 and rows * tc * 4 <= 2 * budget_bytes:
            return rows, tc
    raise ValueError((rows, cols))


def _adamw_update(g, w_ref, m_ref, v_ref, go_ref, d_ref, mo_ref, vo_ref):
    c1 = 1.0 / (1.0 - ADAM_B1 ** ADAM_STEP)
    c2 = 1.0 / (1.0 - ADAM_B2 ** ADAM_STEP)
    m_new = ADAM_B1 * m_ref[...] + (1.0 - ADAM_B1) * g
    v_new = ADAM_B2 * v_ref[...] + (1.0 - ADAM_B2) * (g * g)
    go_ref[...] = g
    mo_ref[...] = m_new
    vo_ref[...] = v_new
    d_ref[...] = -ADAM_LR * ((m_new * c1) / (jnp.sqrt(v_new * c2) + ADAM_EPS) + ADAM_WD * w_ref[...])


def _adamw(w, m, v, gslots, name):
    rows, cols = w.shape
    ns = gslots.shape[0]
    tr, tc = _tile_2d(rows, cols)

    def body(w_ref, m_ref, v_ref, g_ref, go_ref, d_ref, mo_ref, vo_ref):
        g = g_ref[0].astype(F32)
        for s in range(1, ns):
            g = g + g_ref[s].astype(F32)
        _adamw_update(g, w_ref, m_ref, v_ref, go_ref, d_ref, mo_ref, vo_ref)

    blk = pl.BlockSpec((tr, tc), lambda i, j: (i, j))
    o = jax.ShapeDtypeStruct((rows, cols), F32)
    return pl.pallas_call(
        body, name=name, out_shape=(o, o, o, o),
        grid=(rows // tr, cols // tc),
        in_specs=[blk, blk, blk, pl.BlockSpec((ns, tr, tc), lambda i, j: (0, i, j))],
        out_specs=(blk, blk, blk, blk),
        compiler_params=_cparams("parallel", "parallel"),
    )(w, m, v, gslots)


def _slot_sum(gslots, name):
    ns, rows, cols = gslots.shape
    tr, tc = _tile_2d(rows, cols)

    def body(g_ref, o_ref):
        g = g_ref[0]
        for s in range(1, ns):
            g = g + g_ref[s]
        o_ref[...] = g

    return pl.pallas_call(
        body, name=name, out_shape=jax.ShapeDtypeStruct((rows, cols), F32),
        grid=(rows // tr, cols // tc),
        in_specs=[pl.BlockSpec((ns, tr, tc), lambda i, j: (0, i, j))],
        out_specs=pl.BlockSpec((tr, tc), lambda i, j: (i, j)),
        compiler_params=_cparams("parallel", "parallel"),
    )(gslots)


HBM_SPEC = pl.BlockSpec(memory_space=pl.ANY)


def _all_gather(arrs, name, relayed=(), after=None):
    n = len(arrs)
    n_sems = 13
    n_in = n + (after is not None)

    def body(*refs):
        ins, outs = refs[:n], refs[n_in:n_in + n]
        send_sems, recv_sems, local_sems = refs[n_in + n:]
        x, y, c = lax.axis_index("x"), lax.axis_index("y"), lax.axis_index("c")
        me, sibling = (x, y, c), (x, y, 1 - c)
        chips = [(1 - x, y), (x, 1 - y), (1 - x, 1 - y)]
        index = lambda px, py, pc: 4 * px + 2 * py + pc

        def copy(a, k, block, to, src=None, cols=None):
            dst = outs[a].at[index(*block)]
            src = dst if src is None else src
            if cols is not None:
                dst, src = dst.at[:, cols], src.at[:, cols]
            return pltpu.make_async_remote_copy(
                src_ref=src, dst_ref=dst, send_sem=send_sems.at[a, k], recv_sem=recv_sems.at[a, k],
                device_id=to, device_id_type=MESH)

        mine = [pltpu.make_async_copy(ins[a], outs[a].at[index(*me)], local_sems.at[a])
                for a in range(n)]
        for cp in mine:
            cp.start()
        sends = []

        def start(cp):
            cp.start()
            sends.append(cp)

        halves = {a: (pl.ds(0, arrs[a].shape[1] // 2), pl.ds(arrs[a].shape[1] // 2, arrs[a].shape[1] // 2))
                  for a in relayed}
        near_x, near_y, far = [(*chip, c) for chip in chips]
        for a in range(n):
            start(copy(a, 0, me, sibling, src=ins[a]))
            if a in relayed:
                left, right = halves[a]
                for k, to, cols in ((1, near_x, left), (3, near_y, right), (2, near_x, right), (4, near_y, left)):
                    start(copy(a, k, me, to, src=ins[a], cols=cols))
            else:
                for j, chip in enumerate(chips):
                    start(copy(a, 1 + j, me, (*chip, c), src=ins[a]))
        for a in relayed:
            left, right = halves[a]
            for k, block, cols, onward, to_sibling in (
                    (1, near_x, left, (5, near_y), 7), (3, near_y, right, (6, near_x), 10),
                    (2, near_x, right, None, 8), (4, near_y, left, None, 9),
                    (5, far, left, None, 11), (6, far, right, None, 12)):
                copy(a, k, block, me, cols=cols).wait_recv()
                if onward is not None:
                    start(copy(a, onward[0], block, onward[1], cols=cols))
                start(copy(a, to_sibling, block, sibling, cols=cols))
        for j, chip in enumerate(chips):
            for a in range(n):
                if a not in relayed:
                    copy(a, 1 + j, (*chip, c), me).wait_recv()
                    start(copy(a, 4 + j, (*chip, c), sibling))
        for a in range(n):
            copy(a, 0, sibling, me).wait_recv()
            if a in relayed:
                left, right = halves[a]
                for k, chip, cols in ((7, chips[0], left), (8, chips[0], right), (9, chips[1], left),
                                      (10, chips[1], right), (11, chips[2], left), (12, chips[2], right)):
                    copy(a, k, (*chip, 1 - c), me, cols=cols).wait_recv()
            else:
                for j, chip in enumerate(chips):
                    copy(a, 4 + j, (*chip, 1 - c), me).wait_recv()
        for cp in sends:
            cp.wait_send()
        for cp in mine:
            cp.wait()

    return pl.pallas_call(
        body, name=name,
        out_shape=[jax.ShapeDtypeStruct((N_DEV,) + a.shape, a.dtype) for a in arrs],
        in_specs=[HBM_SPEC] * n_in, out_specs=[HBM_SPEC] * n,
        scratch_shapes=[pltpu.SemaphoreType.DMA((n, n_sems)), pltpu.SemaphoreType.DMA((n, n_sems)),
                        pltpu.SemaphoreType.DMA((n,))],
    )(*arrs, *([after] if after is not None else []))


def _sibling_swap(arrs, name):
    n = len(arrs)

    def body(*refs):
        ins, outs = refs[:n], refs[n:2 * n]
        send_sems, recv_sems = refs[2 * n:]
        x, y, c = lax.axis_index("x"), lax.axis_index("y"), lax.axis_index("c")
        copies = [pltpu.make_async_remote_copy(
            src_ref=ins[a].at[:, 1 - c], dst_ref=outs[a],
            send_sem=send_sems.at[a], recv_sem=recv_sems.at[a],
            device_id=(x, y, 1 - c), device_id_type=MESH) for a in range(n)]
        for cp in copies:
            cp.start()
        for cp in copies:
            cp.wait()

    return pl.pallas_call(
        body, name=name,
        out_shape=[jax.ShapeDtypeStruct(a.shape[:1] + a.shape[2:], a.dtype) for a in arrs],
        in_specs=[HBM_SPEC] * n, out_specs=[HBM_SPEC] * n,
        scratch_shapes=[pltpu.SemaphoreType.DMA((n,)), pltpu.SemaphoreType.DMA((n,))],
    )(*arrs)


def _pair_sum(mine, theirs, core, name):
    chips, _, rows, cols = mine.shape
    tr, tc = _tile_2d(rows, cols, budget_bytes=2 << 20)

    def body(core_ref, a_ref, b_ref, o_ref):
        o_ref[...] = (a_ref[...].astype(F32) + b_ref[...].astype(F32)).astype(o_ref.dtype)

    slab = pl.BlockSpec((None, tr, tc), lambda ch, i, j, core_ref: (ch, i, j))
    return pl.pallas_call(
        body, name=name, out_shape=jax.ShapeDtypeStruct((chips, rows, cols), mine.dtype),
        grid_spec=pltpu.PrefetchScalarGridSpec(
            num_scalar_prefetch=1, grid=(chips, rows // tr, cols // tc),
            in_specs=[pl.BlockSpec((None, None, tr, tc),
                                   lambda ch, i, j, core_ref: (ch, core_ref[0], i, j)), slab],
            out_specs=slab),
        compiler_params=_cparams("parallel", "parallel", "parallel"),
    )(core, mine, theirs)


HBM_ONLY = pl.BlockSpec(memory_space=pltpu.HBM)
SEM_SPEC = pl.BlockSpec(memory_space=pltpu.SEMAPHORE)
SPLIT_COPY_EFFECT = pltpu.SideEffectType.DATAFLOW_SIDE_EFFECTING


def _flip(v, bit):
    return 1 - v if bit else v


def _chip_slices_plan(n):
    def plan():
        x, y, c = lax.axis_index("x"), lax.axis_index("y"), lax.axis_index("c")
        copies = []
        for k in range(1, 4):
            px, py = _flip(x, k & 2), _flip(y, k & 1)
            copies += [(a, 2 * px + py, 2 * x + y, (px, py, c)) for a in range(n)]
        return copies
    return plan, 3 * n


def _gather_plan(n):
    def plan():
        x, y, c = lax.axis_index("x"), lax.axis_index("y"), lax.axis_index("c")
        copies = []
        for k in range(1, N_DEV):
            peer = (_flip(x, k & 4), _flip(y, k & 2), _flip(c, k & 1))
            copies += [(a, None, 4 * x + 2 * y + c, peer) for a in range(n)]
        return copies + [(a, None, 4 * x + 2 * y + c, None) for a in range(n)]
    return plan, 8 * n


def _planned_copies(plan, srcs, lands, send_sems, recv_sems):
    copies = []
    for i, (a, src_at, land_at, peer) in enumerate(plan()):
        src, dst = srcs[a] if src_at is None else srcs[a].at[src_at], lands[a].at[land_at]
        if peer is None:
            local = pltpu.make_async_copy(src, dst, send_sems[i])
            copies.append((local, local.wait))
        else:
            remote = pltpu.make_async_remote_copy(src_ref=src, dst_ref=dst, send_sem=send_sems[i],
                                                  recv_sem=recv_sems[i], device_id=peer, device_id_type=MESH)
            copies.append((remote, remote.wait))
    return copies


def _split_exchange_start(plan_and_count, arrs, land_shapes, name, after=None):
    plan, n_sems = plan_and_count
    n = len(arrs)

    n_in = 2 * n + (after is not None)

    def body(*refs):
        srcs, lands = refs[:n], refs[n:2 * n]
        send_sems, recv_sems = refs[n_in:n_in + n_sems], refs[n_in + n_sems:n_in + 2 * n_sems]
        token = refs[-1]
        for copy, _ in _planned_copies(plan, srcs, lands, send_sems, recv_sems):
            copy.start()
        token[...] = jnp.zeros_like(token)

    hbm = lambda a: pltpu.HBM(a.shape, a.dtype)
    operands = [pltpu.with_memory_space_constraint(a, pltpu.HBM) for a in arrs]
    operands += [pltpu.with_memory_space_constraint(lax.empty(shape, a.dtype), pltpu.HBM)
                 for a, shape in zip(arrs, land_shapes)]
    out = pl.pallas_call(
        body, name=name,
        out_shape=(*[pltpu.SemaphoreType.DMA(())] * (2 * n_sems),
                   *[hbm(a) for a in operands],
                   jax.ShapeDtypeStruct((8, 128), F32)),
        in_specs=[HBM_ONLY] * (2 * n) + [pl.BlockSpec(memory_space=pl.ANY)] * (after is not None),
        out_specs=(*[SEM_SPEC] * (2 * n_sems), *[HBM_ONLY] * (2 * n),
                   pl.BlockSpec(memory_space=pltpu.VMEM)),
        input_output_aliases={i: 2 * n_sems + i for i in range(2 * n)},
        compiler_params=pltpu.CompilerParams(has_side_effects=SPLIT_COPY_EFFECT),
    )(*operands, *([after] if after is not None else []))
    sems, rest = list(out[:2 * n_sems]), out[2 * n_sems:]
    return sems, list(rest[:n]), list(rest[n:2 * n]), rest[-1]


def _split_exchange_wait(plan_and_count, sems, srcs, lands, after, name):
    plan, n_sems = plan_and_count
    n = len(srcs)

    def body(*refs):
        src_refs, land_refs = refs[:n], refs[n:2 * n]
        send_sems, recv_sems = refs[2 * n:2 * n + n_sems], refs[2 * n + n_sems:2 * n + 2 * n_sems]
        for _, wait in _planned_copies(plan, src_refs, land_refs, send_sems, recv_sems):
            wait()

    hbm = lambda a: pltpu.HBM(a.shape, a.dtype)
    out = pl.pallas_call(
        body, name=name,
        out_shape=(*[hbm(a) for a in srcs], *[hbm(a) for a in lands]),
        in_specs=[HBM_ONLY] * (2 * n) + [SEM_SPEC] * (2 * n_sems) + [pl.BlockSpec(memory_space=pl.ANY)],
        out_specs=tuple([HBM_ONLY] * (2 * n)),
        input_output_aliases={i: i for i in range(2 * n)},
        compiler_params=pltpu.CompilerParams(has_side_effects=SPLIT_COPY_EFFECT),
    )(*srcs, *lands, *sems, after)
    return list(out[n:])


def _adamw_exchanged(w, m, v, own, landed, chip, name):
    rows, cols = w.shape
    tr, tc = _tile_2d(rows, cols)

    def body(chip_ref, w_ref, m_ref, v_ref, own_ref, l1_ref, l2_ref, l3_ref, go_ref, d_ref, mo_ref, vo_ref):
        g = own_ref[...].astype(F32)
        for ref in (l1_ref, l2_ref, l3_ref):
            g = g + ref[...].astype(F32)
        _adamw_update(g, w_ref, m_ref, v_ref, go_ref, d_ref, mo_ref, vo_ref)

    blk = pl.BlockSpec((tr, tc), lambda i, j, chip_ref: (i, j))
    slot = lambda k: pl.BlockSpec((None, tr, tc), lambda i, j, chip_ref: (chip_ref[0] ^ k, i, j))
    o = jax.ShapeDtypeStruct((rows, cols), F32)
    return pl.pallas_call(
        body, name=name, out_shape=(o, o, o, o),
        grid_spec=pltpu.PrefetchScalarGridSpec(
            num_scalar_prefetch=1, grid=(rows // tr, cols // tc),
            in_specs=[blk, blk, blk, slot(0), slot(1), slot(2), slot(3)],
            out_specs=(blk, blk, blk, blk)),
        compiler_params=_cparams("parallel", "parallel"),
    )(chip, w, m, v, own, landed, landed, landed)


def _block_diag(t):
    nb, gpb, r, c = t.shape
    eye = jnp.eye(gpb, dtype=t.dtype)
    return jnp.einsum("ngrc,gh->ngrhc", t, eye).reshape(nb, gpb * r, gpb * c)


def _diag_blocks(t, r, c):
    nb = t.shape[0]
    gpb = t.shape[1] // r
    t = t.reshape(nb, gpb, r, gpb, c)
    return jnp.einsum("ngrhc,gh->ngrc", t, jnp.eye(gpb, dtype=t.dtype))


def _pack_rows(parts):
    flat = jnp.concatenate([p.reshape(-1).astype(F32) for p in parts])
    pad = (-flat.shape[0]) % (256 * 128)
    return jnp.pad(flat, (0, pad)).reshape(-1, 128)


def _unpack_rows(packed, shapes):
    flat = packed.reshape(-1)
    out, at = [], 0
    for shape in shapes:
        size = math.prod(shape)
        out.append(flat[at:at + size].reshape(shape))
        at += size
    return out


def kernel(x, ln_w, w_in, s5_lam_re, s5_lam_im, s5_log_step, s5_b_re, s5_b_im, s5_c_re, s5_c_im, s5_d, s5_w_glu, s5_w_up, dn_conv_w, dn_a_log, dn_dt_bias, dn_norm_w, dn_w_up, w_out, final_norm_w, loss_target, m_ln_w, m_w_in, m_s5_lam_re, m_s5_lam_im, m_s5_log_step, m_s5_b_re, m_s5_b_im, m_s5_c_re, m_s5_c_im, m_s5_d, m_s5_w_glu, m_s5_w_up, m_dn_conv_w, m_dn_a_log, m_dn_dt_bias, m_dn_norm_w, m_dn_w_up, m_w_out, m_final_norm_w, v_ln_w, v_w_in, v_s5_lam_re, v_s5_lam_im, v_s5_log_step, v_s5_b_re, v_s5_b_im, v_s5_c_re, v_s5_c_im, v_s5_d, v_s5_w_glu, v_s5_w_up, v_dn_conv_w, v_dn_a_log, v_dn_dt_bias, v_dn_norm_w, v_dn_w_up, v_w_out, v_final_norm_w):
    weights = dict(ln_w=ln_w, w_in=w_in, s5_lam_re=s5_lam_re, s5_lam_im=s5_lam_im,
                   s5_log_step=s5_log_step, s5_b_re=s5_b_re, s5_b_im=s5_b_im, s5_c_re=s5_c_re,
                   s5_c_im=s5_c_im, s5_d=s5_d, s5_w_glu=s5_w_glu, s5_w_up=s5_w_up,
                   dn_conv_w=dn_conv_w, dn_a_log=dn_a_log, dn_dt_bias=dn_dt_bias,
                   dn_norm_w=dn_norm_w, dn_w_up=dn_w_up, w_out=w_out, final_norm_w=final_norm_w)
    mom_m = dict(ln_w=m_ln_w, w_in=m_w_in, s5_lam_re=m_s5_lam_re, s5_lam_im=m_s5_lam_im,
                 s5_log_step=m_s5_log_step, s5_b_re=m_s5_b_re, s5_b_im=m_s5_b_im,
                 s5_c_re=m_s5_c_re, s5_c_im=m_s5_c_im, s5_d=m_s5_d, s5_w_glu=m_s5_w_glu,
                 s5_w_up=m_s5_w_up, dn_conv_w=m_dn_conv_w, dn_a_log=m_dn_a_log,
                 dn_dt_bias=m_dn_dt_bias, dn_norm_w=m_dn_norm_w, dn_w_up=m_dn_w_up,
                 w_out=m_w_out, final_norm_w=m_final_norm_w)
    mom_v = dict(ln_w=v_ln_w, w_in=v_w_in, s5_lam_re=v_s5_lam_re, s5_lam_im=v_s5_lam_im,
                 s5_log_step=v_s5_log_step, s5_b_re=v_s5_b_re, s5_b_im=v_s5_b_im,
                 s5_c_re=v_s5_c_re, s5_c_im=v_s5_c_im, s5_d=v_s5_d, s5_w_glu=v_s5_w_glu,
                 s5_w_up=v_s5_w_up, dn_conv_w=v_dn_conv_w, dn_a_log=v_dn_a_log,
                 dn_dt_bias=v_dn_dt_bias, dn_norm_w=v_dn_norm_w, dn_w_up=v_dn_w_up,
                 w_out=v_w_out, final_norm_w=v_final_norm_w)
    names = list(weights)

    l, d = x.shape[1], x.shape[2]
    d_s5 = d // 2
    groups = d_s5 // S5_GROUP
    nb = groups // S5_GPB
    d_dn = DN_HEADS * DN_HEAD_DIM
    w_in_cols = w_in.shape[2]
    d_in = N_DEV * w_in_cols
    off_ba_src = 2 * d_s5 + 4 * d_dn
    off_u, off_zs, off_qkv, off_zd = 0, d_s5, 2 * d_s5, 2 * d_s5 + 3 * d_dn
    off_ba = off_zd + d_dn
    n_main = off_ba + BA_PAD
    off_gs, off_gd = 0, d
    x2d, tgt2d = x[0], loss_target[0]
    my_index = 4 * lax.axis_index("x") + 2 * lax.axis_index("y") + lax.axis_index("c")

    g_win, g_conv = _all_gather([jnp.transpose(w_in[0]).astype(BF16), dn_conv_w[0]], name="gather_weights",
                                relayed=(0,))
    late_plan = _gather_plan(4)
    late_shards = [s5_w_glu[0].astype(BF16), s5_w_up[0].astype(BF16), dn_w_up[0].astype(BF16),
                   w_out[0].astype(BF16)]
    late_sems, late_shards, late_lands, late_token = _split_exchange_start(
        late_plan, late_shards, [(N_DEV,) + s.shape for s in late_shards], name="gather_late_start",
        after=g_conv)
    ba_end = off_ba_src + 2 * DN_HEADS
    w_full_t = g_win.reshape(d_in, d)
    w_gates_t = w_full_t[ba_end:]
    conv_full = jnp.transpose(g_conv, (1, 0, 2)).reshape(CONV_K, 3 * d_dn)

    lam_re, lam_im = s5_lam_re[0], s5_lam_im[0]
    log_step = s5_log_step[0].reshape(groups, 1)
    b_re = s5_b_re[0].reshape(groups * S5_STATE, S5_GROUP)
    b_im = s5_b_im[0].reshape(groups * S5_STATE, S5_GROUP)
    abar_re, abar_im, f_re, f_im = _s5_disc_fwd(lam_re, lam_im, log_step)
    f_re_col, f_im_col = f_re.reshape(-1, 1), f_im.reshape(-1, 1)
    bb_re, bb_im = _s5_bbar_fwd(f_re_col, f_im_col, b_re, b_im)

    def bb_blocks(t):
        t = t.reshape(nb, S5_GPB, S5_STATE, S5_GROUP).transpose(0, 1, 3, 2)
        return _block_diag(t).astype(BF16)

    def c_blocks(t):
        return _block_diag(t.reshape(nb, S5_GPB, S5_GROUP, S5_STATE)).astype(BF16)

    bbr, bbi = bb_blocks(bb_re), bb_blocks(bb_im)
    cbr, cbi = c_blocks(s5_c_re[0]), c_blocks(s5_c_im[0])
    ctr, cti = jnp.transpose(cbr, (0, 2, 1)), jnp.transpose(cbi, (0, 2, 1))
    a_re = abar_re.reshape(nb, 1, S5_GPB * S5_STATE)
    a_im = abar_im.reshape(nb, 1, S5_GPB * S5_STATE)

    h = _rms_fwd(x2d, ln_w)
    proj = _mm(h, w_full_t, tb=True, b_rows=n_main, tm=1024, tn=512, after=late_token, name="proj")
    proj_gates = _mm(h, w_gates_t, tb=True, tm=1024, tn=1024, name="proj_gates")
    y1, car_r, car_i, states_r, states_i = _s5_fwd(proj, bbr, bbi, a_re, a_im, ctr, cti, s5_d, d_s5)
    a_log_row = jnp.pad(dn_a_log, ((0, 0), (DN_HEADS, 128 - 2 * DN_HEADS)))
    dt_row = jnp.pad(dn_dt_bias, ((0, 0), (DN_HEADS, 128 - 2 * DN_HEADS)))
    qkv = _dn_prep_fwd(proj, off_qkv, conv_full)
    gb = _dn_gates_fwd(proj, off_ba, a_log_row, dt_row)
    o_dn, states, inverses = _dn_chunk_fwd(qkv, gb)

    g_glu, g_sup, g_dup, g_wout = _split_exchange_wait(late_plan, late_sems, late_shards, late_lands, o_dn,
                                                       name="gather_late_wait")
    wglu_full = g_glu.reshape(d_s5, d_s5)
    wsup_full = jnp.transpose(g_sup, (1, 0, 2)).reshape(d_s5, d)
    wdup_full = jnp.transpose(g_dup, (1, 0, 2)).reshape(d_dn, d)
    wout_full = g_wout.reshape(d, d)

    out_s = _s5_glu_fwd(y1, proj, off_zs, wglu_full)
    y_s = _mm(out_s, wsup_full, name="s5_up")
    out_d = _dn_out_fwd(o_dn, proj, off_zd, dn_norm_w)
    y_d = _mm(out_d, wdup_full, name="dn_up")

    mixed, branch = _merge_w_out(proj_gates, off_gs, off_gd, y_s, y_d, wout_full)
    dx2, dx2_bf, loss_dev, d_final_w = _final(x2d, branch, final_norm_w.reshape(1, d), tgt2d)

    g_wout_full = _mm(mixed, dx2_bf, ta=True, out_dtype=BF16, name="grad_w_out")
    dmixed = _mm(dx2_bf, wout_full, tb=True, name="d_mixed")
    dgs, dgd, dys, dyd = _merge_bwd(proj_gates, off_gs, off_gd, y_s, y_d, dmixed)

    g_dup_full = _mm(out_d, dyd, ta=True, out_dtype=BF16, name="grad_dn_up")
    dout_d = _mm(dyd, wdup_full, tb=True, name="d_out_d")
    do_dn, dzd, d_norm_w = _dn_out_bwd(o_dn, proj, off_zd, dn_norm_w, dout_d)
    dqkv, dgb_heads = _dn_chunk_bwd(qkv, gb, states, inverses, do_dn)
    dba, d_a_log_row, d_dt_row = _dn_gates_bwd(proj, off_ba, a_log_row, dt_row, dgb_heads)
    dqkv_pre, d_conv_full = _dn_prep_bwd(proj, off_qkv, conv_full, dqkv)

    g_sup_full = _mm(out_s, dys, ta=True, out_dtype=BF16, name="grad_s5_up")
    dout_s = _mm(dys, wsup_full, tb=True, name="d_out_s")
    dy1, dzs, g_glu_full = _s5_glu_bwd(y1, proj, off_zs, wglu_full, dout_s)

    def by_dest(t, axis=0):
        if axis == 1:
            return t.reshape(t.shape[0], 4, 2, t.shape[1] // N_DEV).transpose(1, 2, 0, 3)
        return t.reshape(4, 2, t.shape[0] // N_DEV, t.shape[1])

    core = lax.axis_index("c").astype(jnp.int32).reshape(1)
    chip = (2 * lax.axis_index("x") + lax.axis_index("y")).astype(jnp.int32).reshape(1)

    def chip_sums_of(which, parts, tag):
        from_sibling = _sibling_swap(parts, name="swap_grads_" + tag)
        return [_pair_sum(p, got, core, name="pair_sum_" + nm)
                for nm, p, got in zip(which, parts, from_sibling)]

    early = ["s5_w_glu", "s5_w_up", "dn_w_up", "w_out"]
    sums_a = chip_sums_of(early, [by_dest(g_glu_full.astype(BF16)), by_dest(g_sup_full, 1),
                                  by_dest(g_dup_full, 1), by_dest(g_wout_full)], "a")
    plan_a = _chip_slices_plan(len(sums_a))
    sems_a, src_a, land_a, token_a = _split_exchange_start(
        plan_a, sums_a, [t.shape for t in sums_a], name="exchange_start_a")

    (du, d_a_re, d_a_im, d_bbr, d_bbi, d_cbr, d_cbi, d_s5_d) = _s5_bwd(
        proj, dy1, bbr, bbi, a_re, a_im, cbr, cbi, s5_d + token_a[:1, :1], car_r, car_i,
        states_r, states_i)

    def from_bb_blocks(t):
        t = _diag_blocks(t, S5_GROUP, S5_STATE).transpose(0, 1, 3, 2)
        return t.reshape(groups * S5_STATE, S5_GROUP)

    d_f_re, d_f_im, d_b_re, d_b_im = _s5_bbar_bwd(f_re_col, f_im_col, b_re, b_im,
                                                 from_bb_blocks(d_bbr), from_bb_blocks(d_bbi))
    d_lam_re, d_lam_im, d_log_step = _s5_disc_bwd(
        lam_re, lam_im, log_step, d_a_re.reshape(groups, S5_STATE), d_a_im.reshape(groups, S5_STATE),
        d_f_re.reshape(groups, S5_STATE), d_f_im.reshape(groups, S5_STATE))
    d_c_re = _diag_blocks(d_cbr, S5_GROUP, S5_STATE).reshape(groups, S5_GROUP, S5_STATE)
    d_c_im = _diag_blocks(d_cbi, S5_GROUP, S5_STATE).reshape(groups, S5_GROUP, S5_STATE)

    dproj = jnp.concatenate([du, dzs, dqkv_pre, dzd, jnp.pad(dba, ((0, 0), (0, BA_PAD - 128)))], axis=1)
    dproj_gates = jnp.concatenate([dgs, dgd], axis=1)
    g_main_t = _mm(dproj, h, ta=True, out_dtype=BF16, tm=512, tn=d, name="grad_w_in")
    g_gates_t = _mm(dproj_gates, h, ta=True, out_dtype=BF16, tm=512, tn=d, name="grad_w_in_gates")
    g_win_full_t = jnp.concatenate([g_main_t[:ba_end], g_gates_t], axis=0)
    sums_b = chip_sums_of(["w_in"], [by_dest(g_win_full_t)], "b")
    plan_b = _chip_slices_plan(1)
    sems_b, src_b, land_b, token_b = _split_exchange_start(
        plan_b, sums_b, [t.shape for t in sums_b], name="exchange_start_b")
    dh_main = _mm(dproj, w_full_t, b_rows=n_main, tm=1024, tn=1024, tk=n_main // 4, after=token_b,
                  name="d_h_main")
    dh = _mm(dproj_gates, w_gates_t, tm=1024, tn=1024, tk=2048, addend=dh_main, name="d_h")
    grad_x, d_ln_w = _rms_bwd(x2d, ln_w, dh, dx2)
    big = ["w_in"] + early
    results = {}

    land_a = _split_exchange_wait(plan_a, sems_a, src_a, land_a, grad_x, name="exchange_wait_a")
    for nm, own, landed in zip(early, src_a, land_a):
        results[nm] = _adamw_exchanged(weights[nm][0], mom_m[nm][0], mom_v[nm][0], own, landed, chip,
                                       name="adamw_" + nm)

    small = [nm for nm in names if nm not in big]
    small_grads = dict(
        ln_w=d_ln_w, s5_lam_re=d_lam_re, s5_lam_im=d_lam_im, s5_log_step=d_log_step,
        s5_b_re=d_b_re, s5_b_im=d_b_im, s5_c_re=d_c_re, s5_c_im=d_c_im, s5_d=d_s5_d,
        dn_conv_w=d_conv_full, dn_a_log=d_a_log_row[:, DN_HEADS:2 * DN_HEADS],
        dn_dt_bias=d_dt_row[:, DN_HEADS:2 * DN_HEADS], dn_norm_w=d_norm_w, final_norm_w=d_final_w)
    (all_small,) = _all_gather([_pack_rows([small_grads[nm] for nm in small])], name="gather_small_grads",
                               after=results[early[-1]][0])
    summed = _slot_sum(all_small, name="sum_small_grads")
    full_shapes = [(CONV_K, 3 * d_dn) if nm == "dn_conv_w" else weights[nm].shape for nm in small]
    g_small = dict(zip(small, _unpack_rows(summed, full_shapes)))
    conv_cols = dn_conv_w.shape[2]
    g_small["dn_conv_w"] = lax.dynamic_slice_in_dim(
        g_small["dn_conv_w"], my_index * conv_cols, conv_cols, axis=1).reshape(dn_conv_w.shape)
    packed = [_pack_rows([t[nm] for nm in small]) for t in (weights, mom_m, mom_v, g_small)]
    small_out = _adamw(packed[0], packed[1], packed[2], packed[3][None], name="adamw_small")
    small_shapes = [weights[nm].shape for nm in small]
    for kind, packed_out in enumerate(small_out):
        for nm, val in zip(small, _unpack_rows(packed_out, small_shapes)):
            results.setdefault(nm, [None] * 4)[kind] = val

    (land_b,) = _split_exchange_wait(plan_b, sems_b, src_b, land_b, small_out[0], name="exchange_wait_b")
    res = _adamw_exchanged(jnp.transpose(w_in[0]), jnp.transpose(m_w_in[0]), jnp.transpose(v_w_in[0]),
                           src_b[0], land_b, chip, name="adamw_w_in")
    results["w_in"] = [jnp.transpose(t) for t in res]

    loss = lax.psum(loss_dev[0, 0], ("x", "y", "c"))
    outs = [loss, grad_x[None]]
    for kind in range(4):
        outs += [results[nm][kind].reshape(weights[nm].shape) for nm in names]
    return tuple(outs)
```
